```python
import jax, jax.numpy as jnp
from jax import lax
import numpy as np

D_MODEL = 1024
BATCH = 2
SEQ = 8192
DEPTH = 1
DEC_BATCH = 128
DEC_SEQ = 1
PAST_LEN = 16384
PAGE_SIZE = 128

D_MIX = D_MODEL
D_RWKV = D_MIX // 2
D_ATTN = D_MIX - D_RWKV
HEAD_DIM = 64
H_RWKV = D_RWKV // HEAD_DIM
N_Q_HEADS = D_ATTN // HEAD_DIM
N_KV_HEADS = max(1, N_Q_HEADS // 4)
GQA_GROUP = N_Q_HEADS // N_KV_HEADS
D_KV = N_KV_HEADS * HEAD_DIM
D_DECAY_LORA = 32
D_AAA_LORA = 32
D_GATE_LORA = 96
D_SHIFT = 3 * D_RWKV + D_DECAY_LORA + D_AAA_LORA + D_GATE_LORA
D_IN_PROJ = D_SHIFT + D_ATTN + 2 * D_KV
WINDOW = 128
ATTN_BLOCK = 128
ROPE_DIM = HEAD_DIM // 4
ROPE_THETA = 500000.0
ATTN_SCALE = HEAD_DIM ** -0.5
D_FF = 4 * D_MODEL
RMS_EPS = 1e-6
LNX_EPS = 64e-5
NEG_INF = -1e30

kernel_name = 'hymba_rwkv7_swa_sink_decode_step'


def rmsnorm(x, w):
    xf = x.astype(jnp.float32)
    y = xf * lax.rsqrt(jnp.mean(xf * xf, axis=-1, keepdims=True) + RMS_EPS)
    return (y * w.astype(jnp.float32)).astype(x.dtype)


def partial_rope(x, pos):
    half = ROPE_DIM // 2
    inv_freq = jnp.power(ROPE_THETA, -jnp.arange(half, dtype=jnp.float32) * (2.0 / ROPE_DIM))
    ang = pos[:, None] * inv_freq[None, :]
    cos = jnp.cos(ang)[None, :, None, :]
    sin = jnp.sin(ang)[None, :, None, :]
    xf = x.astype(jnp.float32)
    x1 = xf[..., :half]
    x2 = xf[..., half:ROPE_DIM]
    out = jnp.concatenate([x1 * cos - x2 * sin, x2 * cos + x1 * sin, xf[..., ROPE_DIM:]], axis=-1)
    return out.astype(x.dtype)


def wkv_step(S, inp):
    r, w, k, v, a, b = inp
    sa = jnp.einsum('bhij,bhj->bhi', S, a)
    S = S * w[:, :, None, :] + sa[..., None] * b[:, :, None, :] + v[..., None] * k[:, :, None, :]
    y = jnp.einsum('bhij,bhj->bhi', S, r)
    return S, y


def rwkv7_mix(p, shift_prev, wkv_prev, prm):
    B, T, _ = p.shape
    f32 = jnp.float32
    pf = p.astype(f32)
    prev = jnp.concatenate([shift_prev.astype(f32), pf[:, :-1]], axis=1)
    xs = pf + (prev - pf) * prm['mu_shift'].astype(f32)
    c0, c1, c2 = D_RWKV, 2 * D_RWKV, 3 * D_RWKV
    c3 = c2 + D_DECAY_LORA
    c4 = c3 + D_AAA_LORA
    r = xs[..., :c0]
    k = xs[..., c0:c1]
    v = xs[..., c1:c2]
    wl = xs[..., c2:c3]
    al = xs[..., c3:c4]
    gl = xs[..., c4:]
    w_log = -jax.nn.softplus(-(prm['w0'].astype(f32) + jnp.tanh(wl) @ prm['w_decay_up'].astype(f32))) - 0.5
    decay = jnp.exp(-jnp.exp(w_log))
    a = jax.nn.sigmoid(prm['a0'].astype(f32) + al @ prm['w_a_up'].astype(f32))
    g = jax.nn.sigmoid(gl) @ prm['w_g_up'].astype(f32)
    heads = lambda t: t.reshape(B, T, H_RWKV, HEAD_DIM)
    kk = heads(k * prm['k_k'].astype(f32))
    kk = kk / jnp.maximum(jnp.sqrt(jnp.sum(kk * kk, axis=-1, keepdims=True)), 1e-12)
    k = heads(k * (1.0 + (a - 1.0) * prm['k_a'].astype(f32)))
    r = heads(r)
    v = heads(v)
    decay = heads(decay)
    a = heads(a)
    seq = tuple(jnp.moveaxis(t, 1, 0) for t in (r, decay, k, v, -kk, kk * a))
    wkv_new, ys = lax.scan(wkv_step, wkv_prev.astype(f32), seq)
    y = jnp.moveaxis(ys, 0, 1)
    mean = jnp.mean(y, axis=-1, keepdims=True)
    var = jnp.mean(jnp.square(y - mean), axis=-1, keepdims=True)
    y = ((y - mean) * lax.rsqrt(var + LNX_EPS)).reshape(B, T, D_RWKV)
    y = y * prm['ln_x_w'].astype(f32) + prm['ln_x_b'].astype(f32)
    bonus = jnp.sum(r * k * prm['r_k'].astype(f32), axis=-1, keepdims=True) * v
    y = (y + bonus.reshape(B, T, D_RWKV)) * g
    return y.astype(p.dtype), wkv_new, p[:, -1:]


def sink_attention(qb, kb, vb, mask, sinks):
    s = jnp.einsum('bnqhgd,bnkhd->bnhgqk', qb, kb, preferred_element_type=jnp.float32) * ATTN_SCALE
    s = jnp.where(mask[None, :, None, None], s, NEG_INF)
    sink = sinks.astype(jnp.float32).reshape(N_KV_HEADS, GQA_GROUP)[None, None, :, :, None, None]
    sink = jnp.broadcast_to(sink, s.shape[:-1] + (1,))
    p = jax.nn.softmax(jnp.concatenate([s, sink], axis=-1), axis=-1)[..., :-1]
    return jnp.einsum('bnhgqk,bnkhd->bnqhgd', p.astype(vb.dtype), vb)


def band_attention_prompt(q, k, v, sinks):
    B, T = q.shape[:2]
    nb = T // ATTN_BLOCK
    qb = q.reshape(B, nb, ATTN_BLOCK, N_KV_HEADS, GQA_GROUP, HEAD_DIM)
    kb = k.reshape(B, nb, ATTN_BLOCK, N_KV_HEADS, HEAD_DIM)
    vb = v.reshape(B, nb, ATTN_BLOCK, N_KV_HEADS, HEAD_DIM)
    kb2 = jnp.concatenate([jnp.concatenate([jnp.zeros_like(kb[:, :1]), kb[:, :-1]], axis=1), kb], axis=2)
    vb2 = jnp.concatenate([jnp.concatenate([jnp.zeros_like(vb[:, :1]), vb[:, :-1]], axis=1), vb], axis=2)
    qpos = jnp.arange(T).reshape(nb, ATTN_BLOCK)
    kpos = (jnp.arange(nb) * ATTN_BLOCK - ATTN_BLOCK)[:, None] + jnp.arange(2 * ATTN_BLOCK)[None, :]
    dq = qpos[:, :, None] - kpos[:, None, :]
    mask = (dq >= 0) & (dq < WINDOW) & (kpos[:, None, :] >= 0)
    o = sink_attention(qb, kb2, vb2, mask, sinks)
    return o.reshape(B, T, D_ATTN)


def window_attention_cached(q, k, v, k_past, v_past, sinks):
    B, T = q.shape[:2]
    W = k_past.shape[1]
    k_all = jnp.concatenate([k_past.astype(k.dtype), k], axis=1)
    v_all = jnp.concatenate([v_past.astype(v.dtype), v], axis=1)
    qpos = jnp.arange(T)
    kpos = jnp.arange(W + T) - W
    dq = qpos[:, None] - kpos[None, :]
    mask = ((dq >= 0) & (dq < WINDOW))[None]
    qb = q.reshape(B, 1, T, N_KV_HEADS, GQA_GROUP, HEAD_DIM)
    o = sink_attention(qb, k_all[:, None], v_all[:, None], mask, sinks)
    return o.reshape(B, T, D_ATTN), k_all[:, -W:], v_all[:, -W:]


def hybrid_layer(x, pos0, shift_prev, wkv_prev, k_past, v_past, prm):
    B, T, _ = x.shape
    h = rmsnorm(x, prm['norm_mix_w'])
    proj = h @ prm['w_in']
    o0 = D_SHIFT
    o1 = o0 + D_ATTN
    o2 = o1 + D_KV
    p_rwkv = proj[..., :o0]
    q = proj[..., o0:o1].reshape(B, T, N_Q_HEADS, HEAD_DIM)
    k = proj[..., o1:o2].reshape(B, T, N_KV_HEADS, HEAD_DIM)
    v = proj[..., o2:].reshape(B, T, N_KV_HEADS, HEAD_DIM)
    y_r, wkv_new, shift_new = rwkv7_mix(p_rwkv, shift_prev, wkv_prev, prm)
    pos = jnp.arange(T, dtype=jnp.float32) + pos0
    q = partial_rope(rmsnorm(q, prm['q_norm_w']), pos)
    k = partial_rope(rmsnorm(k, prm['k_norm_w']), pos)
    if k_past is None:
        y_a = band_attention_prompt(q, k, v, prm['sinks'])
        k_win, v_win = k[:, -WINDOW:], v[:, -WINDOW:]
    else:
        y_a, k_win, v_win = window_attention_cached(q, k, v, k_past, v_past, prm['sinks'])
    mix = jnp.concatenate([y_r, y_a.astype(y_r.dtype)], axis=-1)
    x = x + mix @ prm['w_out']
    hf = rmsnorm(x, prm['norm_ffn_w'])
    x = x + jnp.square(jax.nn.relu(hf @ prm['w_ffn_up'])) @ prm['w_ffn_down']
    return x, wkv_new, shift_new, k_win, v_win


def setup_inputs(seed: int = 0) -> dict:
    key = jax.random.key(seed)
    ks = jax.random.split(key, 26)
    f32 = jnp.float32
    nrm = lambda kk, shape, s: s * jax.random.normal(kk, shape, f32)
    L = DEPTH
    win_buf = min(WINDOW, PAST_LEN)
    return {
        'x_prompt': nrm(ks[0], (BATCH, SEQ, D_MODEL), 1.0),
        'x_sample': nrm(ks[1], (DEC_BATCH, DEC_SEQ, D_MODEL), 1.0),
        'state_wkv': nrm(ks[2], (L, DEC_BATCH, H_RWKV, HEAD_DIM, HEAD_DIM), 0.1),
        'state_shift': nrm(ks[3], (L, DEC_BATCH, 1, D_SHIFT), 1.0),
        'cache_k_win': nrm(ks[4], (L, DEC_BATCH, win_buf, N_KV_HEADS, HEAD_DIM), 1.0),
        'cache_v_win': nrm(ks[5], (L, DEC_BATCH, win_buf, N_KV_HEADS, HEAD_DIM), 1.0),
        'norm_mix_w': 1.0 + nrm(ks[6], (L, D_MODEL), 0.02),
        'w_in': nrm(ks[7], (L, D_MODEL, D_IN_PROJ), D_MODEL ** -0.5),
        'mu_shift': jax.random.uniform(ks[8], (L, D_SHIFT), f32),
        'w0': -1.0 + nrm(ks[9], (L, D_RWKV), 0.5),
        'w_decay_up': nrm(ks[10], (L, D_DECAY_LORA, D_RWKV), 0.1),
        'a0': nrm(ks[11], (L, D_RWKV), 0.1),
        'w_a_up': nrm(ks[12], (L, D_AAA_LORA, D_RWKV), 0.5 * D_AAA_LORA ** -0.5),
        'w_g_up': nrm(ks[13], (L, D_GATE_LORA, D_RWKV), D_GATE_LORA ** -0.5),
        'k_k': 0.85 + nrm(ks[14], (L, D_RWKV), 0.05),
        'k_a': 1.0 + nrm(ks[15], (L, D_RWKV), 0.05),
        'r_k': nrm(ks[16], (L, H_RWKV, HEAD_DIM), 0.1),
        'ln_x_w': 1.0 + nrm(ks[17], (L, D_RWKV), 0.02),
        'ln_x_b': nrm(ks[18], (L, D_RWKV), 0.02),
        'q_norm_w': 1.0 + nrm(ks[19], (L, HEAD_DIM), 0.02),
        'k_norm_w': 1.0 + nrm(ks[20], (L, HEAD_DIM), 0.02),
        'sinks': nrm(ks[21], (L, N_Q_HEADS), 0.5),
        'w_out': nrm(ks[22], (L, D_MIX, D_MODEL), D_MIX ** -0.5),
        'norm_ffn_w': 1.0 + nrm(ks[23], (L, D_MODEL), 0.02),
        'w_ffn_up': nrm(ks[24], (L, D_MODEL, D_FF), D_MODEL ** -0.5),
        'w_ffn_down': nrm(ks[25], (L, D_FF, D_MODEL), D_FF ** -0.5),
    }


def reference(x_prompt, x_sample, state_wkv, state_shift, cache_k_win, cache_v_win,
              norm_mix_w, w_in, mu_shift, w0, w_decay_up, a0, w_a_up, w_g_up, k_k, k_a, r_k,
              ln_x_w, ln_x_b, q_norm_w, k_norm_w, sinks, w_out, norm_ffn_w, w_ffn_up, w_ffn_down):
    yp = x_prompt
    ys = x_sample
    B = x_prompt.shape[0]
    wkv_p_l, sh_p_l, kw_p_l, vw_p_l = [], [], [], []
    wkv_s_l, sh_s_l, kw_s_l, vw_s_l = [], [], [], []
    for l in range(DEPTH):
        prm = dict(norm_mix_w=norm_mix_w[l], w_in=w_in[l], mu_shift=mu_shift[l], w0=w0[l],
                   w_decay_up=w_decay_up[l], a0=a0[l], w_a_up=w_a_up[l], w_g_up=w_g_up[l],
                   k_k=k_k[l], k_a=k_a[l], r_k=r_k[l], ln_x_w=ln_x_w[l], ln_x_b=ln_x_b[l],
                   q_norm_w=q_norm_w[l], k_norm_w=k_norm_w[l], sinks=sinks[l], w_out=w_out[l],
                   norm_ffn_w=norm_ffn_w[l], w_ffn_up=w_ffn_up[l], w_ffn_down=w_ffn_down[l])
        shift0 = jnp.zeros((B, 1, D_SHIFT), x_prompt.dtype)
        wkv0 = jnp.zeros((B, H_RWKV, HEAD_DIM, HEAD_DIM), jnp.float32)
        yp, wkv_p, sh_p, kw_p, vw_p = hybrid_layer(yp, 0, shift0, wkv0, None, None, prm)
        ys, wkv_s, sh_s, kw_s, vw_s = hybrid_layer(ys, PAST_LEN, state_shift[l], state_wkv[l],
                                                   cache_k_win[l], cache_v_win[l], prm)
        wkv_p_l.append(wkv_p); sh_p_l.append(sh_p); kw_p_l.append(kw_p); vw_p_l.append(vw_p)
        wkv_s_l.append(wkv_s); sh_s_l.append(sh_s); kw_s_l.append(kw_s); vw_s_l.append(vw_s)
    wkv_prompt = jnp.stack(wkv_p_l, axis=0)
    shift_prompt = jnp.stack(sh_p_l, axis=0)
    k_win_prompt = jnp.stack(kw_p_l, axis=0)
    v_win_prompt = jnp.stack(vw_p_l, axis=0)
    wkv_sample = jnp.stack(wkv_s_l, axis=0)
    shift_sample = jnp.stack(sh_s_l, axis=0)
    k_win_sample = jnp.stack(kw_s_l, axis=0)
    v_win_sample = jnp.stack(vw_s_l, axis=0)
    return (yp, ys, wkv_prompt, shift_prompt, k_win_prompt, v_win_prompt,
            wkv_sample, shift_sample, k_win_sample, v_win_sample)
```

```python
import functools

import jax
import jax.numpy as jnp
import numpy as np
from jax import lax
from jax.experimental import pallas as pl
from jax.experimental.pallas import tpu as pltpu

F32 = jnp.float32
BF16 = jnp.bfloat16

D_MODEL = 1024
D_RWKV = 512
D_ATTN = 512
HEAD_DIM = 64
H_RWKV = 8
N_Q_HEADS = 8
N_KV_HEADS = 2
GQA_GROUP = 4
D_KV = 128
D_LORA = 160
D_LORA_PAD = 256
D_SHIFT = 3 * D_RWKV + D_LORA
D_SHIFT_PAD = 3 * D_RWKV + D_LORA_PAD
D_IN_PAD = D_SHIFT_PAD + D_ATTN + 2 * D_KV
WINDOW = 128
ROPE_DIM = 16
ROPE_HALF = 8
ROPE_THETA = 500000.0
ATTN_SCALE = HEAD_DIM ** -0.5
D_FF = 4096
RMS_EPS = 1e-6
LNX_EPS = 64e-5
NEG_INF = -1e30
PAST_LEN = 16384

CHUNK = 64
QUAD = 4 * HEAD_DIM
VMEM_LIMIT = 56 * 1024 * 1024


def _split3(x):
    hi = x.astype(BF16)
    r1 = x - hi.astype(F32)
    mid = r1.astype(BF16)
    lo = (r1 - mid.astype(F32)).astype(BF16)
    return hi, mid, lo


def _dot_exact_rhs(x, m_bf16, terms=3):
    parts = _split3(x)[:terms]
    acc = jnp.dot(parts[0], m_bf16, preferred_element_type=F32)
    for p in parts[1:]:
        acc = acc + jnp.dot(p, m_bf16, preferred_element_type=F32)
    return acc


def _dot_exact_lhs(m_bf16, x, terms=3):
    parts = _split3(x)[:terms]
    acc = jnp.dot(m_bf16, parts[0], preferred_element_type=F32)
    for p in parts[1:]:
        acc = acc + jnp.dot(m_bf16, p, preferred_element_type=F32)
    return acc


def _mm(a, b):
    return jnp.dot(a.astype(BF16), b.astype(BF16), preferred_element_type=F32)


def _mm_nt(a, b):
    return lax.dot_general(a.astype(BF16), b.astype(BF16), (((1,), (1,)), ((), ())),
                           preferred_element_type=F32)


def _mm_tn(a, b):
    return lax.dot_general(a.astype(BF16), b.astype(BF16), (((0,), (0,)), ((), ())),
                           preferred_element_type=F32)


def _dot_nt_f32(a, b):
    return lax.dot_general(a, b, (((1,), (1,)), ((), ())), preferred_element_type=F32)


def _dot_tn_f32(a, b):
    return lax.dot_general(a, b, (((0,), (0,)), ((), ())), preferred_element_type=F32)


def _sigmoid(x):
    return 1.0 / (1.0 + jnp.exp(-x))


def _softplus(x):
    return jnp.maximum(x, 0.0) + jnp.log1p(jnp.exp(-jnp.abs(x)))


def _inproj_kernel(x_ref, nw_ref, w_ref, p_ref, q_ref, kv_ref):
    x = x_ref[...]
    ms = jnp.mean(x * x, axis=-1, keepdims=True)
    h = (x * lax.rsqrt(ms + RMS_EPS)) * nw_ref[...]
    out = jnp.dot(h.astype(BF16), w_ref[...], preferred_element_type=F32)
    p_ref[...] = out[:, :D_SHIFT_PAD]
    q_ref[...] = out[:, D_SHIFT_PAD:D_SHIFT_PAD + D_ATTN]
    kv_ref[...] = out[:, D_SHIFT_PAD + D_ATTN:]


def _inproj(x2d, norm_w, w_in_pad, tm):
    m = x2d.shape[0]
    return pl.pallas_call(
        _inproj_kernel,
        grid=(m // tm,),
        in_specs=[
            pl.BlockSpec((tm, D_MODEL), lambda i: (i, 0)),
            pl.BlockSpec((1, D_MODEL), lambda i: (0, 0)),
            pl.BlockSpec((D_MODEL, D_IN_PAD), lambda i: (0, 0)),
        ],
        out_specs=[
            pl.BlockSpec((tm, D_SHIFT_PAD), lambda i: (i, 0)),
            pl.BlockSpec((tm, D_ATTN), lambda i: (i, 0)),
            pl.BlockSpec((tm, 2 * D_KV), lambda i: (i, 0)),
        ],
        out_shape=[
            jax.ShapeDtypeStruct((m, D_SHIFT_PAD), F32),
            jax.ShapeDtypeStruct((m, D_ATTN), F32),
            jax.ShapeDtypeStruct((m, 2 * D_KV), F32),
        ],
        compiler_params=pltpu.CompilerParams(
            dimension_semantics=("arbitrary",), vmem_limit_bytes=VMEM_LIMIT),
        name="inproj",
    )(x2d, norm_w, w_in_pad)


def _rwkv_features(xs, prm, wl_ref, ones_ref):
    r = xs[:, 0:D_RWKV]
    k = xs[:, D_RWKV:2 * D_RWKV]
    v = xs[:, 2 * D_RWKV:3 * D_RWKV]
    lora = xs[:, 3 * D_RWKV:]
    col = lax.broadcasted_iota(jnp.int32, lora.shape, 1)
    act = jnp.where(col < 32, jnp.tanh(lora), jnp.where(col < 64, lora, _sigmoid(lora)))
    up = jnp.dot(act.astype(BF16), wl_ref[...], preferred_element_type=F32)
    w0, a0, k_k, k_a, r_k = prm[0:1], prm[1:2], prm[2:3], prm[3:4], prm[4:5]
    w_log = -_softplus(-(w0 + up[:, 0:D_RWKV])) - 0.5
    logw = -jnp.exp(w_log)
    asig = _sigmoid(a0 + up[:, D_RWKV:2 * D_RWKV])
    g = up[:, 2 * D_RWKV:]
    kk = k * k_k
    ss = _dot_exact_rhs(kk * kk, ones_ref[...])
    kk = kk / jnp.maximum(jnp.sqrt(ss), 1e-12)
    k = k * (1.0 + (asig - 1.0) * k_a)
    bonus = _dot_exact_rhs(r * k * r_k, ones_ref[...]) * v
    return r, logw, k, v, -kk, kk * asig, g, bonus


def _rwkv_finish(y, g, bonus, prm, ones_ref):
    ln_w, ln_b = prm[5:6], prm[6:7]
    mean = _dot_exact_rhs(y, ones_ref[...]) * (1.0 / HEAD_DIM)
    d = y - mean
    var = _dot_exact_rhs(d * d, ones_ref[...]) * (1.0 / HEAD_DIM)
    yn = d * lax.rsqrt(var + LNX_EPS) * ln_w + ln_b
    return (yn + bonus) * g


def _block_diag(x, bmask):
    return jnp.concatenate([x] * 4, axis=0) * bmask


def _chunk_quad(r, logw, k, v, a, b, sbd, tri_ref, bmask):
    cum = _dot_exact_lhs(tri_ref[...], logw)
    e_in = jnp.exp(cum)
    e_ex = jnp.exp(cum - logw)
    e_inv = jnp.exp(-cum)
    e_last = e_in[CHUNK - 1:CHUNK, :]
    rt = (r * e_in).astype(BF16)
    at = (a * e_ex).astype(BF16)
    kt = k * e_inv
    bt = b * e_inv
    kh = (kt * e_last).astype(BF16)
    bh = (bt * e_last).astype(BF16)
    kt = kt.astype(BF16)
    bt = bt.astype(BF16)
    vb = v.astype(BF16)
    sb = sbd.astype(BF16)

    t_idx = lax.broadcasted_iota(jnp.int32, (CHUNK, QUAD), 0)
    s_idx = lax.broadcasted_iota(jnp.int32, (CHUNK, QUAD), 1) & (HEAD_DIM - 1)
    strict = s_idx < t_idx
    incl = s_idx <= t_idx

    lhs = jnp.concatenate([at, rt], axis=0)
    rhs = jnp.concatenate([_block_diag(bt, bmask), _block_diag(kt, bmask)], axis=0)
    gm = _mm_nt(lhs, rhs)
    a_ab = jnp.where(strict, gm[:CHUNK, :QUAD], 0.0)
    a_ak = jnp.where(strict, gm[:CHUNK, QUAD:], 0.0)
    a_rb = jnp.where(incl, gm[CHUNK:, :QUAD], 0.0)
    a_rk = jnp.where(incl, gm[CHUNK:, QUAD:], 0.0)

    pw = a_ab
    t_inv = jnp.where(s_idx == t_idx, 1.0, 0.0) + a_ab
    for it in range(6):
        rbd = _block_diag(pw.astype(BF16), bmask)
        if it == 0:
            pw = _mm(pw, rbd)
        elif it < 5:
            out = _mm(jnp.concatenate([pw, t_inv], axis=0), rbd)
            pw = out[:CHUNK]
            t_inv = t_inv + out[CHUNK:]
        else:
            t_inv = t_inv + _mm(t_inv, rbd)

    vbd = _block_diag(vb, bmask)
    x = _mm_nt(at, sb) + _mm(a_ak, vbd)
    u = _mm(t_inv, _block_diag(x.astype(BF16), bmask))
    ub = u.astype(BF16)
    y = _mm_nt(rt, sb) + _mm(jnp.concatenate([a_rb, a_rk], axis=1),
                             jnp.concatenate([_block_diag(ub, bmask), vbd], axis=0))
    upd = _mm_tn(jnp.concatenate([ub, vb], axis=0), jnp.concatenate([bh, kh], axis=0))
    s_new = sbd * e_last + upd * bmask.astype(F32)
    return y, s_new


def _rwkv_prompt_kernel(p_ref, mu_ref, prm_ref, wl_ref, ones_ref, tri_ref, bmask_ref,
                        y_ref, hout_ref, prev_ref, h_ref):
    c = pl.program_id(1)

    @pl.when(c == 0)
    def _():
        prev_ref[...] = jnp.zeros_like(prev_ref)
        h_ref[...] = jnp.zeros_like(h_ref)

    p = p_ref[0]
    row = lax.broadcasted_iota(jnp.int32, p.shape, 0)
    prev = jnp.where(row == 0, jnp.broadcast_to(prev_ref[0:1, :], p.shape), pltpu.roll(p, 1, 0))
    xs = p + (prev - p) * mu_ref[...]
    prev_ref[0:1, :] = p[CHUNK - 1:CHUNK, :]

    prm = prm_ref[...]
    r, logw, k, v, a, b, g, bonus = _rwkv_features(xs, prm, wl_ref, ones_ref)
    bmask = bmask_ref[...]
    ys = []
    for q in range(2):
        sl = slice(q * QUAD, (q + 1) * QUAD)
        y_q, h_new = _chunk_quad(r[:, sl], logw[:, sl], k[:, sl], v[:, sl], a[:, sl], b[:, sl],
                                 h_ref[q], tri_ref, bmask)
        h_ref[q] = h_new
        ys.append(y_q)
    y = jnp.concatenate(ys, axis=1)
    y_ref[0] = _rwkv_finish(y, g, bonus, prm, ones_ref)

    @pl.when(c == pl.num_programs(1) - 1)
    def _():
        hout_ref[0] = h_ref[...]


def _rwkv_prompt(p3d, mu_pad, prm, wl, ones_bd, tri, bmask):
    bsz, t, _ = p3d.shape
    nc = t // CHUNK
    const = lambda shape: pl.BlockSpec(shape, lambda b, c: (0,) * len(shape))
    return pl.pallas_call(
        _rwkv_prompt_kernel,
        grid=(bsz, nc),
        in_specs=[
            pl.BlockSpec((1, CHUNK, D_SHIFT_PAD), lambda b, c: (b, c, 0)),
            const((1, D_SHIFT_PAD)),
            const((16, D_RWKV)),
            const((D_LORA_PAD, 3 * D_RWKV)),
            const((D_RWKV, D_RWKV)),
            const((CHUNK, CHUNK)),
            const((QUAD, QUAD)),
        ],
        out_specs=[
            pl.BlockSpec((1, CHUNK, D_RWKV), lambda b, c: (b, c, 0)),
            pl.BlockSpec((1, 2, QUAD, QUAD), lambda b, c: (b, 0, 0, 0)),
        ],
        out_shape=[
            jax.ShapeDtypeStruct((bsz, t, D_RWKV), F32),
            jax.ShapeDtypeStruct((bsz, 2, QUAD, QUAD), F32),
        ],
        scratch_shapes=[
            pltpu.VMEM((8, D_SHIFT_PAD), F32),
            pltpu.VMEM((2, QUAD, QUAD), F32),
        ],
        compiler_params=pltpu.CompilerParams(
            dimension_semantics=("arbitrary", "arbitrary"), vmem_limit_bytes=VMEM_LIMIT),
        name="rwkv_prompt",
    )(p3d, mu_pad, prm, wl, ones_bd, tri, bmask)


def _qk_norm_rope(x, norm_w, cos_t, sin_lo, sin_hi, ones_ref):
    ms = _dot_exact_rhs(x * x, ones_ref[...]) * (1.0 / HEAD_DIM)
    xn = x * lax.rsqrt(ms + RMS_EPS) * norm_w
    width = x.shape[1]
    fwd = pltpu.roll(xn, width - ROPE_HALF, 1)
    bwd = pltpu.roll(xn, ROPE_HALF, 1)
    return xn * cos_t + fwd * sin_lo + bwd * sin_hi


def _tile_lanes(x, reps):
    return jnp.concatenate([x] * reps, axis=1) if reps > 1 else x


def _attn_prompt_kernel(q_ref, kv_ref, tab_ref, qw_ref, kw_ref, sink_ref, ones_ref,
                        o_ref, kwin_ref, vwin_ref, kprev_ref, vprev_ref):
    i = pl.program_id(1)

    @pl.when(i == 0)
    def _():
        kprev_ref[...] = jnp.zeros_like(kprev_ref)
        vprev_ref[...] = jnp.zeros_like(vprev_ref)

    tab = tab_ref[...]
    cos_t, sin_lo, sin_hi = tab[:, 0:128], tab[:, 128:256], tab[:, 256:384]
    q = _qk_norm_rope(q_ref[0], qw_ref[...], _tile_lanes(cos_t, 4), _tile_lanes(sin_lo, 4),
                      _tile_lanes(sin_hi, 4), ones_ref)
    kv = kv_ref[0]
    k_cur = _qk_norm_rope(kv[:, 0:D_KV], kw_ref[...], cos_t, sin_lo, sin_hi, ones_ref.at[0:D_KV, 0:D_KV])
    v_cur = kv[:, D_KV:]
    k_all = jnp.concatenate([kprev_ref[...], k_cur], axis=0).astype(BF16)
    v_all = jnp.concatenate([vprev_ref[...], v_cur], axis=0).astype(BF16)
    kprev_ref[...] = k_cur
    vprev_ref[...] = v_cur
    kwin_ref[0] = k_cur
    vwin_ref[0] = v_cur

    qi = lax.broadcasted_iota(jnp.int32, (WINDOW, 2 * WINDOW), 0) + WINDOW
    ki = lax.broadcasted_iota(jnp.int32, (WINDOW, 2 * WINDOW), 1)
    dq = qi - ki
    kpos = ki + (i - 1) * WINDOW
    mask = (dq >= 0) & (dq < WINDOW) & (kpos >= 0)
    sinks = sink_ref[...]
    outs = []
    for h in range(N_Q_HEADS):
        g = h // GQA_GROUP
        qh = q[:, h * HEAD_DIM:(h + 1) * HEAD_DIM]
        s = _mm_nt(qh, k_all[:, g * HEAD_DIM:(g + 1) * HEAD_DIM]) * ATTN_SCALE
        s = jnp.where(mask, s, NEG_INF)
        sink = sinks[:, h:h + 1]
        m = jnp.maximum(jnp.max(s, axis=-1, keepdims=True), sink)
        e = jnp.exp(s - m)
        denom = jnp.sum(e, axis=-1, keepdims=True) + jnp.exp(sink - m)
        pr = e / denom
        outs.append(_mm(pr, v_all[:, g * HEAD_DIM:(g + 1) * HEAD_DIM]))
    o_ref[0] = jnp.concatenate(outs, axis=1)


def _attn_prompt(q3d, kv3d, tab, qw, kw, sinks, ones_bd):
    bsz, t, _ = q3d.shape
    nb = t // WINDOW
    const = lambda shape: pl.BlockSpec(shape, lambda b, i: (0,) * len(shape))
    return pl.pallas_call(
        _attn_prompt_kernel,
        grid=(bsz, nb),
        in_specs=[
            pl.BlockSpec((1, WINDOW, D_ATTN), lambda b, i: (b, i, 0)),
            pl.BlockSpec((1, WINDOW, 2 * D_KV), lambda b, i: (b, i, 0)),
            pl.BlockSpec((WINDOW, 384), lambda b, i: (i, 0)),
            const((1, D_ATTN)),
            const((1, D_KV)),
            const((1, N_Q_HEADS)),
            const((D_RWKV, D_RWKV)),
        ],
        out_specs=[
            pl.BlockSpec((1, WINDOW, D_ATTN), lambda b, i: (b, i, 0)),
            pl.BlockSpec((1, WINDOW, D_KV), lambda b, i: (b, 0, 0)),
            pl.BlockSpec((1, WINDOW, D_KV), lambda b, i: (b, 0, 0)),
        ],
        out_shape=[
            jax.ShapeDtypeStruct((bsz, t, D_ATTN), F32),
            jax.ShapeDtypeStruct((bsz, WINDOW, D_KV), F32),
            jax.ShapeDtypeStruct((bsz, WINDOW, D_KV), F32),
        ],
        scratch_shapes=[
            pltpu.VMEM((WINDOW, D_KV), F32),
            pltpu.VMEM((WINDOW, D_KV), F32),
        ],
        compiler_params=pltpu.CompilerParams(
            dimension_semantics=("arbitrary", "arbitrary"), vmem_limit_bytes=VMEM_LIMIT),
        name="attn_prompt",
    )(q3d, kv3d, tab, qw, kw, sinks, ones_bd)


def _outffn_kernel(x_ref, yr_ref, ya_ref, wo_ref, nw_ref, wu_ref, wd_ref, o_ref):
    mix = jnp.concatenate([yr_ref[...], ya_ref[...]], axis=1).astype(BF16)
    x1 = x_ref[...] + jnp.dot(mix, wo_ref[...], preferred_element_type=F32)
    ms = jnp.mean(x1 * x1, axis=-1, keepdims=True)
    hf = ((x1 * lax.rsqrt(ms + RMS_EPS)) * nw_ref[...]).astype(BF16)
    up = jnp.dot(hf, wu_ref[...], preferred_element_type=F32)
    act = jnp.square(jnp.maximum(up, 0.0)).astype(BF16)
    o_ref[...] = x1 + jnp.dot(act, wd_ref[...], preferred_element_type=F32)


def _outffn(x2d, yr, ya, wo, nw, wu, wd, tm):
    m = x2d.shape[0]
    const = lambda shape: pl.BlockSpec(shape, lambda i: (0,) * len(shape))
    return pl.pallas_call(
        _outffn_kernel,
        grid=(m // tm,),
        in_specs=[
            pl.BlockSpec((tm, D_MODEL), lambda i: (i, 0)),
            pl.BlockSpec((tm, D_RWKV), lambda i: (i, 0)),
            pl.BlockSpec((tm, D_ATTN), lambda i: (i, 0)),
            const((D_MODEL, D_MODEL)),
            const((1, D_MODEL)),
            const((D_MODEL, D_FF)),
            const((D_FF, D_MODEL)),
        ],
        out_specs=pl.BlockSpec((tm, D_MODEL), lambda i: (i, 0)),
        out_shape=jax.ShapeDtypeStruct((m, D_MODEL), F32),
        compiler_params=pltpu.CompilerParams(
            dimension_semantics=("arbitrary",), vmem_limit_bytes=VMEM_LIMIT),
        name="outffn",
    )(x2d, yr, ya, wo, nw, wu, wd)


DEC_TILE = 8


def _decode_kernel(p_ref, sh_ref, q_ref, kv_ref, s_ref, ck_ref, cv_ref, mu_ref, prm_ref, wl_ref,
                   ones_ref, tab_ref, qw_ref, kw_ref, sink_ref,
                   yr_ref, ya_ref, sout_ref, kout_ref, vout_ref, ysc_ref, osc_ref):
    p = p_ref[...]
    xs = p + (sh_ref[...] - p) * mu_ref[...]
    prm = prm_ref[...]
    r, logw, k, v, a, b, g, bonus = _rwkv_features(xs, prm, wl_ref, ones_ref)
    w = jnp.exp(logw)
    zeros6 = jnp.zeros((6, HEAD_DIM), F32)
    for bi in range(DEC_TILE):
        for h in range(H_RWKV):
            sl = slice(h * HEAD_DIM, (h + 1) * HEAD_DIM)
            s = s_ref[bi, h]
            a_h = a[bi:bi + 1, sl]
            sa = _dot_nt_f32(a_h, s)
            lhs = jnp.concatenate([sa, v[bi:bi + 1, sl], zeros6], axis=0)
            rhs = jnp.concatenate([b[bi:bi + 1, sl], k[bi:bi + 1, sl], zeros6], axis=0)
            s_new = s * w[bi:bi + 1, sl] + _dot_tn_f32(lhs, rhs)
            sout_ref[bi, h] = s_new
            ysc_ref[bi:bi + 1, sl] = _dot_nt_f32(r[bi:bi + 1, sl], s_new)
    yr_ref[...] = _rwkv_finish(ysc_ref[...], g, bonus, prm, ones_ref)

    tab = tab_ref[...]
    cos_t, sin_lo, sin_hi = tab[:, 0:128], tab[:, 128:256], tab[:, 256:384]
    q = _qk_norm_rope(q_ref[...], qw_ref[...], _tile_lanes(cos_t, 4), _tile_lanes(sin_lo, 4),
                      _tile_lanes(sin_hi, 4), ones_ref)
    kv = kv_ref[...]
    k_new = _qk_norm_rope(kv[:, 0:D_KV], kw_ref[...], cos_t, sin_lo, sin_hi, ones_ref.at[0:D_KV, 0:D_KV])
    v_new = kv[:, D_KV:]
    sinks = sink_ref[...]
    key_idx = lax.broadcasted_iota(jnp.int32, (GQA_GROUP, WINDOW), 1)
    for bi in range(DEC_TILE):
        ck = ck_ref[bi]
        cv = cv_ref[bi]
        kout_ref[bi, 0:WINDOW - 1, :] = ck[1:WINDOW, :]
        kout_ref[bi, WINDOW - 1:WINDOW, :] = k_new[bi:bi + 1, :]
        vout_ref[bi, 0:WINDOW - 1, :] = cv[1:WINDOW, :]
        vout_ref[bi, WINDOW - 1:WINDOW, :] = v_new[bi:bi + 1, :]
        for gi in range(N_KV_HEADS):
            gsl = slice(gi * HEAD_DIM, (gi + 1) * HEAD_DIM)
            qg = jnp.concatenate(
                [q[bi:bi + 1, (gi * GQA_GROUP + j) * HEAD_DIM:(gi * GQA_GROUP + j + 1) * HEAD_DIM]
                 for j in range(GQA_GROUP)], axis=0)
            s_c = _dot_nt_f32(qg, ck[:, gsl]) * ATTN_SCALE
            s_c = jnp.where(key_idx >= 1, s_c, NEG_INF)
            s_n = jnp.sum(qg * k_new[bi:bi + 1, gsl], axis=-1, keepdims=True) * ATTN_SCALE
            sink = sinks[gi * GQA_GROUP:(gi + 1) * GQA_GROUP, :]
            m = jnp.maximum(jnp.maximum(jnp.max(s_c, axis=-1, keepdims=True), s_n), sink)
            e_c = jnp.exp(s_c - m)
            e_n = jnp.exp(s_n - m)
            denom = jnp.sum(e_c, axis=-1, keepdims=True) + e_n + jnp.exp(sink - m)
            o = (jnp.dot(e_c, cv[:, gsl], preferred_element_type=F32)
                 + e_n * v_new[bi:bi + 1, gsl]) / denom
            for j in range(GQA_GROUP):
                hh = gi * GQA_GROUP + j
                osc_ref[bi:bi + 1, hh * HEAD_DIM:(hh + 1) * HEAD_DIM] = o[j:j + 1, :]
    ya_ref[...] = osc_ref[...]


def _decode(p, shift, q, kv, s, ck, cv, mu_pad, prm, wl, ones_bd, tab, qw, kw, sinks_col):
    n = p.shape[0]
    bt = DEC_TILE
    const = lambda shape: pl.BlockSpec(shape, lambda i: (0,) * len(shape))
    return pl.pallas_call(
        _decode_kernel,
        grid=(n // bt,),
        in_specs=[
            pl.BlockSpec((bt, D_SHIFT_PAD), lambda i: (i, 0)),
            pl.BlockSpec((bt, D_SHIFT_PAD), lambda i: (i, 0)),
            pl.BlockSpec((bt, D_ATTN), lambda i: (i, 0)),
            pl.BlockSpec((bt, 2 * D_KV), lambda i: (i, 0)),
            pl.BlockSpec((bt, H_RWKV, HEAD_DIM, HEAD_DIM), lambda i: (i, 0, 0, 0)),
            pl.BlockSpec((bt, WINDOW, D_KV), lambda i: (i, 0, 0)),
            pl.BlockSpec((bt, WINDOW, D_KV), lambda i: (i, 0, 0)),
            const((1, D_SHIFT_PAD)),
            const((16, D_RWKV)),
            const((D_LORA_PAD, 3 * D_RWKV)),
            const((D_RWKV, D_RWKV)),
            const((8, 384)),
            const((1, D_ATTN)),
            const((1, D_KV)),
            const((N_Q_HEADS, 1)),
        ],
        out_specs=[
            pl.BlockSpec((bt, D_RWKV), lambda i: (i, 0)),
            pl.BlockSpec((bt, D_ATTN), lambda i: (i, 0)),
            pl.BlockSpec((bt, H_RWKV, HEAD_DIM, HEAD_DIM), lambda i: (i, 0, 0, 0)),
            pl.BlockSpec((bt, WINDOW, D_KV), lambda i: (i, 0, 0)),
            pl.BlockSpec((bt, WINDOW, D_KV), lambda i: (i, 0, 0)),
        ],
        out_shape=[
            jax.ShapeDtypeStruct((n, D_RWKV), F32),
            jax.ShapeDtypeStruct((n, D_ATTN), F32),
            jax.ShapeDtypeStruct((n, H_RWKV, HEAD_DIM, HEAD_DIM), F32),
            jax.ShapeDtypeStruct((n, WINDOW, D_KV), F32),
            jax.ShapeDtypeStruct((n, WINDOW, D_KV), F32),
        ],
        scratch_shapes=[
            pltpu.VMEM((bt, D_RWKV), F32),
            pltpu.VMEM((bt, D_ATTN), F32),
        ],
        compiler_params=pltpu.CompilerParams(
            dimension_semantics=("arbitrary",), vmem_limit_bytes=VMEM_LIMIT),
        name="decode_mixer",
    )(p, shift, q, kv, s, ck, cv, mu_pad, prm, wl, ones_bd, tab, qw, kw, sinks_col)


def _rope_table(pos):
    inv_freq = jnp.power(ROPE_THETA, -jnp.arange(ROPE_HALF, dtype=F32) * (2.0 / ROPE_DIM))
    ang = pos[:, None] * inv_freq[None, :]
    cos, sin = jnp.cos(ang), jnp.sin(ang)
    n = pos.shape[0]
    ones = jnp.ones((n, HEAD_DIM - ROPE_DIM), F32)
    zeros = jnp.zeros((n, HEAD_DIM - ROPE_DIM), F32)
    z8 = jnp.zeros((n, ROPE_HALF), F32)
    cos_t = jnp.concatenate([cos, cos, ones], axis=1)
    sin_lo = jnp.concatenate([-sin, z8, zeros], axis=1)
    sin_hi = jnp.concatenate([z8, sin, zeros], axis=1)
    two = lambda x: jnp.concatenate([x, x], axis=1)
    return jnp.concatenate([two(cos_t), two(sin_lo), two(sin_hi)], axis=1)


def _pad_cols(w, at, n):
    return jnp.concatenate([w[..., :at], jnp.zeros(w.shape[:-1] + (n,), w.dtype), w[..., at:]], axis=-1)


def kernel(x_prompt, x_sample, state_wkv, state_shift, cache_k_win, cache_v_win, norm_mix_w, w_in, mu_shift, w0, w_decay_up, a0, w_a_up, w_g_up, k_k, k_a, r_k, ln_x_w, ln_x_b, q_norm_w, k_norm_w, sinks, w_out, norm_ffn_w, w_ffn_up, w_ffn_down):
    bsz, t, _ = x_prompt.shape
    nd = x_sample.shape[0]
    l = 0
    pad = D_LORA_PAD - D_LORA

    w_in_pad = _pad_cols(w_in[l], D_SHIFT, pad).astype(BF16)
    mu_pad = _pad_cols(mu_shift[l][None, :], D_SHIFT, pad)
    wl = jnp.zeros((D_LORA_PAD, 3 * D_RWKV), F32)
    wl = wl.at[0:32, 0:D_RWKV].set(w_decay_up[l])
    wl = wl.at[32:64, D_RWKV:2 * D_RWKV].set(w_a_up[l])
    wl = wl.at[64:160, 2 * D_RWKV:].set(w_g_up[l])
    wl = wl.astype(BF16)
    prm = jnp.zeros((16, D_RWKV), F32)
    prm = prm.at[0].set(w0[l]).at[1].set(a0[l]).at[2].set(k_k[l]).at[3].set(k_a[l])
    prm = prm.at[4].set(r_k[l].reshape(-1)).at[5].set(ln_x_w[l]).at[6].set(ln_x_b[l])
    hid = np.arange(D_RWKV) // HEAD_DIM
    ones_bd = jnp.asarray(hid[:, None] == hid[None, :], BF16)
    bmask = jnp.asarray(hid[:QUAD, None] == hid[None, :QUAD], BF16)
    tri = jnp.asarray(np.tril(np.ones((CHUNK, CHUNK))), BF16)
    qw = jnp.tile(q_norm_w[l][None, :], (1, N_Q_HEADS))
    kw = jnp.tile(k_norm_w[l][None, :], (1, N_KV_HEADS))
    nmw = norm_mix_w[l][None, :]
    nfw = norm_ffn_w[l][None, :]
    wo = w_out[l].astype(BF16)
    wu = w_ffn_up[l].astype(BF16)
    wd = w_ffn_down[l].astype(BF16)
    tab_p = _rope_table(jnp.arange(t, dtype=F32) + 0)
    tab_s = jnp.tile(_rope_table(jnp.arange(1, dtype=F32) + PAST_LEN), (8, 1))

    xp = x_prompt.reshape(bsz * t, D_MODEL)
    p_p, q_p, kv_p = _inproj(xp, nmw, w_in_pad, 512)
    yr_p, hbd = _rwkv_prompt(p_p.reshape(bsz, t, D_SHIFT_PAD), mu_pad, prm, wl, ones_bd, tri, bmask)
    ya_p, kwin_p, vwin_p = _attn_prompt(q_p.reshape(bsz, t, D_ATTN), kv_p.reshape(bsz, t, 2 * D_KV), tab_p,
                                        qw, kw, sinks[l][None, :], ones_bd)
    y_p = _outffn(xp, yr_p.reshape(bsz * t, D_RWKV), ya_p.reshape(bsz * t, D_ATTN), wo, nfw, wu, wd, 256)
    y_prompt = y_p.reshape(bsz, t, D_MODEL)
    hb = hbd.reshape(bsz, 2, 4, HEAD_DIM, 4, HEAD_DIM)
    wkv_prompt = jnp.stack([hb[:, :, j, :, j, :] for j in range(4)], axis=2)
    wkv_prompt = wkv_prompt.reshape(bsz, H_RWKV, HEAD_DIM, HEAD_DIM)[None]
    shift_prompt = p_p.reshape(bsz, t, D_SHIFT_PAD)[:, t - 1:, :D_SHIFT][None]
    k_win_prompt = kwin_p.reshape(bsz, WINDOW, N_KV_HEADS, HEAD_DIM)[None]
    v_win_prompt = vwin_p.reshape(bsz, WINDOW, N_KV_HEADS, HEAD_DIM)[None]

    xs = x_sample.reshape(nd, D_MODEL)
    p_s, q_s, kv_s = _inproj(xs, nmw, w_in_pad, 128)
    shift_in = _pad_cols(state_shift[l].reshape(nd, D_SHIFT), D_SHIFT, pad)
    yr_s, ya_s, wkv_s, kc_s, vc_s = _decode(
        p_s, shift_in, q_s, kv_s, state_wkv[l], cache_k_win[l].reshape(nd, WINDOW, D_KV),
        cache_v_win[l].reshape(nd, WINDOW, D_KV), mu_pad, prm, wl, ones_bd, tab_s, qw, kw,
        sinks[l][:, None])
    y_s = _outffn(xs, yr_s, ya_s, wo, nfw, wu, wd, 128)
    y_sample = y_s.reshape(nd, 1, D_MODEL)
    wkv_sample = wkv_s[None]
    shift_sample = p_s[:, :D_SHIFT].reshape(nd, 1, D_SHIFT)[None]
    k_win_sample = kc_s.reshape(nd, WINDOW, N_KV_HEADS, HEAD_DIM)[None]
    v_win_sample = vc_s.reshape(nd, WINDOW, N_KV_HEADS, HEAD_DIM)[None]

    return (y_prompt, y_sample, wkv_prompt, shift_prompt, k_win_prompt, v_win_prompt,
            wkv_sample, shift_sample, k_win_sample, v_win_sample)
```

```python
import functools

import jax
import jax.numpy as jnp
import numpy as np
from jax import lax
from jax.experimental import pallas as pl
from jax.experimental.pallas import tpu as pltpu

F32 = jnp.float32
BF16 = jnp.bfloat16

D_MODEL = 1024
D_RWKV = 512
D_ATTN = 512
HEAD_DIM = 64
H_RWKV = 8
N_Q_HEADS = 8
N_KV_HEADS = 2
GQA_GROUP = 4
D_KV = 128
D_LORA = 160
D_LORA_PAD = 256
D_SHIFT = 3 * D_RWKV + D_LORA
D_SHIFT_PAD = 3 * D_RWKV + D_LORA_PAD
D_IN_PAD = D_SHIFT_PAD + D_ATTN + 2 * D_KV
WINDOW = 128
ROPE_DIM = 16
ROPE_HALF = 8
ROPE_THETA = 500000.0
ATTN_SCALE = HEAD_DIM ** -0.5
D_FF = 4096
RMS_EPS = 1e-6
LNX_EPS = 64e-5
NEG_INF = -1e30
PAST_LEN = 16384

CHUNK = 64
QUAD = 4 * HEAD_DIM
VMEM_LIMIT = 56 * 1024 * 1024


def _split2(x):
    hi = x.astype(BF16)
    lo = (x - hi.astype(F32)).astype(BF16)
    return hi, lo


def _head_sums(xs, ones_ref):
    n, w = xs[0].shape
    tile = min(w, QUAD)
    per = w // tile
    pieces = [x[:, j * tile:(j + 1) * tile] for x in xs for j in range(per)]
    stacked = jnp.concatenate(pieces, axis=0) if len(pieces) > 1 else pieces[0]
    ones = ones_ref[0:tile, 0:tile]
    hi, lo = _split2(stacked)
    out = jnp.dot(hi, ones, preferred_element_type=F32) + jnp.dot(lo, ones, preferred_element_type=F32)
    res = []
    for i in range(len(xs)):
        cols = [out[(i * per + j) * n:(i * per + j + 1) * n] for j in range(per)]
        res.append(jnp.concatenate(cols, axis=1) if per > 1 else cols[0])
    return res


def _cumsum_rows(tri_bf16, x):
    hi, lo = _split2(x)
    return (jnp.dot(tri_bf16, hi, preferred_element_type=F32)
            + jnp.dot(tri_bf16, lo, preferred_element_type=F32))


def _mm(a, b):
    return jnp.dot(a.astype(BF16), b.astype(BF16), preferred_element_type=F32)


def _mm_nt(a, b):
    return lax.dot_general(a.astype(BF16), b.astype(BF16), (((1,), (1,)), ((), ())),
                           preferred_element_type=F32)


def _mm_tn(a, b):
    return lax.dot_general(a.astype(BF16), b.astype(BF16), (((0,), (0,)), ((), ())),
                           preferred_element_type=F32)


def _dot_nt_f32(a, b):
    return lax.dot_general(a, b, (((1,), (1,)), ((), ())), preferred_element_type=F32)


def _dot_tn_f32(a, b):
    return lax.dot_general(a, b, (((0,), (0,)), ((), ())), preferred_element_type=F32)


def _sigmoid(x):
    return 1.0 / (1.0 + jnp.exp(-x))


def _softplus(x):
    return jnp.maximum(x, 0.0) + jnp.log1p(jnp.exp(-jnp.abs(x)))


def _inproj_kernel(x_ref, nw_ref, w_ref, p_ref, q_ref, kv_ref):
    x = x_ref[...]
    ms = jnp.mean(x * x, axis=-1, keepdims=True)
    h = (x * lax.rsqrt(ms + RMS_EPS)) * nw_ref[...]
    out = jnp.dot(h.astype(BF16), w_ref[...], preferred_element_type=F32)
    p_ref[...] = out[:, :D_SHIFT_PAD]
    q_ref[...] = out[:, D_SHIFT_PAD:D_SHIFT_PAD + D_ATTN]
    kv_ref[...] = out[:, D_SHIFT_PAD + D_ATTN:]


def _inproj(x2d, norm_w, w_in_pad, tm):
    m = x2d.shape[0]
    return pl.pallas_call(
        _inproj_kernel,
        grid=(m // tm,),
        in_specs=[
            pl.BlockSpec((tm, D_MODEL), lambda i: (i, 0)),
            pl.BlockSpec((1, D_MODEL), lambda i: (0, 0)),
            pl.BlockSpec((D_MODEL, D_IN_PAD), lambda i: (0, 0)),
        ],
        out_specs=[
            pl.BlockSpec((tm, D_SHIFT_PAD), lambda i: (i, 0)),
            pl.BlockSpec((tm, D_ATTN), lambda i: (i, 0)),
            pl.BlockSpec((tm, 2 * D_KV), lambda i: (i, 0)),
        ],
        out_shape=[
            jax.ShapeDtypeStruct((m, D_SHIFT_PAD), F32),
            jax.ShapeDtypeStruct((m, D_ATTN), F32),
            jax.ShapeDtypeStruct((m, 2 * D_KV), F32),
        ],
        compiler_params=pltpu.CompilerParams(
            dimension_semantics=("arbitrary",), vmem_limit_bytes=VMEM_LIMIT),
        name="inproj",
    )(x2d, norm_w, w_in_pad)


def _rwkv_features(xs, prm, wl_ref, ones_ref):
    r = xs[:, 0:D_RWKV]
    k = xs[:, D_RWKV:2 * D_RWKV]
    v = xs[:, 2 * D_RWKV:3 * D_RWKV]
    lora = xs[:, 3 * D_RWKV:]
    col = lax.broadcasted_iota(jnp.int32, lora.shape, 1)
    act = jnp.where(col < 32, jnp.tanh(lora), jnp.where(col < 64, lora, _sigmoid(lora)))
    up = jnp.dot(act.astype(BF16), wl_ref[...], preferred_element_type=F32)
    w0, a0, k_k, k_a, r_k = prm[0:1], prm[1:2], prm[2:3], prm[3:4], prm[4:5]
    w_log = -_softplus(-(w0 + up[:, 0:D_RWKV])) - 0.5
    logw = -jnp.exp(w_log)
    asig = _sigmoid(a0 + up[:, D_RWKV:2 * D_RWKV])
    g = up[:, 2 * D_RWKV:]
    kk = k * k_k
    k_mod = k * (1.0 + (asig - 1.0) * k_a)
    ss, rk = _head_sums([kk * kk, r * k_mod * r_k], ones_ref)
    kk = kk / jnp.maximum(jnp.sqrt(ss), 1e-12)
    k = k_mod
    bonus = rk * v
    return r, logw, k, v, -kk, kk * asig, g, bonus


def _rwkv_finish(y, g, bonus, prm, ones_ref):
    ln_w, ln_b = prm[5:6], prm[6:7]
    mean = _head_sums([y], ones_ref)[0] * (1.0 / HEAD_DIM)
    d = y - mean
    var = _head_sums([d * d], ones_ref)[0] * (1.0 / HEAD_DIM)
    yn = d * lax.rsqrt(var + LNX_EPS) * ln_w + ln_b
    return (yn + bonus) * g


def _block_diag(x, bmask):
    return jnp.concatenate([x] * 4, axis=0) * bmask


def _chunk_scan(r, logw, k, v, a, b, sbd, tri_ref, bmask):
    n = len(r)
    ch = range(n)
    tri = tri_ref[...]
    cum = [_cumsum_rows(tri, logw[i]) for i in ch]
    e_in = [jnp.exp(cum[i]) for i in ch]
    e_ex = [jnp.exp(cum[i] - logw[i]) for i in ch]
    e_inv = [jnp.exp(-cum[i]) for i in ch]
    e_last = [e_in[i][CHUNK - 1:CHUNK, :] for i in ch]
    rt = [(r[i] * e_in[i]).astype(BF16) for i in ch]
    at = [(a[i] * e_ex[i]).astype(BF16) for i in ch]
    kt = [k[i] * e_inv[i] for i in ch]
    bt = [b[i] * e_inv[i] for i in ch]
    kh = [(kt[i] * e_last[i]).astype(BF16) for i in ch]
    bh = [(bt[i] * e_last[i]).astype(BF16) for i in ch]
    kt = [kt[i].astype(BF16) for i in ch]
    bt = [bt[i].astype(BF16) for i in ch]
    vb = [v[i].astype(BF16) for i in ch]
    sb = [sbd[i].astype(BF16) for i in ch]

    t_idx = lax.broadcasted_iota(jnp.int32, (CHUNK, QUAD), 0)
    s_idx = lax.broadcasted_iota(jnp.int32, (CHUNK, QUAD), 1) & (HEAD_DIM - 1)
    strict = s_idx < t_idx
    incl = s_idx <= t_idx

    gm = [_mm_nt(jnp.concatenate([at[i], rt[i]], axis=0),
                 jnp.concatenate([_block_diag(bt[i], bmask), _block_diag(kt[i], bmask)], axis=0))
          for i in ch]
    a_ab = [jnp.where(strict, gm[i][:CHUNK, :QUAD], 0.0) for i in ch]
    a_ak = [jnp.where(strict, gm[i][:CHUNK, QUAD:], 0.0) for i in ch]
    a_rb = [jnp.where(incl, gm[i][CHUNK:, :QUAD], 0.0) for i in ch]
    a_rk = [jnp.where(incl, gm[i][CHUNK:, QUAD:], 0.0) for i in ch]

    eye = jnp.where(s_idx == t_idx, 1.0, 0.0)
    pw = a_ab
    t_inv = [eye + a_ab[i] for i in ch]
    for it in range(6):
        rbd = [_block_diag(pw[i].astype(BF16), bmask) for i in ch]
        if it == 0:
            pw = [_mm(pw[i], rbd[i]) for i in ch]
        elif it < 5:
            out = [_mm(jnp.concatenate([pw[i], t_inv[i]], axis=0), rbd[i]) for i in ch]
            pw = [out[i][:CHUNK] for i in ch]
            t_inv = [t_inv[i] + out[i][CHUNK:] for i in ch]
        else:
            t_inv = [t_inv[i] + _mm(t_inv[i], rbd[i]) for i in ch]

    vbd = [_block_diag(vb[i], bmask) for i in ch]
    x = [_mm_nt(at[i], sb[i]) + _mm(a_ak[i], vbd[i]) for i in ch]
    u = [_mm(t_inv[i], _block_diag(x[i].astype(BF16), bmask)) for i in ch]
    ub = [u[i].astype(BF16) for i in ch]
    y = [_mm_nt(rt[i], sb[i]) + _mm(jnp.concatenate([a_rb[i], a_rk[i]], axis=1),
                                    jnp.concatenate([_block_diag(ub[i], bmask), vbd[i]], axis=0))
         for i in ch]
    upd = [_mm_tn(jnp.concatenate([ub[i], vb[i]], axis=0), jnp.concatenate([bh[i], kh[i]], axis=0))
           for i in ch]
    bmask_f = bmask.astype(F32)
    s_new = [sbd[i] * e_last[i] + upd[i] * bmask_f for i in ch]
    return y, s_new


def _rwkv_prompt_kernel(p_ref, mu_ref, prm_ref, wl_ref, tri_ref, bmask_ref,
                        y_ref, hout_ref, prev_ref, h_ref):
    c = pl.program_id(0)
    nseq = p_ref.shape[0]

    @pl.when(c == 0)
    def _():
        prev_ref[...] = jnp.zeros_like(prev_ref)
        h_ref[...] = jnp.zeros_like(h_ref)

    row = lax.broadcasted_iota(jnp.int32, (CHUNK, D_SHIFT_PAD), 0)
    xs = []
    for s in range(nseq):
        p = p_ref[s]
        prev = jnp.where(row == 0, jnp.broadcast_to(prev_ref[s, 0:1, :], p.shape), pltpu.roll(p, 1, 0))
        xs.append(p + (prev - p) * mu_ref[...])
        prev_ref[s, 0:1, :] = p[CHUNK - 1:CHUNK, :]
    xs = jnp.concatenate(xs, axis=0)

    prm = prm_ref[...]
    r, logw, k, v, a, b, g, bonus = _rwkv_features(xs, prm, wl_ref, bmask_ref)
    bmask = bmask_ref[...]
    chains = [(s, q) for s in range(nseq) for q in range(2)]
    cut = lambda x: [x[s * CHUNK:(s + 1) * CHUNK, q * QUAD:(q + 1) * QUAD] for s, q in chains]
    ys, h_new = _chunk_scan(cut(r), cut(logw), cut(k), cut(v), cut(a), cut(b),
                            [h_ref[s, q] for s, q in chains], tri_ref, bmask)
    for i, (s, q) in enumerate(chains):
        h_ref[s, q] = h_new[i]
    rows = [jnp.concatenate(ys[2 * s:2 * s + 2], axis=1) for s in range(nseq)]
    y = _rwkv_finish(jnp.concatenate(rows, axis=0), g, bonus, prm, bmask_ref)
    for s in range(nseq):
        y_ref[s] = y[s * CHUNK:(s + 1) * CHUNK]

    @pl.when(c == pl.num_programs(0) - 1)
    def _():
        hout_ref[...] = h_ref[...]


def _rwkv_prompt(p3d, mu_pad, prm, wl, tri, bmask):
    bsz, t, _ = p3d.shape
    nc = t // CHUNK
    const = lambda shape: pl.BlockSpec(shape, lambda c: (0,) * len(shape))
    return pl.pallas_call(
        _rwkv_prompt_kernel,
        grid=(nc,),
        in_specs=[
            pl.BlockSpec((bsz, CHUNK, D_SHIFT_PAD), lambda c: (0, c, 0)),
            const((1, D_SHIFT_PAD)),
            const((16, D_RWKV)),
            const((D_LORA_PAD, 3 * D_RWKV)),
            const((CHUNK, CHUNK)),
            const((QUAD, QUAD)),
        ],
        out_specs=[
            pl.BlockSpec((bsz, CHUNK, D_RWKV), lambda c: (0, c, 0)),
            const((bsz, 2, QUAD, QUAD)),
        ],
        out_shape=[
            jax.ShapeDtypeStruct((bsz, t, D_RWKV), F32),
            jax.ShapeDtypeStruct((bsz, 2, QUAD, QUAD), F32),
        ],
        scratch_shapes=[
            pltpu.VMEM((bsz, 8, D_SHIFT_PAD), F32),
            pltpu.VMEM((bsz, 2, QUAD, QUAD), F32),
        ],
        compiler_params=pltpu.CompilerParams(
            dimension_semantics=("arbitrary",), vmem_limit_bytes=VMEM_LIMIT),
        name="rwkv_prompt",
    )(p3d, mu_pad, prm, wl, tri, bmask)


def _qk_norm_rope(x, norm_w, cos_t, sin_lo, sin_hi, ones_ref):
    ms = _head_sums([x * x], ones_ref)[0] * (1.0 / HEAD_DIM)
    xn = x * lax.rsqrt(ms + RMS_EPS) * norm_w
    width = x.shape[1]
    fwd = pltpu.roll(xn, width - ROPE_HALF, 1)
    bwd = pltpu.roll(xn, ROPE_HALF, 1)
    return xn * cos_t + fwd * sin_lo + bwd * sin_hi


def _tile_lanes(x, reps):
    return jnp.concatenate([x] * reps, axis=1) if reps > 1 else x


def _attn_prompt_kernel(q_ref, kv_ref, tab_ref, qw_ref, kw_ref, sink_ref, ones_ref, hmask_ref,
                        o_ref, kwin_ref, vwin_ref, kprev_ref, vprev_ref):
    i = pl.program_id(1)

    @pl.when(i == 0)
    def _():
        kprev_ref[...] = jnp.zeros_like(kprev_ref)
        vprev_ref[...] = jnp.zeros_like(vprev_ref)

    tab = tab_ref[...]
    cos_t, sin_lo, sin_hi = tab[:, 0:128], tab[:, 128:256], tab[:, 256:384]
    q = _qk_norm_rope(q_ref[0], qw_ref[...], _tile_lanes(cos_t, 4), _tile_lanes(sin_lo, 4),
                      _tile_lanes(sin_hi, 4), ones_ref)
    kv = kv_ref[0]
    k_cur = _qk_norm_rope(kv[:, 0:D_KV], kw_ref[...], cos_t, sin_lo, sin_hi, ones_ref)
    v_cur = kv[:, D_KV:]
    k_all = jnp.concatenate([kprev_ref[...], k_cur], axis=0)
    v_all = jnp.concatenate([vprev_ref[...], v_cur], axis=0)
    kprev_ref[...] = k_cur
    vprev_ref[...] = v_cur
    kwin_ref[0] = k_cur
    vwin_ref[0] = v_cur

    nk = 2 * WINDOW
    qi = lax.broadcasted_iota(jnp.int32, (WINDOW, nk), 0) + WINDOW
    ki = lax.broadcasted_iota(jnp.int32, (WINDOW, nk), 1)
    dq = qi - ki
    kpos = ki + (i - 1) * WINDOW
    mask = (dq >= 0) & (dq < WINDOW) & (kpos >= 0)
    sinks = sink_ref[...]
    low = lax.broadcasted_iota(jnp.int32, (nk, D_KV), 1) < HEAD_DIM
    k_rot = pltpu.roll(k_all, HEAD_DIM, 1)
    v_rot = pltpu.roll(v_all, HEAD_DIM, 1)
    hmask = hmask_ref[...]
    groups = range(N_KV_HEADS)
    heads = range(N_Q_HEADS)
    k2 = [jnp.where(low, k_all, k_rot), jnp.where(low, k_rot, k_all)]
    v2 = [jnp.where(low, v_all, v_rot), jnp.where(low, v_rot, v_all)]
    k4 = [jnp.concatenate([k2[g], k2[g]], axis=1).astype(BF16) for g in groups]
    v4 = [jnp.concatenate([v2[g], v2[g]], axis=1).astype(BF16) for g in groups]
    kstack = [jnp.concatenate([k4[g]] * GQA_GROUP, axis=0) * hmask for g in groups]
    vstack = [jnp.concatenate([v4[g]] * GQA_GROUP, axis=0) * hmask for g in groups]
    s_all = [_mm_nt(q[:, g * QUAD:(g + 1) * QUAD], kstack[g]) * ATTN_SCALE for g in groups]
    s = [jnp.where(mask, s_all[h // GQA_GROUP][:, (h % GQA_GROUP) * nk:(h % GQA_GROUP + 1) * nk], NEG_INF)
         for h in heads]
    m = [jnp.maximum(jnp.max(s[h], axis=-1, keepdims=True), sinks[:, h:h + 1]) for h in heads]
    e = [jnp.exp(s[h] - m[h]) for h in heads]
    denom = [jnp.sum(e[h], axis=-1, keepdims=True) + jnp.exp(sinks[:, h:h + 1] - m[h]) for h in heads]
    pr = [(e[h] / denom[h]).astype(BF16) for h in heads]
    outs = [_mm(jnp.concatenate(pr[g * GQA_GROUP:(g + 1) * GQA_GROUP], axis=1), vstack[g])
            for g in groups]
    o_ref[0] = jnp.concatenate(outs, axis=1)


def _attn_prompt(q3d, kv3d, tab, qw, kw, sinks, ones_bd, hmask):
    bsz, t, _ = q3d.shape
    nb = t // WINDOW
    const = lambda shape: pl.BlockSpec(shape, lambda b, i: (0,) * len(shape))
    return pl.pallas_call(
        _attn_prompt_kernel,
        grid=(bsz, nb),
        in_specs=[
            pl.BlockSpec((1, WINDOW, D_ATTN), lambda b, i: (b, i, 0)),
            pl.BlockSpec((1, WINDOW, 2 * D_KV), lambda b, i: (b, i, 0)),
            pl.BlockSpec((WINDOW, 384), lambda b, i: (i, 0)),
            const((1, D_ATTN)),
            const((1, D_KV)),
            const((1, N_Q_HEADS)),
            const((QUAD, QUAD)),
            const((GQA_GROUP * 2 * WINDOW, QUAD)),
        ],
        out_specs=[
            pl.BlockSpec((1, WINDOW, D_ATTN), lambda b, i: (b, i, 0)),
            pl.BlockSpec((1, WINDOW, D_KV), lambda b, i: (b, 0, 0)),
            pl.BlockSpec((1, WINDOW, D_KV), lambda b, i: (b, 0, 0)),
        ],
        out_shape=[
            jax.ShapeDtypeStruct((bsz, t, D_ATTN), F32),
            jax.ShapeDtypeStruct((bsz, WINDOW, D_KV), F32),
            jax.ShapeDtypeStruct((bsz, WINDOW, D_KV), F32),
        ],
        scratch_shapes=[
            pltpu.VMEM((WINDOW, D_KV), F32),
            pltpu.VMEM((WINDOW, D_KV), F32),
        ],
        compiler_params=pltpu.CompilerParams(
            dimension_semantics=("arbitrary", "arbitrary"), vmem_limit_bytes=VMEM_LIMIT),
        name="attn_prompt",
    )(q3d, kv3d, tab, qw, kw, sinks, ones_bd, hmask)


def _outffn_kernel(x_ref, yr_ref, ya_ref, wo_ref, nw_ref, wu_ref, wd_ref, o_ref):
    mix = jnp.concatenate([yr_ref[...], ya_ref[...]], axis=1).astype(BF16)
    x1 = x_ref[...] + jnp.dot(mix, wo_ref[...], preferred_element_type=F32)
    ms = jnp.mean(x1 * x1, axis=-1, keepdims=True)
    hf = ((x1 * lax.rsqrt(ms + RMS_EPS)) * nw_ref[...]).astype(BF16)
    up = jnp.dot(hf, wu_ref[...], preferred_element_type=F32)
    act = jnp.square(jnp.maximum(up, 0.0)).astype(BF16)
    o_ref[...] = x1 + jnp.dot(act, wd_ref[...], preferred_element_type=F32)


def _outffn(x2d, yr, ya, wo, nw, wu, wd, tm):
    m = x2d.shape[0]
    const = lambda shape: pl.BlockSpec(shape, lambda i: (0,) * len(shape))
    return pl.pallas_call(
        _outffn_kernel,
        grid=(m // tm,),
        in_specs=[
            pl.BlockSpec((tm, D_MODEL), lambda i: (i, 0)),
            pl.BlockSpec((tm, D_RWKV), lambda i: (i, 0)),
            pl.BlockSpec((tm, D_ATTN), lambda i: (i, 0)),
            const((D_MODEL, D_MODEL)),
            const((1, D_MODEL)),
            const((D_MODEL, D_FF)),
            const((D_FF, D_MODEL)),
        ],
        out_specs=pl.BlockSpec((tm, D_MODEL), lambda i: (i, 0)),
        out_shape=jax.ShapeDtypeStruct((m, D_MODEL), F32),
        compiler_params=pltpu.CompilerParams(
            dimension_semantics=("arbitrary",), vmem_limit_bytes=VMEM_LIMIT),
        name="outffn",
    )(x2d, yr, ya, wo, nw, wu, wd)


DEC_TILE = 8


def _decode_kernel(p_ref, sh_ref, q_ref, kv_ref, s_ref, ck_ref, cv_ref, mu_ref, prm_ref, wl_ref,
                   ones_ref, tab_ref, qw_ref, kw_ref, sink_ref,
                   yr_ref, ya_ref, sout_ref, kout_ref, vout_ref, ysc_ref, osc_ref):
    p = p_ref[...]
    xs = p + (sh_ref[...] - p) * mu_ref[...]
    prm = prm_ref[...]
    r, logw, k, v, a, b, g, bonus = _rwkv_features(xs, prm, wl_ref, ones_ref)
    w = jnp.exp(logw)
    zeros6 = jnp.zeros((6, HEAD_DIM), F32)
    for bi in range(DEC_TILE):
        for h in range(H_RWKV):
            sl = slice(h * HEAD_DIM, (h + 1) * HEAD_DIM)
            s = s_ref[bi, h]
            a_h = a[bi:bi + 1, sl]
            sa = _dot_nt_f32(a_h, s)
            lhs = jnp.concatenate([sa, v[bi:bi + 1, sl], zeros6], axis=0)
            rhs = jnp.concatenate([b[bi:bi + 1, sl], k[bi:bi + 1, sl], zeros6], axis=0)
            s_new = s * w[bi:bi + 1, sl] + _dot_tn_f32(lhs, rhs)
            sout_ref[bi, h] = s_new
            ysc_ref[bi:bi + 1, sl] = _dot_nt_f32(r[bi:bi + 1, sl], s_new)
    yr_ref[...] = _rwkv_finish(ysc_ref[...], g, bonus, prm, ones_ref)

    tab = tab_ref[...]
    cos_t, sin_lo, sin_hi = tab[:, 0:128], tab[:, 128:256], tab[:, 256:384]
    q = _qk_norm_rope(q_ref[...], qw_ref[...], _tile_lanes(cos_t, 4), _tile_lanes(sin_lo, 4),
                      _tile_lanes(sin_hi, 4), ones_ref)
    kv = kv_ref[...]
    k_new = _qk_norm_rope(kv[:, 0:D_KV], kw_ref[...], cos_t, sin_lo, sin_hi, ones_ref)
    v_new = kv[:, D_KV:]
    sinks = sink_ref[...]
    key_idx = lax.broadcasted_iota(jnp.int32, (GQA_GROUP, WINDOW), 1)
    for bi in range(DEC_TILE):
        ck = ck_ref[bi]
        cv = cv_ref[bi]
        kout_ref[bi, 0:WINDOW - 1, :] = ck[1:WINDOW, :]
        kout_ref[bi, WINDOW - 1:WINDOW, :] = k_new[bi:bi + 1, :]
        vout_ref[bi, 0:WINDOW - 1, :] = cv[1:WINDOW, :]
        vout_ref[bi, WINDOW - 1:WINDOW, :] = v_new[bi:bi + 1, :]
        for gi in range(N_KV_HEADS):
            gsl = slice(gi * HEAD_DIM, (gi + 1) * HEAD_DIM)
            qg = jnp.concatenate(
                [q[bi:bi + 1, (gi * GQA_GROUP + j) * HEAD_DIM:(gi * GQA_GROUP + j + 1) * HEAD_DIM]
                 for j in range(GQA_GROUP)], axis=0)
            s_c = _dot_nt_f32(qg, ck[:, gsl]) * ATTN_SCALE
            s_c = jnp.where(key_idx >= 1, s_c, NEG_INF)
            s_n = jnp.sum(qg * k_new[bi:bi + 1, gsl], axis=-1, keepdims=True) * ATTN_SCALE
            sink = sinks[gi * GQA_GROUP:(gi + 1) * GQA_GROUP, :]
            m = jnp.maximum(jnp.maximum(jnp.max(s_c, axis=-1, keepdims=True), s_n), sink)
            e_c = jnp.exp(s_c - m)
            e_n = jnp.exp(s_n - m)
            denom = jnp.sum(e_c, axis=-1, keepdims=True) + e_n + jnp.exp(sink - m)
            o = (jnp.dot(e_c, cv[:, gsl], preferred_element_type=F32)
                 + e_n * v_new[bi:bi + 1, gsl]) / denom
            for j in range(GQA_GROUP):
                hh = gi * GQA_GROUP + j
                osc_ref[bi:bi + 1, hh * HEAD_DIM:(hh + 1) * HEAD_DIM] = o[j:j + 1, :]
    ya_ref[...] = osc_ref[...]


def _decode(p, shift, q, kv, s, ck, cv, mu_pad, prm, wl, ones_bd, tab, qw, kw, sinks_col):
    n = p.shape[0]
    bt = DEC_TILE
    const = lambda shape: pl.BlockSpec(shape, lambda i: (0,) * len(shape))
    return pl.pallas_call(
        _decode_kernel,
        grid=(n // bt,),
        in_specs=[
            pl.BlockSpec((bt, D_SHIFT_PAD), lambda i: (i, 0)),
            pl.BlockSpec((bt, D_SHIFT_PAD), lambda i: (i, 0)),
            pl.BlockSpec((bt, D_ATTN), lambda i: (i, 0)),
            pl.BlockSpec((bt, 2 * D_KV), lambda i: (i, 0)),
            pl.BlockSpec((bt, H_RWKV, HEAD_DIM, HEAD_DIM), lambda i: (i, 0, 0, 0)),
            pl.BlockSpec((bt, WINDOW, D_KV), lambda i: (i, 0, 0)),
            pl.BlockSpec((bt, WINDOW, D_KV), lambda i: (i, 0, 0)),
            const((1, D_SHIFT_PAD)),
            const((16, D_RWKV)),
            const((D_LORA_PAD, 3 * D_RWKV)),
            const((QUAD, QUAD)),
            const((8, 384)),
            const((1, D_ATTN)),
            const((1, D_KV)),
            const((N_Q_HEADS, 1)),
        ],
        out_specs=[
            pl.BlockSpec((bt, D_RWKV), lambda i: (i, 0)),
            pl.BlockSpec((bt, D_ATTN), lambda i: (i, 0)),
            pl.BlockSpec((bt, H_RWKV, HEAD_DIM, HEAD_DIM), lambda i: (i, 0, 0, 0)),
            pl.BlockSpec((bt, WINDOW, D_KV), lambda i: (i, 0, 0)),
            pl.BlockSpec((bt, WINDOW, D_KV), lambda i: (i, 0, 0)),
        ],
        out_shape=[
            jax.ShapeDtypeStruct((n, D_RWKV), F32),
            jax.ShapeDtypeStruct((n, D_ATTN), F32),
            jax.ShapeDtypeStruct((n, H_RWKV, HEAD_DIM, HEAD_DIM), F32),
            jax.ShapeDtypeStruct((n, WINDOW, D_KV), F32),
            jax.ShapeDtypeStruct((n, WINDOW, D_KV), F32),
        ],
        scratch_shapes=[
            pltpu.VMEM((bt, D_RWKV), F32),
            pltpu.VMEM((bt, D_ATTN), F32),
        ],
        compiler_params=pltpu.CompilerParams(
            dimension_semantics=("arbitrary",), vmem_limit_bytes=VMEM_LIMIT),
        name="decode_mixer",
    )(p, shift, q, kv, s, ck, cv, mu_pad, prm, wl, ones_bd, tab, qw, kw, sinks_col)


def _rope_table(pos):
    inv_freq = jnp.power(ROPE_THETA, -jnp.arange(ROPE_HALF, dtype=F32) * (2.0 / ROPE_DIM))
    ang = pos[:, None] * inv_freq[None, :]
    cos, sin = jnp.cos(ang), jnp.sin(ang)
    n = pos.shape[0]
    ones = jnp.ones((n, HEAD_DIM - ROPE_DIM), F32)
    zeros = jnp.zeros((n, HEAD_DIM - ROPE_DIM), F32)
    z8 = jnp.zeros((n, ROPE_HALF), F32)
    cos_t = jnp.concatenate([cos, cos, ones], axis=1)
    sin_lo = jnp.concatenate([-sin, z8, zeros], axis=1)
    sin_hi = jnp.concatenate([z8, sin, zeros], axis=1)
    two = lambda x: jnp.concatenate([x, x], axis=1)
    return jnp.concatenate([two(cos_t), two(sin_lo), two(sin_hi)], axis=1)


def _pad_cols(w, at, n):
    return jnp.concatenate([w[..., :at], jnp.zeros(w.shape[:-1] + (n,), w.dtype), w[..., at:]], axis=-1)


def kernel(x_prompt, x_sample, state_wkv, state_shift, cache_k_win, cache_v_win, norm_mix_w, w_in, mu_shift, w0, w_decay_up, a0, w_a_up, w_g_up, k_k, k_a, r_k, ln_x_w, ln_x_b, q_norm_w, k_norm_w, sinks, w_out, norm_ffn_w, w_ffn_up, w_ffn_down):
    bsz, t, _ = x_prompt.shape
    nd = x_sample.shape[0]
    l = 0
    pad = D_LORA_PAD - D_LORA

    w_in_pad = _pad_cols(w_in[l], D_SHIFT, pad).astype(BF16)
    mu_pad = _pad_cols(mu_shift[l][None, :], D_SHIFT, pad)
    wl = jnp.zeros((D_LORA_PAD, 3 * D_RWKV), F32)
    wl = wl.at[0:32, 0:D_RWKV].set(w_decay_up[l])
    wl = wl.at[32:64, D_RWKV:2 * D_RWKV].set(w_a_up[l])
    wl = wl.at[64:160, 2 * D_RWKV:].set(w_g_up[l])
    wl = wl.astype(BF16)
    prm = jnp.zeros((16, D_RWKV), F32)
    prm = prm.at[0].set(w0[l]).at[1].set(a0[l]).at[2].set(k_k[l]).at[3].set(k_a[l])
    prm = prm.at[4].set(r_k[l].reshape(-1)).at[5].set(ln_x_w[l]).at[6].set(ln_x_b[l])
    hid = np.arange(QUAD) // HEAD_DIM
    bmask = jnp.asarray(hid[:, None] == hid[None, :], BF16)
    hmask = jnp.asarray((np.arange(GQA_GROUP * 2 * WINDOW) // (2 * WINDOW))[:, None] == hid[None, :], BF16)
    tri = jnp.asarray(np.tril(np.ones((CHUNK, CHUNK))), BF16)
    qw = jnp.tile(q_norm_w[l][None, :], (1, N_Q_HEADS))
    kw = jnp.tile(k_norm_w[l][None, :], (1, N_KV_HEADS))
    nmw = norm_mix_w[l][None, :]
    nfw = norm_ffn_w[l][None, :]
    wo = w_out[l].astype(BF16)
    wu = w_ffn_up[l].astype(BF16)
    wd = w_ffn_down[l].astype(BF16)
    tab_p = _rope_table(jnp.arange(t, dtype=F32) + 0)
    tab_s = jnp.tile(_rope_table(jnp.arange(1, dtype=F32) + PAST_LEN), (8, 1))

    xp = x_prompt.reshape(bsz * t, D_MODEL)
    p_p, q_p, kv_p = _inproj(xp, nmw, w_in_pad, 512)
    yr_p, hbd = _rwkv_prompt(p_p.reshape(bsz, t, D_SHIFT_PAD), mu_pad, prm, wl, tri, bmask)
    ya_p, kwin_p, vwin_p = _attn_prompt(q_p.reshape(bsz, t, D_ATTN), kv_p.reshape(bsz, t, 2 * D_KV), tab_p,
                                        qw, kw, sinks[l][None, :], bmask, hmask)
    y_p = _outffn(xp, yr_p.reshape(bsz * t, D_RWKV), ya_p.reshape(bsz * t, D_ATTN), wo, nfw, wu, wd, 256)
    y_prompt = y_p.reshape(bsz, t, D_MODEL)
    hb = hbd.reshape(bsz, 2, 4, HEAD_DIM, 4, HEAD_DIM)
    wkv_prompt = jnp.stack([hb[:, :, j, :, j, :] for j in range(4)], axis=2)
    wkv_prompt = wkv_prompt.reshape(bsz, H_RWKV, HEAD_DIM, HEAD_DIM)[None]
    shift_prompt = p_p.reshape(bsz, t, D_SHIFT_PAD)[:, t - 1:, :D_SHIFT][None]
    k_win_prompt = kwin_p.reshape(bsz, WINDOW, N_KV_HEADS, HEAD_DIM)[None]
    v_win_prompt = vwin_p.reshape(bsz, WINDOW, N_KV_HEADS, HEAD_DIM)[None]

    xs = x_sample.reshape(nd, D_MODEL)
    p_s, q_s, kv_s = _inproj(xs, nmw, w_in_pad, 128)
    shift_in = _pad_cols(state_shift[l].reshape(nd, D_SHIFT), D_SHIFT, pad)
    yr_s, ya_s, wkv_s, kc_s, vc_s = _decode(
        p_s, shift_in, q_s, kv_s, state_wkv[l], cache_k_win[l].reshape(nd, WINDOW, D_KV),
        cache_v_win[l].reshape(nd, WINDOW, D_KV), mu_pad, prm, wl, bmask, tab_s, qw, kw,
        sinks[l][:, None])
    y_s = _outffn(xs, yr_s, ya_s, wo, nfw, wu, wd, 128)
    y_sample = y_s.reshape(nd, 1, D_MODEL)
    wkv_sample = wkv_s[None]
    shift_sample = p_s[:, :D_SHIFT].reshape(nd, 1, D_SHIFT)[None]
    k_win_sample = kc_s.reshape(nd, WINDOW, N_KV_HEADS, HEAD_DIM)[None]
    v_win_sample = vc_s.reshape(nd, WINDOW, N_KV_HEADS, HEAD_DIM)[None]

    return (y_prompt, y_sample, wkv_prompt, shift_prompt, k_win_prompt, v_win_prompt,
            wkv_sample, shift_sample, k_win_sample, v_win_sample)
```

```python
import functools

import jax
import jax.numpy as jnp
import numpy as np
from jax import lax
from jax.experimental import pallas as pl
from jax.experimental.pallas import tpu as pltpu

F32 = jnp.float32
BF16 = jnp.bfloat16

D_MODEL = 1024
D_RWKV = 512
D_ATTN = 512
HEAD_DIM = 64
H_RWKV = 8
N_Q_HEADS = 8
N_KV_HEADS = 2
GQA_GROUP = 4
D_KV = 128
D_LORA = 160
D_LORA_PAD = 256
D_SHIFT = 3 * D_RWKV + D_LORA
D_SHIFT_PAD = 3 * D_RWKV + D_LORA_PAD
D_IN_PAD = D_SHIFT_PAD + D_ATTN + 2 * D_KV
WINDOW = 128
ROPE_DIM = 16
ROPE_HALF = 8
ROPE_THETA = 500000.0
ATTN_SCALE = HEAD_DIM ** -0.5
D_FF = 4096
RMS_EPS = 1e-6
LNX_EPS = 64e-5
NEG_INF = -1e30
PAST_LEN = 16384

CHUNK = 64
QUAD = 4 * HEAD_DIM
VMEM_LIMIT = 56 * 1024 * 1024


def _split2(x):
    hi = x.astype(BF16)
    lo = (x - hi.astype(F32)).astype(BF16)
    return hi, lo


def _head_sums(xs, ones_ref):
    n, w = xs[0].shape
    tile = min(w, QUAD)
    per = w // tile
    pieces = [x[:, j * tile:(j + 1) * tile] for x in xs for j in range(per)]
    stacked = jnp.concatenate(pieces, axis=0) if len(pieces) > 1 else pieces[0]
    ones = ones_ref[0:tile, 0:tile]
    hi, lo = _split2(stacked)
    out = jnp.dot(hi, ones, preferred_element_type=F32) + jnp.dot(lo, ones, preferred_element_type=F32)
    res = []
    for i in range(len(xs)):
        cols = [out[(i * per + j) * n:(i * per + j + 1) * n] for j in range(per)]
        res.append(jnp.concatenate(cols, axis=1) if per > 1 else cols[0])
    return res


def _cumsum_rows(tri_bf16, x):
    hi, lo = _split2(x)
    return (jnp.dot(tri_bf16, hi, preferred_element_type=F32)
            + jnp.dot(tri_bf16, lo, preferred_element_type=F32))


def _mm(a, b):
    return jnp.dot(a.astype(BF16), b.astype(BF16), preferred_element_type=F32)


def _mm_nt(a, b):
    return lax.dot_general(a.astype(BF16), b.astype(BF16), (((1,), (1,)), ((), ())),
                           preferred_element_type=F32)


def _mm_tn(a, b):
    return lax.dot_general(a.astype(BF16), b.astype(BF16), (((0,), (0,)), ((), ())),
                           preferred_element_type=F32)


def _dot_nt_f32(a, b):
    return lax.dot_general(a, b, (((1,), (1,)), ((), ())), preferred_element_type=F32)


def _dot_tn_f32(a, b):
    return lax.dot_general(a, b, (((0,), (0,)), ((), ())), preferred_element_type=F32)


def _sigmoid(x):
    return 1.0 / (1.0 + jnp.exp(-x))


def _softplus(x):
    return jnp.maximum(x, 0.0) + jnp.log1p(jnp.exp(-jnp.abs(x)))


def _inproj_kernel(x_ref, nw_ref, w_ref, p_ref, q_ref, kv_ref):
    x = x_ref[...]
    ms = jnp.mean(x * x, axis=-1, keepdims=True)
    h = (x * lax.rsqrt(ms + RMS_EPS)) * nw_ref[...]
    out = jnp.dot(h.astype(BF16), w_ref[...], preferred_element_type=F32)
    p_ref[...] = out[:, :D_SHIFT_PAD]
    q_ref[...] = out[:, D_SHIFT_PAD:D_SHIFT_PAD + D_ATTN]
    kv_ref[...] = out[:, D_SHIFT_PAD + D_ATTN:]


def _inproj(x2d, norm_w, w_in_pad, tm):
    m = x2d.shape[0]
    return pl.pallas_call(
        _inproj_kernel,
        grid=(m // tm,),
        in_specs=[
            pl.BlockSpec((tm, D_MODEL), lambda i: (i, 0)),
            pl.BlockSpec((1, D_MODEL), lambda i: (0, 0)),
            pl.BlockSpec((D_MODEL, D_IN_PAD), lambda i: (0, 0)),
        ],
        out_specs=[
            pl.BlockSpec((tm, D_SHIFT_PAD), lambda i: (i, 0)),
            pl.BlockSpec((tm, D_ATTN), lambda i: (i, 0)),
            pl.BlockSpec((tm, 2 * D_KV), lambda i: (i, 0)),
        ],
        out_shape=[
            jax.ShapeDtypeStruct((m, D_SHIFT_PAD), F32),
            jax.ShapeDtypeStruct((m, D_ATTN), F32),
            jax.ShapeDtypeStruct((m, 2 * D_KV), F32),
        ],
        compiler_params=pltpu.CompilerParams(
            dimension_semantics=("arbitrary",), vmem_limit_bytes=VMEM_LIMIT),
        name="inproj",
    )(x2d, norm_w, w_in_pad)


def _rwkv_features(xs, prm, wl_ref, ones_ref):
    r = xs[:, 0:D_RWKV]
    k = xs[:, D_RWKV:2 * D_RWKV]
    v = xs[:, 2 * D_RWKV:3 * D_RWKV]
    lora = xs[:, 3 * D_RWKV:]
    col = lax.broadcasted_iota(jnp.int32, lora.shape, 1)
    act = jnp.where(col < 32, jnp.tanh(lora), jnp.where(col < 64, lora, _sigmoid(lora)))
    up = jnp.dot(act.astype(BF16), wl_ref[...], preferred_element_type=F32)
    w0, a0, k_k, k_a, r_k = prm[0:1], prm[1:2], prm[2:3], prm[3:4], prm[4:5]
    w_log = -_softplus(-(w0 + up[:, 0:D_RWKV])) - 0.5
    logw = -jnp.exp(w_log)
    asig = _sigmoid(a0 + up[:, D_RWKV:2 * D_RWKV])
    g = up[:, 2 * D_RWKV:]
    kk = k * k_k
    k_mod = k * (1.0 + (asig - 1.0) * k_a)
    ss, rk = _head_sums([kk * kk, r * k_mod * r_k], ones_ref)
    kk = kk / jnp.maximum(jnp.sqrt(ss), 1e-12)
    k = k_mod
    bonus = rk * v
    return r, logw, k, v, -kk, kk * asig, g, bonus


def _rwkv_finish(y, g, bonus, prm, ones_ref):
    ln_w, ln_b = prm[5:6], prm[6:7]
    mean = _head_sums([y], ones_ref)[0] * (1.0 / HEAD_DIM)
    d = y - mean
    var = _head_sums([d * d], ones_ref)[0] * (1.0 / HEAD_DIM)
    yn = d * lax.rsqrt(var + LNX_EPS) * ln_w + ln_b
    return (yn + bonus) * g


def _block_diag(x, bmask):
    return jnp.concatenate([x] * 4, axis=0) * bmask


def _chunk_scan(r, logw, k, v, a, b, sbd, tri_ref, bmask):
    n = len(r)
    ch = range(n)
    tri = tri_ref[...]
    cum = [_cumsum_rows(tri, logw[i]) for i in ch]
    e_in = [jnp.exp(cum[i]) for i in ch]
    e_ex = [jnp.exp(cum[i] - logw[i]) for i in ch]
    e_inv = [jnp.exp(-cum[i]) for i in ch]
    e_last = [e_in[i][CHUNK - 1:CHUNK, :] for i in ch]
    rt = [(r[i] * e_in[i]).astype(BF16) for i in ch]
    at = [(a[i] * e_ex[i]).astype(BF16) for i in ch]
    kt = [k[i] * e_inv[i] for i in ch]
    bt = [b[i] * e_inv[i] for i in ch]
    kh = [(kt[i] * e_last[i]).astype(BF16) for i in ch]
    bh = [(bt[i] * e_last[i]).astype(BF16) for i in ch]
    kt = [kt[i].astype(BF16) for i in ch]
    bt = [bt[i].astype(BF16) for i in ch]
    vb = [v[i].astype(BF16) for i in ch]
    sb = [sbd[i].astype(BF16) for i in ch]

    t_idx = lax.broadcasted_iota(jnp.int32, (CHUNK, QUAD), 0)
    s_idx = lax.broadcasted_iota(jnp.int32, (CHUNK, QUAD), 1) & (HEAD_DIM - 1)
    strict = s_idx < t_idx
    incl = s_idx <= t_idx

    gm = [_mm_nt(jnp.concatenate([at[i], rt[i]], axis=0),
                 jnp.concatenate([_block_diag(bt[i], bmask), _block_diag(kt[i], bmask)], axis=0))
          for i in ch]
    a_ab = [jnp.where(strict, gm[i][:CHUNK, :QUAD], 0.0) for i in ch]
    a_ak = [jnp.where(strict, gm[i][:CHUNK, QUAD:], 0.0) for i in ch]
    a_rb = [jnp.where(incl, gm[i][CHUNK:, :QUAD], 0.0) for i in ch]
    a_rk = [jnp.where(incl, gm[i][CHUNK:, QUAD:], 0.0) for i in ch]

    eye = jnp.where(s_idx == t_idx, 1.0, 0.0)
    pw = a_ab
    t_inv = [eye + a_ab[i] for i in ch]
    for it in range(6):
        rbd = [_block_diag(pw[i].astype(BF16), bmask) for i in ch]
        if it == 0:
            pw = [_mm(pw[i], rbd[i]) for i in ch]
        elif it < 5:
            out = [_mm(jnp.concatenate([pw[i], t_inv[i]], axis=0), rbd[i]) for i in ch]
            pw = [out[i][:CHUNK] for i in ch]
            t_inv = [t_inv[i] + out[i][CHUNK:] for i in ch]
        else:
            t_inv = [t_inv[i] + _mm(t_inv[i], rbd[i]) for i in ch]

    vbd = [_block_diag(vb[i], bmask) for i in ch]
    x = [_mm_nt(at[i], sb[i]) + _mm(a_ak[i], vbd[i]) for i in ch]
    u = [_mm(t_inv[i], _block_diag(x[i].astype(BF16), bmask)) for i in ch]
    ub = [u[i].astype(BF16) for i in ch]
    y = [_mm_nt(rt[i], sb[i]) + _mm(jnp.concatenate([a_rb[i], a_rk[i]], axis=1),
                                    jnp.concatenate([_block_diag(ub[i], bmask), vbd[i]], axis=0))
         for i in ch]
    upd = [_mm_tn(jnp.concatenate([ub[i], vb[i]], axis=0), jnp.concatenate([bh[i], kh[i]], axis=0))
           for i in ch]
    bmask_f = bmask.astype(F32)
    s_new = [sbd[i] * e_last[i] + upd[i] * bmask_f for i in ch]
    return y, s_new


def _rwkv_prompt_kernel(p_ref, mu_ref, prm_ref, wl_ref, tri_ref, bmask_ref,
                        y_ref, hout_ref, prev_ref, h_ref):
    c = pl.program_id(0)
    nseq = p_ref.shape[0]

    @pl.when(c == 0)
    def _():
        prev_ref[...] = jnp.zeros_like(prev_ref)
        h_ref[...] = jnp.zeros_like(h_ref)

    row = lax.broadcasted_iota(jnp.int32, (CHUNK, D_SHIFT_PAD), 0)
    xs = []
    for s in range(nseq):
        p = p_ref[s]
        prev = jnp.where(row == 0, jnp.broadcast_to(prev_ref[s, 0:1, :], p.shape), pltpu.roll(p, 1, 0))
        xs.append(p + (prev - p) * mu_ref[...])
        prev_ref[s, 0:1, :] = p[CHUNK - 1:CHUNK, :]
    xs = jnp.concatenate(xs, axis=0)

    prm = prm_ref[...]
    r, logw, k, v, a, b, g, bonus = _rwkv_features(xs, prm, wl_ref, bmask_ref)
    bmask = bmask_ref[...]
    chains = [(s, q) for s in range(nseq) for q in range(2)]
    cut = lambda x: [x[s * CHUNK:(s + 1) * CHUNK, q * QUAD:(q + 1) * QUAD] for s, q in chains]
    ys, h_new = _chunk_scan(cut(r), cut(logw), cut(k), cut(v), cut(a), cut(b),
                            [h_ref[s, q] for s, q in chains], tri_ref, bmask)
    for i, (s, q) in enumerate(chains):
        h_ref[s, q] = h_new[i]
    rows = [jnp.concatenate(ys[2 * s:2 * s + 2], axis=1) for s in range(nseq)]
    y = _rwkv_finish(jnp.concatenate(rows, axis=0), g, bonus, prm, bmask_ref)
    for s in range(nseq):
        y_ref[s] = y[s * CHUNK:(s + 1) * CHUNK]

    @pl.when(c == pl.num_programs(0) - 1)
    def _():
        hout_ref[...] = h_ref[...]


def _rwkv_prompt(p3d, mu_pad, prm, wl, tri, bmask):
    bsz, t, _ = p3d.shape
    nc = t // CHUNK
    const = lambda shape: pl.BlockSpec(shape, lambda c: (0,) * len(shape))
    return pl.pallas_call(
        _rwkv_prompt_kernel,
        grid=(nc,),
        in_specs=[
            pl.BlockSpec((bsz, CHUNK, D_SHIFT_PAD), lambda c: (0, c, 0)),
            const((1, D_SHIFT_PAD)),
            const((16, D_RWKV)),
            const((D_LORA_PAD, 3 * D_RWKV)),
            const((CHUNK, CHUNK)),
            const((QUAD, QUAD)),
        ],
        out_specs=[
            pl.BlockSpec((bsz, CHUNK, D_RWKV), lambda c: (0, c, 0)),
            const((bsz, 2, QUAD, QUAD)),
        ],
        out_shape=[
            jax.ShapeDtypeStruct((bsz, t, D_RWKV), F32),
            jax.ShapeDtypeStruct((bsz, 2, QUAD, QUAD), F32),
        ],
        scratch_shapes=[
            pltpu.VMEM((bsz, 8, D_SHIFT_PAD), F32),
            pltpu.VMEM((bsz, 2, QUAD, QUAD), F32),
        ],
        compiler_params=pltpu.CompilerParams(
            dimension_semantics=("arbitrary",), vmem_limit_bytes=VMEM_LIMIT),
        name="rwkv_prompt",
    )(p3d, mu_pad, prm, wl, tri, bmask)


def _qk_norm_rope(x, norm_w, cos_t, sin_lo, sin_hi, ones_ref):
    ms = _head_sums([x * x], ones_ref)[0] * (1.0 / HEAD_DIM)
    xn = x * lax.rsqrt(ms + RMS_EPS) * norm_w
    width = x.shape[1]
    fwd = pltpu.roll(xn, width - ROPE_HALF, 1)
    bwd = pltpu.roll(xn, ROPE_HALF, 1)
    return xn * cos_t + fwd * sin_lo + bwd * sin_hi


def _tile_lanes(x, reps):
    return jnp.concatenate([x] * reps, axis=1) if reps > 1 else x


def _attn_prompt_kernel(q_ref, kv_ref, tab_ref, qw_ref, kw_ref, sink_ref, ones_ref, hmask_ref,
                        o_ref, kwin_ref, vwin_ref, kprev_ref, vprev_ref):
    i = pl.program_id(1)

    @pl.when(i == 0)
    def _():
        kprev_ref[...] = jnp.zeros_like(kprev_ref)
        vprev_ref[...] = jnp.zeros_like(vprev_ref)

    tab = tab_ref[...]
    cos_t, sin_lo, sin_hi = tab[:, 0:128], tab[:, 128:256], tab[:, 256:384]
    q = _qk_norm_rope(q_ref[0], qw_ref[...], _tile_lanes(cos_t, 4), _tile_lanes(sin_lo, 4),
                      _tile_lanes(sin_hi, 4), ones_ref)
    kv = kv_ref[0]
    k_cur = _qk_norm_rope(kv[:, 0:D_KV], kw_ref[...], cos_t, sin_lo, sin_hi, ones_ref)
    v_cur = kv[:, D_KV:]
    k_all = jnp.concatenate([kprev_ref[...], k_cur], axis=0)
    v_all = jnp.concatenate([vprev_ref[...], v_cur], axis=0)
    kprev_ref[...] = k_cur
    vprev_ref[...] = v_cur
    kwin_ref[0] = k_cur
    vwin_ref[0] = v_cur

    nk = 2 * WINDOW
    qi = lax.broadcasted_iota(jnp.int32, (WINDOW, nk), 0) + WINDOW
    ki = lax.broadcasted_iota(jnp.int32, (WINDOW, nk), 1)
    dq = qi - ki
    kpos = ki + (i - 1) * WINDOW
    mask = (dq >= 0) & (dq < WINDOW) & (kpos >= 0)
    sinks = sink_ref[...]
    low = lax.broadcasted_iota(jnp.int32, (nk, D_KV), 1) < HEAD_DIM
    k_rot = pltpu.roll(k_all, HEAD_DIM, 1)
    v_rot = pltpu.roll(v_all, HEAD_DIM, 1)
    hmask = hmask_ref[...]
    groups = range(N_KV_HEADS)
    heads = range(N_Q_HEADS)
    k2 = [jnp.where(low, k_all, k_rot), jnp.where(low, k_rot, k_all)]
    v2 = [jnp.where(low, v_all, v_rot), jnp.where(low, v_rot, v_all)]
    k4 = [jnp.concatenate([k2[g], k2[g]], axis=1).astype(BF16) for g in groups]
    v4 = [jnp.concatenate([v2[g], v2[g]], axis=1).astype(BF16) for g in groups]
    kstack = [jnp.concatenate([k4[g]] * GQA_GROUP, axis=0) * hmask for g in groups]
    vstack = [jnp.concatenate([v4[g]] * GQA_GROUP, axis=0) * hmask for g in groups]
    s_all = [_mm_nt(q[:, g * QUAD:(g + 1) * QUAD], kstack[g]) * ATTN_SCALE for g in groups]
    s = [jnp.where(mask, s_all[h // GQA_GROUP][:, (h % GQA_GROUP) * nk:(h % GQA_GROUP + 1) * nk], NEG_INF)
         for h in heads]
    m = [jnp.maximum(jnp.max(s[h], axis=-1, keepdims=True), sinks[:, h:h + 1]) for h in heads]
    e = [jnp.exp(s[h] - m[h]) for h in heads]
    denom = [jnp.sum(e[h], axis=-1, keepdims=True) + jnp.exp(sinks[:, h:h + 1] - m[h]) for h in heads]
    pr = [(e[h] / denom[h]).astype(BF16) for h in heads]
    outs = [_mm(jnp.concatenate(pr[g * GQA_GROUP:(g + 1) * GQA_GROUP], axis=1), vstack[g])
            for g in groups]
    o_ref[0] = jnp.concatenate(outs, axis=1)


def _attn_prompt(q3d, kv3d, tab, qw, kw, sinks, ones_bd, hmask):
    bsz, t, _ = q3d.shape
    nb = t // WINDOW
    const = lambda shape: pl.BlockSpec(shape, lambda b, i: (0,) * len(shape))
    return pl.pallas_call(
        _attn_prompt_kernel,
        grid=(bsz, nb),
        in_specs=[
            pl.BlockSpec((1, WINDOW, D_ATTN), lambda b, i: (b, i, 0)),
            pl.BlockSpec((1, WINDOW, 2 * D_KV), lambda b, i: (b, i, 0)),
            pl.BlockSpec((WINDOW, 384), lambda b, i: (i, 0)),
            const((1, D_ATTN)),
            const((1, D_KV)),
            const((1, N_Q_HEADS)),
            const((QUAD, QUAD)),
            const((GQA_GROUP * 2 * WINDOW, QUAD)),
        ],
        out_specs=[
            pl.BlockSpec((1, WINDOW, D_ATTN), lambda b, i: (b, i, 0)),
            pl.BlockSpec((1, WINDOW, D_KV), lambda b, i: (b, 0, 0)),
            pl.BlockSpec((1, WINDOW, D_KV), lambda b, i: (b, 0, 0)),
        ],
        out_shape=[
            jax.ShapeDtypeStruct((bsz, t, D_ATTN), F32),
            jax.ShapeDtypeStruct((bsz, WINDOW, D_KV), F32),
            jax.ShapeDtypeStruct((bsz, WINDOW, D_KV), F32),
        ],
        scratch_shapes=[
            pltpu.VMEM((WINDOW, D_KV), F32),
            pltpu.VMEM((WINDOW, D_KV), F32),
        ],
        compiler_params=pltpu.CompilerParams(
            dimension_semantics=("arbitrary", "arbitrary"), vmem_limit_bytes=VMEM_LIMIT),
        name="attn_prompt",
    )(q3d, kv3d, tab, qw, kw, sinks, ones_bd, hmask)


def _outffn_kernel(x_ref, yr_ref, ya_ref, wo_ref, nw_ref, wu_ref, wd_ref, o_ref):
    mix = jnp.concatenate([yr_ref[...], ya_ref[...]], axis=1).astype(BF16)
    x1 = x_ref[...] + jnp.dot(mix, wo_ref[...], preferred_element_type=F32)
    ms = jnp.mean(x1 * x1, axis=-1, keepdims=True)
    hf = ((x1 * lax.rsqrt(ms + RMS_EPS)) * nw_ref[...]).astype(BF16)
    up = jnp.dot(hf, wu_ref[...], preferred_element_type=F32)
    act = jnp.square(jnp.maximum(up, 0.0)).astype(BF16)
    o_ref[...] = x1 + jnp.dot(act, wd_ref[...], preferred_element_type=F32)


def _outffn(x2d, yr, ya, wo, nw, wu, wd, tm):
    m = x2d.shape[0]
    const = lambda shape: pl.BlockSpec(shape, lambda i: (0,) * len(shape))
    return pl.pallas_call(
        _outffn_kernel,
        grid=(m // tm,),
        in_specs=[
            pl.BlockSpec((tm, D_MODEL), lambda i: (i, 0)),
            pl.BlockSpec((tm, D_RWKV), lambda i: (i, 0)),
            pl.BlockSpec((tm, D_ATTN), lambda i: (i, 0)),
            const((D_MODEL, D_MODEL)),
            const((1, D_MODEL)),
            const((D_MODEL, D_FF)),
            const((D_FF, D_MODEL)),
        ],
        out_specs=pl.BlockSpec((tm, D_MODEL), lambda i: (i, 0)),
        out_shape=jax.ShapeDtypeStruct((m, D_MODEL), F32),
        compiler_params=pltpu.CompilerParams(
            dimension_semantics=("arbitrary",), vmem_limit_bytes=VMEM_LIMIT),
        name="outffn",
    )(x2d, yr, ya, wo, nw, wu, wd)


DEC_TILE = 8


def _decode_prep_kernel(p_ref, sh_ref, q_ref, kv_ref, mu_ref, prm_ref, wl_ref, ones_ref, tab_ref,
                        qw_ref, kw_ref, vec_ref, vgb_ref, qn_ref, kvn_ref):
    p = p_ref[...]
    xs = p + (sh_ref[...] - p) * mu_ref[...]
    r, logw, k, v, a, b, g, bonus = _rwkv_features(xs, prm_ref[...], wl_ref, ones_ref)
    for i, x in enumerate((a, b, k, jnp.exp(logw), r)):
        vec_ref[i] = x
    for i, x in enumerate((v, g, bonus)):
        vgb_ref[i] = x
    tab = tab_ref[...]
    cos_t, sin_lo, sin_hi = tab[:, 0:128], tab[:, 128:256], tab[:, 256:384]
    qn_ref[...] = _qk_norm_rope(q_ref[...], qw_ref[...], _tile_lanes(cos_t, 4), _tile_lanes(sin_lo, 4),
                                _tile_lanes(sin_hi, 4), ones_ref)
    kv = kv_ref[...]
    kvn_ref[:, 0:D_KV] = _qk_norm_rope(kv[:, 0:D_KV], kw_ref[...], cos_t, sin_lo, sin_hi, ones_ref)
    kvn_ref[:, D_KV:] = kv[:, D_KV:]


def _decode_prep(p, shift, q, kv, mu_pad, prm, wl, bmask, tab, qw, kw):
    n = p.shape[0]
    full = lambda shape: pl.BlockSpec(shape, lambda i: (0,) * len(shape))
    return pl.pallas_call(
        _decode_prep_kernel,
        grid=(1,),
        in_specs=[full((n, D_SHIFT_PAD)), full((n, D_SHIFT_PAD)), full((n, D_ATTN)), full((n, 2 * D_KV)),
                  full((1, D_SHIFT_PAD)), full((16, D_RWKV)), full((D_LORA_PAD, 3 * D_RWKV)),
                  full((QUAD, QUAD)), full((n, 384)), full((1, D_ATTN)), full((1, D_KV))],
        out_specs=[full((5, n, D_RWKV)), full((3, n, D_RWKV)), full((n, D_ATTN)), full((n, 2 * D_KV))],
        out_shape=[
            jax.ShapeDtypeStruct((5, n, D_RWKV), F32),
            jax.ShapeDtypeStruct((3, n, D_RWKV), F32),
            jax.ShapeDtypeStruct((n, D_ATTN), F32),
            jax.ShapeDtypeStruct((n, 2 * D_KV), F32),
        ],
        compiler_params=pltpu.CompilerParams(
            dimension_semantics=("arbitrary",), vmem_limit_bytes=VMEM_LIMIT),
        name="decode_prep",
    )(p, shift, q, kv, mu_pad, prm, wl, bmask, tab, qw, kw)


def _decode_step_kernel(vec_ref, vgb_ref, qr_ref, kvn_ref, s_ref, ck_ref, cv_ref, prm_ref, ones_ref,
                        sink_ref, yr_ref, ya_ref, sout_ref, kout_ref, vout_ref):
    nh = H_RWKV
    seqs = range(DEC_TILE)
    dm = D_RWKV
    hrow = lax.broadcasted_iota(jnp.int32, (nh, dm), 0)
    hlane = lax.broadcasted_iota(jnp.int32, (nh, dm), 1) // HEAD_DIM
    dmask = hrow == hlane

    def rows(i, j):
        return vec_ref[i, j * nh:(j + 1) * nh, :]

    a8 = [rows(0, j) for j in seqs]
    b8 = [rows(1, j) for j in seqs]
    k8 = [rows(2, j) for j in seqs]
    w8 = [rows(3, j) for j in seqs]
    r8 = [rows(4, j) for j in seqs]
    v = vgb_ref[0]
    s3 = [s_ref[j] for j in seqs]
    s2 = [s3[j].reshape(nh * HEAD_DIM, HEAD_DIM) for j in seqs]
    sa = [jnp.where(dmask, _dot_nt_f32(a8[j], s2[j]), 0.0) for j in seqs]
    vt = [jnp.where(dmask, jnp.broadcast_to(v[j:j + 1, :], (nh, dm)), 0.0) for j in seqs]
    upd = [_dot_tn_f32(jnp.concatenate([sa[j], vt[j]], axis=0), jnp.concatenate([b8[j], k8[j]], axis=0))
           for j in seqs]
    s_new = [(s3[j] * w8[j][:, None, :]).reshape(nh * HEAD_DIM, HEAD_DIM) + upd[j] for j in seqs]
    for j in seqs:
        sout_ref[j] = s_new[j].reshape(nh, HEAD_DIM, HEAD_DIM)
    y = [jnp.sum(jnp.where(dmask, _dot_nt_f32(r8[j], s_new[j]), 0.0), axis=0, keepdims=True) for j in seqs]
    yr_ref[...] = _rwkv_finish(jnp.concatenate(y, axis=0), vgb_ref[1], vgb_ref[2], prm_ref[...], ones_ref)

    grow = lax.broadcasted_iota(jnp.int32, (nh, D_KV), 0) // GQA_GROUP
    glane = lax.broadcasted_iota(jnp.int32, (nh, D_KV), 1) // HEAD_DIM
    gmask = grow == glane
    low = glane == 0
    key_idx = lax.broadcasted_iota(jnp.int32, (nh, WINDOW), 1)
    sink = sink_ref[...]
    kvn = kvn_ref[...]
    k_new = [kvn[j:j + 1, 0:D_KV] for j in seqs]
    v_new = [kvn[j:j + 1, D_KV:] for j in seqs]
    ck = [ck_ref[j] for j in seqs]
    cv = [cv_ref[j] for j in seqs]
    for j in seqs:
        kout_ref[j, 0:WINDOW - 1, :] = ck[j][1:WINDOW, :]
        kout_ref[j, WINDOW - 1:WINDOW, :] = k_new[j]
        vout_ref[j, 0:WINDOW - 1, :] = cv[j][1:WINDOW, :]
        vout_ref[j, WINDOW - 1:WINDOW, :] = v_new[j]
    q8 = [qr_ref[j * nh:(j + 1) * nh, :] for j in seqs]
    qp = [jnp.where(gmask, jnp.concatenate([q8[j], q8[j]], axis=1), 0.0) for j in seqs]
    s_c = [jnp.where(key_idx >= 1, _dot_nt_f32(qp[j], ck[j]) * ATTN_SCALE, NEG_INF) for j in seqs]
    s_n = [jnp.sum(qp[j] * k_new[j], axis=-1, keepdims=True) * ATTN_SCALE for j in seqs]
    m = [jnp.maximum(jnp.maximum(jnp.max(s_c[j], axis=-1, keepdims=True), s_n[j]), sink) for j in seqs]
    e_c = [jnp.exp(s_c[j] - m[j]) for j in seqs]
    e_n = [jnp.exp(s_n[j] - m[j]) for j in seqs]
    denom = [jnp.sum(e_c[j], axis=-1, keepdims=True) + e_n[j] + jnp.exp(sink - m[j]) for j in seqs]
    o = [(jnp.dot(e_c[j], cv[j], preferred_element_type=F32) + e_n[j] * v_new[j]) / denom[j]
         for j in seqs]
    out_rows = []
    for j in seqs:
        rot = pltpu.roll(o[j], HEAD_DIM, 1)
        g0 = jnp.where(low, o[j], rot)
        g1 = jnp.where(low, rot, o[j])
        wide = jnp.concatenate([g0, g0, g1, g1], axis=1)
        out_rows.append(jnp.sum(jnp.where(dmask, wide, 0.0), axis=0, keepdims=True))
    ya_ref[...] = jnp.concatenate(out_rows, axis=0)


def _decode_step(vec_r, vgb, q_r, kvn, s, ck, cv, prm, bmask, sinks_col):
    n = vgb.shape[1]
    bt = DEC_TILE
    const = lambda shape: pl.BlockSpec(shape, lambda i: (0,) * len(shape))
    return pl.pallas_call(
        _decode_step_kernel,
        grid=(n // bt,),
        in_specs=[
            pl.BlockSpec((5, bt * H_RWKV, HEAD_DIM), lambda i: (0, i, 0)),
            pl.BlockSpec((3, bt, D_RWKV), lambda i: (0, i, 0)),
            pl.BlockSpec((bt * N_Q_HEADS, HEAD_DIM), lambda i: (i, 0)),
            pl.BlockSpec((bt, 2 * D_KV), lambda i: (i, 0)),
            pl.BlockSpec((bt, H_RWKV, HEAD_DIM, HEAD_DIM), lambda i: (i, 0, 0, 0)),
            pl.BlockSpec((bt, WINDOW, D_KV), lambda i: (i, 0, 0)),
            pl.BlockSpec((bt, WINDOW, D_KV), lambda i: (i, 0, 0)),
            const((16, D_RWKV)),
            const((QUAD, QUAD)),
            const((N_Q_HEADS, 1)),
        ],
        out_specs=[
            pl.BlockSpec((bt, D_RWKV), lambda i: (i, 0)),
            pl.BlockSpec((bt, D_ATTN), lambda i: (i, 0)),
            pl.BlockSpec((bt, H_RWKV, HEAD_DIM, HEAD_DIM), lambda i: (i, 0, 0, 0)),
            pl.BlockSpec((bt, WINDOW, D_KV), lambda i: (i, 0, 0)),
            pl.BlockSpec((bt, WINDOW, D_KV), lambda i: (i, 0, 0)),
        ],
        out_shape=[
            jax.ShapeDtypeStruct((n, D_RWKV), F32),
            jax.ShapeDtypeStruct((n, D_ATTN), F32),
            jax.ShapeDtypeStruct((n, H_RWKV, HEAD_DIM, HEAD_DIM), F32),
            jax.ShapeDtypeStruct((n, WINDOW, D_KV), F32),
            jax.ShapeDtypeStruct((n, WINDOW, D_KV), F32),
        ],
        compiler_params=pltpu.CompilerParams(
            dimension_semantics=("arbitrary",), vmem_limit_bytes=VMEM_LIMIT),
        name="decode_step",
    )(vec_r, vgb, q_r, kvn, s, ck, cv, prm, bmask, sinks_col)


def _rope_table(pos):
    inv_freq = jnp.power(ROPE_THETA, -jnp.arange(ROPE_HALF, dtype=F32) * (2.0 / ROPE_DIM))
    dim = np.arange(2 * HEAD_DIM) % HEAD_DIM
    ang = pos[:, None] * inv_freq[dim % ROPE_HALF][None, :]
    cos, sin = jnp.cos(ang), jnp.sin(ang)
    cos_t = jnp.where((dim < ROPE_DIM)[None, :], cos, 1.0)
    sin_lo = jnp.where((dim < ROPE_HALF)[None, :], -sin, 0.0)
    sin_hi = jnp.where(((dim >= ROPE_HALF) & (dim < ROPE_DIM))[None, :], sin, 0.0)
    return jnp.concatenate([cos_t, sin_lo, sin_hi], axis=1)


def _pad_cols(w, at, n):
    return jnp.concatenate([w[..., :at], jnp.zeros(w.shape[:-1] + (n,), w.dtype), w[..., at:]], axis=-1)


def kernel(x_prompt, x_sample, state_wkv, state_shift, cache_k_win, cache_v_win, norm_mix_w, w_in, mu_shift, w0, w_decay_up, a0, w_a_up, w_g_up, k_k, k_a, r_k, ln_x_w, ln_x_b, q_norm_w, k_norm_w, sinks, w_out, norm_ffn_w, w_ffn_up, w_ffn_down):
    bsz, t, _ = x_prompt.shape
    nd = x_sample.shape[0]
    l = 0
    pad = D_LORA_PAD - D_LORA

    w_in_pad = _pad_cols(w_in[l], D_SHIFT, pad).astype(BF16)
    mu_pad = _pad_cols(mu_shift[l][None, :], D_SHIFT, pad)
    wl = jnp.zeros((D_LORA_PAD, 3 * D_RWKV), F32)
    wl = wl.at[0:32, 0:D_RWKV].set(w_decay_up[l])
    wl = wl.at[32:64, D_RWKV:2 * D_RWKV].set(w_a_up[l])
    wl = wl.at[64:160, 2 * D_RWKV:].set(w_g_up[l])
    wl = wl.astype(BF16)
    prm = jnp.zeros((16, D_RWKV), F32)
    prm = prm.at[0].set(w0[l]).at[1].set(a0[l]).at[2].set(k_k[l]).at[3].set(k_a[l])
    prm = prm.at[4].set(r_k[l].reshape(-1)).at[5].set(ln_x_w[l]).at[6].set(ln_x_b[l])
    hid = np.arange(QUAD) // HEAD_DIM
    bmask = jnp.asarray(hid[:, None] == hid[None, :], BF16)
    hmask = jnp.asarray((np.arange(GQA_GROUP * 2 * WINDOW) // (2 * WINDOW))[:, None] == hid[None, :], BF16)
    tri = jnp.asarray(np.tril(np.ones((CHUNK, CHUNK))), BF16)
    qw = jnp.tile(q_norm_w[l][None, :], (1, N_Q_HEADS))
    kw = jnp.tile(k_norm_w[l][None, :], (1, N_KV_HEADS))
    nmw = norm_mix_w[l][None, :]
    nfw = norm_ffn_w[l][None, :]
    wo = w_out[l].astype(BF16)
    wu = w_ffn_up[l].astype(BF16)
    wd = w_ffn_down[l].astype(BF16)
    tab_p = _rope_table(jnp.arange(t, dtype=F32) + 0)
    tab_s = jnp.tile(_rope_table(jnp.arange(1, dtype=F32) + PAST_LEN), (nd, 1))

    xp = x_prompt.reshape(bsz * t, D_MODEL)
    p_p, q_p, kv_p = _inproj(xp, nmw, w_in_pad, 512)
    yr_p, hbd = _rwkv_prompt(p_p.reshape(bsz, t, D_SHIFT_PAD), mu_pad, prm, wl, tri, bmask)
    ya_p, kwin_p, vwin_p = _attn_prompt(q_p.reshape(bsz, t, D_ATTN), kv_p.reshape(bsz, t, 2 * D_KV), tab_p,
                                        qw, kw, sinks[l][None, :], bmask, hmask)
    y_p = _outffn(xp, yr_p.reshape(bsz * t, D_RWKV), ya_p.reshape(bsz * t, D_ATTN), wo, nfw, wu, wd, 256)
    y_prompt = y_p.reshape(bsz, t, D_MODEL)
    hb = hbd.reshape(bsz, 2, 4, HEAD_DIM, 4, HEAD_DIM)
    wkv_prompt = jnp.stack([hb[:, :, j, :, j, :] for j in range(4)], axis=2)
    wkv_prompt = wkv_prompt.reshape(bsz, H_RWKV, HEAD_DIM, HEAD_DIM)[None]
    shift_prompt = p_p.reshape(bsz, t, D_SHIFT_PAD)[:, t - 1:, :D_SHIFT][None]
    k_win_prompt = kwin_p.reshape(bsz, WINDOW, N_KV_HEADS, HEAD_DIM)[None]
    v_win_prompt = vwin_p.reshape(bsz, WINDOW, N_KV_HEADS, HEAD_DIM)[None]

    xs = x_sample.reshape(nd, D_MODEL)
    p_s, q_s, kv_s = _inproj(xs, nmw, w_in_pad, 128)
    shift_in = _pad_cols(state_shift[l].reshape(nd, D_SHIFT), D_SHIFT, pad)
    vec, vgb, qn_s, kvn_s = _decode_prep(p_s, shift_in, q_s, kv_s, mu_pad, prm, wl, bmask, tab_s, qw, kw)
    vec_r = vec.reshape(5, nd * H_RWKV, HEAD_DIM)
    q_r = qn_s.reshape(nd * N_Q_HEADS, HEAD_DIM)
    yr_s, ya_s, wkv_s, kc_s, vc_s = _decode_step(
        vec_r, vgb, q_r, kvn_s, state_wkv[l], cache_k_win[l].reshape(nd, WINDOW, D_KV),
        cache_v_win[l].reshape(nd, WINDOW, D_KV), prm, bmask, sinks[l][:, None])
    y_s = _outffn(xs, yr_s, ya_s, wo, nfw, wu, wd, 128)
    y_sample = y_s.reshape(nd, 1, D_MODEL)
    wkv_sample = wkv_s[None]
    shift_sample = p_s[:, :D_SHIFT].reshape(nd, 1, D_SHIFT)[None]
    k_win_sample = kc_s.reshape(nd, WINDOW, N_KV_HEADS, HEAD_DIM)[None]
    v_win_sample = vc_s.reshape(nd, WINDOW, N_KV_HEADS, HEAD_DIM)[None]

    return (y_prompt, y_sample, wkv_prompt, shift_prompt, k_win_prompt, v_win_prompt,
            wkv_sample, shift_sample, k_win_sample, v_win_sample)
```

```python
import functools

import jax
import jax.numpy as jnp
import numpy as np
from jax import lax
from jax.experimental import pallas as pl
from jax.experimental.pallas import tpu as pltpu

F32 = jnp.float32
BF16 = jnp.bfloat16

D_MODEL = 1024
D_RWKV = 512
D_ATTN = 512
HEAD_DIM = 64
H_RWKV = 8
N_Q_HEADS = 8
N_KV_HEADS = 2
GQA_GROUP = 4
D_KV = 128
D_LORA = 160
D_LORA_PAD = 256
D_SHIFT = 3 * D_RWKV + D_LORA
D_SHIFT_PAD = 3 * D_RWKV + D_LORA_PAD
D_IN_PAD = D_SHIFT_PAD + D_ATTN + 2 * D_KV
WINDOW = 128
ROPE_DIM = 16
ROPE_HALF = 8
ROPE_THETA = 500000.0
ATTN_SCALE = HEAD_DIM ** -0.5
D_FF = 4096
RMS_EPS = 1e-6
LNX_EPS = 64e-5
NEG_INF = -1e30
LOG2E = 1.4426950408889634
PAST_LEN = 16384

CHUNK = 64
QUAD = 4 * HEAD_DIM
VMEM_LIMIT = 56 * 1024 * 1024


def _split2(x):
    hi = x.astype(BF16)
    lo = (x - hi.astype(F32)).astype(BF16)
    return hi, lo


def _head_sums(xs, ones_ref):
    n, w = xs[0].shape
    tile = min(w, QUAD)
    per = w // tile
    pieces = [x[:, j * tile:(j + 1) * tile] for x in xs for j in range(per)]
    stacked = jnp.concatenate(pieces, axis=0) if len(pieces) > 1 else pieces[0]
    ones = ones_ref[0:tile, 0:tile]
    hi, lo = _split2(stacked)
    out = jnp.dot(hi, ones, preferred_element_type=F32) + jnp.dot(lo, ones, preferred_element_type=F32)
    res = []
    for i in range(len(xs)):
        cols = [out[(i * per + j) * n:(i * per + j + 1) * n] for j in range(per)]
        res.append(jnp.concatenate(cols, axis=1) if per > 1 else cols[0])
    return res


def _cumsum_rows(tri_bf16, x):
    hi, lo = _split2(x)
    return (jnp.dot(tri_bf16, hi, preferred_element_type=F32)
            + jnp.dot(tri_bf16, lo, preferred_element_type=F32))


def _mm(a, b):
    return jnp.dot(a.astype(BF16), b.astype(BF16), preferred_element_type=F32)


def _mm_nt(a, b):
    return lax.dot_general(a.astype(BF16), b.astype(BF16), (((1,), (1,)), ((), ())),
                           preferred_element_type=F32)


def _mm_tn(a, b):
    return lax.dot_general(a.astype(BF16), b.astype(BF16), (((0,), (0,)), ((), ())),
                           preferred_element_type=F32)


def _dot_nt_f32(a, b):
    return lax.dot_general(a, b, (((1,), (1,)), ((), ())), preferred_element_type=F32)


def _dot_tn_f32(a, b):
    return lax.dot_general(a, b, (((0,), (0,)), ((), ())), preferred_element_type=F32)


def _sigmoid(x):
    return 1.0 / (1.0 + jnp.exp(-x))


def _softplus(x):
    return jnp.maximum(x, 0.0) + jnp.log1p(jnp.exp(-jnp.abs(x)))


def _inproj_kernel(x_ref, nw_ref, w_ref, p_ref, q_ref, kv_ref):
    x = x_ref[...]
    ms = jnp.mean(x * x, axis=-1, keepdims=True)
    h = (x * lax.rsqrt(ms + RMS_EPS)) * nw_ref[...]
    out = jnp.dot(h.astype(BF16), w_ref[...], preferred_element_type=F32)
    p_ref[...] = out[:, :D_SHIFT_PAD]
    q_ref[...] = out[:, D_SHIFT_PAD:D_SHIFT_PAD + D_ATTN]
    kv_ref[...] = out[:, D_SHIFT_PAD + D_ATTN:]


def _inproj(x2d, norm_w, w_in_pad, tm):
    m = x2d.shape[0]
    return pl.pallas_call(
        _inproj_kernel,
        grid=(m // tm,),
        in_specs=[
            pl.BlockSpec((tm, D_MODEL), lambda i: (i, 0)),
            pl.BlockSpec((1, D_MODEL), lambda i: (0, 0)),
            pl.BlockSpec((D_MODEL, D_IN_PAD), lambda i: (0, 0)),
        ],
        out_specs=[
            pl.BlockSpec((tm, D_SHIFT_PAD), lambda i: (i, 0)),
            pl.BlockSpec((tm, D_ATTN), lambda i: (i, 0)),
            pl.BlockSpec((tm, 2 * D_KV), lambda i: (i, 0)),
        ],
        out_shape=[
            jax.ShapeDtypeStruct((m, D_SHIFT_PAD), F32),
            jax.ShapeDtypeStruct((m, D_ATTN), F32),
            jax.ShapeDtypeStruct((m, 2 * D_KV), F32),
        ],
        compiler_params=pltpu.CompilerParams(
            dimension_semantics=("arbitrary",), vmem_limit_bytes=VMEM_LIMIT),
        name="inproj",
    )(x2d, norm_w, w_in_pad)


def _rwkv_features(xs, prm, wl_ref, ones_ref):
    r = xs[:, 0:D_RWKV]
    k = xs[:, D_RWKV:2 * D_RWKV]
    v = xs[:, 2 * D_RWKV:3 * D_RWKV]
    lora = xs[:, 3 * D_RWKV:]
    col = lax.broadcasted_iota(jnp.int32, lora.shape, 1)
    act = jnp.where(col < 32, jnp.tanh(lora), jnp.where(col < 64, lora, _sigmoid(lora)))
    up = jnp.dot(act.astype(BF16), wl_ref[...], preferred_element_type=F32)
    w0, a0, k_k, k_a, r_k = prm[0:1], prm[1:2], prm[2:3], prm[3:4], prm[4:5]
    w_log = -_softplus(-(w0 + up[:, 0:D_RWKV])) - 0.5
    logw = -jnp.exp(w_log)
    asig = _sigmoid(a0 + up[:, D_RWKV:2 * D_RWKV])
    g = up[:, 2 * D_RWKV:]
    kk = k * k_k
    k_mod = k * (1.0 + (asig - 1.0) * k_a)
    ss, rk = _head_sums([kk * kk, r * k_mod * r_k], ones_ref)
    kk = kk / jnp.maximum(jnp.sqrt(ss), 1e-12)
    k = k_mod
    bonus = rk * v
    return r, logw, k, v, -kk, kk * asig, g, bonus


def _rwkv_finish(y, g, bonus, prm, ones_ref):
    ln_w, ln_b = prm[5:6], prm[6:7]
    mean = _head_sums([y], ones_ref)[0] * (1.0 / HEAD_DIM)
    d = y - mean
    var = _head_sums([d * d], ones_ref)[0] * (1.0 / HEAD_DIM)
    yn = d * lax.rsqrt(var + LNX_EPS) * ln_w + ln_b
    return (yn + bonus) * g


def _block_diag(x, bmask):
    return jnp.concatenate([x] * 4, axis=0) * bmask


def _chunk_scan(r, logw, k, v, a, b, sbd, tri_ref, bmask):
    n = len(r)
    ch = range(n)
    tri = tri_ref[...]
    cum = [_cumsum_rows(tri, logw[i]) for i in ch]
    e_in = [jnp.exp(cum[i]) for i in ch]
    e_ex = [jnp.exp(cum[i] - logw[i]) for i in ch]
    e_inv = [jnp.exp(-cum[i]) for i in ch]
    e_last = [e_in[i][CHUNK - 1:CHUNK, :] for i in ch]
    rt = [(r[i] * e_in[i]).astype(BF16) for i in ch]
    at = [(a[i] * e_ex[i]).astype(BF16) for i in ch]
    kt = [k[i] * e_inv[i] for i in ch]
    bt = [b[i] * e_inv[i] for i in ch]
    kh = [(kt[i] * e_last[i]).astype(BF16) for i in ch]
    bh = [(bt[i] * e_last[i]).astype(BF16) for i in ch]
    kt = [kt[i].astype(BF16) for i in ch]
    bt = [bt[i].astype(BF16) for i in ch]
    vb = [v[i].astype(BF16) for i in ch]
    sb = [sbd[i].astype(BF16) for i in ch]

    t_idx = lax.broadcasted_iota(jnp.int32, (CHUNK, QUAD), 0)
    s_idx = lax.broadcasted_iota(jnp.int32, (CHUNK, QUAD), 1) & (HEAD_DIM - 1)
    strict = s_idx < t_idx
    incl = s_idx <= t_idx

    gm = [_mm_nt(jnp.concatenate([at[i], rt[i]], axis=0),
                 jnp.concatenate([_block_diag(bt[i], bmask), _block_diag(kt[i], bmask)], axis=0))
          for i in ch]
    a_ab = [jnp.where(strict, gm[i][:CHUNK, :QUAD], 0.0) for i in ch]
    a_ak = [jnp.where(strict, gm[i][:CHUNK, QUAD:], 0.0) for i in ch]
    a_rb = [jnp.where(incl, gm[i][CHUNK:, :QUAD], 0.0) for i in ch]
    a_rk = [jnp.where(incl, gm[i][CHUNK:, QUAD:], 0.0) for i in ch]

    eye = jnp.where(s_idx == t_idx, 1.0, 0.0)
    pw = a_ab
    t_inv = [eye + a_ab[i] for i in ch]
    for it in range(6):
        rbd = [_block_diag(pw[i].astype(BF16), bmask) for i in ch]
        if it == 0:
            pw = [_mm(pw[i], rbd[i]) for i in ch]
        elif it < 5:
            out = [_mm(jnp.concatenate([pw[i], t_inv[i]], axis=0), rbd[i]) for i in ch]
            pw = [out[i][:CHUNK] for i in ch]
            t_inv = [t_inv[i] + out[i][CHUNK:] for i in ch]
        else:
            t_inv = [t_inv[i] + _mm(t_inv[i], rbd[i]) for i in ch]

    vbd = [_block_diag(vb[i], bmask) for i in ch]
    x = [_mm_nt(at[i], sb[i]) + _mm(a_ak[i], vbd[i]) for i in ch]
    u = [_mm(t_inv[i], _block_diag(x[i].astype(BF16), bmask)) for i in ch]
    ub = [u[i].astype(BF16) for i in ch]
    y = [_mm_nt(rt[i], sb[i]) + _mm(jnp.concatenate([a_rb[i], a_rk[i]], axis=1),
                                    jnp.concatenate([_block_diag(ub[i], bmask), vbd[i]], axis=0))
         for i in ch]
    upd = [_mm_tn(jnp.concatenate([ub[i], vb[i]], axis=0), jnp.concatenate([bh[i], kh[i]], axis=0))
           for i in ch]
    bmask_f = bmask.astype(F32)
    s_new = [sbd[i] * e_last[i] + upd[i] * bmask_f for i in ch]
    return y, s_new


def _rwkv_prompt_kernel(p_ref, mu_ref, prm_ref, wl_ref, tri_ref, bmask_ref,
                        y_ref, hout_ref, prev_ref, h_ref):
    c = pl.program_id(0)
    nseq = p_ref.shape[0]

    @pl.when(c == 0)
    def _():
        prev_ref[...] = jnp.zeros_like(prev_ref)
        h_ref[...] = jnp.zeros_like(h_ref)

    row = lax.broadcasted_iota(jnp.int32, (CHUNK, D_SHIFT_PAD), 0)
    xs = []
    for s in range(nseq):
        p = p_ref[s]
        prev = jnp.where(row == 0, jnp.broadcast_to(prev_ref[s, 0:1, :], p.shape), pltpu.roll(p, 1, 0))
        xs.append(p + (prev - p) * mu_ref[...])
        prev_ref[s, 0:1, :] = p[CHUNK - 1:CHUNK, :]
    xs = jnp.concatenate(xs, axis=0)

    prm = prm_ref[...]
    r, logw, k, v, a, b, g, bonus = _rwkv_features(xs, prm, wl_ref, bmask_ref)
    bmask = bmask_ref[...]
    chains = [(s, q) for s in range(nseq) for q in range(2)]
    cut = lambda x: [x[s * CHUNK:(s + 1) * CHUNK, q * QUAD:(q + 1) * QUAD] for s, q in chains]
    ys, h_new = _chunk_scan(cut(r), cut(logw), cut(k), cut(v), cut(a), cut(b),
                            [h_ref[s, q] for s, q in chains], tri_ref, bmask)
    for i, (s, q) in enumerate(chains):
        h_ref[s, q] = h_new[i]
    rows = [jnp.concatenate(ys[2 * s:2 * s + 2], axis=1) for s in range(nseq)]
    y = _rwkv_finish(jnp.concatenate(rows, axis=0), g, bonus, prm, bmask_ref)
    for s in range(nseq):
        y_ref[s] = y[s * CHUNK:(s + 1) * CHUNK]

    @pl.when(c == pl.num_programs(0) - 1)
    def _():
        hout_ref[...] = h_ref[...]


def _rwkv_prompt(p3d, mu_pad, prm, wl, tri, bmask):
    bsz, t, _ = p3d.shape
    nc = t // CHUNK
    const = lambda shape: pl.BlockSpec(shape, lambda c: (0,) * len(shape))
    return pl.pallas_call(
        _rwkv_prompt_kernel,
        grid=(nc,),
        in_specs=[
            pl.BlockSpec((bsz, CHUNK, D_SHIFT_PAD), lambda c: (0, c, 0)),
            const((1, D_SHIFT_PAD)),
            const((16, D_RWKV)),
            const((D_LORA_PAD, 3 * D_RWKV)),
            const((CHUNK, CHUNK)),
            const((QUAD, QUAD)),
        ],
        out_specs=[
            pl.BlockSpec((bsz, CHUNK, D_RWKV), lambda c: (0, c, 0)),
            const((bsz, 2, QUAD, QUAD)),
        ],
        out_shape=[
            jax.ShapeDtypeStruct((bsz, t, D_RWKV), F32),
            jax.ShapeDtypeStruct((bsz, 2, QUAD, QUAD), F32),
        ],
        scratch_shapes=[
            pltpu.VMEM((bsz, 8, D_SHIFT_PAD), F32),
            pltpu.VMEM((bsz, 2, QUAD, QUAD), F32),
        ],
        compiler_params=pltpu.CompilerParams(
            dimension_semantics=("arbitrary",), vmem_limit_bytes=VMEM_LIMIT),
        name="rwkv_prompt",
    )(p3d, mu_pad, prm, wl, tri, bmask)


def _qk_norm_rope(x, norm_w, cos_t, sin_lo, sin_hi, ones_ref):
    ms = _head_sums([x * x], ones_ref)[0] * (1.0 / HEAD_DIM)
    xn = x * lax.rsqrt(ms + RMS_EPS) * norm_w
    width = x.shape[1]
    fwd = pltpu.roll(xn, width - ROPE_HALF, 1)
    bwd = pltpu.roll(xn, ROPE_HALF, 1)
    return xn * cos_t + fwd * sin_lo + bwd * sin_hi


def _tile_lanes(x, reps):
    return jnp.concatenate([x] * reps, axis=1) if reps > 1 else x


def _attn_prompt_kernel(q_ref, kv_ref, tab_ref, qw_ref, kw_ref, sink_ref, ones_ref, bias_ref,
                        o_ref, kwin_ref, vwin_ref, kprev_ref, vprev_ref):
    i = pl.program_id(0)
    nseq = q_ref.shape[0]
    seqs = range(nseq)

    @pl.when(i == 0)
    def _():
        kprev_ref[...] = jnp.zeros_like(kprev_ref)
        vprev_ref[...] = jnp.zeros_like(vprev_ref)

    tab = tab_ref[...]
    cos_t, sin_lo, sin_hi = tab[:, 0:128], tab[:, 128:256], tab[:, 256:384]
    cos4, slo4, shi4 = _tile_lanes(cos_t, 4), _tile_lanes(sin_lo, 4), _tile_lanes(sin_hi, 4)
    q = [_qk_norm_rope(q_ref[s], qw_ref[...], cos4, slo4, shi4, ones_ref) * (ATTN_SCALE * LOG2E) for s in seqs]
    kv = [kv_ref[s] for s in seqs]
    k_cur = [_qk_norm_rope(kv[s][:, 0:D_KV], kw_ref[...], cos_t, sin_lo, sin_hi, ones_ref) for s in seqs]
    v_cur = [kv[s][:, D_KV:] for s in seqs]
    k_all = [jnp.concatenate([kprev_ref[s], k_cur[s]], axis=0) for s in seqs]
    v_all = [jnp.concatenate([vprev_ref[s], v_cur[s]], axis=0) for s in seqs]
    for s in seqs:
        kprev_ref[s] = k_cur[s]
        vprev_ref[s] = v_cur[s]

    @pl.when(i == pl.num_programs(0) - 1)
    def _():
        for s in seqs:
            kwin_ref[s] = k_cur[s]
            vwin_ref[s] = v_cur[s]

    nk = 2 * WINDOW
    bias = bias_ref[0]
    sinks = sink_ref[...] * LOG2E
    low = lax.broadcasted_iota(jnp.int32, (nk, D_KV), 1) < HEAD_DIM
    lane_blk = [ones_ref[j * HEAD_DIM:j * HEAD_DIM + 1, :] for j in range(GQA_GROUP)]

    chains = [(s, g) for s in seqs for g in range(N_KV_HEADS)]
    ch = range(len(chains))
    k_rot = [pltpu.roll(k_all[s], HEAD_DIM, 1) for s in seqs]
    k2 = [jnp.where(low, k_all[s], k_rot[s]) if g == 0 else jnp.where(low, k_rot[s], k_all[s])
          for s, g in chains]
    k4 = [jnp.concatenate([k2[c], k2[c]], axis=1).astype(BF16) for c in ch]
    vb = [v_all[s].astype(BF16) for s in seqs]
    qg = [q[s][:, g * QUAD:(g + 1) * QUAD].astype(BF16) for s, g in chains]
    qstack = [jnp.concatenate([qg[c] * lane_blk[j] for j in range(GQA_GROUP)], axis=0) for c in ch]
    sink_row = [jnp.concatenate(
        [jnp.broadcast_to(sinks[:, g * GQA_GROUP + j:g * GQA_GROUP + j + 1], (1, WINDOW))
         for j in range(GQA_GROUP)], axis=1) for s, g in chains]
    sc = [_mm_nt(k4[c], qstack[c]) + bias for c in ch]
    m = [jnp.maximum(jnp.max(sc[c], axis=0, keepdims=True), sink_row[c]) for c in ch]
    e = [jnp.exp2(sc[c] - m[c]) for c in ch]
    denom = [jnp.sum(e[c], axis=0, keepdims=True) + jnp.exp2(sink_row[c] - m[c]) for c in ch]
    ot = [_mm_tn(vb[s], e[c].astype(BF16))[g * HEAD_DIM:(g + 1) * HEAD_DIM, :] * (1.0 / denom[c])
          for c, (s, g) in enumerate(chains)]
    for s in seqs:
        yt = jnp.concatenate([ot[s * N_KV_HEADS + g][:, j * WINDOW:(j + 1) * WINDOW]
                              for g in range(N_KV_HEADS) for j in range(GQA_GROUP)], axis=0)
        o_ref[s] = jnp.transpose(yt)


def _band_bias():
    ki = np.arange(2 * WINDOW)[:, None]
    qi = (np.arange(GQA_GROUP * WINDOW) % WINDOW + WINDOW)[None, :]
    dq = qi - ki
    band = (dq >= 0) & (dq < WINDOW)
    first = band & (ki >= WINDOW)
    return jnp.asarray(np.where(np.stack([first, band]), 0.0, NEG_INF), F32)


def _attn_prompt(q3d, kv3d, tab, qw, kw, sinks, ones_bd, bias):
    bsz, t, _ = q3d.shape
    nb = t // WINDOW
    const = lambda shape: pl.BlockSpec(shape, lambda i: (0,) * len(shape))
    return pl.pallas_call(
        _attn_prompt_kernel,
        grid=(nb,),
        in_specs=[
            pl.BlockSpec((bsz, WINDOW, D_ATTN), lambda i: (0, i, 0)),
            pl.BlockSpec((bsz, WINDOW, 2 * D_KV), lambda i: (0, i, 0)),
            pl.BlockSpec((WINDOW, 384), lambda i: (i, 0)),
            const((1, D_ATTN)),
            const((1, D_KV)),
            const((1, N_Q_HEADS)),
            const((QUAD, QUAD)),
            pl.BlockSpec((1, 2 * WINDOW, GQA_GROUP * WINDOW), lambda i: (jnp.minimum(i, 1), 0, 0)),
        ],
        out_specs=[
            pl.BlockSpec((bsz, WINDOW, D_ATTN), lambda i: (0, i, 0)),
            const((bsz, WINDOW, D_KV)),
            const((bsz, WINDOW, D_KV)),
        ],
        out_shape=[
            jax.ShapeDtypeStruct((bsz, t, D_ATTN), F32),
            jax.ShapeDtypeStruct((bsz, WINDOW, D_KV), F32),
            jax.ShapeDtypeStruct((bsz, WINDOW, D_KV), F32),
        ],
        scratch_shapes=[
            pltpu.VMEM((bsz, WINDOW, D_KV), F32),
            pltpu.VMEM((bsz, WINDOW, D_KV), F32),
        ],
        compiler_params=pltpu.CompilerParams(
            dimension_semantics=("arbitrary",), vmem_limit_bytes=VMEM_LIMIT),
        name="attn_prompt",
    )(q3d, kv3d, tab, qw, kw, sinks, ones_bd, bias)


def _outffn_kernel(x_ref, yr_ref, ya_ref, wo_ref, nw_ref, wu_ref, wd_ref, o_ref):
    mix = jnp.concatenate([yr_ref[...], ya_ref[...]], axis=1).astype(BF16)
    x1 = x_ref[...] + jnp.dot(mix, wo_ref[...], preferred_element_type=F32)
    ms = jnp.mean(x1 * x1, axis=-1, keepdims=True)
    hf = ((x1 * lax.rsqrt(ms + RMS_EPS)) * nw_ref[...]).astype(BF16)
    up = jnp.dot(hf, wu_ref[...], preferred_element_type=F32)
    act = jnp.square(jnp.maximum(up, 0.0)).astype(BF16)
    o_ref[...] = x1 + jnp.dot(act, wd_ref[...], preferred_element_type=F32)


def _outffn(x2d, yr, ya, wo, nw, wu, wd, tm):
    m = x2d.shape[0]
    const = lambda shape: pl.BlockSpec(shape, lambda i: (0,) * len(shape))
    return pl.pallas_call(
        _outffn_kernel,
        grid=(m // tm,),
        in_specs=[
            pl.BlockSpec((tm, D_MODEL), lambda i: (i, 0)),
            pl.BlockSpec((tm, D_RWKV), lambda i: (i, 0)),
            pl.BlockSpec((tm, D_ATTN), lambda i: (i, 0)),
            const((D_MODEL, D_MODEL)),
            const((1, D_MODEL)),
            const((D_MODEL, D_FF)),
            const((D_FF, D_MODEL)),
        ],
        out_specs=pl.BlockSpec((tm, D_MODEL), lambda i: (i, 0)),
        out_shape=jax.ShapeDtypeStruct((m, D_MODEL), F32),
        compiler_params=pltpu.CompilerParams(
            dimension_semantics=("arbitrary",), vmem_limit_bytes=VMEM_LIMIT),
        name="outffn",
    )(x2d, yr, ya, wo, nw, wu, wd)


DEC_TILE = 8


def _decode_prep_kernel(p_ref, sh_ref, q_ref, kv_ref, mu_ref, prm_ref, wl_ref, ones_ref, tab_ref,
                        qw_ref, kw_ref, vec_ref, vgb_ref, qn_ref, kvn_ref):
    p = p_ref[...]
    xs = p + (sh_ref[...] - p) * mu_ref[...]
    r, logw, k, v, a, b, g, bonus = _rwkv_features(xs, prm_ref[...], wl_ref, ones_ref)
    for i, x in enumerate((a, b, k, jnp.exp(logw), r)):
        vec_ref[i] = x
    for i, x in enumerate((v, g, bonus)):
        vgb_ref[i] = x
    tab = tab_ref[...]
    cos_t, sin_lo, sin_hi = tab[:, 0:128], tab[:, 128:256], tab[:, 256:384]
    qn_ref[...] = _qk_norm_rope(q_ref[...], qw_ref[...], _tile_lanes(cos_t, 4), _tile_lanes(sin_lo, 4),
                                _tile_lanes(sin_hi, 4), ones_ref)
    kv = kv_ref[...]
    kvn_ref[:, 0:D_KV] = _qk_norm_rope(kv[:, 0:D_KV], kw_ref[...], cos_t, sin_lo, sin_hi, ones_ref)
    kvn_ref[:, D_KV:] = kv[:, D_KV:]


def _decode_prep(p, shift, q, kv, mu_pad, prm, wl, bmask, tab, qw, kw):
    n = p.shape[0]
    full = lambda shape: pl.BlockSpec(shape, lambda i: (0,) * len(shape))
    return pl.pallas_call(
        _decode_prep_kernel,
        grid=(1,),
        in_specs=[full((n, D_SHIFT_PAD)), full((n, D_SHIFT_PAD)), full((n, D_ATTN)), full((n, 2 * D_KV)),
                  full((1, D_SHIFT_PAD)), full((16, D_RWKV)), full((D_LORA_PAD, 3 * D_RWKV)),
                  full((QUAD, QUAD)), full((n, 384)), full((1, D_ATTN)), full((1, D_KV))],
        out_specs=[full((5, n, D_RWKV)), full((3, n, D_RWKV)), full((n, D_ATTN)), full((n, 2 * D_KV))],
        out_shape=[
            jax.ShapeDtypeStruct((5, n, D_RWKV), F32),
            jax.ShapeDtypeStruct((3, n, D_RWKV), F32),
            jax.ShapeDtypeStruct((n, D_ATTN), F32),
            jax.ShapeDtypeStruct((n, 2 * D_KV), F32),
        ],
        compiler_params=pltpu.CompilerParams(
            dimension_semantics=("arbitrary",), vmem_limit_bytes=VMEM_LIMIT),
        name="decode_prep",
    )(p, shift, q, kv, mu_pad, prm, wl, bmask, tab, qw, kw)


def _decode_step_kernel(vec_ref, vgb_ref, qr_ref, kvn_ref, s_ref, ck_ref, cv_ref, prm_ref, ones_ref,
                        sink_ref, yr_ref, ya_ref, sout_ref, kout_ref, vout_ref):
    nh = H_RWKV
    seqs = range(DEC_TILE)
    dm = D_RWKV
    hrow = lax.broadcasted_iota(jnp.int32, (nh, dm), 0)
    hlane = lax.broadcasted_iota(jnp.int32, (nh, dm), 1) // HEAD_DIM
    dmask = hrow == hlane

    def rows(i, j):
        return vec_ref[i, j * nh:(j + 1) * nh, :]

    a8 = [rows(0, j) for j in seqs]
    b8 = [rows(1, j) for j in seqs]
    k8 = [rows(2, j) for j in seqs]
    w8 = [rows(3, j) for j in seqs]
    r8 = [rows(4, j) for j in seqs]
    v = vgb_ref[0]
    s3 = [s_ref[j] for j in seqs]
    s2 = [s3[j].reshape(nh * HEAD_DIM, HEAD_DIM) for j in seqs]
    sa = [jnp.where(dmask, _dot_nt_f32(a8[j], s2[j]), 0.0) for j in seqs]
    vt = [jnp.where(dmask, jnp.broadcast_to(v[j:j + 1, :], (nh, dm)), 0.0) for j in seqs]
    upd = [_dot_tn_f32(jnp.concatenate([sa[j], vt[j]], axis=0), jnp.concatenate([b8[j], k8[j]], axis=0))
           for j in seqs]
    s_new = [(s3[j] * w8[j][:, None, :]).reshape(nh * HEAD_DIM, HEAD_DIM) + upd[j] for j in seqs]
    for j in seqs:
        sout_ref[j] = s_new[j].reshape(nh, HEAD_DIM, HEAD_DIM)
    y = [jnp.sum(jnp.where(dmask, _dot_nt_f32(r8[j], s_new[j]), 0.0), axis=0, keepdims=True) for j in seqs]
    yr_ref[...] = _rwkv_finish(jnp.concatenate(y, axis=0), vgb_ref[1], vgb_ref[2], prm_ref[...], ones_ref)

    grow = lax.broadcasted_iota(jnp.int32, (nh, D_KV), 0) // GQA_GROUP
    glane = lax.broadcasted_iota(jnp.int32, (nh, D_KV), 1) // HEAD_DIM
    gmask = grow == glane
    low = glane == 0
    key_idx = lax.broadcasted_iota(jnp.int32, (nh, WINDOW), 1)
    sink = sink_ref[...]
    kvn = kvn_ref[...]
    k_new = [kvn[j:j + 1, 0:D_KV] for j in seqs]
    v_new = [kvn[j:j + 1, D_KV:] for j in seqs]
    ck = [ck_ref[j] for j in seqs]
    cv = [cv_ref[j] for j in seqs]
    for j in seqs:
        kout_ref[j, 0:WINDOW - 1, :] = ck[j][1:WINDOW, :]
        kout_ref[j, WINDOW - 1:WINDOW, :] = k_new[j]
        vout_ref[j, 0:WINDOW - 1, :] = cv[j][1:WINDOW, :]
        vout_ref[j, WINDOW - 1:WINDOW, :] = v_new[j]
    q8 = [qr_ref[j * nh:(j + 1) * nh, :] for j in seqs]
    qp = [jnp.where(gmask, jnp.concatenate([q8[j], q8[j]], axis=1), 0.0) for j in seqs]
    s_c = [jnp.where(key_idx >= 1, _dot_nt_f32(qp[j], ck[j]) * ATTN_SCALE, NEG_INF) for j in seqs]
    s_n = [jnp.sum(qp[j] * k_new[j], axis=-1, keepdims=True) * ATTN_SCALE for j in seqs]
    m = [jnp.maximum(jnp.maximum(jnp.max(s_c[j], axis=-1, keepdims=True), s_n[j]), sink) for j in seqs]
    e_c = [jnp.exp(s_c[j] - m[j]) for j in seqs]
    e_n = [jnp.exp(s_n[j] - m[j]) for j in seqs]
    denom = [jnp.sum(e_c[j], axis=-1, keepdims=True) + e_n[j] + jnp.exp(sink - m[j]) for j in seqs]
    o = [(jnp.dot(e_c[j], cv[j], preferred_element_type=F32) + e_n[j] * v_new[j]) / denom[j]
         for j in seqs]
    out_rows = []
    for j in seqs:
        rot = pltpu.roll(o[j], HEAD_DIM, 1)
        g0 = jnp.where(low, o[j], rot)
        g1 = jnp.where(low, rot, o[j])
        wide = jnp.concatenate([g0, g0, g1, g1], axis=1)
        out_rows.append(jnp.sum(jnp.where(dmask, wide, 0.0), axis=0, keepdims=True))
    ya_ref[...] = jnp.concatenate(out_rows, axis=0)


def _decode_step(vec_r, vgb, q_r, kvn, s, ck, cv, prm, bmask, sinks_col):
    n = vgb.shape[1]
    bt = DEC_TILE
    const = lambda shape: pl.BlockSpec(shape, lambda i: (0,) * len(shape))
    return pl.pallas_call(
        _decode_step_kernel,
        grid=(n // bt,),
        in_specs=[
            pl.BlockSpec((5, bt * H_RWKV, HEAD_DIM), lambda i: (0, i, 0)),
            pl.BlockSpec((3, bt, D_RWKV), lambda i: (0, i, 0)),
            pl.BlockSpec((bt * N_Q_HEADS, HEAD_DIM), lambda i: (i, 0)),
            pl.BlockSpec((bt, 2 * D_KV), lambda i: (i, 0)),
            pl.BlockSpec((bt, H_RWKV, HEAD_DIM, HEAD_DIM), lambda i: (i, 0, 0, 0)),
            pl.BlockSpec((bt, WINDOW, D_KV), lambda i: (i, 0, 0)),
            pl.BlockSpec((bt, WINDOW, D_KV), lambda i: (i, 0, 0)),
            const((16, D_RWKV)),
            const((QUAD, QUAD)),
            const((N_Q_HEADS, 1)),
        ],
        out_specs=[
            pl.BlockSpec((bt, D_RWKV), lambda i: (i, 0)),
            pl.BlockSpec((bt, D_ATTN), lambda i: (i, 0)),
            pl.BlockSpec((bt, H_RWKV, HEAD_DIM, HEAD_DIM), lambda i: (i, 0, 0, 0)),
            pl.BlockSpec((bt, WINDOW, D_KV), lambda i: (i, 0, 0)),
            pl.BlockSpec((bt, WINDOW, D_KV), lambda i: (i, 0, 0)),
        ],
        out_shape=[
            jax.ShapeDtypeStruct((n, D_RWKV), F32),
            jax.ShapeDtypeStruct((n, D_ATTN), F32),
            jax.ShapeDtypeStruct((n, H_RWKV, HEAD_DIM, HEAD_DIM), F32),
            jax.ShapeDtypeStruct((n, WINDOW, D_KV), F32),
            jax.ShapeDtypeStruct((n, WINDOW, D_KV), F32),
        ],
        compiler_params=pltpu.CompilerParams(
            dimension_semantics=("arbitrary",), vmem_limit_bytes=VMEM_LIMIT),
        name="decode_step",
    )(vec_r, vgb, q_r, kvn, s, ck, cv, prm, bmask, sinks_col)


def _rope_table(pos):
    inv_freq = jnp.power(ROPE_THETA, -jnp.arange(ROPE_HALF, dtype=F32) * (2.0 / ROPE_DIM))
    dim = np.arange(2 * HEAD_DIM) % HEAD_DIM
    ang = pos[:, None] * inv_freq[dim % ROPE_HALF][None, :]
    cos, sin = jnp.cos(ang), jnp.sin(ang)
    cos_t = jnp.where((dim < ROPE_DIM)[None, :], cos, 1.0)
    sin_lo = jnp.where((dim < ROPE_HALF)[None, :], -sin, 0.0)
    sin_hi = jnp.where(((dim >= ROPE_HALF) & (dim < ROPE_DIM))[None, :], sin, 0.0)
    return jnp.concatenate([cos_t, sin_lo, sin_hi], axis=1)


def _pad_cols(w, at, n):
    return jnp.concatenate([w[..., :at], jnp.zeros(w.shape[:-1] + (n,), w.dtype), w[..., at:]], axis=-1)


def kernel(x_prompt, x_sample, state_wkv, state_shift, cache_k_win, cache_v_win, norm_mix_w, w_in, mu_shift, w0, w_decay_up, a0, w_a_up, w_g_up, k_k, k_a, r_k, ln_x_w, ln_x_b, q_norm_w, k_norm_w, sinks, w_out, norm_ffn_w, w_ffn_up, w_ffn_down):
    bsz, t, _ = x_prompt.shape
    nd = x_sample.shape[0]
    l = 0
    pad = D_LORA_PAD - D_LORA

    w_in_pad = _pad_cols(w_in[l], D_SHIFT, pad).astype(BF16)
    mu_pad = _pad_cols(mu_shift[l][None, :], D_SHIFT, pad)
    wl = jnp.zeros((D_LORA_PAD, 3 * D_RWKV), F32)
    wl = wl.at[0:32, 0:D_RWKV].set(w_decay_up[l])
    wl = wl.at[32:64, D_RWKV:2 * D_RWKV].set(w_a_up[l])
    wl = wl.at[64:160, 2 * D_RWKV:].set(w_g_up[l])
    wl = wl.astype(BF16)
    prm = jnp.zeros((16, D_RWKV), F32)
    prm = prm.at[0].set(w0[l]).at[1].set(a0[l]).at[2].set(k_k[l]).at[3].set(k_a[l])
    prm = prm.at[4].set(r_k[l].reshape(-1)).at[5].set(ln_x_w[l]).at[6].set(ln_x_b[l])
    hid = np.arange(QUAD) // HEAD_DIM
    bmask = jnp.asarray(hid[:, None] == hid[None, :], BF16)
    tri = jnp.asarray(np.tril(np.ones((CHUNK, CHUNK))), BF16)
    qw = jnp.tile(q_norm_w[l][None, :], (1, N_Q_HEADS))
    kw = jnp.tile(k_norm_w[l][None, :], (1, N_KV_HEADS))
    nmw = norm_mix_w[l][None, :]
    nfw = norm_ffn_w[l][None, :]
    wo = w_out[l].astype(BF16)
    wu = w_ffn_up[l].astype(BF16)
    wd = w_ffn_down[l].astype(BF16)
    tab_p = _rope_table(jnp.arange(t, dtype=F32) + 0)
    tab_s = jnp.tile(_rope_table(jnp.arange(1, dtype=F32) + PAST_LEN), (nd, 1))

    xp = x_prompt.reshape(bsz * t, D_MODEL)
    p_p, q_p, kv_p = _inproj(xp, nmw, w_in_pad, 512)
    yr_p, hbd = _rwkv_prompt(p_p.reshape(bsz, t, D_SHIFT_PAD), mu_pad, prm, wl, tri, bmask)
    ya_p, kwin_p, vwin_p = _attn_prompt(q_p.reshape(bsz, t, D_ATTN), kv_p.reshape(bsz, t, 2 * D_KV), tab_p,
                                        qw, kw, sinks[l][None, :], bmask, _band_bias())
    y_p = _outffn(xp, yr_p.reshape(bsz * t, D_RWKV), ya_p.reshape(bsz * t, D_ATTN), wo, nfw, wu, wd, 256)
    y_prompt = y_p.reshape(bsz, t, D_MODEL)
    hb = hbd.reshape(bsz, 2, 4, HEAD_DIM, 4, HEAD_DIM)
    wkv_prompt = jnp.stack([hb[:, :, j, :, j, :] for j in range(4)], axis=2)
    wkv_prompt = wkv_prompt.reshape(bsz, H_RWKV, HEAD_DIM, HEAD_DIM)[None]
    shift_prompt = p_p.reshape(bsz, t, D_SHIFT_PAD)[:, t - 1:, :D_SHIFT][None]
    k_win_prompt = kwin_p.reshape(bsz, WINDOW, N_KV_HEADS, HEAD_DIM)[None]
    v_win_prompt = vwin_p.reshape(bsz, WINDOW, N_KV_HEADS, HEAD_DIM)[None]

    xs = x_sample.reshape(nd, D_MODEL)
    p_s, q_s, kv_s = _inproj(xs, nmw, w_in_pad, 128)
    shift_in = _pad_cols(state_shift[l].reshape(nd, D_SHIFT), D_SHIFT, pad)
    vec, vgb, qn_s, kvn_s = _decode_prep(p_s, shift_in, q_s, kv_s, mu_pad, prm, wl, bmask, tab_s, qw, kw)
    vec_r = vec.reshape(5, nd * H_RWKV, HEAD_DIM)
    q_r = qn_s.reshape(nd * N_Q_HEADS, HEAD_DIM)
    yr_s, ya_s, wkv_s, kc_s, vc_s = _decode_step(
        vec_r, vgb, q_r, kvn_s, state_wkv[l], cache_k_win[l].reshape(nd, WINDOW, D_KV),
        cache_v_win[l].reshape(nd, WINDOW, D_KV), prm, bmask, sinks[l][:, None])
    y_s = _outffn(xs, yr_s, ya_s, wo, nfw, wu, wd, 128)
    y_sample = y_s.reshape(nd, 1, D_MODEL)
    wkv_sample = wkv_s[None]
    shift_sample = p_s[:, :D_SHIFT].reshape(nd, 1, D_SHIFT)[None]
    k_win_sample = kc_s.reshape(nd, WINDOW, N_KV_HEADS, HEAD_DIM)[None]
    v_win_sample = vc_s.reshape(nd, WINDOW, N_KV_HEADS, HEAD_DIM)[None]

    return (y_prompt, y_sample, wkv_prompt, shift_prompt, k_win_prompt, v_win_prompt,
            wkv_sample, shift_sample, k_win_sample, v_win_sample)
```

```python
import functools

import jax
import jax.numpy as jnp
import numpy as np
from jax import lax
from jax.experimental import pallas as pl
from jax.experimental.pallas import tpu as pltpu

F32 = jnp.float32
BF16 = jnp.bfloat16

D_MODEL = 1024
D_RWKV = 512
D_ATTN = 512
HEAD_DIM = 64
H_RWKV = 8
N_Q_HEADS = 8
N_KV_HEADS = 2
GQA_GROUP = 4
D_KV = 128
D_LORA = 160
D_LORA_PAD = 256
D_SHIFT = 3 * D_RWKV + D_LORA
D_SHIFT_PAD = 3 * D_RWKV + D_LORA_PAD
D_IN_PAD = D_SHIFT_PAD + D_ATTN + 2 * D_KV
WINDOW = 128
ROPE_DIM = 16
ROPE_HALF = 8
ROPE_THETA = 500000.0
ATTN_SCALE = HEAD_DIM ** -0.5
D_FF = 4096
RMS_EPS = 1e-6
LNX_EPS = 64e-5
NEG_INF = -1e30
LOG2E = 1.4426950408889634
PAST_LEN = 16384

CHUNK = 64
QUAD = 4 * HEAD_DIM
VMEM_LIMIT = 56 * 1024 * 1024


def _split2(x):
    hi = x.astype(BF16)
    lo = (x - hi.astype(F32)).astype(BF16)
    return hi, lo


def _head_sums(xs, ones_ref):
    n, w = xs[0].shape
    tile = min(w, QUAD)
    per = w // tile
    pieces = [x[:, j * tile:(j + 1) * tile] for x in xs for j in range(per)]
    stacked = jnp.concatenate(pieces, axis=0) if len(pieces) > 1 else pieces[0]
    ones = ones_ref[0:tile, 0:tile]
    hi, lo = _split2(stacked)
    out = jnp.dot(hi, ones, preferred_element_type=F32) + jnp.dot(lo, ones, preferred_element_type=F32)
    res = []
    for i in range(len(xs)):
        cols = [out[(i * per + j) * n:(i * per + j + 1) * n] for j in range(per)]
        res.append(jnp.concatenate(cols, axis=1) if per > 1 else cols[0])
    return res


def _cumsum_rows(tri_bf16, x):
    hi, lo = _split2(x)
    return (jnp.dot(tri_bf16, hi, preferred_element_type=F32)
            + jnp.dot(tri_bf16, lo, preferred_element_type=F32))


def _mm(a, b):
    return jnp.dot(a.astype(BF16), b.astype(BF16), preferred_element_type=F32)


def _mm_nt(a, b):
    return lax.dot_general(a.astype(BF16), b.astype(BF16), (((1,), (1,)), ((), ())),
                           preferred_element_type=F32)


def _mm_tn(a, b):
    return lax.dot_general(a.astype(BF16), b.astype(BF16), (((0,), (0,)), ((), ())),
                           preferred_element_type=F32)


def _dot_nt_f32(a, b):
    return lax.dot_general(a, b, (((1,), (1,)), ((), ())), preferred_element_type=F32)


def _dot_tn_f32(a, b):
    return lax.dot_general(a, b, (((0,), (0,)), ((), ())), preferred_element_type=F32)


def _sigmoid(x):
    return 1.0 / (1.0 + jnp.exp(-x))


def _softplus(x):
    return jnp.maximum(x, 0.0) + jnp.log1p(jnp.exp(-jnp.abs(x)))


def _inproj_kernel(x_ref, nw_ref, w_ref, p_ref, q_ref, kv_ref):
    x = x_ref[...]
    ms = jnp.mean(x * x, axis=-1, keepdims=True)
    h = (x * lax.rsqrt(ms + RMS_EPS)) * nw_ref[...]
    out = jnp.dot(h.astype(BF16), w_ref[...], preferred_element_type=F32)
    p_ref[...] = out[:, :D_SHIFT_PAD]
    q_ref[...] = out[:, D_SHIFT_PAD:D_SHIFT_PAD + D_ATTN]
    kv_ref[...] = out[:, D_SHIFT_PAD + D_ATTN:]


def _inproj(x2d, norm_w, w_in_pad, tm):
    m = x2d.shape[0]
    return pl.pallas_call(
        _inproj_kernel,
        grid=(m // tm,),
        in_specs=[
            pl.BlockSpec((tm, D_MODEL), lambda i: (i, 0)),
            pl.BlockSpec((1, D_MODEL), lambda i: (0, 0)),
            pl.BlockSpec((D_MODEL, D_IN_PAD), lambda i: (0, 0)),
        ],
        out_specs=[
            pl.BlockSpec((tm, D_SHIFT_PAD), lambda i: (i, 0)),
            pl.BlockSpec((tm, D_ATTN), lambda i: (i, 0)),
            pl.BlockSpec((tm, 2 * D_KV), lambda i: (i, 0)),
        ],
        out_shape=[
            jax.ShapeDtypeStruct((m, D_SHIFT_PAD), F32),
            jax.ShapeDtypeStruct((m, D_ATTN), F32),
            jax.ShapeDtypeStruct((m, 2 * D_KV), F32),
        ],
        compiler_params=pltpu.CompilerParams(
            dimension_semantics=("arbitrary",), vmem_limit_bytes=VMEM_LIMIT),
        name="inproj",
    )(x2d, norm_w, w_in_pad)


def _rwkv_features(xs, prm, wl_ref, ones_ref):
    r = xs[:, 0:D_RWKV]
    k = xs[:, D_RWKV:2 * D_RWKV]
    v = xs[:, 2 * D_RWKV:3 * D_RWKV]
    lora = xs[:, 3 * D_RWKV:]
    col = lax.broadcasted_iota(jnp.int32, lora.shape, 1)
    act = jnp.where(col < 32, jnp.tanh(lora), jnp.where(col < 64, lora, _sigmoid(lora)))
    up = jnp.dot(act.astype(BF16), wl_ref[...], preferred_element_type=F32)
    w0, a0, k_k, k_a, r_k = prm[0:1], prm[1:2], prm[2:3], prm[3:4], prm[4:5]
    w_log = -_softplus(-(w0 + up[:, 0:D_RWKV])) - 0.5
    logw = -jnp.exp(w_log)
    asig = _sigmoid(a0 + up[:, D_RWKV:2 * D_RWKV])
    g = up[:, 2 * D_RWKV:]
    kk = k * k_k
    k_mod = k * (1.0 + (asig - 1.0) * k_a)
    ss, rk = _head_sums([kk * kk, r * k_mod * r_k], ones_ref)
    kk = kk / jnp.maximum(jnp.sqrt(ss), 1e-12)
    k = k_mod
    bonus = rk * v
    return r, logw, k, v, -kk, kk * asig, g, bonus


def _rwkv_finish(y, g, bonus, prm, ones_ref):
    ln_w, ln_b = prm[5:6], prm[6:7]
    mean = _head_sums([y], ones_ref)[0] * (1.0 / HEAD_DIM)
    d = y - mean
    var = _head_sums([d * d], ones_ref)[0] * (1.0 / HEAD_DIM)
    yn = d * lax.rsqrt(var + LNX_EPS) * ln_w + ln_b
    return (yn + bonus) * g


def _block_diag(x, bmask):
    return jnp.concatenate([x] * 4, axis=0) * bmask


def _chunk_scan(r, logw, k, v, a, b, sbd, tri_ref, bmask):
    n = len(r)
    ch = range(n)
    tri = tri_ref[...]
    cum = [_cumsum_rows(tri, logw[i]) for i in ch]
    e_in = [jnp.exp(cum[i]) for i in ch]
    e_ex = [jnp.exp(cum[i] - logw[i]) for i in ch]
    e_inv = [jnp.exp(-cum[i]) for i in ch]
    e_last = [e_in[i][CHUNK - 1:CHUNK, :] for i in ch]
    rt = [(r[i] * e_in[i]).astype(BF16) for i in ch]
    at = [(a[i] * e_ex[i]).astype(BF16) for i in ch]
    kt = [k[i] * e_inv[i] for i in ch]
    bt = [b[i] * e_inv[i] for i in ch]
    kh = [(kt[i] * e_last[i]).astype(BF16) for i in ch]
    bh = [(bt[i] * e_last[i]).astype(BF16) for i in ch]
    kt = [kt[i].astype(BF16) for i in ch]
    bt = [bt[i].astype(BF16) for i in ch]
    vb = [v[i].astype(BF16) for i in ch]
    sb = [sbd[i].astype(BF16) for i in ch]

    t_idx = lax.broadcasted_iota(jnp.int32, (CHUNK, QUAD), 0)
    s_idx = lax.broadcasted_iota(jnp.int32, (CHUNK, QUAD), 1) & (HEAD_DIM - 1)
    strict = s_idx < t_idx
    incl = s_idx <= t_idx

    gm = [_mm_nt(jnp.concatenate([at[i], rt[i]], axis=0),
                 jnp.concatenate([_block_diag(bt[i], bmask), _block_diag(kt[i], bmask)], axis=0))
          for i in ch]
    a_ab = [jnp.where(strict, gm[i][:CHUNK, :QUAD], 0.0) for i in ch]
    a_ak = [jnp.where(strict, gm[i][:CHUNK, QUAD:], 0.0) for i in ch]
    a_rb = [jnp.where(incl, gm[i][CHUNK:, :QUAD], 0.0) for i in ch]
    a_rk = [jnp.where(incl, gm[i][CHUNK:, QUAD:], 0.0) for i in ch]

    eye = jnp.where(s_idx == t_idx, 1.0, 0.0)
    pw = a_ab
    t_inv = [eye + a_ab[i] for i in ch]
    for it in range(6):
        rbd = [_block_diag(pw[i].astype(BF16), bmask) for i in ch]
        if it == 0:
            pw = [_mm(pw[i], rbd[i]) for i in ch]
        elif it < 5:
            out = [_mm(jnp.concatenate([pw[i], t_inv[i]], axis=0), rbd[i]) for i in ch]
            pw = [out[i][:CHUNK] for i in ch]
            t_inv = [t_inv[i] + out[i][CHUNK:] for i in ch]
        else:
            t_inv = [t_inv[i] + _mm(t_inv[i], rbd[i]) for i in ch]

    vbd = [_block_diag(vb[i], bmask) for i in ch]
    x = [_mm_nt(at[i], sb[i]) + _mm(a_ak[i], vbd[i]) for i in ch]
    u = [_mm(t_inv[i], _block_diag(x[i].astype(BF16), bmask)) for i in ch]
    ub = [u[i].astype(BF16) for i in ch]
    y = [_mm_nt(rt[i], sb[i]) + _mm(jnp.concatenate([a_rb[i], a_rk[i]], axis=1),
                                    jnp.concatenate([_block_diag(ub[i], bmask), vbd[i]], axis=0))
         for i in ch]
    upd = [_mm_tn(jnp.concatenate([ub[i], vb[i]], axis=0), jnp.concatenate([bh[i], kh[i]], axis=0))
           for i in ch]
    bmask_f = bmask.astype(F32)
    s_new = [sbd[i] * e_last[i] + upd[i] * bmask_f for i in ch]
    return y, s_new


def _rwkv_prompt_kernel(p_ref, mu_ref, prm_ref, wl_ref, tri_ref, bmask_ref,
                        y_ref, hout_ref, prev_ref, h_ref):
    c = pl.program_id(0)
    nseq = p_ref.shape[0]

    @pl.when(c == 0)
    def _():
        prev_ref[...] = jnp.zeros_like(prev_ref)
        h_ref[...] = jnp.zeros_like(h_ref)

    row = lax.broadcasted_iota(jnp.int32, (CHUNK, D_SHIFT_PAD), 0)
    xs = []
    for s in range(nseq):
        p = p_ref[s]
        prev = jnp.where(row == 0, jnp.broadcast_to(prev_ref[s, 0:1, :], p.shape), pltpu.roll(p, 1, 0))
        xs.append(p + (prev - p) * mu_ref[...])
        prev_ref[s, 0:1, :] = p[CHUNK - 1:CHUNK, :]
    xs = jnp.concatenate(xs, axis=0)

    prm = prm_ref[...]
    r, logw, k, v, a, b, g, bonus = _rwkv_features(xs, prm, wl_ref, bmask_ref)
    bmask = bmask_ref[...]
    chains = [(s, q) for s in range(nseq) for q in range(2)]
    cut = lambda x: [x[s * CHUNK:(s + 1) * CHUNK, q * QUAD:(q + 1) * QUAD] for s, q in chains]
    ys, h_new = _chunk_scan(cut(r), cut(logw), cut(k), cut(v), cut(a), cut(b),
                            [h_ref[s, q] for s, q in chains], tri_ref, bmask)
    for i, (s, q) in enumerate(chains):
        h_ref[s, q] = h_new[i]
    rows = [jnp.concatenate(ys[2 * s:2 * s + 2], axis=1) for s in range(nseq)]
    y = _rwkv_finish(jnp.concatenate(rows, axis=0), g, bonus, prm, bmask_ref)
    for s in range(nseq):
        y_ref[s] = y[s * CHUNK:(s + 1) * CHUNK]

    @pl.when(c == pl.num_programs(0) - 1)
    def _():
        hout_ref[...] = h_ref[...]


def _rwkv_prompt(p3d, mu_pad, prm, wl, tri, bmask):
    bsz, t, _ = p3d.shape
    nc = t // CHUNK
    const = lambda shape: pl.BlockSpec(shape, lambda c: (0,) * len(shape))
    return pl.pallas_call(
        _rwkv_prompt_kernel,
        grid=(nc,),
        in_specs=[
            pl.BlockSpec((bsz, CHUNK, D_SHIFT_PAD), lambda c: (0, c, 0)),
            const((1, D_SHIFT_PAD)),
            const((16, D_RWKV)),
            const((D_LORA_PAD, 3 * D_RWKV)),
            const((CHUNK, CHUNK)),
            const((QUAD, QUAD)),
        ],
        out_specs=[
            pl.BlockSpec((bsz, CHUNK, D_RWKV), lambda c: (0, c, 0)),
            const((bsz, 2, QUAD, QUAD)),
        ],
        out_shape=[
            jax.ShapeDtypeStruct((bsz, t, D_RWKV), F32),
            jax.ShapeDtypeStruct((bsz, 2, QUAD, QUAD), F32),
        ],
        scratch_shapes=[
            pltpu.VMEM((bsz, 8, D_SHIFT_PAD), F32),
            pltpu.VMEM((bsz, 2, QUAD, QUAD), F32),
        ],
        compiler_params=pltpu.CompilerParams(
            dimension_semantics=("arbitrary",), vmem_limit_bytes=VMEM_LIMIT),
        name="rwkv_prompt",
    )(p3d, mu_pad, prm, wl, tri, bmask)


def _qk_norm_rope(x, norm_w, cos_t, sin_lo, sin_hi, ones_ref):
    ms = _head_sums([x * x], ones_ref)[0] * (1.0 / HEAD_DIM)
    xn = x * lax.rsqrt(ms + RMS_EPS) * norm_w
    width = x.shape[1]
    fwd = pltpu.roll(xn, width - ROPE_HALF, 1)
    bwd = pltpu.roll(xn, ROPE_HALF, 1)
    return xn * cos_t + fwd * sin_lo + bwd * sin_hi


def _tile_lanes(x, reps):
    return jnp.concatenate([x] * reps, axis=1) if reps > 1 else x


def _attn_prompt_kernel(q_ref, kv_ref, tab_ref, qw_ref, kw_ref, sink_ref, ones_ref, bias_ref,
                        o_ref, kwin_ref, vwin_ref, kprev_ref, vprev_ref):
    i = pl.program_id(0)
    nseq = q_ref.shape[0]
    seqs = range(nseq)

    @pl.when(i == 0)
    def _():
        kprev_ref[...] = jnp.zeros_like(kprev_ref)
        vprev_ref[...] = jnp.zeros_like(vprev_ref)

    tab = tab_ref[...]
    cos_t, sin_lo, sin_hi = tab[:, 0:128], tab[:, 128:256], tab[:, 256:384]
    cos4, slo4, shi4 = _tile_lanes(cos_t, 4), _tile_lanes(sin_lo, 4), _tile_lanes(sin_hi, 4)
    q = [_qk_norm_rope(q_ref[s], qw_ref[...], cos4, slo4, shi4, ones_ref) * (ATTN_SCALE * LOG2E) for s in seqs]
    kv = [kv_ref[s] for s in seqs]
    k_cur = [_qk_norm_rope(kv[s][:, 0:D_KV], kw_ref[...], cos_t, sin_lo, sin_hi, ones_ref) for s in seqs]
    v_cur = [kv[s][:, D_KV:] for s in seqs]
    k_all = [jnp.concatenate([kprev_ref[s], k_cur[s]], axis=0) for s in seqs]
    v_all = [jnp.concatenate([vprev_ref[s], v_cur[s]], axis=0) for s in seqs]
    for s in seqs:
        kprev_ref[s] = k_cur[s]
        vprev_ref[s] = v_cur[s]

    @pl.when(i == pl.num_programs(0) - 1)
    def _():
        for s in seqs:
            kwin_ref[s] = k_cur[s]
            vwin_ref[s] = v_cur[s]

    nk = 2 * WINDOW
    bias = bias_ref[0]
    sinks = sink_ref[...] * LOG2E
    low = lax.broadcasted_iota(jnp.int32, (nk, D_KV), 1) < HEAD_DIM
    lane_blk = [ones_ref[j * HEAD_DIM:j * HEAD_DIM + 1, :] for j in range(GQA_GROUP)]

    chains = [(s, g) for s in seqs for g in range(N_KV_HEADS)]
    ch = range(len(chains))
    k_rot = [pltpu.roll(k_all[s], HEAD_DIM, 1) for s in seqs]
    k2 = [jnp.where(low, k_all[s], k_rot[s]) if g == 0 else jnp.where(low, k_rot[s], k_all[s])
          for s, g in chains]
    k4 = [jnp.concatenate([k2[c], k2[c]], axis=1).astype(BF16) for c in ch]
    vb = [v_all[s].astype(BF16) for s in seqs]
    qg = [q[s][:, g * QUAD:(g + 1) * QUAD].astype(BF16) for s, g in chains]
    qstack = [jnp.concatenate([qg[c] * lane_blk[j] for j in range(GQA_GROUP)], axis=0) for c in ch]
    sink_row = [jnp.concatenate(
        [jnp.broadcast_to(sinks[:, g * GQA_GROUP + j:g * GQA_GROUP + j + 1], (1, WINDOW))
         for j in range(GQA_GROUP)], axis=1) for s, g in chains]
    sc = [_mm_nt(k4[c], qstack[c]) + bias for c in ch]
    m = [jnp.maximum(jnp.max(sc[c], axis=0, keepdims=True), sink_row[c]) for c in ch]
    e = [jnp.exp2(sc[c] - m[c]) for c in ch]
    denom = [jnp.sum(e[c], axis=0, keepdims=True) + jnp.exp2(sink_row[c] - m[c]) for c in ch]
    ot = [_mm_tn(vb[s], e[c].astype(BF16))[g * HEAD_DIM:(g + 1) * HEAD_DIM, :] * (1.0 / denom[c])
          for c, (s, g) in enumerate(chains)]
    for s in seqs:
        yt = jnp.concatenate([ot[s * N_KV_HEADS + g][:, j * WINDOW:(j + 1) * WINDOW]
                              for g in range(N_KV_HEADS) for j in range(GQA_GROUP)], axis=0)
        o_ref[s] = jnp.transpose(yt)


def _band_bias():
    ki = np.arange(2 * WINDOW)[:, None]
    qi = (np.arange(GQA_GROUP * WINDOW) % WINDOW + WINDOW)[None, :]
    dq = qi - ki
    band = (dq >= 0) & (dq < WINDOW)
    first = band & (ki >= WINDOW)
    return jnp.asarray(np.where(np.stack([first, band]), 0.0, NEG_INF), F32)


def _attn_prompt(q3d, kv3d, tab, qw, kw, sinks, ones_bd, bias):
    bsz, t, _ = q3d.shape
    nb = t // WINDOW
    const = lambda shape: pl.BlockSpec(shape, lambda i: (0,) * len(shape))
    return pl.pallas_call(
        _attn_prompt_kernel,
        grid=(nb,),
        in_specs=[
            pl.BlockSpec((bsz, WINDOW, D_ATTN), lambda i: (0, i, 0)),
            pl.BlockSpec((bsz, WINDOW, 2 * D_KV), lambda i: (0, i, 0)),
            pl.BlockSpec((WINDOW, 384), lambda i: (i, 0)),
            const((1, D_ATTN)),
            const((1, D_KV)),
            const((1, N_Q_HEADS)),
            const((QUAD, QUAD)),
            pl.BlockSpec((1, 2 * WINDOW, GQA_GROUP * WINDOW), lambda i: (jnp.minimum(i, 1), 0, 0)),
        ],
        out_specs=[
            pl.BlockSpec((bsz, WINDOW, D_ATTN), lambda i: (0, i, 0)),
            const((bsz, WINDOW, D_KV)),
            const((bsz, WINDOW, D_KV)),
        ],
        out_shape=[
            jax.ShapeDtypeStruct((bsz, t, D_ATTN), F32),
            jax.ShapeDtypeStruct((bsz, WINDOW, D_KV), F32),
            jax.ShapeDtypeStruct((bsz, WINDOW, D_KV), F32),
        ],
        scratch_shapes=[
            pltpu.VMEM((bsz, WINDOW, D_KV), F32),
            pltpu.VMEM((bsz, WINDOW, D_KV), F32),
        ],
        compiler_params=pltpu.CompilerParams(
            dimension_semantics=("arbitrary",), vmem_limit_bytes=VMEM_LIMIT),
        name="attn_prompt",
    )(q3d, kv3d, tab, qw, kw, sinks, ones_bd, bias)


def _outffn_kernel(x_ref, yr_ref, ya_ref, wo_ref, nw_ref, wu_ref, wd_ref, o_ref):
    mix = jnp.concatenate([yr_ref[...], ya_ref[...]], axis=1).astype(BF16)
    x1 = x_ref[...] + jnp.dot(mix, wo_ref[...], preferred_element_type=F32)
    ms = jnp.mean(x1 * x1, axis=-1, keepdims=True)
    hf = ((x1 * lax.rsqrt(ms + RMS_EPS)) * nw_ref[...]).astype(BF16)
    up = jnp.dot(hf, wu_ref[...], preferred_element_type=F32)
    act = jnp.square(jnp.maximum(up, 0.0)).astype(BF16)
    o_ref[...] = x1 + jnp.dot(act, wd_ref[...], preferred_element_type=F32)


def _outffn(x2d, yr, ya, wo, nw, wu, wd, tm):
    m = x2d.shape[0]
    const = lambda shape: pl.BlockSpec(shape, lambda i: (0,) * len(shape))
    return pl.pallas_call(
        _outffn_kernel,
        grid=(m // tm,),
        in_specs=[
            pl.BlockSpec((tm, D_MODEL), lambda i: (i, 0)),
            pl.BlockSpec((tm, D_RWKV), lambda i: (i, 0)),
            pl.BlockSpec((tm, D_ATTN), lambda i: (i, 0)),
            const((D_MODEL, D_MODEL)),
            const((1, D_MODEL)),
            const((D_MODEL, D_FF)),
            const((D_FF, D_MODEL)),
        ],
        out_specs=pl.BlockSpec((tm, D_MODEL), lambda i: (i, 0)),
        out_shape=jax.ShapeDtypeStruct((m, D_MODEL), F32),
        compiler_params=pltpu.CompilerParams(
            dimension_semantics=("arbitrary",), vmem_limit_bytes=VMEM_LIMIT),
        name="outffn",
    )(x2d, yr, ya, wo, nw, wu, wd)


DEC_TILE = 8


def _decode_prep_kernel(p_ref, sh_ref, q_ref, kv_ref, mu_ref, prm_ref, wl_ref, ones_ref, tab_ref,
                        qw_ref, kw_ref, vec_ref, vgb_ref, qn_ref, kvn_ref):
    p = p_ref[...]
    xs = p + (sh_ref[...] - p) * mu_ref[...]
    r, logw, k, v, a, b, g, bonus = _rwkv_features(xs, prm_ref[...], wl_ref, ones_ref)
    for i, x in enumerate((a, b, k, jnp.exp(logw), r, v)):
        vec_ref[i] = jnp.transpose(x)
    vgb_ref[0] = g
    vgb_ref[1] = bonus
    tab = tab_ref[...]
    cos_t, sin_lo, sin_hi = tab[:, 0:128], tab[:, 128:256], tab[:, 256:384]
    qn_ref[...] = _qk_norm_rope(q_ref[...], qw_ref[...], _tile_lanes(cos_t, 4), _tile_lanes(sin_lo, 4),
                                _tile_lanes(sin_hi, 4), ones_ref)
    kv = kv_ref[...]
    kvn_ref[:, 0:D_KV] = _qk_norm_rope(kv[:, 0:D_KV], kw_ref[...], cos_t, sin_lo, sin_hi, ones_ref)
    kvn_ref[:, D_KV:] = kv[:, D_KV:]


def _decode_prep(p, shift, q, kv, mu_pad, prm, wl, bmask, tab, qw, kw):
    n = p.shape[0]
    full = lambda shape: pl.BlockSpec(shape, lambda i: (0,) * len(shape))
    return pl.pallas_call(
        _decode_prep_kernel,
        grid=(1,),
        in_specs=[full((n, D_SHIFT_PAD)), full((n, D_SHIFT_PAD)), full((n, D_ATTN)), full((n, 2 * D_KV)),
                  full((1, D_SHIFT_PAD)), full((16, D_RWKV)), full((D_LORA_PAD, 3 * D_RWKV)),
                  full((QUAD, QUAD)), full((n, 384)), full((1, D_ATTN)), full((1, D_KV))],
        out_specs=[full((6, D_RWKV, n)), full((2, n, D_RWKV)), full((n, D_ATTN)), full((n, 2 * D_KV))],
        out_shape=[
            jax.ShapeDtypeStruct((6, D_RWKV, n), F32),
            jax.ShapeDtypeStruct((2, n, D_RWKV), F32),
            jax.ShapeDtypeStruct((n, D_ATTN), F32),
            jax.ShapeDtypeStruct((n, 2 * D_KV), F32),
        ],
        compiler_params=pltpu.CompilerParams(
            dimension_semantics=("arbitrary",), vmem_limit_bytes=VMEM_LIMIT),
        name="decode_prep",
    )(p, shift, q, kv, mu_pad, prm, wl, bmask, tab, qw, kw)


def _decode_state_kernel(vec_ref, gb_ref, prm_ref, ones_ref, s_ref, sout_ref, yr_ref, yt_ref):
    h = pl.program_id(0)
    a_t, b_t, k_t, w_t, r_t = (vec_ref[i] for i in range(5))

    def body(i, carry):
        s = s_ref[0, i]
        sa = jnp.sum(s * a_t, axis=0, keepdims=True)
        v_i = vec_ref[5, pl.ds(i, 1), :]
        s_new = s * w_t + sa * b_t + v_i * k_t
        sout_ref[0, i] = s_new
        yt_ref[pl.ds(h * HEAD_DIM + i, 1), :] = jnp.sum(s_new * r_t, axis=0, keepdims=True)
        return carry

    lax.fori_loop(0, HEAD_DIM, body, 0, unroll=8)

    @pl.when(h == pl.num_programs(0) - 1)
    def _():
        y = jnp.transpose(yt_ref[...])
        yr_ref[...] = _rwkv_finish(y, gb_ref[0], gb_ref[1], prm_ref[...], ones_ref)


def _decode_state(vec_t, gb, prm, bmask, s_t):
    n = s_t.shape[-1]
    const = lambda shape: pl.BlockSpec(shape, lambda h: (0,) * len(shape))
    return pl.pallas_call(
        _decode_state_kernel,
        grid=(H_RWKV,),
        in_specs=[
            pl.BlockSpec((6, HEAD_DIM, n), lambda h: (0, h, 0)),
            const((2, n, D_RWKV)),
            const((16, D_RWKV)),
            const((QUAD, QUAD)),
            pl.BlockSpec((1, HEAD_DIM, HEAD_DIM, n), lambda h: (h, 0, 0, 0)),
        ],
        out_specs=[
            pl.BlockSpec((1, HEAD_DIM, HEAD_DIM, n), lambda h: (h, 0, 0, 0)),
            const((n, D_RWKV)),
        ],
        out_shape=[
            jax.ShapeDtypeStruct((H_RWKV, HEAD_DIM, HEAD_DIM, n), F32),
            jax.ShapeDtypeStruct((n, D_RWKV), F32),
        ],
        scratch_shapes=[pltpu.VMEM((D_RWKV, n), F32)],
        compiler_params=pltpu.CompilerParams(
            dimension_semantics=("arbitrary",), vmem_limit_bytes=VMEM_LIMIT),
        name="decode_state",
    )(vec_t, gb, prm, bmask, s_t)


def _decode_attn_kernel(qr_ref, kvn_ref, col_ref, ck_ref, cv_ref, sink_ref, ya_ref, kout_ref, vout_ref):
    nh = N_Q_HEADS
    seqs = range(DEC_TILE)
    hrow = lax.broadcasted_iota(jnp.int32, (nh, D_ATTN), 0)
    hlane = lax.broadcasted_iota(jnp.int32, (nh, D_ATTN), 1) // HEAD_DIM
    dmask = hrow == hlane
    grow = lax.broadcasted_iota(jnp.int32, (nh, D_KV), 0) // GQA_GROUP
    glane = lax.broadcasted_iota(jnp.int32, (nh, D_KV), 1) // HEAD_DIM
    gmask = grow == glane
    low = glane == 0
    key_idx = lax.broadcasted_iota(jnp.int32, (nh, WINDOW), 1)
    last = lax.broadcasted_iota(jnp.int32, (D_KV, WINDOW), 1) == WINDOW - 1
    sink = sink_ref[...]
    kvn = kvn_ref[...]
    col = col_ref[0]
    k_new = [kvn[j:j + 1, 0:D_KV] for j in seqs]
    v_new = [kvn[j:j + 1, D_KV:] for j in seqs]
    ck = [ck_ref[j] for j in seqs]
    cv = [cv_ref[j] for j in seqs]
    for j in seqs:
        kout_ref[j] = jnp.where(last, col[0:D_KV, j:j + 1], pltpu.roll(ck[j], WINDOW - 1, 1))
        vout_ref[j] = jnp.where(last, col[D_KV:, j:j + 1], pltpu.roll(cv[j], WINDOW - 1, 1))
    q8 = [qr_ref[j * nh:(j + 1) * nh, :] for j in seqs]
    qp = [jnp.where(gmask, jnp.concatenate([q8[j], q8[j]], axis=1), 0.0) for j in seqs]
    s_c = [jnp.where(key_idx >= 1, jnp.dot(qp[j], ck[j], preferred_element_type=F32) * ATTN_SCALE, NEG_INF)
           for j in seqs]
    s_n = [jnp.sum(qp[j] * k_new[j], axis=-1, keepdims=True) * ATTN_SCALE for j in seqs]
    m = [jnp.maximum(jnp.maximum(jnp.max(s_c[j], axis=-1, keepdims=True), s_n[j]), sink) for j in seqs]
    e_c = [jnp.exp(s_c[j] - m[j]) for j in seqs]
    e_n = [jnp.exp(s_n[j] - m[j]) for j in seqs]
    denom = [jnp.sum(e_c[j], axis=-1, keepdims=True) + e_n[j] + jnp.exp(sink - m[j]) for j in seqs]
    o = [(_dot_nt_f32(e_c[j], cv[j]) + e_n[j] * v_new[j]) / denom[j]
         for j in seqs]
    out_rows = []
    for j in seqs:
        rot = pltpu.roll(o[j], HEAD_DIM, 1)
        g0 = jnp.where(low, o[j], rot)
        g1 = jnp.where(low, rot, o[j])
        wide = jnp.concatenate([g0, g0, g1, g1], axis=1)
        out_rows.append(jnp.sum(jnp.where(dmask, wide, 0.0), axis=0, keepdims=True))
    ya_ref[...] = jnp.concatenate(out_rows, axis=0)


def _decode_attn(q_r, kvn, cols, ck_t, cv_t, sinks_col):
    n = kvn.shape[0]
    bt = DEC_TILE
    const = lambda shape: pl.BlockSpec(shape, lambda i: (0,) * len(shape))
    return pl.pallas_call(
        _decode_attn_kernel,
        grid=(n // bt,),
        in_specs=[
            pl.BlockSpec((bt * N_Q_HEADS, HEAD_DIM), lambda i: (i, 0)),
            pl.BlockSpec((bt, 2 * D_KV), lambda i: (i, 0)),
            pl.BlockSpec((1, 2 * D_KV, bt), lambda i: (i, 0, 0)),
            pl.BlockSpec((bt, D_KV, WINDOW), lambda i: (i, 0, 0)),
            pl.BlockSpec((bt, D_KV, WINDOW), lambda i: (i, 0, 0)),
            const((N_Q_HEADS, 1)),
        ],
        out_specs=[
            pl.BlockSpec((bt, D_ATTN), lambda i: (i, 0)),
            pl.BlockSpec((bt, D_KV, WINDOW), lambda i: (i, 0, 0)),
            pl.BlockSpec((bt, D_KV, WINDOW), lambda i: (i, 0, 0)),
        ],
        out_shape=[
            jax.ShapeDtypeStruct((n, D_ATTN), F32),
            jax.ShapeDtypeStruct((n, D_KV, WINDOW), F32),
            jax.ShapeDtypeStruct((n, D_KV, WINDOW), F32),
        ],
        compiler_params=pltpu.CompilerParams(
            dimension_semantics=("arbitrary",), vmem_limit_bytes=VMEM_LIMIT),
        name="decode_attn",
    )(q_r, kvn, cols, ck_t, cv_t, sinks_col)


def _rope_table(pos):
    inv_freq = jnp.power(ROPE_THETA, -jnp.arange(ROPE_HALF, dtype=F32) * (2.0 / ROPE_DIM))
    dim = np.arange(2 * HEAD_DIM) % HEAD_DIM
    ang = pos[:, None] * inv_freq[dim % ROPE_HALF][None, :]
    cos, sin = jnp.cos(ang), jnp.sin(ang)
    cos_t = jnp.where((dim < ROPE_DIM)[None, :], cos, 1.0)
    sin_lo = jnp.where((dim < ROPE_HALF)[None, :], -sin, 0.0)
    sin_hi = jnp.where(((dim >= ROPE_HALF) & (dim < ROPE_DIM))[None, :], sin, 0.0)
    return jnp.concatenate([cos_t, sin_lo, sin_hi], axis=1)


def _pad_cols(w, at, n):
    return jnp.concatenate([w[..., :at], jnp.zeros(w.shape[:-1] + (n,), w.dtype), w[..., at:]], axis=-1)


def kernel(x_prompt, x_sample, state_wkv, state_shift, cache_k_win, cache_v_win, norm_mix_w, w_in, mu_shift, w0, w_decay_up, a0, w_a_up, w_g_up, k_k, k_a, r_k, ln_x_w, ln_x_b, q_norm_w, k_norm_w, sinks, w_out, norm_ffn_w, w_ffn_up, w_ffn_down):
    bsz, t, _ = x_prompt.shape
    nd = x_sample.shape[0]
    l = 0
    pad = D_LORA_PAD - D_LORA

    w_in_pad = _pad_cols(w_in[l], D_SHIFT, pad).astype(BF16)
    mu_pad = _pad_cols(mu_shift[l][None, :], D_SHIFT, pad)
    wl = jnp.zeros((D_LORA_PAD, 3 * D_RWKV), F32)
    wl = wl.at[0:32, 0:D_RWKV].set(w_decay_up[l])
    wl = wl.at[32:64, D_RWKV:2 * D_RWKV].set(w_a_up[l])
    wl = wl.at[64:160, 2 * D_RWKV:].set(w_g_up[l])
    wl = wl.astype(BF16)
    prm = jnp.zeros((16, D_RWKV), F32)
    prm = prm.at[0].set(w0[l]).at[1].set(a0[l]).at[2].set(k_k[l]).at[3].set(k_a[l])
    prm = prm.at[4].set(r_k[l].reshape(-1)).at[5].set(ln_x_w[l]).at[6].set(ln_x_b[l])
    hid = np.arange(QUAD) // HEAD_DIM
    bmask = jnp.asarray(hid[:, None] == hid[None, :], BF16)
    tri = jnp.asarray(np.tril(np.ones((CHUNK, CHUNK))), BF16)
    qw = jnp.tile(q_norm_w[l][None, :], (1, N_Q_HEADS))
    kw = jnp.tile(k_norm_w[l][None, :], (1, N_KV_HEADS))
    nmw = norm_mix_w[l][None, :]
    nfw = norm_ffn_w[l][None, :]
    wo = w_out[l].astype(BF16)
    wu = w_ffn_up[l].astype(BF16)
    wd = w_ffn_down[l].astype(BF16)
    tab_p = _rope_table(jnp.arange(t, dtype=F32) + 0)
    tab_s = jnp.tile(_rope_table(jnp.arange(1, dtype=F32) + PAST_LEN), (nd, 1))

    xp = x_prompt.reshape(bsz * t, D_MODEL)
    p_p, q_p, kv_p = _inproj(xp, nmw, w_in_pad, 512)
    yr_p, hbd = _rwkv_prompt(p_p.reshape(bsz, t, D_SHIFT_PAD), mu_pad, prm, wl, tri, bmask)
    ya_p, kwin_p, vwin_p = _attn_prompt(q_p.reshape(bsz, t, D_ATTN), kv_p.reshape(bsz, t, 2 * D_KV), tab_p,
                                        qw, kw, sinks[l][None, :], bmask, _band_bias())
    y_p = _outffn(xp, yr_p.reshape(bsz * t, D_RWKV), ya_p.reshape(bsz * t, D_ATTN), wo, nfw, wu, wd, 256)
    y_prompt = y_p.reshape(bsz, t, D_MODEL)
    hb = hbd.reshape(bsz, 2, 4, HEAD_DIM, 4, HEAD_DIM)
    wkv_prompt = jnp.stack([hb[:, :, j, :, j, :] for j in range(4)], axis=2)
    wkv_prompt = wkv_prompt.reshape(bsz, H_RWKV, HEAD_DIM, HEAD_DIM)[None]
    shift_prompt = p_p.reshape(bsz, t, D_SHIFT_PAD)[:, t - 1:, :D_SHIFT][None]
    k_win_prompt = kwin_p.reshape(bsz, WINDOW, N_KV_HEADS, HEAD_DIM)[None]
    v_win_prompt = vwin_p.reshape(bsz, WINDOW, N_KV_HEADS, HEAD_DIM)[None]

    xs = x_sample.reshape(nd, D_MODEL)
    p_s, q_s, kv_s = _inproj(xs, nmw, w_in_pad, 128)
    shift_in = _pad_cols(state_shift[l].reshape(nd, D_SHIFT), D_SHIFT, pad)
    vec_t, gb, qn_s, kvn_s = _decode_prep(p_s, shift_in, q_s, kv_s, mu_pad, prm, wl, bmask, tab_s, qw, kw)
    s_t = jnp.transpose(state_wkv[l], (1, 2, 3, 0))
    ck_t = jnp.swapaxes(cache_k_win[l].reshape(nd, WINDOW, D_KV), 1, 2)
    cv_t = jnp.swapaxes(cache_v_win[l].reshape(nd, WINDOW, D_KV), 1, 2)
    q_r = qn_s.reshape(nd * N_Q_HEADS, HEAD_DIM)
    cols = jnp.swapaxes(kvn_s.reshape(nd // DEC_TILE, DEC_TILE, 2 * D_KV), 1, 2)
    wkv_t, yr_s = _decode_state(vec_t, gb, prm, bmask, s_t)
    ya_s, kc_t, vc_t = _decode_attn(q_r, kvn_s, cols, ck_t, cv_t, sinks[l][:, None])
    y_s = _outffn(xs, yr_s, ya_s, wo, nfw, wu, wd, 128)
    y_sample = y_s.reshape(nd, 1, D_MODEL)
    wkv_sample = jnp.transpose(wkv_t, (3, 0, 1, 2))[None]
    shift_sample = p_s[:, :D_SHIFT].reshape(nd, 1, D_SHIFT)[None]
    k_win_sample = jnp.swapaxes(kc_t, 1, 2).reshape(nd, WINDOW, N_KV_HEADS, HEAD_DIM)[None]
    v_win_sample = jnp.swapaxes(vc_t, 1, 2).reshape(nd, WINDOW, N_KV_HEADS, HEAD_DIM)[None]

    return (y_prompt, y_sample, wkv_prompt, shift_prompt, k_win_prompt, v_win_prompt,
            wkv_sample, shift_sample, k_win_sample, v_win_sample)
```

```python
import functools

import jax
import jax.numpy as jnp
import numpy as np
from jax import lax
from jax.experimental import pallas as pl
from jax.experimental.pallas import tpu as pltpu

F32 = jnp.float32
BF16 = jnp.bfloat16

D_MODEL = 1024
D_RWKV = 512
D_ATTN = 512
HEAD_DIM = 64
H_RWKV = 8
N_Q_HEADS = 8
N_KV_HEADS = 2
GQA_GROUP = 4
D_KV = 128
D_LORA = 160
D_LORA_PAD = 256
D_SHIFT = 3 * D_RWKV + D_LORA
D_SHIFT_PAD = 3 * D_RWKV + D_LORA_PAD
D_IN_PAD = D_SHIFT_PAD + D_ATTN + 2 * D_KV
WINDOW = 128
ROPE_DIM = 16
ROPE_HALF = 8
ROPE_THETA = 500000.0
ATTN_SCALE = HEAD_DIM ** -0.5
D_FF = 4096
RMS_EPS = 1e-6
LNX_EPS = 64e-5
NEG_INF = -1e30
LOG2E = 1.4426950408889634
PAST_LEN = 16384

CHUNK = 64
QUAD = 4 * HEAD_DIM
VMEM_LIMIT = 56 * 1024 * 1024


def _split2(x):
    hi = x.astype(BF16)
    lo = (x - hi.astype(F32)).astype(BF16)
    return hi, lo


def _head_sums(xs, ones_ref):
    n, w = xs[0].shape
    tile = min(w, QUAD)
    per = w // tile
    pieces = [x[:, j * tile:(j + 1) * tile] for x in xs for j in range(per)]
    stacked = jnp.concatenate(pieces, axis=0) if len(pieces) > 1 else pieces[0]
    ones = ones_ref[0:tile, 0:tile]
    hi, lo = _split2(stacked)
    out = jnp.dot(hi, ones, preferred_element_type=F32) + jnp.dot(lo, ones, preferred_element_type=F32)
    res = []
    for i in range(len(xs)):
        cols = [out[(i * per + j) * n:(i * per + j + 1) * n] for j in range(per)]
        res.append(jnp.concatenate(cols, axis=1) if per > 1 else cols[0])
    return res


def _cumsum_rows(tri_bf16, x):
    hi, lo = _split2(x)
    return (jnp.dot(tri_bf16, hi, preferred_element_type=F32)
            + jnp.dot(tri_bf16, lo, preferred_element_type=F32))


def _mm(a, b):
    return jnp.dot(a.astype(BF16), b.astype(BF16), preferred_element_type=F32)


def _mm_nt(a, b):
    return lax.dot_general(a.astype(BF16), b.astype(BF16), (((1,), (1,)), ((), ())),
                           preferred_element_type=F32)


def _mm_tn(a, b):
    return lax.dot_general(a.astype(BF16), b.astype(BF16), (((0,), (0,)), ((), ())),
                           preferred_element_type=F32)


def _dot_nt_f32(a, b):
    return lax.dot_general(a, b, (((1,), (1,)), ((), ())), preferred_element_type=F32)


def _dot_tn_f32(a, b):
    return lax.dot_general(a, b, (((0,), (0,)), ((), ())), preferred_element_type=F32)


def _sigmoid(x):
    return 1.0 / (1.0 + jnp.exp(-x))


def _softplus(x):
    return jnp.maximum(x, 0.0) + jnp.log1p(jnp.exp(-jnp.abs(x)))


def _inproj_kernel(x_ref, nw_ref, w_ref, p_ref, q_ref, kv_ref):
    x = x_ref[...]
    ms = jnp.mean(x * x, axis=-1, keepdims=True)
    h = (x * lax.rsqrt(ms + RMS_EPS)) * nw_ref[...]
    out = jnp.dot(h.astype(BF16), w_ref[...], preferred_element_type=F32)
    p_ref[...] = out[:, :D_SHIFT_PAD]
    q_ref[...] = out[:, D_SHIFT_PAD:D_SHIFT_PAD + D_ATTN]
    kv_ref[...] = out[:, D_SHIFT_PAD + D_ATTN:]


def _inproj_shift_kernel(tiles_per_seq, x_ref, nw_ref, w_ref, mu_ref, xs_ref, last_ref, q_ref, kv_ref, prev_ref):
    i = pl.program_id(0)

    @pl.when(i % tiles_per_seq == 0)
    def _():
        prev_ref[...] = jnp.zeros_like(prev_ref)

    x = x_ref[...]
    ms = jnp.mean(x * x, axis=-1, keepdims=True)
    h = (x * lax.rsqrt(ms + RMS_EPS)) * nw_ref[...]
    out = jnp.dot(h.astype(BF16), w_ref[...], preferred_element_type=F32)
    p = out[:, :D_SHIFT_PAD]
    tm = p.shape[0]
    row = lax.broadcasted_iota(jnp.int32, p.shape, 0)
    prev = jnp.where(row == 0, jnp.broadcast_to(prev_ref[0:1, :], p.shape), pltpu.roll(p, 1, 0))
    xs_ref[...] = p + (prev - p) * mu_ref[...]
    last = jnp.broadcast_to(p[tm - 1:tm, :], prev_ref.shape)
    prev_ref[...] = last
    last_ref[0] = last
    q_ref[...] = out[:, D_SHIFT_PAD:D_SHIFT_PAD + D_ATTN]
    kv_ref[...] = out[:, D_SHIFT_PAD + D_ATTN:]


def _inproj_shift(x2d, norm_w, w_in_pad, mu_pad, seq_len, tm):
    m = x2d.shape[0]
    tiles_per_seq = seq_len // tm
    return pl.pallas_call(
        functools.partial(_inproj_shift_kernel, tiles_per_seq),
        grid=(m // tm,),
        in_specs=[
            pl.BlockSpec((tm, D_MODEL), lambda i: (i, 0)),
            pl.BlockSpec((1, D_MODEL), lambda i: (0, 0)),
            pl.BlockSpec((D_MODEL, D_IN_PAD), lambda i: (0, 0)),
            pl.BlockSpec((1, D_SHIFT_PAD), lambda i: (0, 0)),
        ],
        out_specs=[
            pl.BlockSpec((tm, D_SHIFT_PAD), lambda i: (i, 0)),
            pl.BlockSpec((1, 8, D_SHIFT_PAD), lambda i: (i // tiles_per_seq, 0, 0)),
            pl.BlockSpec((tm, D_ATTN), lambda i: (i, 0)),
            pl.BlockSpec((tm, 2 * D_KV), lambda i: (i, 0)),
        ],
        out_shape=[
            jax.ShapeDtypeStruct((m, D_SHIFT_PAD), F32),
            jax.ShapeDtypeStruct((m // seq_len, 8, D_SHIFT_PAD), F32),
            jax.ShapeDtypeStruct((m, D_ATTN), F32),
            jax.ShapeDtypeStruct((m, 2 * D_KV), F32),
        ],
        scratch_shapes=[pltpu.VMEM((8, D_SHIFT_PAD), F32)],
        compiler_params=pltpu.CompilerParams(
            dimension_semantics=("arbitrary",), vmem_limit_bytes=VMEM_LIMIT),
        name="inproj_shift",
    )(x2d, norm_w, w_in_pad, mu_pad)


def _inproj(x2d, norm_w, w_in_pad, tm):
    m = x2d.shape[0]
    return pl.pallas_call(
        _inproj_kernel,
        grid=(m // tm,),
        in_specs=[
            pl.BlockSpec((tm, D_MODEL), lambda i: (i, 0)),
            pl.BlockSpec((1, D_MODEL), lambda i: (0, 0)),
            pl.BlockSpec((D_MODEL, D_IN_PAD), lambda i: (0, 0)),
        ],
        out_specs=[
            pl.BlockSpec((tm, D_SHIFT_PAD), lambda i: (i, 0)),
            pl.BlockSpec((tm, D_ATTN), lambda i: (i, 0)),
            pl.BlockSpec((tm, 2 * D_KV), lambda i: (i, 0)),
        ],
        out_shape=[
            jax.ShapeDtypeStruct((m, D_SHIFT_PAD), F32),
            jax.ShapeDtypeStruct((m, D_ATTN), F32),
            jax.ShapeDtypeStruct((m, 2 * D_KV), F32),
        ],
        compiler_params=pltpu.CompilerParams(
            dimension_semantics=("arbitrary",), vmem_limit_bytes=VMEM_LIMIT),
        name="inproj",
    )(x2d, norm_w, w_in_pad)


def _rwkv_features(xs, prm, wl_ref, ones_ref):
    r = xs[:, 0:D_RWKV]
    k = xs[:, D_RWKV:2 * D_RWKV]
    v = xs[:, 2 * D_RWKV:3 * D_RWKV]
    lora = xs[:, 3 * D_RWKV:]
    col = lax.broadcasted_iota(jnp.int32, lora.shape, 1)
    act = jnp.where(col < 32, jnp.tanh(lora), jnp.where(col < 64, lora, _sigmoid(lora)))
    up = jnp.dot(act.astype(BF16), wl_ref[...], preferred_element_type=F32)
    w0, a0, k_k, k_a, r_k = prm[0:1], prm[1:2], prm[2:3], prm[3:4], prm[4:5]
    w_log = -_softplus(-(w0 + up[:, 0:D_RWKV])) - 0.5
    logw = -jnp.exp(w_log)
    asig = _sigmoid(a0 + up[:, D_RWKV:2 * D_RWKV])
    g = up[:, 2 * D_RWKV:]
    kk = k * k_k
    k_mod = k * (1.0 + (asig - 1.0) * k_a)
    ss, rk = _head_sums([kk * kk, r * k_mod * r_k], ones_ref)
    kk = kk / jnp.maximum(jnp.sqrt(ss), 1e-12)
    k = k_mod
    bonus = rk * v
    return r, logw, k, v, -kk, kk * asig, g, bonus


def _rwkv_finish(y, g, bonus, prm, ones_ref):
    ln_w, ln_b = prm[5:6], prm[6:7]
    mean = _head_sums([y], ones_ref)[0] * (1.0 / HEAD_DIM)
    d = y - mean
    var = _head_sums([d * d], ones_ref)[0] * (1.0 / HEAD_DIM)
    yn = d * lax.rsqrt(var + LNX_EPS) * ln_w + ln_b
    return (yn + bonus) * g


def _block_diag(x, bmask):
    return jnp.concatenate([x] * 4, axis=0) * bmask


def _chunk_prep(r, logw, k, v, a, b, tri_ref, bmask):
    n = len(r)
    ch = range(n)
    tri = tri_ref[...]
    cum = [_cumsum_rows(tri, logw[i]) for i in ch]
    e_in = [jnp.exp(cum[i]) for i in ch]
    e_ex = [jnp.exp(cum[i] - logw[i]) for i in ch]
    e_inv = [1.0 / e_in[i] for i in ch]
    e_last = [e_in[i][CHUNK - 1:CHUNK, :] for i in ch]
    rt = [(r[i] * e_in[i]).astype(BF16) for i in ch]
    at = [(a[i] * e_ex[i]).astype(BF16) for i in ch]
    kt = [(k[i] * e_inv[i]).astype(BF16) for i in ch]
    bt = [(b[i] * e_inv[i]).astype(BF16) for i in ch]
    vb = [v[i].astype(BF16) for i in ch]

    t_idx = lax.broadcasted_iota(jnp.int32, (CHUNK, QUAD), 0)
    s_idx = lax.broadcasted_iota(jnp.int32, (CHUNK, QUAD), 1) & (HEAD_DIM - 1)
    strict = s_idx < t_idx
    incl = s_idx <= t_idx

    gm = [_mm_nt(jnp.concatenate([at[i], rt[i]], axis=0),
                 jnp.concatenate([_block_diag(bt[i], bmask), _block_diag(kt[i], bmask)], axis=0))
          for i in ch]
    a_ab = [jnp.where(strict, gm[i][:CHUNK, :QUAD], 0.0) for i in ch]
    a_ak = [jnp.where(strict, gm[i][:CHUNK, QUAD:], 0.0) for i in ch]
    a_rb = [jnp.where(incl, gm[i][CHUNK:, :QUAD], 0.0) for i in ch]
    a_rk = [jnp.where(incl, gm[i][CHUNK:, QUAD:], 0.0) for i in ch]

    eye = jnp.where(s_idx == t_idx, 1.0, 0.0)
    pwb = [a_ab[i].astype(BF16) for i in ch]
    t_inv = [eye + a_ab[i] for i in ch]
    for it in range(6):
        rbd = [_block_diag(pwb[i], bmask) for i in ch]
        if it == 0:
            pwb = [_mm(pwb[i], rbd[i]).astype(BF16) for i in ch]
        elif it < 5:
            out = [_mm(jnp.concatenate([pwb[i], t_inv[i].astype(BF16)], axis=0), rbd[i]) for i in ch]
            pwb = [out[i][:CHUNK].astype(BF16) for i in ch]
            t_inv = [t_inv[i] + out[i][CHUNK:] for i in ch]
        else:
            t_inv = [t_inv[i] + _mm(t_inv[i], rbd[i]) for i in ch]

    vbd = [_block_diag(vb[i], bmask) for i in ch]
    xy0 = [_mm(jnp.concatenate([a_ak[i], a_rk[i]], axis=0), vbd[i]) for i in ch]
    return [dict(ar=jnp.concatenate([at[i], rt[i]], axis=0), x0=xy0[i][:CHUNK], y0=xy0[i][CHUNK:],
                 t_inv=t_inv[i].astype(BF16), a_rb=a_rb[i].astype(BF16), vb=vb[i],
                 bk=jnp.concatenate([bt[i], kt[i]], axis=0), e_last=e_last[i]) for i in ch]


def _chunk_step(pre, state, bmask):
    ch = range(len(pre))
    half = QUAD // 2
    zeros = jnp.zeros((half, half), BF16)
    sc = [state[i].astype(BF16) for i in ch]
    sb = [jnp.concatenate([jnp.concatenate([sc[i][:half], zeros], axis=1),
                           jnp.concatenate([zeros, sc[i][half:]], axis=1)], axis=0) for i in ch]
    xr = [_mm_nt(pre[i]["ar"], sb[i]) for i in ch]
    x = [xr[i][:CHUNK] + pre[i]["x0"] for i in ch]
    u = [_mm(pre[i]["t_inv"], _block_diag(x[i].astype(BF16), bmask)) for i in ch]
    ub = [u[i].astype(BF16) for i in ch]
    y = [xr[i][CHUNK:] + pre[i]["y0"] + _mm(pre[i]["a_rb"], _block_diag(ub[i], bmask)) for i in ch]
    upd = [_mm_tn(jnp.concatenate([ub[i], pre[i]["vb"]], axis=0), pre[i]["bk"]) for i in ch]
    bm = bmask[:half, :half].astype(F32)
    s_new = []
    for i in ch:
        e_last = pre[i]["e_last"]
        top = (state[i][:half] + upd[i][:half, :half] * bm) * e_last[:, :half]
        bot = (state[i][half:] + upd[i][half:, half:] * bm) * e_last[:, half:]
        s_new.append(jnp.concatenate([top, bot], axis=0))
    return y, s_new


def _rwkv_prompt_kernel(xs_ref, prm_ref, wl_ref, tri_ref, bmask_ref, y_ref, hout_ref, h_ref):
    c = pl.program_id(0)
    nseq, tstep, _ = xs_ref.shape
    nchunk = tstep // CHUNK

    @pl.when(c == 0)
    def _():
        h_ref[...] = jnp.zeros_like(h_ref)

    xs = jnp.concatenate([xs_ref[s] for s in range(nseq)], axis=0)
    prm = prm_ref[...]
    r, logw, k, v, a, b, g, bonus = _rwkv_features(xs, prm, wl_ref, bmask_ref)
    bmask = bmask_ref[...]
    lanes = [(s, q) for s in range(nseq) for q in range(2)]
    chains = [(j, s, q) for j in range(nchunk) for s, q in lanes]
    cut = lambda x: [x[s * tstep + j * CHUNK:s * tstep + (j + 1) * CHUNK, q * QUAD:(q + 1) * QUAD]
                     for j, s, q in chains]
    pre = _chunk_prep(cut(r), cut(logw), cut(k), cut(v), cut(a), cut(b), tri_ref, bmask)
    state = [h_ref[s, q] for s, q in lanes]
    ys = []
    for j in range(nchunk):
        y_j, state = _chunk_step(pre[j * len(lanes):(j + 1) * len(lanes)], state, bmask)
        ys.append(y_j)
    for i, (s, q) in enumerate(lanes):
        h_ref[s, q] = state[i]
    rows = [jnp.concatenate(ys[j][2 * s:2 * s + 2], axis=1) for s in range(nseq) for j in range(nchunk)]
    y = _rwkv_finish(jnp.concatenate(rows, axis=0), g, bonus, prm, bmask_ref)
    for s in range(nseq):
        y_ref[s] = y[s * tstep:(s + 1) * tstep]

    @pl.when(c == pl.num_programs(0) - 1)
    def _():
        hout_ref[...] = h_ref[...]


RWKV_CHUNKS_PER_STEP = 4


def _rwkv_prompt(xs3d, prm, wl, tri, bmask):
    bsz, t, _ = xs3d.shape
    tstep = RWKV_CHUNKS_PER_STEP * CHUNK
    nc = t // tstep
    const = lambda shape: pl.BlockSpec(shape, lambda c: (0,) * len(shape))
    state_shape = (bsz, 2, QUAD, QUAD // 2)
    return pl.pallas_call(
        _rwkv_prompt_kernel,
        grid=(nc,),
        in_specs=[
            pl.BlockSpec((bsz, tstep, D_SHIFT_PAD), lambda c: (0, c, 0)),
            const((16, D_RWKV)),
            const((D_LORA_PAD, 3 * D_RWKV)),
            const((CHUNK, CHUNK)),
            const((QUAD, QUAD)),
        ],
        out_specs=[
            pl.BlockSpec((bsz, tstep, D_RWKV), lambda c: (0, c, 0)),
            const(state_shape),
        ],
        out_shape=[
            jax.ShapeDtypeStruct((bsz, t, D_RWKV), F32),
            jax.ShapeDtypeStruct(state_shape, F32),
        ],
        scratch_shapes=[pltpu.VMEM(state_shape, F32)],
        compiler_params=pltpu.CompilerParams(
            dimension_semantics=("arbitrary",), vmem_limit_bytes=VMEM_LIMIT),
        name="rwkv_prompt",
    )(xs3d, prm, wl, tri, bmask)


def _qk_norm_rope(x, norm_w, cos_t, sin_lo, sin_hi, ones_ref):
    ms = _head_sums([x * x], ones_ref)[0] * (1.0 / HEAD_DIM)
    xn = x * lax.rsqrt(ms + RMS_EPS) * norm_w
    width = x.shape[1]
    fwd = pltpu.roll(xn, width - ROPE_HALF, 1)
    bwd = pltpu.roll(xn, ROPE_HALF, 1)
    return xn * cos_t + fwd * sin_lo + bwd * sin_hi


def _tile_lanes(x, reps):
    return jnp.concatenate([x] * reps, axis=1) if reps > 1 else x


def _attn_prompt_kernel(q_ref, kv_ref, tab_ref, qw_ref, kw_ref, sink_ref, ones_ref, bias_ref,
                        o_ref, kwin_ref, vwin_ref, kprev_ref, vprev_ref):
    i = pl.program_id(0)
    nseq = q_ref.shape[0]
    seqs = range(nseq)

    @pl.when(i == 0)
    def _():
        kprev_ref[...] = jnp.zeros_like(kprev_ref)
        vprev_ref[...] = jnp.zeros_like(vprev_ref)

    tab = tab_ref[...]
    cos_t, sin_lo, sin_hi = tab[:, 0:128], tab[:, 128:256], tab[:, 256:384]
    cos4, slo4, shi4 = _tile_lanes(cos_t, 4), _tile_lanes(sin_lo, 4), _tile_lanes(sin_hi, 4)
    q = [_qk_norm_rope(q_ref[s], qw_ref[...], cos4, slo4, shi4, ones_ref) * (ATTN_SCALE * LOG2E) for s in seqs]
    kv = [kv_ref[s] for s in seqs]
    k_cur = [_qk_norm_rope(kv[s][:, 0:D_KV], kw_ref[...], cos_t, sin_lo, sin_hi, ones_ref) for s in seqs]
    v_cur = [kv[s][:, D_KV:] for s in seqs]
    k_all = [jnp.concatenate([kprev_ref[s], k_cur[s]], axis=0) for s in seqs]
    v_all = [jnp.concatenate([vprev_ref[s], v_cur[s]], axis=0) for s in seqs]
    for s in seqs:
        kprev_ref[s] = k_cur[s]
        vprev_ref[s] = v_cur[s]

    @pl.when(i == pl.num_programs(0) - 1)
    def _():
        for s in seqs:
            kwin_ref[s] = k_cur[s]
            vwin_ref[s] = v_cur[s]

    nk = 2 * WINDOW
    bias = bias_ref[0]
    sinks = sink_ref[...] * LOG2E
    low = lax.broadcasted_iota(jnp.int32, (nk, D_KV), 1) < HEAD_DIM
    lane_blk = [ones_ref[j * HEAD_DIM:j * HEAD_DIM + 1, :] for j in range(GQA_GROUP)]

    chains = [(s, g) for s in seqs for g in range(N_KV_HEADS)]
    ch = range(len(chains))
    k_rot = [pltpu.roll(k_all[s], HEAD_DIM, 1) for s in seqs]
    k2 = [jnp.where(low, k_all[s], k_rot[s]) if g == 0 else jnp.where(low, k_rot[s], k_all[s])
          for s, g in chains]
    k4 = [jnp.concatenate([k2[c], k2[c]], axis=1).astype(BF16) for c in ch]
    vb = [v_all[s].astype(BF16) for s in seqs]
    qg = [q[s][:, g * QUAD:(g + 1) * QUAD].astype(BF16) for s, g in chains]
    qstack = [jnp.concatenate([qg[c] * lane_blk[j] for j in range(GQA_GROUP)], axis=0) for c in ch]
    sink_row = [jnp.concatenate(
        [jnp.broadcast_to(sinks[:, g * GQA_GROUP + j:g * GQA_GROUP + j + 1], (1, WINDOW))
         for j in range(GQA_GROUP)], axis=1) for s, g in chains]
    sc = [_mm_nt(k4[c], qstack[c]) + bias for c in ch]
    m = [jnp.maximum(jnp.max(sc[c], axis=0, keepdims=True), sink_row[c]) for c in ch]
    e = [jnp.exp2(sc[c] - m[c]) for c in ch]
    denom = [jnp.sum(e[c], axis=0, keepdims=True) + jnp.exp2(sink_row[c] - m[c]) for c in ch]
    ot = [_mm_tn(vb[s], e[c].astype(BF16))[g * HEAD_DIM:(g + 1) * HEAD_DIM, :] * (1.0 / denom[c])
          for c, (s, g) in enumerate(chains)]
    for s in seqs:
        yt = jnp.concatenate([ot[s * N_KV_HEADS + g][:, j * WINDOW:(j + 1) * WINDOW]
                              for g in range(N_KV_HEADS) for j in range(GQA_GROUP)], axis=0)
        o_ref[s] = jnp.transpose(yt)


def _band_bias():
    ki = np.arange(2 * WINDOW)[:, None]
    qi = (np.arange(GQA_GROUP * WINDOW) % WINDOW + WINDOW)[None, :]
    dq = qi - ki
    band = (dq >= 0) & (dq < WINDOW)
    first = band & (ki >= WINDOW)
    return jnp.asarray(np.where(np.stack([first, band]), 0.0, NEG_INF), F32)


def _attn_prompt(q3d, kv3d, tab, qw, kw, sinks, ones_bd, bias):
    bsz, t, _ = q3d.shape
    nb = t // WINDOW
    const = lambda shape: pl.BlockSpec(shape, lambda i: (0,) * len(shape))
    return pl.pallas_call(
        _attn_prompt_kernel,
        grid=(nb,),
        in_specs=[
            pl.BlockSpec((bsz, WINDOW, D_ATTN), lambda i: (0, i, 0)),
            pl.BlockSpec((bsz, WINDOW, 2 * D_KV), lambda i: (0, i, 0)),
            pl.BlockSpec((WINDOW, 384), lambda i: (i, 0)),
            const((1, D_ATTN)),
            const((1, D_KV)),
            const((1, N_Q_HEADS)),
            const((QUAD, QUAD)),
            pl.BlockSpec((1, 2 * WINDOW, GQA_GROUP * WINDOW), lambda i: (jnp.minimum(i, 1), 0, 0)),
        ],
        out_specs=[
            pl.BlockSpec((bsz, WINDOW, D_ATTN), lambda i: (0, i, 0)),
            const((bsz, WINDOW, D_KV)),
            const((bsz, WINDOW, D_KV)),
        ],
        out_shape=[
            jax.ShapeDtypeStruct((bsz, t, D_ATTN), F32),
            jax.ShapeDtypeStruct((bsz, WINDOW, D_KV), F32),
            jax.ShapeDtypeStruct((bsz, WINDOW, D_KV), F32),
        ],
        scratch_shapes=[
            pltpu.VMEM((bsz, WINDOW, D_KV), F32),
            pltpu.VMEM((bsz, WINDOW, D_KV), F32),
        ],
        compiler_params=pltpu.CompilerParams(
            dimension_semantics=("arbitrary",), vmem_limit_bytes=VMEM_LIMIT),
        name="attn_prompt",
    )(q3d, kv3d, tab, qw, kw, sinks, ones_bd, bias)


def _outffn_kernel(x_ref, yr_ref, ya_ref, wo_ref, nw_ref, wu_ref, wd_ref, o_ref):
    mix = jnp.concatenate([yr_ref[...], ya_ref[...]], axis=1).astype(BF16)
    x1 = x_ref[...] + jnp.dot(mix, wo_ref[...], preferred_element_type=F32)
    ms = jnp.mean(x1 * x1, axis=-1, keepdims=True)
    hf = ((x1 * lax.rsqrt(ms + RMS_EPS)) * nw_ref[...]).astype(BF16)
    up = jnp.dot(hf, wu_ref[...], preferred_element_type=F32)
    act = jnp.square(jnp.maximum(up, 0.0)).astype(BF16)
    o_ref[...] = x1 + jnp.dot(act, wd_ref[...], preferred_element_type=F32)


def _outffn(x2d, yr, ya, wo, nw, wu, wd, tm):
    m = x2d.shape[0]
    const = lambda shape: pl.BlockSpec(shape, lambda i: (0,) * len(shape))
    return pl.pallas_call(
        _outffn_kernel,
        grid=(m // tm,),
        in_specs=[
            pl.BlockSpec((tm, D_MODEL), lambda i: (i, 0)),
            pl.BlockSpec((tm, D_RWKV), lambda i: (i, 0)),
            pl.BlockSpec((tm, D_ATTN), lambda i: (i, 0)),
            const((D_MODEL, D_MODEL)),
            const((1, D_MODEL)),
            const((D_MODEL, D_FF)),
            const((D_FF, D_MODEL)),
        ],
        out_specs=pl.BlockSpec((tm, D_MODEL), lambda i: (i, 0)),
        out_shape=jax.ShapeDtypeStruct((m, D_MODEL), F32),
        compiler_params=pltpu.CompilerParams(
            dimension_semantics=("arbitrary",), vmem_limit_bytes=VMEM_LIMIT),
        name="outffn",
    )(x2d, yr, ya, wo, nw, wu, wd)


DEC_TILE = 8


def _decode_prep_kernel(p_ref, sh_ref, q_ref, kv_ref, mu_ref, prm_ref, wl_ref, ones_ref, tab_ref,
                        qw_ref, kw_ref, vec_ref, vgb_ref, qn_ref, kvn_ref):
    p = p_ref[...]
    xs = p + (sh_ref[...] - p) * mu_ref[...]
    r, logw, k, v, a, b, g, bonus = _rwkv_features(xs, prm_ref[...], wl_ref, ones_ref)
    for i, x in enumerate((a, b, k, jnp.exp(logw), r, v)):
        vec_ref[i] = jnp.transpose(x)
    vgb_ref[0] = g
    vgb_ref[1] = bonus
    tab = tab_ref[...]
    cos_t, sin_lo, sin_hi = tab[:, 0:128], tab[:, 128:256], tab[:, 256:384]
    qn_ref[...] = _qk_norm_rope(q_ref[...], qw_ref[...], _tile_lanes(cos_t, 4), _tile_lanes(sin_lo, 4),
                                _tile_lanes(sin_hi, 4), ones_ref)
    kv = kv_ref[...]
    kvn_ref[:, 0:D_KV] = _qk_norm_rope(kv[:, 0:D_KV], kw_ref[...], cos_t, sin_lo, sin_hi, ones_ref)
    kvn_ref[:, D_KV:] = kv[:, D_KV:]


def _decode_prep(p, shift, q, kv, mu_pad, prm, wl, bmask, tab, qw, kw):
    n = p.shape[0]
    full = lambda shape: pl.BlockSpec(shape, lambda i: (0,) * len(shape))
    return pl.pallas_call(
        _decode_prep_kernel,
        grid=(1,),
        in_specs=[full((n, D_SHIFT_PAD)), full((n, D_SHIFT_PAD)), full((n, D_ATTN)), full((n, 2 * D_KV)),
                  full((1, D_SHIFT_PAD)), full((16, D_RWKV)), full((D_LORA_PAD, 3 * D_RWKV)),
                  full((QUAD, QUAD)), full((n, 384)), full((1, D_ATTN)), full((1, D_KV))],
        out_specs=[full((6, D_RWKV, n)), full((2, n, D_RWKV)), full((n, D_ATTN)), full((n, 2 * D_KV))],
        out_shape=[
            jax.ShapeDtypeStruct((6, D_RWKV, n), F32),
            jax.ShapeDtypeStruct((2, n, D_RWKV), F32),
            jax.ShapeDtypeStruct((n, D_ATTN), F32),
            jax.ShapeDtypeStruct((n, 2 * D_KV), F32),
        ],
        compiler_params=pltpu.CompilerParams(
            dimension_semantics=("arbitrary",), vmem_limit_bytes=VMEM_LIMIT),
        name="decode_prep",
    )(p, shift, q, kv, mu_pad, prm, wl, bmask, tab, qw, kw)


def _decode_state_kernel(vec_ref, gb_ref, prm_ref, ones_ref, s_ref, sout_ref, yr_ref, yt_ref):
    h = pl.program_id(0)
    a_t, b_t, k_t, w_t, r_t = (vec_ref[i] for i in range(5))

    def body(i, carry):
        s = s_ref[0, i]
        sa = jnp.sum(s * a_t, axis=0, keepdims=True)
        v_i = vec_ref[5, pl.ds(i, 1), :]
        s_new = s * w_t + sa * b_t + v_i * k_t
        sout_ref[0, i] = s_new
        yt_ref[pl.ds(h * HEAD_DIM + i, 1), :] = jnp.sum(s_new * r_t, axis=0, keepdims=True)
        return carry

    lax.fori_loop(0, HEAD_DIM, body, 0, unroll=8)

    @pl.when(h == pl.num_programs(0) - 1)
    def _():
        y = jnp.transpose(yt_ref[...])
        yr_ref[...] = _rwkv_finish(y, gb_ref[0], gb_ref[1], prm_ref[...], ones_ref)


def _decode_state(vec_t, gb, prm, bmask, s_t):
    n = s_t.shape[-1]
    const = lambda shape: pl.BlockSpec(shape, lambda h: (0,) * len(shape))
    return pl.pallas_call(
        _decode_state_kernel,
        grid=(H_RWKV,),
        in_specs=[
            pl.BlockSpec((6, HEAD_DIM, n), lambda h: (0, h, 0)),
            const((2, n, D_RWKV)),
            const((16, D_RWKV)),
            const((QUAD, QUAD)),
            pl.BlockSpec((1, HEAD_DIM, HEAD_DIM, n), lambda h: (h, 0, 0, 0)),
        ],
        out_specs=[
            pl.BlockSpec((1, HEAD_DIM, HEAD_DIM, n), lambda h: (h, 0, 0, 0)),
            const((n, D_RWKV)),
        ],
        out_shape=[
            jax.ShapeDtypeStruct((H_RWKV, HEAD_DIM, HEAD_DIM, n), F32),
            jax.ShapeDtypeStruct((n, D_RWKV), F32),
        ],
        scratch_shapes=[pltpu.VMEM((D_RWKV, n), F32)],
        compiler_params=pltpu.CompilerParams(
            dimension_semantics=("arbitrary",), vmem_limit_bytes=VMEM_LIMIT),
        name="decode_state",
    )(vec_t, gb, prm, bmask, s_t)


def _decode_attn_kernel(qr_ref, kvn_ref, col_ref, ck_ref, cv_ref, sink_ref, ya_ref, kout_ref, vout_ref):
    nh = N_Q_HEADS
    seqs = range(DEC_TILE)
    hrow = lax.broadcasted_iota(jnp.int32, (nh, D_ATTN), 0)
    hlane = lax.broadcasted_iota(jnp.int32, (nh, D_ATTN), 1) // HEAD_DIM
    dmask = hrow == hlane
    grow = lax.broadcasted_iota(jnp.int32, (nh, D_KV), 0) // GQA_GROUP
    glane = lax.broadcasted_iota(jnp.int32, (nh, D_KV), 1) // HEAD_DIM
    gmask = grow == glane
    low = glane == 0
    key_idx = lax.broadcasted_iota(jnp.int32, (nh, WINDOW), 1)
    last = lax.broadcasted_iota(jnp.int32, (D_KV, WINDOW), 1) == WINDOW - 1
    sink = sink_ref[...]
    kvn = kvn_ref[...]
    col = col_ref[0]
    k_new = [kvn[j:j + 1, 0:D_KV] for j in seqs]
    v_new = [kvn[j:j + 1, D_KV:] for j in seqs]
    ck = [ck_ref[j] for j in seqs]
    cv = [cv_ref[j] for j in seqs]
    for j in seqs:
        kout_ref[j] = jnp.where(last, col[0:D_KV, j:j + 1], pltpu.roll(ck[j], WINDOW - 1, 1))
        vout_ref[j] = jnp.where(last, col[D_KV:, j:j + 1], pltpu.roll(cv[j], WINDOW - 1, 1))
    q8 = [qr_ref[j * nh:(j + 1) * nh, :] for j in seqs]
    qp = [jnp.where(gmask, jnp.concatenate([q8[j], q8[j]], axis=1), 0.0) for j in seqs]
    s_c = [jnp.where(key_idx >= 1, jnp.dot(qp[j], ck[j], preferred_element_type=F32) * ATTN_SCALE, NEG_INF)
           for j in seqs]
    s_n = [jnp.sum(qp[j] * k_new[j], axis=-1, keepdims=True) * ATTN_SCALE for j in seqs]
    m = [jnp.maximum(jnp.maximum(jnp.max(s_c[j], axis=-1, keepdims=True), s_n[j]), sink) for j in seqs]
    e_c = [jnp.exp(s_c[j] - m[j]) for j in seqs]
    e_n = [jnp.exp(s_n[j] - m[j]) for j in seqs]
    denom = [jnp.sum(e_c[j], axis=-1, keepdims=True) + e_n[j] + jnp.exp(sink - m[j]) for j in seqs]
    o = [(_dot_nt_f32(e_c[j], cv[j]) + e_n[j] * v_new[j]) / denom[j]
         for j in seqs]
    out_rows = []
    for j in seqs:
        rot = pltpu.roll(o[j], HEAD_DIM, 1)
        g0 = jnp.where(low, o[j], rot)
        g1 = jnp.where(low, rot, o[j])
        wide = jnp.concatenate([g0, g0, g1, g1], axis=1)
        out_rows.append(jnp.sum(jnp.where(dmask, wide, 0.0), axis=0, keepdims=True))
    ya_ref[...] = jnp.concatenate(out_rows, axis=0)


def _decode_attn(q_r, kvn, cols, ck_t, cv_t, sinks_col):
    n = kvn.shape[0]
    bt = DEC_TILE
    const = lambda shape: pl.BlockSpec(shape, lambda i: (0,) * len(shape))
    return pl.pallas_call(
        _decode_attn_kernel,
        grid=(n // bt,),
        in_specs=[
            pl.BlockSpec((bt * N_Q_HEADS, HEAD_DIM), lambda i: (i, 0)),
            pl.BlockSpec((bt, 2 * D_KV), lambda i: (i, 0)),
            pl.BlockSpec((1, 2 * D_KV, bt), lambda i: (i, 0, 0)),
            pl.BlockSpec((bt, D_KV, WINDOW), lambda i: (i, 0, 0)),
            pl.BlockSpec((bt, D_KV, WINDOW), lambda i: (i, 0, 0)),
            const((N_Q_HEADS, 1)),
        ],
        out_specs=[
            pl.BlockSpec((bt, D_ATTN), lambda i: (i, 0)),
            pl.BlockSpec((bt, D_KV, WINDOW), lambda i: (i, 0, 0)),
            pl.BlockSpec((bt, D_KV, WINDOW), lambda i: (i, 0, 0)),
        ],
        out_shape=[
            jax.ShapeDtypeStruct((n, D_ATTN), F32),
            jax.ShapeDtypeStruct((n, D_KV, WINDOW), F32),
            jax.ShapeDtypeStruct((n, D_KV, WINDOW), F32),
        ],
        compiler_params=pltpu.CompilerParams(
            dimension_semantics=("arbitrary",), vmem_limit_bytes=VMEM_LIMIT),
        name="decode_attn",
    )(q_r, kvn, cols, ck_t, cv_t, sinks_col)


def _rope_table(pos):
    inv_freq = jnp.power(ROPE_THETA, -jnp.arange(ROPE_HALF, dtype=F32) * (2.0 / ROPE_DIM))
    dim = np.arange(2 * HEAD_DIM) % HEAD_DIM
    ang = pos[:, None] * inv_freq[dim % ROPE_HALF][None, :]
    cos, sin = jnp.cos(ang), jnp.sin(ang)
    cos_t = jnp.where((dim < ROPE_DIM)[None, :], cos, 1.0)
    sin_lo = jnp.where((dim < ROPE_HALF)[None, :], -sin, 0.0)
    sin_hi = jnp.where(((dim >= ROPE_HALF) & (dim < ROPE_DIM))[None, :], sin, 0.0)
    return jnp.concatenate([cos_t, sin_lo, sin_hi], axis=1)


def _pad_cols(w, at, n):
    return jnp.concatenate([w[..., :at], jnp.zeros(w.shape[:-1] + (n,), w.dtype), w[..., at:]], axis=-1)


def kernel(x_prompt, x_sample, state_wkv, state_shift, cache_k_win, cache_v_win, norm_mix_w, w_in, mu_shift, w0, w_decay_up, a0, w_a_up, w_g_up, k_k, k_a, r_k, ln_x_w, ln_x_b, q_norm_w, k_norm_w, sinks, w_out, norm_ffn_w, w_ffn_up, w_ffn_down):
    bsz, t, _ = x_prompt.shape
    nd = x_sample.shape[0]
    l = 0
    pad = D_LORA_PAD - D_LORA

    w_in_pad = _pad_cols(w_in[l], D_SHIFT, pad).astype(BF16)
    mu_pad = _pad_cols(mu_shift[l][None, :], D_SHIFT, pad)
    wl = jnp.zeros((D_LORA_PAD, 3 * D_RWKV), F32)
    wl = wl.at[0:32, 0:D_RWKV].set(w_decay_up[l])
    wl = wl.at[32:64, D_RWKV:2 * D_RWKV].set(w_a_up[l])
    wl = wl.at[64:160, 2 * D_RWKV:].set(w_g_up[l])
    wl = wl.astype(BF16)
    prm = jnp.zeros((16, D_RWKV), F32)
    prm = prm.at[0].set(w0[l]).at[1].set(a0[l]).at[2].set(k_k[l]).at[3].set(k_a[l])
    prm = prm.at[4].set(r_k[l].reshape(-1)).at[5].set(ln_x_w[l]).at[6].set(ln_x_b[l])
    hid = np.arange(QUAD) // HEAD_DIM
    bmask = jnp.asarray(hid[:, None] == hid[None, :], BF16)
    tri = jnp.asarray(np.tril(np.ones((CHUNK, CHUNK))), BF16)
    qw = jnp.tile(q_norm_w[l][None, :], (1, N_Q_HEADS))
    kw = jnp.tile(k_norm_w[l][None, :], (1, N_KV_HEADS))
    nmw = norm_mix_w[l][None, :]
    nfw = norm_ffn_w[l][None, :]
    wo = w_out[l].astype(BF16)
    wu = w_ffn_up[l].astype(BF16)
    wd = w_ffn_down[l].astype(BF16)
    tab_p = _rope_table(jnp.arange(t, dtype=F32) + 0)
    tab_s = jnp.tile(_rope_table(jnp.arange(1, dtype=F32) + PAST_LEN), (nd, 1))

    xp = x_prompt.reshape(bsz * t, D_MODEL)
    xs_p, plast, q_p, kv_p = _inproj_shift(xp, nmw, w_in_pad, mu_pad, t, 512)
    yr_p, hbd = _rwkv_prompt(xs_p.reshape(bsz, t, D_SHIFT_PAD), prm, wl, tri, bmask)
    ya_p, kwin_p, vwin_p = _attn_prompt(q_p.reshape(bsz, t, D_ATTN), kv_p.reshape(bsz, t, 2 * D_KV), tab_p,
                                        qw, kw, sinks[l][None, :], bmask, _band_bias())
    y_p = _outffn(xp, yr_p.reshape(bsz * t, D_RWKV), ya_p.reshape(bsz * t, D_ATTN), wo, nfw, wu, wd, 256)
    y_prompt = y_p.reshape(bsz, t, D_MODEL)
    hb = hbd.reshape(bsz, 2, 4, HEAD_DIM, 2, HEAD_DIM)
    wkv_prompt = jnp.stack([hb[:, :, j, :, j % 2, :] for j in range(4)], axis=2)
    wkv_prompt = wkv_prompt.reshape(bsz, H_RWKV, HEAD_DIM, HEAD_DIM)[None]
    shift_prompt = plast[:, 0:1, :D_SHIFT][None]
    k_win_prompt = kwin_p.reshape(bsz, WINDOW, N_KV_HEADS, HEAD_DIM)[None]
    v_win_prompt = vwin_p.reshape(bsz, WINDOW, N_KV_HEADS, HEAD_DIM)[None]

    xs = x_sample.reshape(nd, D_MODEL)
    p_s, q_s, kv_s = _inproj(xs, nmw, w_in_pad, 128)
    shift_in = _pad_cols(state_shift[l].reshape(nd, D_SHIFT), D_SHIFT, pad)
    vec_t, gb, qn_s, kvn_s = _decode_prep(p_s, shift_in, q_s, kv_s, mu_pad, prm, wl, bmask, tab_s, qw, kw)
    s_t = jnp.transpose(state_wkv[l], (1, 2, 3, 0))
    ck_t = jnp.swapaxes(cache_k_win[l].reshape(nd, WINDOW, D_KV), 1, 2)
    cv_t = jnp.swapaxes(cache_v_win[l].reshape(nd, WINDOW, D_KV), 1, 2)
    q_r = qn_s.reshape(nd * N_Q_HEADS, HEAD_DIM)
    cols = jnp.swapaxes(kvn_s.reshape(nd // DEC_TILE, DEC_TILE, 2 * D_KV), 1, 2)
    wkv_t, yr_s = _decode_state(vec_t, gb, prm, bmask, s_t)
    ya_s, kc_t, vc_t = _decode_attn(q_r, kvn_s, cols, ck_t, cv_t, sinks[l][:, None])
    y_s = _outffn(xs, yr_s, ya_s, wo, nfw, wu, wd, 128)
    y_sample = y_s.reshape(nd, 1, D_MODEL)
    wkv_sample = jnp.transpose(wkv_t, (3, 0, 1, 2))[None]
    shift_sample = p_s[:, :D_SHIFT].reshape(nd, 1, D_SHIFT)[None]
    k_win_sample = jnp.swapaxes(kc_t, 1, 2).reshape(nd, WINDOW, N_KV_HEADS, HEAD_DIM)[None]
    v_win_sample = jnp.swapaxes(vc_t, 1, 2).reshape(nd, WINDOW, N_KV_HEADS, HEAD_DIM)[None]

    return (y_prompt, y_sample, wkv_prompt, shift_prompt, k_win_prompt, v_win_prompt,
            wkv_sample, shift_sample, k_win_sample, v_win_sample)
```

```python
import functools

import jax
import jax.numpy as jnp
import numpy as np
from jax import lax
from jax.experimental import pallas as pl
from jax.experimental.pallas import tpu as pltpu

F32 = jnp.float32
BF16 = jnp.bfloat16

D_MODEL = 1024
D_RWKV = 512
D_ATTN = 512
HEAD_DIM = 64
H_RWKV = 8
N_Q_HEADS = 8
N_KV_HEADS = 2
GQA_GROUP = 4
D_KV = 128
D_LORA = 160
D_LORA_PAD = 256
D_SHIFT = 3 * D_RWKV + D_LORA
D_SHIFT_PAD = 3 * D_RWKV + D_LORA_PAD
D_IN_PAD = D_SHIFT_PAD + D_ATTN + 2 * D_KV
WINDOW = 128
ROPE_DIM = 16
ROPE_HALF = 8
ROPE_THETA = 500000.0
ATTN_SCALE = HEAD_DIM ** -0.5
D_FF = 4096
RMS_EPS = 1e-6
LNX_EPS = 64e-5
NEG_INF = -1e30
LOG2E = 1.4426950408889634
PAST_LEN = 16384

CHUNK = 64
QUAD = 4 * HEAD_DIM
VMEM_LIMIT = 56 * 1024 * 1024


def _split2(x):
    hi = x.astype(BF16)
    lo = (x - hi.astype(F32)).astype(BF16)
    return hi, lo


def _head_sums(xs, ones_ref):
    n, w = xs[0].shape
    tile = min(w, QUAD)
    per = w // tile
    pieces = [x[:, j * tile:(j + 1) * tile] for x in xs for j in range(per)]
    stacked = jnp.concatenate(pieces, axis=0) if len(pieces) > 1 else pieces[0]
    ones = ones_ref[0:tile, 0:tile]
    hi, lo = _split2(stacked)
    out = jnp.dot(hi, ones, preferred_element_type=F32) + jnp.dot(lo, ones, preferred_element_type=F32)
    res = []
    for i in range(len(xs)):
        cols = [out[(i * per + j) * n:(i * per + j + 1) * n] for j in range(per)]
        res.append(jnp.concatenate(cols, axis=1) if per > 1 else cols[0])
    return res


def _cumsum_rows(tri_bf16, x):
    hi, lo = _split2(x)
    return (jnp.dot(tri_bf16, hi, preferred_element_type=F32)
            + jnp.dot(tri_bf16, lo, preferred_element_type=F32))


def _mm(a, b):
    return jnp.dot(a.astype(BF16), b.astype(BF16), preferred_element_type=F32)


def _mm_nt(a, b):
    return lax.dot_general(a.astype(BF16), b.astype(BF16), (((1,), (1,)), ((), ())),
                           preferred_element_type=F32)


def _mm_tn(a, b):
    return lax.dot_general(a.astype(BF16), b.astype(BF16), (((0,), (0,)), ((), ())),
                           preferred_element_type=F32)


def _dot_nt_f32(a, b):
    return lax.dot_general(a, b, (((1,), (1,)), ((), ())), preferred_element_type=F32)


def _dot_tn_f32(a, b):
    return lax.dot_general(a, b, (((0,), (0,)), ((), ())), preferred_element_type=F32)


def _sigmoid(x):
    return 1.0 / (1.0 + jnp.exp(-x))


def _softplus(x):
    return jnp.maximum(x, 0.0) + jnp.log1p(jnp.exp(-jnp.abs(x)))


def _inproj_kernel(x_ref, nw_ref, w_ref, p_ref, q_ref, kv_ref):
    x = x_ref[...]
    ms = jnp.mean(x * x, axis=-1, keepdims=True)
    h = (x * lax.rsqrt(ms + RMS_EPS)) * nw_ref[...]
    out = jnp.dot(h.astype(BF16), w_ref[...], preferred_element_type=F32)
    p_ref[...] = out[:, :D_SHIFT_PAD]
    q_ref[...] = out[:, D_SHIFT_PAD:D_SHIFT_PAD + D_ATTN]
    kv_ref[...] = out[:, D_SHIFT_PAD + D_ATTN:]


def _inproj_shift_kernel(tiles_per_seq, x_ref, nw_ref, w_ref, mu_ref, xs_ref, last_ref, q_ref, kv_ref, prev_ref):
    i = pl.program_id(0)

    @pl.when(i % tiles_per_seq == 0)
    def _():
        prev_ref[...] = jnp.zeros_like(prev_ref)

    x = x_ref[...]
    ms = jnp.mean(x * x, axis=-1, keepdims=True)
    h = (x * lax.rsqrt(ms + RMS_EPS)) * nw_ref[...]
    out = jnp.dot(h.astype(BF16), w_ref[...], preferred_element_type=F32)
    p = out[:, :D_SHIFT_PAD]
    tm = p.shape[0]
    row = lax.broadcasted_iota(jnp.int32, p.shape, 0)
    prev = jnp.where(row == 0, jnp.broadcast_to(prev_ref[0:1, :], p.shape), pltpu.roll(p, 1, 0))
    xs_ref[...] = p + (prev - p) * mu_ref[...]
    last = jnp.broadcast_to(p[tm - 1:tm, :], prev_ref.shape)
    prev_ref[...] = last
    last_ref[0] = last
    q_ref[...] = out[:, D_SHIFT_PAD:D_SHIFT_PAD + D_ATTN]
    kv_ref[...] = out[:, D_SHIFT_PAD + D_ATTN:]


def _inproj_shift(x2d, norm_w, w_in_pad, mu_pad, seq_len, tm):
    m = x2d.shape[0]
    tiles_per_seq = seq_len // tm
    return pl.pallas_call(
        functools.partial(_inproj_shift_kernel, tiles_per_seq),
        grid=(m // tm,),
        in_specs=[
            pl.BlockSpec((tm, D_MODEL), lambda i: (i, 0)),
            pl.BlockSpec((1, D_MODEL), lambda i: (0, 0)),
            pl.BlockSpec((D_MODEL, D_IN_PAD), lambda i: (0, 0)),
            pl.BlockSpec((1, D_SHIFT_PAD), lambda i: (0, 0)),
        ],
        out_specs=[
            pl.BlockSpec((tm, D_SHIFT_PAD), lambda i: (i, 0)),
            pl.BlockSpec((1, 8, D_SHIFT_PAD), lambda i: (i // tiles_per_seq, 0, 0)),
            pl.BlockSpec((tm, D_ATTN), lambda i: (i, 0)),
            pl.BlockSpec((tm, 2 * D_KV), lambda i: (i, 0)),
        ],
        out_shape=[
            jax.ShapeDtypeStruct((m, D_SHIFT_PAD), F32),
            jax.ShapeDtypeStruct((m // seq_len, 8, D_SHIFT_PAD), F32),
            jax.ShapeDtypeStruct((m, D_ATTN), F32),
            jax.ShapeDtypeStruct((m, 2 * D_KV), F32),
        ],
        scratch_shapes=[pltpu.VMEM((8, D_SHIFT_PAD), F32)],
        compiler_params=pltpu.CompilerParams(
            dimension_semantics=("arbitrary",), vmem_limit_bytes=VMEM_LIMIT),
        name="inproj_shift",
    )(x2d, norm_w, w_in_pad, mu_pad)


def _inproj(x2d, norm_w, w_in_pad, tm):
    m = x2d.shape[0]
    return pl.pallas_call(
        _inproj_kernel,
        grid=(m // tm,),
        in_specs=[
            pl.BlockSpec((tm, D_MODEL), lambda i: (i, 0)),
            pl.BlockSpec((1, D_MODEL), lambda i: (0, 0)),
            pl.BlockSpec((D_MODEL, D_IN_PAD), lambda i: (0, 0)),
        ],
        out_specs=[
            pl.BlockSpec((tm, D_SHIFT_PAD), lambda i: (i, 0)),
            pl.BlockSpec((tm, D_ATTN), lambda i: (i, 0)),
            pl.BlockSpec((tm, 2 * D_KV), lambda i: (i, 0)),
        ],
        out_shape=[
            jax.ShapeDtypeStruct((m, D_SHIFT_PAD), F32),
            jax.ShapeDtypeStruct((m, D_ATTN), F32),
            jax.ShapeDtypeStruct((m, 2 * D_KV), F32),
        ],
        compiler_params=pltpu.CompilerParams(
            dimension_semantics=("arbitrary",), vmem_limit_bytes=VMEM_LIMIT),
        name="inproj",
    )(x2d, norm_w, w_in_pad)


def _rwkv_features(xs, prm, wl_ref, ones_ref):
    r = xs[:, 0:D_RWKV]
    k = xs[:, D_RWKV:2 * D_RWKV]
    v = xs[:, 2 * D_RWKV:3 * D_RWKV]
    lora = xs[:, 3 * D_RWKV:]
    col = lax.broadcasted_iota(jnp.int32, lora.shape, 1)
    act = jnp.where(col < 32, jnp.tanh(lora), jnp.where(col < 64, lora, _sigmoid(lora)))
    up = jnp.dot(act.astype(BF16), wl_ref[...], preferred_element_type=F32)
    w0, a0, k_k, k_a, r_k = prm[0:1], prm[1:2], prm[2:3], prm[3:4], prm[4:5]
    w_log = -_softplus(-(w0 + up[:, 0:D_RWKV])) - 0.5
    logw = -jnp.exp(w_log)
    asig = _sigmoid(a0 + up[:, D_RWKV:2 * D_RWKV])
    g = up[:, 2 * D_RWKV:]
    kk = k * k_k
    k_mod = k * (1.0 + (asig - 1.0) * k_a)
    ss, rk = _head_sums([kk * kk, r * k_mod * r_k], ones_ref)
    kk = kk / jnp.maximum(jnp.sqrt(ss), 1e-12)
    k = k_mod
    bonus = rk * v
    return r, logw, k, v, -kk, kk * asig, g, bonus


def _rwkv_finish(y, g, bonus, prm, ones_ref):
    ln_w, ln_b = prm[5:6], prm[6:7]
    mean = _head_sums([y], ones_ref)[0] * (1.0 / HEAD_DIM)
    d = y - mean
    var = _head_sums([d * d], ones_ref)[0] * (1.0 / HEAD_DIM)
    yn = d * lax.rsqrt(var + LNX_EPS) * ln_w + ln_b
    return (yn + bonus) * g


def _block_diag(x, bmask):
    return jnp.concatenate([x] * 4, axis=0) * bmask


def _chunk_prep(r, logw, k, v, a, b, tri_ref, bmask):
    n = len(r)
    ch = range(n)
    tri = tri_ref[...]
    cum = [_cumsum_rows(tri, logw[i]) for i in ch]
    e_in = [jnp.exp(cum[i]) for i in ch]
    e_ex = [jnp.exp(cum[i] - logw[i]) for i in ch]
    e_inv = [1.0 / e_in[i] for i in ch]
    e_last = [e_in[i][CHUNK - 1:CHUNK, :] for i in ch]
    rt = [(r[i] * e_in[i]).astype(BF16) for i in ch]
    at = [(a[i] * e_ex[i]).astype(BF16) for i in ch]
    kt = [(k[i] * e_inv[i]).astype(BF16) for i in ch]
    bt = [(b[i] * e_inv[i]).astype(BF16) for i in ch]
    vb = [v[i].astype(BF16) for i in ch]

    t_idx = lax.broadcasted_iota(jnp.int32, (CHUNK, QUAD), 0)
    s_idx = lax.broadcasted_iota(jnp.int32, (CHUNK, QUAD), 1) & (HEAD_DIM - 1)
    strict = s_idx < t_idx
    incl = s_idx <= t_idx

    gm = [_mm_nt(jnp.concatenate([at[i], rt[i]], axis=0),
                 jnp.concatenate([_block_diag(bt[i], bmask), _block_diag(kt[i], bmask)], axis=0))
          for i in ch]
    a_ab = [jnp.where(strict, gm[i][:CHUNK, :QUAD], 0.0) for i in ch]
    a_ak = [jnp.where(strict, gm[i][:CHUNK, QUAD:], 0.0) for i in ch]
    a_rb = [jnp.where(incl, gm[i][CHUNK:, :QUAD], 0.0) for i in ch]
    a_rk = [jnp.where(incl, gm[i][CHUNK:, QUAD:], 0.0) for i in ch]

    eye = jnp.where(s_idx == t_idx, 1.0, 0.0)
    pwb = [a_ab[i].astype(BF16) for i in ch]
    t_inv = [eye + a_ab[i] for i in ch]
    for it in range(6):
        rbd = [_block_diag(pwb[i], bmask) for i in ch]
        if it == 0:
            pwb = [_mm(pwb[i], rbd[i]).astype(BF16) for i in ch]
        elif it < 5:
            out = [_mm(jnp.concatenate([pwb[i], t_inv[i].astype(BF16)], axis=0), rbd[i]) for i in ch]
            pwb = [out[i][:CHUNK].astype(BF16) for i in ch]
            t_inv = [t_inv[i] + out[i][CHUNK:] for i in ch]
        else:
            t_inv = [t_inv[i] + _mm(t_inv[i], rbd[i]) for i in ch]

    vbd = [_block_diag(vb[i], bmask) for i in ch]
    xy0 = [_mm(jnp.concatenate([a_ak[i], a_rk[i]], axis=0), vbd[i]) for i in ch]
    return [dict(ar=jnp.concatenate([at[i], rt[i]], axis=0), x0=xy0[i][:CHUNK], y0=xy0[i][CHUNK:],
                 t_inv=t_inv[i].astype(BF16), a_rb=a_rb[i].astype(BF16), vb=vb[i],
                 bk=jnp.concatenate([bt[i], kt[i]], axis=0), e_last=e_last[i]) for i in ch]


def _chunk_step(pre, state, bmask):
    ch = range(len(pre))
    half = QUAD // 2
    zeros = jnp.zeros((half, half), BF16)
    sc = [state[i].astype(BF16) for i in ch]
    sb = [jnp.concatenate([jnp.concatenate([sc[i][:half], zeros], axis=1),
                           jnp.concatenate([zeros, sc[i][half:]], axis=1)], axis=0) for i in ch]
    xr = [_mm_nt(pre[i]["ar"], sb[i]) for i in ch]
    x = [xr[i][:CHUNK] + pre[i]["x0"] for i in ch]
    u = [_mm(pre[i]["t_inv"], _block_diag(x[i].astype(BF16), bmask)) for i in ch]
    ub = [u[i].astype(BF16) for i in ch]
    y = [xr[i][CHUNK:] + pre[i]["y0"] + _mm(pre[i]["a_rb"], _block_diag(ub[i], bmask)) for i in ch]
    upd = [_mm_tn(jnp.concatenate([ub[i], pre[i]["vb"]], axis=0), pre[i]["bk"]) for i in ch]
    bm = bmask[:half, :half].astype(F32)
    s_new = []
    for i in ch:
        e_last = pre[i]["e_last"]
        top = (state[i][:half] + upd[i][:half, :half] * bm) * e_last[:, :half]
        bot = (state[i][half:] + upd[i][half:, half:] * bm) * e_last[:, half:]
        s_new.append(jnp.concatenate([top, bot], axis=0))
    return y, s_new


def _rwkv_prompt_kernel(xs_ref, prm_ref, wl_ref, tri_ref, bmask_ref, y_ref, hout_ref, h_ref):
    c = pl.program_id(0)
    nseq, tstep, _ = xs_ref.shape
    nchunk = tstep // CHUNK

    @pl.when(c == 0)
    def _():
        h_ref[...] = jnp.zeros_like(h_ref)

    xs = jnp.concatenate([xs_ref[s] for s in range(nseq)], axis=0)
    prm = prm_ref[...]
    r, logw, k, v, a, b, g, bonus = _rwkv_features(xs, prm, wl_ref, bmask_ref)
    bmask = bmask_ref[...]
    lanes = [(s, q) for s in range(nseq) for q in range(2)]
    chains = [(j, s, q) for j in range(nchunk) for s, q in lanes]
    cut = lambda x: [x[s * tstep + j * CHUNK:s * tstep + (j + 1) * CHUNK, q * QUAD:(q + 1) * QUAD]
                     for j, s, q in chains]
    pre = _chunk_prep(cut(r), cut(logw), cut(k), cut(v), cut(a), cut(b), tri_ref, bmask)
    state = [h_ref[s, q] for s, q in lanes]
    ys = []
    for j in range(nchunk):
        y_j, state = _chunk_step(pre[j * len(lanes):(j + 1) * len(lanes)], state, bmask)
        ys.append(y_j)
    for i, (s, q) in enumerate(lanes):
        h_ref[s, q] = state[i]
    rows = [jnp.concatenate(ys[j][2 * s:2 * s + 2], axis=1) for s in range(nseq) for j in range(nchunk)]
    y = _rwkv_finish(jnp.concatenate(rows, axis=0), g, bonus, prm, bmask_ref)
    for s in range(nseq):
        y_ref[s] = y[s * tstep:(s + 1) * tstep]

    @pl.when(c == pl.num_programs(0) - 1)
    def _():
        hout_ref[...] = h_ref[...]


RWKV_CHUNKS_PER_STEP = 4


def _rwkv_prompt(xs3d, prm, wl, tri, bmask):
    bsz, t, _ = xs3d.shape
    tstep = RWKV_CHUNKS_PER_STEP * CHUNK
    nc = t // tstep
    const = lambda shape: pl.BlockSpec(shape, lambda c: (0,) * len(shape))
    state_shape = (bsz, 2, QUAD, QUAD // 2)
    return pl.pallas_call(
        _rwkv_prompt_kernel,
        grid=(nc,),
        in_specs=[
            pl.BlockSpec((bsz, tstep, D_SHIFT_PAD), lambda c: (0, c, 0)),
            const((16, D_RWKV)),
            const((D_LORA_PAD, 3 * D_RWKV)),
            const((CHUNK, CHUNK)),
            const((QUAD, QUAD)),
        ],
        out_specs=[
            pl.BlockSpec((bsz, tstep, D_RWKV), lambda c: (0, c, 0)),
            const(state_shape),
        ],
        out_shape=[
            jax.ShapeDtypeStruct((bsz, t, D_RWKV), F32),
            jax.ShapeDtypeStruct(state_shape, F32),
        ],
        scratch_shapes=[pltpu.VMEM(state_shape, F32)],
        compiler_params=pltpu.CompilerParams(
            dimension_semantics=("arbitrary",), vmem_limit_bytes=VMEM_LIMIT),
        name="rwkv_prompt",
    )(xs3d, prm, wl, tri, bmask)


def _qk_norm_rope(x, norm_w, cos_t, sin_lo, sin_hi, ones_ref):
    ms = _head_sums([x * x], ones_ref)[0] * (1.0 / HEAD_DIM)
    xn = x * lax.rsqrt(ms + RMS_EPS) * norm_w
    width = x.shape[1]
    fwd = pltpu.roll(xn, width - ROPE_HALF, 1)
    bwd = pltpu.roll(xn, ROPE_HALF, 1)
    return xn * cos_t + fwd * sin_lo + bwd * sin_hi


def _tile_lanes(x, reps):
    return jnp.concatenate([x] * reps, axis=1) if reps > 1 else x


def _interleave(*gens):
    live = list(gens)
    while live:
        for g in list(live):
            try:
                next(g)
            except StopIteration:
                live.remove(g)


def _attn_stages(q_ref, kv_ref, tab_ref, qw_ref, kw_ref, sink_ref, ones_ref, bias_ref,
                 kprev_ref, vprev_ref, out):
    nseq = q_ref.shape[0]
    seqs = range(nseq)
    tab = tab_ref[...]
    cos_t, sin_lo, sin_hi = tab[:, 0:128], tab[:, 128:256], tab[:, 256:384]
    cos4, slo4, shi4 = _tile_lanes(cos_t, 4), _tile_lanes(sin_lo, 4), _tile_lanes(sin_hi, 4)
    q = [_qk_norm_rope(q_ref[s], qw_ref[...], cos4, slo4, shi4, ones_ref) * (ATTN_SCALE * LOG2E) for s in seqs]
    kv = [kv_ref[s] for s in seqs]
    k_cur = [_qk_norm_rope(kv[s][:, 0:D_KV], kw_ref[...], cos_t, sin_lo, sin_hi, ones_ref) for s in seqs]
    v_cur = [kv[s][:, D_KV:] for s in seqs]
    k_all = [jnp.concatenate([kprev_ref[s], k_cur[s]], axis=0) for s in seqs]
    v_all = [jnp.concatenate([vprev_ref[s], v_cur[s]], axis=0) for s in seqs]
    for s in seqs:
        kprev_ref[s] = k_cur[s]
        vprev_ref[s] = v_cur[s]
    out["k_cur"], out["v_cur"] = k_cur, v_cur
    yield

    nk = 2 * WINDOW
    bias = bias_ref[0]
    sinks = sink_ref[...] * LOG2E
    low = lax.broadcasted_iota(jnp.int32, (nk, D_KV), 1) < HEAD_DIM
    lane_blk = [ones_ref[j * HEAD_DIM:j * HEAD_DIM + 1, :] for j in range(GQA_GROUP)]

    chains = [(s, g) for s in seqs for g in range(N_KV_HEADS)]
    ch = range(len(chains))
    k_rot = [pltpu.roll(k_all[s], HEAD_DIM, 1) for s in seqs]
    k2 = [jnp.where(low, k_all[s], k_rot[s]) if g == 0 else jnp.where(low, k_rot[s], k_all[s])
          for s, g in chains]
    k4 = [jnp.concatenate([k2[c], k2[c]], axis=1).astype(BF16) for c in ch]
    vb = [v_all[s].astype(BF16) for s in seqs]
    qg = [q[s][:, g * QUAD:(g + 1) * QUAD].astype(BF16) for s, g in chains]
    qstack = [jnp.concatenate([qg[c] * lane_blk[j] for j in range(GQA_GROUP)], axis=0) for c in ch]
    sink_row = [jnp.concatenate(
        [jnp.broadcast_to(sinks[:, g * GQA_GROUP + j:g * GQA_GROUP + j + 1], (1, WINDOW))
         for j in range(GQA_GROUP)], axis=1) for s, g in chains]
    yield
    sc = [_mm_nt(k4[c], qstack[c]) + bias for c in ch]
    yield
    m = [jnp.maximum(jnp.max(sc[c], axis=0, keepdims=True), sink_row[c]) for c in ch]
    e = [jnp.exp2(sc[c] - m[c]) for c in ch]
    yield
    denom = [jnp.sum(e[c], axis=0, keepdims=True) + jnp.exp2(sink_row[c] - m[c]) for c in ch]
    ot = [_mm_tn(vb[s], e[c].astype(BF16))[g * HEAD_DIM:(g + 1) * HEAD_DIM, :] * (1.0 / denom[c])
          for c, (s, g) in enumerate(chains)]
    yield
    ya = []
    for s in seqs:
        yt = jnp.concatenate([ot[s * N_KV_HEADS + g][:, j * WINDOW:(j + 1) * WINDOW]
                              for g in range(N_KV_HEADS) for j in range(GQA_GROUP)], axis=0)
        ya.append(jnp.transpose(yt))
    out["ya"] = ya


def _ffn_stages(x, yr, ya, wo_ref, nw_ref, wu_ref, wd_ref, out, pieces=4):
    mix = jnp.concatenate([yr, ya], axis=1).astype(BF16)
    x1 = x + jnp.dot(mix, wo_ref[...], preferred_element_type=F32)
    yield
    ms = jnp.mean(x1 * x1, axis=-1, keepdims=True)
    hf = ((x1 * lax.rsqrt(ms + RMS_EPS)) * nw_ref[...]).astype(BF16)
    acc = x1
    step = D_FF // pieces
    for j in range(pieces):
        up = jnp.dot(hf, wu_ref[:, j * step:(j + 1) * step], preferred_element_type=F32)
        yield
        act = jnp.square(jnp.maximum(up, 0.0)).astype(BF16)
        acc = acc + jnp.dot(act, wd_ref[j * step:(j + 1) * step, :], preferred_element_type=F32)
        yield
    out["y"] = acc


def _attn_ffn_kernel(q_ref, kv_ref, tab_ref, qw_ref, kw_ref, sink_ref, ones_ref, bias_ref,
                     x_ref, yr_ref, wo_ref, nw_ref, wu_ref, wd_ref,
                     o_ref, kwin_ref, vwin_ref, kprev_ref, vprev_ref, ya_ref):
    i = pl.program_id(0)
    nseq = q_ref.shape[0]
    seqs = range(nseq)

    @pl.when(i == 0)
    def _():
        kprev_ref[...] = jnp.zeros_like(kprev_ref)
        vprev_ref[...] = jnp.zeros_like(vprev_ref)
        ya_ref[...] = jnp.zeros_like(ya_ref)

    rows = lambda ref: jnp.concatenate([ref[s] for s in seqs], axis=0)
    a_out, f_out = {}, {}
    _interleave(
        _ffn_stages(rows(x_ref), rows(yr_ref), rows(ya_ref), wo_ref, nw_ref, wu_ref, wd_ref, f_out),
        _attn_stages(q_ref, kv_ref, tab_ref, qw_ref, kw_ref, sink_ref, ones_ref, bias_ref,
                     kprev_ref, vprev_ref, a_out))
    for s in seqs:
        o_ref[s] = f_out["y"][s * WINDOW:(s + 1) * WINDOW]
        ya_ref[s] = a_out["ya"][s]

    @pl.when(i == pl.num_programs(0) - 2)
    def _():
        for s in seqs:
            kwin_ref[s] = a_out["k_cur"][s]
            vwin_ref[s] = a_out["v_cur"][s]


def _band_bias():
    ki = np.arange(2 * WINDOW)[:, None]
    qi = (np.arange(GQA_GROUP * WINDOW) % WINDOW + WINDOW)[None, :]
    dq = qi - ki
    band = (dq >= 0) & (dq < WINDOW)
    first = band & (ki >= WINDOW)
    return jnp.asarray(np.where(np.stack([first, band]), 0.0, NEG_INF), F32)


def _attn_ffn(q3d, kv3d, tab, qw, kw, sinks, ones_bd, bias, x3d, yr3d, wo, nw, wu, wd):
    bsz, t, _ = q3d.shape
    nb = t // WINDOW
    const = lambda shape: pl.BlockSpec(shape, lambda i: (0,) * len(shape))
    cur = lambda i: jnp.minimum(i, nb - 1)
    prv = lambda i: jnp.maximum(i - 1, 0)
    return pl.pallas_call(
        _attn_ffn_kernel,
        grid=(nb + 1,),
        in_specs=[
            pl.BlockSpec((bsz, WINDOW, D_ATTN), lambda i: (0, cur(i), 0)),
            pl.BlockSpec((bsz, WINDOW, 2 * D_KV), lambda i: (0, cur(i), 0)),
            pl.BlockSpec((WINDOW, 384), lambda i: (cur(i), 0)),
            const((1, D_ATTN)),
            const((1, D_KV)),
            const((1, N_Q_HEADS)),
            const((QUAD, QUAD)),
            pl.BlockSpec((1, 2 * WINDOW, GQA_GROUP * WINDOW), lambda i: (jnp.minimum(i, 1), 0, 0)),
            pl.BlockSpec((bsz, WINDOW, D_MODEL), lambda i: (0, prv(i), 0)),
            pl.BlockSpec((bsz, WINDOW, D_RWKV), lambda i: (0, prv(i), 0)),
            const((D_MODEL, D_MODEL)),
            const((1, D_MODEL)),
            const((D_MODEL, D_FF)),
            const((D_FF, D_MODEL)),
        ],
        out_specs=[
            pl.BlockSpec((bsz, WINDOW, D_MODEL), lambda i: (0, prv(i), 0)),
            const((bsz, WINDOW, D_KV)),
            const((bsz, WINDOW, D_KV)),
        ],
        out_shape=[
            jax.ShapeDtypeStruct((bsz, t, D_MODEL), F32),
            jax.ShapeDtypeStruct((bsz, WINDOW, D_KV), F32),
            jax.ShapeDtypeStruct((bsz, WINDOW, D_KV), F32),
        ],
        scratch_shapes=[
            pltpu.VMEM((bsz, WINDOW, D_KV), F32),
            pltpu.VMEM((bsz, WINDOW, D_KV), F32),
            pltpu.VMEM((bsz, WINDOW, D_ATTN), F32),
        ],
        compiler_params=pltpu.CompilerParams(
            dimension_semantics=("arbitrary",), vmem_limit_bytes=VMEM_LIMIT),
        name="attn_ffn",
    )(q3d, kv3d, tab, qw, kw, sinks, ones_bd, bias, x3d, yr3d, wo, nw, wu, wd)


def _outffn_kernel(x_ref, yr_ref, ya_ref, wo_ref, nw_ref, wu_ref, wd_ref, o_ref):
    out = {}
    _interleave(_ffn_stages(x_ref[...], yr_ref[...], ya_ref[...], wo_ref, nw_ref, wu_ref, wd_ref, out))
    o_ref[...] = out["y"]


def _outffn(x2d, yr, ya, wo, nw, wu, wd, tm):
    m = x2d.shape[0]
    const = lambda shape: pl.BlockSpec(shape, lambda i: (0,) * len(shape))
    return pl.pallas_call(
        _outffn_kernel,
        grid=(m // tm,),
        in_specs=[
            pl.BlockSpec((tm, D_MODEL), lambda i: (i, 0)),
            pl.BlockSpec((tm, D_RWKV), lambda i: (i, 0)),
            pl.BlockSpec((tm, D_ATTN), lambda i: (i, 0)),
            const((D_MODEL, D_MODEL)),
            const((1, D_MODEL)),
            const((D_MODEL, D_FF)),
            const((D_FF, D_MODEL)),
        ],
        out_specs=pl.BlockSpec((tm, D_MODEL), lambda i: (i, 0)),
        out_shape=jax.ShapeDtypeStruct((m, D_MODEL), F32),
        compiler_params=pltpu.CompilerParams(
            dimension_semantics=("arbitrary",), vmem_limit_bytes=VMEM_LIMIT),
        name="outffn",
    )(x2d, yr, ya, wo, nw, wu, wd)


DEC_TILE = 8


def _decode_prep_kernel(p_ref, sh_ref, q_ref, kv_ref, mu_ref, prm_ref, wl_ref, ones_ref, tab_ref,
                        qw_ref, kw_ref, vec_ref, vgb_ref, qn_ref, kvn_ref):
    p = p_ref[...]
    xs = p + (sh_ref[...] - p) * mu_ref[...]
    r, logw, k, v, a, b, g, bonus = _rwkv_features(xs, prm_ref[...], wl_ref, ones_ref)
    for i, x in enumerate((a, b, k, jnp.exp(logw), r, v)):
        vec_ref[i] = jnp.transpose(x)
    vgb_ref[0] = g
    vgb_ref[1] = bonus
    tab = tab_ref[...]
    cos_t, sin_lo, sin_hi = tab[:, 0:128], tab[:, 128:256], tab[:, 256:384]
    qn_ref[...] = _qk_norm_rope(q_ref[...], qw_ref[...], _tile_lanes(cos_t, 4), _tile_lanes(sin_lo, 4),
                                _tile_lanes(sin_hi, 4), ones_ref)
    kv = kv_ref[...]
    kvn_ref[:, 0:D_KV] = _qk_norm_rope(kv[:, 0:D_KV], kw_ref[...], cos_t, sin_lo, sin_hi, ones_ref)
    kvn_ref[:, D_KV:] = kv[:, D_KV:]


def _decode_prep(p, shift, q, kv, mu_pad, prm, wl, bmask, tab, qw, kw):
    n = p.shape[0]
    full = lambda shape: pl.BlockSpec(shape, lambda i: (0,) * len(shape))
    return pl.pallas_call(
        _decode_prep_kernel,
        grid=(1,),
        in_specs=[full((n, D_SHIFT_PAD)), full((n, D_SHIFT_PAD)), full((n, D_ATTN)), full((n, 2 * D_KV)),
                  full((1, D_SHIFT_PAD)), full((16, D_RWKV)), full((D_LORA_PAD, 3 * D_RWKV)),
                  full((QUAD, QUAD)), full((n, 384)), full((1, D_ATTN)), full((1, D_KV))],
        out_specs=[full((6, D_RWKV, n)), full((2, n, D_RWKV)), full((n, D_ATTN)), full((n, 2 * D_KV))],
        out_shape=[
            jax.ShapeDtypeStruct((6, D_RWKV, n), F32),
            jax.ShapeDtypeStruct((2, n, D_RWKV), F32),
            jax.ShapeDtypeStruct((n, D_ATTN), F32),
            jax.ShapeDtypeStruct((n, 2 * D_KV), F32),
        ],
        compiler_params=pltpu.CompilerParams(
            dimension_semantics=("arbitrary",), vmem_limit_bytes=VMEM_LIMIT),
        name="decode_prep",
    )(p, shift, q, kv, mu_pad, prm, wl, bmask, tab, qw, kw)


def _decode_state_kernel(vec_ref, gb_ref, prm_ref, ones_ref, s_ref, sout_ref, yr_ref, yt_ref):
    h = pl.program_id(0)
    a_t, b_t, k_t, w_t, r_t = (vec_ref[i] for i in range(5))

    def body(i, carry):
        s = s_ref[0, i]
        sa = jnp.sum(s * a_t, axis=0, keepdims=True)
        v_i = vec_ref[5, pl.ds(i, 1), :]
        s_new = s * w_t + sa * b_t + v_i * k_t
        sout_ref[0, i] = s_new
        yt_ref[pl.ds(h * HEAD_DIM + i, 1), :] = jnp.sum(s_new * r_t, axis=0, keepdims=True)
        return carry

    lax.fori_loop(0, HEAD_DIM, body, 0, unroll=8)

    @pl.when(h == pl.num_programs(0) - 1)
    def _():
        y = jnp.transpose(yt_ref[...])
        yr_ref[...] = _rwkv_finish(y, gb_ref[0], gb_ref[1], prm_ref[...], ones_ref)


def _decode_state(vec_t, gb, prm, bmask, s_t):
    n = s_t.shape[-1]
    const = lambda shape: pl.BlockSpec(shape, lambda h: (0,) * len(shape))
    return pl.pallas_call(
        _decode_state_kernel,
        grid=(H_RWKV,),
        in_specs=[
            pl.BlockSpec((6, HEAD_DIM, n), lambda h: (0, h, 0)),
            const((2, n, D_RWKV)),
            const((16, D_RWKV)),
            const((QUAD, QUAD)),
            pl.BlockSpec((1, HEAD_DIM, HEAD_DIM, n), lambda h: (h, 0, 0, 0)),
        ],
        out_specs=[
            pl.BlockSpec((1, HEAD_DIM, HEAD_DIM, n), lambda h: (h, 0, 0, 0)),
            const((n, D_RWKV)),
        ],
        out_shape=[
            jax.ShapeDtypeStruct((H_RWKV, HEAD_DIM, HEAD_DIM, n), F32),
            jax.ShapeDtypeStruct((n, D_RWKV), F32),
        ],
        scratch_shapes=[pltpu.VMEM((D_RWKV, n), F32)],
        compiler_params=pltpu.CompilerParams(
            dimension_semantics=("arbitrary",), vmem_limit_bytes=VMEM_LIMIT),
        name="decode_state",
    )(vec_t, gb, prm, bmask, s_t)


def _decode_attn_kernel(qr_ref, kvn_ref, col_ref, ck_ref, cv_ref, sink_ref, ya_ref, kout_ref, vout_ref):
    nh = N_Q_HEADS
    seqs = range(DEC_TILE)
    hrow = lax.broadcasted_iota(jnp.int32, (nh, D_ATTN), 0)
    hlane = lax.broadcasted_iota(jnp.int32, (nh, D_ATTN), 1) // HEAD_DIM
    dmask = hrow == hlane
    grow = lax.broadcasted_iota(jnp.int32, (nh, D_KV), 0) // GQA_GROUP
    glane = lax.broadcasted_iota(jnp.int32, (nh, D_KV), 1) // HEAD_DIM
    gmask = grow == glane
    low = glane == 0
    key_idx = lax.broadcasted_iota(jnp.int32, (nh, WINDOW), 1)
    last = lax.broadcasted_iota(jnp.int32, (D_KV, WINDOW), 1) == WINDOW - 1
    sink = sink_ref[...]
    kvn = kvn_ref[...]
    col = col_ref[0]
    k_new = [kvn[j:j + 1, 0:D_KV] for j in seqs]
    v_new = [kvn[j:j + 1, D_KV:] for j in seqs]
    ck = [ck_ref[j] for j in seqs]
    cv = [cv_ref[j] for j in seqs]
    for j in seqs:
        kout_ref[j] = jnp.where(last, col[0:D_KV, j:j + 1], pltpu.roll(ck[j], WINDOW - 1, 1))
        vout_ref[j] = jnp.where(last, col[D_KV:, j:j + 1], pltpu.roll(cv[j], WINDOW - 1, 1))
    q8 = [qr_ref[j * nh:(j + 1) * nh, :] for j in seqs]
    qp = [jnp.where(gmask, jnp.concatenate([q8[j], q8[j]], axis=1), 0.0) for j in seqs]
    s_c = [jnp.where(key_idx >= 1, jnp.dot(qp[j], ck[j], preferred_element_type=F32) * ATTN_SCALE, NEG_INF)
           for j in seqs]
    s_n = [jnp.sum(qp[j] * k_new[j], axis=-1, keepdims=True) * ATTN_SCALE for j in seqs]
    m = [jnp.maximum(jnp.maximum(jnp.max(s_c[j], axis=-1, keepdims=True), s_n[j]), sink) for j in seqs]
    e_c = [jnp.exp(s_c[j] - m[j]) for j in seqs]
    e_n = [jnp.exp(s_n[j] - m[j]) for j in seqs]
    denom = [jnp.sum(e_c[j], axis=-1, keepdims=True) + e_n[j] + jnp.exp(sink - m[j]) for j in seqs]
    o = [(_dot_nt_f32(e_c[j], cv[j]) + e_n[j] * v_new[j]) / denom[j]
         for j in seqs]
    out_rows = []
    for j in seqs:
        rot = pltpu.roll(o[j], HEAD_DIM, 1)
        g0 = jnp.where(low, o[j], rot)
        g1 = jnp.where(low, rot, o[j])
        wide = jnp.concatenate([g0, g0, g1, g1], axis=1)
        out_rows.append(jnp.sum(jnp.where(dmask, wide, 0.0), axis=0, keepdims=True))
    ya_ref[...] = jnp.concatenate(out_rows, axis=0)


def _decode_attn(q_r, kvn, cols, ck_t, cv_t, sinks_col):
    n = kvn.shape[0]
    bt = DEC_TILE
    const = lambda shape: pl.BlockSpec(shape, lambda i: (0,) * len(shape))
    return pl.pallas_call(
        _decode_attn_kernel,
        grid=(n // bt,),
        in_specs=[
            pl.BlockSpec((bt * N_Q_HEADS, HEAD_DIM), lambda i: (i, 0)),
            pl.BlockSpec((bt, 2 * D_KV), lambda i: (i, 0)),
            pl.BlockSpec((1, 2 * D_KV, bt), lambda i: (i, 0, 0)),
            pl.BlockSpec((bt, D_KV, WINDOW), lambda i: (i, 0, 0)),
            pl.BlockSpec((bt, D_KV, WINDOW), lambda i: (i, 0, 0)),
            const((N_Q_HEADS, 1)),
        ],
        out_specs=[
            pl.BlockSpec((bt, D_ATTN), lambda i: (i, 0)),
            pl.BlockSpec((bt, D_KV, WINDOW), lambda i: (i, 0, 0)),
            pl.BlockSpec((bt, D_KV, WINDOW), lambda i: (i, 0, 0)),
        ],
        out_shape=[
            jax.ShapeDtypeStruct((n, D_ATTN), F32),
            jax.ShapeDtypeStruct((n, D_KV, WINDOW), F32),
            jax.ShapeDtypeStruct((n, D_KV, WINDOW), F32),
        ],
        compiler_params=pltpu.CompilerParams(
            dimension_semantics=("arbitrary",), vmem_limit_bytes=VMEM_LIMIT),
        name="decode_attn",
    )(q_r, kvn, cols, ck_t, cv_t, sinks_col)


def _rope_table(pos):
    inv_freq = jnp.power(ROPE_THETA, -jnp.arange(ROPE_HALF, dtype=F32) * (2.0 / ROPE_DIM))
    dim = np.arange(2 * HEAD_DIM) % HEAD_DIM
    ang = pos[:, None] * inv_freq[dim % ROPE_HALF][None, :]
    cos, sin = jnp.cos(ang), jnp.sin(ang)
    cos_t = jnp.where((dim < ROPE_DIM)[None, :], cos, 1.0)
    sin_lo = jnp.where((dim < ROPE_HALF)[None, :], -sin, 0.0)
    sin_hi = jnp.where(((dim >= ROPE_HALF) & (dim < ROPE_DIM))[None, :], sin, 0.0)
    return jnp.concatenate([cos_t, sin_lo, sin_hi], axis=1)


def _pad_cols(w, at, n):
    return jnp.concatenate([w[..., :at], jnp.zeros(w.shape[:-1] + (n,), w.dtype), w[..., at:]], axis=-1)


def kernel(x_prompt, x_sample, state_wkv, state_shift, cache_k_win, cache_v_win, norm_mix_w, w_in, mu_shift, w0, w_decay_up, a0, w_a_up, w_g_up, k_k, k_a, r_k, ln_x_w, ln_x_b, q_norm_w, k_norm_w, sinks, w_out, norm_ffn_w, w_ffn_up, w_ffn_down):
    bsz, t, _ = x_prompt.shape
    nd = x_sample.shape[0]
    l = 0
    pad = D_LORA_PAD - D_LORA

    w_in_pad = _pad_cols(w_in[l], D_SHIFT, pad).astype(BF16)
    mu_pad = _pad_cols(mu_shift[l][None, :], D_SHIFT, pad)
    wl = jnp.zeros((D_LORA_PAD, 3 * D_RWKV), F32)
    wl = wl.at[0:32, 0:D_RWKV].set(w_decay_up[l])
    wl = wl.at[32:64, D_RWKV:2 * D_RWKV].set(w_a_up[l])
    wl = wl.at[64:160, 2 * D_RWKV:].set(w_g_up[l])
    wl = wl.astype(BF16)
    prm = jnp.zeros((16, D_RWKV), F32)
    prm = prm.at[0].set(w0[l]).at[1].set(a0[l]).at[2].set(k_k[l]).at[3].set(k_a[l])
    prm = prm.at[4].set(r_k[l].reshape(-1)).at[5].set(ln_x_w[l]).at[6].set(ln_x_b[l])
    hid = np.arange(QUAD) // HEAD_DIM
    bmask = jnp.asarray(hid[:, None] == hid[None, :], BF16)
    tri = jnp.asarray(np.tril(np.ones((CHUNK, CHUNK))), BF16)
    qw = jnp.tile(q_norm_w[l][None, :], (1, N_Q_HEADS))
    kw = jnp.tile(k_norm_w[l][None, :], (1, N_KV_HEADS))
    nmw = norm_mix_w[l][None, :]
    nfw = norm_ffn_w[l][None, :]
    wo = w_out[l].astype(BF16)
    wu = w_ffn_up[l].astype(BF16)
    wd = w_ffn_down[l].astype(BF16)
    tab_p = _rope_table(jnp.arange(t, dtype=F32) + 0)
    tab_s = jnp.tile(_rope_table(jnp.arange(1, dtype=F32) + PAST_LEN), (nd, 1))

    xp = x_prompt.reshape(bsz * t, D_MODEL)
    xs_p, plast, q_p, kv_p = _inproj_shift(xp, nmw, w_in_pad, mu_pad, t, 512)
    yr_p, hbd = _rwkv_prompt(xs_p.reshape(bsz, t, D_SHIFT_PAD), prm, wl, tri, bmask)
    y_prompt, kwin_p, vwin_p = _attn_ffn(q_p.reshape(bsz, t, D_ATTN), kv_p.reshape(bsz, t, 2 * D_KV), tab_p,
                                         qw, kw, sinks[l][None, :], bmask, _band_bias(),
                                         x_prompt, yr_p, wo, nfw, wu, wd)
    hb = hbd.reshape(bsz, 2, 4, HEAD_DIM, 2, HEAD_DIM)
    wkv_prompt = jnp.stack([hb[:, :, j, :, j % 2, :] for j in range(4)], axis=2)
    wkv_prompt = wkv_prompt.reshape(bsz, H_RWKV, HEAD_DIM, HEAD_DIM)[None]
    shift_prompt = plast[:, 0:1, :D_SHIFT][None]
    k_win_prompt = kwin_p.reshape(bsz, WINDOW, N_KV_HEADS, HEAD_DIM)[None]
    v_win_prompt = vwin_p.reshape(bsz, WINDOW, N_KV_HEADS, HEAD_DIM)[None]

    xs = x_sample.reshape(nd, D_MODEL)
    p_s, q_s, kv_s = _inproj(xs, nmw, w_in_pad, 128)
    shift_in = _pad_cols(state_shift[l].reshape(nd, D_SHIFT), D_SHIFT, pad)
    vec_t, gb, qn_s, kvn_s = _decode_prep(p_s, shift_in, q_s, kv_s, mu_pad, prm, wl, bmask, tab_s, qw, kw)
    s_t = jnp.transpose(state_wkv[l], (1, 2, 3, 0))
    ck_t = jnp.swapaxes(cache_k_win[l].reshape(nd, WINDOW, D_KV), 1, 2)
    cv_t = jnp.swapaxes(cache_v_win[l].reshape(nd, WINDOW, D_KV), 1, 2)
    q_r = qn_s.reshape(nd * N_Q_HEADS, HEAD_DIM)
    cols = jnp.swapaxes(kvn_s.reshape(nd // DEC_TILE, DEC_TILE, 2 * D_KV), 1, 2)
    wkv_t, yr_s = _decode_state(vec_t, gb, prm, bmask, s_t)
    ya_s, kc_t, vc_t = _decode_attn(q_r, kvn_s, cols, ck_t, cv_t, sinks[l][:, None])
    y_s = _outffn(xs, yr_s, ya_s, wo, nfw, wu, wd, 128)
    y_sample = y_s.reshape(nd, 1, D_MODEL)
    wkv_sample = jnp.transpose(wkv_t, (3, 0, 1, 2))[None]
    shift_sample = p_s[:, :D_SHIFT].reshape(nd, 1, D_SHIFT)[None]
    k_win_sample = jnp.swapaxes(kc_t, 1, 2).reshape(nd, WINDOW, N_KV_HEADS, HEAD_DIM)[None]
    v_win_sample = jnp.swapaxes(vc_t, 1, 2).reshape(nd, WINDOW, N_KV_HEADS, HEAD_DIM)[None]

    return (y_prompt, y_sample, wkv_prompt, shift_prompt, k_win_prompt, v_win_prompt,
            wkv_sample, shift_sample, k_win_sample, v_win_sample)
```

```python
import functools

import jax
import jax.numpy as jnp
import numpy as np
from jax import lax
from jax.experimental import pallas as pl
from jax.experimental.pallas import tpu as pltpu

F32 = jnp.float32
BF16 = jnp.bfloat16

D_MODEL = 1024
D_RWKV = 512
D_ATTN = 512
HEAD_DIM = 64
H_RWKV = 8
N_Q_HEADS = 8
N_KV_HEADS = 2
GQA_GROUP = 4
D_KV = 128
D_LORA = 160
D_LORA_PAD = 256
D_SHIFT = 3 * D_RWKV + D_LORA
D_SHIFT_PAD = 3 * D_RWKV + D_LORA_PAD
D_IN_PAD = D_SHIFT_PAD + D_ATTN + 2 * D_KV
WINDOW = 128
ROPE_DIM = 16
ROPE_HALF = 8
ROPE_THETA = 500000.0
ATTN_SCALE = HEAD_DIM ** -0.5
D_FF = 4096
RMS_EPS = 1e-6
LNX_EPS = 64e-5
NEG_INF = -1e30
LOG2E = 1.4426950408889634
PAST_LEN = 16384

CHUNK = 64
QUAD = 4 * HEAD_DIM
VMEM_LIMIT = 56 * 1024 * 1024


def _split2(x):
    hi = x.astype(BF16)
    lo = (x - hi.astype(F32)).astype(BF16)
    return hi, lo


def _head_sums(xs, ones_ref):
    n, w = xs[0].shape
    tile = min(w, QUAD)
    per = w // tile
    pieces = [x[:, j * tile:(j + 1) * tile] for x in xs for j in range(per)]
    stacked = jnp.concatenate(pieces, axis=0) if len(pieces) > 1 else pieces[0]
    ones = ones_ref[0:tile, 0:tile]
    hi, lo = _split2(stacked)
    out = jnp.dot(hi, ones, preferred_element_type=F32) + jnp.dot(lo, ones, preferred_element_type=F32)
    res = []
    for i in range(len(xs)):
        cols = [out[(i * per + j) * n:(i * per + j + 1) * n] for j in range(per)]
        res.append(jnp.concatenate(cols, axis=1) if per > 1 else cols[0])
    return res


def _cumsum_rows(tri_bf16, x):
    hi, lo = _split2(x)
    return (jnp.dot(tri_bf16, hi, preferred_element_type=F32)
            + jnp.dot(tri_bf16, lo, preferred_element_type=F32))


def _mm(a, b):
    return jnp.dot(a.astype(BF16), b.astype(BF16), preferred_element_type=F32)


def _mm_nt(a, b):
    return lax.dot_general(a.astype(BF16), b.astype(BF16), (((1,), (1,)), ((), ())),
                           preferred_element_type=F32)


def _mm_tn(a, b):
    return lax.dot_general(a.astype(BF16), b.astype(BF16), (((0,), (0,)), ((), ())),
                           preferred_element_type=F32)


def _dot_nt_f32(a, b):
    return lax.dot_general(a, b, (((1,), (1,)), ((), ())), preferred_element_type=F32)


def _sigmoid(x):
    return 1.0 / (1.0 + jnp.exp(-x))


def _softplus(x):
    return jnp.maximum(x, 0.0) + jnp.log1p(jnp.exp(-jnp.abs(x)))


def _interleave(*gens):
    live = list(gens)
    while live:
        for g in list(live):
            try:
                next(g)
            except StopIteration:
                live.remove(g)


def _norm_project(x, nw_ref, wt_ref):
    ms = jnp.mean(x * x, axis=-1, keepdims=True)
    h = (x * lax.rsqrt(ms + RMS_EPS)) * nw_ref[...]
    return _mm_nt(h, wt_ref[...])


def _inproj_kernel(x_ref, nw_ref, w_ref, p_ref, q_ref, kv_ref):
    out = _norm_project(x_ref[...], nw_ref, w_ref)
    p_ref[...] = out[:, :D_SHIFT_PAD]
    q_ref[...] = out[:, D_SHIFT_PAD:D_SHIFT_PAD + D_ATTN]
    kv_ref[...] = out[:, D_SHIFT_PAD + D_ATTN:]


def _inproj_shift_kernel(tiles_per_seq, x_ref, nw_ref, w_ref, mu_ref, xs_ref, last_ref, q_ref, kv_ref, prev_ref):
    i = pl.program_id(0)

    @pl.when(i % tiles_per_seq == 0)
    def _():
        prev_ref[...] = jnp.zeros_like(prev_ref)

    out = _norm_project(x_ref[...], nw_ref, w_ref)
    p = out[:, :D_SHIFT_PAD]
    tm = p.shape[0]
    row = lax.broadcasted_iota(jnp.int32, p.shape, 0)
    prev = jnp.where(row == 0, jnp.broadcast_to(prev_ref[0:1, :], p.shape), pltpu.roll(p, 1, 0))
    xs_ref[...] = p + (prev - p) * mu_ref[...]
    last = jnp.broadcast_to(p[tm - 1:tm, :], prev_ref.shape)
    prev_ref[...] = last
    last_ref[0] = last
    q_ref[...] = out[:, D_SHIFT_PAD:D_SHIFT_PAD + D_ATTN]
    kv_ref[...] = out[:, D_SHIFT_PAD + D_ATTN:]


def _inproj_shift(x2d, norm_w, w_in_pad, mu_pad, seq_len, tm):
    m = x2d.shape[0]
    tiles_per_seq = seq_len // tm
    return pl.pallas_call(
        functools.partial(_inproj_shift_kernel, tiles_per_seq),
        grid=(m // tm,),
        in_specs=[
            pl.BlockSpec((tm, D_MODEL), lambda i: (i, 0)),
            pl.BlockSpec((1, D_MODEL), lambda i: (0, 0)),
            pl.BlockSpec((D_IN_PAD, D_MODEL), lambda i: (0, 0)),
            pl.BlockSpec((1, D_SHIFT_PAD), lambda i: (0, 0)),
        ],
        out_specs=[
            pl.BlockSpec((tm, D_SHIFT_PAD), lambda i: (i, 0)),
            pl.BlockSpec((1, 8, D_SHIFT_PAD), lambda i: (i // tiles_per_seq, 0, 0)),
            pl.BlockSpec((tm, D_ATTN), lambda i: (i, 0)),
            pl.BlockSpec((tm, 2 * D_KV), lambda i: (i, 0)),
        ],
        out_shape=[
            jax.ShapeDtypeStruct((m, D_SHIFT_PAD), F32),
            jax.ShapeDtypeStruct((m // seq_len, 8, D_SHIFT_PAD), F32),
            jax.ShapeDtypeStruct((m, D_ATTN), F32),
            jax.ShapeDtypeStruct((m, 2 * D_KV), F32),
        ],
        scratch_shapes=[pltpu.VMEM((8, D_SHIFT_PAD), F32)],
        compiler_params=pltpu.CompilerParams(
            dimension_semantics=("arbitrary",), vmem_limit_bytes=VMEM_LIMIT),
        name="inproj_shift",
    )(x2d, norm_w, w_in_pad, mu_pad)


def _inproj(x2d, norm_w, w_in_pad, tm):
    m = x2d.shape[0]
    return pl.pallas_call(
        _inproj_kernel,
        grid=(m // tm,),
        in_specs=[
            pl.BlockSpec((tm, D_MODEL), lambda i: (i, 0)),
            pl.BlockSpec((1, D_MODEL), lambda i: (0, 0)),
            pl.BlockSpec((D_IN_PAD, D_MODEL), lambda i: (0, 0)),
        ],
        out_specs=[
            pl.BlockSpec((tm, D_SHIFT_PAD), lambda i: (i, 0)),
            pl.BlockSpec((tm, D_ATTN), lambda i: (i, 0)),
            pl.BlockSpec((tm, 2 * D_KV), lambda i: (i, 0)),
        ],
        out_shape=[
            jax.ShapeDtypeStruct((m, D_SHIFT_PAD), F32),
            jax.ShapeDtypeStruct((m, D_ATTN), F32),
            jax.ShapeDtypeStruct((m, 2 * D_KV), F32),
        ],
        compiler_params=pltpu.CompilerParams(
            dimension_semantics=("arbitrary",), vmem_limit_bytes=VMEM_LIMIT),
        name="inproj",
    )(x2d, norm_w, w_in_pad)


def _rwkv_features(xs, prm, wl_ref, ones_ref):
    r = xs[:, 0:D_RWKV]
    k = xs[:, D_RWKV:2 * D_RWKV]
    v = xs[:, 2 * D_RWKV:3 * D_RWKV]
    lora = xs[:, 3 * D_RWKV:]
    col = lax.broadcasted_iota(jnp.int32, lora.shape, 1)
    act = jnp.where(col < 32, jnp.tanh(lora), jnp.where(col < 64, lora, _sigmoid(lora)))
    up = jnp.dot(act.astype(BF16), wl_ref[...], preferred_element_type=F32)
    w0, a0, k_k, k_a, r_k = prm[0:1], prm[1:2], prm[2:3], prm[3:4], prm[4:5]
    w_log = -_softplus(-(w0 + up[:, 0:D_RWKV])) - 0.5
    logw = -jnp.exp(w_log)
    asig = _sigmoid(a0 + up[:, D_RWKV:2 * D_RWKV])
    g = up[:, 2 * D_RWKV:]
    kk = k * k_k
    k_mod = k * (1.0 + (asig - 1.0) * k_a)
    ss, rk = _head_sums([kk * kk, r * k_mod * r_k], ones_ref)
    kk = kk / jnp.maximum(jnp.sqrt(ss), 1e-12)
    k = k_mod
    bonus = rk * v
    return r, logw, k, v, -kk, kk * asig, g, bonus


def _rwkv_finish(y, g, bonus, prm, ones_ref):
    ln_w, ln_b = prm[5:6], prm[6:7]
    mean = _head_sums([y], ones_ref)[0] * (1.0 / HEAD_DIM)
    d = y - mean
    var = _head_sums([d * d], ones_ref)[0] * (1.0 / HEAD_DIM)
    yn = d * lax.rsqrt(var + LNX_EPS) * ln_w + ln_b
    return (yn + bonus) * g


def _block_diag(x, bmask):
    return jnp.concatenate([x] * 4, axis=0) * bmask


def _chunk_prep(r, logw, k, v, a, b, tri_ref, bmask):
    n = len(r)
    ch = range(n)
    tri = tri_ref[...]
    cum = [_cumsum_rows(tri, logw[i]) for i in ch]
    e_in = [jnp.exp(cum[i]) for i in ch]
    e_ex = [jnp.exp(cum[i] - logw[i]) for i in ch]
    e_inv = [1.0 / e_in[i] for i in ch]
    e_last = [e_in[i][CHUNK - 1:CHUNK, :] for i in ch]
    rt = [(r[i] * e_in[i]).astype(BF16) for i in ch]
    at = [(a[i] * e_ex[i]).astype(BF16) for i in ch]
    kt = [(k[i] * e_inv[i]).astype(BF16) for i in ch]
    bt = [(b[i] * e_inv[i]).astype(BF16) for i in ch]
    vb = [v[i].astype(BF16) for i in ch]

    t_idx = lax.broadcasted_iota(jnp.int32, (CHUNK, QUAD), 0)
    s_idx = lax.broadcasted_iota(jnp.int32, (CHUNK, QUAD), 1) & (HEAD_DIM - 1)
    strict = s_idx < t_idx
    incl = s_idx <= t_idx

    gm = [_mm_nt(jnp.concatenate([at[i], rt[i]], axis=0),
                 jnp.concatenate([_block_diag(bt[i], bmask), _block_diag(kt[i], bmask)], axis=0))
          for i in ch]
    a_ab = [jnp.where(strict, gm[i][:CHUNK, :QUAD], 0.0) for i in ch]
    a_ak = [jnp.where(strict, gm[i][:CHUNK, QUAD:], 0.0) for i in ch]
    a_rb = [jnp.where(incl, gm[i][CHUNK:, :QUAD], 0.0) for i in ch]
    a_rk = [jnp.where(incl, gm[i][CHUNK:, QUAD:], 0.0) for i in ch]

    eye = jnp.where(s_idx == t_idx, 1.0, 0.0)
    pwb = [a_ab[i].astype(BF16) for i in ch]
    t_inv = [eye + a_ab[i] for i in ch]
    for it in range(6):
        rbd = [_block_diag(pwb[i], bmask) for i in ch]
        if it == 0:
            pwb = [_mm(pwb[i], rbd[i]).astype(BF16) for i in ch]
        elif it < 5:
            out = [_mm(jnp.concatenate([pwb[i], t_inv[i].astype(BF16)], axis=0), rbd[i]) for i in ch]
            pwb = [out[i][:CHUNK].astype(BF16) for i in ch]
            t_inv = [t_inv[i] + out[i][CHUNK:] for i in ch]
        else:
            t_inv = [t_inv[i] + _mm(t_inv[i], rbd[i]) for i in ch]

    vbd = [_block_diag(vb[i], bmask) for i in ch]
    xy0 = [_mm(jnp.concatenate([a_ak[i], a_rk[i]], axis=0), vbd[i]) for i in ch]
    return [dict(ar=jnp.concatenate([at[i], rt[i]], axis=0), x0=xy0[i][:CHUNK], y0=xy0[i][CHUNK:],
                 t_inv=t_inv[i].astype(BF16), a_rb=a_rb[i].astype(BF16), vb=vb[i],
                 bk=jnp.concatenate([bt[i], kt[i]], axis=0), e_last=e_last[i]) for i in ch]


def _chunk_step(pre, state, bmask):
    ch = range(len(pre))
    half = QUAD // 2
    zeros = jnp.zeros((half, half), BF16)
    sc = [state[i].astype(BF16) for i in ch]
    sb = [jnp.concatenate([jnp.concatenate([sc[i][:half], zeros], axis=1),
                           jnp.concatenate([zeros, sc[i][half:]], axis=1)], axis=0) for i in ch]
    xr = [_mm_nt(pre[i]["ar"], sb[i]) for i in ch]
    x = [xr[i][:CHUNK] + pre[i]["x0"] for i in ch]
    u = [_mm(pre[i]["t_inv"], _block_diag(x[i].astype(BF16), bmask)) for i in ch]
    ub = [u[i].astype(BF16) for i in ch]
    y = [xr[i][CHUNK:] + pre[i]["y0"] + _mm(pre[i]["a_rb"], _block_diag(ub[i], bmask)) for i in ch]
    upd = [_mm_tn(jnp.concatenate([ub[i], pre[i]["vb"]], axis=0), pre[i]["bk"]) for i in ch]
    bm = bmask[:half, :half].astype(F32)
    s_new = []
    for i in ch:
        e_last = pre[i]["e_last"]
        top = (state[i][:half] + upd[i][:half, :half] * bm) * e_last[:, :half]
        bot = (state[i][half:] + upd[i][half:, half:] * bm) * e_last[:, half:]
        s_new.append(jnp.concatenate([top, bot], axis=0))
    return y, s_new


def _rwkv_prompt_kernel(xs_ref, prm_ref, wl_ref, tri_ref, bmask_ref, y_ref, hout_ref, h_ref):
    c = pl.program_id(0)
    nseq, tstep, _ = xs_ref.shape
    nchunk = tstep // CHUNK

    @pl.when(c == 0)
    def _():
        h_ref[...] = jnp.zeros_like(h_ref)

    xs = jnp.concatenate([xs_ref[s] for s in range(nseq)], axis=0)
    prm = prm_ref[...]
    r, logw, k, v, a, b, g, bonus = _rwkv_features(xs, prm, wl_ref, bmask_ref)
    bmask = bmask_ref[...]
    lanes = [(s, q) for s in range(nseq) for q in range(2)]
    chains = [(j, s, q) for j in range(nchunk) for s, q in lanes]
    cut = lambda x: [x[s * tstep + j * CHUNK:s * tstep + (j + 1) * CHUNK, q * QUAD:(q + 1) * QUAD]
                     for j, s, q in chains]
    pre = _chunk_prep(cut(r), cut(logw), cut(k), cut(v), cut(a), cut(b), tri_ref, bmask)
    state = [h_ref[s, q] for s, q in lanes]
    ys = []
    for j in range(nchunk):
        y_j, state = _chunk_step(pre[j * len(lanes):(j + 1) * len(lanes)], state, bmask)
        ys.append(y_j)
    for i, (s, q) in enumerate(lanes):
        h_ref[s, q] = state[i]
    rows = [jnp.concatenate(ys[j][2 * s:2 * s + 2], axis=1) for s in range(nseq) for j in range(nchunk)]
    y = _rwkv_finish(jnp.concatenate(rows, axis=0), g, bonus, prm, bmask_ref)
    for s in range(nseq):
        y_ref[s] = y[s * tstep:(s + 1) * tstep]

    @pl.when(c == pl.num_programs(0) - 1)
    def _():
        hout_ref[...] = h_ref[...]


RWKV_CHUNKS_PER_STEP = 4


def _rwkv_prompt(xs3d, prm, wl, tri, bmask):
    bsz, t, _ = xs3d.shape
    tstep = RWKV_CHUNKS_PER_STEP * CHUNK
    nc = t // tstep
    const = lambda shape: pl.BlockSpec(shape, lambda c: (0,) * len(shape))
    state_shape = (bsz, 2, QUAD, QUAD // 2)
    return pl.pallas_call(
        _rwkv_prompt_kernel,
        grid=(nc,),
        in_specs=[
            pl.BlockSpec((bsz, tstep, D_SHIFT_PAD), lambda c: (0, c, 0)),
            const((16, D_RWKV)),
            const((D_LORA_PAD, 3 * D_RWKV)),
            const((CHUNK, CHUNK)),
            const((QUAD, QUAD)),
        ],
        out_specs=[
            pl.BlockSpec((bsz, tstep, D_RWKV), lambda c: (0, c, 0)),
            const(state_shape),
        ],
        out_shape=[
            jax.ShapeDtypeStruct((bsz, t, D_RWKV), F32),
            jax.ShapeDtypeStruct(state_shape, F32),
        ],
        scratch_shapes=[pltpu.VMEM(state_shape, F32)],
        compiler_params=pltpu.CompilerParams(
            dimension_semantics=("arbitrary",), vmem_limit_bytes=VMEM_LIMIT),
        name="rwkv_prompt",
    )(xs3d, prm, wl, tri, bmask)


def _rope_lane_freq():
    inv_freq = jnp.power(ROPE_THETA, -jnp.arange(ROPE_HALF, dtype=F32) * (2.0 / ROPE_DIM))
    return inv_freq[(np.arange(2 * HEAD_DIM) % HEAD_DIM) % ROPE_HALF][None, :]


def _rope_tables(cos, sin):
    dim = lax.broadcasted_iota(jnp.int32, cos.shape, 1) & (HEAD_DIM - 1)
    cos_t = jnp.where(dim < ROPE_DIM, cos, 1.0)
    sin_lo = jnp.where(dim < ROPE_HALF, -sin, 0.0)
    sin_hi = jnp.where((dim >= ROPE_HALF) & (dim < ROPE_DIM), sin, 0.0)
    return cos_t, sin_lo, sin_hi


def _qk_norm_rope(x, norm_w, cos_t, sin_lo, sin_hi, ones_ref):
    ms = _head_sums([x * x], ones_ref)[0] * (1.0 / HEAD_DIM)
    xn = x * lax.rsqrt(ms + RMS_EPS) * norm_w
    width = x.shape[1]
    fwd = pltpu.roll(xn, width - ROPE_HALF, 1)
    bwd = pltpu.roll(xn, ROPE_HALF, 1)
    return xn * cos_t + fwd * sin_lo + bwd * sin_hi


def _tile_lanes(x, reps):
    return jnp.concatenate([x] * reps, axis=1) if reps > 1 else x


def _attn_stages(q_ref, kv_ref, taba_ref, tabb_ref, qw_ref, kw_ref, sink_ref, ones_ref, bias_ref,
                 kprev_ref, vprev_ref, out):
    nseq = q_ref.shape[0]
    seqs = range(nseq)
    ta = taba_ref[0][0:1, :]
    tb = tabb_ref[...]
    cos_a, sin_a, cos_b, sin_b = ta[:, :128], ta[:, 128:], tb[:, :128], tb[:, 128:]
    cos_t, sin_lo, sin_hi = _rope_tables(cos_a * cos_b - sin_a * sin_b, sin_a * cos_b + cos_a * sin_b)
    cos4, slo4, shi4 = _tile_lanes(cos_t, 4), _tile_lanes(sin_lo, 4), _tile_lanes(sin_hi, 4)
    q = [_qk_norm_rope(q_ref[s], qw_ref[...], cos4, slo4, shi4, ones_ref) * (ATTN_SCALE * LOG2E) for s in seqs]
    kv = [kv_ref[s] for s in seqs]
    k_cur = [_qk_norm_rope(kv[s][:, 0:D_KV], kw_ref[...], cos_t, sin_lo, sin_hi, ones_ref) for s in seqs]
    v_cur = [kv[s][:, D_KV:] for s in seqs]
    k_all = [jnp.concatenate([kprev_ref[s], k_cur[s]], axis=0) for s in seqs]
    v_all = [jnp.concatenate([vprev_ref[s], v_cur[s]], axis=0) for s in seqs]
    for s in seqs:
        kprev_ref[s] = k_cur[s]
        vprev_ref[s] = v_cur[s]
    out["k_cur"], out["v_cur"] = k_cur, v_cur
    yield

    nk = 2 * WINDOW
    bias = bias_ref[0]
    sinks = sink_ref[...] * LOG2E
    low = lax.broadcasted_iota(jnp.int32, (nk, D_KV), 1) < HEAD_DIM
    lane_blk = [ones_ref[j * HEAD_DIM:j * HEAD_DIM + 1, :] for j in range(GQA_GROUP)]

    chains = [(s, g) for s in seqs for g in range(N_KV_HEADS)]
    ch = range(len(chains))
    k_rot = [pltpu.roll(k_all[s], HEAD_DIM, 1) for s in seqs]
    k2 = [jnp.where(low, k_all[s], k_rot[s]) if g == 0 else jnp.where(low, k_rot[s], k_all[s])
          for s, g in chains]
    k4 = [jnp.concatenate([k2[c], k2[c]], axis=1).astype(BF16) for c in ch]
    vb = [v_all[s].astype(BF16) for s in seqs]
    qg = [q[s][:, g * QUAD:(g + 1) * QUAD].astype(BF16) for s, g in chains]
    qstack = [jnp.concatenate([qg[c] * lane_blk[j] for j in range(GQA_GROUP)], axis=0) for c in ch]
    sink_row = [jnp.concatenate(
        [jnp.broadcast_to(sinks[:, g * GQA_GROUP + j:g * GQA_GROUP + j + 1], (1, WINDOW))
         for j in range(GQA_GROUP)], axis=1) for s, g in chains]
    yield
    sc = [_mm_nt(k4[c], qstack[c]) + bias for c in ch]
    yield
    m = [jnp.maximum(jnp.max(sc[c], axis=0, keepdims=True), sink_row[c]) for c in ch]
    e = [jnp.exp2(sc[c] - m[c]) for c in ch]
    yield
    denom = [jnp.sum(e[c], axis=0, keepdims=True) + jnp.exp2(sink_row[c] - m[c]) for c in ch]
    ot = [_mm_tn(vb[s], e[c].astype(BF16))[g * HEAD_DIM:(g + 1) * HEAD_DIM, :] * (1.0 / denom[c])
          for c, (s, g) in enumerate(chains)]
    yield
    ya = []
    for s in seqs:
        yt = jnp.concatenate([ot[s * N_KV_HEADS + g][:, j * WINDOW:(j + 1) * WINDOW]
                              for g in range(N_KV_HEADS) for j in range(GQA_GROUP)], axis=0)
        ya.append(jnp.transpose(yt))
    out["ya"] = ya


def _ffn_stages(x, yr, ya, wo_ref, nw_ref, wu_ref, wd_ref, out, pieces=4):
    mix = jnp.concatenate([yr, ya], axis=1).astype(BF16)
    x1 = x + jnp.dot(mix, wo_ref[...], preferred_element_type=F32)
    yield
    ms = jnp.mean(x1 * x1, axis=-1, keepdims=True)
    hf = ((x1 * lax.rsqrt(ms + RMS_EPS)) * nw_ref[...]).astype(BF16)
    acc = x1
    step = D_FF // pieces
    for j in range(pieces):
        up = jnp.dot(hf, wu_ref[:, j * step:(j + 1) * step], preferred_element_type=F32)
        yield
        act = jnp.square(jnp.maximum(up, 0.0)).astype(BF16)
        acc = acc + jnp.dot(act, wd_ref[j * step:(j + 1) * step, :], preferred_element_type=F32)
        yield
    out["y"] = acc


def _attn_ffn_kernel(q_ref, kv_ref, taba_ref, tabb_ref, qw_ref, kw_ref, sink_ref, ones_ref, bias_ref,
                     x_ref, yr_ref, wo_ref, nw_ref, wu_ref, wd_ref,
                     o_ref, kwin_ref, vwin_ref, kprev_ref, vprev_ref, ya_ref):
    i = pl.program_id(0)
    nseq = q_ref.shape[0]
    seqs = range(nseq)

    @pl.when(i == 0)
    def _():
        kprev_ref[...] = jnp.zeros_like(kprev_ref)
        vprev_ref[...] = jnp.zeros_like(vprev_ref)
        ya_ref[...] = jnp.zeros_like(ya_ref)

    rows = lambda ref: jnp.concatenate([ref[s] for s in seqs], axis=0)
    a_out, f_out = {}, {}
    _interleave(
        _ffn_stages(rows(x_ref), rows(yr_ref), rows(ya_ref), wo_ref, nw_ref, wu_ref, wd_ref, f_out),
        _attn_stages(q_ref, kv_ref, taba_ref, tabb_ref, qw_ref, kw_ref, sink_ref, ones_ref, bias_ref,
                     kprev_ref, vprev_ref, a_out))
    for s in seqs:
        o_ref[s] = f_out["y"][s * WINDOW:(s + 1) * WINDOW]
        ya_ref[s] = a_out["ya"][s]

    @pl.when(i == pl.num_programs(0) - 2)
    def _():
        for s in seqs:
            kwin_ref[s] = a_out["k_cur"][s]
            vwin_ref[s] = a_out["v_cur"][s]


def _band_bias():
    ki = np.arange(2 * WINDOW)[:, None]
    qi = (np.arange(GQA_GROUP * WINDOW) % WINDOW + WINDOW)[None, :]
    dq = qi - ki
    band = (dq >= 0) & (dq < WINDOW)
    first = band & (ki >= WINDOW)
    return jnp.asarray(np.where(np.stack([first, band]), 0.0, NEG_INF), F32)


def _rope_block_tables(nb):
    freq = _rope_lane_freq()
    ang_a = (jnp.arange(nb, dtype=F32) * WINDOW)[:, None] * freq
    ang_b = jnp.arange(WINDOW, dtype=F32)[:, None] * freq
    tab_a = jnp.concatenate([jnp.cos(ang_a), jnp.sin(ang_a)], axis=1)
    tab_b = jnp.concatenate([jnp.cos(ang_b), jnp.sin(ang_b)], axis=1)
    return jnp.broadcast_to(tab_a[:, None, :], (nb, 8, 4 * HEAD_DIM)), tab_b


def _attn_ffn(q3d, kv3d, qw, kw, sinks, ones_bd, bias, x3d, yr3d, wo, nw, wu, wd):
    bsz, t, _ = q3d.shape
    nb = t // WINDOW
    const = lambda shape: pl.BlockSpec(shape, lambda i: (0,) * len(shape))
    cur = lambda i: jnp.minimum(i, nb - 1)
    prv = lambda i: jnp.maximum(i - 1, 0)
    tab_a, tab_b = _rope_block_tables(nb)
    return pl.pallas_call(
        _attn_ffn_kernel,
        grid=(nb + 1,),
        in_specs=[
            pl.BlockSpec((bsz, WINDOW, D_ATTN), lambda i: (0, cur(i), 0)),
            pl.BlockSpec((bsz, WINDOW, 2 * D_KV), lambda i: (0, cur(i), 0)),
            pl.BlockSpec((1, 8, 4 * HEAD_DIM), lambda i: (cur(i), 0, 0)),
            const((WINDOW, 4 * HEAD_DIM)),
            const((1, D_ATTN)),
            const((1, D_KV)),
            const((1, N_Q_HEADS)),
            const((QUAD, QUAD)),
            pl.BlockSpec((1, 2 * WINDOW, GQA_GROUP * WINDOW), lambda i: (jnp.minimum(i, 1), 0, 0)),
            pl.BlockSpec((bsz, WINDOW, D_MODEL), lambda i: (0, prv(i), 0)),
            pl.BlockSpec((bsz, WINDOW, D_RWKV), lambda i: (0, prv(i), 0)),
            const((D_MODEL, D_MODEL)),
            const((1, D_MODEL)),
            const((D_MODEL, D_FF)),
            const((D_FF, D_MODEL)),
        ],
        out_specs=[
            pl.BlockSpec((bsz, WINDOW, D_MODEL), lambda i: (0, prv(i), 0)),
            const((bsz, WINDOW, D_KV)),
            const((bsz, WINDOW, D_KV)),
        ],
        out_shape=[
            jax.ShapeDtypeStruct((bsz, t, D_MODEL), F32),
            jax.ShapeDtypeStruct((bsz, WINDOW, D_KV), F32),
            jax.ShapeDtypeStruct((bsz, WINDOW, D_KV), F32),
        ],
        scratch_shapes=[
            pltpu.VMEM((bsz, WINDOW, D_KV), F32),
            pltpu.VMEM((bsz, WINDOW, D_KV), F32),
            pltpu.VMEM((bsz, WINDOW, D_ATTN), F32),
        ],
        compiler_params=pltpu.CompilerParams(
            dimension_semantics=("arbitrary",), vmem_limit_bytes=VMEM_LIMIT),
        name="attn_ffn",
    )(q3d, kv3d, tab_a, tab_b, qw, kw, sinks, ones_bd, bias, x3d, yr3d, wo, nw, wu, wd)


def _outffn_kernel(x_ref, yr_ref, ya_ref, wo_ref, nw_ref, wu_ref, wd_ref, o_ref):
    out = {}
    _interleave(_ffn_stages(x_ref[...], yr_ref[...], ya_ref[...], wo_ref, nw_ref, wu_ref, wd_ref, out))
    o_ref[...] = out["y"]


def _outffn(x2d, yr, ya, wo, nw, wu, wd, tm):
    m = x2d.shape[0]
    const = lambda shape: pl.BlockSpec(shape, lambda i: (0,) * len(shape))
    return pl.pallas_call(
        _outffn_kernel,
        grid=(m // tm,),
        in_specs=[
            pl.BlockSpec((tm, D_MODEL), lambda i: (i, 0)),
            pl.BlockSpec((tm, D_RWKV), lambda i: (i, 0)),
            pl.BlockSpec((tm, D_ATTN), lambda i: (i, 0)),
            const((D_MODEL, D_MODEL)),
            const((1, D_MODEL)),
            const((D_MODEL, D_FF)),
            const((D_FF, D_MODEL)),
        ],
        out_specs=pl.BlockSpec((tm, D_MODEL), lambda i: (i, 0)),
        out_shape=jax.ShapeDtypeStruct((m, D_MODEL), F32),
        compiler_params=pltpu.CompilerParams(
            dimension_semantics=("arbitrary",), vmem_limit_bytes=VMEM_LIMIT),
        name="outffn",
    )(x2d, yr, ya, wo, nw, wu, wd)


DEC_TILE = 8


def _decode_prep_kernel(p_ref, sh_ref, q_ref, kv_ref, mu_ref, prm_ref, wl_ref, ones_ref, tab_ref,
                        qw_ref, kw_ref, vec_ref, vgb_ref, qn_ref, kvn_ref):
    p = p_ref[...]
    xs = p + (sh_ref[...] - p) * mu_ref[...]
    r, logw, k, v, a, b, g, bonus = _rwkv_features(xs, prm_ref[...], wl_ref, ones_ref)
    for i, x in enumerate((a, b, k, jnp.exp(logw), r, v)):
        vec_ref[i] = jnp.transpose(x)
    vgb_ref[0] = g
    vgb_ref[1] = bonus
    n = p.shape[0]
    tab = jnp.broadcast_to(tab_ref[0:1, :], (n, 4 * HEAD_DIM))
    cos_t, sin_lo, sin_hi = _rope_tables(tab[:, :128], tab[:, 128:])
    qn_ref[...] = _qk_norm_rope(q_ref[...], qw_ref[...], _tile_lanes(cos_t, 4), _tile_lanes(sin_lo, 4),
                                _tile_lanes(sin_hi, 4), ones_ref)
    kv = kv_ref[...]
    kvn_ref[:, 0:D_KV] = _qk_norm_rope(kv[:, 0:D_KV], kw_ref[...], cos_t, sin_lo, sin_hi, ones_ref)
    kvn_ref[:, D_KV:] = kv[:, D_KV:]


def _decode_prep(p, shift, q, kv, mu_pad, prm, wl, bmask, tab, qw, kw):
    n = p.shape[0]
    full = lambda shape: pl.BlockSpec(shape, lambda i: (0,) * len(shape))
    return pl.pallas_call(
        _decode_prep_kernel,
        grid=(1,),
        in_specs=[full((n, D_SHIFT_PAD)), full((n, D_SHIFT_PAD)), full((n, D_ATTN)), full((n, 2 * D_KV)),
                  full((1, D_SHIFT_PAD)), full((16, D_RWKV)), full((D_LORA_PAD, 3 * D_RWKV)),
                  full((QUAD, QUAD)), full((8, 4 * HEAD_DIM)), full((1, D_ATTN)), full((1, D_KV))],
        out_specs=[full((6, D_RWKV, n)), full((2, n, D_RWKV)), full((n, D_ATTN)), full((n, 2 * D_KV))],
        out_shape=[
            jax.ShapeDtypeStruct((6, D_RWKV, n), F32),
            jax.ShapeDtypeStruct((2, n, D_RWKV), F32),
            jax.ShapeDtypeStruct((n, D_ATTN), F32),
            jax.ShapeDtypeStruct((n, 2 * D_KV), F32),
        ],
        compiler_params=pltpu.CompilerParams(
            dimension_semantics=("arbitrary",), vmem_limit_bytes=VMEM_LIMIT),
        name="decode_prep",
    )(p, shift, q, kv, mu_pad, prm, wl, bmask, tab, qw, kw)


def _decode_state_kernel(vec_ref, gb_ref, prm_ref, ones_ref, s_ref, sout_ref, yr_ref, yt_ref):
    h = pl.program_id(0)
    a_t, b_t, k_t, w_t, r_t = (vec_ref[i] for i in range(5))

    def body(i, carry):
        s = s_ref[0, i]
        sa = jnp.sum(s * a_t, axis=0, keepdims=True)
        v_i = vec_ref[5, pl.ds(i, 1), :]
        s_new = s * w_t + sa * b_t + v_i * k_t
        sout_ref[0, i] = s_new
        yt_ref[pl.ds(h * HEAD_DIM + i, 1), :] = jnp.sum(s_new * r_t, axis=0, keepdims=True)
        return carry

    lax.fori_loop(0, HEAD_DIM, body, 0, unroll=8)

    @pl.when(h == pl.num_programs(0) - 1)
    def _():
        y = jnp.transpose(yt_ref[...])
        yr_ref[...] = _rwkv_finish(y, gb_ref[0], gb_ref[1], prm_ref[...], ones_ref)


def _decode_state(vec_t, gb, prm, bmask, s_t):
    n = s_t.shape[-1]
    const = lambda shape: pl.BlockSpec(shape, lambda h: (0,) * len(shape))
    return pl.pallas_call(
        _decode_state_kernel,
        grid=(H_RWKV,),
        in_specs=[
            pl.BlockSpec((6, HEAD_DIM, n), lambda h: (0, h, 0)),
            const((2, n, D_RWKV)),
            const((16, D_RWKV)),
            const((QUAD, QUAD)),
            pl.BlockSpec((1, HEAD_DIM, HEAD_DIM, n), lambda h: (h, 0, 0, 0)),
        ],
        out_specs=[
            pl.BlockSpec((1, HEAD_DIM, HEAD_DIM, n), lambda h: (h, 0, 0, 0)),
            const((n, D_RWKV)),
        ],
        out_shape=[
            jax.ShapeDtypeStruct((H_RWKV, HEAD_DIM, HEAD_DIM, n), F32),
            jax.ShapeDtypeStruct((n, D_RWKV), F32),
        ],
        scratch_shapes=[pltpu.VMEM((D_RWKV, n), F32)],
        compiler_params=pltpu.CompilerParams(
            dimension_semantics=("arbitrary",), vmem_limit_bytes=VMEM_LIMIT),
        name="decode_state",
    )(vec_t, gb, prm, bmask, s_t)


def _decode_attn_kernel(qr_ref, kvn_ref, col_ref, ck_ref, cv_ref, sink_ref, ya_ref, kout_ref, vout_ref):
    nh = N_Q_HEADS
    seqs = range(DEC_TILE)
    hrow = lax.broadcasted_iota(jnp.int32, (nh, D_ATTN), 0)
    hlane = lax.broadcasted_iota(jnp.int32, (nh, D_ATTN), 1) // HEAD_DIM
    dmask = hrow == hlane
    grow = lax.broadcasted_iota(jnp.int32, (nh, D_KV), 0) // GQA_GROUP
    glane = lax.broadcasted_iota(jnp.int32, (nh, D_KV), 1) // HEAD_DIM
    gmask = grow == glane
    low = glane == 0
    key_idx = lax.broadcasted_iota(jnp.int32, (nh, WINDOW), 1)
    last = lax.broadcasted_iota(jnp.int32, (D_KV, WINDOW), 1) == WINDOW - 1
    sink = sink_ref[...]
    kvn = kvn_ref[...]
    col = col_ref[0]
    k_new = [kvn[j:j + 1, 0:D_KV] for j in seqs]
    v_new = [kvn[j:j + 1, D_KV:] for j in seqs]
    ck = [ck_ref[j] for j in seqs]
    cv = [cv_ref[j] for j in seqs]
    for j in seqs:
        kout_ref[j] = jnp.where(last, col[0:D_KV, j:j + 1], pltpu.roll(ck[j], WINDOW - 1, 1))
        vout_ref[j] = jnp.where(last, col[D_KV:, j:j + 1], pltpu.roll(cv[j], WINDOW - 1, 1))
    q8 = [qr_ref[j * nh:(j + 1) * nh, :] for j in seqs]
    qp = [jnp.where(gmask, jnp.concatenate([q8[j], q8[j]], axis=1), 0.0) for j in seqs]
    s_c = [jnp.where(key_idx >= 1, jnp.dot(qp[j], ck[j], preferred_element_type=F32) * ATTN_SCALE, NEG_INF)
           for j in seqs]
    s_n = [jnp.sum(qp[j] * k_new[j], axis=-1, keepdims=True) * ATTN_SCALE for j in seqs]
    m = [jnp.maximum(jnp.maximum(jnp.max(s_c[j], axis=-1, keepdims=True), s_n[j]), sink) for j in seqs]
    e_c = [jnp.exp(s_c[j] - m[j]) for j in seqs]
    e_n = [jnp.exp(s_n[j] - m[j]) for j in seqs]
    denom = [jnp.sum(e_c[j], axis=-1, keepdims=True) + e_n[j] + jnp.exp(sink - m[j]) for j in seqs]
    o = [(_dot_nt_f32(e_c[j], cv[j]) + e_n[j] * v_new[j]) / denom[j]
         for j in seqs]
    out_rows = []
    for j in seqs:
        rot = pltpu.roll(o[j], HEAD_DIM, 1)
        g0 = jnp.where(low, o[j], rot)
        g1 = jnp.where(low, rot, o[j])
        wide = jnp.concatenate([g0, g0, g1, g1], axis=1)
        out_rows.append(jnp.sum(jnp.where(dmask, wide, 0.0), axis=0, keepdims=True))
    ya_ref[...] = jnp.concatenate(out_rows, axis=0)


def _decode_attn(q_r, kvn, cols, ck_t, cv_t, sinks_col):
    n = kvn.shape[0]
    bt = DEC_TILE
    const = lambda shape: pl.BlockSpec(shape, lambda i: (0,) * len(shape))
    return pl.pallas_call(
        _decode_attn_kernel,
        grid=(n // bt,),
        in_specs=[
            pl.BlockSpec((bt * N_Q_HEADS, HEAD_DIM), lambda i: (i, 0)),
            pl.BlockSpec((bt, 2 * D_KV), lambda i: (i, 0)),
            pl.BlockSpec((1, 2 * D_KV, bt), lambda i: (i, 0, 0)),
            pl.BlockSpec((bt, D_KV, WINDOW), lambda i: (i, 0, 0)),
            pl.BlockSpec((bt, D_KV, WINDOW), lambda i: (i, 0, 0)),
            const((N_Q_HEADS, 1)),
        ],
        out_specs=[
            pl.BlockSpec((bt, D_ATTN), lambda i: (i, 0)),
            pl.BlockSpec((bt, D_KV, WINDOW), lambda i: (i, 0, 0)),
            pl.BlockSpec((bt, D_KV, WINDOW), lambda i: (i, 0, 0)),
        ],
        out_shape=[
            jax.ShapeDtypeStruct((n, D_ATTN), F32),
            jax.ShapeDtypeStruct((n, D_KV, WINDOW), F32),
            jax.ShapeDtypeStruct((n, D_KV, WINDOW), F32),
        ],
        compiler_params=pltpu.CompilerParams(
            dimension_semantics=("arbitrary",), vmem_limit_bytes=VMEM_LIMIT),
        name="decode_attn",
    )(q_r, kvn, cols, ck_t, cv_t, sinks_col)


def _pad_cols(w, at, n):
    return jnp.concatenate([w[..., :at], jnp.zeros(w.shape[:-1] + (n,), w.dtype), w[..., at:]], axis=-1)


def kernel(x_prompt, x_sample, state_wkv, state_shift, cache_k_win, cache_v_win, norm_mix_w, w_in, mu_shift, w0, w_decay_up, a0, w_a_up, w_g_up, k_k, k_a, r_k, ln_x_w, ln_x_b, q_norm_w, k_norm_w, sinks, w_out, norm_ffn_w, w_ffn_up, w_ffn_down):
    bsz, t, _ = x_prompt.shape
    nd = x_sample.shape[0]
    l = 0
    pad = D_LORA_PAD - D_LORA

    w_in_t = jnp.swapaxes(w_in[l], 0, 1)
    w_in_pad = jnp.concatenate([w_in_t[:D_SHIFT], jnp.zeros((pad, D_MODEL), F32), w_in_t[D_SHIFT:]],
                               axis=0).astype(BF16)
    mu_pad = _pad_cols(mu_shift[l][None, :], D_SHIFT, pad)
    wl = jnp.zeros((D_LORA_PAD, 3 * D_RWKV), F32)
    wl = wl.at[0:32, 0:D_RWKV].set(w_decay_up[l])
    wl = wl.at[32:64, D_RWKV:2 * D_RWKV].set(w_a_up[l])
    wl = wl.at[64:160, 2 * D_RWKV:].set(w_g_up[l])
    wl = wl.astype(BF16)
    prm = jnp.zeros((16, D_RWKV), F32)
    prm = prm.at[0].set(w0[l]).at[1].set(a0[l]).at[2].set(k_k[l]).at[3].set(k_a[l])
    prm = prm.at[4].set(r_k[l].reshape(-1)).at[5].set(ln_x_w[l]).at[6].set(ln_x_b[l])
    hid = np.arange(QUAD) // HEAD_DIM
    bmask = jnp.asarray(hid[:, None] == hid[None, :], BF16)
    tri = jnp.asarray(np.tril(np.ones((CHUNK, CHUNK))), BF16)
    qw = jnp.tile(q_norm_w[l][None, :], (1, N_Q_HEADS))
    kw = jnp.tile(k_norm_w[l][None, :], (1, N_KV_HEADS))
    nmw = norm_mix_w[l][None, :]
    nfw = norm_ffn_w[l][None, :]
    wo = w_out[l].astype(BF16)
    wu = w_ffn_up[l].astype(BF16)
    wd = w_ffn_down[l].astype(BF16)
    ang_s = (jnp.zeros((8, 1), F32) + PAST_LEN) * _rope_lane_freq()
    tab_s = jnp.concatenate([jnp.cos(ang_s), jnp.sin(ang_s)], axis=1)

    xp = x_prompt.reshape(bsz * t, D_MODEL)
    xs_p, plast, q_p, kv_p = _inproj_shift(xp, nmw, w_in_pad, mu_pad, t, 512)
    yr_p, hbd = _rwkv_prompt(xs_p.reshape(bsz, t, D_SHIFT_PAD), prm, wl, tri, bmask)
    y_prompt, kwin_p, vwin_p = _attn_ffn(q_p.reshape(bsz, t, D_ATTN), kv_p.reshape(bsz, t, 2 * D_KV),
                                         qw, kw, sinks[l][None, :], bmask, _band_bias(),
                                         x_prompt, yr_p, wo, nfw, wu, wd)
    hb = hbd.reshape(bsz, 2, 4, HEAD_DIM, 2, HEAD_DIM)
    wkv_prompt = jnp.stack([hb[:, :, j, :, j % 2, :] for j in range(4)], axis=2)
    wkv_prompt = wkv_prompt.reshape(bsz, H_RWKV, HEAD_DIM, HEAD_DIM)[None]
    shift_prompt = plast[:, 0:1, :D_SHIFT][None]
    k_win_prompt = kwin_p.reshape(bsz, WINDOW, N_KV_HEADS, HEAD_DIM)[None]
    v_win_prompt = vwin_p.reshape(bsz, WINDOW, N_KV_HEADS, HEAD_DIM)[None]

    xs = x_sample.reshape(nd, D_MODEL)
    p_s, q_s, kv_s = _inproj(xs, nmw, w_in_pad, 128)
    shift_in = _pad_cols(state_shift[l].reshape(nd, D_SHIFT), D_SHIFT, pad)
    vec_t, gb, qn_s, kvn_s = _decode_prep(p_s, shift_in, q_s, kv_s, mu_pad, prm, wl, bmask, tab_s, qw, kw)
    s_t = jnp.transpose(state_wkv[l], (1, 2, 3, 0))
    ck_t = jnp.swapaxes(cache_k_win[l].reshape(nd, WINDOW, D_KV), 1, 2)
    cv_t = jnp.swapaxes(cache_v_win[l].reshape(nd, WINDOW, D_KV), 1, 2)
    q_r = qn_s.reshape(nd * N_Q_HEADS, HEAD_DIM)
    cols = jnp.swapaxes(kvn_s.reshape(nd // DEC_TILE, DEC_TILE, 2 * D_KV), 1, 2)
    wkv_t, yr_s = _decode_state(vec_t, gb, prm, bmask, s_t)
    ya_s, kc_t, vc_t = _decode_attn(q_r, kvn_s, cols, ck_t, cv_t, sinks[l][:, None])
    y_s = _outffn(xs, yr_s, ya_s, wo, nfw, wu, wd, 128)
    y_sample = y_s.reshape(nd, 1, D_MODEL)
    wkv_sample = jnp.transpose(wkv_t, (3, 0, 1, 2))[None]
    shift_sample = p_s[:, :D_SHIFT].reshape(nd, 1, D_SHIFT)[None]
    k_win_sample = jnp.swapaxes(kc_t, 1, 2).reshape(nd, WINDOW, N_KV_HEADS, HEAD_DIM)[None]
    v_win_sample = jnp.swapaxes(vc_t, 1, 2).reshape(nd, WINDOW, N_KV_HEADS, HEAD_DIM)[None]

    return (y_prompt, y_sample, wkv_prompt, shift_prompt, k_win_prompt, v_win_prompt,
            wkv_sample, shift_sample, k_win_sample, v_win_sample)
```

```python
import functools

import jax
import jax.numpy as jnp
import numpy as np
from jax import lax
from jax.experimental import pallas as pl
from jax.experimental.pallas import tpu as pltpu

F32 = jnp.float32
BF16 = jnp.bfloat16

D_MODEL = 1024
D_RWKV = 512
D_ATTN = 512
HEAD_DIM = 64
H_RWKV = 8
N_Q_HEADS = 8
N_KV_HEADS = 2
GQA_GROUP = 4
D_KV = 128
D_LORA = 160
D_LORA_PAD = 256
D_SHIFT = 3 * D_RWKV + D_LORA
D_SHIFT_PAD = 3 * D_RWKV + D_LORA_PAD
D_IN_PAD = D_SHIFT_PAD + D_ATTN + 2 * D_KV
WINDOW = 128
ROPE_DIM = 16
ROPE_HALF = 8
ROPE_THETA = 500000.0
ATTN_SCALE = HEAD_DIM ** -0.5
D_FF = 4096
RMS_EPS = 1e-6
LNX_EPS = 64e-5
NEG_INF = -1e30
LOG2E = 1.4426950408889634
PAST_LEN = 16384

CHUNK = 64
QUAD = 4 * HEAD_DIM
VMEM_LIMIT = 56 * 1024 * 1024


def _split2(x):
    hi = x.astype(BF16)
    lo = (x - hi.astype(F32)).astype(BF16)
    return hi, lo


def _head_sums(xs, ones_ref):
    n, w = xs[0].shape
    tile = min(w, QUAD)
    per = w // tile
    pieces = [x[:, j * tile:(j + 1) * tile] for x in xs for j in range(per)]
    stacked = jnp.concatenate(pieces, axis=0) if len(pieces) > 1 else pieces[0]
    ones = ones_ref[0:tile, 0:tile]
    hi, lo = _split2(stacked)
    out = jnp.dot(hi, ones, preferred_element_type=F32) + jnp.dot(lo, ones, preferred_element_type=F32)
    res = []
    for i in range(len(xs)):
        cols = [out[(i * per + j) * n:(i * per + j + 1) * n] for j in range(per)]
        res.append(jnp.concatenate(cols, axis=1) if per > 1 else cols[0])
    return res


def _cumsum_rows(tri_bf16, x):
    hi, lo = _split2(x)
    return (jnp.dot(tri_bf16, hi, preferred_element_type=F32)
            + jnp.dot(tri_bf16, lo, preferred_element_type=F32))


def _mm(a, b):
    return jnp.dot(a.astype(BF16), b.astype(BF16), preferred_element_type=F32)


def _mm_nt(a, b):
    return lax.dot_general(a.astype(BF16), b.astype(BF16), (((1,), (1,)), ((), ())),
                           preferred_element_type=F32)


def _mm_tn(a, b):
    return lax.dot_general(a.astype(BF16), b.astype(BF16), (((0,), (0,)), ((), ())),
                           preferred_element_type=F32)


def _dot_nt_f32(a, b):
    return lax.dot_general(a, b, (((1,), (1,)), ((), ())), preferred_element_type=F32)


def _sigmoid(x):
    return 1.0 / (1.0 + jnp.exp(-x))


def _interleave(*gens):
    live = list(gens)
    while live:
        for g in list(live):
            try:
                next(g)
            except StopIteration:
                live.remove(g)


def _norm_project(x, nw_ref, wt_ref):
    ms = jnp.mean(x * x, axis=-1, keepdims=True)
    h = (x * lax.rsqrt(ms + RMS_EPS)) * nw_ref[...]
    return _mm_nt(h, wt_ref[...])


def _inproj_kernel(x_ref, nw_ref, w_ref, p_ref, q_ref, kv_ref):
    out = _norm_project(x_ref[...], nw_ref, w_ref)
    p_ref[...] = out[:, :D_SHIFT_PAD]
    q_ref[...] = out[:, D_SHIFT_PAD:D_SHIFT_PAD + D_ATTN]
    kv_ref[...] = out[:, D_SHIFT_PAD + D_ATTN:]


def _inproj_shift_kernel(tiles_per_seq, x_ref, nw_ref, w_ref, mu_ref, xs_ref, last_ref, q_ref, kv_ref, prev_ref):
    i = pl.program_id(0)

    @pl.when(i % tiles_per_seq == 0)
    def _():
        prev_ref[...] = jnp.zeros_like(prev_ref)

    out = _norm_project(x_ref[...], nw_ref, w_ref)
    p = out[:, :D_SHIFT_PAD]
    tm = p.shape[0]
    row = lax.broadcasted_iota(jnp.int32, p.shape, 0)
    prev = jnp.where(row == 0, jnp.broadcast_to(prev_ref[0:1, :], p.shape), pltpu.roll(p, 1, 0))
    xs_ref[...] = p + (prev - p) * mu_ref[...]
    last = jnp.broadcast_to(p[tm - 1:tm, :], prev_ref.shape)
    prev_ref[...] = last
    last_ref[0] = last
    q_ref[...] = out[:, D_SHIFT_PAD:D_SHIFT_PAD + D_ATTN]
    kv_ref[...] = out[:, D_SHIFT_PAD + D_ATTN:]


def _inproj_shift(x2d, norm_w, w_in_pad, mu_pad, seq_len, tm):
    m = x2d.shape[0]
    tiles_per_seq = seq_len // tm
    return pl.pallas_call(
        functools.partial(_inproj_shift_kernel, tiles_per_seq),
        grid=(m // tm,),
        in_specs=[
            pl.BlockSpec((tm, D_MODEL), lambda i: (i, 0)),
            pl.BlockSpec((1, D_MODEL), lambda i: (0, 0)),
            pl.BlockSpec((D_IN_PAD, D_MODEL), lambda i: (0, 0)),
            pl.BlockSpec((1, D_SHIFT_PAD), lambda i: (0, 0)),
        ],
        out_specs=[
            pl.BlockSpec((tm, D_SHIFT_PAD), lambda i: (i, 0)),
            pl.BlockSpec((1, 8, D_SHIFT_PAD), lambda i: (i // tiles_per_seq, 0, 0)),
            pl.BlockSpec((tm, D_ATTN), lambda i: (i, 0)),
            pl.BlockSpec((tm, 2 * D_KV), lambda i: (i, 0)),
        ],
        out_shape=[
            jax.ShapeDtypeStruct((m, D_SHIFT_PAD), F32),
            jax.ShapeDtypeStruct((m // seq_len, 8, D_SHIFT_PAD), F32),
            jax.ShapeDtypeStruct((m, D_ATTN), F32),
            jax.ShapeDtypeStruct((m, 2 * D_KV), F32),
        ],
        scratch_shapes=[pltpu.VMEM((8, D_SHIFT_PAD), F32)],
        compiler_params=pltpu.CompilerParams(
            dimension_semantics=("arbitrary",), vmem_limit_bytes=VMEM_LIMIT),
        name="inproj_shift",
    )(x2d, norm_w, w_in_pad, mu_pad)


def _inproj(x2d, norm_w, w_in_pad, tm):
    m = x2d.shape[0]
    return pl.pallas_call(
        _inproj_kernel,
        grid=(m // tm,),
        in_specs=[
            pl.BlockSpec((tm, D_MODEL), lambda i: (i, 0)),
            pl.BlockSpec((1, D_MODEL), lambda i: (0, 0)),
            pl.BlockSpec((D_IN_PAD, D_MODEL), lambda i: (0, 0)),
        ],
        out_specs=[
            pl.BlockSpec((tm, D_SHIFT_PAD), lambda i: (i, 0)),
            pl.BlockSpec((tm, D_ATTN), lambda i: (i, 0)),
            pl.BlockSpec((tm, 2 * D_KV), lambda i: (i, 0)),
        ],
        out_shape=[
            jax.ShapeDtypeStruct((m, D_SHIFT_PAD), F32),
            jax.ShapeDtypeStruct((m, D_ATTN), F32),
            jax.ShapeDtypeStruct((m, 2 * D_KV), F32),
        ],
        compiler_params=pltpu.CompilerParams(
            dimension_semantics=("arbitrary",), vmem_limit_bytes=VMEM_LIMIT),
        name="inproj",
    )(x2d, norm_w, w_in_pad)


def _rwkv_features(xs, prm, wl_ref, ones_ref):
    r = xs[:, 0:D_RWKV]
    k = xs[:, D_RWKV:2 * D_RWKV]
    v = xs[:, 2 * D_RWKV:3 * D_RWKV]
    lora = xs[:, 3 * D_RWKV:]
    col = lax.broadcasted_iota(jnp.int32, lora.shape, 1)
    act = jnp.where(col < 32, jnp.tanh(lora), jnp.where(col < 64, lora, _sigmoid(lora)))
    up = jnp.dot(act.astype(BF16), wl_ref[...], preferred_element_type=F32)
    w0, a0, k_k, k_a, r_k = prm[0:1], prm[1:2], prm[2:3], prm[3:4], prm[4:5]
    logw = (-np.exp(-0.5)) * _sigmoid(w0 + up[:, 0:D_RWKV])
    asig = _sigmoid(a0 + up[:, D_RWKV:2 * D_RWKV])
    g = up[:, 2 * D_RWKV:]
    kk = k * k_k
    k_mod = k * (1.0 + (asig - 1.0) * k_a)
    ss, rk = _head_sums([kk * kk, r * k_mod * r_k], ones_ref)
    kk = kk / jnp.maximum(jnp.sqrt(ss), 1e-12)
    k = k_mod
    bonus = rk * v
    return r, logw, k, v, -kk, kk * asig, g, bonus


def _rwkv_finish(y, g, bonus, prm, ones_ref):
    ln_w, ln_b = prm[5:6], prm[6:7]
    mean = _head_sums([y], ones_ref)[0] * (1.0 / HEAD_DIM)
    d = y - mean
    var = _head_sums([d * d], ones_ref)[0] * (1.0 / HEAD_DIM)
    yn = d * lax.rsqrt(var + LNX_EPS) * ln_w + ln_b
    return (yn + bonus) * g


def _block_diag(x, bmask):
    return jnp.concatenate([x] * 4, axis=0) * bmask


def _chunk_prep(r, logw, k, v, a, b, tri_ref, bmask):
    n = len(r)
    ch = range(n)
    tri = tri_ref[...]
    cum = [_cumsum_rows(tri, logw[i]) for i in ch]
    e_in = [jnp.exp(cum[i]) for i in ch]
    e_ex = [jnp.exp(cum[i] - logw[i]) for i in ch]
    e_inv = [1.0 / e_in[i] for i in ch]
    e_last = [e_in[i][CHUNK - 1:CHUNK, :] for i in ch]
    rt = [(r[i] * e_in[i]).astype(BF16) for i in ch]
    at = [(a[i] * e_ex[i]).astype(BF16) for i in ch]
    kt = [(k[i] * e_inv[i]).astype(BF16) for i in ch]
    bt = [(b[i] * e_inv[i]).astype(BF16) for i in ch]
    vb = [v[i].astype(BF16) for i in ch]

    t_idx = lax.broadcasted_iota(jnp.int32, (CHUNK, QUAD), 0)
    s_idx = lax.broadcasted_iota(jnp.int32, (CHUNK, QUAD), 1) & (HEAD_DIM - 1)
    strict = s_idx < t_idx
    incl = s_idx <= t_idx

    gm = [_mm_nt(jnp.concatenate([at[i], rt[i]], axis=0),
                 jnp.concatenate([_block_diag(bt[i], bmask), _block_diag(kt[i], bmask)], axis=0))
          for i in ch]
    a_ab = [jnp.where(strict, gm[i][:CHUNK, :QUAD], 0.0) for i in ch]
    a_ak = [jnp.where(strict, gm[i][:CHUNK, QUAD:], 0.0) for i in ch]
    a_rb = [jnp.where(incl, gm[i][CHUNK:, :QUAD], 0.0) for i in ch]
    a_rk = [jnp.where(incl, gm[i][CHUNK:, QUAD:], 0.0) for i in ch]

    eye = jnp.where(s_idx == t_idx, 1.0, 0.0)
    pwb = [a_ab[i].astype(BF16) for i in ch]
    t_inv = [eye + a_ab[i] for i in ch]
    for it in range(6):
        rbd = [_block_diag(pwb[i], bmask) for i in ch]
        if it == 0:
            pwb = [_mm(pwb[i], rbd[i]).astype(BF16) for i in ch]
        elif it < 5:
            out = [_mm(jnp.concatenate([pwb[i], t_inv[i].astype(BF16)], axis=0), rbd[i]) for i in ch]
            pwb = [out[i][:CHUNK].astype(BF16) for i in ch]
            t_inv = [t_inv[i] + out[i][CHUNK:] for i in ch]
        else:
            t_inv = [t_inv[i] + _mm(t_inv[i], rbd[i]) for i in ch]

    vbd = [_block_diag(vb[i], bmask) for i in ch]
    xy0 = [_mm(jnp.concatenate([a_ak[i], a_rk[i]], axis=0), vbd[i]) for i in ch]
    return [dict(ar=jnp.concatenate([at[i], rt[i]], axis=0), x0=xy0[i][:CHUNK], y0=xy0[i][CHUNK:],
                 t_inv=t_inv[i].astype(BF16), a_rb=a_rb[i].astype(BF16), vb=vb[i],
                 bk=jnp.concatenate([bt[i], kt[i]], axis=0), e_last=e_last[i]) for i in ch]


def _chunk_step(pre, state, bmask):
    ch = range(len(pre))
    half = QUAD // 2
    zeros = jnp.zeros((half, half), BF16)
    sc = [state[i].astype(BF16) for i in ch]
    sb = [jnp.concatenate([jnp.concatenate([sc[i][:half], zeros], axis=1),
                           jnp.concatenate([zeros, sc[i][half:]], axis=1)], axis=0) for i in ch]
    xr = [_mm_nt(pre[i]["ar"], sb[i]) for i in ch]
    x = [xr[i][:CHUNK] + pre[i]["x0"] for i in ch]
    u = [_mm(pre[i]["t_inv"], _block_diag(x[i].astype(BF16), bmask)) for i in ch]
    ub = [u[i].astype(BF16) for i in ch]
    y = [xr[i][CHUNK:] + pre[i]["y0"] + _mm(pre[i]["a_rb"], _block_diag(ub[i], bmask)) for i in ch]
    upd = [_mm_tn(jnp.concatenate([ub[i], pre[i]["vb"]], axis=0), pre[i]["bk"]) for i in ch]
    bm = bmask[:half, :half].astype(F32)
    s_new = []
    for i in ch:
        e_last = pre[i]["e_last"]
        top = (state[i][:half] + upd[i][:half, :half] * bm) * e_last[:, :half]
        bot = (state[i][half:] + upd[i][half:, half:] * bm) * e_last[:, half:]
        s_new.append(jnp.concatenate([top, bot], axis=0))
    return y, s_new


def _rwkv_prompt_kernel(xs_ref, prm_ref, wl_ref, tri_ref, bmask_ref, y_ref, hout_ref, h_ref):
    c = pl.program_id(0)
    nseq, tstep, _ = xs_ref.shape
    nchunk = tstep // CHUNK

    @pl.when(c == 0)
    def _():
        h_ref[...] = jnp.zeros_like(h_ref)

    xs = jnp.concatenate([xs_ref[s] for s in range(nseq)], axis=0)
    prm = prm_ref[...]
    r, logw, k, v, a, b, g, bonus = _rwkv_features(xs, prm, wl_ref, bmask_ref)
    bmask = bmask_ref[...]
    lanes = [(s, q) for s in range(nseq) for q in range(2)]
    chains = [(j, s, q) for j in range(nchunk) for s, q in lanes]
    cut = lambda x: [x[s * tstep + j * CHUNK:s * tstep + (j + 1) * CHUNK, q * QUAD:(q + 1) * QUAD]
                     for j, s, q in chains]
    pre = _chunk_prep(cut(r), cut(logw), cut(k), cut(v), cut(a), cut(b), tri_ref, bmask)
    state = [h_ref[s, q] for s, q in lanes]
    ys = []
    for j in range(nchunk):
        y_j, state = _chunk_step(pre[j * len(lanes):(j + 1) * len(lanes)], state, bmask)
        ys.append(y_j)
    for i, (s, q) in enumerate(lanes):
        h_ref[s, q] = state[i]
    rows = [jnp.concatenate(ys[j][2 * s:2 * s + 2], axis=1) for s in range(nseq) for j in range(nchunk)]
    y = _rwkv_finish(jnp.concatenate(rows, axis=0), g, bonus, prm, bmask_ref)
    for s in range(nseq):
        y_ref[s] = y[s * tstep:(s + 1) * tstep]

    @pl.when(c == pl.num_programs(0) - 1)
    def _():
        hout_ref[...] = h_ref[...]


RWKV_CHUNKS_PER_STEP = 4


def _rwkv_prompt(xs3d, prm, wl, tri, bmask):
    bsz, t, _ = xs3d.shape
    tstep = RWKV_CHUNKS_PER_STEP * CHUNK
    nc = t // tstep
    const = lambda shape: pl.BlockSpec(shape, lambda c: (0,) * len(shape))
    state_shape = (bsz, 2, QUAD, QUAD // 2)
    return pl.pallas_call(
        _rwkv_prompt_kernel,
        grid=(nc,),
        in_specs=[
            pl.BlockSpec((bsz, tstep, D_SHIFT_PAD), lambda c: (0, c, 0)),
            const((16, D_RWKV)),
            const((D_LORA_PAD, 3 * D_RWKV)),
            const((CHUNK, CHUNK)),
            const((QUAD, QUAD)),
        ],
        out_specs=[
            pl.BlockSpec((bsz, tstep, D_RWKV), lambda c: (0, c, 0)),
            const(state_shape),
        ],
        out_shape=[
            jax.ShapeDtypeStruct((bsz, t, D_RWKV), F32),
            jax.ShapeDtypeStruct(state_shape, F32),
        ],
        scratch_shapes=[pltpu.VMEM(state_shape, F32)],
        compiler_params=pltpu.CompilerParams(
            dimension_semantics=("arbitrary",), vmem_limit_bytes=VMEM_LIMIT),
        name="rwkv_prompt",
    )(xs3d, prm, wl, tri, bmask)


def _rope_lane_freq():
    inv_freq = jnp.power(ROPE_THETA, -jnp.arange(ROPE_HALF, dtype=F32) * (2.0 / ROPE_DIM))
    return inv_freq[(np.arange(2 * HEAD_DIM) % HEAD_DIM) % ROPE_HALF][None, :]


def _rope_tables(cos, sin):
    dim = lax.broadcasted_iota(jnp.int32, cos.shape, 1) & (HEAD_DIM - 1)
    cos_t = jnp.where(dim < ROPE_DIM, cos, 1.0)
    sin_lo = jnp.where(dim < ROPE_HALF, -sin, 0.0)
    sin_hi = jnp.where((dim >= ROPE_HALF) & (dim < ROPE_DIM), sin, 0.0)
    return cos_t, sin_lo, sin_hi


def _qk_norm_rope(x, norm_w, cos_t, sin_lo, sin_hi, ones_ref):
    ms = _head_sums([x * x], ones_ref)[0] * (1.0 / HEAD_DIM)
    xn = x * lax.rsqrt(ms + RMS_EPS) * norm_w
    width = x.shape[1]
    fwd = pltpu.roll(xn, width - ROPE_HALF, 1)
    bwd = pltpu.roll(xn, ROPE_HALF, 1)
    return xn * cos_t + fwd * sin_lo + bwd * sin_hi


def _tile_lanes(x, reps):
    return jnp.concatenate([x] * reps, axis=1) if reps > 1 else x


def _attn_stages(q_ref, kv_ref, taba_ref, tabb_ref, qw_ref, kw_ref, sink_ref, ones_ref, bias_ref, first_bias,
                 kprev_ref, vprev_ref, out):
    nseq, tq, _ = q_ref.shape
    nblk = tq // WINDOW
    units = [(s, b) for s in range(nseq) for b in range(nblk)]
    blk = lambda b: slice(b * WINDOW, (b + 1) * WINDOW)
    tb = tabb_ref[...]
    cos_b, sin_b = tb[:, :128], tb[:, 128:]
    rope, rope4 = [], []
    for b in range(nblk):
        ta = taba_ref[b][0:1, :]
        cos_a, sin_a = ta[:, :128], ta[:, 128:]
        tabs = _rope_tables(cos_a * cos_b - sin_a * sin_b, sin_a * cos_b + cos_a * sin_b)
        rope.append(tabs)
        rope4.append([_tile_lanes(x, 4) for x in tabs])
    q = {(s, b): _qk_norm_rope(q_ref[s, blk(b), :], qw_ref[...], *rope4[b], ones_ref) * (ATTN_SCALE * LOG2E)
         for s, b in units}
    kv = {(s, b): kv_ref[s, blk(b), :] for s, b in units}
    k_cur = {u: _qk_norm_rope(kv[u][:, 0:D_KV], kw_ref[...], *rope[u[1]], ones_ref) for u in units}
    v_cur = {u: kv[u][:, D_KV:] for u in units}
    k_all = {(s, b): jnp.concatenate([kprev_ref[s] if b == 0 else k_cur[(s, b - 1)], k_cur[(s, b)]], axis=0)
             for s, b in units}
    v_all = {(s, b): jnp.concatenate([vprev_ref[s] if b == 0 else v_cur[(s, b - 1)], v_cur[(s, b)]], axis=0)
             for s, b in units}
    out["k_cur"] = [k_cur[(s, nblk - 1)] for s in range(nseq)]
    out["v_cur"] = [v_cur[(s, nblk - 1)] for s in range(nseq)]
    for s in range(nseq):
        kprev_ref[s] = out["k_cur"][s]
        vprev_ref[s] = out["v_cur"][s]
    yield

    nk = 2 * WINDOW
    bias = [bias_ref[first_bias] if b == 0 else bias_ref[1] for b in range(nblk)]
    sinks = sink_ref[...] * LOG2E
    low = lax.broadcasted_iota(jnp.int32, (nk, D_KV), 1) < HEAD_DIM
    lane_blk = [ones_ref[j * HEAD_DIM:j * HEAD_DIM + 1, :] for j in range(GQA_GROUP)]

    chains = [(u, g) for u in units for g in range(N_KV_HEADS)]
    ch = range(len(chains))
    k_rot = {u: pltpu.roll(k_all[u], HEAD_DIM, 1) for u in units}
    k2 = [jnp.where(low, k_all[u], k_rot[u]) if g == 0 else jnp.where(low, k_rot[u], k_all[u])
          for u, g in chains]
    k4 = [jnp.concatenate([k2[c], k2[c]], axis=1).astype(BF16) for c in ch]
    vb = {u: v_all[u].astype(BF16) for u in units}
    qg = [q[u][:, g * QUAD:(g + 1) * QUAD].astype(BF16) for u, g in chains]
    qstack = [jnp.concatenate([qg[c] * lane_blk[j] for j in range(GQA_GROUP)], axis=0) for c in ch]
    sink_row = [jnp.concatenate(
        [jnp.broadcast_to(sinks[:, g * GQA_GROUP + j:g * GQA_GROUP + j + 1], (1, WINDOW))
         for j in range(GQA_GROUP)], axis=1) for u, g in chains]
    yield
    sc = [_mm_nt(k4[c], qstack[c]) + bias[u[1]] for c, (u, g) in enumerate(chains)]
    yield
    m = [jnp.maximum(jnp.max(sc[c], axis=0, keepdims=True), sink_row[c]) for c in ch]
    e = [jnp.exp2(sc[c] - m[c]) for c in ch]
    yield
    denom = [jnp.sum(e[c], axis=0, keepdims=True) + jnp.exp2(sink_row[c] - m[c]) for c in ch]
    ot = {(u, g): _mm_tn(vb[u], e[c].astype(BF16))[g * HEAD_DIM:(g + 1) * HEAD_DIM, :] * (1.0 / denom[c])
          for c, (u, g) in enumerate(chains)}
    yield
    ya = []
    for s in range(nseq):
        blocks = []
        for b in range(nblk):
            yt = jnp.concatenate([ot[((s, b), g)][:, j * WINDOW:(j + 1) * WINDOW]
                                  for g in range(N_KV_HEADS) for j in range(GQA_GROUP)], axis=0)
            blocks.append(jnp.transpose(yt))
        ya.append(jnp.concatenate(blocks, axis=0) if nblk > 1 else blocks[0])
    out["ya"] = ya


def _ffn_stages(x, yr, ya, wo_ref, nw_ref, wu_ref, wd_ref, out, pieces=4):
    mix = jnp.concatenate([yr, ya], axis=1).astype(BF16)
    x1 = x + jnp.dot(mix, wo_ref[...], preferred_element_type=F32)
    yield
    ms = jnp.mean(x1 * x1, axis=-1, keepdims=True)
    hf = ((x1 * lax.rsqrt(ms + RMS_EPS)) * nw_ref[...]).astype(BF16)
    acc = x1
    step = D_FF // pieces
    for j in range(pieces):
        up = jnp.dot(hf, wu_ref[:, j * step:(j + 1) * step], preferred_element_type=F32)
        yield
        act = jnp.square(jnp.maximum(up, 0.0)).astype(BF16)
        acc = acc + jnp.dot(act, wd_ref[j * step:(j + 1) * step, :], preferred_element_type=F32)
        yield
    out["y"] = acc


def _attn_ffn_kernel(q_ref, kv_ref, taba_ref, tabb_ref, qw_ref, kw_ref, sink_ref, ones_ref, bias_ref,
                     x_ref, yr_ref, xd_ref, yrd_ref, yad_ref, wo_ref, nw_ref, wu_ref, wd_ref,
                     o_ref, od_ref, kwin_ref, vwin_ref, kprev_ref, vprev_ref, ya_ref):
    i = pl.program_id(0)
    nseq, tq, _ = q_ref.shape
    seqs = range(nseq)
    first = i == 0

    @pl.when(first)
    def _():
        kprev_ref[...] = jnp.zeros_like(kprev_ref)
        vprev_ref[...] = jnp.zeros_like(vprev_ref)
        ya_ref[...] = jnp.zeros_like(ya_ref)

    def rows(ref, dec_ref):
        tile = jnp.concatenate([ref[s] for s in seqs], axis=0)
        dec = dec_ref[...]
        return jnp.where(first, jnp.concatenate([dec] * (tile.shape[0] // dec.shape[0]), axis=0), tile)

    a_out, f_out = {}, {}
    _interleave(
        _ffn_stages(rows(x_ref, xd_ref), rows(yr_ref, yrd_ref), rows(ya_ref, yad_ref),
                    wo_ref, nw_ref, wu_ref, wd_ref, f_out),
        _attn_stages(q_ref, kv_ref, taba_ref, tabb_ref, qw_ref, kw_ref, sink_ref, ones_ref, bias_ref,
                     jnp.minimum(i, 1), kprev_ref, vprev_ref, a_out))
    for s in seqs:
        o_ref[s] = f_out["y"][s * tq:(s + 1) * tq]
        ya_ref[s] = a_out["ya"][s]

    @pl.when(first)
    def _():
        od_ref[...] = f_out["y"][:od_ref.shape[0]]

    @pl.when(i == pl.num_programs(0) - 2)
    def _():
        for s in seqs:
            kwin_ref[s] = a_out["k_cur"][s]
            vwin_ref[s] = a_out["v_cur"][s]


def _band_bias():
    ki = np.arange(2 * WINDOW)[:, None]
    qi = (np.arange(GQA_GROUP * WINDOW) % WINDOW + WINDOW)[None, :]
    dq = qi - ki
    band = (dq >= 0) & (dq < WINDOW)
    first = band & (ki >= WINDOW)
    return jnp.asarray(np.where(np.stack([first, band]), 0.0, NEG_INF), F32)


def _rope_block_tables(nb):
    freq = _rope_lane_freq()
    ang_a = (jnp.arange(nb, dtype=F32) * WINDOW)[:, None] * freq
    ang_b = jnp.arange(WINDOW, dtype=F32)[:, None] * freq
    tab_a = jnp.concatenate([jnp.cos(ang_a), jnp.sin(ang_a)], axis=1)
    tab_b = jnp.concatenate([jnp.cos(ang_b), jnp.sin(ang_b)], axis=1)
    return jnp.broadcast_to(tab_a[:, None, :], (nb, 8, 4 * HEAD_DIM)), tab_b


ATTN_BLOCKS_PER_STEP = 1


def _attn_ffn(q3d, kv3d, qw, kw, sinks, ones_bd, bias, x3d, yr3d, xd, yrd, yad, wo, nw, wu, wd):
    bsz, t, _ = q3d.shape
    nd = xd.shape[0]
    nblk = ATTN_BLOCKS_PER_STEP
    tq = nblk * WINDOW
    nt = t // tq
    const = lambda shape: pl.BlockSpec(shape, lambda i: (0,) * len(shape))
    single = lambda shape: pl.BlockSpec(shape, lambda i: (0,) * len(shape), pipeline_mode=pl.Buffered(1))
    cur = lambda i: jnp.minimum(i, nt - 1)
    prv = lambda i: jnp.maximum(i - 1, 0)
    tab_a, tab_b = _rope_block_tables(t // WINDOW)
    return pl.pallas_call(
        _attn_ffn_kernel,
        grid=(nt + 1,),
        in_specs=[
            pl.BlockSpec((bsz, tq, D_ATTN), lambda i: (0, cur(i), 0)),
            pl.BlockSpec((bsz, tq, 2 * D_KV), lambda i: (0, cur(i), 0)),
            pl.BlockSpec((nblk, 8, 4 * HEAD_DIM), lambda i: (cur(i), 0, 0)),
            const((WINDOW, 4 * HEAD_DIM)),
            const((1, D_ATTN)),
            const((1, D_KV)),
            const((1, N_Q_HEADS)),
            const((QUAD, QUAD)),
            const((2, 2 * WINDOW, GQA_GROUP * WINDOW)),
            pl.BlockSpec((bsz, tq, D_MODEL), lambda i: (0, prv(i), 0)),
            pl.BlockSpec((bsz, tq, D_RWKV), lambda i: (0, prv(i), 0)),
            const((nd, D_MODEL)),
            const((nd, D_RWKV)),
            const((nd, D_ATTN)),
            single((D_MODEL, D_MODEL)),
            const((1, D_MODEL)),
            single((D_MODEL, D_FF)),
            single((D_FF, D_MODEL)),
        ],
        out_specs=[
            pl.BlockSpec((bsz, tq, D_MODEL), lambda i: (0, prv(i), 0)),
            const((nd, D_MODEL)),
            const((bsz, WINDOW, D_KV)),
            const((bsz, WINDOW, D_KV)),
        ],
        out_shape=[
            jax.ShapeDtypeStruct((bsz, t, D_MODEL), F32),
            jax.ShapeDtypeStruct((nd, D_MODEL), F32),
            jax.ShapeDtypeStruct((bsz, WINDOW, D_KV), F32),
            jax.ShapeDtypeStruct((bsz, WINDOW, D_KV), F32),
        ],
        scratch_shapes=[
            pltpu.VMEM((bsz, WINDOW, D_KV), F32),
            pltpu.VMEM((bsz, WINDOW, D_KV), F32),
            pltpu.VMEM((bsz, tq, D_ATTN), F32),
        ],
        compiler_params=pltpu.CompilerParams(
            dimension_semantics=("arbitrary",), vmem_limit_bytes=VMEM_LIMIT),
        name="attn_ffn",
    )(q3d, kv3d, tab_a, tab_b, qw, kw, sinks, ones_bd, bias, x3d, yr3d, xd, yrd, yad, wo, nw, wu, wd)


DEC_TILE = 8


def _decode_prep_kernel(p_ref, sh_ref, q_ref, kv_ref, mu_ref, prm_ref, wl_ref, ones_ref, tab_ref,
                        qw_ref, kw_ref, vec_ref, vgb_ref, qn_ref, kvn_ref):
    p = p_ref[...]
    xs = p + (sh_ref[...] - p) * mu_ref[...]
    r, logw, k, v, a, b, g, bonus = _rwkv_features(xs, prm_ref[...], wl_ref, ones_ref)
    for i, x in enumerate((a, b, k, jnp.exp(logw), r, v)):
        vec_ref[i] = jnp.transpose(x)
    vgb_ref[0] = g
    vgb_ref[1] = bonus
    n = p.shape[0]
    tab = jnp.broadcast_to(tab_ref[0:1, :], (n, 4 * HEAD_DIM))
    cos_t, sin_lo, sin_hi = _rope_tables(tab[:, :128], tab[:, 128:])
    qn_ref[...] = _qk_norm_rope(q_ref[...], qw_ref[...], _tile_lanes(cos_t, 4), _tile_lanes(sin_lo, 4),
                                _tile_lanes(sin_hi, 4), ones_ref)
    kv = kv_ref[...]
    kvn_ref[:, 0:D_KV] = _qk_norm_rope(kv[:, 0:D_KV], kw_ref[...], cos_t, sin_lo, sin_hi, ones_ref)
    kvn_ref[:, D_KV:] = kv[:, D_KV:]


def _decode_prep(p, shift, q, kv, mu_pad, prm, wl, bmask, tab, qw, kw):
    n = p.shape[0]
    full = lambda shape: pl.BlockSpec(shape, lambda i: (0,) * len(shape))
    return pl.pallas_call(
        _decode_prep_kernel,
        grid=(1,),
        in_specs=[full((n, D_SHIFT_PAD)), full((n, D_SHIFT_PAD)), full((n, D_ATTN)), full((n, 2 * D_KV)),
                  full((1, D_SHIFT_PAD)), full((16, D_RWKV)), full((D_LORA_PAD, 3 * D_RWKV)),
                  full((QUAD, QUAD)), full((8, 4 * HEAD_DIM)), full((1, D_ATTN)), full((1, D_KV))],
        out_specs=[full((6, D_RWKV, n)), full((2, n, D_RWKV)), full((n, D_ATTN)), full((n, 2 * D_KV))],
        out_shape=[
            jax.ShapeDtypeStruct((6, D_RWKV, n), F32),
            jax.ShapeDtypeStruct((2, n, D_RWKV), F32),
            jax.ShapeDtypeStruct((n, D_ATTN), F32),
            jax.ShapeDtypeStruct((n, 2 * D_KV), F32),
        ],
        compiler_params=pltpu.CompilerParams(
            dimension_semantics=("arbitrary",), vmem_limit_bytes=VMEM_LIMIT),
        name="decode_prep",
    )(p, shift, q, kv, mu_pad, prm, wl, bmask, tab, qw, kw)


def _decode_state_kernel(vec_ref, gb_ref, prm_ref, ones_ref, s_ref, sout_ref, yr_ref, yt_ref):
    h = pl.program_id(0)
    a_t, b_t, k_t, w_t, r_t = (vec_ref[i] for i in range(5))

    def body(i, carry):
        s = s_ref[0, i]
        sa = jnp.sum(s * a_t, axis=0, keepdims=True)
        v_i = vec_ref[5, pl.ds(i, 1), :]
        s_new = s * w_t + sa * b_t + v_i * k_t
        sout_ref[0, i] = s_new
        yt_ref[pl.ds(h * HEAD_DIM + i, 1), :] = jnp.sum(s_new * r_t, axis=0, keepdims=True)
        return carry

    lax.fori_loop(0, HEAD_DIM, body, 0, unroll=8)

    @pl.when(h == pl.num_programs(0) - 1)
    def _():
        y = jnp.transpose(yt_ref[...])
        yr_ref[...] = _rwkv_finish(y, gb_ref[0], gb_ref[1], prm_ref[...], ones_ref)


def _decode_state(vec_t, gb, prm, bmask, s_t):
    n = s_t.shape[-1]
    const = lambda shape: pl.BlockSpec(shape, lambda h: (0,) * len(shape))
    return pl.pallas_call(
        _decode_state_kernel,
        grid=(H_RWKV,),
        in_specs=[
            pl.BlockSpec((6, HEAD_DIM, n), lambda h: (0, h, 0)),
            const((2, n, D_RWKV)),
            const((16, D_RWKV)),
            const((QUAD, QUAD)),
            pl.BlockSpec((1, HEAD_DIM, HEAD_DIM, n), lambda h: (h, 0, 0, 0)),
        ],
        out_specs=[
            pl.BlockSpec((1, HEAD_DIM, HEAD_DIM, n), lambda h: (h, 0, 0, 0)),
            const((n, D_RWKV)),
        ],
        out_shape=[
            jax.ShapeDtypeStruct((H_RWKV, HEAD_DIM, HEAD_DIM, n), F32),
            jax.ShapeDtypeStruct((n, D_RWKV), F32),
        ],
        scratch_shapes=[pltpu.VMEM((D_RWKV, n), F32)],
        compiler_params=pltpu.CompilerParams(
            dimension_semantics=("arbitrary",), vmem_limit_bytes=VMEM_LIMIT),
        name="decode_state",
    )(vec_t, gb, prm, bmask, s_t)


def _decode_attn_kernel(qr_ref, kvn_ref, col_ref, ck_ref, cv_ref, sink_ref, ya_ref, kout_ref, vout_ref):
    nh = N_Q_HEADS
    seqs = range(DEC_TILE)
    hrow = lax.broadcasted_iota(jnp.int32, (nh, D_ATTN), 0)
    hlane = lax.broadcasted_iota(jnp.int32, (nh, D_ATTN), 1) // HEAD_DIM
    dmask = hrow == hlane
    grow = lax.broadcasted_iota(jnp.int32, (nh, D_KV), 0) // GQA_GROUP
    glane = lax.broadcasted_iota(jnp.int32, (nh, D_KV), 1) // HEAD_DIM
    gmask = grow == glane
    low = glane == 0
    key_idx = lax.broadcasted_iota(jnp.int32, (nh, WINDOW), 1)
    last = lax.broadcasted_iota(jnp.int32, (D_KV, WINDOW), 1) == WINDOW - 1
    sink = sink_ref[...]
    kvn = kvn_ref[...]
    col = col_ref[0]
    k_new = [kvn[j:j + 1, 0:D_KV] for j in seqs]
    v_new = [kvn[j:j + 1, D_KV:] for j in seqs]
    ck = [ck_ref[j] for j in seqs]
    cv = [cv_ref[j] for j in seqs]
    for j in seqs:
        kout_ref[j] = jnp.where(last, col[0:D_KV, j:j + 1], pltpu.roll(ck[j], WINDOW - 1, 1))
        vout_ref[j] = jnp.where(last, col[D_KV:, j:j + 1], pltpu.roll(cv[j], WINDOW - 1, 1))
    q8 = [qr_ref[j * nh:(j + 1) * nh, :] for j in seqs]
    qp = [jnp.where(gmask, jnp.concatenate([q8[j], q8[j]], axis=1), 0.0) for j in seqs]
    s_c = [jnp.where(key_idx >= 1, jnp.dot(qp[j], ck[j], preferred_element_type=F32) * ATTN_SCALE, NEG_INF)
           for j in seqs]
    s_n = [jnp.sum(qp[j] * k_new[j], axis=-1, keepdims=True) * ATTN_SCALE for j in seqs]
    m = [jnp.maximum(jnp.maximum(jnp.max(s_c[j], axis=-1, keepdims=True), s_n[j]), sink) for j in seqs]
    e_c = [jnp.exp(s_c[j] - m[j]) for j in seqs]
    e_n = [jnp.exp(s_n[j] - m[j]) for j in seqs]
    denom = [jnp.sum(e_c[j], axis=-1, keepdims=True) + e_n[j] + jnp.exp(sink - m[j]) for j in seqs]
    o = [(_dot_nt_f32(e_c[j], cv[j]) + e_n[j] * v_new[j]) / denom[j]
         for j in seqs]
    out_rows = []
    for j in seqs:
        rot = pltpu.roll(o[j], HEAD_DIM, 1)
        g0 = jnp.where(low, o[j], rot)
        g1 = jnp.where(low, rot, o[j])
        wide = jnp.concatenate([g0, g0, g1, g1], axis=1)
        out_rows.append(jnp.sum(jnp.where(dmask, wide, 0.0), axis=0, keepdims=True))
    ya_ref[...] = jnp.concatenate(out_rows, axis=0)


def _decode_attn(q_r, kvn, cols, ck_t, cv_t, sinks_col):
    n = kvn.shape[0]
    bt = DEC_TILE
    const = lambda shape: pl.BlockSpec(shape, lambda i: (0,) * len(shape))
    return pl.pallas_call(
        _decode_attn_kernel,
        grid=(n // bt,),
        in_specs=[
            pl.BlockSpec((bt * N_Q_HEADS, HEAD_DIM), lambda i: (i, 0)),
            pl.BlockSpec((bt, 2 * D_KV), lambda i: (i, 0)),
            pl.BlockSpec((1, 2 * D_KV, bt), lambda i: (i, 0, 0)),
            pl.BlockSpec((bt, D_KV, WINDOW), lambda i: (i, 0, 0)),
            pl.BlockSpec((bt, D_KV, WINDOW), lambda i: (i, 0, 0)),
            const((N_Q_HEADS, 1)),
        ],
        out_specs=[
            pl.BlockSpec((bt, D_ATTN), lambda i: (i, 0)),
            pl.BlockSpec((bt, D_KV, WINDOW), lambda i: (i, 0, 0)),
            pl.BlockSpec((bt, D_KV, WINDOW), lambda i: (i, 0, 0)),
        ],
        out_shape=[
            jax.ShapeDtypeStruct((n, D_ATTN), F32),
            jax.ShapeDtypeStruct((n, D_KV, WINDOW), F32),
            jax.ShapeDtypeStruct((n, D_KV, WINDOW), F32),
        ],
        compiler_params=pltpu.CompilerParams(
            dimension_semantics=("arbitrary",), vmem_limit_bytes=VMEM_LIMIT),
        name="decode_attn",
    )(q_r, kvn, cols, ck_t, cv_t, sinks_col)


def _pad_cols(w, at, n):
    return jnp.concatenate([w[..., :at], jnp.zeros(w.shape[:-1] + (n,), w.dtype), w[..., at:]], axis=-1)


def kernel(x_prompt, x_sample, state_wkv, state_shift, cache_k_win, cache_v_win, norm_mix_w, w_in, mu_shift, w0, w_decay_up, a0, w_a_up, w_g_up, k_k, k_a, r_k, ln_x_w, ln_x_b, q_norm_w, k_norm_w, sinks, w_out, norm_ffn_w, w_ffn_up, w_ffn_down):
    bsz, t, _ = x_prompt.shape
    nd = x_sample.shape[0]
    l = 0
    pad = D_LORA_PAD - D_LORA

    w_in_t = jnp.swapaxes(w_in[l], 0, 1)
    w_in_pad = jnp.concatenate([w_in_t[:D_SHIFT], jnp.zeros((pad, D_MODEL), F32), w_in_t[D_SHIFT:]],
                               axis=0).astype(BF16)
    mu_pad = _pad_cols(mu_shift[l][None, :], D_SHIFT, pad)
    wl = jnp.zeros((D_LORA_PAD, 3 * D_RWKV), F32)
    wl = wl.at[0:32, 0:D_RWKV].set(w_decay_up[l])
    wl = wl.at[32:64, D_RWKV:2 * D_RWKV].set(w_a_up[l])
    wl = wl.at[64:160, 2 * D_RWKV:].set(w_g_up[l])
    wl = wl.astype(BF16)
    prm = jnp.zeros((16, D_RWKV), F32)
    prm = prm.at[0].set(w0[l]).at[1].set(a0[l]).at[2].set(k_k[l]).at[3].set(k_a[l])
    prm = prm.at[4].set(r_k[l].reshape(-1)).at[5].set(ln_x_w[l]).at[6].set(ln_x_b[l])
    hid = np.arange(QUAD) // HEAD_DIM
    bmask = jnp.asarray(hid[:, None] == hid[None, :], BF16)
    tri = jnp.asarray(np.tril(np.ones((CHUNK, CHUNK))), BF16)
    qw = jnp.tile(q_norm_w[l][None, :], (1, N_Q_HEADS))
    kw = jnp.tile(k_norm_w[l][None, :], (1, N_KV_HEADS))
    nmw = norm_mix_w[l][None, :]
    nfw = norm_ffn_w[l][None, :]
    wo = w_out[l].astype(BF16)
    wu = w_ffn_up[l].astype(BF16)
    wd = w_ffn_down[l].astype(BF16)
    ang_s = (jnp.zeros((8, 1), F32) + PAST_LEN) * _rope_lane_freq()
    tab_s = jnp.concatenate([jnp.cos(ang_s), jnp.sin(ang_s)], axis=1)

    xs = x_sample.reshape(nd, D_MODEL)
    p_s, q_s, kv_s = _inproj(xs, nmw, w_in_pad, 128)
    shift_in = _pad_cols(state_shift[l].reshape(nd, D_SHIFT), D_SHIFT, pad)
    vec_t, gb, qn_s, kvn_s = _decode_prep(p_s, shift_in, q_s, kv_s, mu_pad, prm, wl, bmask, tab_s, qw, kw)
    s_t = jnp.transpose(state_wkv[l], (1, 2, 3, 0))
    ck_t = jnp.swapaxes(cache_k_win[l].reshape(nd, WINDOW, D_KV), 1, 2)
    cv_t = jnp.swapaxes(cache_v_win[l].reshape(nd, WINDOW, D_KV), 1, 2)
    q_r = qn_s.reshape(nd * N_Q_HEADS, HEAD_DIM)
    cols = jnp.swapaxes(kvn_s.reshape(nd // DEC_TILE, DEC_TILE, 2 * D_KV), 1, 2)
    wkv_t, yr_s = _decode_state(vec_t, gb, prm, bmask, s_t)
    ya_s, kc_t, vc_t = _decode_attn(q_r, kvn_s, cols, ck_t, cv_t, sinks[l][:, None])

    xp = x_prompt.reshape(bsz * t, D_MODEL)
    xs_p, plast, q_p, kv_p = _inproj_shift(xp, nmw, w_in_pad, mu_pad, t, 512)
    yr_p, hbd = _rwkv_prompt(xs_p.reshape(bsz, t, D_SHIFT_PAD), prm, wl, tri, bmask)
    y_prompt, y_s, kwin_p, vwin_p = _attn_ffn(q_p.reshape(bsz, t, D_ATTN), kv_p.reshape(bsz, t, 2 * D_KV),
                                              qw, kw, sinks[l][None, :], bmask, _band_bias(),
                                              x_prompt, yr_p, xs, yr_s, ya_s, wo, nfw, wu, wd)
    hb = hbd.reshape(bsz, 2, 4, HEAD_DIM, 2, HEAD_DIM)
    wkv_prompt = jnp.stack([hb[:, :, j, :, j % 2, :] for j in range(4)], axis=2)
    wkv_prompt = wkv_prompt.reshape(bsz, H_RWKV, HEAD_DIM, HEAD_DIM)[None]
    shift_prompt = plast[:, 0:1, :D_SHIFT][None]
    k_win_prompt = kwin_p.reshape(bsz, WINDOW, N_KV_HEADS, HEAD_DIM)[None]
    v_win_prompt = vwin_p.reshape(bsz, WINDOW, N_KV_HEADS, HEAD_DIM)[None]

    y_sample = y_s.reshape(nd, 1, D_MODEL)
    wkv_sample = jnp.transpose(wkv_t, (3, 0, 1, 2))[None]
    shift_sample = p_s[:, :D_SHIFT].reshape(nd, 1, D_SHIFT)[None]
    k_win_sample = jnp.swapaxes(kc_t, 1, 2).reshape(nd, WINDOW, N_KV_HEADS, HEAD_DIM)[None]
    v_win_sample = jnp.swapaxes(vc_t, 1, 2).reshape(nd, WINDOW, N_KV_HEADS, HEAD_DIM)[None]

    return (y_prompt, y_sample, wkv_prompt, shift_prompt, k_win_prompt, v_win_prompt,
            wkv_sample, shift_sample, k_win_sample, v_win_sample)
```

```python
import functools

import jax
import jax.numpy as jnp
import numpy as np
from jax import lax
from jax.experimental import pallas as pl
from jax.experimental.pallas import tpu as pltpu

F32 = jnp.float32
BF16 = jnp.bfloat16

D_MODEL = 1024
D_RWKV = 512
D_ATTN = 512
HEAD_DIM = 64
H_RWKV = 8
N_Q_HEADS = 8
N_KV_HEADS = 2
GQA_GROUP = 4
D_KV = 128
D_LORA = 160
D_LORA_PAD = 256
D_SHIFT = 3 * D_RWKV + D_LORA
D_SHIFT_PAD = 3 * D_RWKV + D_LORA_PAD
D_IN_PAD = D_SHIFT_PAD + D_ATTN + 2 * D_KV
WINDOW = 128
ROPE_DIM = 16
ROPE_HALF = 8
ROPE_THETA = 500000.0
ATTN_SCALE = HEAD_DIM ** -0.5
D_FF = 4096
RMS_EPS = 1e-6
LNX_EPS = 64e-5
NEG_INF = -1e30
LOG2E = 1.4426950408889634
PAST_LEN = 16384

CHUNK = 64
QUAD = 4 * HEAD_DIM
VMEM_LIMIT = 56 * 1024 * 1024


def _split2(x):
    hi = x.astype(BF16)
    lo = (x - hi.astype(F32)).astype(BF16)
    return hi, lo


def _head_sums(xs, ones_ref, exact=True):
    n, w = xs[0].shape
    tile = min(w, QUAD)
    per = w // tile
    pieces = [x[:, j * tile:(j + 1) * tile] for x in xs for j in range(per)]
    stacked = jnp.concatenate(pieces, axis=0) if len(pieces) > 1 else pieces[0]
    ones = ones_ref[0:tile, 0:tile]
    if exact:
        hi, lo = _split2(stacked)
        out = jnp.dot(hi, ones, preferred_element_type=F32) + jnp.dot(lo, ones, preferred_element_type=F32)
    else:
        out = jnp.dot(stacked.astype(BF16), ones, preferred_element_type=F32)
    res = []
    for i in range(len(xs)):
        cols = [out[(i * per + j) * n:(i * per + j + 1) * n] for j in range(per)]
        res.append(jnp.concatenate(cols, axis=1) if per > 1 else cols[0])
    return res


def _cumsum_rows(tri_bf16, x):
    hi, lo = _split2(x)
    return (jnp.dot(tri_bf16, hi, preferred_element_type=F32)
            + jnp.dot(tri_bf16, lo, preferred_element_type=F32))


def _mm(a, b):
    return jnp.dot(a.astype(BF16), b.astype(BF16), preferred_element_type=F32)


def _mm_nt(a, b):
    return lax.dot_general(a.astype(BF16), b.astype(BF16), (((1,), (1,)), ((), ())),
                           preferred_element_type=F32)


def _mm_tn(a, b):
    return lax.dot_general(a.astype(BF16), b.astype(BF16), (((0,), (0,)), ((), ())),
                           preferred_element_type=F32)


def _dot_nt_f32(a, b):
    return lax.dot_general(a, b, (((1,), (1,)), ((), ())), preferred_element_type=F32)


def _sigmoid(x):
    return 1.0 / (1.0 + jnp.exp(-x))


def _interleave(*gens):
    live = list(gens)
    while live:
        for g in list(live):
            try:
                next(g)
            except StopIteration:
                live.remove(g)


def _norm_project(x, nw_ref, wt_ref):
    ms = jnp.mean(x * x, axis=-1, keepdims=True)
    h = (x * lax.rsqrt(ms + RMS_EPS)) * nw_ref[...]
    return _mm_nt(h, wt_ref[...])


def _inproj_kernel(x_ref, nw_ref, w_ref, p_ref, q_ref, kv_ref):
    out = _norm_project(x_ref[...], nw_ref, w_ref)
    p_ref[...] = out[:, :D_SHIFT_PAD]
    q_ref[...] = out[:, D_SHIFT_PAD:D_SHIFT_PAD + D_ATTN]
    kv_ref[...] = out[:, D_SHIFT_PAD + D_ATTN:]


def _inproj_shift_kernel(tiles_per_seq, x_ref, nw_ref, w_ref, mu_ref, xs_ref, last_ref, q_ref, kv_ref, prev_ref):
    i = pl.program_id(0)

    @pl.when(i % tiles_per_seq == 0)
    def _():
        prev_ref[...] = jnp.zeros_like(prev_ref)

    out = _norm_project(x_ref[...], nw_ref, w_ref)
    p = out[:, :D_SHIFT_PAD]
    tm = p.shape[0]
    row = lax.broadcasted_iota(jnp.int32, p.shape, 0)
    prev = jnp.where(row == 0, jnp.broadcast_to(prev_ref[0:1, :], p.shape), pltpu.roll(p, 1, 0))
    xs_ref[...] = p + (prev - p) * mu_ref[...]
    last = jnp.broadcast_to(p[tm - 1:tm, :], prev_ref.shape)
    prev_ref[...] = last
    last_ref[0] = last
    q_ref[...] = out[:, D_SHIFT_PAD:D_SHIFT_PAD + D_ATTN]
    kv_ref[...] = out[:, D_SHIFT_PAD + D_ATTN:]


def _inproj_shift(x2d, norm_w, w_in_pad, mu_pad, seq_len, tm):
    m = x2d.shape[0]
    tiles_per_seq = seq_len // tm
    return pl.pallas_call(
        functools.partial(_inproj_shift_kernel, tiles_per_seq),
        grid=(m // tm,),
        in_specs=[
            pl.BlockSpec((tm, D_MODEL), lambda i: (i, 0)),
            pl.BlockSpec((1, D_MODEL), lambda i: (0, 0)),
            pl.BlockSpec((D_IN_PAD, D_MODEL), lambda i: (0, 0)),
            pl.BlockSpec((1, D_SHIFT_PAD), lambda i: (0, 0)),
        ],
        out_specs=[
            pl.BlockSpec((tm, D_SHIFT_PAD), lambda i: (i, 0)),
            pl.BlockSpec((1, 8, D_SHIFT_PAD), lambda i: (i // tiles_per_seq, 0, 0)),
            pl.BlockSpec((tm, D_ATTN), lambda i: (i, 0)),
            pl.BlockSpec((tm, 2 * D_KV), lambda i: (i, 0)),
        ],
        out_shape=[
            jax.ShapeDtypeStruct((m, D_SHIFT_PAD), F32),
            jax.ShapeDtypeStruct((m // seq_len, 8, D_SHIFT_PAD), F32),
            jax.ShapeDtypeStruct((m, D_ATTN), F32),
            jax.ShapeDtypeStruct((m, 2 * D_KV), F32),
        ],
        scratch_shapes=[pltpu.VMEM((8, D_SHIFT_PAD), F32)],
        compiler_params=pltpu.CompilerParams(
            dimension_semantics=("arbitrary",), vmem_limit_bytes=VMEM_LIMIT),
        name="inproj_shift",
    )(x2d, norm_w, w_in_pad, mu_pad)


def _inproj(x2d, norm_w, w_in_pad, tm):
    m = x2d.shape[0]
    return pl.pallas_call(
        _inproj_kernel,
        grid=(m // tm,),
        in_specs=[
            pl.BlockSpec((tm, D_MODEL), lambda i: (i, 0)),
            pl.BlockSpec((1, D_MODEL), lambda i: (0, 0)),
            pl.BlockSpec((D_IN_PAD, D_MODEL), lambda i: (0, 0)),
        ],
        out_specs=[
            pl.BlockSpec((tm, D_SHIFT_PAD), lambda i: (i, 0)),
            pl.BlockSpec((tm, D_ATTN), lambda i: (i, 0)),
            pl.BlockSpec((tm, 2 * D_KV), lambda i: (i, 0)),
        ],
        out_shape=[
            jax.ShapeDtypeStruct((m, D_SHIFT_PAD), F32),
            jax.ShapeDtypeStruct((m, D_ATTN), F32),
            jax.ShapeDtypeStruct((m, 2 * D_KV), F32),
        ],
        compiler_params=pltpu.CompilerParams(
            dimension_semantics=("arbitrary",), vmem_limit_bytes=VMEM_LIMIT),
        name="inproj",
    )(x2d, norm_w, w_in_pad)


def _rwkv_features(xs, prm, wl_ref, ones_ref):
    r = xs[:, 0:D_RWKV]
    k = xs[:, D_RWKV:2 * D_RWKV]
    v = xs[:, 2 * D_RWKV:3 * D_RWKV]
    lora = xs[:, 3 * D_RWKV:]
    col = lax.broadcasted_iota(jnp.int32, lora.shape, 1)
    act = jnp.where(col < 32, jnp.tanh(lora), jnp.where(col < 64, lora, _sigmoid(lora)))
    up = jnp.dot(act.astype(BF16), wl_ref[...], preferred_element_type=F32)
    w0, a0, k_k, k_a, r_k = prm[0:1], prm[1:2], prm[2:3], prm[3:4], prm[4:5]
    logw = (-np.exp(-0.5)) * _sigmoid(w0 + up[:, 0:D_RWKV])
    asig = _sigmoid(a0 + up[:, D_RWKV:2 * D_RWKV])
    g = up[:, 2 * D_RWKV:]
    kk = k * k_k
    k_mod = k * (1.0 + (asig - 1.0) * k_a)
    ss, rk = _head_sums([kk * kk, r * k_mod * r_k], ones_ref)
    kk = kk / jnp.maximum(jnp.sqrt(ss), 1e-12)
    k = k_mod
    bonus = rk * v
    return r, logw, k, v, -kk, kk * asig, g, bonus


def _rwkv_finish(y, g, bonus, prm, ones_ref):
    ln_w, ln_b = prm[5:6], prm[6:7]
    mean = _head_sums([y], ones_ref, exact=False)[0] * (1.0 / HEAD_DIM)
    d = y - mean
    var = _head_sums([d * d], ones_ref, exact=False)[0] * (1.0 / HEAD_DIM)
    yn = d * lax.rsqrt(var + LNX_EPS) * ln_w + ln_b
    return (yn + bonus) * g


def _block_diag(x, bmask):
    return jnp.concatenate([x] * 4, axis=0) * bmask


def _chunk_prep(r, logw, k, v, a, b, tri_ref, bmask):
    n = len(r)
    ch = range(n)
    tri = tri_ref[...]
    cum = [_cumsum_rows(tri, logw[i]) for i in ch]
    e_in = [jnp.exp(cum[i]) for i in ch]
    e_ex = [jnp.exp(cum[i] - logw[i]) for i in ch]
    e_inv = [1.0 / e_in[i] for i in ch]
    e_last = [e_in[i][CHUNK - 1:CHUNK, :] for i in ch]
    rt = [(r[i] * e_in[i]).astype(BF16) for i in ch]
    at = [(a[i] * e_ex[i]).astype(BF16) for i in ch]
    kt = [(k[i] * e_inv[i]).astype(BF16) for i in ch]
    bt = [(b[i] * e_inv[i]).astype(BF16) for i in ch]
    vb = [v[i].astype(BF16) for i in ch]

    t_idx = lax.broadcasted_iota(jnp.int32, (CHUNK, QUAD), 0)
    s_idx = lax.broadcasted_iota(jnp.int32, (CHUNK, QUAD), 1) & (HEAD_DIM - 1)
    strict = s_idx < t_idx
    incl = s_idx <= t_idx

    gm = [_mm_nt(jnp.concatenate([at[i], rt[i]], axis=0),
                 jnp.concatenate([_block_diag(bt[i], bmask), _block_diag(kt[i], bmask)], axis=0))
          for i in ch]
    a_ab = [jnp.where(strict, gm[i][:CHUNK, :QUAD], 0.0) for i in ch]
    a_ak = [jnp.where(strict, gm[i][:CHUNK, QUAD:], 0.0) for i in ch]
    a_rb = [jnp.where(incl, gm[i][CHUNK:, :QUAD], 0.0) for i in ch]
    a_rk = [jnp.where(incl, gm[i][CHUNK:, QUAD:], 0.0) for i in ch]

    eye = jnp.where(s_idx == t_idx, 1.0, 0.0)
    pwb = [a_ab[i].astype(BF16) for i in ch]
    t_inv = [eye + a_ab[i] for i in ch]
    for it in range(6):
        rbd = [_block_diag(pwb[i], bmask) for i in ch]
        if it == 0:
            pwb = [_mm(pwb[i], rbd[i]).astype(BF16) for i in ch]
        elif it < 5:
            out = [_mm(jnp.concatenate([pwb[i], t_inv[i].astype(BF16)], axis=0), rbd[i]) for i in ch]
            pwb = [out[i][:CHUNK].astype(BF16) for i in ch]
            t_inv = [t_inv[i] + out[i][CHUNK:] for i in ch]
        else:
            t_inv = [t_inv[i] + _mm(t_inv[i], rbd[i]) for i in ch]

    vbd = [_block_diag(vb[i], bmask) for i in ch]
    xy0 = [_mm(jnp.concatenate([a_ak[i], a_rk[i]], axis=0), vbd[i]) for i in ch]
    return [dict(ar=jnp.concatenate([at[i], rt[i]], axis=0), x0=xy0[i][:CHUNK], y0=xy0[i][CHUNK:],
                 t_inv=t_inv[i].astype(BF16), a_rb=a_rb[i].astype(BF16), vb=vb[i],
                 bk=jnp.concatenate([bt[i], kt[i]], axis=0), e_last=e_last[i]) for i in ch]


def _chunk_step(pre, state, bmask):
    ch = range(len(pre))
    half = QUAD // 2
    zeros = jnp.zeros((half, half), BF16)
    sc = [state[i].astype(BF16) for i in ch]
    sb = [jnp.concatenate([jnp.concatenate([sc[i][:half], zeros], axis=1),
                           jnp.concatenate([zeros, sc[i][half:]], axis=1)], axis=0) for i in ch]
    xr = [_mm_nt(pre[i]["ar"], sb[i]) for i in ch]
    x = [xr[i][:CHUNK] + pre[i]["x0"] for i in ch]
    u = [_mm(pre[i]["t_inv"], _block_diag(x[i].astype(BF16), bmask)) for i in ch]
    ub = [u[i].astype(BF16) for i in ch]
    y = [xr[i][CHUNK:] + pre[i]["y0"] + _mm(pre[i]["a_rb"], _block_diag(ub[i], bmask)) for i in ch]
    upd = [_mm_tn(jnp.concatenate([ub[i], pre[i]["vb"]], axis=0), pre[i]["bk"]) for i in ch]
    bm = bmask[:half, :half].astype(F32)
    s_new = []
    for i in ch:
        e_last = pre[i]["e_last"]
        top = (state[i][:half] + upd[i][:half, :half] * bm) * e_last[:, :half]
        bot = (state[i][half:] + upd[i][half:, half:] * bm) * e_last[:, half:]
        s_new.append(jnp.concatenate([top, bot], axis=0))
    return y, s_new


def _rwkv_prompt_kernel(xs_ref, prm_ref, wl_ref, tri_ref, bmask_ref, y_ref, hout_ref, h_ref):
    c = pl.program_id(0)
    nseq, tstep, _ = xs_ref.shape
    nchunk = tstep // CHUNK

    @pl.when(c == 0)
    def _():
        h_ref[...] = jnp.zeros_like(h_ref)

    xs = jnp.concatenate([xs_ref[s] for s in range(nseq)], axis=0)
    prm = prm_ref[...]
    r, logw, k, v, a, b, g, bonus = _rwkv_features(xs, prm, wl_ref, bmask_ref)
    bmask = bmask_ref[...]
    lanes = [(s, q) for s in range(nseq) for q in range(2)]
    chains = [(j, s, q) for j in range(nchunk) for s, q in lanes]
    cut = lambda x: [x[s * tstep + j * CHUNK:s * tstep + (j + 1) * CHUNK, q * QUAD:(q + 1) * QUAD]
                     for j, s, q in chains]
    pre = _chunk_prep(cut(r), cut(logw), cut(k), cut(v), cut(a), cut(b), tri_ref, bmask)
    state = [h_ref[s, q] for s, q in lanes]
    ys = []
    for j in range(nchunk):
        y_j, state = _chunk_step(pre[j * len(lanes):(j + 1) * len(lanes)], state, bmask)
        ys.append(y_j)
    for i, (s, q) in enumerate(lanes):
        h_ref[s, q] = state[i]
    rows = [jnp.concatenate(ys[j][2 * s:2 * s + 2], axis=1) for s in range(nseq) for j in range(nchunk)]
    y = _rwkv_finish(jnp.concatenate(rows, axis=0), g, bonus, prm, bmask_ref)
    for s in range(nseq):
        y_ref[s] = y[s * tstep:(s + 1) * tstep]

    @pl.when(c == pl.num_programs(0) - 1)
    def _():
        hout_ref[...] = h_ref[...]


RWKV_CHUNKS_PER_STEP = 4


def _rwkv_prompt(xs3d, prm, wl, tri, bmask):
    bsz, t, _ = xs3d.shape
    tstep = RWKV_CHUNKS_PER_STEP * CHUNK
    nc = t // tstep
    const = lambda shape: pl.BlockSpec(shape, lambda c: (0,) * len(shape))
    state_shape = (bsz, 2, QUAD, QUAD // 2)
    return pl.pallas_call(
        _rwkv_prompt_kernel,
        grid=(nc,),
        in_specs=[
            pl.BlockSpec((bsz, tstep, D_SHIFT_PAD), lambda c: (0, c, 0)),
            const((16, D_RWKV)),
            const((D_LORA_PAD, 3 * D_RWKV)),
            const((CHUNK, CHUNK)),
            const((QUAD, QUAD)),
        ],
        out_specs=[
            pl.BlockSpec((bsz, tstep, D_RWKV), lambda c: (0, c, 0)),
            const(state_shape),
        ],
        out_shape=[
            jax.ShapeDtypeStruct((bsz, t, D_RWKV), F32),
            jax.ShapeDtypeStruct(state_shape, F32),
        ],
        scratch_shapes=[pltpu.VMEM(state_shape, F32)],
        compiler_params=pltpu.CompilerParams(
            dimension_semantics=("arbitrary",), vmem_limit_bytes=VMEM_LIMIT),
        name="rwkv_prompt",
    )(xs3d, prm, wl, tri, bmask)


def _rope_lane_freq():
    inv_freq = jnp.power(ROPE_THETA, -jnp.arange(ROPE_HALF, dtype=F32) * (2.0 / ROPE_DIM))
    return inv_freq[(np.arange(2 * HEAD_DIM) % HEAD_DIM) % ROPE_HALF][None, :]


def _rope_tables(cos, sin):
    dim = lax.broadcasted_iota(jnp.int32, cos.shape, 1) & (HEAD_DIM - 1)
    cos_t = jnp.where(dim < ROPE_DIM, cos, 1.0)
    sin_lo = jnp.where(dim < ROPE_HALF, -sin, 0.0)
    sin_hi = jnp.where((dim >= ROPE_HALF) & (dim < ROPE_DIM), sin, 0.0)
    return cos_t, sin_lo, sin_hi


def _qk_norm_rope(x, norm_w, cos_t, sin_lo, sin_hi, ones_ref):
    ms = _head_sums([x * x], ones_ref, exact=False)[0] * (1.0 / HEAD_DIM)
    xn = x * lax.rsqrt(ms + RMS_EPS) * norm_w
    width = x.shape[1]
    fwd = pltpu.roll(xn, width - ROPE_HALF, 1)
    bwd = pltpu.roll(xn, ROPE_HALF, 1)
    return xn * cos_t + fwd * sin_lo + bwd * sin_hi


def _tile_lanes(x, reps):
    return jnp.concatenate([x] * reps, axis=1) if reps > 1 else x


def _attn_stages(q_ref, kv_ref, taba_ref, tabb_ref, qw_ref, kw_ref, sink_ref, ones_ref, bias_ref, first_bias,
                 kprev_ref, vprev_ref, out):
    nseq, tq, _ = q_ref.shape
    nblk = tq // WINDOW
    units = [(s, b) for s in range(nseq) for b in range(nblk)]
    blk = lambda b: slice(b * WINDOW, (b + 1) * WINDOW)
    tb = tabb_ref[...]
    cos_b, sin_b = tb[:, :128], tb[:, 128:]
    rope, rope4 = [], []
    for b in range(nblk):
        ta = taba_ref[b][0:1, :]
        cos_a, sin_a = ta[:, :128], ta[:, 128:]
        tabs = _rope_tables(cos_a * cos_b - sin_a * sin_b, sin_a * cos_b + cos_a * sin_b)
        rope.append(tabs)
        rope4.append([_tile_lanes(x, 4) for x in tabs])
    q = {(s, b): _qk_norm_rope(q_ref[s, blk(b), :], qw_ref[...], *rope4[b], ones_ref) * (ATTN_SCALE * LOG2E)
         for s, b in units}
    kv = {(s, b): kv_ref[s, blk(b), :] for s, b in units}
    k_cur = {u: _qk_norm_rope(kv[u][:, 0:D_KV], kw_ref[...], *rope[u[1]], ones_ref) for u in units}
    v_cur = {u: kv[u][:, D_KV:] for u in units}
    k_all = {(s, b): jnp.concatenate([kprev_ref[s] if b == 0 else k_cur[(s, b - 1)], k_cur[(s, b)]], axis=0)
             for s, b in units}
    v_all = {(s, b): jnp.concatenate([vprev_ref[s] if b == 0 else v_cur[(s, b - 1)], v_cur[(s, b)]], axis=0)
             for s, b in units}
    out["k_cur"] = [k_cur[(s, nblk - 1)] for s in range(nseq)]
    out["v_cur"] = [v_cur[(s, nblk - 1)] for s in range(nseq)]
    for s in range(nseq):
        kprev_ref[s] = out["k_cur"][s]
        vprev_ref[s] = out["v_cur"][s]
    yield

    nk = 2 * WINDOW
    bias = [bias_ref[first_bias] if b == 0 else bias_ref[1] for b in range(nblk)]
    sinks = sink_ref[...] * LOG2E
    low = lax.broadcasted_iota(jnp.int32, (nk, D_KV), 1) < HEAD_DIM
    lane_blk = [ones_ref[j * HEAD_DIM:j * HEAD_DIM + 1, :] for j in range(GQA_GROUP)]

    chains = [(u, g) for u in units for g in range(N_KV_HEADS)]
    ch = range(len(chains))
    k_rot = {u: pltpu.roll(k_all[u], HEAD_DIM, 1) for u in units}
    k2 = [jnp.where(low, k_all[u], k_rot[u]) if g == 0 else jnp.where(low, k_rot[u], k_all[u])
          for u, g in chains]
    k4 = [jnp.concatenate([k2[c], k2[c]], axis=1).astype(BF16) for c in ch]
    vb = {u: v_all[u].astype(BF16) for u in units}
    qg = [q[u][:, g * QUAD:(g + 1) * QUAD].astype(BF16) for u, g in chains]
    qstack = [jnp.concatenate([qg[c] * lane_blk[j] for j in range(GQA_GROUP)], axis=0) for c in ch]
    sink_row = [jnp.concatenate(
        [jnp.broadcast_to(sinks[:, g * GQA_GROUP + j:g * GQA_GROUP + j + 1], (1, WINDOW))
         for j in range(GQA_GROUP)], axis=1) for u, g in chains]
    yield
    sc = [_mm_nt(k4[c], qstack[c]) + bias[u[1]] for c, (u, g) in enumerate(chains)]
    yield
    m = [jnp.maximum(jnp.max(sc[c], axis=0, keepdims=True), sink_row[c]) for c in ch]
    e = [jnp.exp2(sc[c] - m[c]) for c in ch]
    yield
    denom = [jnp.sum(e[c], axis=0, keepdims=True) + jnp.exp2(sink_row[c] - m[c]) for c in ch]
    ot = {(u, g): _mm_tn(vb[u], e[c].astype(BF16))[g * HEAD_DIM:(g + 1) * HEAD_DIM, :] * (1.0 / denom[c])
          for c, (u, g) in enumerate(chains)}
    yield
    ya = []
    for s in range(nseq):
        blocks = []
        for b in range(nblk):
            yt = jnp.concatenate([ot[((s, b), g)][:, j * WINDOW:(j + 1) * WINDOW]
                                  for g in range(N_KV_HEADS) for j in range(GQA_GROUP)], axis=0)
            blocks.append(jnp.transpose(yt))
        ya.append(jnp.concatenate(blocks, axis=0) if nblk > 1 else blocks[0])
    out["ya"] = ya


def _ffn_stages(x, yr, ya, wo_ref, nw_ref, wu_ref, wd_ref, out, pieces=4):
    mix = jnp.concatenate([yr, ya], axis=1).astype(BF16)
    x1 = x + jnp.dot(mix, wo_ref[...], preferred_element_type=F32)
    yield
    ms = jnp.mean(x1 * x1, axis=-1, keepdims=True)
    hf = ((x1 * lax.rsqrt(ms + RMS_EPS)) * nw_ref[...]).astype(BF16)
    acc = x1
    step = D_FF // pieces
    for j in range(pieces):
        up = jnp.dot(hf, wu_ref[:, j * step:(j + 1) * step], preferred_element_type=F32)
        yield
        act = jnp.square(jnp.maximum(up, 0.0)).astype(BF16)
        acc = acc + jnp.dot(act, wd_ref[j * step:(j + 1) * step, :], preferred_element_type=F32)
        yield
    out["y"] = acc


def _attn_ffn_kernel(q_ref, kv_ref, taba_ref, tabb_ref, qw_ref, kw_ref, sink_ref, ones_ref, bias_ref,
                     x_ref, yr_ref, xd_ref, yrd_ref, yad_ref, wo_ref, nw_ref, wu_ref, wd_ref,
                     o_ref, od_ref, kwin_ref, vwin_ref, kprev_ref, vprev_ref, ya_ref):
    i = pl.program_id(0)
    nseq, tq, _ = q_ref.shape
    seqs = range(nseq)
    first = i == 0

    @pl.when(first)
    def _():
        kprev_ref[...] = jnp.zeros_like(kprev_ref)
        vprev_ref[...] = jnp.zeros_like(vprev_ref)
        ya_ref[...] = jnp.zeros_like(ya_ref)

    def rows(ref, dec_ref):
        tile = jnp.concatenate([ref[s] for s in seqs], axis=0)
        dec = dec_ref[...]
        return jnp.where(first, jnp.concatenate([dec] * (tile.shape[0] // dec.shape[0]), axis=0), tile)

    a_out, f_out = {}, {}
    _interleave(
        _ffn_stages(rows(x_ref, xd_ref), rows(yr_ref, yrd_ref), rows(ya_ref, yad_ref),
                    wo_ref, nw_ref, wu_ref, wd_ref, f_out),
        _attn_stages(q_ref, kv_ref, taba_ref, tabb_ref, qw_ref, kw_ref, sink_ref, ones_ref, bias_ref,
                     jnp.minimum(i, 1), kprev_ref, vprev_ref, a_out))
    for s in seqs:
        o_ref[s] = f_out["y"][s * tq:(s + 1) * tq]
        ya_ref[s] = a_out["ya"][s]

    @pl.when(first)
    def _():
        od_ref[...] = f_out["y"][:od_ref.shape[0]]

    @pl.when(i == pl.num_programs(0) - 2)
    def _():
        for s in seqs:
            kwin_ref[s] = a_out["k_cur"][s]
            vwin_ref[s] = a_out["v_cur"][s]


def _band_bias():
    ki = np.arange(2 * WINDOW)[:, None]
    qi = (np.arange(GQA_GROUP * WINDOW) % WINDOW + WINDOW)[None, :]
    dq = qi - ki
    band = (dq >= 0) & (dq < WINDOW)
    first = band & (ki >= WINDOW)
    return jnp.asarray(np.where(np.stack([first, band]), 0.0, NEG_INF), F32)


def _rope_block_tables(nb):
    freq = _rope_lane_freq()
    ang_a = (jnp.arange(nb, dtype=F32) * WINDOW)[:, None] * freq
    ang_b = jnp.arange(WINDOW, dtype=F32)[:, None] * freq
    tab_a = jnp.concatenate([jnp.cos(ang_a), jnp.sin(ang_a)], axis=1)
    tab_b = jnp.concatenate([jnp.cos(ang_b), jnp.sin(ang_b)], axis=1)
    return jnp.broadcast_to(tab_a[:, None, :], (nb, 8, 4 * HEAD_DIM)), tab_b


ATTN_BLOCKS_PER_STEP = 1


def _attn_ffn(q3d, kv3d, qw, kw, sinks, ones_bd, bias, x3d, yr3d, xd, yrd, yad, wo, nw, wu, wd):
    bsz, t, _ = q3d.shape
    nd = xd.shape[0]
    nblk = ATTN_BLOCKS_PER_STEP
    tq = nblk * WINDOW
    nt = t // tq
    const = lambda shape: pl.BlockSpec(shape, lambda i: (0,) * len(shape))
    single = lambda shape: pl.BlockSpec(shape, lambda i: (0,) * len(shape), pipeline_mode=pl.Buffered(1))
    cur = lambda i: jnp.minimum(i, nt - 1)
    prv = lambda i: jnp.maximum(i - 1, 0)
    tab_a, tab_b = _rope_block_tables(t // WINDOW)
    return pl.pallas_call(
        _attn_ffn_kernel,
        grid=(nt + 1,),
        in_specs=[
            pl.BlockSpec((bsz, tq, D_ATTN), lambda i: (0, cur(i), 0)),
            pl.BlockSpec((bsz, tq, 2 * D_KV), lambda i: (0, cur(i), 0)),
            pl.BlockSpec((nblk, 8, 4 * HEAD_DIM), lambda i: (cur(i), 0, 0)),
            const((WINDOW, 4 * HEAD_DIM)),
            const((1, D_ATTN)),
            const((1, D_KV)),
            const((1, N_Q_HEADS)),
            const((QUAD, QUAD)),
            const((2, 2 * WINDOW, GQA_GROUP * WINDOW)),
            pl.BlockSpec((bsz, tq, D_MODEL), lambda i: (0, prv(i), 0)),
            pl.BlockSpec((bsz, tq, D_RWKV), lambda i: (0, prv(i), 0)),
            const((nd, D_MODEL)),
            const((nd, D_RWKV)),
            const((nd, D_ATTN)),
            single((D_MODEL, D_MODEL)),
            const((1, D_MODEL)),
            single((D_MODEL, D_FF)),
            single((D_FF, D_MODEL)),
        ],
        out_specs=[
            pl.BlockSpec((bsz, tq, D_MODEL), lambda i: (0, prv(i), 0)),
            const((nd, D_MODEL)),
            const((bsz, WINDOW, D_KV)),
            const((bsz, WINDOW, D_KV)),
        ],
        out_shape=[
            jax.ShapeDtypeStruct((bsz, t, D_MODEL), F32),
            jax.ShapeDtypeStruct((nd, D_MODEL), F32),
            jax.ShapeDtypeStruct((bsz, WINDOW, D_KV), F32),
            jax.ShapeDtypeStruct((bsz, WINDOW, D_KV), F32),
        ],
        scratch_shapes=[
            pltpu.VMEM((bsz, WINDOW, D_KV), F32),
            pltpu.VMEM((bsz, WINDOW, D_KV), F32),
            pltpu.VMEM((bsz, tq, D_ATTN), F32),
        ],
        compiler_params=pltpu.CompilerParams(
            dimension_semantics=("arbitrary",), vmem_limit_bytes=VMEM_LIMIT),
        name="attn_ffn",
    )(q3d, kv3d, tab_a, tab_b, qw, kw, sinks, ones_bd, bias, x3d, yr3d, xd, yrd, yad, wo, nw, wu, wd)


DEC_TILE = 16


def _decode_prep_kernel(p_ref, sh_ref, q_ref, kv_ref, mu_ref, prm_ref, wl_ref, ones_ref, tab_ref,
                        qw_ref, kw_ref, vec_ref, vgb_ref, qn_ref, kvn_ref):
    p = p_ref[...]
    xs = p + (sh_ref[...] - p) * mu_ref[...]
    r, logw, k, v, a, b, g, bonus = _rwkv_features(xs, prm_ref[...], wl_ref, ones_ref)
    for i, x in enumerate((a, b, k, jnp.exp(logw), r, v)):
        vec_ref[i] = jnp.transpose(x)
    vgb_ref[0] = g
    vgb_ref[1] = bonus
    n = p.shape[0]
    tab = jnp.broadcast_to(tab_ref[0:1, :], (n, 4 * HEAD_DIM))
    cos_t, sin_lo, sin_hi = _rope_tables(tab[:, :128], tab[:, 128:])
    qn_ref[...] = _qk_norm_rope(q_ref[...], qw_ref[...], _tile_lanes(cos_t, 4), _tile_lanes(sin_lo, 4),
                                _tile_lanes(sin_hi, 4), ones_ref)
    kv = kv_ref[...]
    kvn_ref[:, 0:D_KV] = _qk_norm_rope(kv[:, 0:D_KV], kw_ref[...], cos_t, sin_lo, sin_hi, ones_ref)
    kvn_ref[:, D_KV:] = kv[:, D_KV:]


def _decode_prep(p, shift, q, kv, mu_pad, prm, wl, bmask, tab, qw, kw):
    n = p.shape[0]
    full = lambda shape: pl.BlockSpec(shape, lambda i: (0,) * len(shape))
    return pl.pallas_call(
        _decode_prep_kernel,
        grid=(1,),
        in_specs=[full((n, D_SHIFT_PAD)), full((n, D_SHIFT_PAD)), full((n, D_ATTN)), full((n, 2 * D_KV)),
                  full((1, D_SHIFT_PAD)), full((16, D_RWKV)), full((D_LORA_PAD, 3 * D_RWKV)),
                  full((QUAD, QUAD)), full((8, 4 * HEAD_DIM)), full((1, D_ATTN)), full((1, D_KV))],
        out_specs=[full((6, D_RWKV, n)), full((2, n, D_RWKV)), full((n, D_ATTN)), full((n, 2 * D_KV))],
        out_shape=[
            jax.ShapeDtypeStruct((6, D_RWKV, n), F32),
            jax.ShapeDtypeStruct((2, n, D_RWKV), F32),
            jax.ShapeDtypeStruct((n, D_ATTN), F32),
            jax.ShapeDtypeStruct((n, 2 * D_KV), F32),
        ],
        compiler_params=pltpu.CompilerParams(
            dimension_semantics=("arbitrary",), vmem_limit_bytes=VMEM_LIMIT),
        name="decode_prep",
    )(p, shift, q, kv, mu_pad, prm, wl, bmask, tab, qw, kw)


def _decode_state_kernel(vec_ref, gb_ref, prm_ref, ones_ref, s_ref, sout_ref, yr_ref, yt_ref):
    h = pl.program_id(0)
    a_t, b_t, k_t, w_t, r_t = (vec_ref[i] for i in range(5))

    def body(i, carry):
        s = s_ref[0, i]
        sa = jnp.sum(s * a_t, axis=0, keepdims=True)
        v_i = vec_ref[5, pl.ds(i, 1), :]
        s_new = s * w_t + sa * b_t + v_i * k_t
        sout_ref[0, i] = s_new
        yt_ref[pl.ds(h * HEAD_DIM + i, 1), :] = jnp.sum(s_new * r_t, axis=0, keepdims=True)
        return carry

    lax.fori_loop(0, HEAD_DIM, body, 0, unroll=8)

    @pl.when(h == pl.num_programs(0) - 1)
    def _():
        y = jnp.transpose(yt_ref[...])
        yr_ref[...] = _rwkv_finish(y, gb_ref[0], gb_ref[1], prm_ref[...], ones_ref)


def _decode_state(vec_t, gb, prm, bmask, s_t):
    n = s_t.shape[-1]
    const = lambda shape: pl.BlockSpec(shape, lambda h: (0,) * len(shape))
    return pl.pallas_call(
        _decode_state_kernel,
        grid=(H_RWKV,),
        in_specs=[
            pl.BlockSpec((6, HEAD_DIM, n), lambda h: (0, h, 0)),
            const((2, n, D_RWKV)),
            const((16, D_RWKV)),
            const((QUAD, QUAD)),
            pl.BlockSpec((1, HEAD_DIM, HEAD_DIM, n), lambda h: (h, 0, 0, 0)),
        ],
        out_specs=[
            pl.BlockSpec((1, HEAD_DIM, HEAD_DIM, n), lambda h: (h, 0, 0, 0)),
            const((n, D_RWKV)),
        ],
        out_shape=[
            jax.ShapeDtypeStruct((H_RWKV, HEAD_DIM, HEAD_DIM, n), F32),
            jax.ShapeDtypeStruct((n, D_RWKV), F32),
        ],
        scratch_shapes=[pltpu.VMEM((D_RWKV, n), F32)],
        compiler_params=pltpu.CompilerParams(
            dimension_semantics=("arbitrary",), vmem_limit_bytes=VMEM_LIMIT),
        name="decode_state",
    )(vec_t, gb, prm, bmask, s_t)


def _decode_attn_kernel(qr_ref, kvn_ref, col_ref, ck_ref, cv_ref, sink_ref, ya_ref, kout_ref, vout_ref):
    nh = N_Q_HEADS
    seqs = range(DEC_TILE)
    hrow = lax.broadcasted_iota(jnp.int32, (nh, D_ATTN), 0)
    hlane = lax.broadcasted_iota(jnp.int32, (nh, D_ATTN), 1) // HEAD_DIM
    dmask = hrow == hlane
    grow = lax.broadcasted_iota(jnp.int32, (nh, D_KV), 0) // GQA_GROUP
    glane = lax.broadcasted_iota(jnp.int32, (nh, D_KV), 1) // HEAD_DIM
    gmask = grow == glane
    low = glane == 0
    key_idx = lax.broadcasted_iota(jnp.int32, (nh, WINDOW), 1)
    last = lax.broadcasted_iota(jnp.int32, (D_KV, WINDOW), 1) == WINDOW - 1
    sink = sink_ref[...]
    kvn = kvn_ref[...]
    col = col_ref[0]
    k_new = [kvn[j:j + 1, 0:D_KV] for j in seqs]
    v_new = [kvn[j:j + 1, D_KV:] for j in seqs]
    ck = [ck_ref[j] for j in seqs]
    cv = [cv_ref[j] for j in seqs]
    for j in seqs:
        kout_ref[j] = jnp.where(last, col[0:D_KV, j:j + 1], pltpu.roll(ck[j], WINDOW - 1, 1))
        vout_ref[j] = jnp.where(last, col[D_KV:, j:j + 1], pltpu.roll(cv[j], WINDOW - 1, 1))
    q8 = [qr_ref[j * nh:(j + 1) * nh, :] for j in seqs]
    qp = [jnp.where(gmask, jnp.concatenate([q8[j], q8[j]], axis=1), 0.0) for j in seqs]
    s_c = [jnp.where(key_idx >= 1, jnp.dot(qp[j], ck[j], preferred_element_type=F32) * ATTN_SCALE, NEG_INF)
           for j in seqs]
    s_n = [jnp.sum(qp[j] * k_new[j], axis=-1, keepdims=True) * ATTN_SCALE for j in seqs]
    m = [jnp.maximum(jnp.maximum(jnp.max(s_c[j], axis=-1, keepdims=True), s_n[j]), sink) for j in seqs]
    e_c = [jnp.exp(s_c[j] - m[j]) for j in seqs]
    e_n = [jnp.exp(s_n[j] - m[j]) for j in seqs]
    denom = [jnp.sum(e_c[j], axis=-1, keepdims=True) + e_n[j] + jnp.exp(sink - m[j]) for j in seqs]
    o = [(_dot_nt_f32(e_c[j], cv[j]) + e_n[j] * v_new[j]) / denom[j]
         for j in seqs]
    out_rows = []
    for j in seqs:
        rot = pltpu.roll(o[j], HEAD_DIM, 1)
        g0 = jnp.where(low, o[j], rot)
        g1 = jnp.where(low, rot, o[j])
        wide = jnp.concatenate([g0, g0, g1, g1], axis=1)
        out_rows.append(jnp.sum(jnp.where(dmask, wide, 0.0), axis=0, keepdims=True))
    ya_ref[...] = jnp.concatenate(out_rows, axis=0)


def _decode_attn(q_r, kvn, cols, ck_t, cv_t, sinks_col):
    n = kvn.shape[0]
    bt = DEC_TILE
    const = lambda shape: pl.BlockSpec(shape, lambda i: (0,) * len(shape))
    return pl.pallas_call(
        _decode_attn_kernel,
        grid=(n // bt,),
        in_specs=[
            pl.BlockSpec((bt * N_Q_HEADS, HEAD_DIM), lambda i: (i, 0)),
            pl.BlockSpec((bt, 2 * D_KV), lambda i: (i, 0)),
            pl.BlockSpec((1, 2 * D_KV, bt), lambda i: (i, 0, 0)),
            pl.BlockSpec((bt, D_KV, WINDOW), lambda i: (i, 0, 0)),
            pl.BlockSpec((bt, D_KV, WINDOW), lambda i: (i, 0, 0)),
            const((N_Q_HEADS, 1)),
        ],
        out_specs=[
            pl.BlockSpec((bt, D_ATTN), lambda i: (i, 0)),
            pl.BlockSpec((bt, D_KV, WINDOW), lambda i: (i, 0, 0)),
            pl.BlockSpec((bt, D_KV, WINDOW), lambda i: (i, 0, 0)),
        ],
        out_shape=[
            jax.ShapeDtypeStruct((n, D_ATTN), F32),
            jax.ShapeDtypeStruct((n, D_KV, WINDOW), F32),
            jax.ShapeDtypeStruct((n, D_KV, WINDOW), F32),
        ],
        compiler_params=pltpu.CompilerParams(
            dimension_semantics=("arbitrary",), vmem_limit_bytes=VMEM_LIMIT),
        name="decode_attn",
    )(q_r, kvn, cols, ck_t, cv_t, sinks_col)


def _pad_cols(w, at, n):
    return jnp.concatenate([w[..., :at], jnp.zeros(w.shape[:-1] + (n,), w.dtype), w[..., at:]], axis=-1)


def kernel(x_prompt, x_sample, state_wkv, state_shift, cache_k_win, cache_v_win, norm_mix_w, w_in, mu_shift, w0, w_decay_up, a0, w_a_up, w_g_up, k_k, k_a, r_k, ln_x_w, ln_x_b, q_norm_w, k_norm_w, sinks, w_out, norm_ffn_w, w_ffn_up, w_ffn_down):
    bsz, t, _ = x_prompt.shape
    nd = x_sample.shape[0]
    l = 0
    pad = D_LORA_PAD - D_LORA

    w_in_t = jnp.swapaxes(w_in[l], 0, 1)
    w_in_pad = jnp.concatenate([w_in_t[:D_SHIFT], jnp.zeros((pad, D_MODEL), F32), w_in_t[D_SHIFT:]],
                               axis=0).astype(BF16)
    mu_pad = _pad_cols(mu_shift[l][None, :], D_SHIFT, pad)
    wl = jnp.zeros((D_LORA_PAD, 3 * D_RWKV), F32)
    wl = wl.at[0:32, 0:D_RWKV].set(w_decay_up[l])
    wl = wl.at[32:64, D_RWKV:2 * D_RWKV].set(w_a_up[l])
    wl = wl.at[64:160, 2 * D_RWKV:].set(w_g_up[l])
    wl = wl.astype(BF16)
    prm = jnp.zeros((16, D_RWKV), F32)
    prm = prm.at[0].set(w0[l]).at[1].set(a0[l]).at[2].set(k_k[l]).at[3].set(k_a[l])
    prm = prm.at[4].set(r_k[l].reshape(-1)).at[5].set(ln_x_w[l]).at[6].set(ln_x_b[l])
    hid = np.arange(QUAD) // HEAD_DIM
    bmask = jnp.asarray(hid[:, None] == hid[None, :], BF16)
    tri = jnp.asarray(np.tril(np.ones((CHUNK, CHUNK))), BF16)
    qw = jnp.tile(q_norm_w[l][None, :], (1, N_Q_HEADS))
    kw = jnp.tile(k_norm_w[l][None, :], (1, N_KV_HEADS))
    nmw = norm_mix_w[l][None, :]
    nfw = norm_ffn_w[l][None, :]
    wo = w_out[l].astype(BF16)
    wu = w_ffn_up[l].astype(BF16)
    wd = w_ffn_down[l].astype(BF16)
    ang_s = (jnp.zeros((8, 1), F32) + PAST_LEN) * _rope_lane_freq()
    tab_s = jnp.concatenate([jnp.cos(ang_s), jnp.sin(ang_s)], axis=1)

    xs = x_sample.reshape(nd, D_MODEL)
    p_s, q_s, kv_s = _inproj(xs, nmw, w_in_pad, 128)
    shift_in = _pad_cols(state_shift[l].reshape(nd, D_SHIFT), D_SHIFT, pad)
    vec_t, gb, qn_s, kvn_s = _decode_prep(p_s, shift_in, q_s, kv_s, mu_pad, prm, wl, bmask, tab_s, qw, kw)
    s_t = jnp.transpose(state_wkv[l], (1, 2, 3, 0))
    ck_t = jnp.swapaxes(cache_k_win[l].reshape(nd, WINDOW, D_KV), 1, 2)
    cv_t = jnp.swapaxes(cache_v_win[l].reshape(nd, WINDOW, D_KV), 1, 2)
    q_r = qn_s.reshape(nd * N_Q_HEADS, HEAD_DIM)
    cols = jnp.swapaxes(kvn_s.reshape(nd // DEC_TILE, DEC_TILE, 2 * D_KV), 1, 2)
    wkv_t, yr_s = _decode_state(vec_t, gb, prm, bmask, s_t)
    ya_s, kc_t, vc_t = _decode_attn(q_r, kvn_s, cols, ck_t, cv_t, sinks[l][:, None])

    xp = x_prompt.reshape(bsz * t, D_MODEL)
    xs_p, plast, q_p, kv_p = _inproj_shift(xp, nmw, w_in_pad, mu_pad, t, 1024)
    yr_p, hbd = _rwkv_prompt(xs_p.reshape(bsz, t, D_SHIFT_PAD), prm, wl, tri, bmask)
    y_prompt, y_s, kwin_p, vwin_p = _attn_ffn(q_p.reshape(bsz, t, D_ATTN), kv_p.reshape(bsz, t, 2 * D_KV),
                                              qw, kw, sinks[l][None, :], bmask, _band_bias(),
                                              x_prompt, yr_p, xs, yr_s, ya_s, wo, nfw, wu, wd)
    hb = hbd.reshape(bsz, 2, 4, HEAD_DIM, 2, HEAD_DIM)
    wkv_prompt = jnp.stack([hb[:, :, j, :, j % 2, :] for j in range(4)], axis=2)
    wkv_prompt = wkv_prompt.reshape(bsz, H_RWKV, HEAD_DIM, HEAD_DIM)[None]
    shift_prompt = plast[:, 0:1, :D_SHIFT][None]
    k_win_prompt = kwin_p.reshape(bsz, WINDOW, N_KV_HEADS, HEAD_DIM)[None]
    v_win_prompt = vwin_p.reshape(bsz, WINDOW, N_KV_HEADS, HEAD_DIM)[None]

    y_sample = y_s.reshape(nd, 1, D_MODEL)
    wkv_sample = jnp.transpose(wkv_t, (3, 0, 1, 2))[None]
    shift_sample = p_s[:, :D_SHIFT].reshape(nd, 1, D_SHIFT)[None]
    k_win_sample = jnp.swapaxes(kc_t, 1, 2).reshape(nd, WINDOW, N_KV_HEADS, HEAD_DIM)[None]
    v_win_sample = jnp.swapaxes(vc_t, 1, 2).reshape(nd, WINDOW, N_KV_HEADS, HEAD_DIM)[None]

    return (y_prompt, y_sample, wkv_prompt, shift_prompt, k_win_prompt, v_win_prompt,
            wkv_sample, shift_sample, k_win_sample, v_win_sample)
```

```python
import functools

import jax
import jax.numpy as jnp
import numpy as np
from jax import lax
from jax.experimental import pallas as pl
from jax.experimental.pallas import tpu as pltpu

F32 = jnp.float32
BF16 = jnp.bfloat16

D_MODEL = 1024
D_RWKV = 512
D_ATTN = 512
HEAD_DIM = 64
H_RWKV = 8
N_Q_HEADS = 8
N_KV_HEADS = 2
GQA_GROUP = 4
D_KV = 128
D_LORA = 160
D_LORA_PAD = 256
D_SHIFT = 3 * D_RWKV + D_LORA
D_SHIFT_PAD = 3 * D_RWKV + D_LORA_PAD
D_IN_PAD = D_SHIFT_PAD + D_ATTN + 2 * D_KV
WINDOW = 128
ROPE_DIM = 16
ROPE_HALF = 8
ROPE_THETA = 500000.0
ATTN_SCALE = HEAD_DIM ** -0.5
D_FF = 4096
RMS_EPS = 1e-6
LNX_EPS = 64e-5
NEG_INF = -1e30
LOG2E = 1.4426950408889634
PAST_LEN = 16384

CHUNK = 64
QUAD = 4 * HEAD_DIM
VMEM_LIMIT = 56 * 1024 * 1024


def _split2(x):
    hi = x.astype(BF16)
    lo = (x - hi.astype(F32)).astype(BF16)
    return hi, lo


def _head_sums(xs, ones_ref):
    n, w = xs[0].shape
    tile = min(w, QUAD)
    per = w // tile
    pieces = [x[:, j * tile:(j + 1) * tile] for x in xs for j in range(per)]
    stacked = jnp.concatenate(pieces, axis=0) if len(pieces) > 1 else pieces[0]
    ones = ones_ref[0:tile, 0:tile]
    out = jnp.dot(stacked.astype(BF16), ones, preferred_element_type=F32)
    res = []
    for i in range(len(xs)):
        cols = [out[(i * per + j) * n:(i * per + j + 1) * n] for j in range(per)]
        res.append(jnp.concatenate(cols, axis=1) if per > 1 else cols[0])
    return res


def _cumsum_rows(tri_bf16, x):
    hi, lo = _split2(x)
    return (jnp.dot(tri_bf16, hi, preferred_element_type=F32)
            + jnp.dot(tri_bf16, lo, preferred_element_type=F32))


def _mm(a, b):
    return jnp.dot(a.astype(BF16), b.astype(BF16), preferred_element_type=F32)


def _mm_nt(a, b):
    return lax.dot_general(a.astype(BF16), b.astype(BF16), (((1,), (1,)), ((), ())),
                           preferred_element_type=F32)


def _mm_tn(a, b):
    return lax.dot_general(a.astype(BF16), b.astype(BF16), (((0,), (0,)), ((), ())),
                           preferred_element_type=F32)


def _dot_nt_f32(a, b):
    return lax.dot_general(a, b, (((1,), (1,)), ((), ())), preferred_element_type=F32)


def _sigmoid(x):
    return 1.0 / (1.0 + jnp.exp(-x))


def _interleave(*gens):
    live = list(gens)
    while live:
        for g in list(live):
            try:
                next(g)
            except StopIteration:
                live.remove(g)


def _norm_project(x, nw_ref, wt_ref):
    ms = jnp.mean(x * x, axis=-1, keepdims=True)
    h = (x * lax.rsqrt(ms + RMS_EPS)) * nw_ref[...]
    return _mm_nt(h, wt_ref[...])


def _inproj_kernel(x_ref, nw_ref, w_ref, p_ref, q_ref, kv_ref):
    out = _norm_project(x_ref[...], nw_ref, w_ref)
    p_ref[...] = out[:, :D_SHIFT_PAD]
    q_ref[...] = out[:, D_SHIFT_PAD:D_SHIFT_PAD + D_ATTN]
    kv_ref[...] = out[:, D_SHIFT_PAD + D_ATTN:]


def _inproj_shift_kernel(tiles_per_seq, x_ref, nw_ref, w_ref, mu_ref, xs_ref, last_ref, q_ref, kv_ref, prev_ref):
    i = pl.program_id(0)

    @pl.when(i % tiles_per_seq == 0)
    def _():
        prev_ref[...] = jnp.zeros_like(prev_ref)

    out = _norm_project(x_ref[...], nw_ref, w_ref)
    p = out[:, :D_SHIFT_PAD]
    tm = p.shape[0]
    row = lax.broadcasted_iota(jnp.int32, p.shape, 0)
    prev = jnp.where(row == 0, jnp.broadcast_to(prev_ref[0:1, :], p.shape), pltpu.roll(p, 1, 0))
    xs_ref[...] = p + (prev - p) * mu_ref[...]
    last = jnp.broadcast_to(p[tm - 1:tm, :], prev_ref.shape)
    prev_ref[...] = last
    last_ref[0] = last
    q_ref[...] = out[:, D_SHIFT_PAD:D_SHIFT_PAD + D_ATTN]
    kv_ref[...] = out[:, D_SHIFT_PAD + D_ATTN:]


def _inproj_shift(x2d, norm_w, w_in_pad, mu_pad, seq_len, tm):
    m = x2d.shape[0]
    tiles_per_seq = seq_len // tm
    return pl.pallas_call(
        functools.partial(_inproj_shift_kernel, tiles_per_seq),
        grid=(m // tm,),
        in_specs=[
            pl.BlockSpec((tm, D_MODEL), lambda i: (i, 0)),
            pl.BlockSpec((1, D_MODEL), lambda i: (0, 0)),
            pl.BlockSpec((D_IN_PAD, D_MODEL), lambda i: (0, 0)),
            pl.BlockSpec((1, D_SHIFT_PAD), lambda i: (0, 0)),
        ],
        out_specs=[
            pl.BlockSpec((tm, D_SHIFT_PAD), lambda i: (i, 0)),
            pl.BlockSpec((1, 8, D_SHIFT_PAD), lambda i: (i // tiles_per_seq, 0, 0)),
            pl.BlockSpec((tm, D_ATTN), lambda i: (i, 0)),
            pl.BlockSpec((tm, 2 * D_KV), lambda i: (i, 0)),
        ],
        out_shape=[
            jax.ShapeDtypeStruct((m, D_SHIFT_PAD), F32),
            jax.ShapeDtypeStruct((m // seq_len, 8, D_SHIFT_PAD), F32),
            jax.ShapeDtypeStruct((m, D_ATTN), F32),
            jax.ShapeDtypeStruct((m, 2 * D_KV), F32),
        ],
        scratch_shapes=[pltpu.VMEM((8, D_SHIFT_PAD), F32)],
        compiler_params=pltpu.CompilerParams(
            dimension_semantics=("arbitrary",), vmem_limit_bytes=VMEM_LIMIT),
        name="inproj_shift",
    )(x2d, norm_w, w_in_pad, mu_pad)


def _inproj(x2d, norm_w, w_in_pad, tm):
    m = x2d.shape[0]
    return pl.pallas_call(
        _inproj_kernel,
        grid=(m // tm,),
        in_specs=[
            pl.BlockSpec((tm, D_MODEL), lambda i: (i, 0)),
            pl.BlockSpec((1, D_MODEL), lambda i: (0, 0)),
            pl.BlockSpec((D_IN_PAD, D_MODEL), lambda i: (0, 0)),
        ],
        out_specs=[
            pl.BlockSpec((tm, D_SHIFT_PAD), lambda i: (i, 0)),
            pl.BlockSpec((tm, D_ATTN), lambda i: (i, 0)),
            pl.BlockSpec((tm, 2 * D_KV), lambda i: (i, 0)),
        ],
        out_shape=[
            jax.ShapeDtypeStruct((m, D_SHIFT_PAD), F32),
            jax.ShapeDtypeStruct((m, D_ATTN), F32),
            jax.ShapeDtypeStruct((m, 2 * D_KV), F32),
        ],
        compiler_params=pltpu.CompilerParams(
            dimension_semantics=("arbitrary",), vmem_limit_bytes=VMEM_LIMIT),
        name="inproj",
    )(x2d, norm_w, w_in_pad)


def _rwkv_features(xs, prm, wl_ref, ones_ref):
    r = xs[:, 0:D_RWKV]
    k = xs[:, D_RWKV:2 * D_RWKV]
    v = xs[:, 2 * D_RWKV:3 * D_RWKV]
    lora = xs[:, 3 * D_RWKV:]
    col = lax.broadcasted_iota(jnp.int32, lora.shape, 1)
    act = jnp.where(col < 32, jnp.tanh(lora), jnp.where(col < 64, lora, _sigmoid(lora)))
    up = jnp.dot(act.astype(BF16), wl_ref[...], preferred_element_type=F32)
    w0, a0, k_k, k_a, r_k = prm[0:1], prm[1:2], prm[2:3], prm[3:4], prm[4:5]
    logw = (-np.exp(-0.5)) * _sigmoid(w0 + up[:, 0:D_RWKV])
    asig = _sigmoid(a0 + up[:, D_RWKV:2 * D_RWKV])
    g = up[:, 2 * D_RWKV:]
    kk = k * k_k
    k_mod = k * (1.0 + (asig - 1.0) * k_a)
    ss, rk = _head_sums([kk * kk, r * k_mod * r_k], ones_ref)
    kk = kk / jnp.maximum(jnp.sqrt(ss), 1e-12)
    k = k_mod
    bonus = rk * v
    return r, logw, k, v, -kk, kk * asig, g, bonus


def _rwkv_finish(y, g, bonus, prm, ones_ref):
    ln_w, ln_b = prm[5:6], prm[6:7]
    mean = _head_sums([y], ones_ref)[0] * (1.0 / HEAD_DIM)
    d = y - mean
    var = _head_sums([d * d], ones_ref)[0] * (1.0 / HEAD_DIM)
    yn = d * lax.rsqrt(var + LNX_EPS) * ln_w + ln_b
    return (yn + bonus) * g


def _block_diag(x, bmask):
    return jnp.concatenate([x] * 4, axis=0) * bmask


def _chunk_prep(r, logw, k, v, a, b, tri_ref, bmask):
    n = len(r)
    ch = range(n)
    tri = tri_ref[...]
    cum = [_cumsum_rows(tri, logw[i]) for i in ch]
    e_in = [jnp.exp(cum[i]) for i in ch]
    e_ex = [jnp.exp(cum[i] - logw[i]) for i in ch]
    e_inv = [1.0 / e_in[i] for i in ch]
    e_last = [e_in[i][CHUNK - 1:CHUNK, :] for i in ch]
    rt = [(r[i] * e_in[i]).astype(BF16) for i in ch]
    at = [(a[i] * e_ex[i]).astype(BF16) for i in ch]
    kt = [(k[i] * e_inv[i]).astype(BF16) for i in ch]
    bt = [(b[i] * e_inv[i]).astype(BF16) for i in ch]
    vb = [v[i].astype(BF16) for i in ch]

    t_idx = lax.broadcasted_iota(jnp.int32, (CHUNK, QUAD), 0)
    s_idx = lax.broadcasted_iota(jnp.int32, (CHUNK, QUAD), 1) & (HEAD_DIM - 1)
    strict = s_idx < t_idx
    incl = s_idx <= t_idx

    gm = [_mm_nt(jnp.concatenate([at[i], rt[i]], axis=0),
                 jnp.concatenate([_block_diag(bt[i], bmask), _block_diag(kt[i], bmask)], axis=0))
          for i in ch]
    a_ab = [jnp.where(strict, gm[i][:CHUNK, :QUAD], 0.0) for i in ch]
    a_ak = [jnp.where(strict, gm[i][:CHUNK, QUAD:], 0.0) for i in ch]
    a_rb = [jnp.where(incl, gm[i][CHUNK:, :QUAD], 0.0) for i in ch]
    a_rk = [jnp.where(incl, gm[i][CHUNK:, QUAD:], 0.0) for i in ch]

    eye = jnp.where(s_idx == t_idx, 1.0, 0.0)
    pwb = [a_ab[i].astype(BF16) for i in ch]
    t_inv = [eye + a_ab[i] for i in ch]
    for it in range(6):
        rbd = [_block_diag(pwb[i], bmask) for i in ch]
        if it == 0:
            pwb = [_mm(pwb[i], rbd[i]).astype(BF16) for i in ch]
        elif it < 5:
            out = [_mm(jnp.concatenate([pwb[i], t_inv[i].astype(BF16)], axis=0), rbd[i]) for i in ch]
            pwb = [out[i][:CHUNK].astype(BF16) for i in ch]
            t_inv = [t_inv[i] + out[i][CHUNK:] for i in ch]
        else:
            t_inv = [t_inv[i] + _mm(t_inv[i], rbd[i]) for i in ch]

    vbd = [_block_diag(vb[i], bmask) for i in ch]
    xy0 = [_mm(jnp.concatenate([a_ak[i], a_rk[i]], axis=0), vbd[i]) for i in ch]
    return [dict(ar=jnp.concatenate([at[i], rt[i]], axis=0), x0=xy0[i][:CHUNK], y0=xy0[i][CHUNK:],
                 t_inv=t_inv[i].astype(BF16), a_rb=a_rb[i].astype(BF16), vb=vb[i],
                 bk=jnp.concatenate([bt[i], kt[i]], axis=0), e_last=e_last[i]) for i in ch]


def _chunk_step(pre, state, bmask):
    ch = range(len(pre))
    half = QUAD // 2
    zeros = jnp.zeros((half, half), BF16)
    sc = [state[i].astype(BF16) for i in ch]
    sb = [jnp.concatenate([jnp.concatenate([sc[i][:half], zeros], axis=1),
                           jnp.concatenate([zeros, sc[i][half:]], axis=1)], axis=0) for i in ch]
    xr = [_mm_nt(pre[i]["ar"], sb[i]) for i in ch]
    x = [xr[i][:CHUNK] + pre[i]["x0"] for i in ch]
    u = [_mm(pre[i]["t_inv"], _block_diag(x[i].astype(BF16), bmask)) for i in ch]
    ub = [u[i].astype(BF16) for i in ch]
    y = [xr[i][CHUNK:] + pre[i]["y0"] + _mm(pre[i]["a_rb"], _block_diag(ub[i], bmask)) for i in ch]
    upd = [_mm_tn(jnp.concatenate([ub[i], pre[i]["vb"]], axis=0), pre[i]["bk"]) for i in ch]
    bm = bmask[:half, :half].astype(F32)
    s_new = []
    for i in ch:
        e_last = pre[i]["e_last"]
        top = (state[i][:half] + upd[i][:half, :half] * bm) * e_last[:, :half]
        bot = (state[i][half:] + upd[i][half:, half:] * bm) * e_last[:, half:]
        s_new.append(jnp.concatenate([top, bot], axis=0))
    return y, s_new


def _rwkv_prompt_kernel(xs_ref, prm_ref, wl_ref, tri_ref, bmask_ref, y_ref, hout_ref, h_ref):
    c = pl.program_id(0)
    nseq, tstep, _ = xs_ref.shape
    nchunk = tstep // CHUNK

    @pl.when(c == 0)
    def _():
        h_ref[...] = jnp.zeros_like(h_ref)

    xs = jnp.concatenate([xs_ref[s] for s in range(nseq)], axis=0)
    prm = prm_ref[...]
    r, logw, k, v, a, b, g, bonus = _rwkv_features(xs, prm, wl_ref, bmask_ref)
    bmask = bmask_ref[...]
    lanes = [(s, q) for s in range(nseq) for q in range(2)]
    chains = [(j, s, q) for j in range(nchunk) for s, q in lanes]
    cut = lambda x: [x[s * tstep + j * CHUNK:s * tstep + (j + 1) * CHUNK, q * QUAD:(q + 1) * QUAD]
                     for j, s, q in chains]
    pre = _chunk_prep(cut(r), cut(logw), cut(k), cut(v), cut(a), cut(b), tri_ref, bmask)
    state = [h_ref[s, q] for s, q in lanes]
    ys = []
    for j in range(nchunk):
        y_j, state = _chunk_step(pre[j * len(lanes):(j + 1) * len(lanes)], state, bmask)
        ys.append(y_j)
    for i, (s, q) in enumerate(lanes):
        h_ref[s, q] = state[i]
    rows = [jnp.concatenate(ys[j][2 * s:2 * s + 2], axis=1) for s in range(nseq) for j in range(nchunk)]
    y = _rwkv_finish(jnp.concatenate(rows, axis=0), g, bonus, prm, bmask_ref)
    for s in range(nseq):
        y_ref[s] = y[s * tstep:(s + 1) * tstep]

    @pl.when(c == pl.num_programs(0) - 1)
    def _():
        hout_ref[...] = h_ref[...]


RWKV_CHUNKS_PER_STEP = 8


def _rwkv_prompt(xs3d, prm, wl, tri, bmask):
    bsz, t, _ = xs3d.shape
    tstep = RWKV_CHUNKS_PER_STEP * CHUNK
    nc = t // tstep
    const = lambda shape: pl.BlockSpec(shape, lambda c: (0,) * len(shape))
    state_shape = (bsz, 2, QUAD, QUAD // 2)
    return pl.pallas_call(
        _rwkv_prompt_kernel,
        grid=(nc,),
        in_specs=[
            pl.BlockSpec((bsz, tstep, D_SHIFT_PAD), lambda c: (0, c, 0)),
            const((16, D_RWKV)),
            const((D_LORA_PAD, 3 * D_RWKV)),
            const((CHUNK, CHUNK)),
            const((QUAD, QUAD)),
        ],
        out_specs=[
            pl.BlockSpec((bsz, tstep, D_RWKV), lambda c: (0, c, 0)),
            const(state_shape),
        ],
        out_shape=[
            jax.ShapeDtypeStruct((bsz, t, D_RWKV), F32),
            jax.ShapeDtypeStruct(state_shape, F32),
        ],
        scratch_shapes=[pltpu.VMEM(state_shape, F32)],
        compiler_params=pltpu.CompilerParams(
            dimension_semantics=("arbitrary",), vmem_limit_bytes=VMEM_LIMIT),
        name="rwkv_prompt",
    )(xs3d, prm, wl, tri, bmask)


def _rope_lane_freq():
    inv_freq = jnp.power(ROPE_THETA, -jnp.arange(ROPE_HALF, dtype=F32) * (2.0 / ROPE_DIM))
    return inv_freq[(np.arange(2 * HEAD_DIM) % HEAD_DIM) % ROPE_HALF][None, :]


def _rope_tables(cos, sin):
    dim = lax.broadcasted_iota(jnp.int32, cos.shape, 1) & (HEAD_DIM - 1)
    cos_t = jnp.where(dim < ROPE_DIM, cos, 1.0)
    sin_lo = jnp.where(dim < ROPE_HALF, -sin, 0.0)
    sin_hi = jnp.where((dim >= ROPE_HALF) & (dim < ROPE_DIM), sin, 0.0)
    return cos_t, sin_lo, sin_hi


def _qk_norm_rope(x, norm_w, cos_t, sin_lo, sin_hi, ones_ref):
    ms = _head_sums([x * x], ones_ref)[0] * (1.0 / HEAD_DIM)
    xn = x * lax.rsqrt(ms + RMS_EPS) * norm_w
    width = x.shape[1]
    fwd = pltpu.roll(xn, width - ROPE_HALF, 1)
    bwd = pltpu.roll(xn, ROPE_HALF, 1)
    return xn * cos_t + fwd * sin_lo + bwd * sin_hi


def _tile_lanes(x, reps):
    return jnp.concatenate([x] * reps, axis=1) if reps > 1 else x


def _attn_stages(q_ref, kv_ref, taba_ref, tabb_ref, qw_ref, kw_ref, sink_ref, ones_ref, bias_ref, first_bias,
                 kprev_ref, vprev_ref, out):
    nseq, tq, _ = q_ref.shape
    nblk = tq // WINDOW
    units = [(s, b) for s in range(nseq) for b in range(nblk)]
    blk = lambda b: slice(b * WINDOW, (b + 1) * WINDOW)
    tb = tabb_ref[...]
    cos_b, sin_b = tb[:, :128], tb[:, 128:]
    rope, rope4 = [], []
    for b in range(nblk):
        ta = taba_ref[b][0:1, :]
        cos_a, sin_a = ta[:, :128], ta[:, 128:]
        tabs = _rope_tables(cos_a * cos_b - sin_a * sin_b, sin_a * cos_b + cos_a * sin_b)
        rope.append(tabs)
        rope4.append([_tile_lanes(x, 4) for x in tabs])
    q = {(s, b): _qk_norm_rope(q_ref[s, blk(b), :], qw_ref[...], *rope4[b], ones_ref) * (ATTN_SCALE * LOG2E)
         for s, b in units}
    kv = {(s, b): kv_ref[s, blk(b), :] for s, b in units}
    k_cur = {u: _qk_norm_rope(kv[u][:, 0:D_KV], kw_ref[...], *rope[u[1]], ones_ref) for u in units}
    v_cur = {u: kv[u][:, D_KV:] for u in units}
    k_all = {(s, b): jnp.concatenate([kprev_ref[s] if b == 0 else k_cur[(s, b - 1)], k_cur[(s, b)]], axis=0)
             for s, b in units}
    v_all = {(s, b): jnp.concatenate([vprev_ref[s] if b == 0 else v_cur[(s, b - 1)], v_cur[(s, b)]], axis=0)
             for s, b in units}
    out["k_cur"] = [k_cur[(s, nblk - 1)] for s in range(nseq)]
    out["v_cur"] = [v_cur[(s, nblk - 1)] for s in range(nseq)]
    for s in range(nseq):
        kprev_ref[s] = out["k_cur"][s]
        vprev_ref[s] = out["v_cur"][s]
    yield

    nk = 2 * WINDOW
    bias = [bias_ref[first_bias] if b == 0 else bias_ref[1] for b in range(nblk)]
    sinks = sink_ref[...] * LOG2E
    low = lax.broadcasted_iota(jnp.int32, (nk, D_KV), 1) < HEAD_DIM
    lane_blk = [ones_ref[j * HEAD_DIM:j * HEAD_DIM + 1, :] for j in range(GQA_GROUP)]

    chains = [(u, g) for u in units for g in range(N_KV_HEADS)]
    ch = range(len(chains))
    k_rot = {u: pltpu.roll(k_all[u], HEAD_DIM, 1) for u in units}
    k2 = [jnp.where(low, k_all[u], k_rot[u]) if g == 0 else jnp.where(low, k_rot[u], k_all[u])
          for u, g in chains]
    k4 = [jnp.concatenate([k2[c], k2[c]], axis=1).astype(BF16) for c in ch]
    vb = {u: v_all[u].astype(BF16) for u in units}
    qg = [q[u][:, g * QUAD:(g + 1) * QUAD].astype(BF16) for u, g in chains]
    qstack = [jnp.concatenate([qg[c] * lane_blk[j] for j in range(GQA_GROUP)], axis=0) for c in ch]
    sink_row = [jnp.concatenate(
        [jnp.broadcast_to(sinks[:, g * GQA_GROUP + j:g * GQA_GROUP + j + 1], (1, WINDOW))
         for j in range(GQA_GROUP)], axis=1) for u, g in chains]
    yield
    sc = [_mm_nt(k4[c], qstack[c]) + bias[u[1]] for c, (u, g) in enumerate(chains)]
    yield
    m = [jnp.maximum(jnp.max(sc[c], axis=0, keepdims=True), sink_row[c]) for c in ch]
    e = [jnp.exp2(sc[c] - m[c]) for c in ch]
    yield
    denom = [jnp.sum(e[c], axis=0, keepdims=True) + jnp.exp2(sink_row[c] - m[c]) for c in ch]
    ot = {(u, g): _mm_tn(vb[u], e[c].astype(BF16))[g * HEAD_DIM:(g + 1) * HEAD_DIM, :] * (1.0 / denom[c])
          for c, (u, g) in enumerate(chains)}
    yield
    ya = []
    for s in range(nseq):
        blocks = []
        for b in range(nblk):
            yt = jnp.concatenate([ot[((s, b), g)][:, j * WINDOW:(j + 1) * WINDOW]
                                  for g in range(N_KV_HEADS) for j in range(GQA_GROUP)], axis=0)
            blocks.append(jnp.transpose(yt))
        ya.append(jnp.concatenate(blocks, axis=0) if nblk > 1 else blocks[0])
    out["ya"] = ya


def _ffn_stages(x, yr, ya, wo_ref, nw_ref, wu_ref, wd_ref, out, pieces=4):
    mix = jnp.concatenate([yr, ya], axis=1).astype(BF16)
    x1 = x + jnp.dot(mix, wo_ref[...], preferred_element_type=F32)
    yield
    ms = jnp.mean(x1 * x1, axis=-1, keepdims=True)
    hf = ((x1 * lax.rsqrt(ms + RMS_EPS)) * nw_ref[...]).astype(BF16)
    acc = x1
    step = D_FF // pieces
    for j in range(pieces):
        up = jnp.dot(hf, wu_ref[:, j * step:(j + 1) * step], preferred_element_type=F32)
        yield
        act = jnp.square(jnp.maximum(up, 0.0)).astype(BF16)
        acc = acc + jnp.dot(act, wd_ref[j * step:(j + 1) * step, :], preferred_element_type=F32)
        yield
    out["y"] = acc


def _attn_ffn_kernel(q_ref, kv_ref, taba_ref, tabb_ref, qw_ref, kw_ref, sink_ref, ones_ref, bias_ref,
                     x_ref, yr_ref, xd_ref, yrd_ref, yad_ref, wo_ref, nw_ref, wu_ref, wd_ref,
                     o_ref, od_ref, kwin_ref, vwin_ref, kprev_ref, vprev_ref, ya_ref):
    i = pl.program_id(0)
    nseq, tq, _ = q_ref.shape
    seqs = range(nseq)
    first = i == 0

    @pl.when(first)
    def _():
        kprev_ref[...] = jnp.zeros_like(kprev_ref)
        vprev_ref[...] = jnp.zeros_like(vprev_ref)
        ya_ref[...] = jnp.zeros_like(ya_ref)

    def rows(ref, dec_ref):
        tile = jnp.concatenate([ref[s] for s in seqs], axis=0)
        dec = dec_ref[...]
        return jnp.where(first, jnp.concatenate([dec] * (tile.shape[0] // dec.shape[0]), axis=0), tile)

    a_out, f_out = {}, {}
    _interleave(
        _ffn_stages(rows(x_ref, xd_ref), rows(yr_ref, yrd_ref), rows(ya_ref, yad_ref),
                    wo_ref, nw_ref, wu_ref, wd_ref, f_out),
        _attn_stages(q_ref, kv_ref, taba_ref, tabb_ref, qw_ref, kw_ref, sink_ref, ones_ref, bias_ref,
                     jnp.minimum(i, 1), kprev_ref, vprev_ref, a_out))
    for s in seqs:
        o_ref[s] = f_out["y"][s * tq:(s + 1) * tq]
        ya_ref[s] = a_out["ya"][s]

    @pl.when(first)
    def _():
        od_ref[...] = f_out["y"][:od_ref.shape[0]]

    @pl.when(i == pl.num_programs(0) - 2)
    def _():
        for s in seqs:
            kwin_ref[s] = a_out["k_cur"][s]
            vwin_ref[s] = a_out["v_cur"][s]


def _band_bias():
    ki = np.arange(2 * WINDOW)[:, None]
    qi = (np.arange(GQA_GROUP * WINDOW) % WINDOW + WINDOW)[None, :]
    dq = qi - ki
    band = (dq >= 0) & (dq < WINDOW)
    first = band & (ki >= WINDOW)
    return jnp.asarray(np.where(np.stack([first, band]), 0.0, NEG_INF), F32)


def _rope_block_tables(nb):
    freq = _rope_lane_freq()
    ang_a = (jnp.arange(nb, dtype=F32) * WINDOW)[:, None] * freq
    ang_b = jnp.arange(WINDOW, dtype=F32)[:, None] * freq
    tab_a = jnp.concatenate([jnp.cos(ang_a), jnp.sin(ang_a)], axis=1)
    tab_b = jnp.concatenate([jnp.cos(ang_b), jnp.sin(ang_b)], axis=1)
    return jnp.broadcast_to(tab_a[:, None, :], (nb, 8, 4 * HEAD_DIM)), tab_b


ATTN_BLOCKS_PER_STEP = 1


def _attn_ffn(q3d, kv3d, qw, kw, sinks, ones_bd, bias, x3d, yr3d, xd, yrd, yad, wo, nw, wu, wd):
    bsz, t, _ = q3d.shape
    nd = xd.shape[0]
    nblk = ATTN_BLOCKS_PER_STEP
    tq = nblk * WINDOW
    nt = t // tq
    const = lambda shape: pl.BlockSpec(shape, lambda i: (0,) * len(shape))
    single = lambda shape: pl.BlockSpec(shape, lambda i: (0,) * len(shape), pipeline_mode=pl.Buffered(1))
    cur = lambda i: jnp.minimum(i, nt - 1)
    prv = lambda i: jnp.maximum(i - 1, 0)
    tab_a, tab_b = _rope_block_tables(t // WINDOW)
    return pl.pallas_call(
        _attn_ffn_kernel,
        grid=(nt + 1,),
        in_specs=[
            pl.BlockSpec((bsz, tq, D_ATTN), lambda i: (0, cur(i), 0)),
            pl.BlockSpec((bsz, tq, 2 * D_KV), lambda i: (0, cur(i), 0)),
            pl.BlockSpec((nblk, 8, 4 * HEAD_DIM), lambda i: (cur(i), 0, 0)),
            const((WINDOW, 4 * HEAD_DIM)),
            const((1, D_ATTN)),
            const((1, D_KV)),
            const((1, N_Q_HEADS)),
            const((QUAD, QUAD)),
            const((2, 2 * WINDOW, GQA_GROUP * WINDOW)),
            pl.BlockSpec((bsz, tq, D_MODEL), lambda i: (0, prv(i), 0)),
            pl.BlockSpec((bsz, tq, D_RWKV), lambda i: (0, prv(i), 0)),
            const((nd, D_MODEL)),
            const((nd, D_RWKV)),
            const((nd, D_ATTN)),
            single((D_MODEL, D_MODEL)),
            const((1, D_MODEL)),
            single((D_MODEL, D_FF)),
            single((D_FF, D_MODEL)),
        ],
        out_specs=[
            pl.BlockSpec((bsz, tq, D_MODEL), lambda i: (0, prv(i), 0)),
            const((nd, D_MODEL)),
            const((bsz, WINDOW, D_KV)),
            const((bsz, WINDOW, D_KV)),
        ],
        out_shape=[
            jax.ShapeDtypeStruct((bsz, t, D_MODEL), F32),
            jax.ShapeDtypeStruct((nd, D_MODEL), F32),
            jax.ShapeDtypeStruct((bsz, WINDOW, D_KV), F32),
            jax.ShapeDtypeStruct((bsz, WINDOW, D_KV), F32),
        ],
        scratch_shapes=[
            pltpu.VMEM((bsz, WINDOW, D_KV), F32),
            pltpu.VMEM((bsz, WINDOW, D_KV), F32),
            pltpu.VMEM((bsz, tq, D_ATTN), F32),
        ],
        compiler_params=pltpu.CompilerParams(
            dimension_semantics=("arbitrary",), vmem_limit_bytes=VMEM_LIMIT),
        name="attn_ffn",
    )(q3d, kv3d, tab_a, tab_b, qw, kw, sinks, ones_bd, bias, x3d, yr3d, xd, yrd, yad, wo, nw, wu, wd)


DEC_TILE = 16


def _decode_prep_kernel(p_ref, sh_ref, q_ref, kv_ref, mu_ref, prm_ref, wl_ref, ones_ref, tab_ref,
                        qw_ref, kw_ref, vec_ref, vgb_ref, qn_ref, kvn_ref):
    p = p_ref[...]
    xs = p + (sh_ref[...] - p) * mu_ref[...]
    r, logw, k, v, a, b, g, bonus = _rwkv_features(xs, prm_ref[...], wl_ref, ones_ref)
    for i, x in enumerate((a, b, k, jnp.exp(logw), r, v)):
        vec_ref[i] = jnp.transpose(x)
    vgb_ref[0] = g
    vgb_ref[1] = bonus
    n = p.shape[0]
    tab = jnp.broadcast_to(tab_ref[0:1, :], (n, 4 * HEAD_DIM))
    cos_t, sin_lo, sin_hi = _rope_tables(tab[:, :128], tab[:, 128:])
    qn_ref[...] = _qk_norm_rope(q_ref[...], qw_ref[...], _tile_lanes(cos_t, 4), _tile_lanes(sin_lo, 4),
                                _tile_lanes(sin_hi, 4), ones_ref)
    kv = kv_ref[...]
    kvn_ref[:, 0:D_KV] = _qk_norm_rope(kv[:, 0:D_KV], kw_ref[...], cos_t, sin_lo, sin_hi, ones_ref)
    kvn_ref[:, D_KV:] = kv[:, D_KV:]


def _decode_prep(p, shift, q, kv, mu_pad, prm, wl, bmask, tab, qw, kw):
    n = p.shape[0]
    full = lambda shape: pl.BlockSpec(shape, lambda i: (0,) * len(shape))
    return pl.pallas_call(
        _decode_prep_kernel,
        grid=(1,),
        in_specs=[full((n, D_SHIFT_PAD)), full((n, D_SHIFT_PAD)), full((n, D_ATTN)), full((n, 2 * D_KV)),
                  full((1, D_SHIFT_PAD)), full((16, D_RWKV)), full((D_LORA_PAD, 3 * D_RWKV)),
                  full((QUAD, QUAD)), full((8, 4 * HEAD_DIM)), full((1, D_ATTN)), full((1, D_KV))],
        out_specs=[full((6, D_RWKV, n)), full((2, n, D_RWKV)), full((n, D_ATTN)), full((n, 2 * D_KV))],
        out_shape=[
            jax.ShapeDtypeStruct((6, D_RWKV, n), F32),
            jax.ShapeDtypeStruct((2, n, D_RWKV), F32),
            jax.ShapeDtypeStruct((n, D_ATTN), F32),
            jax.ShapeDtypeStruct((n, 2 * D_KV), F32),
        ],
        compiler_params=pltpu.CompilerParams(
            dimension_semantics=("arbitrary",), vmem_limit_bytes=VMEM_LIMIT),
        name="decode_prep",
    )(p, shift, q, kv, mu_pad, prm, wl, bmask, tab, qw, kw)


def _decode_state_kernel(vec_ref, gb_ref, prm_ref, ones_ref, s_ref, sout_ref, yr_ref, yt_ref):
    h = pl.program_id(0)
    a_t, b_t, k_t, w_t, r_t = (vec_ref[i] for i in range(5))

    def body(i, carry):
        s = s_ref[0, i]
        sa = jnp.sum(s * a_t, axis=0, keepdims=True)
        v_i = vec_ref[5, pl.ds(i, 1), :]
        s_new = s * w_t + sa * b_t + v_i * k_t
        sout_ref[0, i] = s_new
        yt_ref[pl.ds(h * HEAD_DIM + i, 1), :] = jnp.sum(s_new * r_t, axis=0, keepdims=True)
        return carry

    lax.fori_loop(0, HEAD_DIM, body, 0, unroll=8)

    @pl.when(h == pl.num_programs(0) - 1)
    def _():
        y = jnp.transpose(yt_ref[...])
        yr_ref[...] = _rwkv_finish(y, gb_ref[0], gb_ref[1], prm_ref[...], ones_ref)


def _decode_state(vec_t, gb, prm, bmask, s_t):
    n = s_t.shape[-1]
    const = lambda shape: pl.BlockSpec(shape, lambda h: (0,) * len(shape))
    return pl.pallas_call(
        _decode_state_kernel,
        grid=(H_RWKV,),
        in_specs=[
            pl.BlockSpec((6, HEAD_DIM, n), lambda h: (0, h, 0)),
            const((2, n, D_RWKV)),
            const((16, D_RWKV)),
            const((QUAD, QUAD)),
            pl.BlockSpec((1, HEAD_DIM, HEAD_DIM, n), lambda h: (h, 0, 0, 0)),
        ],
        out_specs=[
            pl.BlockSpec((1, HEAD_DIM, HEAD_DIM, n), lambda h: (h, 0, 0, 0)),
            const((n, D_RWKV)),
        ],
        out_shape=[
            jax.ShapeDtypeStruct((H_RWKV, HEAD_DIM, HEAD_DIM, n), F32),
            jax.ShapeDtypeStruct((n, D_RWKV), F32),
        ],
        scratch_shapes=[pltpu.VMEM((D_RWKV, n), F32)],
        compiler_params=pltpu.CompilerParams(
            dimension_semantics=("arbitrary",), vmem_limit_bytes=VMEM_LIMIT),
        name="decode_state",
    )(vec_t, gb, prm, bmask, s_t)


def _decode_attn_kernel(qr_ref, kvn_ref, col_ref, ck_ref, cv_ref, sink_ref, ya_ref, kout_ref, vout_ref):
    nh = N_Q_HEADS
    seqs = range(DEC_TILE)
    hrow = lax.broadcasted_iota(jnp.int32, (nh, D_ATTN), 0)
    hlane = lax.broadcasted_iota(jnp.int32, (nh, D_ATTN), 1) // HEAD_DIM
    dmask = hrow == hlane
    grow = lax.broadcasted_iota(jnp.int32, (nh, D_KV), 0) // GQA_GROUP
    glane = lax.broadcasted_iota(jnp.int32, (nh, D_KV), 1) // HEAD_DIM
    gmask = grow == glane
    low = glane == 0
    key_idx = lax.broadcasted_iota(jnp.int32, (nh, WINDOW), 1)
    last = lax.broadcasted_iota(jnp.int32, (D_KV, WINDOW), 1) == WINDOW - 1
    sink = sink_ref[...]
    kvn = kvn_ref[...]
    col = col_ref[0]
    k_new = [kvn[j:j + 1, 0:D_KV] for j in seqs]
    v_new = [kvn[j:j + 1, D_KV:] for j in seqs]
    ck = [ck_ref[j] for j in seqs]
    cv = [cv_ref[j] for j in seqs]
    for j in seqs:
        kout_ref[j] = jnp.where(last, col[0:D_KV, j:j + 1], pltpu.roll(ck[j], WINDOW - 1, 1))
        vout_ref[j] = jnp.where(last, col[D_KV:, j:j + 1], pltpu.roll(cv[j], WINDOW - 1, 1))
    q8 = [qr_ref[j * nh:(j + 1) * nh, :] for j in seqs]
    qp = [jnp.where(gmask, jnp.concatenate([q8[j], q8[j]], axis=1), 0.0) for j in seqs]
    s_c = [jnp.where(key_idx >= 1, jnp.dot(qp[j], ck[j], preferred_element_type=F32) * ATTN_SCALE, NEG_INF)
           for j in seqs]
    s_n = [jnp.sum(qp[j] * k_new[j], axis=-1, keepdims=True) * ATTN_SCALE for j in seqs]
    m = [jnp.maximum(jnp.maximum(jnp.max(s_c[j], axis=-1, keepdims=True), s_n[j]), sink) for j in seqs]
    e_c = [jnp.exp(s_c[j] - m[j]) for j in seqs]
    e_n = [jnp.exp(s_n[j] - m[j]) for j in seqs]
    denom = [jnp.sum(e_c[j], axis=-1, keepdims=True) + e_n[j] + jnp.exp(sink - m[j]) for j in seqs]
    o = [(_dot_nt_f32(e_c[j], cv[j]) + e_n[j] * v_new[j]) / denom[j]
         for j in seqs]
    out_rows = []
    for j in seqs:
        rot = pltpu.roll(o[j], HEAD_DIM, 1)
        g0 = jnp.where(low, o[j], rot)
        g1 = jnp.where(low, rot, o[j])
        wide = jnp.concatenate([g0, g0, g1, g1], axis=1)
        out_rows.append(jnp.sum(jnp.where(dmask, wide, 0.0), axis=0, keepdims=True))
    ya_ref[...] = jnp.concatenate(out_rows, axis=0)


def _decode_attn(q_r, kvn, cols, ck_t, cv_t, sinks_col):
    n = kvn.shape[0]
    bt = DEC_TILE
    const = lambda shape: pl.BlockSpec(shape, lambda i: (0,) * len(shape))
    return pl.pallas_call(
        _decode_attn_kernel,
        grid=(n // bt,),
        in_specs=[
            pl.BlockSpec((bt * N_Q_HEADS, HEAD_DIM), lambda i: (i, 0)),
            pl.BlockSpec((bt, 2 * D_KV), lambda i: (i, 0)),
            pl.BlockSpec((1, 2 * D_KV, bt), lambda i: (i, 0, 0)),
            pl.BlockSpec((bt, D_KV, WINDOW), lambda i: (i, 0, 0)),
            pl.BlockSpec((bt, D_KV, WINDOW), lambda i: (i, 0, 0)),
            const((N_Q_HEADS, 1)),
        ],
        out_specs=[
            pl.BlockSpec((bt, D_ATTN), lambda i: (i, 0)),
            pl.BlockSpec((bt, D_KV, WINDOW), lambda i: (i, 0, 0)),
            pl.BlockSpec((bt, D_KV, WINDOW), lambda i: (i, 0, 0)),
        ],
        out_shape=[
            jax.ShapeDtypeStruct((n, D_ATTN), F32),
            jax.ShapeDtypeStruct((n, D_KV, WINDOW), F32),
            jax.ShapeDtypeStruct((n, D_KV, WINDOW), F32),
        ],
        compiler_params=pltpu.CompilerParams(
            dimension_semantics=("arbitrary",), vmem_limit_bytes=VMEM_LIMIT),
        name="decode_attn",
    )(q_r, kvn, cols, ck_t, cv_t, sinks_col)


def _pad_cols(w, at, n):
    return jnp.concatenate([w[..., :at], jnp.zeros(w.shape[:-1] + (n,), w.dtype), w[..., at:]], axis=-1)


def kernel(x_prompt, x_sample, state_wkv, state_shift, cache_k_win, cache_v_win, norm_mix_w, w_in, mu_shift, w0, w_decay_up, a0, w_a_up, w_g_up, k_k, k_a, r_k, ln_x_w, ln_x_b, q_norm_w, k_norm_w, sinks, w_out, norm_ffn_w, w_ffn_up, w_ffn_down):
    bsz, t, _ = x_prompt.shape
    nd = x_sample.shape[0]
    l = 0
    pad = D_LORA_PAD - D_LORA

    w_in_t = jnp.swapaxes(w_in[l], 0, 1)
    w_in_pad = jnp.concatenate([w_in_t[:D_SHIFT], jnp.zeros((pad, D_MODEL), F32), w_in_t[D_SHIFT:]],
                               axis=0).astype(BF16)
    mu_pad = _pad_cols(mu_shift[l][None, :], D_SHIFT, pad)
    wl = jnp.zeros((D_LORA_PAD, 3 * D_RWKV), F32)
    wl = wl.at[0:32, 0:D_RWKV].set(w_decay_up[l])
    wl = wl.at[32:64, D_RWKV:2 * D_RWKV].set(w_a_up[l])
    wl = wl.at[64:160, 2 * D_RWKV:].set(w_g_up[l])
    wl = wl.astype(BF16)
    prm = jnp.zeros((16, D_RWKV), F32)
    prm = prm.at[0].set(w0[l]).at[1].set(a0[l]).at[2].set(k_k[l]).at[3].set(k_a[l])
    prm = prm.at[4].set(r_k[l].reshape(-1)).at[5].set(ln_x_w[l]).at[6].set(ln_x_b[l])
    hid = np.arange(QUAD) // HEAD_DIM
    bmask = jnp.asarray(hid[:, None] == hid[None, :], BF16)
    tri = jnp.asarray(np.tril(np.ones((CHUNK, CHUNK))), BF16)
    qw = jnp.tile(q_norm_w[l][None, :], (1, N_Q_HEADS))
    kw = jnp.tile(k_norm_w[l][None, :], (1, N_KV_HEADS))
    nmw = norm_mix_w[l][None, :]
    nfw = norm_ffn_w[l][None, :]
    wo = w_out[l].astype(BF16)
    wu = w_ffn_up[l].astype(BF16)
    wd = w_ffn_down[l].astype(BF16)
    ang_s = (jnp.zeros((8, 1), F32) + PAST_LEN) * _rope_lane_freq()
    tab_s = jnp.concatenate([jnp.cos(ang_s), jnp.sin(ang_s)], axis=1)

    xs = x_sample.reshape(nd, D_MODEL)
    p_s, q_s, kv_s = _inproj(xs, nmw, w_in_pad, 128)
    shift_in = _pad_cols(state_shift[l].reshape(nd, D_SHIFT), D_SHIFT, pad)
    vec_t, gb, qn_s, kvn_s = _decode_prep(p_s, shift_in, q_s, kv_s, mu_pad, prm, wl, bmask, tab_s, qw, kw)
    s_t = jnp.transpose(state_wkv[l], (1, 2, 3, 0))
    ck_t = jnp.swapaxes(cache_k_win[l].reshape(nd, WINDOW, D_KV), 1, 2)
    cv_t = jnp.swapaxes(cache_v_win[l].reshape(nd, WINDOW, D_KV), 1, 2)
    q_r = qn_s.reshape(nd * N_Q_HEADS, HEAD_DIM)
    cols = jnp.swapaxes(kvn_s.reshape(nd // DEC_TILE, DEC_TILE, 2 * D_KV), 1, 2)
    wkv_t, yr_s = _decode_state(vec_t, gb, prm, bmask, s_t)
    ya_s, kc_t, vc_t = _decode_attn(q_r, kvn_s, cols, ck_t, cv_t, sinks[l][:, None])

    xp = x_prompt.reshape(bsz * t, D_MODEL)
    xs_p, plast, q_p, kv_p = _inproj_shift(xp, nmw, w_in_pad, mu_pad, t, 1024)
    yr_p, hbd = _rwkv_prompt(xs_p.reshape(bsz, t, D_SHIFT_PAD), prm, wl, tri, bmask)
    y_prompt, y_s, kwin_p, vwin_p = _attn_ffn(q_p.reshape(bsz, t, D_ATTN), kv_p.reshape(bsz, t, 2 * D_KV),
                                              qw, kw, sinks[l][None, :], bmask, _band_bias(),
                                              x_prompt, yr_p, xs, yr_s, ya_s, wo, nfw, wu, wd)
    hb = hbd.reshape(bsz, 2, 4, HEAD_DIM, 2, HEAD_DIM)
    wkv_prompt = jnp.stack([hb[:, :, j, :, j % 2, :] for j in range(4)], axis=2)
    wkv_prompt = wkv_prompt.reshape(bsz, H_RWKV, HEAD_DIM, HEAD_DIM)[None]
    shift_prompt = plast[:, 0:1, :D_SHIFT][None]
    k_win_prompt = kwin_p.reshape(bsz, WINDOW, N_KV_HEADS, HEAD_DIM)[None]
    v_win_prompt = vwin_p.reshape(bsz, WINDOW, N_KV_HEADS, HEAD_DIM)[None]

    y_sample = y_s.reshape(nd, 1, D_MODEL)
    wkv_sample = jnp.transpose(wkv_t, (3, 0, 1, 2))[None]
    shift_sample = p_s[:, :D_SHIFT].reshape(nd, 1, D_SHIFT)[None]
    k_win_sample = jnp.swapaxes(kc_t, 1, 2).reshape(nd, WINDOW, N_KV_HEADS, HEAD_DIM)[None]
    v_win_sample = jnp.swapaxes(vc_t, 1, 2).reshape(nd, WINDOW, N_KV_HEADS, HEAD_DIM)[None]

    return (y_prompt, y_sample, wkv_prompt, shift_prompt, k_win_prompt, v_win_prompt,
            wkv_sample, shift_sample, k_win_sample, v_win_sample)
```

```python
import functools

import jax
import jax.numpy as jnp
import numpy as np
from jax import lax
from jax.experimental import pallas as pl
from jax.experimental.pallas import tpu as pltpu

F32 = jnp.float32
BF16 = jnp.bfloat16

D_MODEL = 1024
D_RWKV = 512
D_ATTN = 512
HEAD_DIM = 64
H_RWKV = 8
N_Q_HEADS = 8
N_KV_HEADS = 2
GQA_GROUP = 4
D_KV = 128
D_LORA = 160
D_LORA_PAD = 256
D_SHIFT = 3 * D_RWKV + D_LORA
D_SHIFT_PAD = 3 * D_RWKV + D_LORA_PAD
D_IN_PAD = D_SHIFT_PAD + D_ATTN + 2 * D_KV
WINDOW = 128
ROPE_DIM = 16
ROPE_HALF = 8
ROPE_THETA = 500000.0
ATTN_SCALE = HEAD_DIM ** -0.5
D_FF = 4096
RMS_EPS = 1e-6
LNX_EPS = 64e-5
NEG_INF = -1e30
LOG2E = 1.4426950408889634
PAST_LEN = 16384

CHUNK = 64
QUAD = 4 * HEAD_DIM
VMEM_LIMIT = 56 * 1024 * 1024


def _split2(x):
    hi = x.astype(BF16)
    lo = (x - hi.astype(F32)).astype(BF16)
    return hi, lo


def _head_sums(xs, ones_ref):
    n, w = xs[0].shape
    tile = min(w, QUAD)
    per = w // tile
    pieces = [x[:, j * tile:(j + 1) * tile] for x in xs for j in range(per)]
    stacked = jnp.concatenate(pieces, axis=0) if len(pieces) > 1 else pieces[0]
    ones = ones_ref[0:tile, 0:tile]
    out = jnp.dot(stacked.astype(BF16), ones, preferred_element_type=F32)
    res = []
    for i in range(len(xs)):
        cols = [out[(i * per + j) * n:(i * per + j + 1) * n] for j in range(per)]
        res.append(jnp.concatenate(cols, axis=1) if per > 1 else cols[0])
    return res


def _cumsum_rows(tri_bf16, x):
    hi, lo = _split2(x)
    return (jnp.dot(tri_bf16, hi, preferred_element_type=F32)
            + jnp.dot(tri_bf16, lo, preferred_element_type=F32))


def _mm(a, b):
    return jnp.dot(a.astype(BF16), b.astype(BF16), preferred_element_type=F32)


def _mm_nt(a, b):
    return lax.dot_general(a.astype(BF16), b.astype(BF16), (((1,), (1,)), ((), ())),
                           preferred_element_type=F32)


def _mm_tn(a, b):
    return lax.dot_general(a.astype(BF16), b.astype(BF16), (((0,), (0,)), ((), ())),
                           preferred_element_type=F32)


def _dot_nt_f32(a, b):
    return lax.dot_general(a, b, (((1,), (1,)), ((), ())), preferred_element_type=F32)


def _sigmoid(x):
    return 1.0 / (1.0 + jnp.exp(-x))


def _interleave(*gens):
    live = list(gens)
    while live:
        for g in list(live):
            try:
                next(g)
            except StopIteration:
                live.remove(g)


def _norm_project(x, nw_ref, wt_ref):
    ms = jnp.mean(x * x, axis=-1, keepdims=True)
    h = (x * lax.rsqrt(ms + RMS_EPS)) * nw_ref[...]
    return _mm_nt(h, wt_ref[...])


def _inproj_kernel(x_ref, nw_ref, w_ref, p_ref, q_ref, kv_ref):
    out = _norm_project(x_ref[...], nw_ref, w_ref)
    p_ref[...] = out[:, :D_SHIFT_PAD]
    q_ref[...] = out[:, D_SHIFT_PAD:D_SHIFT_PAD + D_ATTN]
    kv_ref[...] = out[:, D_SHIFT_PAD + D_ATTN:]


def _inproj_shift_kernel(tiles_per_seq, x_ref, nw_ref, w_ref, mu_ref, xs_ref, last_ref, q_ref, kv_ref, prev_ref):
    i = pl.program_id(0)

    @pl.when(i % tiles_per_seq == 0)
    def _():
        prev_ref[...] = jnp.zeros_like(prev_ref)

    out = _norm_project(x_ref[...], nw_ref, w_ref)
    p = out[:, :D_SHIFT_PAD]
    tm = p.shape[0]
    row = lax.broadcasted_iota(jnp.int32, p.shape, 0)
    prev = jnp.where(row == 0, jnp.broadcast_to(prev_ref[0:1, :], p.shape), pltpu.roll(p, 1, 0))
    xs_ref[...] = p + (prev - p) * mu_ref[...]
    last = jnp.broadcast_to(p[tm - 1:tm, :], prev_ref.shape)
    prev_ref[...] = last
    last_ref[0] = last
    q_ref[...] = out[:, D_SHIFT_PAD:D_SHIFT_PAD + D_ATTN]
    kv_ref[...] = out[:, D_SHIFT_PAD + D_ATTN:]


def _inproj_shift(x2d, norm_w, w_in_pad, mu_pad, seq_len, tm):
    m = x2d.shape[0]
    tiles_per_seq = seq_len // tm
    return pl.pallas_call(
        functools.partial(_inproj_shift_kernel, tiles_per_seq),
        grid=(m // tm,),
        in_specs=[
            pl.BlockSpec((tm, D_MODEL), lambda i: (i, 0)),
            pl.BlockSpec((1, D_MODEL), lambda i: (0, 0)),
            pl.BlockSpec((D_IN_PAD, D_MODEL), lambda i: (0, 0)),
            pl.BlockSpec((1, D_SHIFT_PAD), lambda i: (0, 0)),
        ],
        out_specs=[
            pl.BlockSpec((tm, D_SHIFT_PAD), lambda i: (i, 0)),
            pl.BlockSpec((1, 8, D_SHIFT_PAD), lambda i: (i // tiles_per_seq, 0, 0)),
            pl.BlockSpec((tm, D_ATTN), lambda i: (i, 0)),
            pl.BlockSpec((tm, 2 * D_KV), lambda i: (i, 0)),
        ],
        out_shape=[
            jax.ShapeDtypeStruct((m, D_SHIFT_PAD), F32),
            jax.ShapeDtypeStruct((m // seq_len, 8, D_SHIFT_PAD), F32),
            jax.ShapeDtypeStruct((m, D_ATTN), F32),
            jax.ShapeDtypeStruct((m, 2 * D_KV), F32),
        ],
        scratch_shapes=[pltpu.VMEM((8, D_SHIFT_PAD), F32)],
        compiler_params=pltpu.CompilerParams(
            dimension_semantics=("arbitrary",), vmem_limit_bytes=VMEM_LIMIT),
        name="inproj_shift",
    )(x2d, norm_w, w_in_pad, mu_pad)


def _inproj(x2d, norm_w, w_in_pad, tm):
    m = x2d.shape[0]
    return pl.pallas_call(
        _inproj_kernel,
        grid=(m // tm,),
        in_specs=[
            pl.BlockSpec((tm, D_MODEL), lambda i: (i, 0)),
            pl.BlockSpec((1, D_MODEL), lambda i: (0, 0)),
            pl.BlockSpec((D_IN_PAD, D_MODEL), lambda i: (0, 0)),
        ],
        out_specs=[
            pl.BlockSpec((tm, D_SHIFT_PAD), lambda i: (i, 0)),
            pl.BlockSpec((tm, D_ATTN), lambda i: (i, 0)),
            pl.BlockSpec((tm, 2 * D_KV), lambda i: (i, 0)),
        ],
        out_shape=[
            jax.ShapeDtypeStruct((m, D_SHIFT_PAD), F32),
            jax.ShapeDtypeStruct((m, D_ATTN), F32),
            jax.ShapeDtypeStruct((m, 2 * D_KV), F32),
        ],
        compiler_params=pltpu.CompilerParams(
            dimension_semantics=("arbitrary",), vmem_limit_bytes=VMEM_LIMIT),
        name="inproj",
    )(x2d, norm_w, w_in_pad)


def _rwkv_features(xs, prm, wl_ref, ones_ref):
    r = xs[:, 0:D_RWKV]
    k = xs[:, D_RWKV:2 * D_RWKV]
    v = xs[:, 2 * D_RWKV:3 * D_RWKV]
    lora = xs[:, 3 * D_RWKV:]
    col = lax.broadcasted_iota(jnp.int32, lora.shape, 1)
    act = jnp.where(col < 32, jnp.tanh(lora), jnp.where(col < 64, lora, _sigmoid(lora)))
    up = jnp.dot(act.astype(BF16), wl_ref[...], preferred_element_type=F32)
    w0, a0, k_k, k_a, r_k = prm[0:1], prm[1:2], prm[2:3], prm[3:4], prm[4:5]
    logw = (-np.exp(-0.5)) * _sigmoid(w0 + up[:, 0:D_RWKV])
    asig = _sigmoid(a0 + up[:, D_RWKV:2 * D_RWKV])
    g = up[:, 2 * D_RWKV:]
    kk = k * k_k
    k_mod = k * (1.0 + (asig - 1.0) * k_a)
    ss, rk = _head_sums([kk * kk, r * k_mod * r_k], ones_ref)
    kk = kk / jnp.maximum(jnp.sqrt(ss), 1e-12)
    k = k_mod
    bonus = rk * v
    return r, logw, k, v, -kk, kk * asig, g, bonus


def _rwkv_finish(y, g, bonus, prm, ones_ref):
    ln_w, ln_b = prm[5:6], prm[6:7]
    mean = _head_sums([y], ones_ref)[0] * (1.0 / HEAD_DIM)
    d = y - mean
    var = _head_sums([d * d], ones_ref)[0] * (1.0 / HEAD_DIM)
    yn = d * lax.rsqrt(var + LNX_EPS) * ln_w + ln_b
    return (yn + bonus) * g


def _block_diag(x, bmask):
    return jnp.concatenate([x] * 4, axis=0) * bmask


def _chunk_prep(r, logw, k, v, a, b, tri_ref, bmask):
    n = len(r)
    ch = range(n)
    tri = tri_ref[...]
    cum = [_cumsum_rows(tri, logw[i]) for i in ch]
    e_in = [jnp.exp(cum[i]) for i in ch]
    e_ex = [jnp.exp(cum[i] - logw[i]) for i in ch]
    e_inv = [1.0 / e_in[i] for i in ch]
    e_last = [e_in[i][CHUNK - 1:CHUNK, :] for i in ch]
    rt = [(r[i] * e_in[i]).astype(BF16) for i in ch]
    at = [(a[i] * e_ex[i]).astype(BF16) for i in ch]
    kt = [(k[i] * e_inv[i]).astype(BF16) for i in ch]
    bt = [(b[i] * e_inv[i]).astype(BF16) for i in ch]
    vb = [v[i].astype(BF16) for i in ch]

    t_idx = lax.broadcasted_iota(jnp.int32, (CHUNK, QUAD), 0)
    s_idx = lax.broadcasted_iota(jnp.int32, (CHUNK, QUAD), 1) & (HEAD_DIM - 1)
    strict = s_idx < t_idx
    incl = s_idx <= t_idx

    gm = [_mm_nt(jnp.concatenate([at[i], rt[i]], axis=0),
                 jnp.concatenate([_block_diag(bt[i], bmask), _block_diag(kt[i], bmask)], axis=0))
          for i in ch]
    a_ab = [jnp.where(strict, gm[i][:CHUNK, :QUAD], 0.0) for i in ch]
    a_ak = [jnp.where(strict, gm[i][:CHUNK, QUAD:], 0.0) for i in ch]
    a_rb = [jnp.where(incl, gm[i][CHUNK:, :QUAD], 0.0) for i in ch]
    a_rk = [jnp.where(incl, gm[i][CHUNK:, QUAD:], 0.0) for i in ch]

    eye = jnp.where(s_idx == t_idx, 1.0, 0.0)
    pwb = [a_ab[i].astype(BF16) for i in ch]
    t_inv = [eye + a_ab[i] for i in ch]
    for it in range(6):
        rbd = [_block_diag(pwb[i], bmask) for i in ch]
        if it == 0:
            pwb = [_mm(pwb[i], rbd[i]).astype(BF16) for i in ch]
        elif it < 5:
            out = [_mm(jnp.concatenate([pwb[i], t_inv[i].astype(BF16)], axis=0), rbd[i]) for i in ch]
            pwb = [out[i][:CHUNK].astype(BF16) for i in ch]
            t_inv = [t_inv[i] + out[i][CHUNK:] for i in ch]
        else:
            t_inv = [t_inv[i] + _mm(t_inv[i], rbd[i]) for i in ch]

    vbd = [_block_diag(vb[i], bmask) for i in ch]
    xy0 = [_mm(jnp.concatenate([a_ak[i], a_rk[i]], axis=0), vbd[i]) for i in ch]
    return [dict(ar=jnp.concatenate([at[i], rt[i]], axis=0), x0=xy0[i][:CHUNK], y0=xy0[i][CHUNK:],
                 t_inv=t_inv[i].astype(BF16), a_rb=a_rb[i].astype(BF16), vb=vb[i],
                 bk=jnp.concatenate([bt[i], kt[i]], axis=0), e_last=e_last[i]) for i in ch]


def _chunk_step(pre, state, bmask, out):
    ch = range(len(pre))
    half = QUAD // 2
    zeros = jnp.zeros((half, half), BF16)
    sc = [state[i].astype(BF16) for i in ch]
    sb = [jnp.concatenate([jnp.concatenate([sc[i][:half], zeros], axis=1),
                           jnp.concatenate([zeros, sc[i][half:]], axis=1)], axis=0) for i in ch]
    xr = [_mm_nt(pre[i]["ar"], sb[i]) for i in ch]
    yield
    x = [xr[i][:CHUNK] + pre[i]["x0"] for i in ch]
    u = [_mm(pre[i]["t_inv"], _block_diag(x[i].astype(BF16), bmask)) for i in ch]
    yield
    ub = [u[i].astype(BF16) for i in ch]
    out["y"] = [xr[i][CHUNK:] + pre[i]["y0"] + _mm(pre[i]["a_rb"], _block_diag(ub[i], bmask)) for i in ch]
    upd = [_mm_tn(jnp.concatenate([ub[i], pre[i]["vb"]], axis=0), pre[i]["bk"]) for i in ch]
    yield
    bm = bmask[:half, :half].astype(F32)
    s_new = []
    for i in ch:
        e_last = pre[i]["e_last"]
        top = (state[i][:half] + upd[i][:half, :half] * bm) * e_last[:, :half]
        bot = (state[i][half:] + upd[i][half:, half:] * bm) * e_last[:, half:]
        s_new.append(jnp.concatenate([top, bot], axis=0))
    out["state"] = s_new


def _finish_stages(ys, g, bonus, prm, ones_ref, y_ref, j):
    nseq = len(ys) // 2
    ln_w, ln_b = prm[5:6], prm[6:7]
    y = jnp.concatenate([jnp.concatenate(ys[2 * s:2 * s + 2], axis=1) for s in range(nseq)], axis=0)
    mean = _head_sums([y], ones_ref)[0] * (1.0 / HEAD_DIM)
    yield
    d = y - mean
    var = _head_sums([d * d], ones_ref)[0] * (1.0 / HEAD_DIM)
    yield
    out = (d * lax.rsqrt(var + LNX_EPS) * ln_w + ln_b + bonus) * g
    for s in range(nseq):
        y_ref[s, j * CHUNK:(j + 1) * CHUNK, :] = out[s * CHUNK:(s + 1) * CHUNK]


def _rwkv_prompt_kernel(xs_ref, prm_ref, wl_ref, tri_ref, bmask_ref, y_ref, hout_ref, h_ref):
    c = pl.program_id(0)
    nseq, tstep, _ = xs_ref.shape
    nchunk = tstep // CHUNK

    @pl.when(c == 0)
    def _():
        h_ref[...] = jnp.zeros_like(h_ref)

    xs = jnp.concatenate([xs_ref[s] for s in range(nseq)], axis=0)
    prm = prm_ref[...]
    r, logw, k, v, a, b, g, bonus = _rwkv_features(xs, prm, wl_ref, bmask_ref)
    bmask = bmask_ref[...]
    lanes = [(s, q) for s in range(nseq) for q in range(2)]
    chains = [(j, s, q) for j in range(nchunk) for s, q in lanes]
    cut = lambda x: [x[s * tstep + j * CHUNK:s * tstep + (j + 1) * CHUNK, q * QUAD:(q + 1) * QUAD]
                     for j, s, q in chains]
    pre = _chunk_prep(cut(r), cut(logw), cut(k), cut(v), cut(a), cut(b), tri_ref, bmask)
    state = [h_ref[s, q] for s, q in lanes]
    crow = lambda x, j: jnp.concatenate([x[s * tstep + j * CHUNK:s * tstep + (j + 1) * CHUNK]
                                         for s in range(nseq)], axis=0)
    finish = iter(())
    for j in range(nchunk):
        res = {}
        _interleave(_chunk_step(pre[j * len(lanes):(j + 1) * len(lanes)], state, bmask, res), finish)
        state = res["state"]
        finish = _finish_stages(res["y"], crow(g, j), crow(bonus, j), prm, bmask_ref, y_ref, j)
    _interleave(finish)
    for i, (s, q) in enumerate(lanes):
        h_ref[s, q] = state[i]

    @pl.when(c == pl.num_programs(0) - 1)
    def _():
        hout_ref[...] = h_ref[...]


RWKV_CHUNKS_PER_STEP = 4


def _rwkv_prompt(xs3d, prm, wl, tri, bmask):
    bsz, t, _ = xs3d.shape
    tstep = RWKV_CHUNKS_PER_STEP * CHUNK
    nc = t // tstep
    const = lambda shape: pl.BlockSpec(shape, lambda c: (0,) * len(shape))
    state_shape = (bsz, 2, QUAD, QUAD // 2)
    return pl.pallas_call(
        _rwkv_prompt_kernel,
        grid=(nc,),
        in_specs=[
            pl.BlockSpec((bsz, tstep, D_SHIFT_PAD), lambda c: (0, c, 0)),
            const((16, D_RWKV)),
            const((D_LORA_PAD, 3 * D_RWKV)),
            const((CHUNK, CHUNK)),
            const((QUAD, QUAD)),
        ],
        out_specs=[
            pl.BlockSpec((bsz, tstep, D_RWKV), lambda c: (0, c, 0)),
            const(state_shape),
        ],
        out_shape=[
            jax.ShapeDtypeStruct((bsz, t, D_RWKV), F32),
            jax.ShapeDtypeStruct(state_shape, F32),
        ],
        scratch_shapes=[pltpu.VMEM(state_shape, F32)],
        compiler_params=pltpu.CompilerParams(
            dimension_semantics=("arbitrary",), vmem_limit_bytes=VMEM_LIMIT),
        name="rwkv_prompt",
    )(xs3d, prm, wl, tri, bmask)


def _rope_lane_freq():
    inv_freq = jnp.power(ROPE_THETA, -jnp.arange(ROPE_HALF, dtype=F32) * (2.0 / ROPE_DIM))
    return inv_freq[(np.arange(2 * HEAD_DIM) % HEAD_DIM) % ROPE_HALF][None, :]


def _rope_tables(cos, sin):
    dim = lax.broadcasted_iota(jnp.int32, cos.shape, 1) & (HEAD_DIM - 1)
    cos_t = jnp.where(dim < ROPE_DIM, cos, 1.0)
    sin_lo = jnp.where(dim < ROPE_HALF, -sin, 0.0)
    sin_hi = jnp.where((dim >= ROPE_HALF) & (dim < ROPE_DIM), sin, 0.0)
    return cos_t, sin_lo, sin_hi


def _qk_norm_rope(x, norm_w, cos_t, sin_lo, sin_hi, ones_ref):
    ms = _head_sums([x * x], ones_ref)[0] * (1.0 / HEAD_DIM)
    xn = x * lax.rsqrt(ms + RMS_EPS) * norm_w
    width = x.shape[1]
    fwd = pltpu.roll(xn, width - ROPE_HALF, 1)
    bwd = pltpu.roll(xn, ROPE_HALF, 1)
    return xn * cos_t + fwd * sin_lo + bwd * sin_hi


def _tile_lanes(x, reps):
    return jnp.concatenate([x] * reps, axis=1) if reps > 1 else x


def _attn_stages(q_ref, kv_ref, taba_ref, tabb_ref, qw_ref, kw_ref, sink_ref, ones_ref, bias_ref, first_bias,
                 kprev_ref, vprev_ref, out):
    nseq, tq, _ = q_ref.shape
    nblk = tq // WINDOW
    units = [(s, b) for s in range(nseq) for b in range(nblk)]
    blk = lambda b: slice(b * WINDOW, (b + 1) * WINDOW)
    tb = tabb_ref[...]
    cos_b, sin_b = tb[:, :128], tb[:, 128:]
    rope, rope4 = [], []
    for b in range(nblk):
        ta = taba_ref[b][0:1, :]
        cos_a, sin_a = ta[:, :128], ta[:, 128:]
        tabs = _rope_tables(cos_a * cos_b - sin_a * sin_b, sin_a * cos_b + cos_a * sin_b)
        rope.append(tabs)
        rope4.append([_tile_lanes(x, 4) for x in tabs])
    q = {(s, b): _qk_norm_rope(q_ref[s, blk(b), :], qw_ref[...], *rope4[b], ones_ref) * (ATTN_SCALE * LOG2E)
         for s, b in units}
    kv = {(s, b): kv_ref[s, blk(b), :] for s, b in units}
    k_cur = {u: _qk_norm_rope(kv[u][:, 0:D_KV], kw_ref[...], *rope[u[1]], ones_ref) for u in units}
    v_cur = {u: kv[u][:, D_KV:] for u in units}
    k_all = {(s, b): jnp.concatenate([kprev_ref[s] if b == 0 else k_cur[(s, b - 1)], k_cur[(s, b)]], axis=0)
             for s, b in units}
    v_all = {(s, b): jnp.concatenate([vprev_ref[s] if b == 0 else v_cur[(s, b - 1)], v_cur[(s, b)]], axis=0)
             for s, b in units}
    out["k_cur"] = [k_cur[(s, nblk - 1)] for s in range(nseq)]
    out["v_cur"] = [v_cur[(s, nblk - 1)] for s in range(nseq)]
    for s in range(nseq):
        kprev_ref[s] = out["k_cur"][s]
        vprev_ref[s] = out["v_cur"][s]
    yield

    nk = 2 * WINDOW
    bias = [bias_ref[first_bias] if b == 0 else bias_ref[1] for b in range(nblk)]
    sinks = sink_ref[...] * LOG2E
    low = lax.broadcasted_iota(jnp.int32, (nk, D_KV), 1) < HEAD_DIM
    lane_blk = [ones_ref[j * HEAD_DIM:j * HEAD_DIM + 1, :] for j in range(GQA_GROUP)]

    chains = [(u, g) for u in units for g in range(N_KV_HEADS)]
    ch = range(len(chains))
    k_rot = {u: pltpu.roll(k_all[u], HEAD_DIM, 1) for u in units}
    k2 = [jnp.where(low, k_all[u], k_rot[u]) if g == 0 else jnp.where(low, k_rot[u], k_all[u])
          for u, g in chains]
    k4 = [jnp.concatenate([k2[c], k2[c]], axis=1).astype(BF16) for c in ch]
    vb = {u: v_all[u].astype(BF16) for u in units}
    qg = [q[u][:, g * QUAD:(g + 1) * QUAD].astype(BF16) for u, g in chains]
    qstack = [jnp.concatenate([qg[c] * lane_blk[j] for j in range(GQA_GROUP)], axis=0) for c in ch]
    sink_row = [jnp.concatenate(
        [jnp.broadcast_to(sinks[:, g * GQA_GROUP + j:g * GQA_GROUP + j + 1], (1, WINDOW))
         for j in range(GQA_GROUP)], axis=1) for u, g in chains]
    yield
    sc = [_mm_nt(k4[c], qstack[c]) + bias[u[1]] for c, (u, g) in enumerate(chains)]
    yield
    m = [jnp.maximum(jnp.max(sc[c], axis=0, keepdims=True), sink_row[c]) for c in ch]
    e = [jnp.exp2(sc[c] - m[c]) for c in ch]
    yield
    denom = [jnp.sum(e[c], axis=0, keepdims=True) + jnp.exp2(sink_row[c] - m[c]) for c in ch]
    ot = {(u, g): _mm_tn(vb[u], e[c].astype(BF16))[g * HEAD_DIM:(g + 1) * HEAD_DIM, :] * (1.0 / denom[c])
          for c, (u, g) in enumerate(chains)}
    yield
    ya = []
    for s in range(nseq):
        blocks = []
        for b in range(nblk):
            yt = jnp.concatenate([ot[((s, b), g)][:, j * WINDOW:(j + 1) * WINDOW]
                                  for g in range(N_KV_HEADS) for j in range(GQA_GROUP)], axis=0)
            blocks.append(jnp.transpose(yt))
        ya.append(jnp.concatenate(blocks, axis=0) if nblk > 1 else blocks[0])
    out["ya"] = ya


def _ffn_stages(x, yr, ya, wo_ref, nw_ref, wu_ref, wd_ref, out, pieces=4):
    mix = jnp.concatenate([yr, ya], axis=1).astype(BF16)
    x1 = x + jnp.dot(mix, wo_ref[...], preferred_element_type=F32)
    yield
    ms = jnp.mean(x1 * x1, axis=-1, keepdims=True)
    hf = ((x1 * lax.rsqrt(ms + RMS_EPS)) * nw_ref[...]).astype(BF16)
    acc = x1
    step = D_FF // pieces
    for j in range(pieces):
        up = jnp.dot(hf, wu_ref[:, j * step:(j + 1) * step], preferred_element_type=F32)
        yield
        act = jnp.square(jnp.maximum(up, 0.0)).astype(BF16)
        acc = acc + jnp.dot(act, wd_ref[j * step:(j + 1) * step, :], preferred_element_type=F32)
        yield
    out["y"] = acc


def _attn_ffn_kernel(q_ref, kv_ref, taba_ref, tabb_ref, qw_ref, kw_ref, sink_ref, ones_ref, bias_ref,
                     x_ref, yr_ref, xd_ref, yrd_ref, yad_ref, wo_ref, nw_ref, wu_ref, wd_ref,
                     o_ref, od_ref, kwin_ref, vwin_ref, kprev_ref, vprev_ref, ya_ref):
    i = pl.program_id(0)
    nseq, tq, _ = q_ref.shape
    seqs = range(nseq)
    first = i == 0

    @pl.when(first)
    def _():
        kprev_ref[...] = jnp.zeros_like(kprev_ref)
        vprev_ref[...] = jnp.zeros_like(vprev_ref)
        ya_ref[...] = jnp.zeros_like(ya_ref)

    def rows(ref, dec_ref):
        tile = jnp.concatenate([ref[s] for s in seqs], axis=0)
        dec = dec_ref[...]
        return jnp.where(first, jnp.concatenate([dec] * (tile.shape[0] // dec.shape[0]), axis=0), tile)

    a_out, f_out = {}, {}
    _interleave(
        _ffn_stages(rows(x_ref, xd_ref), rows(yr_ref, yrd_ref), rows(ya_ref, yad_ref),
                    wo_ref, nw_ref, wu_ref, wd_ref, f_out),
        _attn_stages(q_ref, kv_ref, taba_ref, tabb_ref, qw_ref, kw_ref, sink_ref, ones_ref, bias_ref,
                     jnp.minimum(i, 1), kprev_ref, vprev_ref, a_out))
    for s in seqs:
        o_ref[s] = f_out["y"][s * tq:(s + 1) * tq]
        ya_ref[s] = a_out["ya"][s]

    @pl.when(first)
    def _():
        od_ref[...] = f_out["y"][:od_ref.shape[0]]

    @pl.when(i == pl.num_programs(0) - 2)
    def _():
        for s in seqs:
            kwin_ref[s] = a_out["k_cur"][s]
            vwin_ref[s] = a_out["v_cur"][s]


def _band_bias():
    ki = np.arange(2 * WINDOW)[:, None]
    qi = (np.arange(GQA_GROUP * WINDOW) % WINDOW + WINDOW)[None, :]
    dq = qi - ki
    band = (dq >= 0) & (dq < WINDOW)
    first = band & (ki >= WINDOW)
    return jnp.asarray(np.where(np.stack([first, band]), 0.0, NEG_INF), F32)


def _rope_block_tables(nb):
    freq = _rope_lane_freq()
    ang_a = (jnp.arange(nb, dtype=F32) * WINDOW)[:, None] * freq
    ang_b = jnp.arange(WINDOW, dtype=F32)[:, None] * freq
    tab_a = jnp.concatenate([jnp.cos(ang_a), jnp.sin(ang_a)], axis=1)
    tab_b = jnp.concatenate([jnp.cos(ang_b), jnp.sin(ang_b)], axis=1)
    return jnp.broadcast_to(tab_a[:, None, :], (nb, 8, 4 * HEAD_DIM)), tab_b


ATTN_BLOCKS_PER_STEP = 1


def _attn_ffn(q3d, kv3d, qw, kw, sinks, ones_bd, bias, x3d, yr3d, xd, yrd, yad, wo, nw, wu, wd):
    bsz, t, _ = q3d.shape
    nd = xd.shape[0]
    nblk = ATTN_BLOCKS_PER_STEP
    tq = nblk * WINDOW
    nt = t // tq
    const = lambda shape: pl.BlockSpec(shape, lambda i: (0,) * len(shape))
    single = lambda shape: pl.BlockSpec(shape, lambda i: (0,) * len(shape), pipeline_mode=pl.Buffered(1))
    cur = lambda i: jnp.minimum(i, nt - 1)
    prv = lambda i: jnp.maximum(i - 1, 0)
    tab_a, tab_b = _rope_block_tables(t // WINDOW)
    return pl.pallas_call(
        _attn_ffn_kernel,
        grid=(nt + 1,),
        in_specs=[
            pl.BlockSpec((bsz, tq, D_ATTN), lambda i: (0, cur(i), 0)),
            pl.BlockSpec((bsz, tq, 2 * D_KV), lambda i: (0, cur(i), 0)),
            pl.BlockSpec((nblk, 8, 4 * HEAD_DIM), lambda i: (cur(i), 0, 0)),
            const((WINDOW, 4 * HEAD_DIM)),
            const((1, D_ATTN)),
            const((1, D_KV)),
            const((1, N_Q_HEADS)),
            const((QUAD, QUAD)),
            const((2, 2 * WINDOW, GQA_GROUP * WINDOW)),
            pl.BlockSpec((bsz, tq, D_MODEL), lambda i: (0, prv(i), 0)),
            pl.BlockSpec((bsz, tq, D_RWKV), lambda i: (0, prv(i), 0)),
            const((nd, D_MODEL)),
            const((nd, D_RWKV)),
            const((nd, D_ATTN)),
            single((D_MODEL, D_MODEL)),
            const((1, D_MODEL)),
            single((D_MODEL, D_FF)),
            single((D_FF, D_MODEL)),
        ],
        out_specs=[
            pl.BlockSpec((bsz, tq, D_MODEL), lambda i: (0, prv(i), 0)),
            const((nd, D_MODEL)),
            const((bsz, WINDOW, D_KV)),
            const((bsz, WINDOW, D_KV)),
        ],
        out_shape=[
            jax.ShapeDtypeStruct((bsz, t, D_MODEL), F32),
            jax.ShapeDtypeStruct((nd, D_MODEL), F32),
            jax.ShapeDtypeStruct((bsz, WINDOW, D_KV), F32),
            jax.ShapeDtypeStruct((bsz, WINDOW, D_KV), F32),
        ],
        scratch_shapes=[
            pltpu.VMEM((bsz, WINDOW, D_KV), F32),
            pltpu.VMEM((bsz, WINDOW, D_KV), F32),
            pltpu.VMEM((bsz, tq, D_ATTN), F32),
        ],
        compiler_params=pltpu.CompilerParams(
            dimension_semantics=("arbitrary",), vmem_limit_bytes=VMEM_LIMIT),
        name="attn_ffn",
    )(q3d, kv3d, tab_a, tab_b, qw, kw, sinks, ones_bd, bias, x3d, yr3d, xd, yrd, yad, wo, nw, wu, wd)


DEC_TILE = 16


def _decode_prep_kernel(p_ref, sh_ref, q_ref, kv_ref, mu_ref, prm_ref, wl_ref, ones_ref, tab_ref,
                        qw_ref, kw_ref, vec_ref, vgb_ref, qn_ref, kvn_ref):
    p = p_ref[...]
    xs = p + (sh_ref[...] - p) * mu_ref[...]
    r, logw, k, v, a, b, g, bonus = _rwkv_features(xs, prm_ref[...], wl_ref, ones_ref)
    for i, x in enumerate((a, b, k, jnp.exp(logw), r, v)):
        vec_ref[i] = jnp.transpose(x)
    vgb_ref[0] = g
    vgb_ref[1] = bonus
    n = p.shape[0]
    tab = jnp.broadcast_to(tab_ref[0:1, :], (n, 4 * HEAD_DIM))
    cos_t, sin_lo, sin_hi = _rope_tables(tab[:, :128], tab[:, 128:])
    qn_ref[...] = _qk_norm_rope(q_ref[...], qw_ref[...], _tile_lanes(cos_t, 4), _tile_lanes(sin_lo, 4),
                                _tile_lanes(sin_hi, 4), ones_ref)
    kv = kv_ref[...]
    kvn_ref[:, 0:D_KV] = _qk_norm_rope(kv[:, 0:D_KV], kw_ref[...], cos_t, sin_lo, sin_hi, ones_ref)
    kvn_ref[:, D_KV:] = kv[:, D_KV:]


def _decode_prep(p, shift, q, kv, mu_pad, prm, wl, bmask, tab, qw, kw):
    n = p.shape[0]
    full = lambda shape: pl.BlockSpec(shape, lambda i: (0,) * len(shape))
    return pl.pallas_call(
        _decode_prep_kernel,
        grid=(1,),
        in_specs=[full((n, D_SHIFT_PAD)), full((n, D_SHIFT_PAD)), full((n, D_ATTN)), full((n, 2 * D_KV)),
                  full((1, D_SHIFT_PAD)), full((16, D_RWKV)), full((D_LORA_PAD, 3 * D_RWKV)),
                  full((QUAD, QUAD)), full((8, 4 * HEAD_DIM)), full((1, D_ATTN)), full((1, D_KV))],
        out_specs=[full((6, D_RWKV, n)), full((2, n, D_RWKV)), full((n, D_ATTN)), full((n, 2 * D_KV))],
        out_shape=[
            jax.ShapeDtypeStruct((6, D_RWKV, n), F32),
            jax.ShapeDtypeStruct((2, n, D_RWKV), F32),
            jax.ShapeDtypeStruct((n, D_ATTN), F32),
            jax.ShapeDtypeStruct((n, 2 * D_KV), F32),
        ],
        compiler_params=pltpu.CompilerParams(
            dimension_semantics=("arbitrary",), vmem_limit_bytes=VMEM_LIMIT),
        name="decode_prep",
    )(p, shift, q, kv, mu_pad, prm, wl, bmask, tab, qw, kw)


def _decode_state_kernel(vec_ref, gb_ref, prm_ref, ones_ref, s_ref, sout_ref, yr_ref, yt_ref):
    h = pl.program_id(0)
    a_t, b_t, k_t, w_t, r_t = (vec_ref[i] for i in range(5))

    def body(i, carry):
        s = s_ref[0, i]
        sa = jnp.sum(s * a_t, axis=0, keepdims=True)
        v_i = vec_ref[5, pl.ds(i, 1), :]
        s_new = s * w_t + sa * b_t + v_i * k_t
        sout_ref[0, i] = s_new
        yt_ref[pl.ds(h * HEAD_DIM + i, 1), :] = jnp.sum(s_new * r_t, axis=0, keepdims=True)
        return carry

    lax.fori_loop(0, HEAD_DIM, body, 0, unroll=8)

    @pl.when(h == pl.num_programs(0) - 1)
    def _():
        y = jnp.transpose(yt_ref[...])
        yr_ref[...] = _rwkv_finish(y, gb_ref[0], gb_ref[1], prm_ref[...], ones_ref)


def _decode_state(vec_t, gb, prm, bmask, s_t):
    n = s_t.shape[-1]
    const = lambda shape: pl.BlockSpec(shape, lambda h: (0,) * len(shape))
    return pl.pallas_call(
        _decode_state_kernel,
        grid=(H_RWKV,),
        in_specs=[
            pl.BlockSpec((6, HEAD_DIM, n), lambda h: (0, h, 0)),
            const((2, n, D_RWKV)),
            const((16, D_RWKV)),
            const((QUAD, QUAD)),
            pl.BlockSpec((1, HEAD_DIM, HEAD_DIM, n), lambda h: (h, 0, 0, 0)),
        ],
        out_specs=[
            pl.BlockSpec((1, HEAD_DIM, HEAD_DIM, n), lambda h: (h, 0, 0, 0)),
            const((n, D_RWKV)),
        ],
        out_shape=[
            jax.ShapeDtypeStruct((H_RWKV, HEAD_DIM, HEAD_DIM, n), F32),
            jax.ShapeDtypeStruct((n, D_RWKV), F32),
        ],
        scratch_shapes=[pltpu.VMEM((D_RWKV, n), F32)],
        compiler_params=pltpu.CompilerParams(
            dimension_semantics=("arbitrary",), vmem_limit_bytes=VMEM_LIMIT),
        name="decode_state",
    )(vec_t, gb, prm, bmask, s_t)


def _decode_attn_kernel(qr_ref, kvn_ref, col_ref, ck_ref, cv_ref, sink_ref, ya_ref, kout_ref, vout_ref):
    nh = N_Q_HEADS
    seqs = range(DEC_TILE)
    hrow = lax.broadcasted_iota(jnp.int32, (nh, D_ATTN), 0)
    hlane = lax.broadcasted_iota(jnp.int32, (nh, D_ATTN), 1) // HEAD_DIM
    dmask = hrow == hlane
    grow = lax.broadcasted_iota(jnp.int32, (nh, D_KV), 0) // GQA_GROUP
    glane = lax.broadcasted_iota(jnp.int32, (nh, D_KV), 1) // HEAD_DIM
    gmask = grow == glane
    low = glane == 0
    key_idx = lax.broadcasted_iota(jnp.int32, (nh, WINDOW), 1)
    last = lax.broadcasted_iota(jnp.int32, (D_KV, WINDOW), 1) == WINDOW - 1
    sink = sink_ref[...]
    kvn = kvn_ref[...]
    col = col_ref[0]
    k_new = [kvn[j:j + 1, 0:D_KV] for j in seqs]
    v_new = [kvn[j:j + 1, D_KV:] for j in seqs]
    ck = [ck_ref[j] for j in seqs]
    cv = [cv_ref[j] for j in seqs]
    for j in seqs:
        kout_ref[j] = jnp.where(last, col[0:D_KV, j:j + 1], pltpu.roll(ck[j], WINDOW - 1, 1))
        vout_ref[j] = jnp.where(last, col[D_KV:, j:j + 1], pltpu.roll(cv[j], WINDOW - 1, 1))
    q8 = [qr_ref[j * nh:(j + 1) * nh, :] for j in seqs]
    qp = [jnp.where(gmask, jnp.concatenate([q8[j], q8[j]], axis=1), 0.0) for j in seqs]
    s_c = [jnp.where(key_idx >= 1, jnp.dot(qp[j], ck[j], preferred_element_type=F32) * ATTN_SCALE, NEG_INF)
           for j in seqs]
    s_n = [jnp.sum(qp[j] * k_new[j], axis=-1, keepdims=True) * ATTN_SCALE for j in seqs]
    m = [jnp.maximum(jnp.maximum(jnp.max(s_c[j], axis=-1, keepdims=True), s_n[j]), sink) for j in seqs]
    e_c = [jnp.exp(s_c[j] - m[j]) for j in seqs]
    e_n = [jnp.exp(s_n[j] - m[j]) for j in seqs]
    denom = [jnp.sum(e_c[j], axis=-1, keepdims=True) + e_n[j] + jnp.exp(sink - m[j]) for j in seqs]
    o = [(_dot_nt_f32(e_c[j], cv[j]) + e_n[j] * v_new[j]) / denom[j]
         for j in seqs]
    out_rows = []
    for j in seqs:
        rot = pltpu.roll(o[j], HEAD_DIM, 1)
        g0 = jnp.where(low, o[j], rot)
        g1 = jnp.where(low, rot, o[j])
        wide = jnp.concatenate([g0, g0, g1, g1], axis=1)
        out_rows.append(jnp.sum(jnp.where(dmask, wide, 0.0), axis=0, keepdims=True))
    ya_ref[...] = jnp.concatenate(out_rows, axis=0)


def _decode_attn(q_r, kvn, cols, ck_t, cv_t, sinks_col):
    n = kvn.shape[0]
    bt = DEC_TILE
    const = lambda shape: pl.BlockSpec(shape, lambda i: (0,) * len(shape))
    return pl.pallas_call(
        _decode_attn_kernel,
        grid=(n // bt,),
        in_specs=[
            pl.BlockSpec((bt * N_Q_HEADS, HEAD_DIM), lambda i: (i, 0)),
            pl.BlockSpec((bt, 2 * D_KV), lambda i: (i, 0)),
            pl.BlockSpec((1, 2 * D_KV, bt), lambda i: (i, 0, 0)),
            pl.BlockSpec((bt, D_KV, WINDOW), lambda i: (i, 0, 0)),
            pl.BlockSpec((bt, D_KV, WINDOW), lambda i: (i, 0, 0)),
            const((N_Q_HEADS, 1)),
        ],
        out_specs=[
            pl.BlockSpec((bt, D_ATTN), lambda i: (i, 0)),
            pl.BlockSpec((bt, D_KV, WINDOW), lambda i: (i, 0, 0)),
            pl.BlockSpec((bt, D_KV, WINDOW), lambda i: (i, 0, 0)),
        ],
        out_shape=[
            jax.ShapeDtypeStruct((n, D_ATTN), F32),
            jax.ShapeDtypeStruct((n, D_KV, WINDOW), F32),
            jax.ShapeDtypeStruct((n, D_KV, WINDOW), F32),
        ],
        compiler_params=pltpu.CompilerParams(
            dimension_semantics=("arbitrary",), vmem_limit_bytes=VMEM_LIMIT),
        name="decode_attn",
    )(q_r, kvn, cols, ck_t, cv_t, sinks_col)


def _pad_cols(w, at, n):
    return jnp.concatenate([w[..., :at], jnp.zeros(w.shape[:-1] + (n,), w.dtype), w[..., at:]], axis=-1)


def kernel(x_prompt, x_sample, state_wkv, state_shift, cache_k_win, cache_v_win, norm_mix_w, w_in, mu_shift, w0, w_decay_up, a0, w_a_up, w_g_up, k_k, k_a, r_k, ln_x_w, ln_x_b, q_norm_w, k_norm_w, sinks, w_out, norm_ffn_w, w_ffn_up, w_ffn_down):
    bsz, t, _ = x_prompt.shape
    nd = x_sample.shape[0]
    l = 0
    pad = D_LORA_PAD - D_LORA

    w_in_t = jnp.swapaxes(w_in[l], 0, 1)
    w_in_pad = jnp.concatenate([w_in_t[:D_SHIFT], jnp.zeros((pad, D_MODEL), F32), w_in_t[D_SHIFT:]],
                               axis=0).astype(BF16)
    mu_pad = _pad_cols(mu_shift[l][None, :], D_SHIFT, pad)
    wl = jnp.zeros((D_LORA_PAD, 3 * D_RWKV), F32)
    wl = wl.at[0:32, 0:D_RWKV].set(w_decay_up[l])
    wl = wl.at[32:64, D_RWKV:2 * D_RWKV].set(w_a_up[l])
    wl = wl.at[64:160, 2 * D_RWKV:].set(w_g_up[l])
    wl = wl.astype(BF16)
    prm = jnp.zeros((16, D_RWKV), F32)
    prm = prm.at[0].set(w0[l]).at[1].set(a0[l]).at[2].set(k_k[l]).at[3].set(k_a[l])
    prm = prm.at[4].set(r_k[l].reshape(-1)).at[5].set(ln_x_w[l]).at[6].set(ln_x_b[l])
    hid = np.arange(QUAD) // HEAD_DIM
    bmask = jnp.asarray(hid[:, None] == hid[None, :], BF16)
    tri = jnp.asarray(np.tril(np.ones((CHUNK, CHUNK))), BF16)
    qw = jnp.tile(q_norm_w[l][None, :], (1, N_Q_HEADS))
    kw = jnp.tile(k_norm_w[l][None, :], (1, N_KV_HEADS))
    nmw = norm_mix_w[l][None, :]
    nfw = norm_ffn_w[l][None, :]
    wo = w_out[l].astype(BF16)
    wu = w_ffn_up[l].astype(BF16)
    wd = w_ffn_down[l].astype(BF16)
    ang_s = (jnp.zeros((8, 1), F32) + PAST_LEN) * _rope_lane_freq()
    tab_s = jnp.concatenate([jnp.cos(ang_s), jnp.sin(ang_s)], axis=1)

    xs = x_sample.reshape(nd, D_MODEL)
    p_s, q_s, kv_s = _inproj(xs, nmw, w_in_pad, 128)
    shift_in = _pad_cols(state_shift[l].reshape(nd, D_SHIFT), D_SHIFT, pad)
    vec_t, gb, qn_s, kvn_s = _decode_prep(p_s, shift_in, q_s, kv_s, mu_pad, prm, wl, bmask, tab_s, qw, kw)
    s_t = jnp.transpose(state_wkv[l], (1, 2, 3, 0))
    ck_t = jnp.swapaxes(cache_k_win[l].reshape(nd, WINDOW, D_KV), 1, 2)
    cv_t = jnp.swapaxes(cache_v_win[l].reshape(nd, WINDOW, D_KV), 1, 2)
    q_r = qn_s.reshape(nd * N_Q_HEADS, HEAD_DIM)
    cols = jnp.swapaxes(kvn_s.reshape(nd // DEC_TILE, DEC_TILE, 2 * D_KV), 1, 2)
    wkv_t, yr_s = _decode_state(vec_t, gb, prm, bmask, s_t)
    ya_s, kc_t, vc_t = _decode_attn(q_r, kvn_s, cols, ck_t, cv_t, sinks[l][:, None])

    xp = x_prompt.reshape(bsz * t, D_MODEL)
    xs_p, plast, q_p, kv_p = _inproj_shift(xp, nmw, w_in_pad, mu_pad, t, 1024)
    yr_p, hbd = _rwkv_prompt(xs_p.reshape(bsz, t, D_SHIFT_PAD), prm, wl, tri, bmask)
    y_prompt, y_s, kwin_p, vwin_p = _attn_ffn(q_p.reshape(bsz, t, D_ATTN), kv_p.reshape(bsz, t, 2 * D_KV),
                                              qw, kw, sinks[l][None, :], bmask, _band_bias(),
                                              x_prompt, yr_p, xs, yr_s, ya_s, wo, nfw, wu, wd)
    hb = hbd.reshape(bsz, 2, 4, HEAD_DIM, 2, HEAD_DIM)
    wkv_prompt = jnp.stack([hb[:, :, j, :, j % 2, :] for j in range(4)], axis=2)
    wkv_prompt = wkv_prompt.reshape(bsz, H_RWKV, HEAD_DIM, HEAD_DIM)[None]
    shift_prompt = plast[:, 0:1, :D_SHIFT][None]
    k_win_prompt = kwin_p.reshape(bsz, WINDOW, N_KV_HEADS, HEAD_DIM)[None]
    v_win_prompt = vwin_p.reshape(bsz, WINDOW, N_KV_HEADS, HEAD_DIM)[None]

    y_sample = y_s.reshape(nd, 1, D_MODEL)
    wkv_sample = jnp.transpose(wkv_t, (3, 0, 1, 2))[None]
    shift_sample = p_s[:, :D_SHIFT].reshape(nd, 1, D_SHIFT)[None]
    k_win_sample = jnp.swapaxes(kc_t, 1, 2).reshape(nd, WINDOW, N_KV_HEADS, HEAD_DIM)[None]
    v_win_sample = jnp.swapaxes(vc_t, 1, 2).reshape(nd, WINDOW, N_KV_HEADS, HEAD_DIM)[None]

    return (y_prompt, y_sample, wkv_prompt, shift_prompt, k_win_prompt, v_win_prompt,
            wkv_sample, shift_sample, k_win_sample, v_win_sample)
```

```python
import functools

import jax
import jax.numpy as jnp
import numpy as np
from jax import lax
from jax.experimental import pallas as pl
from jax.experimental.pallas import tpu as pltpu

F32 = jnp.float32
BF16 = jnp.bfloat16

D_MODEL = 1024
D_RWKV = 512
D_ATTN = 512
HEAD_DIM = 64
H_RWKV = 8
N_Q_HEADS = 8
N_KV_HEADS = 2
GQA_GROUP = 4
D_KV = 128
D_LORA = 160
D_LORA_PAD = 256
D_SHIFT = 3 * D_RWKV + D_LORA
D_SHIFT_PAD = 3 * D_RWKV + D_LORA_PAD
D_IN_PAD = D_SHIFT_PAD + D_ATTN + 2 * D_KV
WINDOW = 128
ROPE_DIM = 16
ROPE_HALF = 8
ROPE_THETA = 500000.0
ATTN_SCALE = HEAD_DIM ** -0.5
D_FF = 4096
RMS_EPS = 1e-6
LNX_EPS = 64e-5
NEG_INF = -1e30
LOG2E = 1.4426950408889634
PAST_LEN = 16384

CHUNK = 64
QUAD = 4 * HEAD_DIM
VMEM_LIMIT = 56 * 1024 * 1024


def _split2(x):
    hi = x.astype(BF16)
    lo = (x - hi.astype(F32)).astype(BF16)
    return hi, lo


def _head_sums(xs, ones_ref):
    n, w = xs[0].shape
    tile = min(w, QUAD)
    per = w // tile
    pieces = [x[:, j * tile:(j + 1) * tile] for x in xs for j in range(per)]
    stacked = jnp.concatenate(pieces, axis=0) if len(pieces) > 1 else pieces[0]
    ones = ones_ref[0:tile, 0:tile]
    out = jnp.dot(stacked.astype(BF16), ones, preferred_element_type=F32)
    res = []
    for i in range(len(xs)):
        cols = [out[(i * per + j) * n:(i * per + j + 1) * n] for j in range(per)]
        res.append(jnp.concatenate(cols, axis=1) if per > 1 else cols[0])
    return res


def _cumsum_rows(tri_bf16, x):
    hi, lo = _split2(x)
    return (jnp.dot(tri_bf16, hi, preferred_element_type=F32)
            + jnp.dot(tri_bf16, lo, preferred_element_type=F32))


def _mm(a, b):
    return jnp.dot(a.astype(BF16), b.astype(BF16), preferred_element_type=F32)


def _mm_nt(a, b):
    return lax.dot_general(a.astype(BF16), b.astype(BF16), (((1,), (1,)), ((), ())),
                           preferred_element_type=F32)


def _mm_tn(a, b):
    return lax.dot_general(a.astype(BF16), b.astype(BF16), (((0,), (0,)), ((), ())),
                           preferred_element_type=F32)


def _dot_nt_f32(a, b):
    return lax.dot_general(a, b, (((1,), (1,)), ((), ())), preferred_element_type=F32)


def _sigmoid(x):
    return 1.0 / (1.0 + jnp.exp(-x))


def _interleave(*gens):
    live = list(gens)
    while live:
        for g in list(live):
            try:
                next(g)
            except StopIteration:
                live.remove(g)


def _norm_project(x, nw_ref, wt_ref):
    ms = jnp.mean(x * x, axis=-1, keepdims=True)
    h = (x * lax.rsqrt(ms + RMS_EPS)) * nw_ref[...]
    return _mm_nt(h, wt_ref[...])


def _inproj_kernel(x_ref, nw_ref, w_ref, p_ref, q_ref, kv_ref):
    out = _norm_project(x_ref[...], nw_ref, w_ref)
    p_ref[...] = out[:, :D_SHIFT_PAD]
    q_ref[...] = out[:, D_SHIFT_PAD:D_SHIFT_PAD + D_ATTN]
    kv_ref[...] = out[:, D_SHIFT_PAD + D_ATTN:]


def _inproj_shift_kernel(tiles_per_seq, x_ref, nw_ref, w_ref, mu_ref, xs_ref, last_ref, q_ref, kv_ref, prev_ref):
    i = pl.program_id(0)

    @pl.when(i % tiles_per_seq == 0)
    def _():
        prev_ref[...] = jnp.zeros_like(prev_ref)

    out = _norm_project(x_ref[...], nw_ref, w_ref)
    p = out[:, :D_SHIFT_PAD]
    tm = p.shape[0]
    row = lax.broadcasted_iota(jnp.int32, p.shape, 0)
    prev = jnp.where(row == 0, jnp.broadcast_to(prev_ref[0:1, :], p.shape), pltpu.roll(p, 1, 0))
    xs_ref[...] = p + (prev - p) * mu_ref[...]
    last = jnp.broadcast_to(p[tm - 1:tm, :], prev_ref.shape)
    prev_ref[...] = last
    last_ref[0] = last
    q_ref[...] = out[:, D_SHIFT_PAD:D_SHIFT_PAD + D_ATTN]
    kv_ref[...] = out[:, D_SHIFT_PAD + D_ATTN:]


def _inproj_shift(x2d, norm_w, w_in_pad, mu_pad, seq_len, tm):
    m = x2d.shape[0]
    tiles_per_seq = seq_len // tm
    return pl.pallas_call(
        functools.partial(_inproj_shift_kernel, tiles_per_seq),
        grid=(m // tm,),
        in_specs=[
            pl.BlockSpec((tm, D_MODEL), lambda i: (i, 0)),
            pl.BlockSpec((1, D_MODEL), lambda i: (0, 0)),
            pl.BlockSpec((D_IN_PAD, D_MODEL), lambda i: (0, 0)),
            pl.BlockSpec((1, D_SHIFT_PAD), lambda i: (0, 0)),
        ],
        out_specs=[
            pl.BlockSpec((tm, D_SHIFT_PAD), lambda i: (i, 0)),
            pl.BlockSpec((1, 8, D_SHIFT_PAD), lambda i: (i // tiles_per_seq, 0, 0)),
            pl.BlockSpec((tm, D_ATTN), lambda i: (i, 0)),
            pl.BlockSpec((tm, 2 * D_KV), lambda i: (i, 0)),
        ],
        out_shape=[
            jax.ShapeDtypeStruct((m, D_SHIFT_PAD), F32),
            jax.ShapeDtypeStruct((m // seq_len, 8, D_SHIFT_PAD), F32),
            jax.ShapeDtypeStruct((m, D_ATTN), F32),
            jax.ShapeDtypeStruct((m, 2 * D_KV), F32),
        ],
        scratch_shapes=[pltpu.VMEM((8, D_SHIFT_PAD), F32)],
        compiler_params=pltpu.CompilerParams(
            dimension_semantics=("arbitrary",), vmem_limit_bytes=VMEM_LIMIT),
        name="inproj_shift",
    )(x2d, norm_w, w_in_pad, mu_pad)


def _inproj(x2d, norm_w, w_in_pad, tm):
    m = x2d.shape[0]
    return pl.pallas_call(
        _inproj_kernel,
        grid=(m // tm,),
        in_specs=[
            pl.BlockSpec((tm, D_MODEL), lambda i: (i, 0)),
            pl.BlockSpec((1, D_MODEL), lambda i: (0, 0)),
            pl.BlockSpec((D_IN_PAD, D_MODEL), lambda i: (0, 0)),
        ],
        out_specs=[
            pl.BlockSpec((tm, D_SHIFT_PAD), lambda i: (i, 0)),
            pl.BlockSpec((tm, D_ATTN), lambda i: (i, 0)),
            pl.BlockSpec((tm, 2 * D_KV), lambda i: (i, 0)),
        ],
        out_shape=[
            jax.ShapeDtypeStruct((m, D_SHIFT_PAD), F32),
            jax.ShapeDtypeStruct((m, D_ATTN), F32),
            jax.ShapeDtypeStruct((m, 2 * D_KV), F32),
        ],
        compiler_params=pltpu.CompilerParams(
            dimension_semantics=("arbitrary",), vmem_limit_bytes=VMEM_LIMIT),
        name="inproj",
    )(x2d, norm_w, w_in_pad)


def _rwkv_features(xs, prm, wl_ref, ones_ref):
    r = xs[:, 0:D_RWKV]
    k = xs[:, D_RWKV:2 * D_RWKV]
    v = xs[:, 2 * D_RWKV:3 * D_RWKV]
    lora = xs[:, 3 * D_RWKV:]
    col = lax.broadcasted_iota(jnp.int32, lora.shape, 1)
    act = jnp.where(col < 32, jnp.tanh(lora), jnp.where(col < 64, lora, _sigmoid(lora)))
    up = jnp.dot(act.astype(BF16), wl_ref[...], preferred_element_type=F32)
    w0, a0, k_k, k_a, r_k = prm[0:1], prm[1:2], prm[2:3], prm[3:4], prm[4:5]
    logw = (-np.exp(-0.5)) * _sigmoid(w0 + up[:, 0:D_RWKV])
    asig = _sigmoid(a0 + up[:, D_RWKV:2 * D_RWKV])
    g = up[:, 2 * D_RWKV:]
    kk = k * k_k
    k_mod = k * (1.0 + (asig - 1.0) * k_a)
    ss, rk = _head_sums([kk * kk, r * k_mod * r_k], ones_ref)
    kk = kk / jnp.maximum(jnp.sqrt(ss), 1e-12)
    k = k_mod
    bonus = rk * v
    return r, logw, k, v, -kk, kk * asig, g, bonus


def _rwkv_finish(y, g, bonus, prm, ones_ref):
    ln_w, ln_b = prm[5:6], prm[6:7]
    mean = _head_sums([y], ones_ref)[0] * (1.0 / HEAD_DIM)
    d = y - mean
    var = _head_sums([d * d], ones_ref)[0] * (1.0 / HEAD_DIM)
    yn = d * lax.rsqrt(var + LNX_EPS) * ln_w + ln_b
    return (yn + bonus) * g


def _block_diag(x, bmask):
    return jnp.concatenate([x] * 4, axis=0) * bmask


def _chunk_prep(r, logw, k, v, a, b, tri_ref, bmask):
    n = len(r)
    ch = range(n)
    tri = tri_ref[...]
    cum = [_cumsum_rows(tri, logw[i]) for i in ch]
    e_in = [jnp.exp(cum[i]) for i in ch]
    e_ex = [jnp.exp(cum[i] - logw[i]) for i in ch]
    e_inv = [1.0 / e_in[i] for i in ch]
    e_last = [e_in[i][CHUNK - 1:CHUNK, :] for i in ch]
    rt = [(r[i] * e_in[i]).astype(BF16) for i in ch]
    at = [(a[i] * e_ex[i]).astype(BF16) for i in ch]
    kt = [(k[i] * e_inv[i]).astype(BF16) for i in ch]
    bt = [(b[i] * e_inv[i]).astype(BF16) for i in ch]
    vb = [v[i].astype(BF16) for i in ch]

    t_idx = lax.broadcasted_iota(jnp.int32, (CHUNK, QUAD), 0)
    s_idx = lax.broadcasted_iota(jnp.int32, (CHUNK, QUAD), 1) & (HEAD_DIM - 1)
    strict = s_idx < t_idx
    incl = s_idx <= t_idx

    gm = [_mm_nt(jnp.concatenate([at[i], rt[i]], axis=0),
                 jnp.concatenate([_block_diag(bt[i], bmask), _block_diag(kt[i], bmask)], axis=0))
          for i in ch]
    a_ab = [jnp.where(strict, gm[i][:CHUNK, :QUAD], 0.0) for i in ch]
    a_ak = [jnp.where(strict, gm[i][:CHUNK, QUAD:], 0.0) for i in ch]
    a_rb = [jnp.where(incl, gm[i][CHUNK:, :QUAD], 0.0) for i in ch]
    a_rk = [jnp.where(incl, gm[i][CHUNK:, QUAD:], 0.0) for i in ch]

    eye = jnp.where(s_idx == t_idx, 1.0, 0.0)
    pwb = [a_ab[i].astype(BF16) for i in ch]
    t_inv = [eye + a_ab[i] for i in ch]
    for it in range(6):
        rbd = [_block_diag(pwb[i], bmask) for i in ch]
        if it == 0:
            pwb = [_mm(pwb[i], rbd[i]).astype(BF16) for i in ch]
        elif it < 5:
            out = [_mm(jnp.concatenate([pwb[i], t_inv[i].astype(BF16)], axis=0), rbd[i]) for i in ch]
            pwb = [out[i][:CHUNK].astype(BF16) for i in ch]
            t_inv = [t_inv[i] + out[i][CHUNK:] for i in ch]
        else:
            t_inv = [t_inv[i] + _mm(t_inv[i], rbd[i]) for i in ch]

    vbd = [_block_diag(vb[i], bmask) for i in ch]
    xy0 = [_mm(jnp.concatenate([a_ak[i], a_rk[i]], axis=0), vbd[i]) for i in ch]
    return [dict(ar=jnp.concatenate([at[i], rt[i]], axis=0), x0=xy0[i][:CHUNK], y0=xy0[i][CHUNK:],
                 t_inv=t_inv[i].astype(BF16), a_rb=a_rb[i].astype(BF16), vb=vb[i],
                 bk=jnp.concatenate([bt[i], kt[i]], axis=0), e_last=e_last[i]) for i in ch]


def _chunk_step(pre, state, bmask, out):
    ch = range(len(pre))
    half = QUAD // 2
    zeros = jnp.zeros((half, half), BF16)
    sc = [state[i].astype(BF16) for i in ch]
    sb = [jnp.concatenate([jnp.concatenate([sc[i][:half], zeros], axis=1),
                           jnp.concatenate([zeros, sc[i][half:]], axis=1)], axis=0) for i in ch]
    xr = [_mm_nt(pre[i]["ar"], sb[i]) for i in ch]
    yield
    x = [xr[i][:CHUNK] + pre[i]["x0"] for i in ch]
    u = [_mm(pre[i]["t_inv"], _block_diag(x[i].astype(BF16), bmask)) for i in ch]
    yield
    ub = [u[i].astype(BF16) for i in ch]
    out["y"] = [xr[i][CHUNK:] + pre[i]["y0"] + _mm(pre[i]["a_rb"], _block_diag(ub[i], bmask)) for i in ch]
    upd = [_mm_tn(jnp.concatenate([ub[i], pre[i]["vb"]], axis=0), pre[i]["bk"]) for i in ch]
    yield
    bm = bmask[:half, :half].astype(F32)
    s_new = []
    for i in ch:
        e_last = pre[i]["e_last"]
        top = (state[i][:half] + upd[i][:half, :half] * bm) * e_last[:, :half]
        bot = (state[i][half:] + upd[i][half:, half:] * bm) * e_last[:, half:]
        s_new.append(jnp.concatenate([top, bot], axis=0))
    out["state"] = s_new


def _finish_stages(ys, g, bonus, prm, ones_ref, y_ref, j):
    nseq = len(ys) // 2
    ln_w, ln_b = prm[5:6], prm[6:7]
    y = jnp.concatenate([jnp.concatenate(ys[2 * s:2 * s + 2], axis=1) for s in range(nseq)], axis=0)
    mean = _head_sums([y], ones_ref)[0] * (1.0 / HEAD_DIM)
    yield
    d = y - mean
    var = _head_sums([d * d], ones_ref)[0] * (1.0 / HEAD_DIM)
    yield
    out = (d * lax.rsqrt(var + LNX_EPS) * ln_w + ln_b + bonus) * g
    for s in range(nseq):
        y_ref[s, j * CHUNK:(j + 1) * CHUNK, :] = out[s * CHUNK:(s + 1) * CHUNK]


def _rwkv_prompt_kernel(xs_ref, prm_ref, wl_ref, tri_ref, bmask_ref, y_ref, hout_ref, h_ref):
    c = pl.program_id(0)
    nseq, tstep, _ = xs_ref.shape
    nchunk = tstep // CHUNK

    @pl.when(c == 0)
    def _():
        h_ref[...] = jnp.zeros_like(h_ref)

    xs = jnp.concatenate([xs_ref[s] for s in range(nseq)], axis=0)
    prm = prm_ref[...]
    r, logw, k, v, a, b, g, bonus = _rwkv_features(xs, prm, wl_ref, bmask_ref)
    bmask = bmask_ref[...]
    lanes = [(s, q) for s in range(nseq) for q in range(2)]
    chains = [(j, s, q) for j in range(nchunk) for s, q in lanes]
    cut = lambda x: [x[s * tstep + j * CHUNK:s * tstep + (j + 1) * CHUNK, q * QUAD:(q + 1) * QUAD]
                     for j, s, q in chains]
    pre = _chunk_prep(cut(r), cut(logw), cut(k), cut(v), cut(a), cut(b), tri_ref, bmask)
    state = [h_ref[s, q] for s, q in lanes]
    crow = lambda x, j: jnp.concatenate([x[s * tstep + j * CHUNK:s * tstep + (j + 1) * CHUNK]
                                         for s in range(nseq)], axis=0)
    finish = iter(())
    for j in range(nchunk):
        res = {}
        _interleave(_chunk_step(pre[j * len(lanes):(j + 1) * len(lanes)], state, bmask, res), finish)
        state = res["state"]
        finish = _finish_stages(res["y"], crow(g, j), crow(bonus, j), prm, bmask_ref, y_ref, j)
    _interleave(finish)
    for i, (s, q) in enumerate(lanes):
        h_ref[s, q] = state[i]

    @pl.when(c == pl.num_programs(0) - 1)
    def _():
        hout_ref[...] = h_ref[...]


RWKV_CHUNKS_PER_STEP = 4


def _rwkv_prompt(xs3d, prm, wl, tri, bmask):
    bsz, t, _ = xs3d.shape
    tstep = RWKV_CHUNKS_PER_STEP * CHUNK
    nc = t // tstep
    const = lambda shape: pl.BlockSpec(shape, lambda c: (0,) * len(shape))
    state_shape = (bsz, 2, QUAD, QUAD // 2)
    return pl.pallas_call(
        _rwkv_prompt_kernel,
        grid=(nc,),
        in_specs=[
            pl.BlockSpec((bsz, tstep, D_SHIFT_PAD), lambda c: (0, c, 0)),
            const((16, D_RWKV)),
            const((D_LORA_PAD, 3 * D_RWKV)),
            const((CHUNK, CHUNK)),
            const((QUAD, QUAD)),
        ],
        out_specs=[
            pl.BlockSpec((bsz, tstep, D_RWKV), lambda c: (0, c, 0)),
            const(state_shape),
        ],
        out_shape=[
            jax.ShapeDtypeStruct((bsz, t, D_RWKV), F32),
            jax.ShapeDtypeStruct(state_shape, F32),
        ],
        scratch_shapes=[pltpu.VMEM(state_shape, F32)],
        compiler_params=pltpu.CompilerParams(
            dimension_semantics=("arbitrary",), vmem_limit_bytes=VMEM_LIMIT),
        name="rwkv_prompt",
    )(xs3d, prm, wl, tri, bmask)


def _rope_lane_freq():
    inv_freq = jnp.power(ROPE_THETA, -jnp.arange(ROPE_HALF, dtype=F32) * (2.0 / ROPE_DIM))
    return inv_freq[(np.arange(2 * HEAD_DIM) % HEAD_DIM) % ROPE_HALF][None, :]


def _rope_tables(cos, sin):
    dim = lax.broadcasted_iota(jnp.int32, cos.shape, 1) & (HEAD_DIM - 1)
    cos_t = jnp.where(dim < ROPE_DIM, cos, 1.0)
    sin_lo = jnp.where(dim < ROPE_HALF, -sin, 0.0)
    sin_hi = jnp.where((dim >= ROPE_HALF) & (dim < ROPE_DIM), sin, 0.0)
    return cos_t, sin_lo, sin_hi


def _qk_norm_rope(x, norm_w, cos_t, sin_lo, sin_hi, ones_ref):
    ms = _head_sums([x * x], ones_ref)[0] * (1.0 / HEAD_DIM)
    xn = x * lax.rsqrt(ms + RMS_EPS) * norm_w
    width = x.shape[1]
    fwd = pltpu.roll(xn, width - ROPE_HALF, 1)
    bwd = pltpu.roll(xn, ROPE_HALF, 1)
    return xn * cos_t + fwd * sin_lo + bwd * sin_hi


def _tile_lanes(x, reps):
    return jnp.concatenate([x] * reps, axis=1) if reps > 1 else x


def _attn_stages(q_ref, kv_ref, taba_ref, tabb_ref, qw_ref, kw_ref, sink_ref, ones_ref, bias_ref, first_bias,
                 kprev_ref, vprev_ref, out):
    nseq, tq, _ = q_ref.shape
    nblk = tq // WINDOW
    units = [(s, b) for s in range(nseq) for b in range(nblk)]
    blk = lambda b: slice(b * WINDOW, (b + 1) * WINDOW)
    tb = tabb_ref[...]
    cos_b, sin_b = tb[:, :128], tb[:, 128:]
    rope, rope4 = [], []
    for b in range(nblk):
        ta = taba_ref[b][0:1, :]
        cos_a, sin_a = ta[:, :128], ta[:, 128:]
        tabs = _rope_tables(cos_a * cos_b - sin_a * sin_b, sin_a * cos_b + cos_a * sin_b)
        rope.append(tabs)
        rope4.append([_tile_lanes(x, 4) for x in tabs])
    q = {(s, b): _qk_norm_rope(q_ref[s, blk(b), :], qw_ref[...], *rope4[b], ones_ref) * (ATTN_SCALE * LOG2E)
         for s, b in units}
    kv = {(s, b): kv_ref[s, blk(b), :] for s, b in units}
    k_cur = {u: _qk_norm_rope(kv[u][:, 0:D_KV], kw_ref[...], *rope[u[1]], ones_ref) for u in units}
    v_cur = {u: kv[u][:, D_KV:] for u in units}
    k_all = {(s, b): jnp.concatenate([kprev_ref[s] if b == 0 else k_cur[(s, b - 1)], k_cur[(s, b)]], axis=0)
             for s, b in units}
    v_all = {(s, b): jnp.concatenate([vprev_ref[s] if b == 0 else v_cur[(s, b - 1)], v_cur[(s, b)]], axis=0)
             for s, b in units}
    out["k_cur"] = [k_cur[(s, nblk - 1)] for s in range(nseq)]
    out["v_cur"] = [v_cur[(s, nblk - 1)] for s in range(nseq)]
    for s in range(nseq):
        kprev_ref[s] = out["k_cur"][s]
        vprev_ref[s] = out["v_cur"][s]
    yield

    nk = 2 * WINDOW
    bias = [bias_ref[first_bias] if b == 0 else bias_ref[1] for b in range(nblk)]
    sinks = sink_ref[...] * LOG2E
    low = lax.broadcasted_iota(jnp.int32, (nk, D_KV), 1) < HEAD_DIM
    lane_blk = [ones_ref[j * HEAD_DIM:j * HEAD_DIM + 1, :] for j in range(GQA_GROUP)]

    chains = [(u, g) for u in units for g in range(N_KV_HEADS)]
    ch = range(len(chains))
    k_rot = {u: pltpu.roll(k_all[u], HEAD_DIM, 1) for u in units}
    k2 = [jnp.where(low, k_all[u], k_rot[u]) if g == 0 else jnp.where(low, k_rot[u], k_all[u])
          for u, g in chains]
    k4 = [jnp.concatenate([k2[c], k2[c]], axis=1).astype(BF16) for c in ch]
    vb = {u: v_all[u].astype(BF16) for u in units}
    qg = [q[u][:, g * QUAD:(g + 1) * QUAD].astype(BF16) for u, g in chains]
    qstack = [jnp.concatenate([qg[c] * lane_blk[j] for j in range(GQA_GROUP)], axis=0) for c in ch]
    sink_row = [jnp.concatenate(
        [jnp.broadcast_to(sinks[:, g * GQA_GROUP + j:g * GQA_GROUP + j + 1], (1, WINDOW))
         for j in range(GQA_GROUP)], axis=1) for u, g in chains]
    yield
    sc = [_mm_nt(k4[c], qstack[c]) + bias[u[1]] for c, (u, g) in enumerate(chains)]
    yield
    m = [jnp.maximum(jnp.max(sc[c], axis=0, keepdims=True), sink_row[c]) for c in ch]
    e = [jnp.exp2(sc[c] - m[c]) for c in ch]
    yield
    denom = [jnp.sum(e[c], axis=0, keepdims=True) + jnp.exp2(sink_row[c] - m[c]) for c in ch]
    ot = {(u, g): _mm_tn(vb[u], e[c].astype(BF16))[g * HEAD_DIM:(g + 1) * HEAD_DIM, :] * (1.0 / denom[c])
          for c, (u, g) in enumerate(chains)}
    yield
    ya = []
    for s in range(nseq):
        blocks = []
        for b in range(nblk):
            yt = jnp.concatenate([ot[((s, b), g)][:, j * WINDOW:(j + 1) * WINDOW]
                                  for g in range(N_KV_HEADS) for j in range(GQA_GROUP)], axis=0)
            blocks.append(jnp.transpose(yt))
        ya.append(jnp.concatenate(blocks, axis=0) if nblk > 1 else blocks[0])
    out["ya"] = ya


def _ffn_stages(x, yr, ya, wo_ref, nw_ref, wu_ref, wd_ref, out, pieces=4):
    mix = jnp.concatenate([yr, ya], axis=1).astype(BF16)
    x1 = x + jnp.dot(mix, wo_ref[...], preferred_element_type=F32)
    yield
    ms = jnp.mean(x1 * x1, axis=-1, keepdims=True)
    hf = ((x1 * lax.rsqrt(ms + RMS_EPS)) * nw_ref[...]).astype(BF16)
    acc = x1
    step = D_FF // pieces
    for j in range(pieces):
        up = jnp.dot(hf, wu_ref[:, j * step:(j + 1) * step], preferred_element_type=F32)
        yield
        act = jnp.square(jnp.maximum(up, 0.0)).astype(BF16)
        acc = acc + jnp.dot(act, wd_ref[j * step:(j + 1) * step, :], preferred_element_type=F32)
        yield
    out["y"] = acc


def _attn_ffn_kernel(q_ref, kv_ref, taba_ref, tabb_ref, qw_ref, kw_ref, sink_ref, ones_ref, bias_ref,
                     x_ref, yr_ref, xd_ref, yrd_ref, yad_ref, wo_ref, nw_ref, wu_ref, wd_ref,
                     o_ref, od_ref, kwin_ref, vwin_ref, kprev_ref, vprev_ref, ya_ref):
    i = pl.program_id(0)
    nseq, tq, _ = q_ref.shape
    seqs = range(nseq)
    first = i == 0

    @pl.when(first)
    def _():
        kprev_ref[...] = jnp.zeros_like(kprev_ref)
        vprev_ref[...] = jnp.zeros_like(vprev_ref)
        ya_ref[...] = jnp.zeros_like(ya_ref)

    def rows(ref, dec_ref):
        tile = jnp.concatenate([ref[s] for s in seqs], axis=0)
        dec = dec_ref[...]
        return jnp.where(first, jnp.concatenate([dec] * (tile.shape[0] // dec.shape[0]), axis=0), tile)

    a_out, f_out = {}, {}
    _interleave(
        _ffn_stages(rows(x_ref, xd_ref), rows(yr_ref, yrd_ref), rows(ya_ref, yad_ref),
                    wo_ref, nw_ref, wu_ref, wd_ref, f_out),
        _attn_stages(q_ref, kv_ref, taba_ref, tabb_ref, qw_ref, kw_ref, sink_ref, ones_ref, bias_ref,
                     jnp.minimum(i, 1), kprev_ref, vprev_ref, a_out))
    for s in seqs:
        o_ref[s] = f_out["y"][s * tq:(s + 1) * tq]
        ya_ref[s] = a_out["ya"][s]

    @pl.when(first)
    def _():
        od_ref[...] = f_out["y"][:od_ref.shape[0]]

    @pl.when(i == pl.num_programs(0) - 2)
    def _():
        for s in seqs:
            kwin_ref[s] = jnp.transpose(a_out["k_cur"][s])
            vwin_ref[s] = jnp.transpose(a_out["v_cur"][s])


def _band_bias():
    ki = np.arange(2 * WINDOW)[:, None]
    qi = (np.arange(GQA_GROUP * WINDOW) % WINDOW + WINDOW)[None, :]
    dq = qi - ki
    band = (dq >= 0) & (dq < WINDOW)
    first = band & (ki >= WINDOW)
    return jnp.asarray(np.where(np.stack([first, band]), 0.0, NEG_INF), F32)


def _rope_block_tables(nb):
    freq = _rope_lane_freq()
    ang_a = (jnp.arange(nb, dtype=F32) * WINDOW)[:, None] * freq
    ang_b = jnp.arange(WINDOW, dtype=F32)[:, None] * freq
    tab_a = jnp.concatenate([jnp.cos(ang_a), jnp.sin(ang_a)], axis=1)
    tab_b = jnp.concatenate([jnp.cos(ang_b), jnp.sin(ang_b)], axis=1)
    return jnp.broadcast_to(tab_a[:, None, :], (nb, 8, 4 * HEAD_DIM)), tab_b


ATTN_BLOCKS_PER_STEP = 1


def _attn_ffn(q3d, kv3d, tab_a, tab_b, qw, kw, sinks, ones_bd, bias, x3d, yr3d, xd, yrd, yad, wo, nw, wu, wd):
    bsz, t, _ = q3d.shape
    nd = xd.shape[0]
    nblk = ATTN_BLOCKS_PER_STEP
    tq = nblk * WINDOW
    nt = t // tq
    const = lambda shape: pl.BlockSpec(shape, lambda i: (0,) * len(shape))
    single = lambda shape: pl.BlockSpec(shape, lambda i: (0,) * len(shape), pipeline_mode=pl.Buffered(1))
    cur = lambda i: jnp.minimum(i, nt - 1)
    prv = lambda i: jnp.maximum(i - 1, 0)
    return pl.pallas_call(
        _attn_ffn_kernel,
        grid=(nt + 1,),
        in_specs=[
            pl.BlockSpec((bsz, tq, D_ATTN), lambda i: (0, cur(i), 0)),
            pl.BlockSpec((bsz, tq, 2 * D_KV), lambda i: (0, cur(i), 0)),
            pl.BlockSpec((nblk, 8, 4 * HEAD_DIM), lambda i: (cur(i), 0, 0)),
            const((WINDOW, 4 * HEAD_DIM)),
            const((1, D_ATTN)),
            const((1, D_KV)),
            const((1, N_Q_HEADS)),
            const((QUAD, QUAD)),
            const((2, 2 * WINDOW, GQA_GROUP * WINDOW)),
            pl.BlockSpec((bsz, tq, D_MODEL), lambda i: (0, prv(i), 0)),
            pl.BlockSpec((bsz, tq, D_RWKV), lambda i: (0, prv(i), 0)),
            const((nd, D_MODEL)),
            const((nd, D_RWKV)),
            const((nd, D_ATTN)),
            single((D_MODEL, D_MODEL)),
            const((1, D_MODEL)),
            single((D_MODEL, D_FF)),
            single((D_FF, D_MODEL)),
        ],
        out_specs=[
            pl.BlockSpec((bsz, tq, D_MODEL), lambda i: (0, prv(i), 0)),
            const((nd, D_MODEL)),
            const((bsz, WINDOW, D_KV)),
            const((bsz, WINDOW, D_KV)),
        ],
        out_shape=[
            jax.ShapeDtypeStruct((bsz, t, D_MODEL), F32),
            jax.ShapeDtypeStruct((nd, D_MODEL), F32),
            jax.ShapeDtypeStruct((bsz, WINDOW, D_KV), F32),
            jax.ShapeDtypeStruct((bsz, WINDOW, D_KV), F32),
        ],
        scratch_shapes=[
            pltpu.VMEM((bsz, WINDOW, D_KV), F32),
            pltpu.VMEM((bsz, WINDOW, D_KV), F32),
            pltpu.VMEM((bsz, tq, D_ATTN), F32),
        ],
        compiler_params=pltpu.CompilerParams(
            dimension_semantics=("arbitrary",), vmem_limit_bytes=VMEM_LIMIT),
        name="attn_ffn",
    )(q3d, kv3d, tab_a, tab_b, qw, kw, sinks, ones_bd, bias, x3d, yr3d, xd, yrd, yad, wo, nw, wu, wd)


DEC_TILE = 16


def _decode_prep_kernel(p_ref, sh_ref, q_ref, kv_ref, mu_ref, prm_ref, wl_ref, ones_ref, tab_ref,
                        qw_ref, kw_ref, vec_ref, vgb_ref, qn_ref, kvn_ref):
    p = p_ref[...]
    xs = p + (sh_ref[...] - p) * mu_ref[...]
    r, logw, k, v, a, b, g, bonus = _rwkv_features(xs, prm_ref[...], wl_ref, ones_ref)
    for i, x in enumerate((a, b, k, jnp.exp(logw), r, v)):
        vec_ref[i] = jnp.transpose(x)
    vgb_ref[0] = g
    vgb_ref[1] = bonus
    n = p.shape[0]
    tab = jnp.broadcast_to(tab_ref[0:1, :], (n, 4 * HEAD_DIM))
    cos_t, sin_lo, sin_hi = _rope_tables(tab[:, :128], tab[:, 128:])
    qn_ref[...] = _qk_norm_rope(q_ref[...], qw_ref[...], _tile_lanes(cos_t, 4), _tile_lanes(sin_lo, 4),
                                _tile_lanes(sin_hi, 4), ones_ref)
    kv = kv_ref[...]
    kvn_ref[:, 0:D_KV] = _qk_norm_rope(kv[:, 0:D_KV], kw_ref[...], cos_t, sin_lo, sin_hi, ones_ref)
    kvn_ref[:, D_KV:] = kv[:, D_KV:]


def _decode_prep(p, shift, q, kv, mu_pad, prm, wl, bmask, tab, qw, kw):
    n = p.shape[0]
    full = lambda shape: pl.BlockSpec(shape, lambda i: (0,) * len(shape))
    return pl.pallas_call(
        _decode_prep_kernel,
        grid=(1,),
        in_specs=[full((n, D_SHIFT_PAD)), full((n, D_SHIFT_PAD)), full((n, D_ATTN)), full((n, 2 * D_KV)),
                  full((1, D_SHIFT_PAD)), full((16, D_RWKV)), full((D_LORA_PAD, 3 * D_RWKV)),
                  full((QUAD, QUAD)), full((8, 4 * HEAD_DIM)), full((1, D_ATTN)), full((1, D_KV))],
        out_specs=[full((6, D_RWKV, n)), full((2, n, D_RWKV)), full((n, D_ATTN)), full((n, 2 * D_KV))],
        out_shape=[
            jax.ShapeDtypeStruct((6, D_RWKV, n), F32),
            jax.ShapeDtypeStruct((2, n, D_RWKV), F32),
            jax.ShapeDtypeStruct((n, D_ATTN), F32),
            jax.ShapeDtypeStruct((n, 2 * D_KV), F32),
        ],
        compiler_params=pltpu.CompilerParams(
            dimension_semantics=("arbitrary",), vmem_limit_bytes=VMEM_LIMIT),
        name="decode_prep",
    )(p, shift, q, kv, mu_pad, prm, wl, bmask, tab, qw, kw)


def _decode_state_kernel(vec_ref, gb_ref, prm_ref, ones_ref, s_ref, sout_ref, yr_ref, yt_ref):
    h = pl.program_id(0)
    a_t, b_t, k_t, w_t, r_t = (vec_ref[i] for i in range(5))

    def body(i, carry):
        s = s_ref[0, i]
        sa = jnp.sum(s * a_t, axis=0, keepdims=True)
        v_i = vec_ref[5, pl.ds(i, 1), :]
        s_new = s * w_t + sa * b_t + v_i * k_t
        sout_ref[0, i] = s_new
        yt_ref[pl.ds(h * HEAD_DIM + i, 1), :] = jnp.sum(s_new * r_t, axis=0, keepdims=True)
        return carry

    lax.fori_loop(0, HEAD_DIM, body, 0, unroll=8)

    @pl.when(h == pl.num_programs(0) - 1)
    def _():
        y = jnp.transpose(yt_ref[...])
        yr_ref[...] = _rwkv_finish(y, gb_ref[0], gb_ref[1], prm_ref[...], ones_ref)


def _decode_state(vec_t, gb, prm, bmask, s_t):
    n = s_t.shape[-1]
    const = lambda shape: pl.BlockSpec(shape, lambda h: (0,) * len(shape))
    return pl.pallas_call(
        _decode_state_kernel,
        grid=(H_RWKV,),
        in_specs=[
            pl.BlockSpec((6, HEAD_DIM, n), lambda h: (0, h, 0)),
            const((2, n, D_RWKV)),
            const((16, D_RWKV)),
            const((QUAD, QUAD)),
            pl.BlockSpec((1, HEAD_DIM, HEAD_DIM, n), lambda h: (h, 0, 0, 0)),
        ],
        out_specs=[
            pl.BlockSpec((1, HEAD_DIM, HEAD_DIM, n), lambda h: (h, 0, 0, 0)),
            const((n, D_RWKV)),
        ],
        out_shape=[
            jax.ShapeDtypeStruct((H_RWKV, HEAD_DIM, HEAD_DIM, n), F32),
            jax.ShapeDtypeStruct((n, D_RWKV), F32),
        ],
        scratch_shapes=[pltpu.VMEM((D_RWKV, n), F32)],
        compiler_params=pltpu.CompilerParams(
            dimension_semantics=("arbitrary",), vmem_limit_bytes=VMEM_LIMIT),
        name="decode_state",
    )(vec_t, gb, prm, bmask, s_t)


def _decode_attn_kernel(qr_ref, kvn_ref, col_ref, ck_ref, cv_ref, sink_ref, ya_ref, kout_ref, vout_ref):
    nh = N_Q_HEADS
    seqs = range(DEC_TILE)
    hrow = lax.broadcasted_iota(jnp.int32, (nh, D_ATTN), 0)
    hlane = lax.broadcasted_iota(jnp.int32, (nh, D_ATTN), 1) // HEAD_DIM
    dmask = hrow == hlane
    grow = lax.broadcasted_iota(jnp.int32, (nh, D_KV), 0) // GQA_GROUP
    glane = lax.broadcasted_iota(jnp.int32, (nh, D_KV), 1) // HEAD_DIM
    gmask = grow == glane
    low = glane == 0
    key_idx = lax.broadcasted_iota(jnp.int32, (nh, WINDOW), 1)
    last = lax.broadcasted_iota(jnp.int32, (D_KV, WINDOW), 1) == WINDOW - 1
    sink = sink_ref[...]
    kvn = kvn_ref[...]
    col = col_ref[0]
    k_new = [kvn[j:j + 1, 0:D_KV] for j in seqs]
    v_new = [kvn[j:j + 1, D_KV:] for j in seqs]
    ck = [ck_ref[j] for j in seqs]
    cv = [cv_ref[j] for j in seqs]
    for j in seqs:
        kout_ref[j] = jnp.where(last, col[0:D_KV, j:j + 1], pltpu.roll(ck[j], WINDOW - 1, 1))
        vout_ref[j] = jnp.where(last, col[D_KV:, j:j + 1], pltpu.roll(cv[j], WINDOW - 1, 1))
    q8 = [qr_ref[j * nh:(j + 1) * nh, :] for j in seqs]
    qp = [jnp.where(gmask, jnp.concatenate([q8[j], q8[j]], axis=1), 0.0) for j in seqs]
    s_c = [jnp.where(key_idx >= 1, jnp.dot(qp[j], ck[j], preferred_element_type=F32) * ATTN_SCALE, NEG_INF)
           for j in seqs]
    s_n = [jnp.sum(qp[j] * k_new[j], axis=-1, keepdims=True) * ATTN_SCALE for j in seqs]
    m = [jnp.maximum(jnp.maximum(jnp.max(s_c[j], axis=-1, keepdims=True), s_n[j]), sink) for j in seqs]
    e_c = [jnp.exp(s_c[j] - m[j]) for j in seqs]
    e_n = [jnp.exp(s_n[j] - m[j]) for j in seqs]
    denom = [jnp.sum(e_c[j], axis=-1, keepdims=True) + e_n[j] + jnp.exp(sink - m[j]) for j in seqs]
    o = [(_dot_nt_f32(e_c[j], cv[j]) + e_n[j] * v_new[j]) / denom[j]
         for j in seqs]
    out_rows = []
    for j in seqs:
        rot = pltpu.roll(o[j], HEAD_DIM, 1)
        g0 = jnp.where(low, o[j], rot)
        g1 = jnp.where(low, rot, o[j])
        wide = jnp.concatenate([g0, g0, g1, g1], axis=1)
        out_rows.append(jnp.sum(jnp.where(dmask, wide, 0.0), axis=0, keepdims=True))
    ya_ref[...] = jnp.concatenate(out_rows, axis=0)


def _decode_attn(q_r, kvn, cols, ck_t, cv_t, sinks_col):
    n = kvn.shape[0]
    bt = DEC_TILE
    const = lambda shape: pl.BlockSpec(shape, lambda i: (0,) * len(shape))
    return pl.pallas_call(
        _decode_attn_kernel,
        grid=(n // bt,),
        in_specs=[
            pl.BlockSpec((bt * N_Q_HEADS, HEAD_DIM), lambda i: (i, 0)),
            pl.BlockSpec((bt, 2 * D_KV), lambda i: (i, 0)),
            pl.BlockSpec((1, 2 * D_KV, bt), lambda i: (i, 0, 0)),
            pl.BlockSpec((bt, D_KV, WINDOW), lambda i: (i, 0, 0)),
            pl.BlockSpec((bt, D_KV, WINDOW), lambda i: (i, 0, 0)),
            const((N_Q_HEADS, 1)),
        ],
        out_specs=[
            pl.BlockSpec((bt, D_ATTN), lambda i: (i, 0)),
            pl.BlockSpec((bt, D_KV, WINDOW), lambda i: (i, 0, 0)),
            pl.BlockSpec((bt, D_KV, WINDOW), lambda i: (i, 0, 0)),
        ],
        out_shape=[
            jax.ShapeDtypeStruct((n, D_ATTN), F32),
            jax.ShapeDtypeStruct((n, D_KV, WINDOW), F32),
            jax.ShapeDtypeStruct((n, D_KV, WINDOW), F32),
        ],
        compiler_params=pltpu.CompilerParams(
            dimension_semantics=("arbitrary",), vmem_limit_bytes=VMEM_LIMIT),
        name="decode_attn",
    )(q_r, kvn, cols, ck_t, cv_t, sinks_col)


def _pad_cols(w, at, n):
    return jnp.concatenate([w[..., :at], jnp.zeros(w.shape[:-1] + (n,), w.dtype), w[..., at:]], axis=-1)


def kernel(x_prompt, x_sample, state_wkv, state_shift, cache_k_win, cache_v_win, norm_mix_w, w_in, mu_shift, w0, w_decay_up, a0, w_a_up, w_g_up, k_k, k_a, r_k, ln_x_w, ln_x_b, q_norm_w, k_norm_w, sinks, w_out, norm_ffn_w, w_ffn_up, w_ffn_down):
    bsz, t, _ = x_prompt.shape
    nd = x_sample.shape[0]
    l = 0
    pad = D_LORA_PAD - D_LORA

    w_in_t = jnp.swapaxes(w_in[l], 0, 1)
    w_in_pad = jnp.concatenate([w_in_t[:D_SHIFT], jnp.zeros((pad, D_MODEL), F32), w_in_t[D_SHIFT:]],
                               axis=0).astype(BF16)
    mu_pad = _pad_cols(mu_shift[l][None, :], D_SHIFT, pad)
    wl = jnp.zeros((D_LORA_PAD, 3 * D_RWKV), F32)
    wl = wl.at[0:32, 0:D_RWKV].set(w_decay_up[l])
    wl = wl.at[32:64, D_RWKV:2 * D_RWKV].set(w_a_up[l])
    wl = wl.at[64:160, 2 * D_RWKV:].set(w_g_up[l])
    wl = wl.astype(BF16)
    prm = jnp.zeros((16, D_RWKV), F32)
    prm = prm.at[0].set(w0[l]).at[1].set(a0[l]).at[2].set(k_k[l]).at[3].set(k_a[l])
    prm = prm.at[4].set(r_k[l].reshape(-1)).at[5].set(ln_x_w[l]).at[6].set(ln_x_b[l])
    hid = np.arange(QUAD) // HEAD_DIM
    bmask = jnp.asarray(hid[:, None] == hid[None, :], BF16)
    tri = jnp.asarray(np.tril(np.ones((CHUNK, CHUNK))), BF16)
    qw = jnp.tile(q_norm_w[l][None, :], (1, N_Q_HEADS))
    kw = jnp.tile(k_norm_w[l][None, :], (1, N_KV_HEADS))
    nmw = norm_mix_w[l][None, :]
    nfw = norm_ffn_w[l][None, :]
    wo = w_out[l].astype(BF16)
    wu = w_ffn_up[l].astype(BF16)
    wd = w_ffn_down[l].astype(BF16)
    tab_a, tab_b = _rope_block_tables(max(t, PAST_LEN + 1) // WINDOW + 1)
    ta, tb = tab_a[PAST_LEN // WINDOW], tab_b[PAST_LEN % WINDOW][None, :]
    tab_s = jnp.concatenate([ta[:, :128] * tb[:, :128] - ta[:, 128:] * tb[:, 128:],
                             ta[:, 128:] * tb[:, :128] + ta[:, :128] * tb[:, 128:]], axis=1)

    xs = x_sample.reshape(nd, D_MODEL)
    p_s, q_s, kv_s = _inproj(xs, nmw, w_in_pad, 128)
    shift_in = _pad_cols(state_shift[l].reshape(nd, D_SHIFT), D_SHIFT, pad)
    vec_t, gb, qn_s, kvn_s = _decode_prep(p_s, shift_in, q_s, kv_s, mu_pad, prm, wl, bmask, tab_s, qw, kw)
    s_t = jnp.transpose(state_wkv[l], (1, 2, 3, 0))
    ck_t = jnp.swapaxes(cache_k_win[l].reshape(nd, WINDOW, D_KV), 1, 2)
    cv_t = jnp.swapaxes(cache_v_win[l].reshape(nd, WINDOW, D_KV), 1, 2)
    q_r = qn_s.reshape(nd * N_Q_HEADS, HEAD_DIM)
    cols = jnp.swapaxes(kvn_s.reshape(nd // DEC_TILE, DEC_TILE, 2 * D_KV), 1, 2)
    wkv_t, yr_s = _decode_state(vec_t, gb, prm, bmask, s_t)
    ya_s, kc_t, vc_t = _decode_attn(q_r, kvn_s, cols, ck_t, cv_t, sinks[l][:, None])

    xp = x_prompt.reshape(bsz * t, D_MODEL)
    xs_p, plast, q_p, kv_p = _inproj_shift(xp, nmw, w_in_pad, mu_pad, t, 1024)
    yr_p, hbd = _rwkv_prompt(xs_p.reshape(bsz, t, D_SHIFT_PAD), prm, wl, tri, bmask)
    y_prompt, y_s, kwin_p, vwin_p = _attn_ffn(q_p.reshape(bsz, t, D_ATTN), kv_p.reshape(bsz, t, 2 * D_KV),
                                              tab_a, tab_b, qw, kw, sinks[l][None, :], bmask, _band_bias(),
                                              x_prompt, yr_p, xs, yr_s, ya_s, wo, nfw, wu, wd)
    hb = hbd.reshape(bsz, 2, 4, HEAD_DIM, 2, HEAD_DIM)
    wkv_prompt = jnp.stack([hb[:, :, j, :, j % 2, :] for j in range(4)], axis=2)
    wkv_prompt = wkv_prompt.reshape(bsz, H_RWKV, HEAD_DIM, HEAD_DIM)[None]
    shift_prompt = plast[:, 0:1, :D_SHIFT][None]
    k_win_prompt = jnp.swapaxes(kwin_p, 1, 2).reshape(bsz, WINDOW, N_KV_HEADS, HEAD_DIM)[None]
    v_win_prompt = jnp.swapaxes(vwin_p, 1, 2).reshape(bsz, WINDOW, N_KV_HEADS, HEAD_DIM)[None]

    y_sample = y_s.reshape(nd, 1, D_MODEL)
    wkv_sample = jnp.transpose(wkv_t, (3, 0, 1, 2))[None]
    shift_sample = p_s[:, :D_SHIFT].reshape(nd, 1, D_SHIFT)[None]
    k_win_sample = jnp.swapaxes(kc_t, 1, 2).reshape(nd, WINDOW, N_KV_HEADS, HEAD_DIM)[None]
    v_win_sample = jnp.swapaxes(vc_t, 1, 2).reshape(nd, WINDOW, N_KV_HEADS, HEAD_DIM)[None]

    return (y_prompt, y_sample, wkv_prompt, shift_prompt, k_win_prompt, v_win_prompt,
            wkv_sample, shift_sample, k_win_sample, v_win_sample)
```

```python
import functools

import jax
import jax.numpy as jnp
import numpy as np
from jax import lax
from jax.experimental import pallas as pl
from jax.experimental.pallas import tpu as pltpu

F32 = jnp.float32
BF16 = jnp.bfloat16

D_MODEL = 1024
D_RWKV = 512
D_ATTN = 512
HEAD_DIM = 64
H_RWKV = 8
N_Q_HEADS = 8
N_KV_HEADS = 2
GQA_GROUP = 4
D_KV = 128
D_LORA = 160
D_LORA_PAD = 256
D_SHIFT = 3 * D_RWKV + D_LORA
D_SHIFT_PAD = 3 * D_RWKV + D_LORA_PAD
D_IN_PAD = D_SHIFT_PAD + D_ATTN + 2 * D_KV
WINDOW = 128
ROPE_DIM = 16
ROPE_HALF = 8
ROPE_THETA = 500000.0
ATTN_SCALE = HEAD_DIM ** -0.5
D_FF = 4096
RMS_EPS = 1e-6
LNX_EPS = 64e-5
NEG_INF = -1e30
LOG2E = 1.4426950408889634
PAST_LEN = 16384

CHUNK = 64
QUAD = 4 * HEAD_DIM
V7X_VMEM_BYTES = 64 * 1024 * 1024
VMEM_LIMIT = V7X_VMEM_BYTES // 8 * 7


def _split2(x):
    hi = x.astype(BF16)
    lo = (x - hi.astype(F32)).astype(BF16)
    return hi, lo


def _head_sums(xs, ones_ref):
    n, w = xs[0].shape
    tile = min(w, QUAD)
    per = w // tile
    pieces = [x[:, j * tile:(j + 1) * tile] for x in xs for j in range(per)]
    stacked = jnp.concatenate(pieces, axis=0) if len(pieces) > 1 else pieces[0]
    ones = ones_ref[0:tile, 0:tile]
    out = jnp.dot(stacked.astype(BF16), ones, preferred_element_type=F32)
    res = []
    for i in range(len(xs)):
        cols = [out[(i * per + j) * n:(i * per + j + 1) * n] for j in range(per)]
        res.append(jnp.concatenate(cols, axis=1) if per > 1 else cols[0])
    return res


def _cumsum_rows(tri_bf16, x):
    hi, lo = _split2(x)
    return (jnp.dot(tri_bf16, hi, preferred_element_type=F32)
            + jnp.dot(tri_bf16, lo, preferred_element_type=F32))


def _mm(a, b):
    return jnp.dot(a.astype(BF16), b.astype(BF16), preferred_element_type=F32)


def _mm_nt(a, b):
    return lax.dot_general(a.astype(BF16), b.astype(BF16), (((1,), (1,)), ((), ())),
                           preferred_element_type=F32)


def _mm_tn(a, b):
    return lax.dot_general(a.astype(BF16), b.astype(BF16), (((0,), (0,)), ((), ())),
                           preferred_element_type=F32)


def _dot_nt_f32(a, b):
    return lax.dot_general(a, b, (((1,), (1,)), ((), ())), preferred_element_type=F32)


def _sigmoid(x):
    return 1.0 / (1.0 + jnp.exp(-x))


def _interleave(*gens):
    live = list(gens)
    while live:
        for g in list(live):
            try:
                next(g)
            except StopIteration:
                live.remove(g)


def _norm_project(x, nw_ref, wt_ref):
    ms = jnp.mean(x * x, axis=-1, keepdims=True)
    h = (x * lax.rsqrt(ms + RMS_EPS)) * nw_ref[...]
    return _mm_nt(h, wt_ref[...])


def _inproj_kernel(x_ref, nw_ref, w_ref, p_ref, q_ref, kv_ref):
    out = _norm_project(x_ref[...], nw_ref, w_ref)
    p_ref[...] = out[:, :D_SHIFT_PAD]
    q_ref[...] = out[:, D_SHIFT_PAD:D_SHIFT_PAD + D_ATTN]
    kv_ref[...] = out[:, D_SHIFT_PAD + D_ATTN:]


def _inproj_shift_kernel(tiles_per_seq, x_ref, nw_ref, w_ref, mu_ref, xs_ref, last_ref, q_ref, kv_ref, prev_ref):
    i = pl.program_id(0)

    @pl.when(i % tiles_per_seq == 0)
    def _():
        prev_ref[...] = jnp.zeros_like(prev_ref)

    out = _norm_project(x_ref[...], nw_ref, w_ref)
    p = out[:, :D_SHIFT_PAD]
    tm = p.shape[0]
    row = lax.broadcasted_iota(jnp.int32, p.shape, 0)
    prev = jnp.where(row == 0, jnp.broadcast_to(prev_ref[0:1, :], p.shape), pltpu.roll(p, 1, 0))
    xs_ref[...] = p + (prev - p) * mu_ref[...]
    last = jnp.broadcast_to(p[tm - 1:tm, :], prev_ref.shape)
    prev_ref[...] = last
    last_ref[0] = last
    q_ref[...] = out[:, D_SHIFT_PAD:D_SHIFT_PAD + D_ATTN]
    kv_ref[...] = out[:, D_SHIFT_PAD + D_ATTN:]


def _inproj_shift(x2d, norm_w, w_in_pad, mu_pad, seq_len, tm):
    m = x2d.shape[0]
    tiles_per_seq = seq_len // tm
    return pl.pallas_call(
        functools.partial(_inproj_shift_kernel, tiles_per_seq),
        grid=(m // tm,),
        in_specs=[
            pl.BlockSpec((tm, D_MODEL), lambda i: (i, 0)),
            pl.BlockSpec((1, D_MODEL), lambda i: (0, 0)),
            pl.BlockSpec((D_IN_PAD, D_MODEL), lambda i: (0, 0)),
            pl.BlockSpec((1, D_SHIFT_PAD), lambda i: (0, 0)),
        ],
        out_specs=[
            pl.BlockSpec((tm, D_SHIFT_PAD), lambda i: (i, 0)),
            pl.BlockSpec((1, 8, D_SHIFT_PAD), lambda i: (i // tiles_per_seq, 0, 0)),
            pl.BlockSpec((tm, D_ATTN), lambda i: (i, 0)),
            pl.BlockSpec((tm, 2 * D_KV), lambda i: (i, 0)),
        ],
        out_shape=[
            jax.ShapeDtypeStruct((m, D_SHIFT_PAD), F32),
            jax.ShapeDtypeStruct((m // seq_len, 8, D_SHIFT_PAD), F32),
            jax.ShapeDtypeStruct((m, D_ATTN), F32),
            jax.ShapeDtypeStruct((m, 2 * D_KV), F32),
        ],
        scratch_shapes=[pltpu.VMEM((8, D_SHIFT_PAD), F32)],
        compiler_params=pltpu.CompilerParams(
            dimension_semantics=("arbitrary",), vmem_limit_bytes=VMEM_LIMIT),
        name="inproj_shift",
    )(x2d, norm_w, w_in_pad, mu_pad)


def _inproj(x2d, norm_w, w_in_pad, tm):
    m = x2d.shape[0]
    return pl.pallas_call(
        _inproj_kernel,
        grid=(m // tm,),
        in_specs=[
            pl.BlockSpec((tm, D_MODEL), lambda i: (i, 0)),
            pl.BlockSpec((1, D_MODEL), lambda i: (0, 0)),
            pl.BlockSpec((D_IN_PAD, D_MODEL), lambda i: (0, 0)),
        ],
        out_specs=[
            pl.BlockSpec((tm, D_SHIFT_PAD), lambda i: (i, 0)),
            pl.BlockSpec((tm, D_ATTN), lambda i: (i, 0)),
            pl.BlockSpec((tm, 2 * D_KV), lambda i: (i, 0)),
        ],
        out_shape=[
            jax.ShapeDtypeStruct((m, D_SHIFT_PAD), F32),
            jax.ShapeDtypeStruct((m, D_ATTN), F32),
            jax.ShapeDtypeStruct((m, 2 * D_KV), F32),
        ],
        compiler_params=pltpu.CompilerParams(
            dimension_semantics=("arbitrary",), vmem_limit_bytes=VMEM_LIMIT),
        name="inproj",
    )(x2d, norm_w, w_in_pad)


def _rwkv_features(xs, prm, wl_ref, ones_ref):
    r = xs[:, 0:D_RWKV]
    k = xs[:, D_RWKV:2 * D_RWKV]
    v = xs[:, 2 * D_RWKV:3 * D_RWKV]
    lora = xs[:, 3 * D_RWKV:]
    col = lax.broadcasted_iota(jnp.int32, lora.shape, 1)
    act = jnp.where(col < 32, jnp.tanh(lora), jnp.where(col < 64, lora, _sigmoid(lora)))
    up = jnp.dot(act.astype(BF16), wl_ref[...], preferred_element_type=F32)
    w0, a0, k_k, k_a, r_k = prm[0:1], prm[1:2], prm[2:3], prm[3:4], prm[4:5]
    logw = (-np.exp(-0.5)) * _sigmoid(w0 + up[:, 0:D_RWKV])
    asig = _sigmoid(a0 + up[:, D_RWKV:2 * D_RWKV])
    g = up[:, 2 * D_RWKV:]
    kk = k * k_k
    k_mod = k * (1.0 + (asig - 1.0) * k_a)
    ss, rk = _head_sums([kk * kk, r * k_mod * r_k], ones_ref)
    kk = kk / jnp.maximum(jnp.sqrt(ss), 1e-12)
    k = k_mod
    bonus = rk * v
    return r, logw, k, v, -kk, kk * asig, g, bonus


def _rwkv_finish(y, g, bonus, prm, ones_ref):
    ln_w, ln_b = prm[5:6], prm[6:7]
    mean = _head_sums([y], ones_ref)[0] * (1.0 / HEAD_DIM)
    d = y - mean
    var = _head_sums([d * d], ones_ref)[0] * (1.0 / HEAD_DIM)
    yn = d * lax.rsqrt(var + LNX_EPS) * ln_w + ln_b
    return (yn + bonus) * g


def _block_diag(x, bmask):
    return jnp.concatenate([x] * 4, axis=0) * bmask


def _chunk_prep(r, logw, k, v, a, b, tri_ref, bmask):
    n = len(r)
    ch = range(n)
    tri = tri_ref[...]
    cum = [_cumsum_rows(tri, logw[i]) for i in ch]
    yield
    e_in = [jnp.exp(cum[i]) for i in ch]
    e_ex = [jnp.exp(cum[i] - logw[i]) for i in ch]
    e_inv = [1.0 / e_in[i] for i in ch]
    e_last = [e_in[i][CHUNK - 1:CHUNK, :] for i in ch]
    rt = [(r[i] * e_in[i]).astype(BF16) for i in ch]
    at = [(a[i] * e_ex[i]).astype(BF16) for i in ch]
    kt = [(k[i] * e_inv[i]).astype(BF16) for i in ch]
    bt = [(b[i] * e_inv[i]).astype(BF16) for i in ch]
    vb = [v[i].astype(BF16) for i in ch]
    yield

    t_idx = lax.broadcasted_iota(jnp.int32, (CHUNK, QUAD), 0)
    s_idx = lax.broadcasted_iota(jnp.int32, (CHUNK, QUAD), 1) & (HEAD_DIM - 1)
    strict = s_idx < t_idx
    incl = s_idx <= t_idx

    gm = [_mm_nt(jnp.concatenate([at[i], rt[i]], axis=0),
                 jnp.concatenate([_block_diag(bt[i], bmask), _block_diag(kt[i], bmask)], axis=0))
          for i in ch]
    yield
    a_ab = [jnp.where(strict, gm[i][:CHUNK, :QUAD], 0.0) for i in ch]
    a_ak = [jnp.where(strict, gm[i][:CHUNK, QUAD:], 0.0) for i in ch]
    a_rb = [jnp.where(incl, gm[i][CHUNK:, :QUAD], 0.0) for i in ch]
    a_rk = [jnp.where(incl, gm[i][CHUNK:, QUAD:], 0.0) for i in ch]

    eye = jnp.where(s_idx == t_idx, 1.0, 0.0)
    pwb = [a_ab[i].astype(BF16) for i in ch]
    t_inv = [eye + a_ab[i] for i in ch]
    for it in range(6):
        rbd = [_block_diag(pwb[i], bmask) for i in ch]
        if it == 0:
            pwb = [_mm(pwb[i], rbd[i]).astype(BF16) for i in ch]
        elif it < 5:
            out = [_mm(jnp.concatenate([pwb[i], t_inv[i].astype(BF16)], axis=0), rbd[i]) for i in ch]
            pwb = [out[i][:CHUNK].astype(BF16) for i in ch]
            t_inv = [t_inv[i] + out[i][CHUNK:] for i in ch]
        else:
            t_inv = [t_inv[i] + _mm(t_inv[i], rbd[i]) for i in ch]
        yield

    vbd = [_block_diag(vb[i], bmask) for i in ch]
    xy0 = [_mm(jnp.concatenate([a_ak[i], a_rk[i]], axis=0), vbd[i]) for i in ch]
    return [dict(ar=jnp.concatenate([at[i], rt[i]], axis=0), x0=xy0[i][:CHUNK], y0=xy0[i][CHUNK:],
                 t_inv=t_inv[i].astype(BF16), a_rb=a_rb[i].astype(BF16), vb=vb[i],
                 bk=jnp.concatenate([bt[i], kt[i]], axis=0), e_last=e_last[i]) for i in ch]


def _chunk_step(pre, state, bmask, out):
    ch = range(len(pre))
    half = QUAD // 2
    zeros = jnp.zeros((half, half), BF16)
    sc = [state[i].astype(BF16) for i in ch]
    sb = [jnp.concatenate([jnp.concatenate([sc[i][:half], zeros], axis=1),
                           jnp.concatenate([zeros, sc[i][half:]], axis=1)], axis=0) for i in ch]
    xr = [_mm_nt(pre[i]["ar"], sb[i]) for i in ch]
    yield
    x = [xr[i][:CHUNK] + pre[i]["x0"] for i in ch]
    u = [_mm(pre[i]["t_inv"], _block_diag(x[i].astype(BF16), bmask)) for i in ch]
    yield
    ub = [u[i].astype(BF16) for i in ch]
    out["y"] = [xr[i][CHUNK:] + pre[i]["y0"] + _mm(pre[i]["a_rb"], _block_diag(ub[i], bmask)) for i in ch]
    upd = [_mm_tn(jnp.concatenate([ub[i], pre[i]["vb"]], axis=0), pre[i]["bk"]) for i in ch]
    yield
    bm = bmask[:half, :half].astype(F32)
    s_new = []
    for i in ch:
        e_last = pre[i]["e_last"]
        top = (state[i][:half] + upd[i][:half, :half] * bm) * e_last[:, :half]
        bot = (state[i][half:] + upd[i][half:, half:] * bm) * e_last[:, half:]
        s_new.append(jnp.concatenate([top, bot], axis=0))
    out["state"] = s_new


def _finish_stages(ys, g, bonus, prm, ones_ref, y_ref, j):
    nseq = len(ys) // 2
    ln_w, ln_b = prm[5:6], prm[6:7]
    y = jnp.concatenate([jnp.concatenate(ys[2 * s:2 * s + 2], axis=1) for s in range(nseq)], axis=0)
    mean = _head_sums([y], ones_ref)[0] * (1.0 / HEAD_DIM)
    yield
    d = y - mean
    var = _head_sums([d * d], ones_ref)[0] * (1.0 / HEAD_DIM)
    yield
    out = (d * lax.rsqrt(var + LNX_EPS) * ln_w + ln_b + bonus) * g
    for s in range(nseq):
        y_ref[s, j * CHUNK:(j + 1) * CHUNK, :] = out[s * CHUNK:(s + 1) * CHUNK]


def _rwkv_prompt_kernel(xs_ref, prm_ref, wl_ref, tri_ref, bmask_ref, y_ref, hout_ref, h_ref):
    c = pl.program_id(0)
    nseq, tstep, _ = xs_ref.shape
    nchunk = tstep // CHUNK

    @pl.when(c == 0)
    def _():
        h_ref[...] = jnp.zeros_like(h_ref)

    xs = jnp.concatenate([xs_ref[s] for s in range(nseq)], axis=0)
    prm = prm_ref[...]
    r, logw, k, v, a, b, g, bonus = _rwkv_features(xs, prm, wl_ref, bmask_ref)
    bmask = bmask_ref[...]
    lanes = [(s, q) for s in range(nseq) for q in range(2)]
    pre = {}

    def prep(chunks):
        chains = [(j, s, q) for j in chunks for s, q in lanes]
        cut = lambda x: [x[s * tstep + j * CHUNK:s * tstep + (j + 1) * CHUNK, q * QUAD:(q + 1) * QUAD]
                         for j, s, q in chains]
        res = yield from _chunk_prep(cut(r), cut(logw), cut(k), cut(v), cut(a), cut(b), tri_ref, bmask)
        for n, j in enumerate(chunks):
            pre[j] = res[n * len(lanes):(n + 1) * len(lanes)]

    nfirst = max(nchunk - 1, 1)
    _interleave(prep(range(nfirst)))
    later = prep(range(nfirst, nchunk))
    state = [h_ref[s, q] for s, q in lanes]
    crow = lambda x, j: jnp.concatenate([x[s * tstep + j * CHUNK:s * tstep + (j + 1) * CHUNK]
                                         for s in range(nseq)], axis=0)
    finish = iter(())
    for j in range(nchunk):
        res = {}
        for _ in _chunk_step(pre[j], state, bmask, res):
            next(later, None)
            next(finish, None)
        _interleave(finish)
        if j == nfirst - 1:
            _interleave(later)
        state = res["state"]
        finish = _finish_stages(res["y"], crow(g, j), crow(bonus, j), prm, bmask_ref, y_ref, j)
    _interleave(finish)
    for i, (s, q) in enumerate(lanes):
        h_ref[s, q] = state[i]

    @pl.when(c == pl.num_programs(0) - 1)
    def _():
        hout_ref[...] = h_ref[...]


RWKV_CHUNKS_PER_STEP = 4


def _rwkv_prompt(xs3d, prm, wl, tri, bmask):
    bsz, t, _ = xs3d.shape
    tstep = RWKV_CHUNKS_PER_STEP * CHUNK
    nc = t // tstep
    const = lambda shape: pl.BlockSpec(shape, lambda c: (0,) * len(shape))
    state_shape = (bsz, 2, QUAD, QUAD // 2)
    return pl.pallas_call(
        _rwkv_prompt_kernel,
        grid=(nc,),
        in_specs=[
            pl.BlockSpec((bsz, tstep, D_SHIFT_PAD), lambda c: (0, c, 0)),
            const((16, D_RWKV)),
            const((D_LORA_PAD, 3 * D_RWKV)),
            const((CHUNK, CHUNK)),
            const((QUAD, QUAD)),
        ],
        out_specs=[
            pl.BlockSpec((bsz, tstep, D_RWKV), lambda c: (0, c, 0)),
            const(state_shape),
        ],
        out_shape=[
            jax.ShapeDtypeStruct((bsz, t, D_RWKV), F32),
            jax.ShapeDtypeStruct(state_shape, F32),
        ],
        scratch_shapes=[pltpu.VMEM(state_shape, F32)],
        compiler_params=pltpu.CompilerParams(
            dimension_semantics=("arbitrary",), vmem_limit_bytes=VMEM_LIMIT),
        name="rwkv_prompt",
    )(xs3d, prm, wl, tri, bmask)


def _rope_lane_freq():
    inv_freq = jnp.power(ROPE_THETA, -jnp.arange(ROPE_HALF, dtype=F32) * (2.0 / ROPE_DIM))
    return inv_freq[(np.arange(2 * HEAD_DIM) % HEAD_DIM) % ROPE_HALF][None, :]


def _rope_tables(cos, sin):
    dim = lax.broadcasted_iota(jnp.int32, cos.shape, 1) & (HEAD_DIM - 1)
    cos_t = jnp.where(dim < ROPE_DIM, cos, 1.0)
    sin_lo = jnp.where(dim < ROPE_HALF, -sin, 0.0)
    sin_hi = jnp.where((dim >= ROPE_HALF) & (dim < ROPE_DIM), sin, 0.0)
    return cos_t, sin_lo, sin_hi


def _qk_norm_rope(x, norm_w, cos_t, sin_lo, sin_hi, ones_ref):
    ms = _head_sums([x * x], ones_ref)[0] * (1.0 / HEAD_DIM)
    xn = x * lax.rsqrt(ms + RMS_EPS) * norm_w
    width = x.shape[1]
    fwd = pltpu.roll(xn, width - ROPE_HALF, 1)
    bwd = pltpu.roll(xn, ROPE_HALF, 1)
    return xn * cos_t + fwd * sin_lo + bwd * sin_hi


def _tile_lanes(x, reps):
    return jnp.concatenate([x] * reps, axis=1) if reps > 1 else x


def _attn_stages(q_ref, kv_ref, taba_ref, tabb_ref, qw_ref, kw_ref, sink_ref, ones_ref, bias_ref, first_bias,
                 kprev_ref, vprev_ref, out):
    nseq, tq, _ = q_ref.shape
    nblk = tq // WINDOW
    units = [(s, b) for s in range(nseq) for b in range(nblk)]
    blk = lambda b: slice(b * WINDOW, (b + 1) * WINDOW)
    tb = tabb_ref[...]
    cos_b, sin_b = tb[:, :128], tb[:, 128:]
    rope, rope4 = [], []
    for b in range(nblk):
        ta = taba_ref[b][0:1, :]
        cos_a, sin_a = ta[:, :128], ta[:, 128:]
        tabs = _rope_tables(cos_a * cos_b - sin_a * sin_b, sin_a * cos_b + cos_a * sin_b)
        rope.append(tabs)
        rope4.append([_tile_lanes(x, 4) for x in tabs])
    q = {(s, b): _qk_norm_rope(q_ref[s, blk(b), :], qw_ref[...], *rope4[b], ones_ref) * (ATTN_SCALE * LOG2E)
         for s, b in units}
    kv = {(s, b): kv_ref[s, blk(b), :] for s, b in units}
    k_cur = {u: _qk_norm_rope(kv[u][:, 0:D_KV], kw_ref[...], *rope[u[1]], ones_ref) for u in units}
    v_cur = {u: kv[u][:, D_KV:] for u in units}
    k_all = {(s, b): jnp.concatenate([kprev_ref[s] if b == 0 else k_cur[(s, b - 1)], k_cur[(s, b)]], axis=0)
             for s, b in units}
    v_all = {(s, b): jnp.concatenate([vprev_ref[s] if b == 0 else v_cur[(s, b - 1)], v_cur[(s, b)]], axis=0)
             for s, b in units}
    out["k_cur"] = [k_cur[(s, nblk - 1)] for s in range(nseq)]
    out["v_cur"] = [v_cur[(s, nblk - 1)] for s in range(nseq)]
    for s in range(nseq):
        kprev_ref[s] = out["k_cur"][s]
        vprev_ref[s] = out["v_cur"][s]
    yield

    nk = 2 * WINDOW
    bias = [bias_ref[first_bias] if b == 0 else bias_ref[1] for b in range(nblk)]
    sinks = sink_ref[...] * LOG2E
    low = lax.broadcasted_iota(jnp.int32, (nk, D_KV), 1) < HEAD_DIM
    lane_blk = [ones_ref[j * HEAD_DIM:j * HEAD_DIM + 1, :] for j in range(GQA_GROUP)]

    chains = [(u, g) for u in units for g in range(N_KV_HEADS)]
    ch = range(len(chains))
    k_rot = {u: pltpu.roll(k_all[u], HEAD_DIM, 1) for u in units}
    k2 = [jnp.where(low, k_all[u], k_rot[u]) if g == 0 else jnp.where(low, k_rot[u], k_all[u])
          for u, g in chains]
    k4 = [jnp.concatenate([k2[c], k2[c]], axis=1).astype(BF16) for c in ch]
    vb = {u: v_all[u].astype(BF16) for u in units}
    qg = [q[u][:, g * QUAD:(g + 1) * QUAD].astype(BF16) for u, g in chains]
    qstack = [jnp.concatenate([qg[c] * lane_blk[j] for j in range(GQA_GROUP)], axis=0) for c in ch]
    sink_row = [jnp.concatenate(
        [jnp.broadcast_to(sinks[:, g * GQA_GROUP + j:g * GQA_GROUP + j + 1], (1, WINDOW))
         for j in range(GQA_GROUP)], axis=1) for u, g in chains]
    yield
    sc = [_mm_nt(k4[c], qstack[c]) + bias[u[1]] for c, (u, g) in enumerate(chains)]
    yield
    m = [jnp.maximum(jnp.max(sc[c], axis=0, keepdims=True), sink_row[c]) for c in ch]
    e = [jnp.exp2(sc[c] - m[c]) for c in ch]
    yield
    denom = [jnp.sum(e[c], axis=0, keepdims=True) + jnp.exp2(sink_row[c] - m[c]) for c in ch]
    ot = {(u, g): _mm_tn(vb[u], e[c].astype(BF16))[g * HEAD_DIM:(g + 1) * HEAD_DIM, :] * (1.0 / denom[c])
          for c, (u, g) in enumerate(chains)}
    yield
    ya = []
    for s in range(nseq):
        blocks = []
        for b in range(nblk):
            yt = jnp.concatenate([ot[((s, b), g)][:, j * WINDOW:(j + 1) * WINDOW]
                                  for g in range(N_KV_HEADS) for j in range(GQA_GROUP)], axis=0)
            blocks.append(jnp.transpose(yt))
        ya.append(jnp.concatenate(blocks, axis=0) if nblk > 1 else blocks[0])
    out["ya"] = ya


def _ffn_stages(x, yr, ya, wo_ref, nw_ref, wu_ref, wd_ref, out, pieces=4):
    mix = jnp.concatenate([yr, ya], axis=1).astype(BF16)
    x1 = x + jnp.dot(mix, wo_ref[...], preferred_element_type=F32)
    yield
    ms = jnp.mean(x1 * x1, axis=-1, keepdims=True)
    hf = ((x1 * lax.rsqrt(ms + RMS_EPS)) * nw_ref[...]).astype(BF16)
    acc = x1
    step = D_FF // pieces
    for j in range(pieces):
        up = jnp.dot(hf, wu_ref[:, j * step:(j + 1) * step], preferred_element_type=F32)
        yield
        act = jnp.square(jnp.maximum(up, 0.0)).astype(BF16)
        acc = acc + jnp.dot(act, wd_ref[j * step:(j + 1) * step, :], preferred_element_type=F32)
        yield
    out["y"] = acc


def _attn_ffn_kernel(q_ref, kv_ref, taba_ref, tabb_ref, qw_ref, kw_ref, sink_ref, ones_ref, bias_ref,
                     x_ref, yr_ref, xd_ref, yrd_ref, yad_ref, wo_ref, nw_ref, wu_ref, wd_ref,
                     o_ref, od_ref, kwin_ref, vwin_ref, kprev_ref, vprev_ref, ya_ref):
    i = pl.program_id(0)
    nseq, tq, _ = q_ref.shape
    seqs = range(nseq)
    first = i == 0

    @pl.when(first)
    def _():
        kprev_ref[...] = jnp.zeros_like(kprev_ref)
        vprev_ref[...] = jnp.zeros_like(vprev_ref)
        ya_ref[...] = jnp.zeros_like(ya_ref)

    def rows(ref, dec_ref):
        tile = jnp.concatenate([ref[s] for s in seqs], axis=0)
        dec = dec_ref[...]
        return jnp.where(first, jnp.concatenate([dec] * (tile.shape[0] // dec.shape[0]), axis=0), tile)

    a_out, f_out = {}, {}
    _interleave(
        _ffn_stages(rows(x_ref, xd_ref), rows(yr_ref, yrd_ref), rows(ya_ref, yad_ref),
                    wo_ref, nw_ref, wu_ref, wd_ref, f_out),
        _attn_stages(q_ref, kv_ref, taba_ref, tabb_ref, qw_ref, kw_ref, sink_ref, ones_ref, bias_ref,
                     jnp.minimum(i, 1), kprev_ref, vprev_ref, a_out))
    for s in seqs:
        o_ref[s] = f_out["y"][s * tq:(s + 1) * tq]
        ya_ref[s] = a_out["ya"][s]

    @pl.when(first)
    def _():
        od_ref[...] = f_out["y"][:od_ref.shape[0]]

    @pl.when(i == pl.num_programs(0) - 2)
    def _():
        for s in seqs:
            kwin_ref[s] = jnp.transpose(a_out["k_cur"][s])
            vwin_ref[s] = jnp.transpose(a_out["v_cur"][s])


def _band_bias():
    ki = np.arange(2 * WINDOW)[:, None]
    qi = (np.arange(GQA_GROUP * WINDOW) % WINDOW + WINDOW)[None, :]
    dq = qi - ki
    band = (dq >= 0) & (dq < WINDOW)
    first = band & (ki >= WINDOW)
    return jnp.asarray(np.where(np.stack([first, band]), 0.0, NEG_INF), F32)


def _rope_block_tables(nb):
    freq = _rope_lane_freq()
    ang_a = (jnp.arange(nb, dtype=F32) * WINDOW)[:, None] * freq
    ang_b = jnp.arange(WINDOW, dtype=F32)[:, None] * freq
    tab_a = jnp.concatenate([jnp.cos(ang_a), jnp.sin(ang_a)], axis=1)
    tab_b = jnp.concatenate([jnp.cos(ang_b), jnp.sin(ang_b)], axis=1)
    return jnp.broadcast_to(tab_a[:, None, :], (nb, 8, 4 * HEAD_DIM)), tab_b


ATTN_BLOCKS_PER_STEP = 1


def _attn_ffn(q3d, kv3d, tab_a, tab_b, qw, kw, sinks, ones_bd, bias, x3d, yr3d, xd, yrd, yad, wo, nw, wu, wd):
    bsz, t, _ = q3d.shape
    nd = xd.shape[0]
    nblk = ATTN_BLOCKS_PER_STEP
    tq = nblk * WINDOW
    nt = t // tq
    const = lambda shape: pl.BlockSpec(shape, lambda i: (0,) * len(shape))
    single = lambda shape: pl.BlockSpec(shape, lambda i: (0,) * len(shape), pipeline_mode=pl.Buffered(1))
    cur = lambda i: jnp.minimum(i, nt - 1)
    prv = lambda i: jnp.maximum(i - 1, 0)
    return pl.pallas_call(
        _attn_ffn_kernel,
        grid=(nt + 1,),
        in_specs=[
            pl.BlockSpec((bsz, tq, D_ATTN), lambda i: (0, cur(i), 0)),
            pl.BlockSpec((bsz, tq, 2 * D_KV), lambda i: (0, cur(i), 0)),
            pl.BlockSpec((nblk, 8, 4 * HEAD_DIM), lambda i: (cur(i), 0, 0)),
            const((WINDOW, 4 * HEAD_DIM)),
            const((1, D_ATTN)),
            const((1, D_KV)),
            const((1, N_Q_HEADS)),
            const((QUAD, QUAD)),
            const((2, 2 * WINDOW, GQA_GROUP * WINDOW)),
            pl.BlockSpec((bsz, tq, D_MODEL), lambda i: (0, prv(i), 0)),
            pl.BlockSpec((bsz, tq, D_RWKV), lambda i: (0, prv(i), 0)),
            const((nd, D_MODEL)),
            const((nd, D_RWKV)),
            const((nd, D_ATTN)),
            single((D_MODEL, D_MODEL)),
            const((1, D_MODEL)),
            single((D_MODEL, D_FF)),
            single((D_FF, D_MODEL)),
        ],
        out_specs=[
            pl.BlockSpec((bsz, tq, D_MODEL), lambda i: (0, prv(i), 0)),
            const((nd, D_MODEL)),
            const((bsz, WINDOW, D_KV)),
            const((bsz, WINDOW, D_KV)),
        ],
        out_shape=[
            jax.ShapeDtypeStruct((bsz, t, D_MODEL), F32),
            jax.ShapeDtypeStruct((nd, D_MODEL), F32),
            jax.ShapeDtypeStruct((bsz, WINDOW, D_KV), F32),
            jax.ShapeDtypeStruct((bsz, WINDOW, D_KV), F32),
        ],
        scratch_shapes=[
            pltpu.VMEM((bsz, WINDOW, D_KV), F32),
            pltpu.VMEM((bsz, WINDOW, D_KV), F32),
            pltpu.VMEM((bsz, tq, D_ATTN), F32),
        ],
        compiler_params=pltpu.CompilerParams(
            dimension_semantics=("arbitrary",), vmem_limit_bytes=VMEM_LIMIT),
        name="attn_ffn",
    )(q3d, kv3d, tab_a, tab_b, qw, kw, sinks, ones_bd, bias, x3d, yr3d, xd, yrd, yad, wo, nw, wu, wd)


DEC_TILE = 16


def _decode_prep_kernel(p_ref, sh_ref, q_ref, kv_ref, mu_ref, prm_ref, wl_ref, ones_ref, tab_ref,
                        qw_ref, kw_ref, vec_ref, vgb_ref, qn_ref, kvn_ref):
    p = p_ref[...]
    xs = p + (sh_ref[...] - p) * mu_ref[...]
    r, logw, k, v, a, b, g, bonus = _rwkv_features(xs, prm_ref[...], wl_ref, ones_ref)
    for i, x in enumerate((a, b, k, jnp.exp(logw), r, v)):
        vec_ref[i] = jnp.transpose(x)
    vgb_ref[0] = g
    vgb_ref[1] = bonus
    n = p.shape[0]
    tab = jnp.broadcast_to(tab_ref[0:1, :], (n, 4 * HEAD_DIM))
    cos_t, sin_lo, sin_hi = _rope_tables(tab[:, :128], tab[:, 128:])
    qn_ref[...] = _qk_norm_rope(q_ref[...], qw_ref[...], _tile_lanes(cos_t, 4), _tile_lanes(sin_lo, 4),
                                _tile_lanes(sin_hi, 4), ones_ref)
    kv = kv_ref[...]
    kvn_ref[:, 0:D_KV] = _qk_norm_rope(kv[:, 0:D_KV], kw_ref[...], cos_t, sin_lo, sin_hi, ones_ref)
    kvn_ref[:, D_KV:] = kv[:, D_KV:]


def _decode_prep(p, shift, q, kv, mu_pad, prm, wl, bmask, tab, qw, kw):
    n = p.shape[0]
    full = lambda shape: pl.BlockSpec(shape, lambda i: (0,) * len(shape))
    return pl.pallas_call(
        _decode_prep_kernel,
        grid=(1,),
        in_specs=[full((n, D_SHIFT_PAD)), full((n, D_SHIFT_PAD)), full((n, D_ATTN)), full((n, 2 * D_KV)),
                  full((1, D_SHIFT_PAD)), full((16, D_RWKV)), full((D_LORA_PAD, 3 * D_RWKV)),
                  full((QUAD, QUAD)), full((8, 4 * HEAD_DIM)), full((1, D_ATTN)), full((1, D_KV))],
        out_specs=[full((6, D_RWKV, n)), full((2, n, D_RWKV)), full((n, D_ATTN)), full((n, 2 * D_KV))],
        out_shape=[
            jax.ShapeDtypeStruct((6, D_RWKV, n), F32),
            jax.ShapeDtypeStruct((2, n, D_RWKV), F32),
            jax.ShapeDtypeStruct((n, D_ATTN), F32),
            jax.ShapeDtypeStruct((n, 2 * D_KV), F32),
        ],
        compiler_params=pltpu.CompilerParams(
            dimension_semantics=("arbitrary",), vmem_limit_bytes=VMEM_LIMIT),
        name="decode_prep",
    )(p, shift, q, kv, mu_pad, prm, wl, bmask, tab, qw, kw)


def _decode_state_kernel(vec_ref, gb_ref, prm_ref, ones_ref, s_ref, sout_ref, yr_ref, yt_ref):
    h = pl.program_id(0)
    a_t, b_t, k_t, w_t, r_t = (vec_ref[i] for i in range(5))

    def body(i, carry):
        s = s_ref[0, i]
        sa = jnp.sum(s * a_t, axis=0, keepdims=True)
        v_i = vec_ref[5, pl.ds(i, 1), :]
        s_new = s * w_t + sa * b_t + v_i * k_t
        sout_ref[0, i] = s_new
        yt_ref[pl.ds(h * HEAD_DIM + i, 1), :] = jnp.sum(s_new * r_t, axis=0, keepdims=True)
        return carry

    lax.fori_loop(0, HEAD_DIM, body, 0, unroll=8)

    @pl.when(h == pl.num_programs(0) - 1)
    def _():
        y = jnp.transpose(yt_ref[...])
        yr_ref[...] = _rwkv_finish(y, gb_ref[0], gb_ref[1], prm_ref[...], ones_ref)


def _decode_state(vec_t, gb, prm, bmask, s_t):
    n = s_t.shape[-1]
    const = lambda shape: pl.BlockSpec(shape, lambda h: (0,) * len(shape))
    return pl.pallas_call(
        _decode_state_kernel,
        grid=(H_RWKV,),
        in_specs=[
            pl.BlockSpec((6, HEAD_DIM, n), lambda h: (0, h, 0)),
            const((2, n, D_RWKV)),
            const((16, D_RWKV)),
            const((QUAD, QUAD)),
            pl.BlockSpec((1, HEAD_DIM, HEAD_DIM, n), lambda h: (h, 0, 0, 0)),
        ],
        out_specs=[
            pl.BlockSpec((1, HEAD_DIM, HEAD_DIM, n), lambda h: (h, 0, 0, 0)),
            const((n, D_RWKV)),
        ],
        out_shape=[
            jax.ShapeDtypeStruct((H_RWKV, HEAD_DIM, HEAD_DIM, n), F32),
            jax.ShapeDtypeStruct((n, D_RWKV), F32),
        ],
        scratch_shapes=[pltpu.VMEM((D_RWKV, n), F32)],
        compiler_params=pltpu.CompilerParams(
            dimension_semantics=("arbitrary",), vmem_limit_bytes=VMEM_LIMIT),
        name="decode_state",
    )(vec_t, gb, prm, bmask, s_t)


def _decode_attn_kernel(qr_ref, kvn_ref, col_ref, ck_ref, cv_ref, sink_ref, ya_ref, kout_ref, vout_ref):
    nh = N_Q_HEADS
    seqs = range(DEC_TILE)
    hrow = lax.broadcasted_iota(jnp.int32, (nh, D_ATTN), 0)
    hlane = lax.broadcasted_iota(jnp.int32, (nh, D_ATTN), 1) // HEAD_DIM
    dmask = hrow == hlane
    grow = lax.broadcasted_iota(jnp.int32, (nh, D_KV), 0) // GQA_GROUP
    glane = lax.broadcasted_iota(jnp.int32, (nh, D_KV), 1) // HEAD_DIM
    gmask = grow == glane
    low = glane == 0
    key_idx = lax.broadcasted_iota(jnp.int32, (nh, WINDOW), 1)
    last = lax.broadcasted_iota(jnp.int32, (D_KV, WINDOW), 1) == WINDOW - 1
    sink = sink_ref[...]
    kvn = kvn_ref[...]
    col = col_ref[0]
    k_new = [kvn[j:j + 1, 0:D_KV] for j in seqs]
    v_new = [kvn[j:j + 1, D_KV:] for j in seqs]
    ck = [ck_ref[j] for j in seqs]
    cv = [cv_ref[j] for j in seqs]
    for j in seqs:
        kout_ref[j] = jnp.where(last, col[0:D_KV, j:j + 1], pltpu.roll(ck[j], WINDOW - 1, 1))
        vout_ref[j] = jnp.where(last, col[D_KV:, j:j + 1], pltpu.roll(cv[j], WINDOW - 1, 1))
    q8 = [qr_ref[j * nh:(j + 1) * nh, :] for j in seqs]
    qp = [jnp.where(gmask, jnp.concatenate([q8[j], q8[j]], axis=1), 0.0) for j in seqs]
    s_c = [jnp.where(key_idx >= 1, jnp.dot(qp[j], ck[j], preferred_element_type=F32) * ATTN_SCALE, NEG_INF)
           for j in seqs]
    s_n = [jnp.sum(qp[j] * k_new[j], axis=-1, keepdims=True) * ATTN_SCALE for j in seqs]
    m = [jnp.maximum(jnp.maximum(jnp.max(s_c[j], axis=-1, keepdims=True), s_n[j]), sink) for j in seqs]
    e_c = [jnp.exp(s_c[j] - m[j]) for j in seqs]
    e_n = [jnp.exp(s_n[j] - m[j]) for j in seqs]
    denom = [jnp.sum(e_c[j], axis=-1, keepdims=True) + e_n[j] + jnp.exp(sink - m[j]) for j in seqs]
    o = [(_dot_nt_f32(e_c[j], cv[j]) + e_n[j] * v_new[j]) / denom[j]
         for j in seqs]
    out_rows = []
    for j in seqs:
        rot = pltpu.roll(o[j], HEAD_DIM, 1)
        g0 = jnp.where(low, o[j], rot)
        g1 = jnp.where(low, rot, o[j])
        wide = jnp.concatenate([g0, g0, g1, g1], axis=1)
        out_rows.append(jnp.sum(jnp.where(dmask, wide, 0.0), axis=0, keepdims=True))
    ya_ref[...] = jnp.concatenate(out_rows, axis=0)


def _decode_attn(q_r, kvn, cols, ck_t, cv_t, sinks_col):
    n = kvn.shape[0]
    bt = DEC_TILE
    const = lambda shape: pl.BlockSpec(shape, lambda i: (0,) * len(shape))
    return pl.pallas_call(
        _decode_attn_kernel,
        grid=(n // bt,),
        in_specs=[
            pl.BlockSpec((bt * N_Q_HEADS, HEAD_DIM), lambda i: (i, 0)),
            pl.BlockSpec((bt, 2 * D_KV), lambda i: (i, 0)),
            pl.BlockSpec((1, 2 * D_KV, bt), lambda i: (i, 0, 0)),
            pl.BlockSpec((bt, D_KV, WINDOW), lambda i: (i, 0, 0)),
            pl.BlockSpec((bt, D_KV, WINDOW), lambda i: (i, 0, 0)),
            const((N_Q_HEADS, 1)),
        ],
        out_specs=[
            pl.BlockSpec((bt, D_ATTN), lambda i: (i, 0)),
            pl.BlockSpec((bt, D_KV, WINDOW), lambda i: (i, 0, 0)),
            pl.BlockSpec((bt, D_KV, WINDOW), lambda i: (i, 0, 0)),
        ],
        out_shape=[
            jax.ShapeDtypeStruct((n, D_ATTN), F32),
            jax.ShapeDtypeStruct((n, D_KV, WINDOW), F32),
            jax.ShapeDtypeStruct((n, D_KV, WINDOW), F32),
        ],
        compiler_params=pltpu.CompilerParams(
            dimension_semantics=("arbitrary",), vmem_limit_bytes=VMEM_LIMIT),
        name="decode_attn",
    )(q_r, kvn, cols, ck_t, cv_t, sinks_col)


def _pad_cols(w, at, n):
    return jnp.concatenate([w[..., :at], jnp.zeros(w.shape[:-1] + (n,), w.dtype), w[..., at:]], axis=-1)


def kernel(x_prompt, x_sample, state_wkv, state_shift, cache_k_win, cache_v_win, norm_mix_w, w_in, mu_shift, w0, w_decay_up, a0, w_a_up, w_g_up, k_k, k_a, r_k, ln_x_w, ln_x_b, q_norm_w, k_norm_w, sinks, w_out, norm_ffn_w, w_ffn_up, w_ffn_down):
    bsz, t, _ = x_prompt.shape
    nd = x_sample.shape[0]
    l = 0
    pad = D_LORA_PAD - D_LORA

    w_in_t = jnp.swapaxes(w_in[l], 0, 1)
    w_in_pad = jnp.concatenate([w_in_t[:D_SHIFT], jnp.zeros((pad, D_MODEL), F32), w_in_t[D_SHIFT:]],
                               axis=0).astype(BF16)
    mu_pad = _pad_cols(mu_shift[l][None, :], D_SHIFT, pad)
    wl = jnp.zeros((D_LORA_PAD, 3 * D_RWKV), F32)
    wl = wl.at[0:32, 0:D_RWKV].set(w_decay_up[l])
    wl = wl.at[32:64, D_RWKV:2 * D_RWKV].set(w_a_up[l])
    wl = wl.at[64:160, 2 * D_RWKV:].set(w_g_up[l])
    wl = wl.astype(BF16)
    prm = jnp.zeros((16, D_RWKV), F32)
    prm = prm.at[0].set(w0[l]).at[1].set(a0[l]).at[2].set(k_k[l]).at[3].set(k_a[l])
    prm = prm.at[4].set(r_k[l].reshape(-1)).at[5].set(ln_x_w[l]).at[6].set(ln_x_b[l])
    hid = np.arange(QUAD) // HEAD_DIM
    bmask = jnp.asarray(hid[:, None] == hid[None, :], BF16)
    tri = jnp.asarray(np.tril(np.ones((CHUNK, CHUNK))), BF16)
    qw = jnp.tile(q_norm_w[l][None, :], (1, N_Q_HEADS))
    kw = jnp.tile(k_norm_w[l][None, :], (1, N_KV_HEADS))
    nmw = norm_mix_w[l][None, :]
    nfw = norm_ffn_w[l][None, :]
    wo = w_out[l].astype(BF16)
    wu = w_ffn_up[l].astype(BF16)
    wd = w_ffn_down[l].astype(BF16)
    tab_a, tab_b = _rope_block_tables(max(t, PAST_LEN + 1) // WINDOW + 1)
    ta, tb = tab_a[PAST_LEN // WINDOW], tab_b[PAST_LEN % WINDOW][None, :]
    tab_s = jnp.concatenate([ta[:, :128] * tb[:, :128] - ta[:, 128:] * tb[:, 128:],
                             ta[:, 128:] * tb[:, :128] + ta[:, :128] * tb[:, 128:]], axis=1)

    xs = x_sample.reshape(nd, D_MODEL)
    p_s, q_s, kv_s = _inproj(xs, nmw, w_in_pad, 128)
    shift_in = _pad_cols(state_shift[l].reshape(nd, D_SHIFT), D_SHIFT, pad)
    vec_t, gb, qn_s, kvn_s = _decode_prep(p_s, shift_in, q_s, kv_s, mu_pad, prm, wl, bmask, tab_s, qw, kw)
    s_t = jnp.transpose(state_wkv[l], (1, 2, 3, 0))
    ck_t = jnp.swapaxes(cache_k_win[l].reshape(nd, WINDOW, D_KV), 1, 2)
    cv_t = jnp.swapaxes(cache_v_win[l].reshape(nd, WINDOW, D_KV), 1, 2)
    q_r = qn_s.reshape(nd * N_Q_HEADS, HEAD_DIM)
    cols = jnp.swapaxes(kvn_s.reshape(nd // DEC_TILE, DEC_TILE, 2 * D_KV), 1, 2)
    wkv_t, yr_s = _decode_state(vec_t, gb, prm, bmask, s_t)
    ya_s, kc_t, vc_t = _decode_attn(q_r, kvn_s, cols, ck_t, cv_t, sinks[l][:, None])

    xp = x_prompt.reshape(bsz * t, D_MODEL)
    xs_p, plast, q_p, kv_p = _inproj_shift(xp, nmw, w_in_pad, mu_pad, t, 1024)
    yr_p, hbd = _rwkv_prompt(xs_p.reshape(bsz, t, D_SHIFT_PAD), prm, wl, tri, bmask)
    y_prompt, y_s, kwin_p, vwin_p = _attn_ffn(q_p.reshape(bsz, t, D_ATTN), kv_p.reshape(bsz, t, 2 * D_KV),
                                              tab_a, tab_b, qw, kw, sinks[l][None, :], bmask, _band_bias(),
                                              x_prompt, yr_p, xs, yr_s, ya_s, wo, nfw, wu, wd)
    hb = hbd.reshape(bsz, 2, 4, HEAD_DIM, 2, HEAD_DIM)
    wkv_prompt = jnp.stack([hb[:, :, j, :, j % 2, :] for j in range(4)], axis=2)
    wkv_prompt = wkv_prompt.reshape(bsz, H_RWKV, HEAD_DIM, HEAD_DIM)[None]
    shift_prompt = plast[:, 0:1, :D_SHIFT][None]
    k_win_prompt = jnp.swapaxes(kwin_p, 1, 2).reshape(bsz, WINDOW, N_KV_HEADS, HEAD_DIM)[None]
    v_win_prompt = jnp.swapaxes(vwin_p, 1, 2).reshape(bsz, WINDOW, N_KV_HEADS, HEAD_DIM)[None]

    y_sample = y_s.reshape(nd, 1, D_MODEL)
    wkv_sample = jnp.transpose(wkv_t, (3, 0, 1, 2))[None]
    shift_sample = p_s[:, :D_SHIFT].reshape(nd, 1, D_SHIFT)[None]
    k_win_sample = jnp.swapaxes(kc_t, 1, 2).reshape(nd, WINDOW, N_KV_HEADS, HEAD_DIM)[None]
    v_win_sample = jnp.swapaxes(vc_t, 1, 2).reshape(nd, WINDOW, N_KV_HEADS, HEAD_DIM)[None]

    return (y_prompt, y_sample, wkv_prompt, shift_prompt, k_win_prompt, v_win_prompt,
            wkv_sample, shift_sample, k_win_sample, v_win_sample)
```

```python
import functools

import jax
import jax.numpy as jnp
import numpy as np
from jax import lax
from jax.experimental import pallas as pl
from jax.experimental.pallas import tpu as pltpu

F32 = jnp.float32
BF16 = jnp.bfloat16

D_MODEL = 1024
D_RWKV = 512
D_ATTN = 512
HEAD_DIM = 64
H_RWKV = 8
N_Q_HEADS = 8
N_KV_HEADS = 2
GQA_GROUP = 4
D_KV = 128
D_LORA = 160
D_SHIFT = 3 * D_RWKV + D_LORA
D_IN = D_SHIFT + D_ATTN + 2 * D_KV
WINDOW = 128
ROPE_DIM = 16
ROPE_HALF = 8
ROPE_THETA = 500000.0
ATTN_SCALE = HEAD_DIM ** -0.5
D_FF = 4096
RMS_EPS = 1e-6
LNX_EPS = 64e-5
NEG_INF = -1e30
LOG2E = 1.4426950408889634
PAST_LEN = 16384

CHUNK = 64
QUAD = 4 * HEAD_DIM
V7X_VMEM_BYTES = 64 * 1024 * 1024
VMEM_LIMIT = V7X_VMEM_BYTES // 8 * 7


def _split2(x):
    hi = x.astype(BF16)
    lo = (x - hi.astype(F32)).astype(BF16)
    return hi, lo


def _head_sums(xs, ones_ref):
    n, w = xs[0].shape
    tile = min(w, QUAD)
    per = w // tile
    pieces = [x[:, j * tile:(j + 1) * tile] for x in xs for j in range(per)]
    stacked = jnp.concatenate(pieces, axis=0) if len(pieces) > 1 else pieces[0]
    ones = ones_ref[0:tile, 0:tile]
    out = jnp.dot(stacked.astype(BF16), ones, preferred_element_type=F32)
    res = []
    for i in range(len(xs)):
        cols = [out[(i * per + j) * n:(i * per + j + 1) * n] for j in range(per)]
        res.append(jnp.concatenate(cols, axis=1) if per > 1 else cols[0])
    return res


def _cumsum_rows(tri_bf16, x):
    hi, lo = _split2(x)
    return (jnp.dot(tri_bf16, hi, preferred_element_type=F32)
            + jnp.dot(tri_bf16, lo, preferred_element_type=F32))


def _mm(a, b):
    return jnp.dot(a.astype(BF16), b.astype(BF16), preferred_element_type=F32)


def _mm_nt(a, b):
    return lax.dot_general(a.astype(BF16), b.astype(BF16), (((1,), (1,)), ((), ())),
                           preferred_element_type=F32)


def _mm_tn(a, b):
    return lax.dot_general(a.astype(BF16), b.astype(BF16), (((0,), (0,)), ((), ())),
                           preferred_element_type=F32)


def _dot_nt_f32(a, b):
    return lax.dot_general(a, b, (((1,), (1,)), ((), ())), preferred_element_type=F32)


def _sigmoid(x):
    return 1.0 / (1.0 + jnp.exp(-x))


def _interleave(*gens):
    live = list(gens)
    while live:
        for g in list(live):
            try:
                next(g)
            except StopIteration:
                live.remove(g)


def _norm_project(x, nw_ref, wt_ref):
    ms = jnp.mean(x * x, axis=-1, keepdims=True)
    h = ((x * lax.rsqrt(ms + RMS_EPS)) * nw_ref[...]).astype(BF16)
    return _mm_nt(h, wt_ref[0:D_SHIFT, :]), _mm_nt(h, wt_ref[D_SHIFT:, :])


def _inproj_kernel(x_ref, nw_ref, w_ref, p_ref, q_ref, kv_ref):
    p, qkv = _norm_project(x_ref[...], nw_ref, w_ref)
    p_ref[...] = p
    q_ref[...] = qkv[:, :D_ATTN]
    kv_ref[...] = qkv[:, D_ATTN:]


def _inproj_shift_kernel(tiles_per_seq, x_ref, nw_ref, w_ref, mu_ref, xs_ref, last_ref, q_ref, kv_ref, prev_ref):
    i = pl.program_id(0)

    @pl.when(i % tiles_per_seq == 0)
    def _():
        prev_ref[...] = jnp.zeros_like(prev_ref)

    p, qkv = _norm_project(x_ref[...], nw_ref, w_ref)
    tm = p.shape[0]
    row = lax.broadcasted_iota(jnp.int32, p.shape, 0)
    prev = jnp.where(row == 0, jnp.broadcast_to(prev_ref[0:1, :], p.shape), pltpu.roll(p, 1, 0))
    xs_ref[...] = p + (prev - p) * mu_ref[...]
    last = jnp.broadcast_to(p[tm - 1:tm, :], prev_ref.shape)
    prev_ref[...] = last
    last_ref[0] = last
    q_ref[...] = qkv[:, :D_ATTN]
    kv_ref[...] = qkv[:, D_ATTN:]


def _inproj_shift(x2d, norm_w, w_in_pad, mu_pad, seq_len, tm):
    m = x2d.shape[0]
    tiles_per_seq = seq_len // tm
    return pl.pallas_call(
        functools.partial(_inproj_shift_kernel, tiles_per_seq),
        grid=(m // tm,),
        in_specs=[
            pl.BlockSpec((tm, D_MODEL), lambda i: (i, 0)),
            pl.BlockSpec((1, D_MODEL), lambda i: (0, 0)),
            pl.BlockSpec((D_IN, D_MODEL), lambda i: (0, 0)),
            pl.BlockSpec((1, D_SHIFT), lambda i: (0, 0)),
        ],
        out_specs=[
            pl.BlockSpec((tm, D_SHIFT), lambda i: (i, 0)),
            pl.BlockSpec((1, 8, D_SHIFT), lambda i: (i // tiles_per_seq, 0, 0)),
            pl.BlockSpec((tm, D_ATTN), lambda i: (i, 0)),
            pl.BlockSpec((tm, 2 * D_KV), lambda i: (i, 0)),
        ],
        out_shape=[
            jax.ShapeDtypeStruct((m, D_SHIFT), F32),
            jax.ShapeDtypeStruct((m // seq_len, 8, D_SHIFT), F32),
            jax.ShapeDtypeStruct((m, D_ATTN), F32),
            jax.ShapeDtypeStruct((m, 2 * D_KV), F32),
        ],
        scratch_shapes=[pltpu.VMEM((8, D_SHIFT), F32)],
        compiler_params=pltpu.CompilerParams(
            dimension_semantics=("arbitrary",), vmem_limit_bytes=VMEM_LIMIT),
        name="inproj_shift",
    )(x2d, norm_w, w_in_pad, mu_pad)


def _inproj(x2d, norm_w, w_in_pad, tm):
    m = x2d.shape[0]
    return pl.pallas_call(
        _inproj_kernel,
        grid=(m // tm,),
        in_specs=[
            pl.BlockSpec((tm, D_MODEL), lambda i: (i, 0)),
            pl.BlockSpec((1, D_MODEL), lambda i: (0, 0)),
            pl.BlockSpec((D_IN, D_MODEL), lambda i: (0, 0)),
        ],
        out_specs=[
            pl.BlockSpec((tm, D_SHIFT), lambda i: (i, 0)),
            pl.BlockSpec((tm, D_ATTN), lambda i: (i, 0)),
            pl.BlockSpec((tm, 2 * D_KV), lambda i: (i, 0)),
        ],
        out_shape=[
            jax.ShapeDtypeStruct((m, D_SHIFT), F32),
            jax.ShapeDtypeStruct((m, D_ATTN), F32),
            jax.ShapeDtypeStruct((m, 2 * D_KV), F32),
        ],
        compiler_params=pltpu.CompilerParams(
            dimension_semantics=("arbitrary",), vmem_limit_bytes=VMEM_LIMIT),
        name="inproj",
    )(x2d, norm_w, w_in_pad)


def _rwkv_features(xs, prm, wl_ref, ones_ref):
    r = xs[:, 0:D_RWKV]
    k = xs[:, D_RWKV:2 * D_RWKV]
    v = xs[:, 2 * D_RWKV:3 * D_RWKV]
    lora = xs[:, 3 * D_RWKV:]
    col = lax.broadcasted_iota(jnp.int32, lora.shape, 1)
    act = jnp.where(col < 32, jnp.tanh(lora), jnp.where(col < 64, lora, _sigmoid(lora)))
    up = jnp.dot(act.astype(BF16), wl_ref[...], preferred_element_type=F32)
    w0, a0, k_k, k_a, r_k = prm[0:1], prm[1:2], prm[2:3], prm[3:4], prm[4:5]
    logw = (-np.exp(-0.5)) * _sigmoid(w0 + up[:, 0:D_RWKV])
    asig = _sigmoid(a0 + up[:, D_RWKV:2 * D_RWKV])
    g = up[:, 2 * D_RWKV:]
    kk = k * k_k
    k_mod = k * (1.0 + (asig - 1.0) * k_a)
    ss, rk = _head_sums([kk * kk, r * k_mod * r_k], ones_ref)
    kk = kk / jnp.maximum(jnp.sqrt(ss), 1e-12)
    k = k_mod
    bonus = rk * v
    return r, logw, k, v, -kk, kk * asig, g, bonus


def _rwkv_finish(y, g, bonus, prm, ones_ref):
    ln_w, ln_b = prm[5:6], prm[6:7]
    mean = _head_sums([y], ones_ref)[0] * (1.0 / HEAD_DIM)
    d = y - mean
    var = _head_sums([d * d], ones_ref)[0] * (1.0 / HEAD_DIM)
    yn = d * lax.rsqrt(var + LNX_EPS) * ln_w + ln_b
    return (yn + bonus) * g


def _block_diag(x, bmask):
    return jnp.concatenate([x] * 4, axis=0) * bmask


def _chunk_prep(r, logw, k, v, a, b, tri_ref, bmask):
    n = len(r)
    ch = range(n)
    tri = tri_ref[...]
    cum = [_cumsum_rows(tri, logw[i]) for i in ch]
    yield
    e_in = [jnp.exp(cum[i]) for i in ch]
    e_ex = [jnp.exp(cum[i] - logw[i]) for i in ch]
    e_inv = [1.0 / e_in[i] for i in ch]
    e_last = [e_in[i][CHUNK - 1:CHUNK, :] for i in ch]
    rt = [(r[i] * e_in[i]).astype(BF16) for i in ch]
    at = [(a[i] * e_ex[i]).astype(BF16) for i in ch]
    kt = [(k[i] * e_inv[i]).astype(BF16) for i in ch]
    bt = [(b[i] * e_inv[i]).astype(BF16) for i in ch]
    vb = [v[i].astype(BF16) for i in ch]
    yield

    t_idx = lax.broadcasted_iota(jnp.int32, (CHUNK, QUAD), 0)
    s_idx = lax.broadcasted_iota(jnp.int32, (CHUNK, QUAD), 1) & (HEAD_DIM - 1)
    strict = s_idx < t_idx
    incl = s_idx <= t_idx

    gm = [_mm_nt(jnp.concatenate([at[i], rt[i]], axis=0),
                 jnp.concatenate([_block_diag(bt[i], bmask), _block_diag(kt[i], bmask)], axis=0))
          for i in ch]
    yield
    a_ab = [jnp.where(strict, gm[i][:CHUNK, :QUAD], 0.0) for i in ch]
    a_ak = [jnp.where(strict, gm[i][:CHUNK, QUAD:], 0.0) for i in ch]
    a_rb = [jnp.where(incl, gm[i][CHUNK:, :QUAD], 0.0) for i in ch]
    a_rk = [jnp.where(incl, gm[i][CHUNK:, QUAD:], 0.0) for i in ch]

    eye = jnp.where(s_idx == t_idx, 1.0, 0.0)
    pwb = [a_ab[i].astype(BF16) for i in ch]
    t_inv = [eye + a_ab[i] for i in ch]
    for it in range(6):
        rbd = [_block_diag(pwb[i], bmask) for i in ch]
        if it == 0:
            pwb = [_mm(pwb[i], rbd[i]).astype(BF16) for i in ch]
        elif it < 5:
            out = [_mm(jnp.concatenate([pwb[i], t_inv[i].astype(BF16)], axis=0), rbd[i]) for i in ch]
            pwb = [out[i][:CHUNK].astype(BF16) for i in ch]
            t_inv = [t_inv[i] + out[i][CHUNK:] for i in ch]
        else:
            t_inv = [t_inv[i] + _mm(t_inv[i], rbd[i]) for i in ch]
        yield

    vbd = [_block_diag(vb[i], bmask) for i in ch]
    xy0 = [_mm(jnp.concatenate([a_ak[i], a_rk[i]], axis=0), vbd[i]) for i in ch]
    return [dict(ar=jnp.concatenate([at[i], rt[i]], axis=0), x0=xy0[i][:CHUNK], y0=xy0[i][CHUNK:],
                 t_inv=t_inv[i].astype(BF16), a_rb=a_rb[i].astype(BF16), vb=vb[i],
                 bk=jnp.concatenate([bt[i], kt[i]], axis=0), e_last=e_last[i]) for i in ch]


def _chunk_step(pre, state, bmask, out):
    ch = range(len(pre))
    half = QUAD // 2
    zeros = jnp.zeros((half, half), BF16)
    sc = [state[i].astype(BF16) for i in ch]
    sb = [jnp.concatenate([jnp.concatenate([sc[i][:half], zeros], axis=1),
                           jnp.concatenate([zeros, sc[i][half:]], axis=1)], axis=0) for i in ch]
    xr = [_mm_nt(pre[i]["ar"], sb[i]) for i in ch]
    yield
    x = [xr[i][:CHUNK] + pre[i]["x0"] for i in ch]
    u = [_mm(pre[i]["t_inv"], _block_diag(x[i].astype(BF16), bmask)) for i in ch]
    yield
    ub = [u[i].astype(BF16) for i in ch]
    out["y"] = [xr[i][CHUNK:] + pre[i]["y0"] + _mm(pre[i]["a_rb"], _block_diag(ub[i], bmask)) for i in ch]
    upd = [_mm_tn(jnp.concatenate([ub[i], pre[i]["vb"]], axis=0), pre[i]["bk"]) for i in ch]
    yield
    bm = bmask[:half, :half].astype(F32)
    s_new = []
    for i in ch:
        e_last = pre[i]["e_last"]
        top = (state[i][:half] + upd[i][:half, :half] * bm) * e_last[:, :half]
        bot = (state[i][half:] + upd[i][half:, half:] * bm) * e_last[:, half:]
        s_new.append(jnp.concatenate([top, bot], axis=0))
    out["state"] = s_new


def _finish_stages(ys, g, bonus, prm, ones_ref, y_ref, j):
    nseq = len(ys) // 2
    ln_w, ln_b = prm[5:6], prm[6:7]
    y = jnp.concatenate([jnp.concatenate(ys[2 * s:2 * s + 2], axis=1) for s in range(nseq)], axis=0)
    mean = _head_sums([y], ones_ref)[0] * (1.0 / HEAD_DIM)
    yield
    d = y - mean
    var = _head_sums([d * d], ones_ref)[0] * (1.0 / HEAD_DIM)
    yield
    out = (d * lax.rsqrt(var + LNX_EPS) * ln_w + ln_b + bonus) * g
    for s in range(nseq):
        y_ref[s, j * CHUNK:(j + 1) * CHUNK, :] = out[s * CHUNK:(s + 1) * CHUNK]


def _rwkv_prompt_kernel(xs_ref, prm_ref, wl_ref, tri_ref, bmask_ref, y_ref, hout_ref, h_ref):
    c = pl.program_id(0)
    nseq, tstep, _ = xs_ref.shape
    nchunk = tstep // CHUNK

    @pl.when(c == 0)
    def _():
        h_ref[...] = jnp.zeros_like(h_ref)

    xs = jnp.concatenate([xs_ref[s] for s in range(nseq)], axis=0)
    prm = prm_ref[...]
    r, logw, k, v, a, b, g, bonus = _rwkv_features(xs, prm, wl_ref, bmask_ref)
    bmask = bmask_ref[...]
    lanes = [(s, q) for s in range(nseq) for q in range(2)]
    pre = {}

    def prep(chunks):
        chains = [(j, s, q) for j in chunks for s, q in lanes]
        cut = lambda x: [x[s * tstep + j * CHUNK:s * tstep + (j + 1) * CHUNK, q * QUAD:(q + 1) * QUAD]
                         for j, s, q in chains]
        res = yield from _chunk_prep(cut(r), cut(logw), cut(k), cut(v), cut(a), cut(b), tri_ref, bmask)
        for n, j in enumerate(chunks):
            pre[j] = res[n * len(lanes):(n + 1) * len(lanes)]

    nfirst = max(nchunk - 1, 1)
    _interleave(prep(range(nfirst)))
    later = prep(range(nfirst, nchunk))
    state = [h_ref[s, q] for s, q in lanes]
    crow = lambda x, j: jnp.concatenate([x[s * tstep + j * CHUNK:s * tstep + (j + 1) * CHUNK]
                                         for s in range(nseq)], axis=0)
    finish = iter(())
    for j in range(nchunk):
        res = {}
        for _ in _chunk_step(pre[j], state, bmask, res):
            next(later, None)
            next(finish, None)
        _interleave(finish)
        if j == nfirst - 1:
            _interleave(later)
        state = res["state"]
        finish = _finish_stages(res["y"], crow(g, j), crow(bonus, j), prm, bmask_ref, y_ref, j)
    _interleave(finish)
    for i, (s, q) in enumerate(lanes):
        h_ref[s, q] = state[i]

    @pl.when(c == pl.num_programs(0) - 1)
    def _():
        hout_ref[...] = h_ref[...]


RWKV_CHUNKS_PER_STEP = 4


def _rwkv_prompt(xs3d, prm, wl, tri, bmask):
    bsz, t, _ = xs3d.shape
    tstep = RWKV_CHUNKS_PER_STEP * CHUNK
    nc = t // tstep
    const = lambda shape: pl.BlockSpec(shape, lambda c: (0,) * len(shape))
    state_shape = (bsz, 2, QUAD, QUAD // 2)
    return pl.pallas_call(
        _rwkv_prompt_kernel,
        grid=(nc,),
        in_specs=[
            pl.BlockSpec((bsz, tstep, D_SHIFT), lambda c: (0, c, 0)),
            const((16, D_RWKV)),
            const((D_LORA, 3 * D_RWKV)),
            const((CHUNK, CHUNK)),
            const((QUAD, QUAD)),
        ],
        out_specs=[
            pl.BlockSpec((bsz, tstep, D_RWKV), lambda c: (0, c, 0)),
            const(state_shape),
        ],
        out_shape=[
            jax.ShapeDtypeStruct((bsz, t, D_RWKV), F32),
            jax.ShapeDtypeStruct(state_shape, F32),
        ],
        scratch_shapes=[pltpu.VMEM(state_shape, F32)],
        compiler_params=pltpu.CompilerParams(
            dimension_semantics=("arbitrary",), vmem_limit_bytes=VMEM_LIMIT),
        name="rwkv_prompt",
    )(xs3d, prm, wl, tri, bmask)


def _rope_lane_freq():
    inv_freq = jnp.power(ROPE_THETA, -jnp.arange(ROPE_HALF, dtype=F32) * (2.0 / ROPE_DIM))
    return inv_freq[(np.arange(2 * HEAD_DIM) % HEAD_DIM) % ROPE_HALF][None, :]


def _rope_tables(cos, sin):
    dim = lax.broadcasted_iota(jnp.int32, cos.shape, 1) & (HEAD_DIM - 1)
    cos_t = jnp.where(dim < ROPE_DIM, cos, 1.0)
    sin_lo = jnp.where(dim < ROPE_HALF, -sin, 0.0)
    sin_hi = jnp.where((dim >= ROPE_HALF) & (dim < ROPE_DIM), sin, 0.0)
    return cos_t, sin_lo, sin_hi


def _qk_norm_rope(x, norm_w, cos_t, sin_lo, sin_hi, ones_ref):
    ms = _head_sums([x * x], ones_ref)[0] * (1.0 / HEAD_DIM)
    xn = x * lax.rsqrt(ms + RMS_EPS) * norm_w
    width = x.shape[1]
    fwd = pltpu.roll(xn, width - ROPE_HALF, 1)
    bwd = pltpu.roll(xn, ROPE_HALF, 1)
    return xn * cos_t + fwd * sin_lo + bwd * sin_hi


def _tile_lanes(x, reps):
    return jnp.concatenate([x] * reps, axis=1) if reps > 1 else x


def _attn_stages(q_ref, kv_ref, taba_ref, tabb_ref, qw_ref, kw_ref, sink_ref, ones_ref, bias_ref, first_bias,
                 kprev_ref, vprev_ref, out):
    nseq, tq, _ = q_ref.shape
    nblk = tq // WINDOW
    units = [(s, b) for s in range(nseq) for b in range(nblk)]
    blk = lambda b: slice(b * WINDOW, (b + 1) * WINDOW)
    tb = tabb_ref[...]
    cos_b, sin_b = tb[:, :128], tb[:, 128:]
    rope, rope4 = [], []
    for b in range(nblk):
        ta = taba_ref[b][0:1, :]
        cos_a, sin_a = ta[:, :128], ta[:, 128:]
        tabs = _rope_tables(cos_a * cos_b - sin_a * sin_b, sin_a * cos_b + cos_a * sin_b)
        rope.append(tabs)
        rope4.append([_tile_lanes(x, 4) for x in tabs])
    q = {(s, b): _qk_norm_rope(q_ref[s, blk(b), :], qw_ref[...], *rope4[b], ones_ref) * (ATTN_SCALE * LOG2E)
         for s, b in units}
    kv = {(s, b): kv_ref[s, blk(b), :] for s, b in units}
    k_cur = {u: _qk_norm_rope(kv[u][:, 0:D_KV], kw_ref[...], *rope[u[1]], ones_ref) for u in units}
    v_cur = {u: kv[u][:, D_KV:] for u in units}
    k_all = {(s, b): jnp.concatenate([kprev_ref[s] if b == 0 else k_cur[(s, b - 1)], k_cur[(s, b)]], axis=0)
             for s, b in units}
    v_all = {(s, b): jnp.concatenate([vprev_ref[s] if b == 0 else v_cur[(s, b - 1)], v_cur[(s, b)]], axis=0)
             for s, b in units}
    out["k_cur"] = [k_cur[(s, nblk - 1)] for s in range(nseq)]
    out["v_cur"] = [v_cur[(s, nblk - 1)] for s in range(nseq)]
    for s in range(nseq):
        kprev_ref[s] = out["k_cur"][s]
        vprev_ref[s] = out["v_cur"][s]
    yield

    nk = 2 * WINDOW
    bias = [bias_ref[first_bias] if b == 0 else bias_ref[1] for b in range(nblk)]
    sinks = sink_ref[...] * LOG2E
    low = lax.broadcasted_iota(jnp.int32, (nk, D_KV), 1) < HEAD_DIM
    lane_blk = [ones_ref[j * HEAD_DIM:j * HEAD_DIM + 1, :] for j in range(GQA_GROUP)]

    chains = [(u, g) for u in units for g in range(N_KV_HEADS)]
    ch = range(len(chains))
    k_rot = {u: pltpu.roll(k_all[u], HEAD_DIM, 1) for u in units}
    k2 = [jnp.where(low, k_all[u], k_rot[u]) if g == 0 else jnp.where(low, k_rot[u], k_all[u])
          for u, g in chains]
    k4 = [jnp.concatenate([k2[c], k2[c]], axis=1).astype(BF16) for c in ch]
    vb = {u: v_all[u].astype(BF16) for u in units}
    qg = [q[u][:, g * QUAD:(g + 1) * QUAD].astype(BF16) for u, g in chains]
    qstack = [jnp.concatenate([qg[c] * lane_blk[j] for j in range(GQA_GROUP)], axis=0) for c in ch]
    sink_row = [jnp.concatenate(
        [jnp.broadcast_to(sinks[:, g * GQA_GROUP + j:g * GQA_GROUP + j + 1], (1, WINDOW))
         for j in range(GQA_GROUP)], axis=1) for u, g in chains]
    yield
    sc = [_mm_nt(k4[c], qstack[c]) + bias[u[1]] for c, (u, g) in enumerate(chains)]
    yield
    m = [jnp.maximum(jnp.max(sc[c], axis=0, keepdims=True), sink_row[c]) for c in ch]
    e = [jnp.exp2(sc[c] - m[c]) for c in ch]
    yield
    denom = [jnp.sum(e[c], axis=0, keepdims=True) + jnp.exp2(sink_row[c] - m[c]) for c in ch]
    ot = {(u, g): _mm_tn(vb[u], e[c].astype(BF16))[g * HEAD_DIM:(g + 1) * HEAD_DIM, :] * (1.0 / denom[c])
          for c, (u, g) in enumerate(chains)}
    yield
    ya = []
    for s in range(nseq):
        blocks = []
        for b in range(nblk):
            yt = jnp.concatenate([ot[((s, b), g)][:, j * WINDOW:(j + 1) * WINDOW]
                                  for g in range(N_KV_HEADS) for j in range(GQA_GROUP)], axis=0)
            blocks.append(jnp.transpose(yt))
        ya.append(jnp.concatenate(blocks, axis=0) if nblk > 1 else blocks[0])
    out["ya"] = ya


def _ffn_stages(x, yr, ya, wo_ref, nw_ref, wu_ref, wd_ref, out, pieces=4):
    mix = jnp.concatenate([yr, ya], axis=1).astype(BF16)
    x1 = x + jnp.dot(mix, wo_ref[...], preferred_element_type=F32)
    yield
    ms = jnp.mean(x1 * x1, axis=-1, keepdims=True)
    hf = ((x1 * lax.rsqrt(ms + RMS_EPS)) * nw_ref[...]).astype(BF16)
    acc = x1
    step = D_FF // pieces
    for j in range(pieces):
        up = jnp.dot(hf, wu_ref[:, j * step:(j + 1) * step], preferred_element_type=F32)
        yield
        act = jnp.square(jnp.maximum(up, 0.0)).astype(BF16)
        acc = acc + jnp.dot(act, wd_ref[j * step:(j + 1) * step, :], preferred_element_type=F32)
        yield
    out["y"] = acc


def _attn_ffn_kernel(q_ref, kv_ref, taba_ref, tabb_ref, qw_ref, kw_ref, sink_ref, ones_ref, bias_ref,
                     x_ref, yr_ref, xd_ref, yrd_ref, yad_ref, wo_ref, nw_ref, wu_ref, wd_ref,
                     o_ref, od_ref, kwin_ref, vwin_ref, kprev_ref, vprev_ref, ya_ref):
    i = pl.program_id(0)
    nseq, tq, _ = q_ref.shape
    seqs = range(nseq)
    first = i == 0

    @pl.when(first)
    def _():
        kprev_ref[...] = jnp.zeros_like(kprev_ref)
        vprev_ref[...] = jnp.zeros_like(vprev_ref)
        ya_ref[...] = jnp.zeros_like(ya_ref)

    def rows(ref, dec_ref):
        tile = jnp.concatenate([ref[s] for s in seqs], axis=0)
        dec = dec_ref[...]
        return jnp.where(first, jnp.concatenate([dec] * (tile.shape[0] // dec.shape[0]), axis=0), tile)

    a_out, f_out = {}, {}
    _interleave(
        _ffn_stages(rows(x_ref, xd_ref), rows(yr_ref, yrd_ref), rows(ya_ref, yad_ref),
                    wo_ref, nw_ref, wu_ref, wd_ref, f_out),
        _attn_stages(q_ref, kv_ref, taba_ref, tabb_ref, qw_ref, kw_ref, sink_ref, ones_ref, bias_ref,
                     jnp.minimum(i, 1), kprev_ref, vprev_ref, a_out))
    for s in seqs:
        o_ref[s] = f_out["y"][s * tq:(s + 1) * tq]
        ya_ref[s] = a_out["ya"][s]

    @pl.when(first)
    def _():
        od_ref[...] = f_out["y"][:od_ref.shape[0]]

    @pl.when(i == pl.num_programs(0) - 2)
    def _():
        for s in seqs:
            kwin_ref[s] = jnp.transpose(a_out["k_cur"][s])
            vwin_ref[s] = jnp.transpose(a_out["v_cur"][s])


def _band_bias():
    ki = np.arange(2 * WINDOW)[:, None]
    qi = (np.arange(GQA_GROUP * WINDOW) % WINDOW + WINDOW)[None, :]
    dq = qi - ki
    band = (dq >= 0) & (dq < WINDOW)
    first = band & (ki >= WINDOW)
    return jnp.asarray(np.where(np.stack([first, band]), 0.0, NEG_INF), F32)


def _rope_block_tables(nb):
    freq = _rope_lane_freq()
    ang_a = (jnp.arange(nb, dtype=F32) * WINDOW)[:, None] * freq
    ang_b = jnp.arange(WINDOW, dtype=F32)[:, None] * freq
    tab_a = jnp.concatenate([jnp.cos(ang_a), jnp.sin(ang_a)], axis=1)
    tab_b = jnp.concatenate([jnp.cos(ang_b), jnp.sin(ang_b)], axis=1)
    return jnp.broadcast_to(tab_a[:, None, :], (nb, 8, 4 * HEAD_DIM)), tab_b


ATTN_BLOCKS_PER_STEP = 1


def _attn_ffn(q3d, kv3d, tab_a, tab_b, qw, kw, sinks, ones_bd, bias, x3d, yr3d, xd, yrd, yad, wo, nw, wu, wd):
    bsz, t, _ = q3d.shape
    nd = xd.shape[0]
    nblk = ATTN_BLOCKS_PER_STEP
    tq = nblk * WINDOW
    nt = t // tq
    const = lambda shape: pl.BlockSpec(shape, lambda i: (0,) * len(shape))
    single = lambda shape: pl.BlockSpec(shape, lambda i: (0,) * len(shape), pipeline_mode=pl.Buffered(1))
    cur = lambda i: jnp.minimum(i, nt - 1)
    prv = lambda i: jnp.maximum(i - 1, 0)
    return pl.pallas_call(
        _attn_ffn_kernel,
        grid=(nt + 1,),
        in_specs=[
            pl.BlockSpec((bsz, tq, D_ATTN), lambda i: (0, cur(i), 0)),
            pl.BlockSpec((bsz, tq, 2 * D_KV), lambda i: (0, cur(i), 0)),
            pl.BlockSpec((nblk, 8, 4 * HEAD_DIM), lambda i: (cur(i), 0, 0)),
            const((WINDOW, 4 * HEAD_DIM)),
            const((1, D_ATTN)),
            const((1, D_KV)),
            const((1, N_Q_HEADS)),
            const((QUAD, QUAD)),
            const((2, 2 * WINDOW, GQA_GROUP * WINDOW)),
            pl.BlockSpec((bsz, tq, D_MODEL), lambda i: (0, prv(i), 0)),
            pl.BlockSpec((bsz, tq, D_RWKV), lambda i: (0, prv(i), 0)),
            const((nd, D_MODEL)),
            const((nd, D_RWKV)),
            const((nd, D_ATTN)),
            single((D_MODEL, D_MODEL)),
            const((1, D_MODEL)),
            single((D_MODEL, D_FF)),
            single((D_FF, D_MODEL)),
        ],
        out_specs=[
            pl.BlockSpec((bsz, tq, D_MODEL), lambda i: (0, prv(i), 0)),
            const((nd, D_MODEL)),
            const((bsz, WINDOW, D_KV)),
            const((bsz, WINDOW, D_KV)),
        ],
        out_shape=[
            jax.ShapeDtypeStruct((bsz, t, D_MODEL), F32),
            jax.ShapeDtypeStruct((nd, D_MODEL), F32),
            jax.ShapeDtypeStruct((bsz, WINDOW, D_KV), F32),
            jax.ShapeDtypeStruct((bsz, WINDOW, D_KV), F32),
        ],
        scratch_shapes=[
            pltpu.VMEM((bsz, WINDOW, D_KV), F32),
            pltpu.VMEM((bsz, WINDOW, D_KV), F32),
            pltpu.VMEM((bsz, tq, D_ATTN), F32),
        ],
        compiler_params=pltpu.CompilerParams(
            dimension_semantics=("arbitrary",), vmem_limit_bytes=VMEM_LIMIT),
        name="attn_ffn",
    )(q3d, kv3d, tab_a, tab_b, qw, kw, sinks, ones_bd, bias, x3d, yr3d, xd, yrd, yad, wo, nw, wu, wd)


DEC_TILE = 16


def _decode_prep_kernel(p_ref, sh_ref, q_ref, kv_ref, mu_ref, prm_ref, wl_ref, ones_ref, tab_ref,
                        qw_ref, kw_ref, vec_ref, vgb_ref, qn_ref, kvn_ref):
    p = p_ref[...]
    xs = p + (sh_ref[...] - p) * mu_ref[...]
    r, logw, k, v, a, b, g, bonus = _rwkv_features(xs, prm_ref[...], wl_ref, ones_ref)
    for i, x in enumerate((a, b, k, jnp.exp(logw), r, v)):
        vec_ref[i] = jnp.transpose(x)
    vgb_ref[0] = g
    vgb_ref[1] = bonus
    n = p.shape[0]
    tab = jnp.broadcast_to(tab_ref[0:1, :], (n, 4 * HEAD_DIM))
    cos_t, sin_lo, sin_hi = _rope_tables(tab[:, :128], tab[:, 128:])
    qn_ref[...] = _qk_norm_rope(q_ref[...], qw_ref[...], _tile_lanes(cos_t, 4), _tile_lanes(sin_lo, 4),
                                _tile_lanes(sin_hi, 4), ones_ref)
    kv = kv_ref[...]
    kvn_ref[:, 0:D_KV] = _qk_norm_rope(kv[:, 0:D_KV], kw_ref[...], cos_t, sin_lo, sin_hi, ones_ref)
    kvn_ref[:, D_KV:] = kv[:, D_KV:]


def _decode_prep(p, shift, q, kv, mu_pad, prm, wl, bmask, tab, qw, kw):
    n = p.shape[0]
    full = lambda shape: pl.BlockSpec(shape, lambda i: (0,) * len(shape))
    return pl.pallas_call(
        _decode_prep_kernel,
        grid=(1,),
        in_specs=[full((n, D_SHIFT)), full((n, D_SHIFT)), full((n, D_ATTN)), full((n, 2 * D_KV)),
                  full((1, D_SHIFT)), full((16, D_RWKV)), full((D_LORA, 3 * D_RWKV)),
                  full((QUAD, QUAD)), full((8, 4 * HEAD_DIM)), full((1, D_ATTN)), full((1, D_KV))],
        out_specs=[full((6, D_RWKV, n)), full((2, n, D_RWKV)), full((n, D_ATTN)), full((n, 2 * D_KV))],
        out_shape=[
            jax.ShapeDtypeStruct((6, D_RWKV, n), F32),
            jax.ShapeDtypeStruct((2, n, D_RWKV), F32),
            jax.ShapeDtypeStruct((n, D_ATTN), F32),
            jax.ShapeDtypeStruct((n, 2 * D_KV), F32),
        ],
        compiler_params=pltpu.CompilerParams(
            dimension_semantics=("arbitrary",), vmem_limit_bytes=VMEM_LIMIT),
        name="decode_prep",
    )(p, shift, q, kv, mu_pad, prm, wl, bmask, tab, qw, kw)


def _decode_state_kernel(vec_ref, gb_ref, prm_ref, ones_ref, s_ref, sout_ref, yr_ref, yt_ref):
    h = pl.program_id(0)
    a_t, b_t, k_t, w_t, r_t = (vec_ref[i] for i in range(5))

    def body(i, carry):
        s = s_ref[0, i]
        sa = jnp.sum(s * a_t, axis=0, keepdims=True)
        v_i = vec_ref[5, pl.ds(i, 1), :]
        s_new = s * w_t + sa * b_t + v_i * k_t
        sout_ref[0, i] = s_new
        yt_ref[pl.ds(h * HEAD_DIM + i, 1), :] = jnp.sum(s_new * r_t, axis=0, keepdims=True)
        return carry

    lax.fori_loop(0, HEAD_DIM, body, 0, unroll=8)

    @pl.when(h == pl.num_programs(0) - 1)
    def _():
        y = jnp.transpose(yt_ref[...])
        yr_ref[...] = _rwkv_finish(y, gb_ref[0], gb_ref[1], prm_ref[...], ones_ref)


def _decode_state(vec_t, gb, prm, bmask, s_t):
    n = s_t.shape[-1]
    const = lambda shape: pl.BlockSpec(shape, lambda h: (0,) * len(shape))
    return pl.pallas_call(
        _decode_state_kernel,
        grid=(H_RWKV,),
        in_specs=[
            pl.BlockSpec((6, HEAD_DIM, n), lambda h: (0, h, 0)),
            const((2, n, D_RWKV)),
            const((16, D_RWKV)),
            const((QUAD, QUAD)),
            pl.BlockSpec((1, HEAD_DIM, HEAD_DIM, n), lambda h: (h, 0, 0, 0)),
        ],
        out_specs=[
            pl.BlockSpec((1, HEAD_DIM, HEAD_DIM, n), lambda h: (h, 0, 0, 0)),
            const((n, D_RWKV)),
        ],
        out_shape=[
            jax.ShapeDtypeStruct((H_RWKV, HEAD_DIM, HEAD_DIM, n), F32),
            jax.ShapeDtypeStruct((n, D_RWKV), F32),
        ],
        scratch_shapes=[pltpu.VMEM((D_RWKV, n), F32)],
        compiler_params=pltpu.CompilerParams(
            dimension_semantics=("arbitrary",), vmem_limit_bytes=VMEM_LIMIT),
        name="decode_state",
    )(vec_t, gb, prm, bmask, s_t)


def _decode_attn_kernel(qr_ref, kvn_ref, col_ref, ck_ref, cv_ref, sink_ref, ya_ref, kout_ref, vout_ref):
    nh = N_Q_HEADS
    seqs = range(DEC_TILE)
    hrow = lax.broadcasted_iota(jnp.int32, (nh, D_ATTN), 0)
    hlane = lax.broadcasted_iota(jnp.int32, (nh, D_ATTN), 1) // HEAD_DIM
    dmask = hrow == hlane
    grow = lax.broadcasted_iota(jnp.int32, (nh, D_KV), 0) // GQA_GROUP
    glane = lax.broadcasted_iota(jnp.int32, (nh, D_KV), 1) // HEAD_DIM
    gmask = grow == glane
    low = glane == 0
    key_idx = lax.broadcasted_iota(jnp.int32, (nh, WINDOW), 1)
    last = lax.broadcasted_iota(jnp.int32, (D_KV, WINDOW), 1) == WINDOW - 1
    sink = sink_ref[...]
    kvn = kvn_ref[...]
    col = col_ref[0]
    k_new = [kvn[j:j + 1, 0:D_KV] for j in seqs]
    v_new = [kvn[j:j + 1, D_KV:] for j in seqs]
    ck = [ck_ref[j] for j in seqs]
    cv = [cv_ref[j] for j in seqs]
    for j in seqs:
        kout_ref[j] = jnp.where(last, col[0:D_KV, j:j + 1], pltpu.roll(ck[j], WINDOW - 1, 1))
        vout_ref[j] = jnp.where(last, col[D_KV:, j:j + 1], pltpu.roll(cv[j], WINDOW - 1, 1))
    q8 = [qr_ref[j * nh:(j + 1) * nh, :] for j in seqs]
    qp = [jnp.where(gmask, jnp.concatenate([q8[j], q8[j]], axis=1), 0.0) for j in seqs]
    s_c = [jnp.where(key_idx >= 1, jnp.dot(qp[j], ck[j], preferred_element_type=F32) * ATTN_SCALE, NEG_INF)
           for j in seqs]
    s_n = [jnp.sum(qp[j] * k_new[j], axis=-1, keepdims=True) * ATTN_SCALE for j in seqs]
    m = [jnp.maximum(jnp.maximum(jnp.max(s_c[j], axis=-1, keepdims=True), s_n[j]), sink) for j in seqs]
    e_c = [jnp.exp(s_c[j] - m[j]) for j in seqs]
    e_n = [jnp.exp(s_n[j] - m[j]) for j in seqs]
    denom = [jnp.sum(e_c[j], axis=-1, keepdims=True) + e_n[j] + jnp.exp(sink - m[j]) for j in seqs]
    o = [(_dot_nt_f32(e_c[j], cv[j]) + e_n[j] * v_new[j]) / denom[j]
         for j in seqs]
    out_rows = []
    for j in seqs:
        rot = pltpu.roll(o[j], HEAD_DIM, 1)
        g0 = jnp.where(low, o[j], rot)
        g1 = jnp.where(low, rot, o[j])
        wide = jnp.concatenate([g0, g0, g1, g1], axis=1)
        out_rows.append(jnp.sum(jnp.where(dmask, wide, 0.0), axis=0, keepdims=True))
    ya_ref[...] = jnp.concatenate(out_rows, axis=0)


def _decode_attn(q_r, kvn, cols, ck_t, cv_t, sinks_col):
    n = kvn.shape[0]
    bt = DEC_TILE
    const = lambda shape: pl.BlockSpec(shape, lambda i: (0,) * len(shape))
    return pl.pallas_call(
        _decode_attn_kernel,
        grid=(n // bt,),
        in_specs=[
            pl.BlockSpec((bt * N_Q_HEADS, HEAD_DIM), lambda i: (i, 0)),
            pl.BlockSpec((bt, 2 * D_KV), lambda i: (i, 0)),
            pl.BlockSpec((1, 2 * D_KV, bt), lambda i: (i, 0, 0)),
            pl.BlockSpec((bt, D_KV, WINDOW), lambda i: (i, 0, 0)),
            pl.BlockSpec((bt, D_KV, WINDOW), lambda i: (i, 0, 0)),
            const((N_Q_HEADS, 1)),
        ],
        out_specs=[
            pl.BlockSpec((bt, D_ATTN), lambda i: (i, 0)),
            pl.BlockSpec((bt, D_KV, WINDOW), lambda i: (i, 0, 0)),
            pl.BlockSpec((bt, D_KV, WINDOW), lambda i: (i, 0, 0)),
        ],
        out_shape=[
            jax.ShapeDtypeStruct((n, D_ATTN), F32),
            jax.ShapeDtypeStruct((n, D_KV, WINDOW), F32),
            jax.ShapeDtypeStruct((n, D_KV, WINDOW), F32),
        ],
        compiler_params=pltpu.CompilerParams(
            dimension_semantics=("arbitrary",), vmem_limit_bytes=VMEM_LIMIT),
        name="decode_attn",
    )(q_r, kvn, cols, ck_t, cv_t, sinks_col)


def kernel(x_prompt, x_sample, state_wkv, state_shift, cache_k_win, cache_v_win, norm_mix_w, w_in, mu_shift, w0, w_decay_up, a0, w_a_up, w_g_up, k_k, k_a, r_k, ln_x_w, ln_x_b, q_norm_w, k_norm_w, sinks, w_out, norm_ffn_w, w_ffn_up, w_ffn_down):
    bsz, t, _ = x_prompt.shape
    nd = x_sample.shape[0]
    l = 0

    w_in_pad = jnp.swapaxes(w_in[l], 0, 1).astype(BF16)
    mu_pad = mu_shift[l][None, :]
    wl = jnp.zeros((D_LORA, 3 * D_RWKV), F32)
    wl = wl.at[0:32, 0:D_RWKV].set(w_decay_up[l])
    wl = wl.at[32:64, D_RWKV:2 * D_RWKV].set(w_a_up[l])
    wl = wl.at[64:160, 2 * D_RWKV:].set(w_g_up[l])
    wl = wl.astype(BF16)
    prm = jnp.zeros((16, D_RWKV), F32)
    prm = prm.at[0].set(w0[l]).at[1].set(a0[l]).at[2].set(k_k[l]).at[3].set(k_a[l])
    prm = prm.at[4].set(r_k[l].reshape(-1)).at[5].set(ln_x_w[l]).at[6].set(ln_x_b[l])
    hid = np.arange(QUAD) // HEAD_DIM
    bmask = jnp.asarray(hid[:, None] == hid[None, :], BF16)
    tri = jnp.asarray(np.tril(np.ones((CHUNK, CHUNK))), BF16)
    qw = jnp.tile(q_norm_w[l][None, :], (1, N_Q_HEADS))
    kw = jnp.tile(k_norm_w[l][None, :], (1, N_KV_HEADS))
    nmw = norm_mix_w[l][None, :]
    nfw = norm_ffn_w[l][None, :]
    wo = w_out[l].astype(BF16)
    wu = w_ffn_up[l].astype(BF16)
    wd = w_ffn_down[l].astype(BF16)
    tab_a, tab_b = _rope_block_tables(max(t, PAST_LEN + 1) // WINDOW + 1)
    ta, tb = tab_a[PAST_LEN // WINDOW], tab_b[PAST_LEN % WINDOW][None, :]
    tab_s = jnp.concatenate([ta[:, :128] * tb[:, :128] - ta[:, 128:] * tb[:, 128:],
                             ta[:, 128:] * tb[:, :128] + ta[:, :128] * tb[:, 128:]], axis=1)

    xs = x_sample.reshape(nd, D_MODEL)
    p_s, q_s, kv_s = _inproj(xs, nmw, w_in_pad, 128)
    shift_in = state_shift[l].reshape(nd, D_SHIFT)
    vec_t, gb, qn_s, kvn_s = _decode_prep(p_s, shift_in, q_s, kv_s, mu_pad, prm, wl, bmask, tab_s, qw, kw)
    s_t = jnp.transpose(state_wkv[l], (1, 2, 3, 0))
    ck_t = jnp.swapaxes(cache_k_win[l].reshape(nd, WINDOW, D_KV), 1, 2)
    cv_t = jnp.swapaxes(cache_v_win[l].reshape(nd, WINDOW, D_KV), 1, 2)
    q_r = qn_s.reshape(nd * N_Q_HEADS, HEAD_DIM)
    cols = jnp.swapaxes(kvn_s.reshape(nd // DEC_TILE, DEC_TILE, 2 * D_KV), 1, 2)
    wkv_t, yr_s = _decode_state(vec_t, gb, prm, bmask, s_t)
    ya_s, kc_t, vc_t = _decode_attn(q_r, kvn_s, cols, ck_t, cv_t, sinks[l][:, None])

    xp = x_prompt.reshape(bsz * t, D_MODEL)
    xs_p, plast, q_p, kv_p = _inproj_shift(xp, nmw, w_in_pad, mu_pad, t, 1024)
    yr_p, hbd = _rwkv_prompt(xs_p.reshape(bsz, t, D_SHIFT), prm, wl, tri, bmask)
    y_prompt, y_s, kwin_p, vwin_p = _attn_ffn(q_p.reshape(bsz, t, D_ATTN), kv_p.reshape(bsz, t, 2 * D_KV),
                                              tab_a, tab_b, qw, kw, sinks[l][None, :], bmask, _band_bias(),
                                              x_prompt, yr_p, xs, yr_s, ya_s, wo, nfw, wu, wd)
    hb = hbd.reshape(bsz, 2, 4, HEAD_DIM, 2, HEAD_DIM)
    wkv_prompt = jnp.stack([hb[:, :, j, :, j % 2, :] for j in range(4)], axis=2)
    wkv_prompt = wkv_prompt.reshape(bsz, H_RWKV, HEAD_DIM, HEAD_DIM)[None]
    shift_prompt = plast[:, 0:1, :][None]
    k_win_prompt = jnp.swapaxes(kwin_p, 1, 2).reshape(bsz, WINDOW, N_KV_HEADS, HEAD_DIM)[None]
    v_win_prompt = jnp.swapaxes(vwin_p, 1, 2).reshape(bsz, WINDOW, N_KV_HEADS, HEAD_DIM)[None]

    y_sample = y_s.reshape(nd, 1, D_MODEL)
    wkv_sample = jnp.transpose(wkv_t, (3, 0, 1, 2))[None]
    shift_sample = p_s.reshape(nd, 1, D_SHIFT)[None]
    k_win_sample = jnp.swapaxes(kc_t, 1, 2).reshape(nd, WINDOW, N_KV_HEADS, HEAD_DIM)[None]
    v_win_sample = jnp.swapaxes(vc_t, 1, 2).reshape(nd, WINDOW, N_KV_HEADS, HEAD_DIM)[None]

    return (y_prompt, y_sample, wkv_prompt, shift_prompt, k_win_prompt, v_win_prompt,
            wkv_sample, shift_sample, k_win_sample, v_win_sample)
```

```python
import functools

import jax
import jax.numpy as jnp
import numpy as np
from jax import lax
from jax.experimental import pallas as pl
from jax.experimental.pallas import tpu as pltpu

F32 = jnp.float32
BF16 = jnp.bfloat16

D_MODEL = 1024
D_RWKV = 512
D_ATTN = 512
HEAD_DIM = 64
H_RWKV = 8
N_Q_HEADS = 8
N_KV_HEADS = 2
GQA_GROUP = 4
D_KV = 128
D_LORA = 160
D_LORA_PAD = 256
D_SHIFT = 3 * D_RWKV + D_LORA
D_IN = D_SHIFT + D_ATTN + 2 * D_KV
WINDOW = 128
ROPE_DIM = 16
ROPE_HALF = 8
ROPE_THETA = 500000.0
ATTN_SCALE = HEAD_DIM ** -0.5
D_FF = 4096
RMS_EPS = 1e-6
LNX_EPS = 64e-5
NEG_INF = -1e30
LOG2E = 1.4426950408889634
PAST_LEN = 16384

CHUNK = 64
QUAD = 4 * HEAD_DIM
V7X_VMEM_BYTES = 64 * 1024 * 1024
VMEM_LIMIT = V7X_VMEM_BYTES // 8 * 7


def _split2(x):
    hi = x.astype(BF16)
    lo = (x - hi.astype(F32)).astype(BF16)
    return hi, lo


def _head_sums(xs, ones_ref):
    n, w = xs[0].shape
    tile = min(w, QUAD)
    per = w // tile
    pieces = [x[:, j * tile:(j + 1) * tile] for x in xs for j in range(per)]
    stacked = jnp.concatenate(pieces, axis=0) if len(pieces) > 1 else pieces[0]
    ones = ones_ref[0:tile, 0:tile]
    out = jnp.dot(stacked.astype(BF16), ones, preferred_element_type=F32)
    res = []
    for i in range(len(xs)):
        cols = [out[(i * per + j) * n:(i * per + j + 1) * n] for j in range(per)]
        res.append(jnp.concatenate(cols, axis=1) if per > 1 else cols[0])
    return res


def _cumsum_rows(tri_bf16, x):
    hi, lo = _split2(x)
    return (jnp.dot(tri_bf16, hi, preferred_element_type=F32)
            + jnp.dot(tri_bf16, lo, preferred_element_type=F32))


def _mm(a, b):
    return jnp.dot(a.astype(BF16), b.astype(BF16), preferred_element_type=F32)


def _mm_nt(a, b):
    return lax.dot_general(a.astype(BF16), b.astype(BF16), (((1,), (1,)), ((), ())),
                           preferred_element_type=F32)


def _mm_tn(a, b):
    return lax.dot_general(a.astype(BF16), b.astype(BF16), (((0,), (0,)), ((), ())),
                           preferred_element_type=F32)


def _dot_nt_f32(a, b):
    return lax.dot_general(a, b, (((1,), (1,)), ((), ())), preferred_element_type=F32)


def _sigmoid(x):
    return 1.0 / (1.0 + jnp.exp(-x))


def _interleave(*gens):
    live = list(gens)
    while live:
        for g in list(live):
            try:
                next(g)
            except StopIteration:
                live.remove(g)


def _norm_project(x, nw_ref, wt_ref):
    ms = jnp.mean(x * x, axis=-1, keepdims=True)
    h = ((x * lax.rsqrt(ms + RMS_EPS)) * nw_ref[...]).astype(BF16)
    return _mm_nt(h, wt_ref[0:D_SHIFT, :]), _mm_nt(h, wt_ref[D_SHIFT:, :])


def _inproj_kernel(x_ref, nw_ref, w_ref, p_ref, q_ref, kv_ref):
    p, qkv = _norm_project(x_ref[...], nw_ref, w_ref)
    p_ref[...] = p
    q_ref[...] = qkv[:, :D_ATTN]
    kv_ref[...] = qkv[:, D_ATTN:]


def _inproj_shift_kernel(tiles_per_seq, x_ref, nw_ref, w_ref, mu_ref, xs_ref, last_ref, q_ref, kv_ref, prev_ref):
    i = pl.program_id(0)

    @pl.when(i % tiles_per_seq == 0)
    def _():
        prev_ref[...] = jnp.zeros_like(prev_ref)

    p, qkv = _norm_project(x_ref[...], nw_ref, w_ref)
    tm = p.shape[0]
    row = lax.broadcasted_iota(jnp.int32, p.shape, 0)
    prev = jnp.where(row == 0, jnp.broadcast_to(prev_ref[0:1, :], p.shape), pltpu.roll(p, 1, 0))
    xs_ref[...] = p + (prev - p) * mu_ref[...]
    last = jnp.broadcast_to(p[tm - 1:tm, :], prev_ref.shape)
    prev_ref[...] = last
    last_ref[0] = last
    q_ref[...] = qkv[:, :D_ATTN]
    kv_ref[...] = qkv[:, D_ATTN:]


def _inproj_shift(x2d, norm_w, w_in_pad, mu_pad, seq_len, tm):
    m = x2d.shape[0]
    tiles_per_seq = seq_len // tm
    return pl.pallas_call(
        functools.partial(_inproj_shift_kernel, tiles_per_seq),
        grid=(m // tm,),
        in_specs=[
            pl.BlockSpec((tm, D_MODEL), lambda i: (i, 0)),
            pl.BlockSpec((1, D_MODEL), lambda i: (0, 0)),
            pl.BlockSpec((D_IN, D_MODEL), lambda i: (0, 0)),
            pl.BlockSpec((1, D_SHIFT), lambda i: (0, 0)),
        ],
        out_specs=[
            pl.BlockSpec((tm, D_SHIFT), lambda i: (i, 0)),
            pl.BlockSpec((1, 8, D_SHIFT), lambda i: (i // tiles_per_seq, 0, 0)),
            pl.BlockSpec((tm, D_ATTN), lambda i: (i, 0)),
            pl.BlockSpec((tm, 2 * D_KV), lambda i: (i, 0)),
        ],
        out_shape=[
            jax.ShapeDtypeStruct((m, D_SHIFT), F32),
            jax.ShapeDtypeStruct((m // seq_len, 8, D_SHIFT), F32),
            jax.ShapeDtypeStruct((m, D_ATTN), F32),
            jax.ShapeDtypeStruct((m, 2 * D_KV), F32),
        ],
        scratch_shapes=[pltpu.VMEM((8, D_SHIFT), F32)],
        compiler_params=pltpu.CompilerParams(
            dimension_semantics=("arbitrary",), vmem_limit_bytes=VMEM_LIMIT),
        name="inproj_shift",
    )(x2d, norm_w, w_in_pad, mu_pad)


def _inproj(x2d, norm_w, w_in_pad, tm):
    m = x2d.shape[0]
    return pl.pallas_call(
        _inproj_kernel,
        grid=(m // tm,),
        in_specs=[
            pl.BlockSpec((tm, D_MODEL), lambda i: (i, 0)),
            pl.BlockSpec((1, D_MODEL), lambda i: (0, 0)),
            pl.BlockSpec((D_IN, D_MODEL), lambda i: (0, 0)),
        ],
        out_specs=[
            pl.BlockSpec((tm, D_SHIFT), lambda i: (i, 0)),
            pl.BlockSpec((tm, D_ATTN), lambda i: (i, 0)),
            pl.BlockSpec((tm, 2 * D_KV), lambda i: (i, 0)),
        ],
        out_shape=[
            jax.ShapeDtypeStruct((m, D_SHIFT), F32),
            jax.ShapeDtypeStruct((m, D_ATTN), F32),
            jax.ShapeDtypeStruct((m, 2 * D_KV), F32),
        ],
        compiler_params=pltpu.CompilerParams(
            dimension_semantics=("arbitrary",), vmem_limit_bytes=VMEM_LIMIT),
        name="inproj",
    )(x2d, norm_w, w_in_pad)


def _rwkv_features(xs, prm, wl_ref, ones_ref):
    r = xs[:, 0:D_RWKV]
    k = xs[:, D_RWKV:2 * D_RWKV]
    v = xs[:, 2 * D_RWKV:3 * D_RWKV]
    lora = xs[:, 3 * D_RWKV:]
    lora = jnp.concatenate([lora, jnp.zeros((lora.shape[0], wl_ref.shape[0] - D_LORA), F32)], axis=1)
    col = lax.broadcasted_iota(jnp.int32, lora.shape, 1)
    act = jnp.where(col < 32, jnp.tanh(lora), jnp.where(col < 64, lora, _sigmoid(lora)))
    up = jnp.dot(act.astype(BF16), wl_ref[...], preferred_element_type=F32)
    w0, a0, k_k, k_a, r_k = prm[0:1], prm[1:2], prm[2:3], prm[3:4], prm[4:5]
    logw = (-np.exp(-0.5)) * _sigmoid(w0 + up[:, 0:D_RWKV])
    asig = _sigmoid(a0 + up[:, D_RWKV:2 * D_RWKV])
    g = up[:, 2 * D_RWKV:]
    kk = k * k_k
    k_mod = k * (1.0 + (asig - 1.0) * k_a)
    ss, rk = _head_sums([kk * kk, r * k_mod * r_k], ones_ref)
    kk = kk / jnp.maximum(jnp.sqrt(ss), 1e-12)
    k = k_mod
    bonus = rk * v
    return r, logw, k, v, -kk, kk * asig, g, bonus


def _rwkv_finish(y, g, bonus, prm, ones_ref):
    ln_w, ln_b = prm[5:6], prm[6:7]
    mean = _head_sums([y], ones_ref)[0] * (1.0 / HEAD_DIM)
    d = y - mean
    var = _head_sums([d * d], ones_ref)[0] * (1.0 / HEAD_DIM)
    yn = d * lax.rsqrt(var + LNX_EPS) * ln_w + ln_b
    return (yn + bonus) * g


def _block_diag(x, bmask):
    return jnp.concatenate([x] * 4, axis=0) * bmask


def _chunk_prep(r, logw, k, v, a, b, tri_ref, bmask):
    n = len(r)
    ch = range(n)
    tri = tri_ref[...]
    cum = [_cumsum_rows(tri, logw[i]) for i in ch]
    yield
    e_in = [jnp.exp(cum[i]) for i in ch]
    e_ex = [jnp.exp(cum[i] - logw[i]) for i in ch]
    e_inv = [1.0 / e_in[i] for i in ch]
    e_last = [e_in[i][CHUNK - 1:CHUNK, :] for i in ch]
    rt = [(r[i] * e_in[i]).astype(BF16) for i in ch]
    at = [(a[i] * e_ex[i]).astype(BF16) for i in ch]
    kt = [(k[i] * e_inv[i]).astype(BF16) for i in ch]
    bt = [(b[i] * e_inv[i]).astype(BF16) for i in ch]
    vb = [v[i].astype(BF16) for i in ch]
    yield

    t_idx = lax.broadcasted_iota(jnp.int32, (CHUNK, QUAD), 0)
    s_idx = lax.broadcasted_iota(jnp.int32, (CHUNK, QUAD), 1) & (HEAD_DIM - 1)
    strict = s_idx < t_idx
    incl = s_idx <= t_idx

    gm = [_mm_nt(jnp.concatenate([at[i], rt[i]], axis=0),
                 jnp.concatenate([_block_diag(bt[i], bmask), _block_diag(kt[i], bmask)], axis=0))
          for i in ch]
    yield
    a_ab = [jnp.where(strict, gm[i][:CHUNK, :QUAD], 0.0) for i in ch]
    a_ak = [jnp.where(strict, gm[i][:CHUNK, QUAD:], 0.0) for i in ch]
    a_rb = [jnp.where(incl, gm[i][CHUNK:, :QUAD], 0.0) for i in ch]
    a_rk = [jnp.where(incl, gm[i][CHUNK:, QUAD:], 0.0) for i in ch]

    eye = jnp.where(s_idx == t_idx, 1.0, 0.0)
    pwb = [a_ab[i].astype(BF16) for i in ch]
    t_inv = [eye + a_ab[i] for i in ch]
    for it in range(6):
        rbd = [_block_diag(pwb[i], bmask) for i in ch]
        if it == 0:
            pwb = [_mm(pwb[i], rbd[i]).astype(BF16) for i in ch]
        elif it < 5:
            out = [_mm(jnp.concatenate([pwb[i], t_inv[i].astype(BF16)], axis=0), rbd[i]) for i in ch]
            pwb = [out[i][:CHUNK].astype(BF16) for i in ch]
            t_inv = [t_inv[i] + out[i][CHUNK:] for i in ch]
        else:
            t_inv = [t_inv[i] + _mm(t_inv[i], rbd[i]) for i in ch]
        yield

    vbd = [_block_diag(vb[i], bmask) for i in ch]
    xy0 = [_mm(jnp.concatenate([a_ak[i], a_rk[i]], axis=0), vbd[i]) for i in ch]
    return [dict(ar=jnp.concatenate([at[i], rt[i]], axis=0), x0=xy0[i][:CHUNK], y0=xy0[i][CHUNK:],
                 t_inv=t_inv[i].astype(BF16), a_rb=a_rb[i].astype(BF16), vb=vb[i],
                 bk=jnp.concatenate([bt[i], kt[i]], axis=0), e_last=e_last[i]) for i in ch]


def _chunk_step(pre, state, bmask, out):
    ch = range(len(pre))
    half = QUAD // 2
    zeros = jnp.zeros((half, half), BF16)
    sc = [state[i].astype(BF16) for i in ch]
    sb = [jnp.concatenate([jnp.concatenate([sc[i][:half], zeros], axis=1),
                           jnp.concatenate([zeros, sc[i][half:]], axis=1)], axis=0) for i in ch]
    xr = [_mm_nt(pre[i]["ar"], sb[i]) for i in ch]
    yield
    x = [xr[i][:CHUNK] + pre[i]["x0"] for i in ch]
    u = [_mm(pre[i]["t_inv"], _block_diag(x[i].astype(BF16), bmask)) for i in ch]
    yield
    ub = [u[i].astype(BF16) for i in ch]
    out["y"] = [xr[i][CHUNK:] + pre[i]["y0"] + _mm(pre[i]["a_rb"], _block_diag(ub[i], bmask)) for i in ch]
    upd = [_mm_tn(jnp.concatenate([ub[i], pre[i]["vb"]], axis=0), pre[i]["bk"]) for i in ch]
    yield
    bm = bmask[:half, :half].astype(F32)
    s_new = []
    for i in ch:
        e_last = pre[i]["e_last"]
        top = (state[i][:half] + upd[i][:half, :half] * bm) * e_last[:, :half]
        bot = (state[i][half:] + upd[i][half:, half:] * bm) * e_last[:, half:]
        s_new.append(jnp.concatenate([top, bot], axis=0))
    out["state"] = s_new


def _finish_stages(ys, g, bonus, prm, ones_ref, y_ref, j):
    nseq = len(ys) // 2
    ln_w, ln_b = prm[5:6], prm[6:7]
    y = jnp.concatenate([jnp.concatenate(ys[2 * s:2 * s + 2], axis=1) for s in range(nseq)], axis=0)
    mean = _head_sums([y], ones_ref)[0] * (1.0 / HEAD_DIM)
    yield
    d = y - mean
    var = _head_sums([d * d], ones_ref)[0] * (1.0 / HEAD_DIM)
    yield
    out = (d * lax.rsqrt(var + LNX_EPS) * ln_w + ln_b + bonus) * g
    for s in range(nseq):
        y_ref[s, j * CHUNK:(j + 1) * CHUNK, :] = out[s * CHUNK:(s + 1) * CHUNK]


def _rwkv_prompt_kernel(xs_ref, prm_ref, wl_ref, tri_ref, bmask_ref, y_ref, hout_ref, h_ref):
    c = pl.program_id(0)
    nseq, tstep, _ = xs_ref.shape
    nchunk = tstep // CHUNK

    @pl.when(c == 0)
    def _():
        h_ref[...] = jnp.zeros_like(h_ref)

    xs = jnp.concatenate([xs_ref[s] for s in range(nseq)], axis=0)
    prm = prm_ref[...]
    r, logw, k, v, a, b, g, bonus = _rwkv_features(xs, prm, wl_ref, bmask_ref)
    bmask = bmask_ref[...]
    lanes = [(s, q) for s in range(nseq) for q in range(2)]
    pre = {}

    def prep(chunks):
        chains = [(j, s, q) for j in chunks for s, q in lanes]
        cut = lambda x: [x[s * tstep + j * CHUNK:s * tstep + (j + 1) * CHUNK, q * QUAD:(q + 1) * QUAD]
                         for j, s, q in chains]
        res = yield from _chunk_prep(cut(r), cut(logw), cut(k), cut(v), cut(a), cut(b), tri_ref, bmask)
        for n, j in enumerate(chunks):
            pre[j] = res[n * len(lanes):(n + 1) * len(lanes)]

    nfirst = max(nchunk - 1, 1)
    _interleave(prep(range(nfirst)))
    later = prep(range(nfirst, nchunk))
    state = [h_ref[s, q] for s, q in lanes]
    crow = lambda x, j: jnp.concatenate([x[s * tstep + j * CHUNK:s * tstep + (j + 1) * CHUNK]
                                         for s in range(nseq)], axis=0)
    finish = iter(())
    for j in range(nchunk):
        res = {}
        for _ in _chunk_step(pre[j], state, bmask, res):
            next(later, None)
            next(finish, None)
        _interleave(finish)
        if j == nfirst - 1:
            _interleave(later)
        state = res["state"]
        finish = _finish_stages(res["y"], crow(g, j), crow(bonus, j), prm, bmask_ref, y_ref, j)
    _interleave(finish)
    for i, (s, q) in enumerate(lanes):
        h_ref[s, q] = state[i]

    @pl.when(c == pl.num_programs(0) - 1)
    def _():
        hout_ref[...] = h_ref[...]


RWKV_CHUNKS_PER_STEP = 4


def _rwkv_prompt(xs3d, prm, wl, tri, bmask):
    bsz, t, _ = xs3d.shape
    tstep = RWKV_CHUNKS_PER_STEP * CHUNK
    nc = t // tstep
    const = lambda shape: pl.BlockSpec(shape, lambda c: (0,) * len(shape))
    state_shape = (bsz, 2, QUAD, QUAD // 2)
    return pl.pallas_call(
        _rwkv_prompt_kernel,
        grid=(nc,),
        in_specs=[
            pl.BlockSpec((bsz, tstep, D_SHIFT), lambda c: (0, c, 0)),
            const((16, D_RWKV)),
            const((D_LORA_PAD, 3 * D_RWKV)),
            const((CHUNK, CHUNK)),
            const((QUAD, QUAD)),
        ],
        out_specs=[
            pl.BlockSpec((bsz, tstep, D_RWKV), lambda c: (0, c, 0)),
            const(state_shape),
        ],
        out_shape=[
            jax.ShapeDtypeStruct((bsz, t, D_RWKV), F32),
            jax.ShapeDtypeStruct(state_shape, F32),
        ],
        scratch_shapes=[pltpu.VMEM(state_shape, F32)],
        compiler_params=pltpu.CompilerParams(
            dimension_semantics=("arbitrary",), vmem_limit_bytes=VMEM_LIMIT),
        name="rwkv_prompt",
    )(xs3d, prm, wl, tri, bmask)


def _rope_lane_freq():
    inv_freq = jnp.power(ROPE_THETA, -jnp.arange(ROPE_HALF, dtype=F32) * (2.0 / ROPE_DIM))
    return inv_freq[(np.arange(2 * HEAD_DIM) % HEAD_DIM) % ROPE_HALF][None, :]


def _rope_tables(cos, sin):
    dim = lax.broadcasted_iota(jnp.int32, cos.shape, 1) & (HEAD_DIM - 1)
    cos_t = jnp.where(dim < ROPE_DIM, cos, 1.0)
    sin_lo = jnp.where(dim < ROPE_HALF, -sin, 0.0)
    sin_hi = jnp.where((dim >= ROPE_HALF) & (dim < ROPE_DIM), sin, 0.0)
    return cos_t, sin_lo, sin_hi


def _qk_norm_rope(x, norm_w, cos_t, sin_lo, sin_hi, ones_ref):
    ms = _head_sums([x * x], ones_ref)[0] * (1.0 / HEAD_DIM)
    xn = x * lax.rsqrt(ms + RMS_EPS) * norm_w
    width = x.shape[1]
    fwd = pltpu.roll(xn, width - ROPE_HALF, 1)
    bwd = pltpu.roll(xn, ROPE_HALF, 1)
    return xn * cos_t + fwd * sin_lo + bwd * sin_hi


def _tile_lanes(x, reps):
    return jnp.concatenate([x] * reps, axis=1) if reps > 1 else x


def _attn_stages(q_ref, kv_ref, taba_ref, tabb_ref, qw_ref, kw_ref, sink_ref, ones_ref, bias_ref, first_bias,
                 kprev_ref, vprev_ref, out):
    nseq, tq, _ = q_ref.shape
    nblk = tq // WINDOW
    units = [(s, b) for s in range(nseq) for b in range(nblk)]
    blk = lambda b: slice(b * WINDOW, (b + 1) * WINDOW)
    tb = tabb_ref[...]
    cos_b, sin_b = tb[:, :128], tb[:, 128:]
    rope, rope4 = [], []
    for b in range(nblk):
        ta = taba_ref[b][0:1, :]
        cos_a, sin_a = ta[:, :128], ta[:, 128:]
        tabs = _rope_tables(cos_a * cos_b - sin_a * sin_b, sin_a * cos_b + cos_a * sin_b)
        rope.append(tabs)
        rope4.append([_tile_lanes(x, 4) for x in tabs])
    q = {(s, b): _qk_norm_rope(q_ref[s, blk(b), :], qw_ref[...], *rope4[b], ones_ref) * (ATTN_SCALE * LOG2E)
         for s, b in units}
    kv = {(s, b): kv_ref[s, blk(b), :] for s, b in units}
    k_cur = {u: _qk_norm_rope(kv[u][:, 0:D_KV], kw_ref[...], *rope[u[1]], ones_ref) for u in units}
    v_cur = {u: kv[u][:, D_KV:] for u in units}
    k_all = {(s, b): jnp.concatenate([kprev_ref[s] if b == 0 else k_cur[(s, b - 1)], k_cur[(s, b)]], axis=0)
             for s, b in units}
    v_all = {(s, b): jnp.concatenate([vprev_ref[s] if b == 0 else v_cur[(s, b - 1)], v_cur[(s, b)]], axis=0)
             for s, b in units}
    out["k_cur"] = [k_cur[(s, nblk - 1)] for s in range(nseq)]
    out["v_cur"] = [v_cur[(s, nblk - 1)] for s in range(nseq)]
    for s in range(nseq):
        kprev_ref[s] = out["k_cur"][s]
        vprev_ref[s] = out["v_cur"][s]
    yield

    nk = 2 * WINDOW
    bias = [bias_ref[first_bias] if b == 0 else bias_ref[1] for b in range(nblk)]
    sinks = sink_ref[...] * LOG2E
    low = lax.broadcasted_iota(jnp.int32, (nk, D_KV), 1) < HEAD_DIM
    lane_blk = [ones_ref[j * HEAD_DIM:j * HEAD_DIM + 1, :] for j in range(GQA_GROUP)]

    chains = [(u, g) for u in units for g in range(N_KV_HEADS)]
    ch = range(len(chains))
    k_rot = {u: pltpu.roll(k_all[u], HEAD_DIM, 1) for u in units}
    k2 = [jnp.where(low, k_all[u], k_rot[u]) if g == 0 else jnp.where(low, k_rot[u], k_all[u])
          for u, g in chains]
    k4 = [jnp.concatenate([k2[c], k2[c]], axis=1).astype(BF16) for c in ch]
    vb = {u: v_all[u].astype(BF16) for u in units}
    qg = [q[u][:, g * QUAD:(g + 1) * QUAD].astype(BF16) for u, g in chains]
    qstack = [jnp.concatenate([qg[c] * lane_blk[j] for j in range(GQA_GROUP)], axis=0) for c in ch]
    sink_row = [jnp.concatenate(
        [jnp.broadcast_to(sinks[:, g * GQA_GROUP + j:g * GQA_GROUP + j + 1], (1, WINDOW))
         for j in range(GQA_GROUP)], axis=1) for u, g in chains]
    yield
    sc = [_mm_nt(k4[c], qstack[c]) + bias[u[1]] for c, (u, g) in enumerate(chains)]
    yield
    m = [jnp.maximum(jnp.max(sc[c], axis=0, keepdims=True), sink_row[c]) for c in ch]
    e = [jnp.exp2(sc[c] - m[c]) for c in ch]
    yield
    denom = [jnp.sum(e[c], axis=0, keepdims=True) + jnp.exp2(sink_row[c] - m[c]) for c in ch]
    ot = {(u, g): _mm_tn(vb[u], e[c].astype(BF16))[g * HEAD_DIM:(g + 1) * HEAD_DIM, :] * (1.0 / denom[c])
          for c, (u, g) in enumerate(chains)}
    yield
    ya = []
    for s in range(nseq):
        blocks = []
        for b in range(nblk):
            yt = jnp.concatenate([ot[((s, b), g)][:, j * WINDOW:(j + 1) * WINDOW]
                                  for g in range(N_KV_HEADS) for j in range(GQA_GROUP)], axis=0)
            blocks.append(jnp.transpose(yt))
        ya.append(jnp.concatenate(blocks, axis=0) if nblk > 1 else blocks[0])
    out["ya"] = ya


def _ffn_stages(x, yr, ya, wo_ref, nw_ref, wu_ref, wd_ref, out, pieces=4):
    mix = jnp.concatenate([yr, ya], axis=1).astype(BF16)
    x1 = x + jnp.dot(mix, wo_ref[...], preferred_element_type=F32)
    yield
    ms = jnp.mean(x1 * x1, axis=-1, keepdims=True)
    hf = ((x1 * lax.rsqrt(ms + RMS_EPS)) * nw_ref[...]).astype(BF16)
    acc = x1
    step = D_FF // pieces
    for j in range(pieces):
        up = jnp.dot(hf, wu_ref[:, j * step:(j + 1) * step], preferred_element_type=F32)
        yield
        act = jnp.square(jnp.maximum(up, 0.0)).astype(BF16)
        acc = acc + jnp.dot(act, wd_ref[j * step:(j + 1) * step, :], preferred_element_type=F32)
        yield
    out["y"] = acc


def _attn_ffn_kernel(q_ref, kv_ref, taba_ref, tabb_ref, qw_ref, kw_ref, sink_ref, ones_ref, bias_ref,
                     x_ref, yr_ref, xd_ref, yrd_ref, yad_ref, wo_ref, nw_ref, wu_ref, wd_ref,
                     o_ref, od_ref, kwin_ref, vwin_ref, kprev_ref, vprev_ref, ya_ref):
    i = pl.program_id(0)
    nseq, tq, _ = q_ref.shape
    seqs = range(nseq)
    first = i == 0

    @pl.when(first)
    def _():
        kprev_ref[...] = jnp.zeros_like(kprev_ref)
        vprev_ref[...] = jnp.zeros_like(vprev_ref)
        ya_ref[...] = jnp.zeros_like(ya_ref)

    def rows(ref, dec_ref):
        tile = jnp.concatenate([ref[s] for s in seqs], axis=0)
        dec = dec_ref[...]
        return jnp.where(first, jnp.concatenate([dec] * (tile.shape[0] // dec.shape[0]), axis=0), tile)

    a_out, f_out = {}, {}
    _interleave(
        _ffn_stages(rows(x_ref, xd_ref), rows(yr_ref, yrd_ref), rows(ya_ref, yad_ref),
                    wo_ref, nw_ref, wu_ref, wd_ref, f_out),
        _attn_stages(q_ref, kv_ref, taba_ref, tabb_ref, qw_ref, kw_ref, sink_ref, ones_ref, bias_ref,
                     jnp.minimum(i, 1), kprev_ref, vprev_ref, a_out))
    for s in seqs:
        o_ref[s] = f_out["y"][s * tq:(s + 1) * tq]
        ya_ref[s] = a_out["ya"][s]

    @pl.when(first)
    def _():
        od_ref[...] = f_out["y"][:od_ref.shape[0]]

    @pl.when(i == pl.num_programs(0) - 2)
    def _():
        for s in seqs:
            kwin_ref[s] = jnp.transpose(a_out["k_cur"][s])
            vwin_ref[s] = jnp.transpose(a_out["v_cur"][s])


def _band_bias():
    ki = np.arange(2 * WINDOW)[:, None]
    qi = (np.arange(GQA_GROUP * WINDOW) % WINDOW + WINDOW)[None, :]
    dq = qi - ki
    band = (dq >= 0) & (dq < WINDOW)
    first = band & (ki >= WINDOW)
    return jnp.asarray(np.where(np.stack([first, band]), 0.0, NEG_INF), F32)


def _rope_block_tables(nb):
    freq = _rope_lane_freq()
    ang_a = (jnp.arange(nb, dtype=F32) * WINDOW)[:, None] * freq
    ang_b = jnp.arange(WINDOW, dtype=F32)[:, None] * freq
    tab_a = jnp.concatenate([jnp.cos(ang_a), jnp.sin(ang_a)], axis=1)
    tab_b = jnp.concatenate([jnp.cos(ang_b), jnp.sin(ang_b)], axis=1)
    return jnp.broadcast_to(tab_a[:, None, :], (nb, 8, 4 * HEAD_DIM)), tab_b


ATTN_BLOCKS_PER_STEP = 1


def _attn_ffn(q3d, kv3d, tab_a, tab_b, qw, kw, sinks, ones_bd, bias, x3d, yr3d, xd, yrd, yad, wo, nw, wu, wd):
    bsz, t, _ = q3d.shape
    nd = xd.shape[0]
    nblk = ATTN_BLOCKS_PER_STEP
    tq = nblk * WINDOW
    nt = t // tq
    const = lambda shape: pl.BlockSpec(shape, lambda i: (0,) * len(shape))
    single = lambda shape: pl.BlockSpec(shape, lambda i: (0,) * len(shape), pipeline_mode=pl.Buffered(1))
    cur = lambda i: jnp.minimum(i, nt - 1)
    prv = lambda i: jnp.maximum(i - 1, 0)
    return pl.pallas_call(
        _attn_ffn_kernel,
        grid=(nt + 1,),
        in_specs=[
            pl.BlockSpec((bsz, tq, D_ATTN), lambda i: (0, cur(i), 0)),
            pl.BlockSpec((bsz, tq, 2 * D_KV), lambda i: (0, cur(i), 0)),
            pl.BlockSpec((nblk, 8, 4 * HEAD_DIM), lambda i: (cur(i), 0, 0)),
            const((WINDOW, 4 * HEAD_DIM)),
            const((1, D_ATTN)),
            const((1, D_KV)),
            const((1, N_Q_HEADS)),
            const((QUAD, QUAD)),
            const((2, 2 * WINDOW, GQA_GROUP * WINDOW)),
            pl.BlockSpec((bsz, tq, D_MODEL), lambda i: (0, prv(i), 0)),
            pl.BlockSpec((bsz, tq, D_RWKV), lambda i: (0, prv(i), 0)),
            const((nd, D_MODEL)),
            const((nd, D_RWKV)),
            const((nd, D_ATTN)),
            single((D_MODEL, D_MODEL)),
            const((1, D_MODEL)),
            single((D_MODEL, D_FF)),
            single((D_FF, D_MODEL)),
        ],
        out_specs=[
            pl.BlockSpec((bsz, tq, D_MODEL), lambda i: (0, prv(i), 0)),
            const((nd, D_MODEL)),
            const((bsz, WINDOW, D_KV)),
            const((bsz, WINDOW, D_KV)),
        ],
        out_shape=[
            jax.ShapeDtypeStruct((bsz, t, D_MODEL), F32),
            jax.ShapeDtypeStruct((nd, D_MODEL), F32),
            jax.ShapeDtypeStruct((bsz, WINDOW, D_KV), F32),
            jax.ShapeDtypeStruct((bsz, WINDOW, D_KV), F32),
        ],
        scratch_shapes=[
            pltpu.VMEM((bsz, WINDOW, D_KV), F32),
            pltpu.VMEM((bsz, WINDOW, D_KV), F32),
            pltpu.VMEM((bsz, tq, D_ATTN), F32),
        ],
        compiler_params=pltpu.CompilerParams(
            dimension_semantics=("arbitrary",), vmem_limit_bytes=VMEM_LIMIT),
        name="attn_ffn",
    )(q3d, kv3d, tab_a, tab_b, qw, kw, sinks, ones_bd, bias, x3d, yr3d, xd, yrd, yad, wo, nw, wu, wd)


DEC_TILE = 16


def _decode_prep_kernel(p_ref, sh_ref, q_ref, kv_ref, mu_ref, prm_ref, wl_ref, ones_ref, tab_ref,
                        qw_ref, kw_ref, vec_ref, vgb_ref, qn_ref, kvn_ref):
    p = p_ref[...]
    xs = p + (sh_ref[...] - p) * mu_ref[...]
    r, logw, k, v, a, b, g, bonus = _rwkv_features(xs, prm_ref[...], wl_ref, ones_ref)
    for i, x in enumerate((a, b, k, jnp.exp(logw), r, v)):
        vec_ref[i] = jnp.transpose(x)
    vgb_ref[0] = g
    vgb_ref[1] = bonus
    n = p.shape[0]
    tab = jnp.broadcast_to(tab_ref[0:1, :], (n, 4 * HEAD_DIM))
    cos_t, sin_lo, sin_hi = _rope_tables(tab[:, :128], tab[:, 128:])
    qn_ref[...] = _qk_norm_rope(q_ref[...], qw_ref[...], _tile_lanes(cos_t, 4), _tile_lanes(sin_lo, 4),
                                _tile_lanes(sin_hi, 4), ones_ref)
    kv = kv_ref[...]
    kvn_ref[:, 0:D_KV] = _qk_norm_rope(kv[:, 0:D_KV], kw_ref[...], cos_t, sin_lo, sin_hi, ones_ref)
    kvn_ref[:, D_KV:] = kv[:, D_KV:]


def _decode_prep(p, shift, q, kv, mu_pad, prm, wl, bmask, tab, qw, kw):
    n = p.shape[0]
    full = lambda shape: pl.BlockSpec(shape, lambda i: (0,) * len(shape))
    return pl.pallas_call(
        _decode_prep_kernel,
        grid=(1,),
        in_specs=[full((n, D_SHIFT)), full((n, D_SHIFT)), full((n, D_ATTN)), full((n, 2 * D_KV)),
                  full((1, D_SHIFT)), full((16, D_RWKV)), full((D_LORA_PAD, 3 * D_RWKV)),
                  full((QUAD, QUAD)), full((8, 4 * HEAD_DIM)), full((1, D_ATTN)), full((1, D_KV))],
        out_specs=[full((6, D_RWKV, n)), full((2, n, D_RWKV)), full((n, D_ATTN)), full((n, 2 * D_KV))],
        out_shape=[
            jax.ShapeDtypeStruct((6, D_RWKV, n), F32),
            jax.ShapeDtypeStruct((2, n, D_RWKV), F32),
            jax.ShapeDtypeStruct((n, D_ATTN), F32),
            jax.ShapeDtypeStruct((n, 2 * D_KV), F32),
        ],
        compiler_params=pltpu.CompilerParams(
            dimension_semantics=("arbitrary",), vmem_limit_bytes=VMEM_LIMIT),
        name="decode_prep",
    )(p, shift, q, kv, mu_pad, prm, wl, bmask, tab, qw, kw)


def _decode_state_kernel(vec_ref, gb_ref, prm_ref, ones_ref, s_ref, sout_ref, yr_ref, yt_ref):
    h = pl.program_id(0)
    a_t, b_t, k_t, w_t, r_t = (vec_ref[i] for i in range(5))

    def body(i, carry):
        s = s_ref[0, i]
        sa = jnp.sum(s * a_t, axis=0, keepdims=True)
        v_i = vec_ref[5, pl.ds(i, 1), :]
        s_new = s * w_t + sa * b_t + v_i * k_t
        sout_ref[0, i] = s_new
        yt_ref[pl.ds(h * HEAD_DIM + i, 1), :] = jnp.sum(s_new * r_t, axis=0, keepdims=True)
        return carry

    lax.fori_loop(0, HEAD_DIM, body, 0, unroll=8)

    @pl.when(h == pl.num_programs(0) - 1)
    def _():
        y = jnp.transpose(yt_ref[...])
        yr_ref[...] = _rwkv_finish(y, gb_ref[0], gb_ref[1], prm_ref[...], ones_ref)


def _decode_state(vec_t, gb, prm, bmask, s_t):
    n = s_t.shape[-1]
    const = lambda shape: pl.BlockSpec(shape, lambda h: (0,) * len(shape))
    return pl.pallas_call(
        _decode_state_kernel,
        grid=(H_RWKV,),
        in_specs=[
            pl.BlockSpec((6, HEAD_DIM, n), lambda h: (0, h, 0)),
            const((2, n, D_RWKV)),
            const((16, D_RWKV)),
            const((QUAD, QUAD)),
            pl.BlockSpec((1, HEAD_DIM, HEAD_DIM, n), lambda h: (h, 0, 0, 0)),
        ],
        out_specs=[
            pl.BlockSpec((1, HEAD_DIM, HEAD_DIM, n), lambda h: (h, 0, 0, 0)),
            const((n, D_RWKV)),
        ],
        out_shape=[
            jax.ShapeDtypeStruct((H_RWKV, HEAD_DIM, HEAD_DIM, n), F32),
            jax.ShapeDtypeStruct((n, D_RWKV), F32),
        ],
        scratch_shapes=[pltpu.VMEM((D_RWKV, n), F32)],
        compiler_params=pltpu.CompilerParams(
            dimension_semantics=("arbitrary",), vmem_limit_bytes=VMEM_LIMIT),
        name="decode_state",
    )(vec_t, gb, prm, bmask, s_t)


def _decode_attn_kernel(qr_ref, kvn_ref, col_ref, ck_ref, cv_ref, sink_ref, ya_ref, kout_ref, vout_ref):
    nh = N_Q_HEADS
    seqs = range(DEC_TILE)
    hrow = lax.broadcasted_iota(jnp.int32, (nh, D_ATTN), 0)
    hlane = lax.broadcasted_iota(jnp.int32, (nh, D_ATTN), 1) // HEAD_DIM
    dmask = hrow == hlane
    grow = lax.broadcasted_iota(jnp.int32, (nh, D_KV), 0) // GQA_GROUP
    glane = lax.broadcasted_iota(jnp.int32, (nh, D_KV), 1) // HEAD_DIM
    gmask = grow == glane
    low = glane == 0
    key_idx = lax.broadcasted_iota(jnp.int32, (nh, WINDOW), 1)
    last = lax.broadcasted_iota(jnp.int32, (D_KV, WINDOW), 1) == WINDOW - 1
    sink = sink_ref[...]
    kvn = kvn_ref[...]
    col = col_ref[0]
    k_new = [kvn[j:j + 1, 0:D_KV] for j in seqs]
    v_new = [kvn[j:j + 1, D_KV:] for j in seqs]
    ck = [ck_ref[j] for j in seqs]
    cv = [cv_ref[j] for j in seqs]
    for j in seqs:
        kout_ref[j] = jnp.where(last, col[0:D_KV, j:j + 1], pltpu.roll(ck[j], WINDOW - 1, 1))
        vout_ref[j] = jnp.where(last, col[D_KV:, j:j + 1], pltpu.roll(cv[j], WINDOW - 1, 1))
    q8 = [qr_ref[j * nh:(j + 1) * nh, :] for j in seqs]
    qp = [jnp.where(gmask, jnp.concatenate([q8[j], q8[j]], axis=1), 0.0) for j in seqs]
    s_c = [jnp.where(key_idx >= 1, jnp.dot(qp[j], ck[j], preferred_element_type=F32) * ATTN_SCALE, NEG_INF)
           for j in seqs]
    s_n = [jnp.sum(qp[j] * k_new[j], axis=-1, keepdims=True) * ATTN_SCALE for j in seqs]
    m = [jnp.maximum(jnp.maximum(jnp.max(s_c[j], axis=-1, keepdims=True), s_n[j]), sink) for j in seqs]
    e_c = [jnp.exp(s_c[j] - m[j]) for j in seqs]
    e_n = [jnp.exp(s_n[j] - m[j]) for j in seqs]
    denom = [jnp.sum(e_c[j], axis=-1, keepdims=True) + e_n[j] + jnp.exp(sink - m[j]) for j in seqs]
    o = [(_dot_nt_f32(e_c[j], cv[j]) + e_n[j] * v_new[j]) / denom[j]
         for j in seqs]
    out_rows = []
    for j in seqs:
        rot = pltpu.roll(o[j], HEAD_DIM, 1)
        g0 = jnp.where(low, o[j], rot)
        g1 = jnp.where(low, rot, o[j])
        wide = jnp.concatenate([g0, g0, g1, g1], axis=1)
        out_rows.append(jnp.sum(jnp.where(dmask, wide, 0.0), axis=0, keepdims=True))
    ya_ref[...] = jnp.concatenate(out_rows, axis=0)


def _decode_attn(q_r, kvn, cols, ck_t, cv_t, sinks_col):
    n = kvn.shape[0]
    bt = DEC_TILE
    const = lambda shape: pl.BlockSpec(shape, lambda i: (0,) * len(shape))
    return pl.pallas_call(
        _decode_attn_kernel,
        grid=(n // bt,),
        in_specs=[
            pl.BlockSpec((bt * N_Q_HEADS, HEAD_DIM), lambda i: (i, 0)),
            pl.BlockSpec((bt, 2 * D_KV), lambda i: (i, 0)),
            pl.BlockSpec((1, 2 * D_KV, bt), lambda i: (i, 0, 0)),
            pl.BlockSpec((bt, D_KV, WINDOW), lambda i: (i, 0, 0)),
            pl.BlockSpec((bt, D_KV, WINDOW), lambda i: (i, 0, 0)),
            const((N_Q_HEADS, 1)),
        ],
        out_specs=[
            pl.BlockSpec((bt, D_ATTN), lambda i: (i, 0)),
            pl.BlockSpec((bt, D_KV, WINDOW), lambda i: (i, 0, 0)),
            pl.BlockSpec((bt, D_KV, WINDOW), lambda i: (i, 0, 0)),
        ],
        out_shape=[
            jax.ShapeDtypeStruct((n, D_ATTN), F32),
            jax.ShapeDtypeStruct((n, D_KV, WINDOW), F32),
            jax.ShapeDtypeStruct((n, D_KV, WINDOW), F32),
        ],
        compiler_params=pltpu.CompilerParams(
            dimension_semantics=("arbitrary",), vmem_limit_bytes=VMEM_LIMIT),
        name="decode_attn",
    )(q_r, kvn, cols, ck_t, cv_t, sinks_col)


def kernel(x_prompt, x_sample, state_wkv, state_shift, cache_k_win, cache_v_win, norm_mix_w, w_in, mu_shift, w0, w_decay_up, a0, w_a_up, w_g_up, k_k, k_a, r_k, ln_x_w, ln_x_b, q_norm_w, k_norm_w, sinks, w_out, norm_ffn_w, w_ffn_up, w_ffn_down):
    bsz, t, _ = x_prompt.shape
    nd = x_sample.shape[0]
    l = 0

    w_in_pad = jnp.swapaxes(w_in[l], 0, 1).astype(BF16)
    mu_pad = mu_shift[l][None, :]
    wl = jnp.zeros((D_LORA_PAD, 3 * D_RWKV), F32)
    wl = wl.at[0:32, 0:D_RWKV].set(w_decay_up[l])
    wl = wl.at[32:64, D_RWKV:2 * D_RWKV].set(w_a_up[l])
    wl = wl.at[64:160, 2 * D_RWKV:].set(w_g_up[l])
    wl = wl.astype(BF16)
    prm = jnp.zeros((16, D_RWKV), F32)
    prm = prm.at[0].set(w0[l]).at[1].set(a0[l]).at[2].set(k_k[l]).at[3].set(k_a[l])
    prm = prm.at[4].set(r_k[l].reshape(-1)).at[5].set(ln_x_w[l]).at[6].set(ln_x_b[l])
    hid = np.arange(QUAD) // HEAD_DIM
    bmask = jnp.asarray(hid[:, None] == hid[None, :], BF16)
    tri = jnp.asarray(np.tril(np.ones((CHUNK, CHUNK))), BF16)
    qw = jnp.tile(q_norm_w[l][None, :], (1, N_Q_HEADS))
    kw = jnp.tile(k_norm_w[l][None, :], (1, N_KV_HEADS))
    nmw = norm_mix_w[l][None, :]
    nfw = norm_ffn_w[l][None, :]
    wo = w_out[l].astype(BF16)
    wu = w_ffn_up[l].astype(BF16)
    wd = w_ffn_down[l].astype(BF16)
    tab_a, tab_b = _rope_block_tables(max(t, PAST_LEN + 1) // WINDOW + 1)
    ta, tb = tab_a[PAST_LEN // WINDOW], tab_b[PAST_LEN % WINDOW][None, :]
    tab_s = jnp.concatenate([ta[:, :128] * tb[:, :128] - ta[:, 128:] * tb[:, 128:],
                             ta[:, 128:] * tb[:, :128] + ta[:, :128] * tb[:, 128:]], axis=1)

    xs = x_sample.reshape(nd, D_MODEL)
    p_s, q_s, kv_s = _inproj(xs, nmw, w_in_pad, 128)
    shift_in = state_shift[l].reshape(nd, D_SHIFT)
    vec_t, gb, qn_s, kvn_s = _decode_prep(p_s, shift_in, q_s, kv_s, mu_pad, prm, wl, bmask, tab_s, qw, kw)
    s_t = jnp.transpose(state_wkv[l], (1, 2, 3, 0))
    ck_t = jnp.swapaxes(cache_k_win[l].reshape(nd, WINDOW, D_KV), 1, 2)
    cv_t = jnp.swapaxes(cache_v_win[l].reshape(nd, WINDOW, D_KV), 1, 2)
    q_r = qn_s.reshape(nd * N_Q_HEADS, HEAD_DIM)
    cols = jnp.swapaxes(kvn_s.reshape(nd // DEC_TILE, DEC_TILE, 2 * D_KV), 1, 2)
    wkv_t, yr_s = _decode_state(vec_t, gb, prm, bmask, s_t)
    ya_s, kc_t, vc_t = _decode_attn(q_r, kvn_s, cols, ck_t, cv_t, sinks[l][:, None])

    xp = x_prompt.reshape(bsz * t, D_MODEL)
    xs_p, plast, q_p, kv_p = _inproj_shift(xp, nmw, w_in_pad, mu_pad, t, 1024)
    yr_p, hbd = _rwkv_prompt(xs_p.reshape(bsz, t, D_SHIFT), prm, wl, tri, bmask)
    y_prompt, y_s, kwin_p, vwin_p = _attn_ffn(q_p.reshape(bsz, t, D_ATTN), kv_p.reshape(bsz, t, 2 * D_KV),
                                              tab_a, tab_b, qw, kw, sinks[l][None, :], bmask, _band_bias(),
                                              x_prompt, yr_p, xs, yr_s, ya_s, wo, nfw, wu, wd)
    hb = hbd.reshape(bsz, 2, 4, HEAD_DIM, 2, HEAD_DIM)
    wkv_prompt = jnp.stack([hb[:, :, j, :, j % 2, :] for j in range(4)], axis=2)
    wkv_prompt = wkv_prompt.reshape(bsz, H_RWKV, HEAD_DIM, HEAD_DIM)[None]
    shift_prompt = plast[:, 0:1, :][None]
    k_win_prompt = jnp.swapaxes(kwin_p, 1, 2).reshape(bsz, WINDOW, N_KV_HEADS, HEAD_DIM)[None]
    v_win_prompt = jnp.swapaxes(vwin_p, 1, 2).reshape(bsz, WINDOW, N_KV_HEADS, HEAD_DIM)[None]

    y_sample = y_s.reshape(nd, 1, D_MODEL)
    wkv_sample = jnp.transpose(wkv_t, (3, 0, 1, 2))[None]
    shift_sample = p_s.reshape(nd, 1, D_SHIFT)[None]
    k_win_sample = jnp.swapaxes(kc_t, 1, 2).reshape(nd, WINDOW, N_KV_HEADS, HEAD_DIM)[None]
    v_win_sample = jnp.swapaxes(vc_t, 1, 2).reshape(nd, WINDOW, N_KV_HEADS, HEAD_DIM)[None]

    return (y_prompt, y_sample, wkv_prompt, shift_prompt, k_win_prompt, v_win_prompt,
            wkv_sample, shift_sample, k_win_sample, v_win_sample)
```

```python
import functools

import jax
import jax.numpy as jnp
import numpy as np
from jax import lax
from jax.experimental import pallas as pl
from jax.experimental.pallas import tpu as pltpu

F32 = jnp.float32
BF16 = jnp.bfloat16

D_MODEL = 1024
D_RWKV = 512
D_ATTN = 512
HEAD_DIM = 64
H_RWKV = 8
N_Q_HEADS = 8
N_KV_HEADS = 2
GQA_GROUP = 4
D_KV = 128
D_LORA = 160
D_LORA_PAD = 256
D_SHIFT = 3 * D_RWKV + D_LORA
D_IN = D_SHIFT + D_ATTN + 2 * D_KV
WINDOW = 128
ROPE_DIM = 16
ROPE_HALF = 8
ROPE_THETA = 500000.0
ATTN_SCALE = HEAD_DIM ** -0.5
D_FF = 4096
RMS_EPS = 1e-6
LNX_EPS = 64e-5
NEG_INF = -1e30
LOG2E = 1.4426950408889634
PAST_LEN = 16384

CHUNK = 64
QUAD = 4 * HEAD_DIM
V7X_VMEM_BYTES = 64 * 1024 * 1024
VMEM_LIMIT = V7X_VMEM_BYTES // 8 * 7


def _split2(x):
    hi = x.astype(BF16)
    lo = (x - hi.astype(F32)).astype(BF16)
    return hi, lo


def _head_sums(xs, ones_ref):
    n, w = xs[0].shape
    tile = min(w, QUAD)
    per = w // tile
    pieces = [x[:, j * tile:(j + 1) * tile] for x in xs for j in range(per)]
    stacked = jnp.concatenate(pieces, axis=0) if len(pieces) > 1 else pieces[0]
    ones = ones_ref[0:tile, 0:tile]
    out = jnp.dot(stacked.astype(BF16), ones, preferred_element_type=F32)
    res = []
    for i in range(len(xs)):
        cols = [out[(i * per + j) * n:(i * per + j + 1) * n] for j in range(per)]
        res.append(jnp.concatenate(cols, axis=1) if per > 1 else cols[0])
    return res


def _cumsum_rows(tri_bf16, x):
    hi, lo = _split2(x)
    return (jnp.dot(tri_bf16, hi, preferred_element_type=F32)
            + jnp.dot(tri_bf16, lo, preferred_element_type=F32))


def _mm(a, b):
    return jnp.dot(a.astype(BF16), b.astype(BF16), preferred_element_type=F32)


def _mm_nt(a, b):
    return lax.dot_general(a.astype(BF16), b.astype(BF16), (((1,), (1,)), ((), ())),
                           preferred_element_type=F32)


def _mm_tn(a, b):
    return lax.dot_general(a.astype(BF16), b.astype(BF16), (((0,), (0,)), ((), ())),
                           preferred_element_type=F32)


def _dot_nt_f32(a, b):
    return lax.dot_general(a, b, (((1,), (1,)), ((), ())), preferred_element_type=F32)


def _sigmoid(x):
    return 1.0 / (1.0 + jnp.exp(-x))


def _interleave(*gens):
    live = list(gens)
    while live:
        for g in list(live):
            try:
                next(g)
            except StopIteration:
                live.remove(g)


def _norm_project(x, nw_ref, wt_ref):
    ms = jnp.mean(x * x, axis=-1, keepdims=True)
    h = ((x * lax.rsqrt(ms + RMS_EPS)) * nw_ref[...]).astype(BF16)
    return _mm_nt(h, wt_ref[0:D_SHIFT, :]), _mm_nt(h, wt_ref[D_SHIFT:, :])


def _inproj_kernel(x_ref, nw_ref, w_ref, p_ref, q_ref, kv_ref):
    p, qkv = _norm_project(x_ref[...], nw_ref, w_ref)
    p_ref[...] = p
    q_ref[...] = qkv[:, :D_ATTN]
    kv_ref[...] = qkv[:, D_ATTN:]


def _inproj_shift_kernel(tiles_per_seq, x_ref, nw_ref, w_ref, mu_ref, xs_ref, last_ref, q_ref, kv_ref, prev_ref):
    i = pl.program_id(0)

    @pl.when(i % tiles_per_seq == 0)
    def _():
        prev_ref[...] = jnp.zeros_like(prev_ref)

    p, qkv = _norm_project(x_ref[...], nw_ref, w_ref)
    tm = p.shape[0]
    row = lax.broadcasted_iota(jnp.int32, p.shape, 0)
    prev = jnp.where(row == 0, jnp.broadcast_to(prev_ref[0:1, :], p.shape), pltpu.roll(p, 1, 0))
    xs_ref[...] = p + (prev - p) * mu_ref[...]
    last = jnp.broadcast_to(p[tm - 1:tm, :], prev_ref.shape)
    prev_ref[...] = last
    last_ref[0] = last
    q_ref[...] = qkv[:, :D_ATTN]
    kv_ref[...] = qkv[:, D_ATTN:]


def _inproj_shift(x2d, norm_w, w_in_pad, mu_pad, seq_len, tm):
    m = x2d.shape[0]
    tiles_per_seq = seq_len // tm
    return pl.pallas_call(
        functools.partial(_inproj_shift_kernel, tiles_per_seq),
        grid=(m // tm,),
        in_specs=[
            pl.BlockSpec((tm, D_MODEL), lambda i: (i, 0)),
            pl.BlockSpec((1, D_MODEL), lambda i: (0, 0)),
            pl.BlockSpec((D_IN, D_MODEL), lambda i: (0, 0)),
            pl.BlockSpec((1, D_SHIFT), lambda i: (0, 0)),
        ],
        out_specs=[
            pl.BlockSpec((tm, D_SHIFT), lambda i: (i, 0)),
            pl.BlockSpec((1, 8, D_SHIFT), lambda i: (i // tiles_per_seq, 0, 0)),
            pl.BlockSpec((tm, D_ATTN), lambda i: (i, 0)),
            pl.BlockSpec((tm, 2 * D_KV), lambda i: (i, 0)),
        ],
        out_shape=[
            jax.ShapeDtypeStruct((m, D_SHIFT), F32),
            jax.ShapeDtypeStruct((m // seq_len, 8, D_SHIFT), F32),
            jax.ShapeDtypeStruct((m, D_ATTN), F32),
            jax.ShapeDtypeStruct((m, 2 * D_KV), F32),
        ],
        scratch_shapes=[pltpu.VMEM((8, D_SHIFT), F32)],
        compiler_params=pltpu.CompilerParams(
            dimension_semantics=("arbitrary",), vmem_limit_bytes=VMEM_LIMIT),
        name="inproj_shift",
    )(x2d, norm_w, w_in_pad, mu_pad)


def _inproj(x2d, norm_w, w_in_pad, tm):
    m = x2d.shape[0]
    return pl.pallas_call(
        _inproj_kernel,
        grid=(m // tm,),
        in_specs=[
            pl.BlockSpec((tm, D_MODEL), lambda i: (i, 0)),
            pl.BlockSpec((1, D_MODEL), lambda i: (0, 0)),
            pl.BlockSpec((D_IN, D_MODEL), lambda i: (0, 0)),
        ],
        out_specs=[
            pl.BlockSpec((tm, D_SHIFT), lambda i: (i, 0)),
            pl.BlockSpec((tm, D_ATTN), lambda i: (i, 0)),
            pl.BlockSpec((tm, 2 * D_KV), lambda i: (i, 0)),
        ],
        out_shape=[
            jax.ShapeDtypeStruct((m, D_SHIFT), F32),
            jax.ShapeDtypeStruct((m, D_ATTN), F32),
            jax.ShapeDtypeStruct((m, 2 * D_KV), F32),
        ],
        compiler_params=pltpu.CompilerParams(
            dimension_semantics=("arbitrary",), vmem_limit_bytes=VMEM_LIMIT),
        name="inproj",
    )(x2d, norm_w, w_in_pad)


def _rwkv_features(xs, prm, wl_ref, ones_ref):
    r = xs[:, 0:D_RWKV]
    k = xs[:, D_RWKV:2 * D_RWKV]
    v = xs[:, 2 * D_RWKV:3 * D_RWKV]
    lora = xs[:, 3 * D_RWKV:]
    lora = jnp.concatenate([lora, jnp.zeros((lora.shape[0], wl_ref.shape[0] - D_LORA), F32)], axis=1)
    col = lax.broadcasted_iota(jnp.int32, lora.shape, 1)
    act = jnp.where(col < 32, jnp.tanh(lora), jnp.where(col < 64, lora, _sigmoid(lora)))
    up = jnp.dot(act.astype(BF16), wl_ref[...], preferred_element_type=F32)
    w0, a0, k_k, k_a, r_k = prm[0:1], prm[1:2], prm[2:3], prm[3:4], prm[4:5]
    logw = (-np.exp(-0.5)) * _sigmoid(w0 + up[:, 0:D_RWKV])
    asig = _sigmoid(a0 + up[:, D_RWKV:2 * D_RWKV])
    g = up[:, 2 * D_RWKV:]
    kk = k * k_k
    k_mod = k * (1.0 + (asig - 1.0) * k_a)
    ss, rk = _head_sums([kk * kk, r * k_mod * r_k], ones_ref)
    kk = kk / jnp.maximum(jnp.sqrt(ss), 1e-12)
    k = k_mod
    bonus = rk * v
    return r, logw, k, v, -kk, kk * asig, g, bonus


def _rwkv_finish(y, g, bonus, prm, ones_ref):
    ln_w, ln_b = prm[5:6], prm[6:7]
    mean = _head_sums([y], ones_ref)[0] * (1.0 / HEAD_DIM)
    d = y - mean
    var = _head_sums([d * d], ones_ref)[0] * (1.0 / HEAD_DIM)
    yn = d * lax.rsqrt(var + LNX_EPS) * ln_w + ln_b
    return (yn + bonus) * g


def _block_diag(x, bmask):
    return jnp.concatenate([x] * 4, axis=0) * bmask


def _chunk_prep(r, logw, k, v, a, b, tri_ref, bmask):
    n = len(r)
    ch = range(n)
    tri = tri_ref[...]
    cum = [_cumsum_rows(tri, logw[i]) for i in ch]
    yield
    e_in = [jnp.exp(cum[i]) for i in ch]
    e_ex = [jnp.exp(cum[i] - logw[i]) for i in ch]
    e_inv = [1.0 / e_in[i] for i in ch]
    e_last = [e_in[i][CHUNK - 1:CHUNK, :] for i in ch]
    rt = [(r[i] * e_in[i]).astype(BF16) for i in ch]
    at = [(a[i] * e_ex[i]).astype(BF16) for i in ch]
    kt = [(k[i] * e_inv[i]).astype(BF16) for i in ch]
    bt = [(b[i] * e_inv[i]).astype(BF16) for i in ch]
    vb = [v[i].astype(BF16) for i in ch]
    yield

    t_idx = lax.broadcasted_iota(jnp.int32, (CHUNK, QUAD), 0)
    s_idx = lax.broadcasted_iota(jnp.int32, (CHUNK, QUAD), 1) & (HEAD_DIM - 1)
    strict = s_idx < t_idx
    incl = s_idx <= t_idx

    gm = [_mm_nt(jnp.concatenate([at[i], rt[i]], axis=0),
                 jnp.concatenate([_block_diag(bt[i], bmask), _block_diag(kt[i], bmask)], axis=0))
          for i in ch]
    yield
    a_ab = [jnp.where(strict, gm[i][:CHUNK, :QUAD], 0.0) for i in ch]
    a_ak = [jnp.where(strict, gm[i][:CHUNK, QUAD:], 0.0) for i in ch]
    a_rb = [jnp.where(incl, gm[i][CHUNK:, :QUAD], 0.0) for i in ch]
    a_rk = [jnp.where(incl, gm[i][CHUNK:, QUAD:], 0.0) for i in ch]

    eye = jnp.where(s_idx == t_idx, 1.0, 0.0)
    pwb = [a_ab[i].astype(BF16) for i in ch]
    t_inv = [eye + a_ab[i] for i in ch]
    for it in range(6):
        rbd = [_block_diag(pwb[i], bmask) for i in ch]
        if it == 0:
            pwb = [_mm(pwb[i], rbd[i]).astype(BF16) for i in ch]
        elif it < 5:
            out = [_mm(jnp.concatenate([pwb[i], t_inv[i].astype(BF16)], axis=0), rbd[i]) for i in ch]
            pwb = [out[i][:CHUNK].astype(BF16) for i in ch]
            t_inv = [t_inv[i] + out[i][CHUNK:] for i in ch]
        else:
            t_inv = [t_inv[i] + _mm(t_inv[i], rbd[i]) for i in ch]
        yield

    vbd = [_block_diag(vb[i], bmask) for i in ch]
    xy0 = [_mm(jnp.concatenate([a_ak[i], a_rk[i]], axis=0), vbd[i]) for i in ch]
    return [dict(ar=jnp.concatenate([at[i], rt[i]], axis=0), x0=xy0[i][:CHUNK], y0=xy0[i][CHUNK:],
                 t_inv=t_inv[i].astype(BF16), a_rb=a_rb[i].astype(BF16), vb=vb[i],
                 bk=jnp.concatenate([bt[i], kt[i]], axis=0), e_last=e_last[i]) for i in ch]


def _chunk_step(pre, state, bmask, out):
    ch = range(len(pre))
    half = QUAD // 2
    zeros = jnp.zeros((half, half), BF16)
    sc = [state[i].astype(BF16) for i in ch]
    sb = [jnp.concatenate([jnp.concatenate([sc[i][:half], zeros], axis=1),
                           jnp.concatenate([zeros, sc[i][half:]], axis=1)], axis=0) for i in ch]
    xr = [_mm_nt(pre[i]["ar"], sb[i]) for i in ch]
    yield
    x = [xr[i][:CHUNK] + pre[i]["x0"] for i in ch]
    u = [_mm(pre[i]["t_inv"], _block_diag(x[i].astype(BF16), bmask)) for i in ch]
    yield
    ub = [u[i].astype(BF16) for i in ch]
    out["y"] = [xr[i][CHUNK:] + pre[i]["y0"] + _mm(pre[i]["a_rb"], _block_diag(ub[i], bmask)) for i in ch]
    upd = [_mm_tn(jnp.concatenate([ub[i], pre[i]["vb"]], axis=0), pre[i]["bk"]) for i in ch]
    yield
    bm = bmask[:half, :half].astype(F32)
    s_new = []
    for i in ch:
        e_last = pre[i]["e_last"]
        top = (state[i][:half] + upd[i][:half, :half] * bm) * e_last[:, :half]
        bot = (state[i][half:] + upd[i][half:, half:] * bm) * e_last[:, half:]
        s_new.append(jnp.concatenate([top, bot], axis=0))
    out["state"] = s_new


def _finish_stages(ys, g, bonus, prm, ones_ref, y_ref, j):
    nseq = len(ys) // 2
    ln_w, ln_b = prm[5:6], prm[6:7]
    y = jnp.concatenate([jnp.concatenate(ys[2 * s:2 * s + 2], axis=1) for s in range(nseq)], axis=0)
    mean = _head_sums([y], ones_ref)[0] * (1.0 / HEAD_DIM)
    yield
    d = y - mean
    var = _head_sums([d * d], ones_ref)[0] * (1.0 / HEAD_DIM)
    yield
    out = (d * lax.rsqrt(var + LNX_EPS) * ln_w + ln_b + bonus) * g
    for s in range(nseq):
        y_ref[s, j * CHUNK:(j + 1) * CHUNK, :] = out[s * CHUNK:(s + 1) * CHUNK]


def _rwkv_prompt_kernel(xs_ref, prm_ref, wl_ref, tri_ref, bmask_ref, y_ref, hout_ref, h_ref):
    c = pl.program_id(0)
    nseq, tstep, _ = xs_ref.shape
    nchunk = tstep // CHUNK

    @pl.when(c == 0)
    def _():
        h_ref[...] = jnp.zeros_like(h_ref)

    xs = jnp.concatenate([xs_ref[s] for s in range(nseq)], axis=0)
    prm = prm_ref[...]
    r, logw, k, v, a, b, g, bonus = _rwkv_features(xs, prm, wl_ref, bmask_ref)
    bmask = bmask_ref[...]
    lanes = [(s, q) for s in range(nseq) for q in range(2)]
    pre = {}

    def prep(chunks):
        chains = [(j, s, q) for j in chunks for s, q in lanes]
        cut = lambda x: [x[s * tstep + j * CHUNK:s * tstep + (j + 1) * CHUNK, q * QUAD:(q + 1) * QUAD]
                         for j, s, q in chains]
        res = yield from _chunk_prep(cut(r), cut(logw), cut(k), cut(v), cut(a), cut(b), tri_ref, bmask)
        for n, j in enumerate(chunks):
            pre[j] = res[n * len(lanes):(n + 1) * len(lanes)]

    nfirst = max(nchunk - 1, 1)
    _interleave(prep(range(nfirst)))
    later = prep(range(nfirst, nchunk))
    state = [h_ref[s, q] for s, q in lanes]
    crow = lambda x, j: jnp.concatenate([x[s * tstep + j * CHUNK:s * tstep + (j + 1) * CHUNK]
                                         for s in range(nseq)], axis=0)
    finish = iter(())
    for j in range(nchunk):
        res = {}
        for _ in _chunk_step(pre[j], state, bmask, res):
            next(later, None)
            next(finish, None)
        _interleave(finish)
        if j == nfirst - 1:
            _interleave(later)
        state = res["state"]
        finish = _finish_stages(res["y"], crow(g, j), crow(bonus, j), prm, bmask_ref, y_ref, j)
    _interleave(finish)
    for i, (s, q) in enumerate(lanes):
        h_ref[s, q] = state[i]

    @pl.when(c == pl.num_programs(0) - 1)
    def _():
        hout_ref[...] = h_ref[...]


RWKV_CHUNKS_PER_STEP = 4


def _rwkv_prompt(xs3d, prm, wl, tri, bmask):
    bsz, t, _ = xs3d.shape
    tstep = RWKV_CHUNKS_PER_STEP * CHUNK
    nc = t // tstep
    const = lambda shape: pl.BlockSpec(shape, lambda c: (0,) * len(shape))
    state_shape = (bsz, 2, QUAD, QUAD // 2)
    return pl.pallas_call(
        _rwkv_prompt_kernel,
        grid=(nc,),
        in_specs=[
            pl.BlockSpec((bsz, tstep, D_SHIFT), lambda c: (0, c, 0)),
            const((16, D_RWKV)),
            const((D_LORA_PAD, 3 * D_RWKV)),
            const((CHUNK, CHUNK)),
            const((QUAD, QUAD)),
        ],
        out_specs=[
            pl.BlockSpec((bsz, tstep, D_RWKV), lambda c: (0, c, 0)),
            const(state_shape),
        ],
        out_shape=[
            jax.ShapeDtypeStruct((bsz, t, D_RWKV), F32),
            jax.ShapeDtypeStruct(state_shape, F32),
        ],
        scratch_shapes=[pltpu.VMEM(state_shape, F32)],
        compiler_params=pltpu.CompilerParams(
            dimension_semantics=("arbitrary",), vmem_limit_bytes=VMEM_LIMIT),
        name="rwkv_prompt",
    )(xs3d, prm, wl, tri, bmask)


def _rope_lane_freq():
    inv_freq = jnp.power(ROPE_THETA, -jnp.arange(ROPE_HALF, dtype=F32) * (2.0 / ROPE_DIM))
    return inv_freq[(np.arange(2 * HEAD_DIM) % HEAD_DIM) % ROPE_HALF][None, :]


def _rope_tables(cos, sin):
    dim = lax.broadcasted_iota(jnp.int32, cos.shape, 1) & (HEAD_DIM - 1)
    cos_t = jnp.where(dim < ROPE_DIM, cos, 1.0)
    sin_lo = jnp.where(dim < ROPE_HALF, -sin, 0.0)
    sin_hi = jnp.where((dim >= ROPE_HALF) & (dim < ROPE_DIM), sin, 0.0)
    return cos_t, sin_lo, sin_hi


def _qk_norm_rope(x, norm_w, cos_t, sin_lo, sin_hi, ones_ref):
    ms = _head_sums([x * x], ones_ref)[0] * (1.0 / HEAD_DIM)
    xn = x * lax.rsqrt(ms + RMS_EPS) * norm_w
    width = x.shape[1]
    fwd = pltpu.roll(xn, width - ROPE_HALF, 1)
    bwd = pltpu.roll(xn, ROPE_HALF, 1)
    return xn * cos_t + fwd * sin_lo + bwd * sin_hi


def _tile_lanes(x, reps):
    return jnp.concatenate([x] * reps, axis=1) if reps > 1 else x


def _attn_stages(q_ref, kv_ref, taba_ref, tabb_ref, qw_ref, kw_ref, sink_ref, ones_ref, bias_ref, first_bias,
                 kprev_ref, vprev_ref, out):
    nseq, tq, _ = q_ref.shape
    nblk = tq // WINDOW
    units = [(s, b) for s in range(nseq) for b in range(nblk)]
    blk = lambda b: slice(b * WINDOW, (b + 1) * WINDOW)
    tb = tabb_ref[...]
    cos_b, sin_b = tb[:, :128], tb[:, 128:]
    rope, rope4 = [], []
    for b in range(nblk):
        ta = taba_ref[b][0:1, :]
        cos_a, sin_a = ta[:, :128], ta[:, 128:]
        tabs = _rope_tables(cos_a * cos_b - sin_a * sin_b, sin_a * cos_b + cos_a * sin_b)
        rope.append(tabs)
        rope4.append([_tile_lanes(x, 4) for x in tabs])
    q = {(s, b): _qk_norm_rope(q_ref[s, blk(b), :], qw_ref[...], *rope4[b], ones_ref) * (ATTN_SCALE * LOG2E)
         for s, b in units}
    kv = {(s, b): kv_ref[s, blk(b), :] for s, b in units}
    k_cur = {u: _qk_norm_rope(kv[u][:, 0:D_KV], kw_ref[...], *rope[u[1]], ones_ref) for u in units}
    v_cur = {u: kv[u][:, D_KV:] for u in units}
    k_all = {(s, b): jnp.concatenate([kprev_ref[s] if b == 0 else k_cur[(s, b - 1)], k_cur[(s, b)]], axis=0)
             for s, b in units}
    v_all = {(s, b): jnp.concatenate([vprev_ref[s] if b == 0 else v_cur[(s, b - 1)], v_cur[(s, b)]], axis=0)
             for s, b in units}
    out["k_cur"] = [k_cur[(s, nblk - 1)] for s in range(nseq)]
    out["v_cur"] = [v_cur[(s, nblk - 1)] for s in range(nseq)]
    for s in range(nseq):
        kprev_ref[s] = out["k_cur"][s]
        vprev_ref[s] = out["v_cur"][s]
    yield

    nk = 2 * WINDOW
    bias = [bias_ref[first_bias] if b == 0 else bias_ref[1] for b in range(nblk)]
    sinks = sink_ref[...] * LOG2E
    low = lax.broadcasted_iota(jnp.int32, (nk, D_KV), 1) < HEAD_DIM
    lane_blk = [ones_ref[j * HEAD_DIM:j * HEAD_DIM + 1, :] for j in range(GQA_GROUP)]

    chains = [(u, g) for u in units for g in range(N_KV_HEADS)]
    ch = range(len(chains))
    k_rot = {u: pltpu.roll(k_all[u], HEAD_DIM, 1) for u in units}
    k2 = [jnp.where(low, k_all[u], k_rot[u]) if g == 0 else jnp.where(low, k_rot[u], k_all[u])
          for u, g in chains]
    k4 = [jnp.concatenate([k2[c], k2[c]], axis=1).astype(BF16) for c in ch]
    vb = {u: v_all[u].astype(BF16) for u in units}
    qg = [q[u][:, g * QUAD:(g + 1) * QUAD].astype(BF16) for u, g in chains]
    qstack = [jnp.concatenate([qg[c] * lane_blk[j] for j in range(GQA_GROUP)], axis=0) for c in ch]
    sink_row = [jnp.concatenate(
        [jnp.broadcast_to(sinks[:, g * GQA_GROUP + j:g * GQA_GROUP + j + 1], (1, WINDOW))
         for j in range(GQA_GROUP)], axis=1) for u, g in chains]
    yield
    sc = [_mm_nt(k4[c], qstack[c]) + bias[u[1]] for c, (u, g) in enumerate(chains)]
    yield
    m = [jnp.maximum(jnp.max(sc[c], axis=0, keepdims=True), sink_row[c]) for c in ch]
    e = [jnp.exp2(sc[c] - m[c]) for c in ch]
    yield
    denom = [jnp.sum(e[c], axis=0, keepdims=True) + jnp.exp2(sink_row[c] - m[c]) for c in ch]
    ot = {(u, g): _mm_tn(vb[u], e[c].astype(BF16))[g * HEAD_DIM:(g + 1) * HEAD_DIM, :] * (1.0 / denom[c])
          for c, (u, g) in enumerate(chains)}
    yield
    ya = []
    for s in range(nseq):
        blocks = []
        for b in range(nblk):
            yt = jnp.concatenate([ot[((s, b), g)][:, j * WINDOW:(j + 1) * WINDOW]
                                  for g in range(N_KV_HEADS) for j in range(GQA_GROUP)], axis=0)
            blocks.append(jnp.transpose(yt))
        ya.append(jnp.concatenate(blocks, axis=0) if nblk > 1 else blocks[0])
    out["ya"] = ya


def _ffn_stages(x, yr, ya, wo_ref, nw_ref, wu_ref, wd_ref, out, pieces=4):
    mix = jnp.concatenate([yr, ya], axis=1).astype(BF16)
    x1 = x + jnp.dot(mix, wo_ref[...], preferred_element_type=F32)
    yield
    ms = jnp.mean(x1 * x1, axis=-1, keepdims=True)
    hf = ((x1 * lax.rsqrt(ms + RMS_EPS)) * nw_ref[...]).astype(BF16)
    acc = x1
    step = D_FF // pieces
    for j in range(pieces):
        up = jnp.dot(hf, wu_ref[:, j * step:(j + 1) * step], preferred_element_type=F32)
        yield
        act = jnp.square(jnp.maximum(up, 0.0)).astype(BF16)
        acc = acc + jnp.dot(act, wd_ref[j * step:(j + 1) * step, :], preferred_element_type=F32)
        yield
    out["y"] = acc


def _attn_ffn_kernel(q_ref, kv_ref, taba_ref, tabb_ref, qw_ref, kw_ref, sink_ref, ones_ref, bias_ref,
                     x_ref, yr_ref, xd_ref, yrd_ref, yad_ref, wo_ref, nw_ref, wu_ref, wd_ref,
                     o_ref, od_ref, kwin_ref, vwin_ref, kprev_ref, vprev_ref, ya_ref):
    i = pl.program_id(0)
    nseq, tq, _ = q_ref.shape
    seqs = range(nseq)
    first = i == 0

    @pl.when(first)
    def _():
        kprev_ref[...] = jnp.zeros_like(kprev_ref)
        vprev_ref[...] = jnp.zeros_like(vprev_ref)
        ya_ref[...] = jnp.zeros_like(ya_ref)

    def rows(ref, dec_ref):
        tile = jnp.concatenate([ref[s] for s in seqs], axis=0)
        dec = dec_ref[...]
        return jnp.where(first, jnp.concatenate([dec] * (tile.shape[0] // dec.shape[0]), axis=0), tile)

    a_out, f_out = {}, {}
    _interleave(
        _ffn_stages(rows(x_ref, xd_ref), rows(yr_ref, yrd_ref), rows(ya_ref, yad_ref),
                    wo_ref, nw_ref, wu_ref, wd_ref, f_out),
        _attn_stages(q_ref, kv_ref, taba_ref, tabb_ref, qw_ref, kw_ref, sink_ref, ones_ref, bias_ref,
                     jnp.minimum(i, 1), kprev_ref, vprev_ref, a_out))
    for s in seqs:
        o_ref[s] = f_out["y"][s * tq:(s + 1) * tq]
        ya_ref[s] = a_out["ya"][s]

    @pl.when(first)
    def _():
        od_ref[...] = f_out["y"][:od_ref.shape[0]]

    @pl.when(i == pl.num_programs(0) - 2)
    def _():
        for s in seqs:
            kwin_ref[s] = jnp.transpose(a_out["k_cur"][s])
            vwin_ref[s] = jnp.transpose(a_out["v_cur"][s])


def _band_bias():
    ki = np.arange(2 * WINDOW)[:, None]
    qi = (np.arange(GQA_GROUP * WINDOW) % WINDOW + WINDOW)[None, :]
    dq = qi - ki
    band = (dq >= 0) & (dq < WINDOW)
    first = band & (ki >= WINDOW)
    return jnp.asarray(np.where(np.stack([first, band]), 0.0, NEG_INF), F32)


def _rope_block_tables(nb):
    freq = _rope_lane_freq()
    ang_a = (jnp.arange(nb, dtype=F32) * WINDOW)[:, None] * freq
    ang_b = jnp.arange(WINDOW, dtype=F32)[:, None] * freq
    tab_a = jnp.concatenate([jnp.cos(ang_a), jnp.sin(ang_a)], axis=1)
    tab_b = jnp.concatenate([jnp.cos(ang_b), jnp.sin(ang_b)], axis=1)
    return jnp.broadcast_to(tab_a[:, None, :], (nb, 8, 4 * HEAD_DIM)), tab_b


ATTN_BLOCKS_PER_STEP = 1


def _attn_ffn(q3d, kv3d, tab_a, tab_b, qw, kw, sinks, ones_bd, bias, x3d, yr3d, xd, yrd, yad, wo, nw, wu, wd):
    bsz, t, _ = q3d.shape
    nd = xd.shape[0]
    nblk = ATTN_BLOCKS_PER_STEP
    tq = nblk * WINDOW
    nt = t // tq
    const = lambda shape: pl.BlockSpec(shape, lambda i: (0,) * len(shape))
    single = lambda shape: pl.BlockSpec(shape, lambda i: (0,) * len(shape), pipeline_mode=pl.Buffered(1))
    cur = lambda i: jnp.minimum(i, nt - 1)
    prv = lambda i: jnp.maximum(i - 1, 0)
    return pl.pallas_call(
        _attn_ffn_kernel,
        grid=(nt + 1,),
        in_specs=[
            pl.BlockSpec((bsz, tq, D_ATTN), lambda i: (0, cur(i), 0)),
            pl.BlockSpec((bsz, tq, 2 * D_KV), lambda i: (0, cur(i), 0)),
            pl.BlockSpec((nblk, 8, 4 * HEAD_DIM), lambda i: (cur(i), 0, 0)),
            const((WINDOW, 4 * HEAD_DIM)),
            const((1, D_ATTN)),
            const((1, D_KV)),
            const((1, N_Q_HEADS)),
            const((QUAD, QUAD)),
            const((2, 2 * WINDOW, GQA_GROUP * WINDOW)),
            pl.BlockSpec((bsz, tq, D_MODEL), lambda i: (0, prv(i), 0)),
            pl.BlockSpec((bsz, tq, D_RWKV), lambda i: (0, prv(i), 0)),
            const((nd, D_MODEL)),
            const((nd, D_RWKV)),
            const((nd, D_ATTN)),
            single((D_MODEL, D_MODEL)),
            const((1, D_MODEL)),
            single((D_MODEL, D_FF)),
            single((D_FF, D_MODEL)),
        ],
        out_specs=[
            pl.BlockSpec((bsz, tq, D_MODEL), lambda i: (0, prv(i), 0)),
            const((nd, D_MODEL)),
            const((bsz, WINDOW, D_KV)),
            const((bsz, WINDOW, D_KV)),
        ],
        out_shape=[
            jax.ShapeDtypeStruct((bsz, t, D_MODEL), F32),
            jax.ShapeDtypeStruct((nd, D_MODEL), F32),
            jax.ShapeDtypeStruct((bsz, WINDOW, D_KV), F32),
            jax.ShapeDtypeStruct((bsz, WINDOW, D_KV), F32),
        ],
        scratch_shapes=[
            pltpu.VMEM((bsz, WINDOW, D_KV), F32),
            pltpu.VMEM((bsz, WINDOW, D_KV), F32),
            pltpu.VMEM((bsz, tq, D_ATTN), F32),
        ],
        compiler_params=pltpu.CompilerParams(
            dimension_semantics=("arbitrary",), vmem_limit_bytes=VMEM_LIMIT),
        name="attn_ffn",
    )(q3d, kv3d, tab_a, tab_b, qw, kw, sinks, ones_bd, bias, x3d, yr3d, xd, yrd, yad, wo, nw, wu, wd)


DEC_TILE = 16


def _decode_prep_kernel(p_ref, sh_ref, q_ref, kv_ref, mu_ref, prm_ref, wl_ref, ones_ref, tab_ref,
                        qw_ref, kw_ref, vec_ref, vgb_ref, qn_ref, kvn_ref):
    p = p_ref[...]
    xs = p + (sh_ref[...] - p) * mu_ref[...]
    r, logw, k, v, a, b, g, bonus = _rwkv_features(xs, prm_ref[...], wl_ref, ones_ref)
    for i, x in enumerate((a, b, k, jnp.exp(logw), r, v)):
        vec_ref[i] = jnp.transpose(x)
    vgb_ref[0] = g
    vgb_ref[1] = bonus
    n = p.shape[0]
    tab = jnp.broadcast_to(tab_ref[0:1, :], (n, 4 * HEAD_DIM))
    cos_t, sin_lo, sin_hi = _rope_tables(tab[:, :128], tab[:, 128:])
    qn_ref[...] = _qk_norm_rope(q_ref[...], qw_ref[...], _tile_lanes(cos_t, 4), _tile_lanes(sin_lo, 4),
                                _tile_lanes(sin_hi, 4), ones_ref)
    kv = kv_ref[...]
    kvn_ref[:, 0:D_KV] = _qk_norm_rope(kv[:, 0:D_KV], kw_ref[...], cos_t, sin_lo, sin_hi, ones_ref)
    kvn_ref[:, D_KV:] = kv[:, D_KV:]


def _decode_prep(p, shift, q, kv, mu_pad, prm, wl, bmask, tab, qw, kw):
    n = p.shape[0]
    full = lambda shape: pl.BlockSpec(shape, lambda i: (0,) * len(shape))
    return pl.pallas_call(
        _decode_prep_kernel,
        grid=(1,),
        in_specs=[full((n, D_SHIFT)), full((n, D_SHIFT)), full((n, D_ATTN)), full((n, 2 * D_KV)),
                  full((1, D_SHIFT)), full((16, D_RWKV)), full((D_LORA_PAD, 3 * D_RWKV)),
                  full((QUAD, QUAD)), full((8, 4 * HEAD_DIM)), full((1, D_ATTN)), full((1, D_KV))],
        out_specs=[full((6, D_RWKV, n)), full((2, n, D_RWKV)), full((n, D_ATTN)), full((n, 2 * D_KV))],
        out_shape=[
            jax.ShapeDtypeStruct((6, D_RWKV, n), F32),
            jax.ShapeDtypeStruct((2, n, D_RWKV), F32),
            jax.ShapeDtypeStruct((n, D_ATTN), F32),
            jax.ShapeDtypeStruct((n, 2 * D_KV), F32),
        ],
        compiler_params=pltpu.CompilerParams(
            dimension_semantics=("arbitrary",), vmem_limit_bytes=VMEM_LIMIT),
        name="decode_prep",
    )(p, shift, q, kv, mu_pad, prm, wl, bmask, tab, qw, kw)


def _decode_state_stages(vec_ref, s_ref, sout_ref, yt_ref, h):
    a_t, b_t, k_t, w_t, r_t = (vec_ref[i] for i in range(5))
    for i in range(HEAD_DIM):
        s = s_ref[0, i]
        sa = jnp.sum(s * a_t, axis=0, keepdims=True)
        v_i = vec_ref[5, i:i + 1, :]
        s_new = s * w_t + sa * b_t + v_i * k_t
        sout_ref[0, i] = s_new
        yt_ref[pl.ds(h * HEAD_DIM + i, 1), :] = jnp.sum(s_new * r_t, axis=0, keepdims=True)
        if i % 8 == 7:
            yield


def _decode_attn_stages(qr_ref, kvn_ref, col_ref, ck_ref, cv_ref, sink_ref, ya_ref, kout_ref, vout_ref):
    nh = N_Q_HEADS
    seqs = range(DEC_TILE)
    hrow = lax.broadcasted_iota(jnp.int32, (nh, D_ATTN), 0)
    hlane = lax.broadcasted_iota(jnp.int32, (nh, D_ATTN), 1) // HEAD_DIM
    dmask = hrow == hlane
    grow = lax.broadcasted_iota(jnp.int32, (nh, D_KV), 0) // GQA_GROUP
    glane = lax.broadcasted_iota(jnp.int32, (nh, D_KV), 1) // HEAD_DIM
    gmask = grow == glane
    low = glane == 0
    key_idx = lax.broadcasted_iota(jnp.int32, (nh, WINDOW), 1)
    last = lax.broadcasted_iota(jnp.int32, (D_KV, WINDOW), 1) == WINDOW - 1
    sink = sink_ref[...]
    kvn = kvn_ref[...]
    col = col_ref[0]
    k_new = [kvn[j:j + 1, 0:D_KV] for j in seqs]
    v_new = [kvn[j:j + 1, D_KV:] for j in seqs]
    ck = [ck_ref[j] for j in seqs]
    cv = [cv_ref[j] for j in seqs]
    for j in seqs:
        kout_ref[j] = jnp.where(last, col[0:D_KV, j:j + 1], pltpu.roll(ck[j], WINDOW - 1, 1))
        vout_ref[j] = jnp.where(last, col[D_KV:, j:j + 1], pltpu.roll(cv[j], WINDOW - 1, 1))
    yield
    q8 = [qr_ref[j * nh:(j + 1) * nh, :] for j in seqs]
    qp = [jnp.where(gmask, jnp.concatenate([q8[j], q8[j]], axis=1), 0.0) for j in seqs]
    s_c = [jnp.where(key_idx >= 1, jnp.dot(qp[j], ck[j], preferred_element_type=F32) * ATTN_SCALE, NEG_INF)
           for j in seqs]
    s_n = [jnp.sum(qp[j] * k_new[j], axis=-1, keepdims=True) * ATTN_SCALE for j in seqs]
    yield
    m = [jnp.maximum(jnp.maximum(jnp.max(s_c[j], axis=-1, keepdims=True), s_n[j]), sink) for j in seqs]
    yield
    e_c = [jnp.exp(s_c[j] - m[j]) for j in seqs]
    e_n = [jnp.exp(s_n[j] - m[j]) for j in seqs]
    denom = [jnp.sum(e_c[j], axis=-1, keepdims=True) + e_n[j] + jnp.exp(sink - m[j]) for j in seqs]
    yield
    o = [(_dot_nt_f32(e_c[j], cv[j]) + e_n[j] * v_new[j]) / denom[j]
         for j in seqs]
    yield
    out_rows = []
    for j in seqs:
        rot = pltpu.roll(o[j], HEAD_DIM, 1)
        g0 = jnp.where(low, o[j], rot)
        g1 = jnp.where(low, rot, o[j])
        wide = jnp.concatenate([g0, g0, g1, g1], axis=1)
        out_rows.append(jnp.sum(jnp.where(dmask, wide, 0.0), axis=0, keepdims=True))
    ya_ref[...] = jnp.concatenate(out_rows, axis=0)


def _decode_step_kernel(vec_ref, gb_ref, prm_ref, ones_ref, s_ref, qr_ref, kvn_ref, col_ref, ck_ref, cv_ref,
                        sink_ref, sout_ref, yr_ref, ya_ref, kout_ref, vout_ref, yt_ref):
    i = pl.program_id(0)
    _interleave(
        _decode_attn_stages(qr_ref, kvn_ref, col_ref, ck_ref, cv_ref, sink_ref, ya_ref, kout_ref, vout_ref),
        _decode_state_stages(vec_ref, s_ref, sout_ref, yt_ref, i))

    @pl.when(i == pl.num_programs(0) - 1)
    def _():
        y = jnp.transpose(yt_ref[...])
        yr_ref[...] = _rwkv_finish(y, gb_ref[0], gb_ref[1], prm_ref[...], ones_ref)


def _decode_step(vec_t, gb, prm, bmask, s_t, q_r, kvn, cols, ck_t, cv_t, sinks_col):
    n = s_t.shape[-1]
    bt = DEC_TILE
    assert n // bt == H_RWKV, "one sequence tile per RWKV head"
    const = lambda shape: pl.BlockSpec(shape, lambda i: (0,) * len(shape))
    return pl.pallas_call(
        _decode_step_kernel,
        grid=(H_RWKV,),
        in_specs=[
            pl.BlockSpec((6, HEAD_DIM, n), lambda i: (0, i, 0)),
            const((2, n, D_RWKV)),
            const((16, D_RWKV)),
            const((QUAD, QUAD)),
            pl.BlockSpec((1, HEAD_DIM, HEAD_DIM, n), lambda i: (i, 0, 0, 0)),
            pl.BlockSpec((bt * N_Q_HEADS, HEAD_DIM), lambda i: (i, 0)),
            pl.BlockSpec((bt, 2 * D_KV), lambda i: (i, 0)),
            pl.BlockSpec((1, 2 * D_KV, bt), lambda i: (i, 0, 0)),
            pl.BlockSpec((bt, D_KV, WINDOW), lambda i: (i, 0, 0)),
            pl.BlockSpec((bt, D_KV, WINDOW), lambda i: (i, 0, 0)),
            const((N_Q_HEADS, 1)),
        ],
        out_specs=[
            pl.BlockSpec((1, HEAD_DIM, HEAD_DIM, n), lambda i: (i, 0, 0, 0)),
            const((n, D_RWKV)),
            pl.BlockSpec((bt, D_ATTN), lambda i: (i, 0)),
            pl.BlockSpec((bt, D_KV, WINDOW), lambda i: (i, 0, 0)),
            pl.BlockSpec((bt, D_KV, WINDOW), lambda i: (i, 0, 0)),
        ],
        out_shape=[
            jax.ShapeDtypeStruct((H_RWKV, HEAD_DIM, HEAD_DIM, n), F32),
            jax.ShapeDtypeStruct((n, D_RWKV), F32),
            jax.ShapeDtypeStruct((n, D_ATTN), F32),
            jax.ShapeDtypeStruct((n, D_KV, WINDOW), F32),
            jax.ShapeDtypeStruct((n, D_KV, WINDOW), F32),
        ],
        scratch_shapes=[pltpu.VMEM((D_RWKV, n), F32)],
        compiler_params=pltpu.CompilerParams(
            dimension_semantics=("arbitrary",), vmem_limit_bytes=VMEM_LIMIT),
        name="decode_step",
    )(vec_t, gb, prm, bmask, s_t, q_r, kvn, cols, ck_t, cv_t, sinks_col)


def kernel(x_prompt, x_sample, state_wkv, state_shift, cache_k_win, cache_v_win, norm_mix_w, w_in, mu_shift, w0, w_decay_up, a0, w_a_up, w_g_up, k_k, k_a, r_k, ln_x_w, ln_x_b, q_norm_w, k_norm_w, sinks, w_out, norm_ffn_w, w_ffn_up, w_ffn_down):
    bsz, t, _ = x_prompt.shape
    nd = x_sample.shape[0]
    l = 0

    w_in_pad = jnp.swapaxes(w_in[l], 0, 1).astype(BF16)
    mu_pad = mu_shift[l][None, :]
    wl = jnp.zeros((D_LORA_PAD, 3 * D_RWKV), F32)
    wl = wl.at[0:32, 0:D_RWKV].set(w_decay_up[l])
    wl = wl.at[32:64, D_RWKV:2 * D_RWKV].set(w_a_up[l])
    wl = wl.at[64:160, 2 * D_RWKV:].set(w_g_up[l])
    wl = wl.astype(BF16)
    prm = jnp.zeros((16, D_RWKV), F32)
    prm = prm.at[0].set(w0[l]).at[1].set(a0[l]).at[2].set(k_k[l]).at[3].set(k_a[l])
    prm = prm.at[4].set(r_k[l].reshape(-1)).at[5].set(ln_x_w[l]).at[6].set(ln_x_b[l])
    hid = np.arange(QUAD) // HEAD_DIM
    bmask = jnp.asarray(hid[:, None] == hid[None, :], BF16)
    tri = jnp.asarray(np.tril(np.ones((CHUNK, CHUNK))), BF16)
    qw = jnp.tile(q_norm_w[l][None, :], (1, N_Q_HEADS))
    kw = jnp.tile(k_norm_w[l][None, :], (1, N_KV_HEADS))
    nmw = norm_mix_w[l][None, :]
    nfw = norm_ffn_w[l][None, :]
    wo = w_out[l].astype(BF16)
    wu = w_ffn_up[l].astype(BF16)
    wd = w_ffn_down[l].astype(BF16)
    tab_a, tab_b = _rope_block_tables(max(t, PAST_LEN + 1) // WINDOW + 1)
    ta, tb = tab_a[PAST_LEN // WINDOW], tab_b[PAST_LEN % WINDOW][None, :]
    tab_s = jnp.concatenate([ta[:, :128] * tb[:, :128] - ta[:, 128:] * tb[:, 128:],
                             ta[:, 128:] * tb[:, :128] + ta[:, :128] * tb[:, 128:]], axis=1)

    xs = x_sample.reshape(nd, D_MODEL)
    p_s, q_s, kv_s = _inproj(xs, nmw, w_in_pad, 128)
    shift_in = state_shift[l].reshape(nd, D_SHIFT)
    vec_t, gb, qn_s, kvn_s = _decode_prep(p_s, shift_in, q_s, kv_s, mu_pad, prm, wl, bmask, tab_s, qw, kw)
    s_t = jnp.transpose(state_wkv[l], (1, 2, 3, 0))
    ck_t = jnp.swapaxes(cache_k_win[l].reshape(nd, WINDOW, D_KV), 1, 2)
    cv_t = jnp.swapaxes(cache_v_win[l].reshape(nd, WINDOW, D_KV), 1, 2)
    q_r = qn_s.reshape(nd * N_Q_HEADS, HEAD_DIM)
    cols = jnp.swapaxes(kvn_s.reshape(nd // DEC_TILE, DEC_TILE, 2 * D_KV), 1, 2)
    wkv_t, yr_s, ya_s, kc_t, vc_t = _decode_step(vec_t, gb, prm, bmask, s_t, q_r, kvn_s, cols, ck_t, cv_t,
                                                 sinks[l][:, None])

    xp = x_prompt.reshape(bsz * t, D_MODEL)
    xs_p, plast, q_p, kv_p = _inproj_shift(xp, nmw, w_in_pad, mu_pad, t, 1024)
    yr_p, hbd = _rwkv_prompt(xs_p.reshape(bsz, t, D_SHIFT), prm, wl, tri, bmask)
    y_prompt, y_s, kwin_p, vwin_p = _attn_ffn(q_p.reshape(bsz, t, D_ATTN), kv_p.reshape(bsz, t, 2 * D_KV),
                                              tab_a, tab_b, qw, kw, sinks[l][None, :], bmask, _band_bias(),
                                              x_prompt, yr_p, xs, yr_s, ya_s, wo, nfw, wu, wd)
    hb = hbd.reshape(bsz, 2, 4, HEAD_DIM, 2, HEAD_DIM)
    wkv_prompt = jnp.stack([hb[:, :, j, :, j % 2, :] for j in range(4)], axis=2)
    wkv_prompt = wkv_prompt.reshape(bsz, H_RWKV, HEAD_DIM, HEAD_DIM)[None]
    shift_prompt = plast[:, 0:1, :][None]
    k_win_prompt = jnp.swapaxes(kwin_p, 1, 2).reshape(bsz, WINDOW, N_KV_HEADS, HEAD_DIM)[None]
    v_win_prompt = jnp.swapaxes(vwin_p, 1, 2).reshape(bsz, WINDOW, N_KV_HEADS, HEAD_DIM)[None]

    y_sample = y_s.reshape(nd, 1, D_MODEL)
    wkv_sample = jnp.transpose(wkv_t, (3, 0, 1, 2))[None]
    shift_sample = p_s.reshape(nd, 1, D_SHIFT)[None]
    k_win_sample = jnp.swapaxes(kc_t, 1, 2).reshape(nd, WINDOW, N_KV_HEADS, HEAD_DIM)[None]
    v_win_sample = jnp.swapaxes(vc_t, 1, 2).reshape(nd, WINDOW, N_KV_HEADS, HEAD_DIM)[None]

    return (y_prompt, y_sample, wkv_prompt, shift_prompt, k_win_prompt, v_win_prompt,
            wkv_sample, shift_sample, k_win_sample, v_win_sample)
```

```python
import functools

import jax
import jax.numpy as jnp
import numpy as np
from jax import lax
from jax.experimental import pallas as pl
from jax.experimental.pallas import tpu as pltpu

F32 = jnp.float32
BF16 = jnp.bfloat16

D_MODEL = 1024
D_RWKV = 512
D_ATTN = 512
HEAD_DIM = 64
H_RWKV = 8
N_Q_HEADS = 8
N_KV_HEADS = 2
GQA_GROUP = 4
D_KV = 128
D_LORA = 160
D_LORA_PAD = 256
D_SHIFT = 3 * D_RWKV + D_LORA
D_IN = D_SHIFT + D_ATTN + 2 * D_KV
WINDOW = 128
ROPE_DIM = 16
ROPE_HALF = 8
ROPE_THETA = 500000.0
ATTN_SCALE = HEAD_DIM ** -0.5
D_FF = 4096
RMS_EPS = 1e-6
LNX_EPS = 64e-5
NEG_INF = -1e30
LOG2E = 1.4426950408889634
PAST_LEN = 16384

CHUNK = 64
QUAD = 4 * HEAD_DIM
V7X_VMEM_BYTES = 64 * 1024 * 1024
VMEM_LIMIT = V7X_VMEM_BYTES // 8 * 7


def _split2(x):
    hi = x.astype(BF16)
    lo = (x - hi.astype(F32)).astype(BF16)
    return hi, lo


def _head_sums(xs, ones_ref):
    n, w = xs[0].shape
    tile = min(w, QUAD)
    per = w // tile
    pieces = [x[:, j * tile:(j + 1) * tile] for x in xs for j in range(per)]
    stacked = jnp.concatenate(pieces, axis=0) if len(pieces) > 1 else pieces[0]
    ones = ones_ref[0:tile, 0:tile]
    out = jnp.dot(stacked.astype(BF16), ones, preferred_element_type=F32)
    res = []
    for i in range(len(xs)):
        cols = [out[(i * per + j) * n:(i * per + j + 1) * n] for j in range(per)]
        res.append(jnp.concatenate(cols, axis=1) if per > 1 else cols[0])
    return res


def _cumsum_rows(tri_bf16, x):
    hi, lo = _split2(x)
    return (jnp.dot(tri_bf16, hi, preferred_element_type=F32)
            + jnp.dot(tri_bf16, lo, preferred_element_type=F32))


def _mm(a, b):
    return jnp.dot(a.astype(BF16), b.astype(BF16), preferred_element_type=F32)


def _mm_nt(a, b):
    return lax.dot_general(a.astype(BF16), b.astype(BF16), (((1,), (1,)), ((), ())),
                           preferred_element_type=F32)


def _mm_tn(a, b):
    return lax.dot_general(a.astype(BF16), b.astype(BF16), (((0,), (0,)), ((), ())),
                           preferred_element_type=F32)


def _dot_nt_f32(a, b):
    return lax.dot_general(a, b, (((1,), (1,)), ((), ())), preferred_element_type=F32)


def _sigmoid(x):
    return 1.0 / (1.0 + jnp.exp(-x))


def _interleave(*gens):
    live = list(gens)
    while live:
        for g in list(live):
            try:
                next(g)
            except StopIteration:
                live.remove(g)


def _norm_project(x, nw_ref, wt_ref):
    ms = jnp.mean(x * x, axis=-1, keepdims=True)
    h = ((x * lax.rsqrt(ms + RMS_EPS)) * nw_ref[...]).astype(BF16)
    return _mm_nt(h, wt_ref[0:D_SHIFT, :]), _mm_nt(h, wt_ref[D_SHIFT:, :])


def _inproj_shift_kernel(tiles_per_seq, x_ref, nw_ref, w_ref, mu_ref, xs_ref, last_ref, q_ref, kv_ref, prev_ref):
    i = pl.program_id(0)

    @pl.when(i % tiles_per_seq == 0)
    def _():
        prev_ref[...] = jnp.zeros_like(prev_ref)

    p, qkv = _norm_project(x_ref[...], nw_ref, w_ref)
    tm = p.shape[0]
    row = lax.broadcasted_iota(jnp.int32, p.shape, 0)
    prev = jnp.where(row == 0, jnp.broadcast_to(prev_ref[0:1, :], p.shape), pltpu.roll(p, 1, 0))
    xs_ref[...] = p + (prev - p) * mu_ref[...]
    last = jnp.broadcast_to(p[tm - 1:tm, :], prev_ref.shape)
    prev_ref[...] = last
    last_ref[0] = last
    q_ref[...] = qkv[:, :D_ATTN]
    kv_ref[...] = qkv[:, D_ATTN:]


def _inproj_shift(x2d, norm_w, w_in_pad, mu_pad, seq_len, tm):
    m = x2d.shape[0]
    tiles_per_seq = seq_len // tm
    return pl.pallas_call(
        functools.partial(_inproj_shift_kernel, tiles_per_seq),
        grid=(m // tm,),
        in_specs=[
            pl.BlockSpec((tm, D_MODEL), lambda i: (i, 0)),
            pl.BlockSpec((1, D_MODEL), lambda i: (0, 0)),
            pl.BlockSpec((D_IN, D_MODEL), lambda i: (0, 0)),
            pl.BlockSpec((1, D_SHIFT), lambda i: (0, 0)),
        ],
        out_specs=[
            pl.BlockSpec((tm, D_SHIFT), lambda i: (i, 0)),
            pl.BlockSpec((1, 8, D_SHIFT), lambda i: (i // tiles_per_seq, 0, 0)),
            pl.BlockSpec((tm, D_ATTN), lambda i: (i, 0)),
            pl.BlockSpec((tm, 2 * D_KV), lambda i: (i, 0)),
        ],
        out_shape=[
            jax.ShapeDtypeStruct((m, D_SHIFT), F32),
            jax.ShapeDtypeStruct((m // seq_len, 8, D_SHIFT), F32),
            jax.ShapeDtypeStruct((m, D_ATTN), F32),
            jax.ShapeDtypeStruct((m, 2 * D_KV), F32),
        ],
        scratch_shapes=[pltpu.VMEM((8, D_SHIFT), F32)],
        compiler_params=pltpu.CompilerParams(
            dimension_semantics=("arbitrary",), vmem_limit_bytes=VMEM_LIMIT),
        name="inproj_shift",
    )(x2d, norm_w, w_in_pad, mu_pad)


def _rwkv_features(xs, prm, wl_ref, ones_ref):
    r = xs[:, 0:D_RWKV]
    k = xs[:, D_RWKV:2 * D_RWKV]
    v = xs[:, 2 * D_RWKV:3 * D_RWKV]
    lora = xs[:, 3 * D_RWKV:]
    lora = jnp.concatenate([lora, jnp.zeros((lora.shape[0], wl_ref.shape[0] - D_LORA), F32)], axis=1)
    col = lax.broadcasted_iota(jnp.int32, lora.shape, 1)
    act = jnp.where(col < 32, jnp.tanh(lora), jnp.where(col < 64, lora, _sigmoid(lora)))
    up = jnp.dot(act.astype(BF16), wl_ref[...], preferred_element_type=F32)
    w0, a0, k_k, k_a, r_k = prm[0:1], prm[1:2], prm[2:3], prm[3:4], prm[4:5]
    logw = (-np.exp(-0.5)) * _sigmoid(w0 + up[:, 0:D_RWKV])
    asig = _sigmoid(a0 + up[:, D_RWKV:2 * D_RWKV])
    g = up[:, 2 * D_RWKV:]
    kk = k * k_k
    k_mod = k * (1.0 + (asig - 1.0) * k_a)
    ss, rk = _head_sums([kk * kk, r * k_mod * r_k], ones_ref)
    kk = kk / jnp.maximum(jnp.sqrt(ss), 1e-12)
    k = k_mod
    bonus = rk * v
    return r, logw, k, v, -kk, kk * asig, g, bonus


def _rwkv_finish(y, g, bonus, prm, ones_ref):
    ln_w, ln_b = prm[5:6], prm[6:7]
    mean = _head_sums([y], ones_ref)[0] * (1.0 / HEAD_DIM)
    d = y - mean
    var = _head_sums([d * d], ones_ref)[0] * (1.0 / HEAD_DIM)
    yn = d * lax.rsqrt(var + LNX_EPS) * ln_w + ln_b
    return (yn + bonus) * g


def _block_diag(x, bmask):
    return jnp.concatenate([x] * 4, axis=0) * bmask


def _chunk_prep(r, logw, k, v, a, b, tri_ref, bmask):
    n = len(r)
    ch = range(n)
    tri = tri_ref[...]
    cum = [_cumsum_rows(tri, logw[i]) for i in ch]
    yield
    e_in = [jnp.exp(cum[i]) for i in ch]
    e_ex = [jnp.exp(cum[i] - logw[i]) for i in ch]
    e_inv = [1.0 / e_in[i] for i in ch]
    e_last = [e_in[i][CHUNK - 1:CHUNK, :] for i in ch]
    rt = [(r[i] * e_in[i]).astype(BF16) for i in ch]
    at = [(a[i] * e_ex[i]).astype(BF16) for i in ch]
    kt = [(k[i] * e_inv[i]).astype(BF16) for i in ch]
    bt = [(b[i] * e_inv[i]).astype(BF16) for i in ch]
    vb = [v[i].astype(BF16) for i in ch]
    yield

    t_idx = lax.broadcasted_iota(jnp.int32, (CHUNK, QUAD), 0)
    s_idx = lax.broadcasted_iota(jnp.int32, (CHUNK, QUAD), 1) & (HEAD_DIM - 1)
    strict = s_idx < t_idx
    incl = s_idx <= t_idx

    gm = [_mm_nt(jnp.concatenate([at[i], rt[i]], axis=0),
                 jnp.concatenate([_block_diag(bt[i], bmask), _block_diag(kt[i], bmask)], axis=0))
          for i in ch]
    yield
    a_ab = [jnp.where(strict, gm[i][:CHUNK, :QUAD], 0.0) for i in ch]
    a_ak = [jnp.where(strict, gm[i][:CHUNK, QUAD:], 0.0) for i in ch]
    a_rb = [jnp.where(incl, gm[i][CHUNK:, :QUAD], 0.0) for i in ch]
    a_rk = [jnp.where(incl, gm[i][CHUNK:, QUAD:], 0.0) for i in ch]

    eye = jnp.where(s_idx == t_idx, 1.0, 0.0)
    pwb = [a_ab[i].astype(BF16) for i in ch]
    t_inv = [eye + a_ab[i] for i in ch]
    for it in range(6):
        rbd = [_block_diag(pwb[i], bmask) for i in ch]
        if it == 0:
            pwb = [_mm(pwb[i], rbd[i]).astype(BF16) for i in ch]
        elif it < 5:
            out = [_mm(jnp.concatenate([pwb[i], t_inv[i].astype(BF16)], axis=0), rbd[i]) for i in ch]
            pwb = [out[i][:CHUNK].astype(BF16) for i in ch]
            t_inv = [t_inv[i] + out[i][CHUNK:] for i in ch]
        else:
            t_inv = [t_inv[i] + _mm(t_inv[i], rbd[i]) for i in ch]
        yield

    vbd = [_block_diag(vb[i], bmask) for i in ch]
    xy0 = [_mm(jnp.concatenate([a_ak[i], a_rk[i]], axis=0), vbd[i]) for i in ch]
    return [dict(ar=jnp.concatenate([at[i], rt[i]], axis=0), x0=xy0[i][:CHUNK], y0=xy0[i][CHUNK:],
                 t_inv=t_inv[i].astype(BF16), a_rb=a_rb[i].astype(BF16), vb=vb[i],
                 bk=jnp.concatenate([bt[i], kt[i]], axis=0), e_last=e_last[i]) for i in ch]


def _chunk_step(pre, state, bmask, out):
    ch = range(len(pre))
    half = QUAD // 2
    zeros = jnp.zeros((half, half), BF16)
    sc = [state[i].astype(BF16) for i in ch]
    sb = [jnp.concatenate([jnp.concatenate([sc[i][:half], zeros], axis=1),
                           jnp.concatenate([zeros, sc[i][half:]], axis=1)], axis=0) for i in ch]
    xr = [_mm_nt(pre[i]["ar"], sb[i]) for i in ch]
    yield
    x = [xr[i][:CHUNK] + pre[i]["x0"] for i in ch]
    u = [_mm(pre[i]["t_inv"], _block_diag(x[i].astype(BF16), bmask)) for i in ch]
    yield
    ub = [u[i].astype(BF16) for i in ch]
    out["y"] = [xr[i][CHUNK:] + pre[i]["y0"] + _mm(pre[i]["a_rb"], _block_diag(ub[i], bmask)) for i in ch]
    upd = [_mm_tn(jnp.concatenate([ub[i], pre[i]["vb"]], axis=0), pre[i]["bk"]) for i in ch]
    yield
    bm = bmask[:half, :half].astype(F32)
    s_new = []
    for i in ch:
        e_last = pre[i]["e_last"]
        top = (state[i][:half] + upd[i][:half, :half] * bm) * e_last[:, :half]
        bot = (state[i][half:] + upd[i][half:, half:] * bm) * e_last[:, half:]
        s_new.append(jnp.concatenate([top, bot], axis=0))
    out["state"] = s_new


def _finish_stages(ys, g, bonus, prm, ones_ref, y_ref, j):
    nseq = len(ys) // 2
    ln_w, ln_b = prm[5:6], prm[6:7]
    y = jnp.concatenate([jnp.concatenate(ys[2 * s:2 * s + 2], axis=1) for s in range(nseq)], axis=0)
    mean = _head_sums([y], ones_ref)[0] * (1.0 / HEAD_DIM)
    yield
    d = y - mean
    var = _head_sums([d * d], ones_ref)[0] * (1.0 / HEAD_DIM)
    yield
    out = (d * lax.rsqrt(var + LNX_EPS) * ln_w + ln_b + bonus) * g
    for s in range(nseq):
        y_ref[s, j * CHUNK:(j + 1) * CHUNK, :] = out[s * CHUNK:(s + 1) * CHUNK]


def _rwkv_prompt_kernel(xs_ref, prm_ref, wl_ref, tri_ref, bmask_ref, y_ref, hout_ref, h_ref):
    c = pl.program_id(0)
    nseq, tstep, _ = xs_ref.shape
    nchunk = tstep // CHUNK

    @pl.when(c == 0)
    def _():
        h_ref[...] = jnp.zeros_like(h_ref)

    xs = jnp.concatenate([xs_ref[s] for s in range(nseq)], axis=0)
    prm = prm_ref[...]
    r, logw, k, v, a, b, g, bonus = _rwkv_features(xs, prm, wl_ref, bmask_ref)
    bmask = bmask_ref[...]
    lanes = [(s, q) for s in range(nseq) for q in range(2)]
    pre = {}

    def prep(chunks):
        chains = [(j, s, q) for j in chunks for s, q in lanes]
        cut = lambda x: [x[s * tstep + j * CHUNK:s * tstep + (j + 1) * CHUNK, q * QUAD:(q + 1) * QUAD]
                         for j, s, q in chains]
        res = yield from _chunk_prep(cut(r), cut(logw), cut(k), cut(v), cut(a), cut(b), tri_ref, bmask)
        for n, j in enumerate(chunks):
            pre[j] = res[n * len(lanes):(n + 1) * len(lanes)]

    nfirst = max(nchunk - 1, 1)
    _interleave(prep(range(nfirst)))
    later = prep(range(nfirst, nchunk))
    state = [h_ref[s, q] for s, q in lanes]
    crow = lambda x, j: jnp.concatenate([x[s * tstep + j * CHUNK:s * tstep + (j + 1) * CHUNK]
                                         for s in range(nseq)], axis=0)
    finish = iter(())
    for j in range(nchunk):
        res = {}
        for _ in _chunk_step(pre[j], state, bmask, res):
            next(later, None)
            next(finish, None)
        _interleave(finish)
        if j == nfirst - 1:
            _interleave(later)
        state = res["state"]
        finish = _finish_stages(res["y"], crow(g, j), crow(bonus, j), prm, bmask_ref, y_ref, j)
    _interleave(finish)
    for i, (s, q) in enumerate(lanes):
        h_ref[s, q] = state[i]

    @pl.when(c == pl.num_programs(0) - 1)
    def _():
        hout_ref[...] = h_ref[...]


RWKV_CHUNKS_PER_STEP = 4


def _rwkv_prompt(xs3d, prm, wl, tri, bmask):
    bsz, t, _ = xs3d.shape
    tstep = RWKV_CHUNKS_PER_STEP * CHUNK
    nc = t // tstep
    const = lambda shape: pl.BlockSpec(shape, lambda c: (0,) * len(shape))
    state_shape = (bsz, 2, QUAD, QUAD // 2)
    return pl.pallas_call(
        _rwkv_prompt_kernel,
        grid=(nc,),
        in_specs=[
            pl.BlockSpec((bsz, tstep, D_SHIFT), lambda c: (0, c, 0)),
            const((16, D_RWKV)),
            const((D_LORA_PAD, 3 * D_RWKV)),
            const((CHUNK, CHUNK)),
            const((QUAD, QUAD)),
        ],
        out_specs=[
            pl.BlockSpec((bsz, tstep, D_RWKV), lambda c: (0, c, 0)),
            const(state_shape),
        ],
        out_shape=[
            jax.ShapeDtypeStruct((bsz, t, D_RWKV), F32),
            jax.ShapeDtypeStruct(state_shape, F32),
        ],
        scratch_shapes=[pltpu.VMEM(state_shape, F32)],
        compiler_params=pltpu.CompilerParams(
            dimension_semantics=("arbitrary",), vmem_limit_bytes=VMEM_LIMIT),
        name="rwkv_prompt",
    )(xs3d, prm, wl, tri, bmask)


def _rope_lane_freq():
    inv_freq = jnp.power(ROPE_THETA, -jnp.arange(ROPE_HALF, dtype=F32) * (2.0 / ROPE_DIM))
    return inv_freq[(np.arange(2 * HEAD_DIM) % HEAD_DIM) % ROPE_HALF][None, :]


def _rope_tables(cos, sin):
    dim = lax.broadcasted_iota(jnp.int32, cos.shape, 1) & (HEAD_DIM - 1)
    cos_t = jnp.where(dim < ROPE_DIM, cos, 1.0)
    sin_lo = jnp.where(dim < ROPE_HALF, -sin, 0.0)
    sin_hi = jnp.where((dim >= ROPE_HALF) & (dim < ROPE_DIM), sin, 0.0)
    return cos_t, sin_lo, sin_hi


def _qk_norm_rope(x, norm_w, cos_t, sin_lo, sin_hi, ones_ref):
    ms = _head_sums([x * x], ones_ref)[0] * (1.0 / HEAD_DIM)
    xn = x * lax.rsqrt(ms + RMS_EPS) * norm_w
    width = x.shape[1]
    fwd = pltpu.roll(xn, width - ROPE_HALF, 1)
    bwd = pltpu.roll(xn, ROPE_HALF, 1)
    return xn * cos_t + fwd * sin_lo + bwd * sin_hi


def _tile_lanes(x, reps):
    return jnp.concatenate([x] * reps, axis=1) if reps > 1 else x


def _attn_stages(q_ref, kv_ref, taba_ref, tabb_ref, qw_ref, kw_ref, sink_ref, ones_ref, bias_ref, first_bias,
                 kprev_ref, vprev_ref, out):
    nseq, tq, _ = q_ref.shape
    nblk = tq // WINDOW
    units = [(s, b) for s in range(nseq) for b in range(nblk)]
    blk = lambda b: slice(b * WINDOW, (b + 1) * WINDOW)
    tb = tabb_ref[...]
    cos_b, sin_b = tb[:, :128], tb[:, 128:]
    rope, rope4 = [], []
    for b in range(nblk):
        ta = taba_ref[b][0:1, :]
        cos_a, sin_a = ta[:, :128], ta[:, 128:]
        tabs = _rope_tables(cos_a * cos_b - sin_a * sin_b, sin_a * cos_b + cos_a * sin_b)
        rope.append(tabs)
        rope4.append([_tile_lanes(x, 4) for x in tabs])
    q = {(s, b): _qk_norm_rope(q_ref[s, blk(b), :], qw_ref[...], *rope4[b], ones_ref) * (ATTN_SCALE * LOG2E)
         for s, b in units}
    kv = {(s, b): kv_ref[s, blk(b), :] for s, b in units}
    k_cur = {u: _qk_norm_rope(kv[u][:, 0:D_KV], kw_ref[...], *rope[u[1]], ones_ref) for u in units}
    v_cur = {u: kv[u][:, D_KV:] for u in units}
    k_all = {(s, b): jnp.concatenate([kprev_ref[s] if b == 0 else k_cur[(s, b - 1)], k_cur[(s, b)]], axis=0)
             for s, b in units}
    v_all = {(s, b): jnp.concatenate([vprev_ref[s] if b == 0 else v_cur[(s, b - 1)], v_cur[(s, b)]], axis=0)
             for s, b in units}
    out["k_cur"] = [k_cur[(s, nblk - 1)] for s in range(nseq)]
    out["v_cur"] = [v_cur[(s, nblk - 1)] for s in range(nseq)]
    for s in range(nseq):
        kprev_ref[s] = out["k_cur"][s]
        vprev_ref[s] = out["v_cur"][s]
    yield

    nk = 2 * WINDOW
    bias = [bias_ref[first_bias] if b == 0 else bias_ref[1] for b in range(nblk)]
    sinks = sink_ref[...] * LOG2E
    low = lax.broadcasted_iota(jnp.int32, (nk, D_KV), 1) < HEAD_DIM
    lane_blk = [ones_ref[j * HEAD_DIM:j * HEAD_DIM + 1, :] for j in range(GQA_GROUP)]

    chains = [(u, g) for u in units for g in range(N_KV_HEADS)]
    ch = range(len(chains))
    k_rot = {u: pltpu.roll(k_all[u], HEAD_DIM, 1) for u in units}
    k2 = [jnp.where(low, k_all[u], k_rot[u]) if g == 0 else jnp.where(low, k_rot[u], k_all[u])
          for u, g in chains]
    k4 = [jnp.concatenate([k2[c], k2[c]], axis=1).astype(BF16) for c in ch]
    vb = {u: v_all[u].astype(BF16) for u in units}
    qg = [q[u][:, g * QUAD:(g + 1) * QUAD].astype(BF16) for u, g in chains]
    qstack = [jnp.concatenate([qg[c] * lane_blk[j] for j in range(GQA_GROUP)], axis=0) for c in ch]
    sink_row = [jnp.concatenate(
        [jnp.broadcast_to(sinks[:, g * GQA_GROUP + j:g * GQA_GROUP + j + 1], (1, WINDOW))
         for j in range(GQA_GROUP)], axis=1) for u, g in chains]
    yield
    sc = [_mm_nt(k4[c], qstack[c]) + bias[u[1]] for c, (u, g) in enumerate(chains)]
    yield
    m = [jnp.maximum(jnp.max(sc[c], axis=0, keepdims=True), sink_row[c]) for c in ch]
    e = [jnp.exp2(sc[c] - m[c]) for c in ch]
    yield
    denom = [jnp.sum(e[c], axis=0, keepdims=True) + jnp.exp2(sink_row[c] - m[c]) for c in ch]
    ot = {(u, g): _mm_tn(vb[u], e[c].astype(BF16))[g * HEAD_DIM:(g + 1) * HEAD_DIM, :] * (1.0 / denom[c])
          for c, (u, g) in enumerate(chains)}
    yield
    ya = []
    for s in range(nseq):
        blocks = []
        for b in range(nblk):
            yt = jnp.concatenate([ot[((s, b), g)][:, j * WINDOW:(j + 1) * WINDOW]
                                  for g in range(N_KV_HEADS) for j in range(GQA_GROUP)], axis=0)
            blocks.append(jnp.transpose(yt))
        ya.append(jnp.concatenate(blocks, axis=0) if nblk > 1 else blocks[0])
    out["ya"] = ya


def _ffn_stages(x, yr, ya, wo_ref, nw_ref, wu_ref, wd_ref, out, pieces=4):
    mix = jnp.concatenate([yr, ya], axis=1).astype(BF16)
    x1 = x + jnp.dot(mix, wo_ref[...], preferred_element_type=F32)
    yield
    ms = jnp.mean(x1 * x1, axis=-1, keepdims=True)
    hf = ((x1 * lax.rsqrt(ms + RMS_EPS)) * nw_ref[...]).astype(BF16)
    acc = x1
    step = D_FF // pieces
    for j in range(pieces):
        up = jnp.dot(hf, wu_ref[:, j * step:(j + 1) * step], preferred_element_type=F32)
        yield
        act = jnp.square(jnp.maximum(up, 0.0)).astype(BF16)
        acc = acc + jnp.dot(act, wd_ref[j * step:(j + 1) * step, :], preferred_element_type=F32)
        yield
    out["y"] = acc


def _attn_ffn_kernel(q_ref, kv_ref, taba_ref, tabb_ref, qw_ref, kw_ref, sink_ref, ones_ref, bias_ref,
                     x_ref, yr_ref, xd_ref, yrd_ref, yad_ref, wo_ref, nw_ref, wu_ref, wd_ref,
                     o_ref, od_ref, kwin_ref, vwin_ref, kprev_ref, vprev_ref, ya_ref):
    i = pl.program_id(0)
    nseq, tq, _ = q_ref.shape
    seqs = range(nseq)
    first = i == 0

    @pl.when(first)
    def _():
        kprev_ref[...] = jnp.zeros_like(kprev_ref)
        vprev_ref[...] = jnp.zeros_like(vprev_ref)
        ya_ref[...] = jnp.zeros_like(ya_ref)

    def rows(ref, dec_ref):
        tile = jnp.concatenate([ref[s] for s in seqs], axis=0)
        dec = dec_ref[...]
        return jnp.where(first, jnp.concatenate([dec] * (tile.shape[0] // dec.shape[0]), axis=0), tile)

    a_out, f_out = {}, {}
    _interleave(
        _ffn_stages(rows(x_ref, xd_ref), rows(yr_ref, yrd_ref), rows(ya_ref, yad_ref),
                    wo_ref, nw_ref, wu_ref, wd_ref, f_out),
        _attn_stages(q_ref, kv_ref, taba_ref, tabb_ref, qw_ref, kw_ref, sink_ref, ones_ref, bias_ref,
                     jnp.minimum(i, 1), kprev_ref, vprev_ref, a_out))
    for s in seqs:
        o_ref[s] = f_out["y"][s * tq:(s + 1) * tq]
        ya_ref[s] = a_out["ya"][s]

    @pl.when(first)
    def _():
        od_ref[...] = f_out["y"][:od_ref.shape[0]]

    @pl.when(i == pl.num_programs(0) - 2)
    def _():
        for s in seqs:
            kwin_ref[s] = jnp.transpose(a_out["k_cur"][s])
            vwin_ref[s] = jnp.transpose(a_out["v_cur"][s])


def _band_bias():
    ki = np.arange(2 * WINDOW)[:, None]
    qi = (np.arange(GQA_GROUP * WINDOW) % WINDOW + WINDOW)[None, :]
    dq = qi - ki
    band = (dq >= 0) & (dq < WINDOW)
    first = band & (ki >= WINDOW)
    return jnp.asarray(np.where(np.stack([first, band]), 0.0, NEG_INF), F32)


def _rope_block_tables(nb):
    freq = _rope_lane_freq()
    ang_a = (jnp.arange(nb, dtype=F32) * WINDOW)[:, None] * freq
    ang_b = jnp.arange(WINDOW, dtype=F32)[:, None] * freq
    tab_a = jnp.concatenate([jnp.cos(ang_a), jnp.sin(ang_a)], axis=1)
    tab_b = jnp.concatenate([jnp.cos(ang_b), jnp.sin(ang_b)], axis=1)
    return jnp.broadcast_to(tab_a[:, None, :], (nb, 8, 4 * HEAD_DIM)), tab_b


ATTN_BLOCKS_PER_STEP = 1


def _attn_ffn(q3d, kv3d, tab_a, tab_b, qw, kw, sinks, ones_bd, bias, x3d, yr3d, xd, yrd, yad, wo, nw, wu, wd):
    bsz, t, _ = q3d.shape
    nd = xd.shape[0]
    nblk = ATTN_BLOCKS_PER_STEP
    tq = nblk * WINDOW
    nt = t // tq
    const = lambda shape: pl.BlockSpec(shape, lambda i: (0,) * len(shape))
    single = lambda shape: pl.BlockSpec(shape, lambda i: (0,) * len(shape), pipeline_mode=pl.Buffered(1))
    cur = lambda i: jnp.minimum(i, nt - 1)
    prv = lambda i: jnp.maximum(i - 1, 0)
    return pl.pallas_call(
        _attn_ffn_kernel,
        grid=(nt + 1,),
        in_specs=[
            pl.BlockSpec((bsz, tq, D_ATTN), lambda i: (0, cur(i), 0)),
            pl.BlockSpec((bsz, tq, 2 * D_KV), lambda i: (0, cur(i), 0)),
            pl.BlockSpec((nblk, 8, 4 * HEAD_DIM), lambda i: (cur(i), 0, 0)),
            const((WINDOW, 4 * HEAD_DIM)),
            const((1, D_ATTN)),
            const((1, D_KV)),
            const((1, N_Q_HEADS)),
            const((QUAD, QUAD)),
            const((2, 2 * WINDOW, GQA_GROUP * WINDOW)),
            pl.BlockSpec((bsz, tq, D_MODEL), lambda i: (0, prv(i), 0)),
            pl.BlockSpec((bsz, tq, D_RWKV), lambda i: (0, prv(i), 0)),
            const((nd, D_MODEL)),
            const((nd, D_RWKV)),
            const((nd, D_ATTN)),
            single((D_MODEL, D_MODEL)),
            const((1, D_MODEL)),
            single((D_MODEL, D_FF)),
            single((D_FF, D_MODEL)),
        ],
        out_specs=[
            pl.BlockSpec((bsz, tq, D_MODEL), lambda i: (0, prv(i), 0)),
            const((nd, D_MODEL)),
            const((bsz, WINDOW, D_KV)),
            const((bsz, WINDOW, D_KV)),
        ],
        out_shape=[
            jax.ShapeDtypeStruct((bsz, t, D_MODEL), F32),
            jax.ShapeDtypeStruct((nd, D_MODEL), F32),
            jax.ShapeDtypeStruct((bsz, WINDOW, D_KV), F32),
            jax.ShapeDtypeStruct((bsz, WINDOW, D_KV), F32),
        ],
        scratch_shapes=[
            pltpu.VMEM((bsz, WINDOW, D_KV), F32),
            pltpu.VMEM((bsz, WINDOW, D_KV), F32),
            pltpu.VMEM((bsz, tq, D_ATTN), F32),
        ],
        compiler_params=pltpu.CompilerParams(
            dimension_semantics=("arbitrary",), vmem_limit_bytes=VMEM_LIMIT),
        name="attn_ffn",
    )(q3d, kv3d, tab_a, tab_b, qw, kw, sinks, ones_bd, bias, x3d, yr3d, xd, yrd, yad, wo, nw, wu, wd)


DEC_TILE = 16


def _decode_prep_kernel(x_ref, nw_ref, w_ref, sh_ref, mu_ref, prm_ref, wl_ref, ones_ref, tab_ref,
                        qw_ref, kw_ref, p_ref, vec_ref, vgb_ref, qn_ref, kvn_ref):
    p, qkv = _norm_project(x_ref[...], nw_ref, w_ref)
    p_ref[...] = p
    xs = p + (sh_ref[...] - p) * mu_ref[...]
    r, logw, k, v, a, b, g, bonus = _rwkv_features(xs, prm_ref[...], wl_ref, ones_ref)
    for i, x in enumerate((a, b, k, jnp.exp(logw), r, v)):
        vec_ref[i] = jnp.transpose(x)
    vgb_ref[0] = g
    vgb_ref[1] = bonus
    n = p.shape[0]
    tab = jnp.broadcast_to(tab_ref[0:1, :], (n, 4 * HEAD_DIM))
    cos_t, sin_lo, sin_hi = _rope_tables(tab[:, :128], tab[:, 128:])
    qn_ref[...] = _qk_norm_rope(qkv[:, :D_ATTN], qw_ref[...], _tile_lanes(cos_t, 4), _tile_lanes(sin_lo, 4),
                                _tile_lanes(sin_hi, 4), ones_ref)
    kvn_ref[:, 0:D_KV] = _qk_norm_rope(qkv[:, D_ATTN:D_ATTN + D_KV], kw_ref[...], cos_t, sin_lo, sin_hi,
                                       ones_ref)
    kvn_ref[:, D_KV:] = qkv[:, D_ATTN + D_KV:]


def _decode_prep(x2d, norm_w, w_in_t, shift, mu_pad, prm, wl, bmask, tab, qw, kw):
    n = x2d.shape[0]
    full = lambda shape: pl.BlockSpec(shape, lambda i: (0,) * len(shape))
    return pl.pallas_call(
        _decode_prep_kernel,
        grid=(1,),
        in_specs=[full((n, D_MODEL)), full((1, D_MODEL)), full((D_IN, D_MODEL)), full((n, D_SHIFT)),
                  full((1, D_SHIFT)), full((16, D_RWKV)), full((D_LORA_PAD, 3 * D_RWKV)),
                  full((QUAD, QUAD)), full((8, 4 * HEAD_DIM)), full((1, D_ATTN)), full((1, D_KV))],
        out_specs=[full((n, D_SHIFT)), full((6, D_RWKV, n)), full((2, n, D_RWKV)), full((n, D_ATTN)),
                   full((n, 2 * D_KV))],
        out_shape=[
            jax.ShapeDtypeStruct((n, D_SHIFT), F32),
            jax.ShapeDtypeStruct((6, D_RWKV, n), F32),
            jax.ShapeDtypeStruct((2, n, D_RWKV), F32),
            jax.ShapeDtypeStruct((n, D_ATTN), F32),
            jax.ShapeDtypeStruct((n, 2 * D_KV), F32),
        ],
        compiler_params=pltpu.CompilerParams(
            dimension_semantics=("arbitrary",), vmem_limit_bytes=VMEM_LIMIT),
        name="decode_prep",
    )(x2d, norm_w, w_in_t, shift, mu_pad, prm, wl, bmask, tab, qw, kw)


def _decode_state_stages(vec_ref, s_ref, sout_ref, yt_ref, h):
    a_t, b_t, k_t, w_t, r_t = (vec_ref[i] for i in range(5))
    for i in range(HEAD_DIM):
        s = s_ref[0, i]
        sa = jnp.sum(s * a_t, axis=0, keepdims=True)
        v_i = vec_ref[5, i:i + 1, :]
        s_new = s * w_t + sa * b_t + v_i * k_t
        sout_ref[0, i] = s_new
        yt_ref[pl.ds(h * HEAD_DIM + i, 1), :] = jnp.sum(s_new * r_t, axis=0, keepdims=True)
        if i % 8 == 7:
            yield


def _decode_attn_stages(qr_ref, kvn_ref, col_ref, ck_ref, cv_ref, sink_ref, ya_ref, kout_ref, vout_ref):
    nh = N_Q_HEADS
    seqs = range(DEC_TILE)
    hrow = lax.broadcasted_iota(jnp.int32, (nh, D_ATTN), 0)
    hlane = lax.broadcasted_iota(jnp.int32, (nh, D_ATTN), 1) // HEAD_DIM
    dmask = hrow == hlane
    grow = lax.broadcasted_iota(jnp.int32, (nh, D_KV), 0) // GQA_GROUP
    glane = lax.broadcasted_iota(jnp.int32, (nh, D_KV), 1) // HEAD_DIM
    gmask = grow == glane
    low = glane == 0
    key_idx = lax.broadcasted_iota(jnp.int32, (nh, WINDOW), 1)
    last = lax.broadcasted_iota(jnp.int32, (D_KV, WINDOW), 1) == WINDOW - 1
    sink = sink_ref[...]
    kvn = kvn_ref[...]
    col = col_ref[0]
    k_new = [kvn[j:j + 1, 0:D_KV] for j in seqs]
    v_new = [kvn[j:j + 1, D_KV:] for j in seqs]
    ck = [ck_ref[j] for j in seqs]
    cv = [cv_ref[j] for j in seqs]
    for j in seqs:
        kout_ref[j] = jnp.where(last, col[0:D_KV, j:j + 1], pltpu.roll(ck[j], WINDOW - 1, 1))
        vout_ref[j] = jnp.where(last, col[D_KV:, j:j + 1], pltpu.roll(cv[j], WINDOW - 1, 1))
    yield
    q8 = [qr_ref[j * nh:(j + 1) * nh, :] for j in seqs]
    qp = [jnp.where(gmask, jnp.concatenate([q8[j], q8[j]], axis=1), 0.0) for j in seqs]
    s_c = [jnp.where(key_idx >= 1, jnp.dot(qp[j], ck[j], preferred_element_type=F32) * ATTN_SCALE, NEG_INF)
           for j in seqs]
    s_n = [jnp.sum(qp[j] * k_new[j], axis=-1, keepdims=True) * ATTN_SCALE for j in seqs]
    yield
    m = [jnp.maximum(jnp.maximum(jnp.max(s_c[j], axis=-1, keepdims=True), s_n[j]), sink) for j in seqs]
    yield
    e_c = [jnp.exp(s_c[j] - m[j]) for j in seqs]
    e_n = [jnp.exp(s_n[j] - m[j]) for j in seqs]
    denom = [jnp.sum(e_c[j], axis=-1, keepdims=True) + e_n[j] + jnp.exp(sink - m[j]) for j in seqs]
    yield
    o = [(_dot_nt_f32(e_c[j], cv[j]) + e_n[j] * v_new[j]) / denom[j]
         for j in seqs]
    yield
    out_rows = []
    for j in seqs:
        rot = pltpu.roll(o[j], HEAD_DIM, 1)
        g0 = jnp.where(low, o[j], rot)
        g1 = jnp.where(low, rot, o[j])
        wide = jnp.concatenate([g0, g0, g1, g1], axis=1)
        out_rows.append(jnp.sum(jnp.where(dmask, wide, 0.0), axis=0, keepdims=True))
    ya_ref[...] = jnp.concatenate(out_rows, axis=0)


def _decode_step_kernel(vec_ref, gb_ref, prm_ref, ones_ref, s_ref, qr_ref, kvn_ref, col_ref, ck_ref, cv_ref,
                        sink_ref, sout_ref, yr_ref, ya_ref, kout_ref, vout_ref, yt_ref):
    i = pl.program_id(0)
    _interleave(
        _decode_attn_stages(qr_ref, kvn_ref, col_ref, ck_ref, cv_ref, sink_ref, ya_ref, kout_ref, vout_ref),
        _decode_state_stages(vec_ref, s_ref, sout_ref, yt_ref, i))

    @pl.when(i == pl.num_programs(0) - 1)
    def _():
        y = jnp.transpose(yt_ref[...])
        yr_ref[...] = _rwkv_finish(y, gb_ref[0], gb_ref[1], prm_ref[...], ones_ref)


def _decode_step(vec_t, gb, prm, bmask, s_t, q_r, kvn, cols, ck_t, cv_t, sinks_col):
    n = s_t.shape[-1]
    bt = DEC_TILE
    assert n // bt == H_RWKV, "one sequence tile per RWKV head"
    const = lambda shape: pl.BlockSpec(shape, lambda i: (0,) * len(shape))
    return pl.pallas_call(
        _decode_step_kernel,
        grid=(H_RWKV,),
        in_specs=[
            pl.BlockSpec((6, HEAD_DIM, n), lambda i: (0, i, 0)),
            const((2, n, D_RWKV)),
            const((16, D_RWKV)),
            const((QUAD, QUAD)),
            pl.BlockSpec((1, HEAD_DIM, HEAD_DIM, n), lambda i: (i, 0, 0, 0)),
            pl.BlockSpec((bt * N_Q_HEADS, HEAD_DIM), lambda i: (i, 0)),
            pl.BlockSpec((bt, 2 * D_KV), lambda i: (i, 0)),
            pl.BlockSpec((1, 2 * D_KV, bt), lambda i: (i, 0, 0)),
            pl.BlockSpec((bt, D_KV, WINDOW), lambda i: (i, 0, 0)),
            pl.BlockSpec((bt, D_KV, WINDOW), lambda i: (i, 0, 0)),
            const((N_Q_HEADS, 1)),
        ],
        out_specs=[
            pl.BlockSpec((1, HEAD_DIM, HEAD_DIM, n), lambda i: (i, 0, 0, 0)),
            const((n, D_RWKV)),
            pl.BlockSpec((bt, D_ATTN), lambda i: (i, 0)),
            pl.BlockSpec((bt, D_KV, WINDOW), lambda i: (i, 0, 0)),
            pl.BlockSpec((bt, D_KV, WINDOW), lambda i: (i, 0, 0)),
        ],
        out_shape=[
            jax.ShapeDtypeStruct((H_RWKV, HEAD_DIM, HEAD_DIM, n), F32),
            jax.ShapeDtypeStruct((n, D_RWKV), F32),
            jax.ShapeDtypeStruct((n, D_ATTN), F32),
            jax.ShapeDtypeStruct((n, D_KV, WINDOW), F32),
            jax.ShapeDtypeStruct((n, D_KV, WINDOW), F32),
        ],
        scratch_shapes=[pltpu.VMEM((D_RWKV, n), F32)],
        compiler_params=pltpu.CompilerParams(
            dimension_semantics=("arbitrary",), vmem_limit_bytes=VMEM_LIMIT),
        name="decode_step",
    )(vec_t, gb, prm, bmask, s_t, q_r, kvn, cols, ck_t, cv_t, sinks_col)


def kernel(x_prompt, x_sample, state_wkv, state_shift, cache_k_win, cache_v_win, norm_mix_w, w_in, mu_shift, w0, w_decay_up, a0, w_a_up, w_g_up, k_k, k_a, r_k, ln_x_w, ln_x_b, q_norm_w, k_norm_w, sinks, w_out, norm_ffn_w, w_ffn_up, w_ffn_down):
    bsz, t, _ = x_prompt.shape
    nd = x_sample.shape[0]
    l = 0

    w_in_pad = jnp.swapaxes(w_in[l], 0, 1).astype(BF16)
    mu_pad = mu_shift[l][None, :]
    wl = jnp.zeros((D_LORA_PAD, 3 * D_RWKV), F32)
    wl = wl.at[0:32, 0:D_RWKV].set(w_decay_up[l])
    wl = wl.at[32:64, D_RWKV:2 * D_RWKV].set(w_a_up[l])
    wl = wl.at[64:160, 2 * D_RWKV:].set(w_g_up[l])
    wl = wl.astype(BF16)
    prm = jnp.zeros((16, D_RWKV), F32)
    prm = prm.at[0].set(w0[l]).at[1].set(a0[l]).at[2].set(k_k[l]).at[3].set(k_a[l])
    prm = prm.at[4].set(r_k[l].reshape(-1)).at[5].set(ln_x_w[l]).at[6].set(ln_x_b[l])
    hid = np.arange(QUAD) // HEAD_DIM
    bmask = jnp.asarray(hid[:, None] == hid[None, :], BF16)
    tri = jnp.asarray(np.tril(np.ones((CHUNK, CHUNK))), BF16)
    qw = jnp.tile(q_norm_w[l][None, :], (1, N_Q_HEADS))
    kw = jnp.tile(k_norm_w[l][None, :], (1, N_KV_HEADS))
    nmw = norm_mix_w[l][None, :]
    nfw = norm_ffn_w[l][None, :]
    wo = w_out[l].astype(BF16)
    wu = w_ffn_up[l].astype(BF16)
    wd = w_ffn_down[l].astype(BF16)
    tab_a, tab_b = _rope_block_tables(max(t, PAST_LEN + 1) // WINDOW + 1)
    ta, tb = tab_a[PAST_LEN // WINDOW], tab_b[PAST_LEN % WINDOW][None, :]
    tab_s = jnp.concatenate([ta[:, :128] * tb[:, :128] - ta[:, 128:] * tb[:, 128:],
                             ta[:, 128:] * tb[:, :128] + ta[:, :128] * tb[:, 128:]], axis=1)

    xs = x_sample.reshape(nd, D_MODEL)
    shift_in = state_shift[l].reshape(nd, D_SHIFT)
    p_s, vec_t, gb, qn_s, kvn_s = _decode_prep(xs, nmw, w_in_pad, shift_in, mu_pad, prm, wl, bmask, tab_s, qw, kw)
    s_t = jnp.transpose(state_wkv[l], (1, 2, 3, 0))
    ck_t = jnp.swapaxes(cache_k_win[l].reshape(nd, WINDOW, D_KV), 1, 2)
    cv_t = jnp.swapaxes(cache_v_win[l].reshape(nd, WINDOW, D_KV), 1, 2)
    q_r = qn_s.reshape(nd * N_Q_HEADS, HEAD_DIM)
    cols = jnp.swapaxes(kvn_s.reshape(nd // DEC_TILE, DEC_TILE, 2 * D_KV), 1, 2)
    wkv_t, yr_s, ya_s, kc_t, vc_t = _decode_step(vec_t, gb, prm, bmask, s_t, q_r, kvn_s, cols, ck_t, cv_t,
                                                 sinks[l][:, None])

    xp = x_prompt.reshape(bsz * t, D_MODEL)
    xs_p, plast, q_p, kv_p = _inproj_shift(xp, nmw, w_in_pad, mu_pad, t, 1024)
    yr_p, hbd = _rwkv_prompt(xs_p.reshape(bsz, t, D_SHIFT), prm, wl, tri, bmask)
    y_prompt, y_s, kwin_p, vwin_p = _attn_ffn(q_p.reshape(bsz, t, D_ATTN), kv_p.reshape(bsz, t, 2 * D_KV),
                                              tab_a, tab_b, qw, kw, sinks[l][None, :], bmask, _band_bias(),
                                              x_prompt, yr_p, xs, yr_s, ya_s, wo, nfw, wu, wd)
    hb = hbd.reshape(bsz, 2, 4, HEAD_DIM, 2, HEAD_DIM)
    wkv_prompt = jnp.stack([hb[:, :, j, :, j % 2, :] for j in range(4)], axis=2)
    wkv_prompt = wkv_prompt.reshape(bsz, H_RWKV, HEAD_DIM, HEAD_DIM)[None]
    shift_prompt = plast[:, 0:1, :][None]
    k_win_prompt = jnp.swapaxes(kwin_p, 1, 2).reshape(bsz, WINDOW, N_KV_HEADS, HEAD_DIM)[None]
    v_win_prompt = jnp.swapaxes(vwin_p, 1, 2).reshape(bsz, WINDOW, N_KV_HEADS, HEAD_DIM)[None]

    y_sample = y_s.reshape(nd, 1, D_MODEL)
    wkv_sample = jnp.transpose(wkv_t, (3, 0, 1, 2))[None]
    shift_sample = p_s.reshape(nd, 1, D_SHIFT)[None]
    k_win_sample = jnp.swapaxes(kc_t, 1, 2).reshape(nd, WINDOW, N_KV_HEADS, HEAD_DIM)[None]
    v_win_sample = jnp.swapaxes(vc_t, 1, 2).reshape(nd, WINDOW, N_KV_HEADS, HEAD_DIM)[None]

    return (y_prompt, y_sample, wkv_prompt, shift_prompt, k_win_prompt, v_win_prompt,
            wkv_sample, shift_sample, k_win_sample, v_win_sample)
```

```python
import functools

import jax
import jax.numpy as jnp
import numpy as np
from jax import lax
from jax.experimental import pallas as pl
from jax.experimental.pallas import tpu as pltpu

F32 = jnp.float32
BF16 = jnp.bfloat16

D_MODEL = 1024
D_RWKV = 512
D_ATTN = 512
HEAD_DIM = 64
H_RWKV = 8
N_Q_HEADS = 8
N_KV_HEADS = 2
GQA_GROUP = 4
D_KV = 128
D_LORA = 160
D_LORA_PAD = 256
D_SHIFT = 3 * D_RWKV + D_LORA
D_IN = D_SHIFT + D_ATTN + 2 * D_KV
WINDOW = 128
ROPE_DIM = 16
ROPE_HALF = 8
ROPE_THETA = 500000.0
ATTN_SCALE = HEAD_DIM ** -0.5
D_FF = 4096
RMS_EPS = 1e-6
LNX_EPS = 64e-5
NEG_INF = -1e30
LOG2E = 1.4426950408889634
PAST_LEN = 16384

CHUNK = 64
QUAD = 4 * HEAD_DIM
V7X_VMEM_BYTES = 64 * 1024 * 1024
VMEM_LIMIT = V7X_VMEM_BYTES // 8 * 7


def _split2(x):
    hi = x.astype(BF16)
    lo = (x - hi.astype(F32)).astype(BF16)
    return hi, lo


def _head_sums(xs, ones_ref):
    n, w = xs[0].shape
    tile = min(w, QUAD)
    per = w // tile
    pieces = [x[:, j * tile:(j + 1) * tile] for x in xs for j in range(per)]
    stacked = jnp.concatenate(pieces, axis=0) if len(pieces) > 1 else pieces[0]
    ones = ones_ref[0:tile, 0:tile]
    out = jnp.dot(stacked.astype(BF16), ones, preferred_element_type=F32)
    res = []
    for i in range(len(xs)):
        cols = [out[(i * per + j) * n:(i * per + j + 1) * n] for j in range(per)]
        res.append(jnp.concatenate(cols, axis=1) if per > 1 else cols[0])
    return res


def _cumsum_rows(tri_bf16, x):
    hi, lo = _split2(x)
    return (jnp.dot(tri_bf16, hi, preferred_element_type=F32)
            + jnp.dot(tri_bf16, lo, preferred_element_type=F32))


def _mm(a, b):
    return jnp.dot(a.astype(BF16), b.astype(BF16), preferred_element_type=F32)


def _mm_nt(a, b):
    return lax.dot_general(a.astype(BF16), b.astype(BF16), (((1,), (1,)), ((), ())),
                           preferred_element_type=F32)


def _mm_tn(a, b):
    return lax.dot_general(a.astype(BF16), b.astype(BF16), (((0,), (0,)), ((), ())),
                           preferred_element_type=F32)


def _dot_nt_f32(a, b):
    return lax.dot_general(a, b, (((1,), (1,)), ((), ())), preferred_element_type=F32)


def _sigmoid(x):
    return 1.0 / (1.0 + jnp.exp(-x))


def _interleave(*gens):
    live = list(gens)
    while live:
        for g in list(live):
            try:
                next(g)
            except StopIteration:
                live.remove(g)


def _norm_project(x, nw_ref, wt_ref):
    ms = jnp.mean(x * x, axis=-1, keepdims=True)
    h = ((x * lax.rsqrt(ms + RMS_EPS)) * nw_ref[...]).astype(BF16)
    return _mm_nt(h, wt_ref[0:D_SHIFT, :]), _mm_nt(h, wt_ref[D_SHIFT:, :])


def _inproj_shift_kernel(tiles_per_seq, x_ref, nw_ref, w_ref, mu_ref, xs_ref, last_ref, q_ref, kv_ref, prev_ref):
    i = pl.program_id(0)

    @pl.when(i % tiles_per_seq == 0)
    def _():
        prev_ref[...] = jnp.zeros_like(prev_ref)

    p, qkv = _norm_project(x_ref[...], nw_ref, w_ref)
    tm = p.shape[0]
    row = lax.broadcasted_iota(jnp.int32, p.shape, 0)
    prev = jnp.where(row == 0, jnp.broadcast_to(prev_ref[0:1, :], p.shape), pltpu.roll(p, 1, 0))
    xs_ref[...] = p + (prev - p) * mu_ref[...]
    last = jnp.broadcast_to(p[tm - 1:tm, :], prev_ref.shape)
    prev_ref[...] = last
    last_ref[0] = last
    q_ref[...] = qkv[:, :D_ATTN]
    kv_ref[...] = qkv[:, D_ATTN:]


def _inproj_shift(x2d, norm_w, w_in_pad, mu_pad, seq_len, tm):
    m = x2d.shape[0]
    tiles_per_seq = seq_len // tm
    return pl.pallas_call(
        functools.partial(_inproj_shift_kernel, tiles_per_seq),
        grid=(m // tm,),
        in_specs=[
            pl.BlockSpec((tm, D_MODEL), lambda i: (i, 0)),
            pl.BlockSpec((1, D_MODEL), lambda i: (0, 0)),
            pl.BlockSpec((D_IN, D_MODEL), lambda i: (0, 0)),
            pl.BlockSpec((1, D_SHIFT), lambda i: (0, 0)),
        ],
        out_specs=[
            pl.BlockSpec((tm, D_SHIFT), lambda i: (i, 0)),
            pl.BlockSpec((1, 8, D_SHIFT), lambda i: (i // tiles_per_seq, 0, 0)),
            pl.BlockSpec((tm, D_ATTN), lambda i: (i, 0)),
            pl.BlockSpec((tm, 2 * D_KV), lambda i: (i, 0)),
        ],
        out_shape=[
            jax.ShapeDtypeStruct((m, D_SHIFT), F32),
            jax.ShapeDtypeStruct((m // seq_len, 8, D_SHIFT), F32),
            jax.ShapeDtypeStruct((m, D_ATTN), F32),
            jax.ShapeDtypeStruct((m, 2 * D_KV), F32),
        ],
        scratch_shapes=[pltpu.VMEM((8, D_SHIFT), F32)],
        compiler_params=pltpu.CompilerParams(
            dimension_semantics=("arbitrary",), vmem_limit_bytes=VMEM_LIMIT),
        name="inproj_shift",
    )(x2d, norm_w, w_in_pad, mu_pad)


def _rwkv_features(xs, prm, wl_ref, ones_ref):
    r = xs[:, 0:D_RWKV]
    k = xs[:, D_RWKV:2 * D_RWKV]
    v = xs[:, 2 * D_RWKV:3 * D_RWKV]
    lora = xs[:, 3 * D_RWKV:]
    lora = jnp.concatenate([lora, jnp.zeros((lora.shape[0], wl_ref.shape[0] - D_LORA), F32)], axis=1)
    col = lax.broadcasted_iota(jnp.int32, lora.shape, 1)
    act = jnp.where(col < 32, jnp.tanh(lora), jnp.where(col < 64, lora, _sigmoid(lora)))
    up = jnp.dot(act.astype(BF16), wl_ref[...], preferred_element_type=F32)
    w0, a0, k_k, k_a, r_k = prm[0:1], prm[1:2], prm[2:3], prm[3:4], prm[4:5]
    logw = (-np.exp(-0.5)) * _sigmoid(w0 + up[:, 0:D_RWKV])
    asig = _sigmoid(a0 + up[:, D_RWKV:2 * D_RWKV])
    g = up[:, 2 * D_RWKV:]
    kk = k * k_k
    k_mod = k * (1.0 + (asig - 1.0) * k_a)
    ss, rk = _head_sums([kk * kk, r * k_mod * r_k], ones_ref)
    kk = kk / jnp.maximum(jnp.sqrt(ss), 1e-12)
    k = k_mod
    bonus = rk * v
    return r, logw, k, v, -kk, kk * asig, g, bonus


def _rwkv_finish(y, g, bonus, prm, ones_ref):
    ln_w, ln_b = prm[5:6], prm[6:7]
    mean = _head_sums([y], ones_ref)[0] * (1.0 / HEAD_DIM)
    d = y - mean
    var = _head_sums([d * d], ones_ref)[0] * (1.0 / HEAD_DIM)
    yn = d * lax.rsqrt(var + LNX_EPS) * ln_w + ln_b
    return (yn + bonus) * g


def _block_diag(x, bmask):
    return jnp.concatenate([x] * 4, axis=0) * bmask


def _chunk_prep(r, logw, k, v, a, b, tri_ref, bmask):
    n = len(r)
    ch = range(n)
    tri = tri_ref[...]
    cum = [_cumsum_rows(tri, logw[i]) for i in ch]
    yield
    e_in = [jnp.exp(cum[i]) for i in ch]
    e_ex = [jnp.exp(cum[i] - logw[i]) for i in ch]
    e_inv = [1.0 / e_in[i] for i in ch]
    e_last = [e_in[i][CHUNK - 1:CHUNK, :] for i in ch]
    rt = [(r[i] * e_in[i]).astype(BF16) for i in ch]
    at = [(a[i] * e_ex[i]).astype(BF16) for i in ch]
    kt = [(k[i] * e_inv[i]).astype(BF16) for i in ch]
    bt = [(b[i] * e_inv[i]).astype(BF16) for i in ch]
    vb = [v[i].astype(BF16) for i in ch]
    yield

    t_idx = lax.broadcasted_iota(jnp.int32, (CHUNK, QUAD), 0)
    s_idx = lax.broadcasted_iota(jnp.int32, (CHUNK, QUAD), 1) & (HEAD_DIM - 1)
    strict = s_idx < t_idx
    incl = s_idx <= t_idx

    gm = [_mm_nt(jnp.concatenate([at[i], rt[i]], axis=0),
                 jnp.concatenate([_block_diag(bt[i], bmask), _block_diag(kt[i], bmask)], axis=0))
          for i in ch]
    yield
    a_ab = [jnp.where(strict, gm[i][:CHUNK, :QUAD], 0.0) for i in ch]
    a_ak = [jnp.where(strict, gm[i][:CHUNK, QUAD:], 0.0) for i in ch]
    a_rb = [jnp.where(incl, gm[i][CHUNK:, :QUAD], 0.0) for i in ch]
    a_rk = [jnp.where(incl, gm[i][CHUNK:, QUAD:], 0.0) for i in ch]

    eye = jnp.where(s_idx == t_idx, 1.0, 0.0)
    pwb = [a_ab[i].astype(BF16) for i in ch]
    t_inv = [eye + a_ab[i] for i in ch]
    for it in range(6):
        rbd = [_block_diag(pwb[i], bmask) for i in ch]
        if it == 0:
            pwb = [_mm(pwb[i], rbd[i]).astype(BF16) for i in ch]
        elif it < 5:
            out = [_mm(jnp.concatenate([pwb[i], t_inv[i].astype(BF16)], axis=0), rbd[i]) for i in ch]
            pwb = [out[i][:CHUNK].astype(BF16) for i in ch]
            t_inv = [t_inv[i] + out[i][CHUNK:] for i in ch]
        else:
            t_inv = [t_inv[i] + _mm(t_inv[i], rbd[i]) for i in ch]
        yield

    vbd = [_block_diag(vb[i], bmask) for i in ch]
    xy0 = [_mm(jnp.concatenate([a_ak[i], a_rk[i]], axis=0), vbd[i]) for i in ch]
    return [dict(ar=jnp.concatenate([at[i], rt[i]], axis=0), x0=xy0[i][:CHUNK], y0=xy0[i][CHUNK:],
                 t_inv=t_inv[i].astype(BF16), a_rb=a_rb[i].astype(BF16), vb=vb[i],
                 bk=jnp.concatenate([bt[i], kt[i]], axis=0), e_last=e_last[i]) for i in ch]


def _chunk_step(pre, state, bmask, out):
    ch = range(len(pre))
    half = QUAD // 2
    zeros = jnp.zeros((half, half), BF16)
    sc = [state[i].astype(BF16) for i in ch]
    sb = [jnp.concatenate([jnp.concatenate([sc[i][:half], zeros], axis=1),
                           jnp.concatenate([zeros, sc[i][half:]], axis=1)], axis=0) for i in ch]
    xr = [_mm_nt(pre[i]["ar"], sb[i]) for i in ch]
    yield
    x = [xr[i][:CHUNK] + pre[i]["x0"] for i in ch]
    u = [_mm(pre[i]["t_inv"], _block_diag(x[i].astype(BF16), bmask)) for i in ch]
    yield
    ub = [u[i].astype(BF16) for i in ch]
    out["y"] = [xr[i][CHUNK:] + pre[i]["y0"] + _mm(pre[i]["a_rb"], _block_diag(ub[i], bmask)) for i in ch]
    upd = [_mm_tn(jnp.concatenate([ub[i], pre[i]["vb"]], axis=0), pre[i]["bk"]) for i in ch]
    yield
    bm = bmask[:half, :half].astype(F32)
    s_new = []
    for i in ch:
        e_last = pre[i]["e_last"]
        top = (state[i][:half] + upd[i][:half, :half] * bm) * e_last[:, :half]
        bot = (state[i][half:] + upd[i][half:, half:] * bm) * e_last[:, half:]
        s_new.append(jnp.concatenate([top, bot], axis=0))
    out["state"] = s_new


def _finish_stages(ys, g, bonus, prm, ones_ref, y_ref, j):
    nseq = len(ys) // 2
    ln_w, ln_b = prm[5:6], prm[6:7]
    y = jnp.concatenate([jnp.concatenate(ys[2 * s:2 * s + 2], axis=1) for s in range(nseq)], axis=0)
    mean = _head_sums([y], ones_ref)[0] * (1.0 / HEAD_DIM)
    yield
    d = y - mean
    var = _head_sums([d * d], ones_ref)[0] * (1.0 / HEAD_DIM)
    yield
    out = (d * lax.rsqrt(var + LNX_EPS) * ln_w + ln_b + bonus) * g
    for s in range(nseq):
        y_ref[s, j * CHUNK:(j + 1) * CHUNK, :] = out[s * CHUNK:(s + 1) * CHUNK]


def _rwkv_prompt_kernel(xs_ref, prm_ref, wl_ref, tri_ref, bmask_ref, y_ref, hout_ref, h_ref):
    c = pl.program_id(0)
    nseq, tstep, _ = xs_ref.shape
    nchunk = tstep // CHUNK

    @pl.when(c == 0)
    def _():
        h_ref[...] = jnp.zeros_like(h_ref)

    xs = jnp.concatenate([xs_ref[s] for s in range(nseq)], axis=0)
    prm = prm_ref[...]
    r, logw, k, v, a, b, g, bonus = _rwkv_features(xs, prm, wl_ref, bmask_ref)
    bmask = bmask_ref[...]
    lanes = [(s, q) for s in range(nseq) for q in range(2)]
    pre = {}

    def prep(chunks):
        chains = [(j, s, q) for j in chunks for s, q in lanes]
        cut = lambda x: [x[s * tstep + j * CHUNK:s * tstep + (j + 1) * CHUNK, q * QUAD:(q + 1) * QUAD]
                         for j, s, q in chains]
        res = yield from _chunk_prep(cut(r), cut(logw), cut(k), cut(v), cut(a), cut(b), tri_ref, bmask)
        for n, j in enumerate(chunks):
            pre[j] = res[n * len(lanes):(n + 1) * len(lanes)]

    nfirst = max(nchunk - 1, 1)
    _interleave(prep(range(nfirst)))
    later = prep(range(nfirst, nchunk))
    state = [h_ref[s, q] for s, q in lanes]
    crow = lambda x, j: jnp.concatenate([x[s * tstep + j * CHUNK:s * tstep + (j + 1) * CHUNK]
                                         for s in range(nseq)], axis=0)
    finish = iter(())
    for j in range(nchunk):
        res = {}
        for _ in _chunk_step(pre[j], state, bmask, res):
            next(later, None)
            next(finish, None)
        _interleave(finish)
        if j == nfirst - 1:
            _interleave(later)
        state = res["state"]
        finish = _finish_stages(res["y"], crow(g, j), crow(bonus, j), prm, bmask_ref, y_ref, j)
    _interleave(finish)
    for i, (s, q) in enumerate(lanes):
        h_ref[s, q] = state[i]

    @pl.when(c == pl.num_programs(0) - 1)
    def _():
        hout_ref[...] = h_ref[...]


RWKV_CHUNKS_PER_STEP = 4


def _rwkv_prompt(xs3d, prm, wl, tri, bmask):
    bsz, t, _ = xs3d.shape
    tstep = RWKV_CHUNKS_PER_STEP * CHUNK
    nc = t // tstep
    const = lambda shape: pl.BlockSpec(shape, lambda c: (0,) * len(shape))
    state_shape = (bsz, 2, QUAD, QUAD // 2)
    return pl.pallas_call(
        _rwkv_prompt_kernel,
        grid=(nc,),
        in_specs=[
            pl.BlockSpec((bsz, tstep, D_SHIFT), lambda c: (0, c, 0)),
            const((16, D_RWKV)),
            const((D_LORA_PAD, 3 * D_RWKV)),
            const((CHUNK, CHUNK)),
            const((QUAD, QUAD)),
        ],
        out_specs=[
            pl.BlockSpec((bsz, tstep, D_RWKV), lambda c: (0, c, 0)),
            const(state_shape),
        ],
        out_shape=[
            jax.ShapeDtypeStruct((bsz, t, D_RWKV), F32),
            jax.ShapeDtypeStruct(state_shape, F32),
        ],
        scratch_shapes=[pltpu.VMEM(state_shape, F32)],
        compiler_params=pltpu.CompilerParams(
            dimension_semantics=("arbitrary",), vmem_limit_bytes=VMEM_LIMIT),
        name="rwkv_prompt",
    )(xs3d, prm, wl, tri, bmask)


def _rope_lane_freq():
    inv_freq = jnp.power(ROPE_THETA, -jnp.arange(ROPE_HALF, dtype=F32) * (2.0 / ROPE_DIM))
    return inv_freq[(np.arange(2 * HEAD_DIM) % HEAD_DIM) % ROPE_HALF][None, :]


def _rope_tables(cos, sin):
    dim = lax.broadcasted_iota(jnp.int32, cos.shape, 1) & (HEAD_DIM - 1)
    cos_t = jnp.where(dim < ROPE_DIM, cos, 1.0)
    sin_lo = jnp.where(dim < ROPE_HALF, -sin, 0.0)
    sin_hi = jnp.where((dim >= ROPE_HALF) & (dim < ROPE_DIM), sin, 0.0)
    return cos_t, sin_lo, sin_hi


def _qk_norm_rope(x, norm_w, cos_t, sin_lo, sin_hi, ones_ref):
    ms = _head_sums([x * x], ones_ref)[0] * (1.0 / HEAD_DIM)
    xn = x * lax.rsqrt(ms + RMS_EPS) * norm_w
    width = x.shape[1]
    fwd = pltpu.roll(xn, width - ROPE_HALF, 1)
    bwd = pltpu.roll(xn, ROPE_HALF, 1)
    return xn * cos_t + fwd * sin_lo + bwd * sin_hi


def _tile_lanes(x, reps):
    return jnp.concatenate([x] * reps, axis=1) if reps > 1 else x


def _attn_stages(q_ref, kv_ref, taba_ref, tabb_ref, qw_ref, kw_ref, sink_ref, ones_ref, bias_ref, first_bias,
                 kprev_ref, vprev_ref, out):
    nseq, tq, _ = q_ref.shape
    nblk = tq // WINDOW
    units = [(s, b) for s in range(nseq) for b in range(nblk)]
    blk = lambda b: slice(b * WINDOW, (b + 1) * WINDOW)
    tb = tabb_ref[...]
    cos_b, sin_b = tb[:, :128], tb[:, 128:]
    rope, rope4 = [], []
    for b in range(nblk):
        ta = taba_ref[b][0:1, :]
        cos_a, sin_a = ta[:, :128], ta[:, 128:]
        tabs = _rope_tables(cos_a * cos_b - sin_a * sin_b, sin_a * cos_b + cos_a * sin_b)
        rope.append(tabs)
        rope4.append([_tile_lanes(x, 4) for x in tabs])
    q = {(s, b): _qk_norm_rope(q_ref[s, blk(b), :], qw_ref[...], *rope4[b], ones_ref) * (ATTN_SCALE * LOG2E)
         for s, b in units}
    kv = {(s, b): kv_ref[s, blk(b), :] for s, b in units}
    k_cur = {u: _qk_norm_rope(kv[u][:, 0:D_KV], kw_ref[...], *rope[u[1]], ones_ref) for u in units}
    v_cur = {u: kv[u][:, D_KV:] for u in units}
    k_all = {(s, b): jnp.concatenate([kprev_ref[s] if b == 0 else k_cur[(s, b - 1)], k_cur[(s, b)]], axis=0)
             for s, b in units}
    v_all = {(s, b): jnp.concatenate([vprev_ref[s] if b == 0 else v_cur[(s, b - 1)], v_cur[(s, b)]], axis=0)
             for s, b in units}
    out["k_cur"] = [k_cur[(s, nblk - 1)] for s in range(nseq)]
    out["v_cur"] = [v_cur[(s, nblk - 1)] for s in range(nseq)]
    for s in range(nseq):
        kprev_ref[s] = out["k_cur"][s]
        vprev_ref[s] = out["v_cur"][s]
    yield

    nk = 2 * WINDOW
    bias = [bias_ref[first_bias] if b == 0 else bias_ref[1] for b in range(nblk)]
    sinks = sink_ref[...] * LOG2E
    low = lax.broadcasted_iota(jnp.int32, (nk, D_KV), 1) < HEAD_DIM
    lane_blk = [ones_ref[j * HEAD_DIM:j * HEAD_DIM + 1, :] for j in range(GQA_GROUP)]

    chains = [(u, g) for u in units for g in range(N_KV_HEADS)]
    ch = range(len(chains))
    k_rot = {u: pltpu.roll(k_all[u], HEAD_DIM, 1) for u in units}
    k2 = [jnp.where(low, k_all[u], k_rot[u]) if g == 0 else jnp.where(low, k_rot[u], k_all[u])
          for u, g in chains]
    k4 = [jnp.concatenate([k2[c], k2[c]], axis=1).astype(BF16) for c in ch]
    vb = {u: v_all[u].astype(BF16) for u in units}
    qg = [q[u][:, g * QUAD:(g + 1) * QUAD].astype(BF16) for u, g in chains]
    qstack = [jnp.concatenate([qg[c] * lane_blk[j] for j in range(GQA_GROUP)], axis=0) for c in ch]
    sink_row = [jnp.concatenate(
        [jnp.broadcast_to(sinks[:, g * GQA_GROUP + j:g * GQA_GROUP + j + 1], (1, WINDOW))
         for j in range(GQA_GROUP)], axis=1) for u, g in chains]
    yield
    sc = [_mm_nt(k4[c], qstack[c]) + bias[u[1]] for c, (u, g) in enumerate(chains)]
    yield
    m = [jnp.maximum(jnp.max(sc[c], axis=0, keepdims=True), sink_row[c]) for c in ch]
    e = [jnp.exp2(sc[c] - m[c]) for c in ch]
    yield
    denom = [jnp.sum(e[c], axis=0, keepdims=True) + jnp.exp2(sink_row[c] - m[c]) for c in ch]
    ot = {(u, g): _mm_tn(vb[u], e[c].astype(BF16))[g * HEAD_DIM:(g + 1) * HEAD_DIM, :] * (1.0 / denom[c])
          for c, (u, g) in enumerate(chains)}
    yield
    ya = []
    for s in range(nseq):
        blocks = []
        for b in range(nblk):
            yt = jnp.concatenate([ot[((s, b), g)][:, j * WINDOW:(j + 1) * WINDOW]
                                  for g in range(N_KV_HEADS) for j in range(GQA_GROUP)], axis=0)
            blocks.append(jnp.transpose(yt))
        ya.append(jnp.concatenate(blocks, axis=0) if nblk > 1 else blocks[0])
    out["ya"] = ya


def _ffn_stages(x, yr, ya, wo_ref, nw_ref, wu_ref, wd_ref, out, pieces=4):
    mix = jnp.concatenate([yr, ya], axis=1).astype(BF16)
    x1 = x + _mm(mix, wo_ref[...])
    yield
    ms = jnp.mean(x1 * x1, axis=-1, keepdims=True)
    hf = ((x1 * lax.rsqrt(ms + RMS_EPS)) * nw_ref[...]).astype(BF16)
    acc = x1
    step = D_FF // pieces
    for j in range(pieces):
        up = _mm(hf, wu_ref[:, j * step:(j + 1) * step])
        yield
        act = jnp.square(jnp.maximum(up, 0.0)).astype(BF16)
        acc = acc + _mm(act, wd_ref[j * step:(j + 1) * step, :])
        yield
    out["y"] = acc


def _attn_ffn_kernel(q_ref, kv_ref, taba_ref, tabb_ref, qw_ref, kw_ref, sink_ref, ones_ref, bias_ref,
                     x_ref, yr_ref, xd_ref, yrd_ref, yad_ref, wo_ref, nw_ref, wu_ref, wd_ref,
                     o_ref, od_ref, kwin_ref, vwin_ref, kprev_ref, vprev_ref, ya_ref):
    i = pl.program_id(0)
    nseq, tq, _ = q_ref.shape
    seqs = range(nseq)
    first = i == 0

    @pl.when(first)
    def _():
        kprev_ref[...] = jnp.zeros_like(kprev_ref)
        vprev_ref[...] = jnp.zeros_like(vprev_ref)
        ya_ref[...] = jnp.zeros_like(ya_ref)

    def rows(ref, dec_ref):
        tile = jnp.concatenate([ref[s] for s in seqs], axis=0)
        dec = dec_ref[...]
        return jnp.where(first, jnp.concatenate([dec] * (tile.shape[0] // dec.shape[0]), axis=0), tile)

    a_out, f_out = {}, {}
    _interleave(
        _ffn_stages(rows(x_ref, xd_ref), rows(yr_ref, yrd_ref), rows(ya_ref, yad_ref),
                    wo_ref, nw_ref, wu_ref, wd_ref, f_out),
        _attn_stages(q_ref, kv_ref, taba_ref, tabb_ref, qw_ref, kw_ref, sink_ref, ones_ref, bias_ref,
                     jnp.minimum(i, 1), kprev_ref, vprev_ref, a_out))
    for s in seqs:
        o_ref[s] = f_out["y"][s * tq:(s + 1) * tq]
        ya_ref[s] = a_out["ya"][s]

    @pl.when(first)
    def _():
        od_ref[...] = f_out["y"][:od_ref.shape[0]]

    @pl.when(i == pl.num_programs(0) - 2)
    def _():
        for s in seqs:
            kwin_ref[s] = jnp.transpose(a_out["k_cur"][s])
            vwin_ref[s] = jnp.transpose(a_out["v_cur"][s])


def _band_bias():
    ki = np.arange(2 * WINDOW)[:, None]
    qi = (np.arange(GQA_GROUP * WINDOW) % WINDOW + WINDOW)[None, :]
    dq = qi - ki
    band = (dq >= 0) & (dq < WINDOW)
    first = band & (ki >= WINDOW)
    return jnp.asarray(np.where(np.stack([first, band]), 0.0, NEG_INF), F32)


def _rope_block_tables(nb):
    freq = _rope_lane_freq()
    ang_a = (jnp.arange(nb, dtype=F32) * WINDOW)[:, None] * freq
    ang_b = jnp.arange(WINDOW, dtype=F32)[:, None] * freq
    tab_a = jnp.concatenate([jnp.cos(ang_a), jnp.sin(ang_a)], axis=1)
    tab_b = jnp.concatenate([jnp.cos(ang_b), jnp.sin(ang_b)], axis=1)
    return jnp.broadcast_to(tab_a[:, None, :], (nb, 8, 4 * HEAD_DIM)), tab_b


ATTN_BLOCKS_PER_STEP = 1


def _attn_ffn(q3d, kv3d, tab_a, tab_b, qw, kw, sinks, ones_bd, bias, x3d, yr3d, xd, yrd, yad, wo, nw, wu, wd):
    bsz, t, _ = q3d.shape
    nd = xd.shape[0]
    nblk = ATTN_BLOCKS_PER_STEP
    tq = nblk * WINDOW
    nt = t // tq
    const = lambda shape: pl.BlockSpec(shape, lambda i: (0,) * len(shape))
    single = lambda shape: pl.BlockSpec(shape, lambda i: (0,) * len(shape), pipeline_mode=pl.Buffered(1))
    cur = lambda i: jnp.minimum(i, nt - 1)
    prv = lambda i: jnp.maximum(i - 1, 0)
    return pl.pallas_call(
        _attn_ffn_kernel,
        grid=(nt + 1,),
        in_specs=[
            pl.BlockSpec((bsz, tq, D_ATTN), lambda i: (0, cur(i), 0)),
            pl.BlockSpec((bsz, tq, 2 * D_KV), lambda i: (0, cur(i), 0)),
            pl.BlockSpec((nblk, 8, 4 * HEAD_DIM), lambda i: (cur(i), 0, 0)),
            const((WINDOW, 4 * HEAD_DIM)),
            const((1, D_ATTN)),
            const((1, D_KV)),
            const((1, N_Q_HEADS)),
            const((QUAD, QUAD)),
            const((2, 2 * WINDOW, GQA_GROUP * WINDOW)),
            pl.BlockSpec((bsz, tq, D_MODEL), lambda i: (0, prv(i), 0)),
            pl.BlockSpec((bsz, tq, D_RWKV), lambda i: (0, prv(i), 0)),
            const((nd, D_MODEL)),
            const((nd, D_RWKV)),
            const((nd, D_ATTN)),
            single((D_MODEL, D_MODEL)),
            const((1, D_MODEL)),
            single((D_MODEL, D_FF)),
            single((D_FF, D_MODEL)),
        ],
        out_specs=[
            pl.BlockSpec((bsz, tq, D_MODEL), lambda i: (0, prv(i), 0)),
            const((nd, D_MODEL)),
            const((bsz, WINDOW, D_KV)),
            const((bsz, WINDOW, D_KV)),
        ],
        out_shape=[
            jax.ShapeDtypeStruct((bsz, t, D_MODEL), F32),
            jax.ShapeDtypeStruct((nd, D_MODEL), F32),
            jax.ShapeDtypeStruct((bsz, WINDOW, D_KV), F32),
            jax.ShapeDtypeStruct((bsz, WINDOW, D_KV), F32),
        ],
        scratch_shapes=[
            pltpu.VMEM((bsz, WINDOW, D_KV), F32),
            pltpu.VMEM((bsz, WINDOW, D_KV), F32),
            pltpu.VMEM((bsz, tq, D_ATTN), F32),
        ],
        compiler_params=pltpu.CompilerParams(
            dimension_semantics=("arbitrary",), vmem_limit_bytes=VMEM_LIMIT),
        name="attn_ffn",
    )(q3d, kv3d, tab_a, tab_b, qw, kw, sinks, ones_bd, bias, x3d, yr3d, xd, yrd, yad, wo, nw, wu, wd)


DEC_TILE = 16


def _decode_prep_kernel(x_ref, nw_ref, w_ref, sh_ref, mu_ref, prm_ref, wl_ref, ones_ref, tab_ref,
                        qw_ref, kw_ref, p_ref, vec_ref, vgb_ref, qn_ref, kvn_ref):
    p, qkv = _norm_project(x_ref[...], nw_ref, w_ref)
    p_ref[...] = p
    xs = p + (sh_ref[...] - p) * mu_ref[...]
    r, logw, k, v, a, b, g, bonus = _rwkv_features(xs, prm_ref[...], wl_ref, ones_ref)
    for i, x in enumerate((a, b, k, jnp.exp(logw), r, v)):
        vec_ref[i] = jnp.transpose(x)
    vgb_ref[0] = g
    vgb_ref[1] = bonus
    n = p.shape[0]
    tab = jnp.broadcast_to(tab_ref[0:1, :], (n, 4 * HEAD_DIM))
    cos_t, sin_lo, sin_hi = _rope_tables(tab[:, :128], tab[:, 128:])
    qn_ref[...] = _qk_norm_rope(qkv[:, :D_ATTN], qw_ref[...], _tile_lanes(cos_t, 4), _tile_lanes(sin_lo, 4),
                                _tile_lanes(sin_hi, 4), ones_ref)
    kvn_ref[:, 0:D_KV] = _qk_norm_rope(qkv[:, D_ATTN:D_ATTN + D_KV], kw_ref[...], cos_t, sin_lo, sin_hi,
                                       ones_ref)
    kvn_ref[:, D_KV:] = qkv[:, D_ATTN + D_KV:]


def _decode_prep(x2d, norm_w, w_in_t, shift, mu_pad, prm, wl, bmask, tab, qw, kw):
    n = x2d.shape[0]
    full = lambda shape: pl.BlockSpec(shape, lambda i: (0,) * len(shape))
    return pl.pallas_call(
        _decode_prep_kernel,
        grid=(1,),
        in_specs=[full((n, D_MODEL)), full((1, D_MODEL)), full((D_IN, D_MODEL)), full((n, D_SHIFT)),
                  full((1, D_SHIFT)), full((16, D_RWKV)), full((D_LORA_PAD, 3 * D_RWKV)),
                  full((QUAD, QUAD)), full((8, 4 * HEAD_DIM)), full((1, D_ATTN)), full((1, D_KV))],
        out_specs=[full((n, D_SHIFT)), full((6, D_RWKV, n)), full((2, n, D_RWKV)), full((n, D_ATTN)),
                   full((n, 2 * D_KV))],
        out_shape=[
            jax.ShapeDtypeStruct((n, D_SHIFT), F32),
            jax.ShapeDtypeStruct((6, D_RWKV, n), F32),
            jax.ShapeDtypeStruct((2, n, D_RWKV), F32),
            jax.ShapeDtypeStruct((n, D_ATTN), F32),
            jax.ShapeDtypeStruct((n, 2 * D_KV), F32),
        ],
        compiler_params=pltpu.CompilerParams(
            dimension_semantics=("arbitrary",), vmem_limit_bytes=VMEM_LIMIT),
        name="decode_prep",
    )(x2d, norm_w, w_in_t, shift, mu_pad, prm, wl, bmask, tab, qw, kw)


def _decode_state_stages(vec_ref, s_ref, sout_ref, yt_ref, h):
    a_t, b_t, k_t, w_t, r_t = (vec_ref[i] for i in range(5))
    for i in range(HEAD_DIM):
        s = s_ref[0, i]
        sa = jnp.sum(s * a_t, axis=0, keepdims=True)
        v_i = vec_ref[5, i:i + 1, :]
        s_new = s * w_t + sa * b_t + v_i * k_t
        sout_ref[0, i] = s_new
        yt_ref[pl.ds(h * HEAD_DIM + i, 1), :] = jnp.sum(s_new * r_t, axis=0, keepdims=True)
        if i % 8 == 7:
            yield


def _decode_attn_stages(qr_ref, kvn_ref, col_ref, ck_ref, cv_ref, sink_ref, ya_ref, kout_ref, vout_ref):
    nh = N_Q_HEADS
    seqs = range(DEC_TILE)
    hrow = lax.broadcasted_iota(jnp.int32, (nh, D_ATTN), 0)
    hlane = lax.broadcasted_iota(jnp.int32, (nh, D_ATTN), 1) // HEAD_DIM
    dmask = hrow == hlane
    grow = lax.broadcasted_iota(jnp.int32, (nh, D_KV), 0) // GQA_GROUP
    glane = lax.broadcasted_iota(jnp.int32, (nh, D_KV), 1) // HEAD_DIM
    gmask = grow == glane
    low = glane == 0
    key_idx = lax.broadcasted_iota(jnp.int32, (nh, WINDOW), 1)
    last = lax.broadcasted_iota(jnp.int32, (D_KV, WINDOW), 1) == WINDOW - 1
    sink = sink_ref[...]
    kvn = kvn_ref[...]
    col = col_ref[0]
    k_new = [kvn[j:j + 1, 0:D_KV] for j in seqs]
    v_new = [kvn[j:j + 1, D_KV:] for j in seqs]
    ck = [ck_ref[j] for j in seqs]
    cv = [cv_ref[j] for j in seqs]
    for j in seqs:
        kout_ref[j] = jnp.where(last, col[0:D_KV, j:j + 1], pltpu.roll(ck[j], WINDOW - 1, 1))
        vout_ref[j] = jnp.where(last, col[D_KV:, j:j + 1], pltpu.roll(cv[j], WINDOW - 1, 1))
    yield
    q8 = [qr_ref[j * nh:(j + 1) * nh, :] for j in seqs]
    qp = [jnp.where(gmask, jnp.concatenate([q8[j], q8[j]], axis=1), 0.0) for j in seqs]
    s_c = [jnp.where(key_idx >= 1, jnp.dot(qp[j], ck[j], preferred_element_type=F32) * ATTN_SCALE, NEG_INF)
           for j in seqs]
    s_n = [jnp.sum(qp[j] * k_new[j], axis=-1, keepdims=True) * ATTN_SCALE for j in seqs]
    yield
    m = [jnp.maximum(jnp.maximum(jnp.max(s_c[j], axis=-1, keepdims=True), s_n[j]), sink) for j in seqs]
    yield
    e_c = [jnp.exp(s_c[j] - m[j]) for j in seqs]
    e_n = [jnp.exp(s_n[j] - m[j]) for j in seqs]
    denom = [jnp.sum(e_c[j], axis=-1, keepdims=True) + e_n[j] + jnp.exp(sink - m[j]) for j in seqs]
    yield
    o = [(_dot_nt_f32(e_c[j], cv[j]) + e_n[j] * v_new[j]) / denom[j]
         for j in seqs]
    yield
    out_rows = []
    for j in seqs:
        rot = pltpu.roll(o[j], HEAD_DIM, 1)
        g0 = jnp.where(low, o[j], rot)
        g1 = jnp.where(low, rot, o[j])
        wide = jnp.concatenate([g0, g0, g1, g1], axis=1)
        out_rows.append(jnp.sum(jnp.where(dmask, wide, 0.0), axis=0, keepdims=True))
    ya_ref[...] = jnp.concatenate(out_rows, axis=0)


def _decode_step_kernel(vec_ref, gb_ref, prm_ref, ones_ref, s_ref, qr_ref, kvn_ref, col_ref, ck_ref, cv_ref,
                        sink_ref, sout_ref, yr_ref, ya_ref, kout_ref, vout_ref, yt_ref):
    i = pl.program_id(0)
    _interleave(
        _decode_attn_stages(qr_ref, kvn_ref, col_ref, ck_ref, cv_ref, sink_ref, ya_ref, kout_ref, vout_ref),
        _decode_state_stages(vec_ref, s_ref, sout_ref, yt_ref, i))

    @pl.when(i == pl.num_programs(0) - 1)
    def _():
        y = jnp.transpose(yt_ref[...])
        yr_ref[...] = _rwkv_finish(y, gb_ref[0], gb_ref[1], prm_ref[...], ones_ref)


def _decode_step(vec_t, gb, prm, bmask, s_t, q_r, kvn, cols, ck_t, cv_t, sinks_col):
    n = s_t.shape[-1]
    bt = DEC_TILE
    assert n // bt == H_RWKV, "one sequence tile per RWKV head"
    const = lambda shape: pl.BlockSpec(shape, lambda i: (0,) * len(shape))
    return pl.pallas_call(
        _decode_step_kernel,
        grid=(H_RWKV,),
        in_specs=[
            pl.BlockSpec((6, HEAD_DIM, n), lambda i: (0, i, 0)),
            const((2, n, D_RWKV)),
            const((16, D_RWKV)),
            const((QUAD, QUAD)),
            pl.BlockSpec((1, HEAD_DIM, HEAD_DIM, n), lambda i: (i, 0, 0, 0)),
            pl.BlockSpec((bt * N_Q_HEADS, HEAD_DIM), lambda i: (i, 0)),
            pl.BlockSpec((bt, 2 * D_KV), lambda i: (i, 0)),
            pl.BlockSpec((1, 2 * D_KV, bt), lambda i: (i, 0, 0)),
            pl.BlockSpec((bt, D_KV, WINDOW), lambda i: (i, 0, 0)),
            pl.BlockSpec((bt, D_KV, WINDOW), lambda i: (i, 0, 0)),
            const((N_Q_HEADS, 1)),
        ],
        out_specs=[
            pl.BlockSpec((1, HEAD_DIM, HEAD_DIM, n), lambda i: (i, 0, 0, 0)),
            const((n, D_RWKV)),
            pl.BlockSpec((bt, D_ATTN), lambda i: (i, 0)),
            pl.BlockSpec((bt, D_KV, WINDOW), lambda i: (i, 0, 0)),
            pl.BlockSpec((bt, D_KV, WINDOW), lambda i: (i, 0, 0)),
        ],
        out_shape=[
            jax.ShapeDtypeStruct((H_RWKV, HEAD_DIM, HEAD_DIM, n), F32),
            jax.ShapeDtypeStruct((n, D_RWKV), F32),
            jax.ShapeDtypeStruct((n, D_ATTN), F32),
            jax.ShapeDtypeStruct((n, D_KV, WINDOW), F32),
            jax.ShapeDtypeStruct((n, D_KV, WINDOW), F32),
        ],
        scratch_shapes=[pltpu.VMEM((D_RWKV, n), F32)],
        compiler_params=pltpu.CompilerParams(
            dimension_semantics=("arbitrary",), vmem_limit_bytes=VMEM_LIMIT),
        name="decode_step",
    )(vec_t, gb, prm, bmask, s_t, q_r, kvn, cols, ck_t, cv_t, sinks_col)


def kernel(x_prompt, x_sample, state_wkv, state_shift, cache_k_win, cache_v_win, norm_mix_w, w_in, mu_shift, w0, w_decay_up, a0, w_a_up, w_g_up, k_k, k_a, r_k, ln_x_w, ln_x_b, q_norm_w, k_norm_w, sinks, w_out, norm_ffn_w, w_ffn_up, w_ffn_down):
    bsz, t, _ = x_prompt.shape
    nd = x_sample.shape[0]
    l = 0

    w_in_pad = jnp.swapaxes(w_in[l], 0, 1)
    mu_pad = mu_shift[l][None, :]
    wl = jnp.zeros((D_LORA_PAD, 3 * D_RWKV), F32)
    wl = wl.at[0:32, 0:D_RWKV].set(w_decay_up[l])
    wl = wl.at[32:64, D_RWKV:2 * D_RWKV].set(w_a_up[l])
    wl = wl.at[64:160, 2 * D_RWKV:].set(w_g_up[l])
    wl = wl.astype(BF16)
    prm = jnp.zeros((16, D_RWKV), F32)
    prm = prm.at[0].set(w0[l]).at[1].set(a0[l]).at[2].set(k_k[l]).at[3].set(k_a[l])
    prm = prm.at[4].set(r_k[l].reshape(-1)).at[5].set(ln_x_w[l]).at[6].set(ln_x_b[l])
    hid = np.arange(QUAD) // HEAD_DIM
    bmask = jnp.asarray(hid[:, None] == hid[None, :], BF16)
    tri = jnp.asarray(np.tril(np.ones((CHUNK, CHUNK))), BF16)
    qw = jnp.tile(q_norm_w[l][None, :], (1, N_Q_HEADS))
    kw = jnp.tile(k_norm_w[l][None, :], (1, N_KV_HEADS))
    nmw = norm_mix_w[l][None, :]
    nfw = norm_ffn_w[l][None, :]
    wo, wu, wd = w_out[l], w_ffn_up[l], w_ffn_down[l]
    tab_a, tab_b = _rope_block_tables(max(t, PAST_LEN + 1) // WINDOW + 1)
    ta, tb = tab_a[PAST_LEN // WINDOW], tab_b[PAST_LEN % WINDOW][None, :]
    tab_s = jnp.concatenate([ta[:, :128] * tb[:, :128] - ta[:, 128:] * tb[:, 128:],
                             ta[:, 128:] * tb[:, :128] + ta[:, :128] * tb[:, 128:]], axis=1)

    xs = x_sample.reshape(nd, D_MODEL)
    shift_in = state_shift[l].reshape(nd, D_SHIFT)
    p_s, vec_t, gb, qn_s, kvn_s = _decode_prep(xs, nmw, w_in_pad, shift_in, mu_pad, prm, wl, bmask, tab_s, qw, kw)
    s_t = jnp.transpose(state_wkv[l], (1, 2, 3, 0))
    ck_t = jnp.swapaxes(cache_k_win[l].reshape(nd, WINDOW, D_KV), 1, 2)
    cv_t = jnp.swapaxes(cache_v_win[l].reshape(nd, WINDOW, D_KV), 1, 2)
    q_r = qn_s.reshape(nd * N_Q_HEADS, HEAD_DIM)
    cols = jnp.swapaxes(kvn_s.reshape(nd // DEC_TILE, DEC_TILE, 2 * D_KV), 1, 2)
    wkv_t, yr_s, ya_s, kc_t, vc_t = _decode_step(vec_t, gb, prm, bmask, s_t, q_r, kvn_s, cols, ck_t, cv_t,
                                                 sinks[l][:, None])

    xp = x_prompt.reshape(bsz * t, D_MODEL)
    xs_p, plast, q_p, kv_p = _inproj_shift(xp, nmw, w_in_pad, mu_pad, t, 1024)
    yr_p, hbd = _rwkv_prompt(xs_p.reshape(bsz, t, D_SHIFT), prm, wl, tri, bmask)
    y_prompt, y_s, kwin_p, vwin_p = _attn_ffn(q_p.reshape(bsz, t, D_ATTN), kv_p.reshape(bsz, t, 2 * D_KV),
                                              tab_a, tab_b, qw, kw, sinks[l][None, :], bmask, _band_bias(),
                                              x_prompt, yr_p, xs, yr_s, ya_s, wo, nfw, wu, wd)
    hb = hbd.reshape(bsz, 2, 4, HEAD_DIM, 2, HEAD_DIM)
    wkv_prompt = jnp.stack([hb[:, :, j, :, j % 2, :] for j in range(4)], axis=2)
    wkv_prompt = wkv_prompt.reshape(bsz, H_RWKV, HEAD_DIM, HEAD_DIM)[None]
    shift_prompt = plast[:, 0:1, :][None]
    k_win_prompt = jnp.swapaxes(kwin_p, 1, 2).reshape(bsz, WINDOW, N_KV_HEADS, HEAD_DIM)[None]
    v_win_prompt = jnp.swapaxes(vwin_p, 1, 2).reshape(bsz, WINDOW, N_KV_HEADS, HEAD_DIM)[None]

    y_sample = y_s.reshape(nd, 1, D_MODEL)
    wkv_sample = jnp.transpose(wkv_t, (3, 0, 1, 2))[None]
    shift_sample = p_s.reshape(nd, 1, D_SHIFT)[None]
    k_win_sample = jnp.swapaxes(kc_t, 1, 2).reshape(nd, WINDOW, N_KV_HEADS, HEAD_DIM)[None]
    v_win_sample = jnp.swapaxes(vc_t, 1, 2).reshape(nd, WINDOW, N_KV_HEADS, HEAD_DIM)[None]

    return (y_prompt, y_sample, wkv_prompt, shift_prompt, k_win_prompt, v_win_prompt,
            wkv_sample, shift_sample, k_win_sample, v_win_sample)
```

```python
import functools

import jax
import jax.numpy as jnp
import numpy as np
from jax import lax
from jax.experimental import pallas as pl
from jax.experimental.pallas import tpu as pltpu

F32 = jnp.float32
BF16 = jnp.bfloat16

D_MODEL = 1024
D_RWKV = 512
D_ATTN = 512
HEAD_DIM = 64
H_RWKV = 8
N_Q_HEADS = 8
N_KV_HEADS = 2
GQA_GROUP = 4
D_KV = 128
D_LORA = 160
D_LORA_PAD = 256
D_SHIFT = 3 * D_RWKV + D_LORA
D_IN = D_SHIFT + D_ATTN + 2 * D_KV
WINDOW = 128
ROPE_DIM = 16
ROPE_HALF = 8
ROPE_THETA = 500000.0
ATTN_SCALE = HEAD_DIM ** -0.5
D_FF = 4096
RMS_EPS = 1e-6
LNX_EPS = 64e-5
NEG_INF = -1e30
LOG2E = 1.4426950408889634
PAST_LEN = 16384

CHUNK = 64
QUAD = 4 * HEAD_DIM
V7X_VMEM_BYTES = 64 * 1024 * 1024
VMEM_LIMIT = V7X_VMEM_BYTES // 8 * 7


def _split2(x):
    hi = x.astype(BF16)
    lo = (x - hi.astype(F32)).astype(BF16)
    return hi, lo


def _head_sums(xs, ones_ref):
    n, w = xs[0].shape
    tile = min(w, QUAD)
    per = w // tile
    pieces = [x[:, j * tile:(j + 1) * tile] for x in xs for j in range(per)]
    stacked = jnp.concatenate(pieces, axis=0) if len(pieces) > 1 else pieces[0]
    ones = ones_ref[0:tile, 0:tile]
    out = jnp.dot(stacked.astype(BF16), ones, preferred_element_type=F32)
    res = []
    for i in range(len(xs)):
        cols = [out[(i * per + j) * n:(i * per + j + 1) * n] for j in range(per)]
        res.append(jnp.concatenate(cols, axis=1) if per > 1 else cols[0])
    return res


def _cumsum_rows(tri_bf16, x):
    hi, lo = _split2(x)
    return (jnp.dot(tri_bf16, hi, preferred_element_type=F32)
            + jnp.dot(tri_bf16, lo, preferred_element_type=F32))


def _mm(a, b):
    return jnp.dot(a.astype(BF16), b.astype(BF16), preferred_element_type=F32)


def _mm_nt(a, b):
    return lax.dot_general(a.astype(BF16), b.astype(BF16), (((1,), (1,)), ((), ())),
                           preferred_element_type=F32)


def _mm_tn(a, b):
    return lax.dot_general(a.astype(BF16), b.astype(BF16), (((0,), (0,)), ((), ())),
                           preferred_element_type=F32)


def _dot_nt_f32(a, b):
    return lax.dot_general(a, b, (((1,), (1,)), ((), ())), preferred_element_type=F32)


def _sigmoid(x):
    return 1.0 / (1.0 + jnp.exp(-x))


def _interleave(*gens):
    live = list(gens)
    while live:
        for g in list(live):
            try:
                next(g)
            except StopIteration:
                live.remove(g)


def _norm_project(x, nw_ref, wt_ref):
    ms = jnp.mean(x * x, axis=-1, keepdims=True)
    h = ((x * lax.rsqrt(ms + RMS_EPS)) * nw_ref[...]).astype(BF16)
    return _mm_nt(h, wt_ref[0:D_SHIFT, :]), _mm_nt(h, wt_ref[D_SHIFT:, :])


def _inproj_shift_kernel(tiles_per_seq, x_ref, nw_ref, w_ref, mu_ref, xs_ref, last_ref, q_ref, kv_ref, prev_ref):
    i = pl.program_id(0)

    @pl.when(i % tiles_per_seq == 0)
    def _():
        prev_ref[...] = jnp.zeros_like(prev_ref)

    p, qkv = _norm_project(x_ref[...], nw_ref, w_ref)
    tm = p.shape[0]
    row = lax.broadcasted_iota(jnp.int32, p.shape, 0)
    prev = jnp.where(row == 0, jnp.broadcast_to(prev_ref[0:1, :], p.shape), pltpu.roll(p, 1, 0))
    xs_ref[...] = p + (prev - p) * mu_ref[...]
    last = jnp.broadcast_to(p[tm - 1:tm, :], prev_ref.shape)
    prev_ref[...] = last
    last_ref[0] = last
    q_ref[...] = qkv[:, :D_ATTN]
    kv_ref[...] = qkv[:, D_ATTN:]


def _inproj_shift(x2d, norm_w, w_in_pad, mu_pad, seq_len, tm):
    m = x2d.shape[0]
    tiles_per_seq = seq_len // tm
    return pl.pallas_call(
        functools.partial(_inproj_shift_kernel, tiles_per_seq),
        grid=(m // tm,),
        in_specs=[
            pl.BlockSpec((tm, D_MODEL), lambda i: (i, 0)),
            pl.BlockSpec((1, D_MODEL), lambda i: (0, 0)),
            pl.BlockSpec((D_IN, D_MODEL), lambda i: (0, 0)),
            pl.BlockSpec((1, D_SHIFT), lambda i: (0, 0)),
        ],
        out_specs=[
            pl.BlockSpec((tm, D_SHIFT), lambda i: (i, 0)),
            pl.BlockSpec((1, 8, D_SHIFT), lambda i: (i // tiles_per_seq, 0, 0)),
            pl.BlockSpec((tm, D_ATTN), lambda i: (i, 0)),
            pl.BlockSpec((tm, 2 * D_KV), lambda i: (i, 0)),
        ],
        out_shape=[
            jax.ShapeDtypeStruct((m, D_SHIFT), F32),
            jax.ShapeDtypeStruct((m // seq_len, 8, D_SHIFT), F32),
            jax.ShapeDtypeStruct((m, D_ATTN), F32),
            jax.ShapeDtypeStruct((m, 2 * D_KV), F32),
        ],
        scratch_shapes=[pltpu.VMEM((8, D_SHIFT), F32)],
        compiler_params=pltpu.CompilerParams(
            dimension_semantics=("arbitrary",), vmem_limit_bytes=VMEM_LIMIT),
        name="inproj_shift",
    )(x2d, norm_w, w_in_pad, mu_pad)


def _rwkv_features(xs, prm, wl_ref, ones_ref):
    r = xs[:, 0:D_RWKV]
    k = xs[:, D_RWKV:2 * D_RWKV]
    v = xs[:, 2 * D_RWKV:3 * D_RWKV]
    lora = xs[:, 3 * D_RWKV:]
    lora = jnp.concatenate([lora, jnp.zeros((lora.shape[0], wl_ref.shape[0] - D_LORA), F32)], axis=1)
    col = lax.broadcasted_iota(jnp.int32, lora.shape, 1)
    act = jnp.where(col < 32, jnp.tanh(lora), jnp.where(col < 64, lora, _sigmoid(lora)))
    up = jnp.dot(act.astype(BF16), wl_ref[...], preferred_element_type=F32)
    w0, a0, k_k, k_a, r_k = prm[0:1], prm[1:2], prm[2:3], prm[3:4], prm[4:5]
    logw = (-np.exp(-0.5)) * _sigmoid(w0 + up[:, 0:D_RWKV])
    asig = _sigmoid(a0 + up[:, D_RWKV:2 * D_RWKV])
    g = up[:, 2 * D_RWKV:]
    kk = k * k_k
    k_mod = k * (1.0 + (asig - 1.0) * k_a)
    ss, rk = _head_sums([kk * kk, r * k_mod * r_k], ones_ref)
    kk = kk / jnp.maximum(jnp.sqrt(ss), 1e-12)
    k = k_mod
    bonus = rk * v
    return r, logw, k, v, -kk, kk * asig, g, bonus


def _rwkv_finish(y, g, bonus, prm, ones_ref):
    ln_w, ln_b = prm[5:6], prm[6:7]
    mean = _head_sums([y], ones_ref)[0] * (1.0 / HEAD_DIM)
    d = y - mean
    var = _head_sums([d * d], ones_ref)[0] * (1.0 / HEAD_DIM)
    yn = d * lax.rsqrt(var + LNX_EPS) * ln_w + ln_b
    return (yn + bonus) * g


def _block_diag(x, bmask):
    return jnp.concatenate([x] * 4, axis=0) * bmask


def _chunk_prep(r, logw, k, v, a, b, tri_ref, bmask):
    n = len(r)
    ch = range(n)
    tri = tri_ref[...]
    cum = [_cumsum_rows(tri, logw[i]) for i in ch]
    yield
    e_in = [jnp.exp(cum[i]) for i in ch]
    e_ex = [jnp.exp(cum[i] - logw[i]) for i in ch]
    e_inv = [1.0 / e_in[i] for i in ch]
    e_last = [e_in[i][CHUNK - 1:CHUNK, :] for i in ch]
    rt = [(r[i] * e_in[i]).astype(BF16) for i in ch]
    at = [(a[i] * e_ex[i]).astype(BF16) for i in ch]
    kt = [(k[i] * e_inv[i]).astype(BF16) for i in ch]
    bt = [(b[i] * e_inv[i]).astype(BF16) for i in ch]
    vb = [v[i].astype(BF16) for i in ch]
    yield

    t_idx = lax.broadcasted_iota(jnp.int32, (CHUNK, QUAD), 0)
    s_idx = lax.broadcasted_iota(jnp.int32, (CHUNK, QUAD), 1) & (HEAD_DIM - 1)
    strict = s_idx < t_idx
    incl = s_idx <= t_idx

    gm = [_mm_nt(jnp.concatenate([at[i], rt[i]], axis=0),
                 jnp.concatenate([_block_diag(bt[i], bmask), _block_diag(kt[i], bmask)], axis=0))
          for i in ch]
    yield
    a_ab = [jnp.where(strict, gm[i][:CHUNK, :QUAD], 0.0) for i in ch]
    a_ak = [jnp.where(strict, gm[i][:CHUNK, QUAD:], 0.0) for i in ch]
    a_rb = [jnp.where(incl, gm[i][CHUNK:, :QUAD], 0.0) for i in ch]
    a_rk = [jnp.where(incl, gm[i][CHUNK:, QUAD:], 0.0) for i in ch]

    eye = jnp.where(s_idx == t_idx, 1.0, 0.0)
    pwb = [a_ab[i].astype(BF16) for i in ch]
    t_inv = [eye + a_ab[i] for i in ch]
    for it in range(6):
        rbd = [_block_diag(pwb[i], bmask) for i in ch]
        if it == 0:
            pwb = [_mm(pwb[i], rbd[i]).astype(BF16) for i in ch]
        elif it < 5:
            out = [_mm(jnp.concatenate([pwb[i], t_inv[i].astype(BF16)], axis=0), rbd[i]) for i in ch]
            pwb = [out[i][:CHUNK].astype(BF16) for i in ch]
            t_inv = [t_inv[i] + out[i][CHUNK:] for i in ch]
        else:
            t_inv = [t_inv[i] + _mm(t_inv[i], rbd[i]) for i in ch]
        yield

    vbd = [_block_diag(vb[i], bmask) for i in ch]
    xy0 = [_mm(jnp.concatenate([a_ak[i], a_rk[i]], axis=0), vbd[i]) for i in ch]
    return [dict(ar=jnp.concatenate([at[i], rt[i]], axis=0), x0=xy0[i][:CHUNK], y0=xy0[i][CHUNK:],
                 t_inv=t_inv[i].astype(BF16), a_rb=a_rb[i].astype(BF16), vb=vb[i],
                 bk=jnp.concatenate([bt[i], kt[i]], axis=0), e_last=e_last[i]) for i in ch]


def _chunk_step(pre, state, bmask, out):
    ch = range(len(pre))
    half = QUAD // 2
    zeros = jnp.zeros((half, half), BF16)
    sc = [state[i].astype(BF16) for i in ch]
    sb = [jnp.concatenate([jnp.concatenate([sc[i][:half], zeros], axis=1),
                           jnp.concatenate([zeros, sc[i][half:]], axis=1)], axis=0) for i in ch]
    xr = [_mm_nt(pre[i]["ar"], sb[i]) for i in ch]
    yield
    x = [xr[i][:CHUNK] + pre[i]["x0"] for i in ch]
    u = [_mm(pre[i]["t_inv"], _block_diag(x[i].astype(BF16), bmask)) for i in ch]
    yield
    ub = [u[i].astype(BF16) for i in ch]
    out["y"] = [xr[i][CHUNK:] + pre[i]["y0"] + _mm(pre[i]["a_rb"], _block_diag(ub[i], bmask)) for i in ch]
    upd = [_mm_tn(jnp.concatenate([ub[i], pre[i]["vb"]], axis=0), pre[i]["bk"]) for i in ch]
    yield
    bm = bmask[:half, :half].astype(F32)
    s_new = []
    for i in ch:
        e_last = pre[i]["e_last"]
        top = (state[i][:half] + upd[i][:half, :half] * bm) * e_last[:, :half]
        bot = (state[i][half:] + upd[i][half:, half:] * bm) * e_last[:, half:]
        s_new.append(jnp.concatenate([top, bot], axis=0))
    out["state"] = s_new


def _finish_stages(ys, g, bonus, prm, ones_ref, y_ref, j):
    nseq = len(ys) // 2
    ln_w, ln_b = prm[5:6], prm[6:7]
    y = jnp.concatenate([jnp.concatenate(ys[2 * s:2 * s + 2], axis=1) for s in range(nseq)], axis=0)
    mean = _head_sums([y], ones_ref)[0] * (1.0 / HEAD_DIM)
    yield
    d = y - mean
    var = _head_sums([d * d], ones_ref)[0] * (1.0 / HEAD_DIM)
    yield
    out = (d * lax.rsqrt(var + LNX_EPS) * ln_w + ln_b + bonus) * g
    for s in range(nseq):
        y_ref[s, j * CHUNK:(j + 1) * CHUNK, :] = out[s * CHUNK:(s + 1) * CHUNK]


def _rwkv_prompt_kernel(xs_ref, prm_ref, wl_ref, tri_ref, bmask_ref, y_ref, hout_ref, h_ref):
    c = pl.program_id(0)
    nseq, tstep, _ = xs_ref.shape
    nchunk = tstep // CHUNK

    @pl.when(c == 0)
    def _():
        h_ref[...] = jnp.zeros_like(h_ref)

    xs = jnp.concatenate([xs_ref[s] for s in range(nseq)], axis=0)
    prm = prm_ref[...]
    r, logw, k, v, a, b, g, bonus = _rwkv_features(xs, prm, wl_ref, bmask_ref)
    bmask = bmask_ref[...]
    lanes = [(s, q) for s in range(nseq) for q in range(2)]
    pre = {}

    def prep(chunks):
        chains = [(j, s, q) for j in chunks for s, q in lanes]
        cut = lambda x: [x[s * tstep + j * CHUNK:s * tstep + (j + 1) * CHUNK, q * QUAD:(q + 1) * QUAD]
                         for j, s, q in chains]
        res = yield from _chunk_prep(cut(r), cut(logw), cut(k), cut(v), cut(a), cut(b), tri_ref, bmask)
        for n, j in enumerate(chunks):
            pre[j] = res[n * len(lanes):(n + 1) * len(lanes)]

    nfirst = max(nchunk - 1, 1)
    _interleave(prep(range(nfirst)))
    later = prep(range(nfirst, nchunk))
    state = [h_ref[s, q] for s, q in lanes]
    crow = lambda x, j: jnp.concatenate([x[s * tstep + j * CHUNK:s * tstep + (j + 1) * CHUNK]
                                         for s in range(nseq)], axis=0)
    finish = iter(())
    for j in range(nchunk):
        res = {}
        for _ in _chunk_step(pre[j], state, bmask, res):
            next(later, None)
            next(finish, None)
        _interleave(finish)
        if j == nfirst - 1:
            _interleave(later)
        state = res["state"]
        finish = _finish_stages(res["y"], crow(g, j), crow(bonus, j), prm, bmask_ref, y_ref, j)
    _interleave(finish)
    for i, (s, q) in enumerate(lanes):
        h_ref[s, q] = state[i]

    @pl.when(c == pl.num_programs(0) - 1)
    def _():
        hout_ref[...] = h_ref[...]


RWKV_CHUNKS_PER_STEP = 4


def _rwkv_prompt(xs3d, prm, wl, tri, bmask):
    bsz, t, _ = xs3d.shape
    tstep = RWKV_CHUNKS_PER_STEP * CHUNK
    nc = t // tstep
    const = lambda shape: pl.BlockSpec(shape, lambda c: (0,) * len(shape))
    state_shape = (bsz, 2, QUAD, QUAD // 2)
    return pl.pallas_call(
        _rwkv_prompt_kernel,
        grid=(nc,),
        in_specs=[
            pl.BlockSpec((bsz, tstep, D_SHIFT), lambda c: (0, c, 0)),
            const((16, D_RWKV)),
            const((D_LORA_PAD, 3 * D_RWKV)),
            const((CHUNK, CHUNK)),
            const((QUAD, QUAD)),
        ],
        out_specs=[
            pl.BlockSpec((bsz, tstep, D_RWKV), lambda c: (0, c, 0)),
            const(state_shape),
        ],
        out_shape=[
            jax.ShapeDtypeStruct((bsz, t, D_RWKV), F32),
            jax.ShapeDtypeStruct(state_shape, F32),
        ],
        scratch_shapes=[pltpu.VMEM(state_shape, F32)],
        compiler_params=pltpu.CompilerParams(
            dimension_semantics=("arbitrary",), vmem_limit_bytes=VMEM_LIMIT),
        name="rwkv_prompt",
    )(xs3d, prm, wl, tri, bmask)


def _rope_lane_freq():
    pair = jnp.asarray((np.arange(2 * HEAD_DIM) % HEAD_DIM) % ROPE_HALF, F32)
    return jnp.power(ROPE_THETA, -pair * (2.0 / ROPE_DIM))[None, :]


def _rope_tables(cos, sin):
    dim = lax.broadcasted_iota(jnp.int32, cos.shape, 1) & (HEAD_DIM - 1)
    cos_t = jnp.where(dim < ROPE_DIM, cos, 1.0)
    sin_lo = jnp.where(dim < ROPE_HALF, -sin, 0.0)
    sin_hi = jnp.where((dim >= ROPE_HALF) & (dim < ROPE_DIM), sin, 0.0)
    return cos_t, sin_lo, sin_hi


def _qk_norm_rope(x, norm_w, cos_t, sin_lo, sin_hi, ones_ref):
    ms = _head_sums([x * x], ones_ref)[0] * (1.0 / HEAD_DIM)
    xn = x * lax.rsqrt(ms + RMS_EPS) * norm_w
    width = x.shape[1]
    fwd = pltpu.roll(xn, width - ROPE_HALF, 1)
    bwd = pltpu.roll(xn, ROPE_HALF, 1)
    return xn * cos_t + fwd * sin_lo + bwd * sin_hi


def _tile_lanes(x, reps):
    return jnp.concatenate([x] * reps, axis=1) if reps > 1 else x


def _attn_stages(q_ref, kv_ref, taba_ref, tabb_ref, qw_ref, kw_ref, sink_ref, ones_ref, bias_ref, first_bias,
                 kprev_ref, vprev_ref, out):
    nseq, tq, _ = q_ref.shape
    nblk = tq // WINDOW
    units = [(s, b) for s in range(nseq) for b in range(nblk)]
    blk = lambda b: slice(b * WINDOW, (b + 1) * WINDOW)
    tb = tabb_ref[...]
    cos_b, sin_b = tb[:, :128], tb[:, 128:]
    rope, rope4 = [], []
    for b in range(nblk):
        ta = taba_ref[b][0:1, :]
        cos_a, sin_a = ta[:, :128], ta[:, 128:]
        tabs = _rope_tables(cos_a * cos_b - sin_a * sin_b, sin_a * cos_b + cos_a * sin_b)
        rope.append(tabs)
        rope4.append([_tile_lanes(x, 4) for x in tabs])
    q = {(s, b): _qk_norm_rope(q_ref[s, blk(b), :], qw_ref[...], *rope4[b], ones_ref) * (ATTN_SCALE * LOG2E)
         for s, b in units}
    kv = {(s, b): kv_ref[s, blk(b), :] for s, b in units}
    k_cur = {u: _qk_norm_rope(kv[u][:, 0:D_KV], kw_ref[...], *rope[u[1]], ones_ref) for u in units}
    v_cur = {u: kv[u][:, D_KV:] for u in units}
    k_all = {(s, b): jnp.concatenate([kprev_ref[s] if b == 0 else k_cur[(s, b - 1)], k_cur[(s, b)]], axis=0)
             for s, b in units}
    v_all = {(s, b): jnp.concatenate([vprev_ref[s] if b == 0 else v_cur[(s, b - 1)], v_cur[(s, b)]], axis=0)
             for s, b in units}
    out["k_cur"] = [k_cur[(s, nblk - 1)] for s in range(nseq)]
    out["v_cur"] = [v_cur[(s, nblk - 1)] for s in range(nseq)]
    for s in range(nseq):
        kprev_ref[s] = out["k_cur"][s]
        vprev_ref[s] = out["v_cur"][s]
    yield

    nk = 2 * WINDOW
    bias = [bias_ref[first_bias] if b == 0 else bias_ref[1] for b in range(nblk)]
    sinks = sink_ref[...] * LOG2E
    low = lax.broadcasted_iota(jnp.int32, (nk, D_KV), 1) < HEAD_DIM
    lane_blk = [ones_ref[j * HEAD_DIM:j * HEAD_DIM + 1, :] for j in range(GQA_GROUP)]

    chains = [(u, g) for u in units for g in range(N_KV_HEADS)]
    ch = range(len(chains))
    k_rot = {u: pltpu.roll(k_all[u], HEAD_DIM, 1) for u in units}
    k2 = [jnp.where(low, k_all[u], k_rot[u]) if g == 0 else jnp.where(low, k_rot[u], k_all[u])
          for u, g in chains]
    k4 = [jnp.concatenate([k2[c], k2[c]], axis=1).astype(BF16) for c in ch]
    vb = {u: v_all[u].astype(BF16) for u in units}
    qg = [q[u][:, g * QUAD:(g + 1) * QUAD].astype(BF16) for u, g in chains]
    qstack = [jnp.concatenate([qg[c] * lane_blk[j] for j in range(GQA_GROUP)], axis=0) for c in ch]
    sink_row = [jnp.concatenate(
        [jnp.broadcast_to(sinks[:, g * GQA_GROUP + j:g * GQA_GROUP + j + 1], (1, WINDOW))
         for j in range(GQA_GROUP)], axis=1) for u, g in chains]
    yield
    sc = [_mm_nt(k4[c], qstack[c]) + bias[u[1]] for c, (u, g) in enumerate(chains)]
    yield
    m = [jnp.maximum(jnp.max(sc[c], axis=0, keepdims=True), sink_row[c]) for c in ch]
    e = [jnp.exp2(sc[c] - m[c]) for c in ch]
    yield
    denom = [jnp.sum(e[c], axis=0, keepdims=True) + jnp.exp2(sink_row[c] - m[c]) for c in ch]
    ot = {(u, g): _mm_tn(vb[u], e[c].astype(BF16))[g * HEAD_DIM:(g + 1) * HEAD_DIM, :] * (1.0 / denom[c])
          for c, (u, g) in enumerate(chains)}
    yield
    ya = []
    for s in range(nseq):
        blocks = []
        for b in range(nblk):
            yt = jnp.concatenate([ot[((s, b), g)][:, j * WINDOW:(j + 1) * WINDOW]
                                  for g in range(N_KV_HEADS) for j in range(GQA_GROUP)], axis=0)
            blocks.append(jnp.transpose(yt))
        ya.append(jnp.concatenate(blocks, axis=0) if nblk > 1 else blocks[0])
    out["ya"] = ya


def _ffn_stages(x, yr, ya, wo_ref, nw_ref, wu_ref, wd_ref, out, pieces=4):
    mix = jnp.concatenate([yr, ya], axis=1).astype(BF16)
    x1 = x + _mm(mix, wo_ref[...])
    yield
    ms = jnp.mean(x1 * x1, axis=-1, keepdims=True)
    hf = ((x1 * lax.rsqrt(ms + RMS_EPS)) * nw_ref[...]).astype(BF16)
    acc = x1
    step = D_FF // pieces
    for j in range(pieces):
        up = _mm(hf, wu_ref[:, j * step:(j + 1) * step])
        yield
        act = jnp.square(jnp.maximum(up, 0.0)).astype(BF16)
        acc = acc + _mm(act, wd_ref[j * step:(j + 1) * step, :])
        yield
    out["y"] = acc


def _attn_ffn_kernel(q_ref, kv_ref, taba_ref, tabb_ref, qw_ref, kw_ref, sink_ref, ones_ref, bias_ref,
                     x_ref, yr_ref, xd_ref, yrd_ref, yad_ref, wo_ref, nw_ref, wu_ref, wd_ref,
                     o_ref, od_ref, kwin_ref, vwin_ref, kprev_ref, vprev_ref, ya_ref):
    i = pl.program_id(0)
    nseq, tq, _ = q_ref.shape
    seqs = range(nseq)
    first = i == 0

    @pl.when(first)
    def _():
        kprev_ref[...] = jnp.zeros_like(kprev_ref)
        vprev_ref[...] = jnp.zeros_like(vprev_ref)
        ya_ref[...] = jnp.zeros_like(ya_ref)

    def rows(ref, dec_ref):
        tile = jnp.concatenate([ref[s] for s in seqs], axis=0)
        dec = dec_ref[...]
        return jnp.where(first, jnp.concatenate([dec] * (tile.shape[0] // dec.shape[0]), axis=0), tile)

    a_out, f_out = {}, {}
    _interleave(
        _ffn_stages(rows(x_ref, xd_ref), rows(yr_ref, yrd_ref), rows(ya_ref, yad_ref),
                    wo_ref, nw_ref, wu_ref, wd_ref, f_out),
        _attn_stages(q_ref, kv_ref, taba_ref, tabb_ref, qw_ref, kw_ref, sink_ref, ones_ref, bias_ref,
                     jnp.minimum(i, 1), kprev_ref, vprev_ref, a_out))
    for s in seqs:
        o_ref[s] = f_out["y"][s * tq:(s + 1) * tq]
        ya_ref[s] = a_out["ya"][s]

    @pl.when(first)
    def _():
        od_ref[...] = f_out["y"][:od_ref.shape[0]].reshape(od_ref.shape)

    @pl.when(i == pl.num_programs(0) - 2)
    def _():
        for s in seqs:
            kwin_ref[s] = jnp.transpose(a_out["k_cur"][s])
            vwin_ref[s] = jnp.transpose(a_out["v_cur"][s])


def _band_bias():
    ki = np.arange(2 * WINDOW)[:, None]
    qi = (np.arange(GQA_GROUP * WINDOW) % WINDOW + WINDOW)[None, :]
    dq = qi - ki
    band = (dq >= 0) & (dq < WINDOW)
    first = band & (ki >= WINDOW)
    return jnp.asarray(np.where(np.stack([first, band]), 0.0, NEG_INF), F32)


def _rope_block_tables(nb):
    freq = _rope_lane_freq()
    ang_a = (jnp.arange(nb, dtype=F32) * WINDOW)[:, None] * freq
    ang_b = jnp.arange(WINDOW, dtype=F32)[:, None] * freq
    tab_a = jnp.concatenate([jnp.cos(ang_a), jnp.sin(ang_a)], axis=1)
    tab_b = jnp.concatenate([jnp.cos(ang_b), jnp.sin(ang_b)], axis=1)
    return jnp.broadcast_to(tab_a[:, None, :], (nb, 8, 4 * HEAD_DIM)), tab_b


ATTN_BLOCKS_PER_STEP = 1


def _attn_ffn(q3d, kv3d, tab_a, tab_b, qw, kw, sinks, ones_bd, bias, x3d, yr3d, xd, yrd, yad, wo, nw, wu, wd):
    bsz, t, _ = q3d.shape
    nd = xd.shape[0]
    nblk = ATTN_BLOCKS_PER_STEP
    tq = nblk * WINDOW
    nt = t // tq
    const = lambda shape: pl.BlockSpec(shape, lambda i: (0,) * len(shape))
    single = lambda shape: pl.BlockSpec(shape, lambda i: (0,) * len(shape), pipeline_mode=pl.Buffered(1))
    cur = lambda i: jnp.minimum(i, nt - 1)
    prv = lambda i: jnp.maximum(i - 1, 0)
    return pl.pallas_call(
        _attn_ffn_kernel,
        grid=(nt + 1,),
        in_specs=[
            pl.BlockSpec((bsz, tq, D_ATTN), lambda i: (0, cur(i), 0)),
            pl.BlockSpec((bsz, tq, 2 * D_KV), lambda i: (0, cur(i), 0)),
            pl.BlockSpec((nblk, 8, 4 * HEAD_DIM), lambda i: (cur(i), 0, 0)),
            const((WINDOW, 4 * HEAD_DIM)),
            const((1, D_ATTN)),
            const((1, D_KV)),
            const((1, N_Q_HEADS)),
            const((QUAD, QUAD)),
            const((2, 2 * WINDOW, GQA_GROUP * WINDOW)),
            pl.BlockSpec((bsz, tq, D_MODEL), lambda i: (0, prv(i), 0)),
            pl.BlockSpec((bsz, tq, D_RWKV), lambda i: (0, prv(i), 0)),
            const((nd, D_MODEL)),
            const((nd, D_RWKV)),
            const((nd, D_ATTN)),
            single((D_MODEL, D_MODEL)),
            const((1, D_MODEL)),
            single((D_MODEL, D_FF)),
            single((D_FF, D_MODEL)),
        ],
        out_specs=[
            pl.BlockSpec((bsz, tq, D_MODEL), lambda i: (0, prv(i), 0)),
            const((nd, 1, D_MODEL)),
            const((bsz, WINDOW, D_KV)),
            const((bsz, WINDOW, D_KV)),
        ],
        out_shape=[
            jax.ShapeDtypeStruct((bsz, t, D_MODEL), F32),
            jax.ShapeDtypeStruct((nd, 1, D_MODEL), F32),
            jax.ShapeDtypeStruct((bsz, WINDOW, D_KV), F32),
            jax.ShapeDtypeStruct((bsz, WINDOW, D_KV), F32),
        ],
        scratch_shapes=[
            pltpu.VMEM((bsz, WINDOW, D_KV), F32),
            pltpu.VMEM((bsz, WINDOW, D_KV), F32),
            pltpu.VMEM((bsz, tq, D_ATTN), F32),
        ],
        compiler_params=pltpu.CompilerParams(
            dimension_semantics=("arbitrary",), vmem_limit_bytes=VMEM_LIMIT),
        name="attn_ffn",
    )(q3d, kv3d, tab_a, tab_b, qw, kw, sinks, ones_bd, bias, x3d, yr3d, xd, yrd, yad, wo, nw, wu, wd)


DEC_TILE = 16


def _decode_prep_kernel(x_ref, nw_ref, w_ref, sh_ref, mu_ref, prm_ref, wl_ref, ones_ref, tab_ref,
                        qw_ref, kw_ref, p_ref, vec_ref, vgb_ref, qn_ref, kvn_ref):
    p, qkv = _norm_project(x_ref[...], nw_ref, w_ref)
    p_ref[...] = p
    xs = p + (sh_ref[...] - p) * mu_ref[...]
    r, logw, k, v, a, b, g, bonus = _rwkv_features(xs, prm_ref[...], wl_ref, ones_ref)
    for i, x in enumerate((a, b, k, jnp.exp(logw), r, v)):
        vec_ref[i] = jnp.transpose(x)
    vgb_ref[0] = g
    vgb_ref[1] = bonus
    n = p.shape[0]
    tab = jnp.broadcast_to(tab_ref[0:1, :], (n, 4 * HEAD_DIM))
    cos_t, sin_lo, sin_hi = _rope_tables(tab[:, :128], tab[:, 128:])
    qn_ref[...] = _qk_norm_rope(qkv[:, :D_ATTN], qw_ref[...], _tile_lanes(cos_t, 4), _tile_lanes(sin_lo, 4),
                                _tile_lanes(sin_hi, 4), ones_ref)
    kvn_ref[:, 0:D_KV] = _qk_norm_rope(qkv[:, D_ATTN:D_ATTN + D_KV], kw_ref[...], cos_t, sin_lo, sin_hi,
                                       ones_ref)
    kvn_ref[:, D_KV:] = qkv[:, D_ATTN + D_KV:]


def _decode_prep(x2d, norm_w, w_in_t, shift, mu_pad, prm, wl, bmask, tab, qw, kw):
    n = x2d.shape[0]
    full = lambda shape: pl.BlockSpec(shape, lambda i: (0,) * len(shape))
    return pl.pallas_call(
        _decode_prep_kernel,
        grid=(1,),
        in_specs=[full((n, D_MODEL)), full((1, D_MODEL)), full((D_IN, D_MODEL)), full((n, D_SHIFT)),
                  full((1, D_SHIFT)), full((16, D_RWKV)), full((D_LORA_PAD, 3 * D_RWKV)),
                  full((QUAD, QUAD)), full((8, 4 * HEAD_DIM)), full((1, D_ATTN)), full((1, D_KV))],
        out_specs=[full((n, D_SHIFT)), full((6, D_RWKV, n)), full((2, n, D_RWKV)), full((n, D_ATTN)),
                   full((n, 2 * D_KV))],
        out_shape=[
            jax.ShapeDtypeStruct((n, D_SHIFT), F32),
            jax.ShapeDtypeStruct((6, D_RWKV, n), F32),
            jax.ShapeDtypeStruct((2, n, D_RWKV), F32),
            jax.ShapeDtypeStruct((n, D_ATTN), F32),
            jax.ShapeDtypeStruct((n, 2 * D_KV), F32),
        ],
        compiler_params=pltpu.CompilerParams(
            dimension_semantics=("arbitrary",), vmem_limit_bytes=VMEM_LIMIT),
        name="decode_prep",
    )(x2d, norm_w, w_in_t, shift, mu_pad, prm, wl, bmask, tab, qw, kw)


def _decode_state_stages(vec_ref, s_ref, sout_ref, yt_ref, h):
    a_t, b_t, k_t, w_t, r_t = (vec_ref[i] for i in range(5))
    for i in range(HEAD_DIM):
        s = s_ref[0, i]
        sa = jnp.sum(s * a_t, axis=0, keepdims=True)
        v_i = vec_ref[5, i:i + 1, :]
        s_new = s * w_t + sa * b_t + v_i * k_t
        sout_ref[0, i] = s_new
        yt_ref[pl.ds(h * HEAD_DIM + i, 1), :] = jnp.sum(s_new * r_t, axis=0, keepdims=True)
        if i % 8 == 7:
            yield


def _decode_attn_stages(qr_ref, kvn_ref, col_ref, ck_ref, cv_ref, sink_ref, ya_ref, kout_ref, vout_ref):
    nh = N_Q_HEADS
    seqs = range(DEC_TILE)
    hrow = lax.broadcasted_iota(jnp.int32, (nh, D_ATTN), 0)
    hlane = lax.broadcasted_iota(jnp.int32, (nh, D_ATTN), 1) // HEAD_DIM
    dmask = hrow == hlane
    grow = lax.broadcasted_iota(jnp.int32, (nh, D_KV), 0) // GQA_GROUP
    glane = lax.broadcasted_iota(jnp.int32, (nh, D_KV), 1) // HEAD_DIM
    gmask = grow == glane
    low = glane == 0
    key_idx = lax.broadcasted_iota(jnp.int32, (nh, WINDOW), 1)
    last = lax.broadcasted_iota(jnp.int32, (D_KV, WINDOW), 1) == WINDOW - 1
    sink = sink_ref[...]
    kvn = kvn_ref[...]
    col = col_ref[0]
    k_new = [kvn[j:j + 1, 0:D_KV] for j in seqs]
    v_new = [kvn[j:j + 1, D_KV:] for j in seqs]
    ck = [ck_ref[j] for j in seqs]
    cv = [cv_ref[j] for j in seqs]
    for j in seqs:
        kout_ref[j] = jnp.where(last, col[0:D_KV, j:j + 1], pltpu.roll(ck[j], WINDOW - 1, 1))
        vout_ref[j] = jnp.where(last, col[D_KV:, j:j + 1], pltpu.roll(cv[j], WINDOW - 1, 1))
    yield
    q8 = [qr_ref[j * nh:(j + 1) * nh, :] for j in seqs]
    qp = [jnp.where(gmask, jnp.concatenate([q8[j], q8[j]], axis=1), 0.0) for j in seqs]
    s_c = [jnp.where(key_idx >= 1, jnp.dot(qp[j], ck[j], preferred_element_type=F32) * ATTN_SCALE, NEG_INF)
           for j in seqs]
    s_n = [jnp.sum(qp[j] * k_new[j], axis=-1, keepdims=True) * ATTN_SCALE for j in seqs]
    yield
    m = [jnp.maximum(jnp.maximum(jnp.max(s_c[j], axis=-1, keepdims=True), s_n[j]), sink) for j in seqs]
    yield
    e_c = [jnp.exp(s_c[j] - m[j]) for j in seqs]
    e_n = [jnp.exp(s_n[j] - m[j]) for j in seqs]
    denom = [jnp.sum(e_c[j], axis=-1, keepdims=True) + e_n[j] + jnp.exp(sink - m[j]) for j in seqs]
    yield
    o = [(_dot_nt_f32(e_c[j], cv[j]) + e_n[j] * v_new[j]) / denom[j]
         for j in seqs]
    yield
    out_rows = []
    for j in seqs:
        rot = pltpu.roll(o[j], HEAD_DIM, 1)
        g0 = jnp.where(low, o[j], rot)
        g1 = jnp.where(low, rot, o[j])
        wide = jnp.concatenate([g0, g0, g1, g1], axis=1)
        out_rows.append(jnp.sum(jnp.where(dmask, wide, 0.0), axis=0, keepdims=True))
    ya_ref[...] = jnp.concatenate(out_rows, axis=0)


def _decode_step_kernel(vec_ref, gb_ref, prm_ref, ones_ref, s_ref, qr_ref, kvn_ref, col_ref, ck_ref, cv_ref,
                        sink_ref, sout_ref, yr_ref, ya_ref, kout_ref, vout_ref, yt_ref):
    i = pl.program_id(0)
    _interleave(
        _decode_attn_stages(qr_ref, kvn_ref, col_ref, ck_ref, cv_ref, sink_ref, ya_ref, kout_ref, vout_ref),
        _decode_state_stages(vec_ref, s_ref, sout_ref, yt_ref, i))

    @pl.when(i == pl.num_programs(0) - 1)
    def _():
        y = jnp.transpose(yt_ref[...])
        yr_ref[...] = _rwkv_finish(y, gb_ref[0], gb_ref[1], prm_ref[...], ones_ref)


def _decode_step(vec_t, gb, prm, bmask, s_t, q_r, kvn, cols, ck_t, cv_t, sinks_col):
    n = s_t.shape[-1]
    bt = DEC_TILE
    assert n // bt == H_RWKV, "one sequence tile per RWKV head"
    const = lambda shape: pl.BlockSpec(shape, lambda i: (0,) * len(shape))
    return pl.pallas_call(
        _decode_step_kernel,
        grid=(H_RWKV,),
        in_specs=[
            pl.BlockSpec((6, HEAD_DIM, n), lambda i: (0, i, 0)),
            const((2, n, D_RWKV)),
            const((16, D_RWKV)),
            const((QUAD, QUAD)),
            pl.BlockSpec((1, HEAD_DIM, HEAD_DIM, n), lambda i: (i, 0, 0, 0)),
            pl.BlockSpec((bt * N_Q_HEADS, HEAD_DIM), lambda i: (i, 0)),
            pl.BlockSpec((bt, 2 * D_KV), lambda i: (i, 0)),
            pl.BlockSpec((1, 2 * D_KV, bt), lambda i: (i, 0, 0)),
            pl.BlockSpec((bt, D_KV, WINDOW), lambda i: (i, 0, 0)),
            pl.BlockSpec((bt, D_KV, WINDOW), lambda i: (i, 0, 0)),
            const((N_Q_HEADS, 1)),
        ],
        out_specs=[
            pl.BlockSpec((1, HEAD_DIM, HEAD_DIM, n), lambda i: (i, 0, 0, 0)),
            const((n, D_RWKV)),
            pl.BlockSpec((bt, D_ATTN), lambda i: (i, 0)),
            pl.BlockSpec((bt, D_KV, WINDOW), lambda i: (i, 0, 0)),
            pl.BlockSpec((bt, D_KV, WINDOW), lambda i: (i, 0, 0)),
        ],
        out_shape=[
            jax.ShapeDtypeStruct((H_RWKV, HEAD_DIM, HEAD_DIM, n), F32),
            jax.ShapeDtypeStruct((n, D_RWKV), F32),
            jax.ShapeDtypeStruct((n, D_ATTN), F32),
            jax.ShapeDtypeStruct((n, D_KV, WINDOW), F32),
            jax.ShapeDtypeStruct((n, D_KV, WINDOW), F32),
        ],
        scratch_shapes=[pltpu.VMEM((D_RWKV, n), F32)],
        compiler_params=pltpu.CompilerParams(
            dimension_semantics=("arbitrary",), vmem_limit_bytes=VMEM_LIMIT),
        name="decode_step",
    )(vec_t, gb, prm, bmask, s_t, q_r, kvn, cols, ck_t, cv_t, sinks_col)


def kernel(x_prompt, x_sample, state_wkv, state_shift, cache_k_win, cache_v_win, norm_mix_w, w_in, mu_shift, w0, w_decay_up, a0, w_a_up, w_g_up, k_k, k_a, r_k, ln_x_w, ln_x_b, q_norm_w, k_norm_w, sinks, w_out, norm_ffn_w, w_ffn_up, w_ffn_down):
    bsz, t, _ = x_prompt.shape
    nd = x_sample.shape[0]
    l = 0

    w_in_pad = jnp.swapaxes(w_in[l], 0, 1)
    mu_pad = mu_shift[l][None, :]
    wl = jnp.zeros((D_LORA_PAD, 3 * D_RWKV), F32)
    wl = wl.at[0:32, 0:D_RWKV].set(w_decay_up[l])
    wl = wl.at[32:64, D_RWKV:2 * D_RWKV].set(w_a_up[l])
    wl = wl.at[64:160, 2 * D_RWKV:].set(w_g_up[l])
    wl = wl.astype(BF16)
    prm = jnp.zeros((16, D_RWKV), F32)
    prm = prm.at[0].set(w0[l]).at[1].set(a0[l]).at[2].set(k_k[l]).at[3].set(k_a[l])
    prm = prm.at[4].set(r_k[l].reshape(-1)).at[5].set(ln_x_w[l]).at[6].set(ln_x_b[l])
    hid = np.arange(QUAD) // HEAD_DIM
    bmask = jnp.asarray(hid[:, None] == hid[None, :], BF16)
    tri = jnp.asarray(np.tril(np.ones((CHUNK, CHUNK))), BF16)
    qw = jnp.tile(q_norm_w[l][None, :], (1, N_Q_HEADS))
    kw = jnp.tile(k_norm_w[l][None, :], (1, N_KV_HEADS))
    nmw = norm_mix_w[l][None, :]
    nfw = norm_ffn_w[l][None, :]
    wo, wu, wd = w_out[l], w_ffn_up[l], w_ffn_down[l]
    tab_a, tab_b = _rope_block_tables(max(t, PAST_LEN + 1) // WINDOW + 1)
    ta, tb = tab_a[PAST_LEN // WINDOW], tab_b[PAST_LEN % WINDOW][None, :]
    tab_s = jnp.concatenate([ta[:, :128] * tb[:, :128] - ta[:, 128:] * tb[:, 128:],
                             ta[:, 128:] * tb[:, :128] + ta[:, :128] * tb[:, 128:]], axis=1)

    xs = x_sample.reshape(nd, D_MODEL)
    shift_in = state_shift[l].reshape(nd, D_SHIFT)
    p_s, vec_t, gb, qn_s, kvn_s = _decode_prep(xs, nmw, w_in_pad, shift_in, mu_pad, prm, wl, bmask, tab_s, qw, kw)
    s_t = jnp.transpose(state_wkv[l], (1, 2, 3, 0))
    ck_t = jnp.swapaxes(cache_k_win[l].reshape(nd, WINDOW, D_KV), 1, 2)
    cv_t = jnp.swapaxes(cache_v_win[l].reshape(nd, WINDOW, D_KV), 1, 2)
    q_r = qn_s.reshape(nd * N_Q_HEADS, HEAD_DIM)
    cols = jnp.swapaxes(kvn_s.reshape(nd // DEC_TILE, DEC_TILE, 2 * D_KV), 1, 2)
    wkv_t, yr_s, ya_s, kc_t, vc_t = _decode_step(vec_t, gb, prm, bmask, s_t, q_r, kvn_s, cols, ck_t, cv_t,
                                                 sinks[l][:, None])

    xp = x_prompt.reshape(bsz * t, D_MODEL)
    xs_p, plast, q_p, kv_p = _inproj_shift(xp, nmw, w_in_pad, mu_pad, t, 1024)
    yr_p, hbd = _rwkv_prompt(xs_p.reshape(bsz, t, D_SHIFT), prm, wl, tri, bmask)
    y_prompt, y_s, kwin_p, vwin_p = _attn_ffn(q_p.reshape(bsz, t, D_ATTN), kv_p.reshape(bsz, t, 2 * D_KV),
                                              tab_a, tab_b, qw, kw, sinks[l][None, :], bmask, _band_bias(),
                                              x_prompt, yr_p, xs, yr_s, ya_s, wo, nfw, wu, wd)
    hb = hbd.reshape(bsz, 2, 4, HEAD_DIM, 2, HEAD_DIM)
    wkv_prompt = jnp.stack([hb[:, :, j, :, j % 2, :] for j in range(4)], axis=2)
    wkv_prompt = wkv_prompt.reshape(bsz, H_RWKV, HEAD_DIM, HEAD_DIM)[None]
    shift_prompt = plast[:, 0:1, :][None]
    k_win_prompt = jnp.swapaxes(kwin_p, 1, 2).reshape(bsz, WINDOW, N_KV_HEADS, HEAD_DIM)[None]
    v_win_prompt = jnp.swapaxes(vwin_p, 1, 2).reshape(bsz, WINDOW, N_KV_HEADS, HEAD_DIM)[None]

    y_sample = y_s.reshape(nd, 1, D_MODEL)
    wkv_sample = jnp.transpose(wkv_t, (3, 0, 1, 2))[None]
    shift_sample = p_s.reshape(nd, 1, D_SHIFT)[None]
    k_win_sample = jnp.swapaxes(kc_t, 1, 2).reshape(nd, WINDOW, N_KV_HEADS, HEAD_DIM)[None]
    v_win_sample = jnp.swapaxes(vc_t, 1, 2).reshape(nd, WINDOW, N_KV_HEADS, HEAD_DIM)[None]

    return (y_prompt, y_sample, wkv_prompt, shift_prompt, k_win_prompt, v_win_prompt,
            wkv_sample, shift_sample, k_win_sample, v_win_sample)
```

```python
import functools

import jax
import jax.numpy as jnp
import numpy as np
from jax import lax
from jax.experimental import pallas as pl
from jax.experimental.pallas import tpu as pltpu

F32 = jnp.float32
BF16 = jnp.bfloat16

D_MODEL = 1024
D_RWKV = 512
D_ATTN = 512
HEAD_DIM = 64
H_RWKV = 8
N_Q_HEADS = 8
N_KV_HEADS = 2
GQA_GROUP = 4
D_KV = 128
D_LORA = 160
D_LORA_PAD = 256
D_SHIFT = 3 * D_RWKV + D_LORA
D_IN = D_SHIFT + D_ATTN + 2 * D_KV
WINDOW = 128
ROPE_DIM = 16
ROPE_HALF = 8
ROPE_THETA = 500000.0
ATTN_SCALE = HEAD_DIM ** -0.5
D_FF = 4096
RMS_EPS = 1e-6
LNX_EPS = 64e-5
NEG_INF = -1e30
LOG2E = 1.4426950408889634
PAST_LEN = 16384

CHUNK = 64
QUAD = 4 * HEAD_DIM
V7X_VMEM_BYTES = 64 * 1024 * 1024
VMEM_LIMIT = V7X_VMEM_BYTES // 8 * 7


def _split2(x):
    hi = x.astype(BF16)
    lo = (x - hi.astype(F32)).astype(BF16)
    return hi, lo


def _head_sums(xs, ones_ref):
    n, w = xs[0].shape
    tile = min(w, QUAD)
    per = w // tile
    pieces = [x[:, j * tile:(j + 1) * tile] for x in xs for j in range(per)]
    stacked = jnp.concatenate(pieces, axis=0) if len(pieces) > 1 else pieces[0]
    ones = ones_ref[0:tile, 0:tile]
    out = jnp.dot(stacked.astype(BF16), ones, preferred_element_type=F32)
    res = []
    for i in range(len(xs)):
        cols = [out[(i * per + j) * n:(i * per + j + 1) * n] for j in range(per)]
        res.append(jnp.concatenate(cols, axis=1) if per > 1 else cols[0])
    return res


def _cumsum_rows(tri_bf16, x):
    hi, lo = _split2(x)
    return (jnp.dot(tri_bf16, hi, preferred_element_type=F32)
            + jnp.dot(tri_bf16, lo, preferred_element_type=F32))


def _mm(a, b):
    return jnp.dot(a.astype(BF16), b.astype(BF16), preferred_element_type=F32)


def _mm_nt(a, b):
    return lax.dot_general(a.astype(BF16), b.astype(BF16), (((1,), (1,)), ((), ())),
                           preferred_element_type=F32)


def _mm_tn(a, b):
    return lax.dot_general(a.astype(BF16), b.astype(BF16), (((0,), (0,)), ((), ())),
                           preferred_element_type=F32)


def _dot_nt_f32(a, b):
    return lax.dot_general(a, b, (((1,), (1,)), ((), ())), preferred_element_type=F32)


def _sigmoid(x):
    return 1.0 / (1.0 + jnp.exp(-x))


def _interleave(*gens):
    live = list(gens)
    while live:
        for g in list(live):
            try:
                next(g)
            except StopIteration:
                live.remove(g)


def _norm_project(x, nw_ref, wt_ref):
    ms = jnp.mean(x * x, axis=-1, keepdims=True)
    h = ((x * lax.rsqrt(ms + RMS_EPS)) * nw_ref[...]).astype(BF16)
    return _mm_nt(h, wt_ref[0:D_SHIFT, :]), _mm_nt(h, wt_ref[D_SHIFT:, :])


def _inproj_shift_kernel(tiles_per_seq, x_ref, nw_ref, w_ref, mu_ref, xs_ref, last_ref, q_ref, kv_ref, prev_ref):
    i = pl.program_id(0)

    @pl.when(i % tiles_per_seq == 0)
    def _():
        prev_ref[...] = jnp.zeros_like(prev_ref)

    p, qkv = _norm_project(x_ref[...], nw_ref, w_ref)
    tm = p.shape[0]
    row = lax.broadcasted_iota(jnp.int32, p.shape, 0)
    prev = jnp.where(row == 0, jnp.broadcast_to(prev_ref[0:1, :], p.shape), pltpu.roll(p, 1, 0))
    xs_ref[...] = p + (prev - p) * mu_ref[...]
    last = jnp.broadcast_to(p[tm - 1:tm, :], prev_ref.shape)
    prev_ref[...] = last
    last_ref[0] = p[tm - 1:tm, :]
    q_ref[...] = qkv[:, :D_ATTN]
    kv_ref[...] = qkv[:, D_ATTN:]


def _inproj_shift(x2d, norm_w, w_in_pad, mu_pad, seq_len, tm):
    m = x2d.shape[0]
    tiles_per_seq = seq_len // tm
    return pl.pallas_call(
        functools.partial(_inproj_shift_kernel, tiles_per_seq),
        grid=(m // tm,),
        in_specs=[
            pl.BlockSpec((tm, D_MODEL), lambda i: (i, 0)),
            pl.BlockSpec((1, D_MODEL), lambda i: (0, 0)),
            pl.BlockSpec((D_IN, D_MODEL), lambda i: (0, 0)),
            pl.BlockSpec((1, D_SHIFT), lambda i: (0, 0)),
        ],
        out_specs=[
            pl.BlockSpec((tm, D_SHIFT), lambda i: (i, 0)),
            pl.BlockSpec((1, 1, D_SHIFT), lambda i: (i // tiles_per_seq, 0, 0)),
            pl.BlockSpec((tm, D_ATTN), lambda i: (i, 0)),
            pl.BlockSpec((tm, 2 * D_KV), lambda i: (i, 0)),
        ],
        out_shape=[
            jax.ShapeDtypeStruct((m, D_SHIFT), F32),
            jax.ShapeDtypeStruct((m // seq_len, 1, D_SHIFT), F32),
            jax.ShapeDtypeStruct((m, D_ATTN), F32),
            jax.ShapeDtypeStruct((m, 2 * D_KV), F32),
        ],
        scratch_shapes=[pltpu.VMEM((8, D_SHIFT), F32)],
        compiler_params=pltpu.CompilerParams(
            dimension_semantics=("arbitrary",), vmem_limit_bytes=VMEM_LIMIT),
        name="inproj_shift",
    )(x2d, norm_w, w_in_pad, mu_pad)


def _rwkv_features(xs, prm, wl_ref, ones_ref):
    r = xs[:, 0:D_RWKV]
    k = xs[:, D_RWKV:2 * D_RWKV]
    v = xs[:, 2 * D_RWKV:3 * D_RWKV]
    lora = xs[:, 3 * D_RWKV:]
    lora = jnp.concatenate([lora, jnp.zeros((lora.shape[0], wl_ref.shape[0] - D_LORA), F32)], axis=1)
    col = lax.broadcasted_iota(jnp.int32, lora.shape, 1)
    act = jnp.where(col < 32, jnp.tanh(lora), jnp.where(col < 64, lora, _sigmoid(lora)))
    up = jnp.dot(act.astype(BF16), wl_ref[...], preferred_element_type=F32)
    w0, a0, k_k, k_a, r_k = prm[0:1], prm[1:2], prm[2:3], prm[3:4], prm[4:5]
    logw = (-np.exp(-0.5)) * _sigmoid(w0 + up[:, 0:D_RWKV])
    asig = _sigmoid(a0 + up[:, D_RWKV:2 * D_RWKV])
    g = up[:, 2 * D_RWKV:]
    kk = k * k_k
    k_mod = k * (1.0 + (asig - 1.0) * k_a)
    ss, rk = _head_sums([kk * kk, r * k_mod * r_k], ones_ref)
    kk = kk / jnp.maximum(jnp.sqrt(ss), 1e-12)
    k = k_mod
    bonus = rk * v
    return r, logw, k, v, -kk, kk * asig, g, bonus


def _rwkv_finish(y, g, bonus, prm, ones_ref):
    ln_w, ln_b = prm[5:6], prm[6:7]
    mean = _head_sums([y], ones_ref)[0] * (1.0 / HEAD_DIM)
    d = y - mean
    var = _head_sums([d * d], ones_ref)[0] * (1.0 / HEAD_DIM)
    yn = d * lax.rsqrt(var + LNX_EPS) * ln_w + ln_b
    return (yn + bonus) * g


def _block_diag(x, bmask):
    return jnp.concatenate([x] * 4, axis=0) * bmask


def _chunk_prep(r, logw, k, v, a, b, tri_ref, bmask):
    n = len(r)
    ch = range(n)
    tri = tri_ref[...]
    cum = [_cumsum_rows(tri, logw[i]) for i in ch]
    yield
    e_in = [jnp.exp(cum[i]) for i in ch]
    e_ex = [jnp.exp(cum[i] - logw[i]) for i in ch]
    e_inv = [1.0 / e_in[i] for i in ch]
    e_last = [e_in[i][CHUNK - 1:CHUNK, :] for i in ch]
    rt = [(r[i] * e_in[i]).astype(BF16) for i in ch]
    at = [(a[i] * e_ex[i]).astype(BF16) for i in ch]
    kt = [(k[i] * e_inv[i]).astype(BF16) for i in ch]
    bt = [(b[i] * e_inv[i]).astype(BF16) for i in ch]
    vb = [v[i].astype(BF16) for i in ch]
    yield

    t_idx = lax.broadcasted_iota(jnp.int32, (CHUNK, QUAD), 0)
    s_idx = lax.broadcasted_iota(jnp.int32, (CHUNK, QUAD), 1) & (HEAD_DIM - 1)
    strict = s_idx < t_idx
    incl = s_idx <= t_idx

    gm = [_mm_nt(jnp.concatenate([at[i], rt[i]], axis=0),
                 jnp.concatenate([_block_diag(bt[i], bmask), _block_diag(kt[i], bmask)], axis=0))
          for i in ch]
    yield
    a_ab = [jnp.where(strict, gm[i][:CHUNK, :QUAD], 0.0) for i in ch]
    a_ak = [jnp.where(strict, gm[i][:CHUNK, QUAD:], 0.0) for i in ch]
    a_rb = [jnp.where(incl, gm[i][CHUNK:, :QUAD], 0.0) for i in ch]
    a_rk = [jnp.where(incl, gm[i][CHUNK:, QUAD:], 0.0) for i in ch]

    eye = jnp.where(s_idx == t_idx, 1.0, 0.0)
    pwb = [a_ab[i].astype(BF16) for i in ch]
    t_inv = [eye + a_ab[i] for i in ch]
    for it in range(6):
        rbd = [_block_diag(pwb[i], bmask) for i in ch]
        if it == 0:
            pwb = [_mm(pwb[i], rbd[i]).astype(BF16) for i in ch]
        elif it < 5:
            out = [_mm(jnp.concatenate([pwb[i], t_inv[i].astype(BF16)], axis=0), rbd[i]) for i in ch]
            pwb = [out[i][:CHUNK].astype(BF16) for i in ch]
            t_inv = [t_inv[i] + out[i][CHUNK:] for i in ch]
        else:
            t_inv = [t_inv[i] + _mm(t_inv[i], rbd[i]) for i in ch]
        yield

    vbd = [_block_diag(vb[i], bmask) for i in ch]
    xy0 = [_mm(jnp.concatenate([a_ak[i], a_rk[i]], axis=0), vbd[i]) for i in ch]
    return [dict(ar=jnp.concatenate([at[i], rt[i]], axis=0), x0=xy0[i][:CHUNK], y0=xy0[i][CHUNK:],
                 t_inv=t_inv[i].astype(BF16), a_rb=a_rb[i].astype(BF16), vb=vb[i],
                 bk=jnp.concatenate([bt[i], kt[i]], axis=0), e_last=e_last[i]) for i in ch]


def _chunk_step(pre, state, bmask, out):
    ch = range(len(pre))
    half = QUAD // 2
    zeros = jnp.zeros((half, half), BF16)
    sc = [state[i].astype(BF16) for i in ch]
    sb = [jnp.concatenate([jnp.concatenate([sc[i][:half], zeros], axis=1),
                           jnp.concatenate([zeros, sc[i][half:]], axis=1)], axis=0) for i in ch]
    xr = [_mm_nt(pre[i]["ar"], sb[i]) for i in ch]
    yield
    x = [xr[i][:CHUNK] + pre[i]["x0"] for i in ch]
    u = [_mm(pre[i]["t_inv"], _block_diag(x[i].astype(BF16), bmask)) for i in ch]
    yield
    ub = [u[i].astype(BF16) for i in ch]
    out["y"] = [xr[i][CHUNK:] + pre[i]["y0"] + _mm(pre[i]["a_rb"], _block_diag(ub[i], bmask)) for i in ch]
    upd = [_mm_tn(jnp.concatenate([ub[i], pre[i]["vb"]], axis=0), pre[i]["bk"]) for i in ch]
    yield
    bm = bmask[:half, :half].astype(F32)
    s_new = []
    for i in ch:
        e_last = pre[i]["e_last"]
        top = (state[i][:half] + upd[i][:half, :half] * bm) * e_last[:, :half]
        bot = (state[i][half:] + upd[i][half:, half:] * bm) * e_last[:, half:]
        s_new.append(jnp.concatenate([top, bot], axis=0))
    out["state"] = s_new


def _finish_stages(ys, g, bonus, prm, ones_ref, y_ref, j):
    nseq = len(ys) // 2
    ln_w, ln_b = prm[5:6], prm[6:7]
    y = jnp.concatenate([jnp.concatenate(ys[2 * s:2 * s + 2], axis=1) for s in range(nseq)], axis=0)
    mean = _head_sums([y], ones_ref)[0] * (1.0 / HEAD_DIM)
    yield
    d = y - mean
    var = _head_sums([d * d], ones_ref)[0] * (1.0 / HEAD_DIM)
    yield
    out = (d * lax.rsqrt(var + LNX_EPS) * ln_w + ln_b + bonus) * g
    for s in range(nseq):
        y_ref[s, j * CHUNK:(j + 1) * CHUNK, :] = out[s * CHUNK:(s + 1) * CHUNK]


def _rwkv_prompt_kernel(xs_ref, prm_ref, wl_ref, tri_ref, bmask_ref, y_ref, hout_ref, h_ref):
    c = pl.program_id(0)
    nseq, tstep, _ = xs_ref.shape
    nchunk = tstep // CHUNK

    @pl.when(c == 0)
    def _():
        h_ref[...] = jnp.zeros_like(h_ref)

    xs = jnp.concatenate([xs_ref[s] for s in range(nseq)], axis=0)
    prm = prm_ref[...]
    r, logw, k, v, a, b, g, bonus = _rwkv_features(xs, prm, wl_ref, bmask_ref)
    bmask = bmask_ref[...]
    lanes = [(s, q) for s in range(nseq) for q in range(2)]
    pre = {}

    def prep(chunks):
        chains = [(j, s, q) for j in chunks for s, q in lanes]
        cut = lambda x: [x[s * tstep + j * CHUNK:s * tstep + (j + 1) * CHUNK, q * QUAD:(q + 1) * QUAD]
                         for j, s, q in chains]
        res = yield from _chunk_prep(cut(r), cut(logw), cut(k), cut(v), cut(a), cut(b), tri_ref, bmask)
        for n, j in enumerate(chunks):
            pre[j] = res[n * len(lanes):(n + 1) * len(lanes)]

    nfirst = max(nchunk - 1, 1)
    _interleave(prep(range(nfirst)))
    later = prep(range(nfirst, nchunk))
    state = [h_ref[s, q] for s, q in lanes]
    crow = lambda x, j: jnp.concatenate([x[s * tstep + j * CHUNK:s * tstep + (j + 1) * CHUNK]
                                         for s in range(nseq)], axis=0)
    finish = iter(())
    for j in range(nchunk):
        res = {}
        for _ in _chunk_step(pre[j], state, bmask, res):
            next(later, None)
            next(finish, None)
        _interleave(finish)
        if j == nfirst - 1:
            _interleave(later)
        state = res["state"]
        finish = _finish_stages(res["y"], crow(g, j), crow(bonus, j), prm, bmask_ref, y_ref, j)
    _interleave(finish)
    for i, (s, q) in enumerate(lanes):
        h_ref[s, q] = state[i]

    @pl.when(c == pl.num_programs(0) - 1)
    def _():
        hout_ref[...] = h_ref[...]


RWKV_CHUNKS_PER_STEP = 4


def _rwkv_prompt(xs3d, prm, wl, tri, bmask):
    bsz, t, _ = xs3d.shape
    tstep = RWKV_CHUNKS_PER_STEP * CHUNK
    nc = t // tstep
    const = lambda shape: pl.BlockSpec(shape, lambda c: (0,) * len(shape))
    state_shape = (bsz, 2, QUAD, QUAD // 2)
    return pl.pallas_call(
        _rwkv_prompt_kernel,
        grid=(nc,),
        in_specs=[
            pl.BlockSpec((bsz, tstep, D_SHIFT), lambda c: (0, c, 0)),
            const((16, D_RWKV)),
            const((D_LORA_PAD, 3 * D_RWKV)),
            const((CHUNK, CHUNK)),
            const((QUAD, QUAD)),
        ],
        out_specs=[
            pl.BlockSpec((bsz, tstep, D_RWKV), lambda c: (0, c, 0)),
            const(state_shape),
        ],
        out_shape=[
            jax.ShapeDtypeStruct((bsz, t, D_RWKV), F32),
            jax.ShapeDtypeStruct(state_shape, F32),
        ],
        scratch_shapes=[pltpu.VMEM(state_shape, F32)],
        compiler_params=pltpu.CompilerParams(
            dimension_semantics=("arbitrary",), vmem_limit_bytes=VMEM_LIMIT),
        name="rwkv_prompt",
    )(xs3d, prm, wl, tri, bmask)


def _rope_lane_freq():
    pair = jnp.asarray((np.arange(2 * HEAD_DIM) % HEAD_DIM) % ROPE_HALF, F32)
    return jnp.power(ROPE_THETA, -pair * (2.0 / ROPE_DIM))[None, :]


def _rope_tables(cos, sin):
    dim = lax.broadcasted_iota(jnp.int32, cos.shape, 1) & (HEAD_DIM - 1)
    cos_t = jnp.where(dim < ROPE_DIM, cos, 1.0)
    sin_lo = jnp.where(dim < ROPE_HALF, -sin, 0.0)
    sin_hi = jnp.where((dim >= ROPE_HALF) & (dim < ROPE_DIM), sin, 0.0)
    return cos_t, sin_lo, sin_hi


def _qk_norm_rope(x, norm_w, cos_t, sin_lo, sin_hi, ones_ref):
    ms = _head_sums([x * x], ones_ref)[0] * (1.0 / HEAD_DIM)
    xn = x * lax.rsqrt(ms + RMS_EPS) * norm_w
    width = x.shape[1]
    fwd = pltpu.roll(xn, width - ROPE_HALF, 1)
    bwd = pltpu.roll(xn, ROPE_HALF, 1)
    return xn * cos_t + fwd * sin_lo + bwd * sin_hi


def _tile_lanes(x, reps):
    return jnp.concatenate([x] * reps, axis=1) if reps > 1 else x


def _attn_stages(q_ref, kv_ref, taba_ref, tabb_ref, qw_ref, kw_ref, sink_ref, ones_ref, bias_ref, first_bias,
                 kprev_ref, vprev_ref, out):
    nseq, tq, _ = q_ref.shape
    nblk = tq // WINDOW
    units = [(s, b) for s in range(nseq) for b in range(nblk)]
    blk = lambda b: slice(b * WINDOW, (b + 1) * WINDOW)
    tb = tabb_ref[...]
    cos_b, sin_b = tb[:, :128], tb[:, 128:]
    rope, rope4 = [], []
    for b in range(nblk):
        ta = taba_ref[b]
        cos_a, sin_a = ta[:, :128], ta[:, 128:]
        tabs = _rope_tables(cos_a * cos_b - sin_a * sin_b, sin_a * cos_b + cos_a * sin_b)
        rope.append(tabs)
        rope4.append([_tile_lanes(x, 4) for x in tabs])
    q = {(s, b): _qk_norm_rope(q_ref[s, blk(b), :], qw_ref[...], *rope4[b], ones_ref) * (ATTN_SCALE * LOG2E)
         for s, b in units}
    kv = {(s, b): kv_ref[s, blk(b), :] for s, b in units}
    k_cur = {u: _qk_norm_rope(kv[u][:, 0:D_KV], kw_ref[...], *rope[u[1]], ones_ref) for u in units}
    v_cur = {u: kv[u][:, D_KV:] for u in units}
    k_all = {(s, b): jnp.concatenate([kprev_ref[s] if b == 0 else k_cur[(s, b - 1)], k_cur[(s, b)]], axis=0)
             for s, b in units}
    v_all = {(s, b): jnp.concatenate([vprev_ref[s] if b == 0 else v_cur[(s, b - 1)], v_cur[(s, b)]], axis=0)
             for s, b in units}
    out["k_cur"] = [k_cur[(s, nblk - 1)] for s in range(nseq)]
    out["v_cur"] = [v_cur[(s, nblk - 1)] for s in range(nseq)]
    for s in range(nseq):
        kprev_ref[s] = out["k_cur"][s]
        vprev_ref[s] = out["v_cur"][s]
    yield

    nk = 2 * WINDOW
    bias = [bias_ref[first_bias] if b == 0 else bias_ref[1] for b in range(nblk)]
    sinks = sink_ref[...] * LOG2E
    low = lax.broadcasted_iota(jnp.int32, (nk, D_KV), 1) < HEAD_DIM
    lane_blk = [ones_ref[j * HEAD_DIM:j * HEAD_DIM + 1, :] for j in range(GQA_GROUP)]

    chains = [(u, g) for u in units for g in range(N_KV_HEADS)]
    ch = range(len(chains))
    k_rot = {u: pltpu.roll(k_all[u], HEAD_DIM, 1) for u in units}
    k2 = [jnp.where(low, k_all[u], k_rot[u]) if g == 0 else jnp.where(low, k_rot[u], k_all[u])
          for u, g in chains]
    k4 = [jnp.concatenate([k2[c], k2[c]], axis=1).astype(BF16) for c in ch]
    vb = {u: v_all[u].astype(BF16) for u in units}
    qg = [q[u][:, g * QUAD:(g + 1) * QUAD].astype(BF16) for u, g in chains]
    qstack = [jnp.concatenate([qg[c] * lane_blk[j] for j in range(GQA_GROUP)], axis=0) for c in ch]
    sink_row = [jnp.concatenate(
        [jnp.broadcast_to(sinks[:, g * GQA_GROUP + j:g * GQA_GROUP + j + 1], (1, WINDOW))
         for j in range(GQA_GROUP)], axis=1) for u, g in chains]
    yield
    sc = [_mm_nt(k4[c], qstack[c]) + bias[u[1]] for c, (u, g) in enumerate(chains)]
    yield
    m = [jnp.maximum(jnp.max(sc[c], axis=0, keepdims=True), sink_row[c]) for c in ch]
    e = [jnp.exp2(sc[c] - m[c]) for c in ch]
    yield
    denom = [jnp.sum(e[c], axis=0, keepdims=True) + jnp.exp2(sink_row[c] - m[c]) for c in ch]
    ot = {(u, g): _mm_tn(vb[u], e[c].astype(BF16))[g * HEAD_DIM:(g + 1) * HEAD_DIM, :] * (1.0 / denom[c])
          for c, (u, g) in enumerate(chains)}
    yield
    ya = []
    for s in range(nseq):
        blocks = []
        for b in range(nblk):
            yt = jnp.concatenate([ot[((s, b), g)][:, j * WINDOW:(j + 1) * WINDOW]
                                  for g in range(N_KV_HEADS) for j in range(GQA_GROUP)], axis=0)
            blocks.append(jnp.transpose(yt))
        ya.append(jnp.concatenate(blocks, axis=0) if nblk > 1 else blocks[0])
    out["ya"] = ya


def _ffn_stages(x, yr, ya, wo_ref, nw_ref, wu_ref, wd_ref, out, pieces=4):
    mix = jnp.concatenate([yr, ya], axis=1).astype(BF16)
    x1 = x + _mm(mix, wo_ref[...])
    yield
    ms = jnp.mean(x1 * x1, axis=-1, keepdims=True)
    hf = ((x1 * lax.rsqrt(ms + RMS_EPS)) * nw_ref[...]).astype(BF16)
    acc = x1
    step = D_FF // pieces
    for j in range(pieces):
        up = _mm(hf, wu_ref[:, j * step:(j + 1) * step])
        yield
        act = jnp.square(jnp.maximum(up, 0.0)).astype(BF16)
        acc = acc + _mm(act, wd_ref[j * step:(j + 1) * step, :])
        yield
    out["y"] = acc


def _attn_ffn_kernel(q_ref, kv_ref, taba_ref, tabb_ref, qw_ref, kw_ref, sink_ref, ones_ref, bias_ref,
                     x_ref, yr_ref, xd_ref, yrd_ref, yad_ref, wo_ref, nw_ref, wu_ref, wd_ref,
                     o_ref, od_ref, kwin_ref, vwin_ref, kprev_ref, vprev_ref, ya_ref):
    i = pl.program_id(0)
    nseq, tq, _ = q_ref.shape
    seqs = range(nseq)
    first = i == 0

    @pl.when(first)
    def _():
        kprev_ref[...] = jnp.zeros_like(kprev_ref)
        vprev_ref[...] = jnp.zeros_like(vprev_ref)
        ya_ref[...] = jnp.zeros_like(ya_ref)

    def rows(ref, dec_ref):
        tile = jnp.concatenate([ref[s] for s in seqs], axis=0)
        dec = dec_ref[...]
        return jnp.where(first, jnp.concatenate([dec] * (tile.shape[0] // dec.shape[0]), axis=0), tile)

    a_out, f_out = {}, {}
    _interleave(
        _ffn_stages(rows(x_ref, xd_ref), rows(yr_ref, yrd_ref), rows(ya_ref, yad_ref),
                    wo_ref, nw_ref, wu_ref, wd_ref, f_out),
        _attn_stages(q_ref, kv_ref, taba_ref, tabb_ref, qw_ref, kw_ref, sink_ref, ones_ref, bias_ref,
                     jnp.minimum(i, 1), kprev_ref, vprev_ref, a_out))
    for s in seqs:
        o_ref[s] = f_out["y"][s * tq:(s + 1) * tq]
        ya_ref[s] = a_out["ya"][s]

    @pl.when(first)
    def _():
        od_ref[...] = f_out["y"][:od_ref.shape[0]].reshape(od_ref.shape)

    @pl.when(i == pl.num_programs(0) - 2)
    def _():
        for s in seqs:
            kwin_ref[s] = jnp.transpose(a_out["k_cur"][s])
            vwin_ref[s] = jnp.transpose(a_out["v_cur"][s])


def _band_bias():
    ki = np.arange(2 * WINDOW)[:, None]
    qi = (np.arange(GQA_GROUP * WINDOW) % WINDOW + WINDOW)[None, :]
    dq = qi - ki
    band = (dq >= 0) & (dq < WINDOW)
    first = band & (ki >= WINDOW)
    return jnp.asarray(np.where(np.stack([first, band]), 0.0, NEG_INF), F32)


def _rope_block_tables(nb):
    freq = _rope_lane_freq()
    ang_a = (jnp.arange(nb, dtype=F32) * WINDOW)[:, None] * freq
    ang_b = jnp.arange(WINDOW, dtype=F32)[:, None] * freq
    tab_a = jnp.concatenate([jnp.cos(ang_a), jnp.sin(ang_a)], axis=1)
    tab_b = jnp.concatenate([jnp.cos(ang_b), jnp.sin(ang_b)], axis=1)
    return tab_a[:, None, :], tab_b


ATTN_BLOCKS_PER_STEP = 1


def _attn_ffn(q3d, kv3d, tab_a, tab_b, qw, kw, sinks, ones_bd, bias, x3d, yr3d, xd, yrd, yad, wo, nw, wu, wd):
    bsz, t, _ = q3d.shape
    nd = xd.shape[0]
    nblk = ATTN_BLOCKS_PER_STEP
    tq = nblk * WINDOW
    nt = t // tq
    const = lambda shape: pl.BlockSpec(shape, lambda i: (0,) * len(shape))
    single = lambda shape: pl.BlockSpec(shape, lambda i: (0,) * len(shape), pipeline_mode=pl.Buffered(1))
    cur = lambda i: jnp.minimum(i, nt - 1)
    prv = lambda i: jnp.maximum(i - 1, 0)
    return pl.pallas_call(
        _attn_ffn_kernel,
        grid=(nt + 1,),
        in_specs=[
            pl.BlockSpec((bsz, tq, D_ATTN), lambda i: (0, cur(i), 0)),
            pl.BlockSpec((bsz, tq, 2 * D_KV), lambda i: (0, cur(i), 0)),
            pl.BlockSpec((nblk, 1, 4 * HEAD_DIM), lambda i: (cur(i), 0, 0)),
            const((WINDOW, 4 * HEAD_DIM)),
            const((1, D_ATTN)),
            const((1, D_KV)),
            const((1, N_Q_HEADS)),
            const((QUAD, QUAD)),
            const((2, 2 * WINDOW, GQA_GROUP * WINDOW)),
            pl.BlockSpec((bsz, tq, D_MODEL), lambda i: (0, prv(i), 0)),
            pl.BlockSpec((bsz, tq, D_RWKV), lambda i: (0, prv(i), 0)),
            const((nd, D_MODEL)),
            const((nd, D_RWKV)),
            const((nd, D_ATTN)),
            single((D_MODEL, D_MODEL)),
            const((1, D_MODEL)),
            single((D_MODEL, D_FF)),
            single((D_FF, D_MODEL)),
        ],
        out_specs=[
            pl.BlockSpec((bsz, tq, D_MODEL), lambda i: (0, prv(i), 0)),
            const((nd, 1, D_MODEL)),
            const((bsz, WINDOW, D_KV)),
            const((bsz, WINDOW, D_KV)),
        ],
        out_shape=[
            jax.ShapeDtypeStruct((bsz, t, D_MODEL), F32),
            jax.ShapeDtypeStruct((nd, 1, D_MODEL), F32),
            jax.ShapeDtypeStruct((bsz, WINDOW, D_KV), F32),
            jax.ShapeDtypeStruct((bsz, WINDOW, D_KV), F32),
        ],
        scratch_shapes=[
            pltpu.VMEM((bsz, WINDOW, D_KV), F32),
            pltpu.VMEM((bsz, WINDOW, D_KV), F32),
            pltpu.VMEM((bsz, tq, D_ATTN), F32),
        ],
        compiler_params=pltpu.CompilerParams(
            dimension_semantics=("arbitrary",), vmem_limit_bytes=VMEM_LIMIT),
        name="attn_ffn",
    )(q3d, kv3d, tab_a, tab_b, qw, kw, sinks, ones_bd, bias, x3d, yr3d, xd, yrd, yad, wo, nw, wu, wd)


DEC_TILE = 16


def _decode_prep_kernel(x_ref, nw_ref, w_ref, sh_ref, mu_ref, prm_ref, wl_ref, ones_ref, tab_ref,
                        qw_ref, kw_ref, p_ref, vec_ref, vgb_ref, qn_ref, kvn_ref, col_ref):
    p, qkv = _norm_project(x_ref[...], nw_ref, w_ref)
    p_ref[...] = p
    xs = p + (sh_ref[...] - p) * mu_ref[...]
    r, logw, k, v, a, b, g, bonus = _rwkv_features(xs, prm_ref[...], wl_ref, ones_ref)
    for i, x in enumerate((a, b, k, jnp.exp(logw), r, v)):
        vec_ref[i] = jnp.transpose(x)
    vgb_ref[0] = g
    vgb_ref[1] = bonus
    n = p.shape[0]
    tab = jnp.broadcast_to(tab_ref[...], (n, 4 * HEAD_DIM))
    cos_t, sin_lo, sin_hi = _rope_tables(tab[:, :128], tab[:, 128:])
    qn_ref[...] = _qk_norm_rope(qkv[:, :D_ATTN], qw_ref[...], _tile_lanes(cos_t, 4), _tile_lanes(sin_lo, 4),
                                _tile_lanes(sin_hi, 4), ones_ref)
    k_new = _qk_norm_rope(qkv[:, D_ATTN:D_ATTN + D_KV], kw_ref[...], cos_t, sin_lo, sin_hi, ones_ref)
    kvn = jnp.concatenate([k_new, qkv[:, D_ATTN + D_KV:]], axis=1)
    kvn_ref[...] = kvn
    kvn_t = jnp.transpose(kvn)
    for i in range(n // DEC_TILE):
        col_ref[i] = kvn_t[:, i * DEC_TILE:(i + 1) * DEC_TILE]


def _decode_prep(x2d, norm_w, w_in_t, shift, mu_pad, prm, wl, bmask, tab, qw, kw):
    n = x2d.shape[0]
    full = lambda shape: pl.BlockSpec(shape, lambda i: (0,) * len(shape))
    return pl.pallas_call(
        _decode_prep_kernel,
        grid=(1,),
        in_specs=[full((n, D_MODEL)), full((1, D_MODEL)), full((D_IN, D_MODEL)), full((n, D_SHIFT)),
                  full((1, D_SHIFT)), full((16, D_RWKV)), full((D_LORA_PAD, 3 * D_RWKV)),
                  full((QUAD, QUAD)), full((1, 4 * HEAD_DIM)), full((1, D_ATTN)), full((1, D_KV))],
        out_specs=[full((n, D_SHIFT)), full((6, D_RWKV, n)), full((2, n, D_RWKV)), full((n, D_ATTN)),
                   full((n, 2 * D_KV)), full((n // DEC_TILE, 2 * D_KV, DEC_TILE))],
        out_shape=[
            jax.ShapeDtypeStruct((n, D_SHIFT), F32),
            jax.ShapeDtypeStruct((6, D_RWKV, n), F32),
            jax.ShapeDtypeStruct((2, n, D_RWKV), F32),
            jax.ShapeDtypeStruct((n, D_ATTN), F32),
            jax.ShapeDtypeStruct((n, 2 * D_KV), F32),
            jax.ShapeDtypeStruct((n // DEC_TILE, 2 * D_KV, DEC_TILE), F32),
        ],
        compiler_params=pltpu.CompilerParams(
            dimension_semantics=("arbitrary",), vmem_limit_bytes=VMEM_LIMIT),
        name="decode_prep",
    )(x2d, norm_w, w_in_t, shift, mu_pad, prm, wl, bmask, tab, qw, kw)


def _decode_state_stages(vec_ref, s_ref, sout_ref, yt_ref, h):
    a_t, b_t, k_t, w_t, r_t = (vec_ref[i] for i in range(5))
    for i in range(HEAD_DIM):
        s = s_ref[0, i]
        sa = jnp.sum(s * a_t, axis=0, keepdims=True)
        v_i = vec_ref[5, i:i + 1, :]
        s_new = s * w_t + sa * b_t + v_i * k_t
        sout_ref[0, i] = s_new
        yt_ref[pl.ds(h * HEAD_DIM + i, 1), :] = jnp.sum(s_new * r_t, axis=0, keepdims=True)
        if i % 8 == 7:
            yield


def _decode_attn_stages(qn_ref, kvn_ref, col_ref, ck_ref, cv_ref, sink_ref, ya_ref, kout_ref, vout_ref):
    nh = N_Q_HEADS
    seqs = range(DEC_TILE)
    hrow = lax.broadcasted_iota(jnp.int32, (nh, D_ATTN), 0)
    hlane = lax.broadcasted_iota(jnp.int32, (nh, D_ATTN), 1) // HEAD_DIM
    dmask = hrow == hlane
    grow = lax.broadcasted_iota(jnp.int32, (nh, D_KV), 0) // GQA_GROUP
    glane = lax.broadcasted_iota(jnp.int32, (nh, D_KV), 1) // HEAD_DIM
    low = glane == 0
    key_idx = lax.broadcasted_iota(jnp.int32, (nh, WINDOW), 1)
    last = lax.broadcasted_iota(jnp.int32, (D_KV, WINDOW), 1) == WINDOW - 1
    sink = sink_ref[...]
    kvn = kvn_ref[...]
    col = col_ref[0]
    k_new = [kvn[j:j + 1, 0:D_KV] for j in seqs]
    v_new = [kvn[j:j + 1, D_KV:] for j in seqs]
    ck = [ck_ref[j] for j in seqs]
    cv = [cv_ref[j] for j in seqs]
    for j in seqs:
        kout_ref[j] = jnp.where(last, col[0:D_KV, j:j + 1], pltpu.roll(ck[j], WINDOW - 1, 1))
        vout_ref[j] = jnp.where(last, col[D_KV:, j:j + 1], pltpu.roll(cv[j], WINDOW - 1, 1))
    yield
    qn = qn_ref[...]
    swap = (grow != (lax.broadcasted_iota(jnp.int32, (nh, D_KV), 0) & 1))
    qp = []
    for j in seqs:
        own = jnp.where(dmask, jnp.broadcast_to(qn[j:j + 1, :], (nh, D_ATTN)), 0.0)
        fold = own[:, 0:D_KV] + own[:, D_KV:2 * D_KV] + own[:, 2 * D_KV:3 * D_KV] + own[:, 3 * D_KV:]
        qp.append(jnp.where(swap, pltpu.roll(fold, HEAD_DIM, 1), fold))
    s_c = [jnp.where(key_idx >= 1, jnp.dot(qp[j], ck[j], preferred_element_type=F32) * ATTN_SCALE, NEG_INF)
           for j in seqs]
    s_n = [jnp.sum(qp[j] * k_new[j], axis=-1, keepdims=True) * ATTN_SCALE for j in seqs]
    yield
    m = [jnp.maximum(jnp.maximum(jnp.max(s_c[j], axis=-1, keepdims=True), s_n[j]), sink) for j in seqs]
    yield
    e_c = [jnp.exp(s_c[j] - m[j]) for j in seqs]
    e_n = [jnp.exp(s_n[j] - m[j]) for j in seqs]
    denom = [jnp.sum(e_c[j], axis=-1, keepdims=True) + e_n[j] + jnp.exp(sink - m[j]) for j in seqs]
    yield
    o = [(_dot_nt_f32(e_c[j], cv[j]) + e_n[j] * v_new[j]) / denom[j]
         for j in seqs]
    yield
    out_rows = []
    for j in seqs:
        rot = pltpu.roll(o[j], HEAD_DIM, 1)
        g0 = jnp.where(low, o[j], rot)
        g1 = jnp.where(low, rot, o[j])
        wide = jnp.concatenate([g0, g0, g1, g1], axis=1)
        out_rows.append(jnp.sum(jnp.where(dmask, wide, 0.0), axis=0, keepdims=True))
    ya_ref[...] = jnp.concatenate(out_rows, axis=0)


def _decode_step_kernel(vec_ref, gb_ref, prm_ref, ones_ref, s_ref, qn_ref, kvn_ref, col_ref, ck_ref, cv_ref,
                        sink_ref, sout_ref, yr_ref, ya_ref, kout_ref, vout_ref, yt_ref):
    i = pl.program_id(0)
    _interleave(
        _decode_attn_stages(qn_ref, kvn_ref, col_ref, ck_ref, cv_ref, sink_ref, ya_ref, kout_ref, vout_ref),
        _decode_state_stages(vec_ref, s_ref, sout_ref, yt_ref, i))

    @pl.when(i == pl.num_programs(0) - 1)
    def _():
        y = jnp.transpose(yt_ref[...])
        yr_ref[...] = _rwkv_finish(y, gb_ref[0], gb_ref[1], prm_ref[...], ones_ref)


def _decode_step(vec_t, gb, prm, bmask, s_t, qn, kvn, cols, ck_t, cv_t, sinks_col):
    n = s_t.shape[-1]
    bt = DEC_TILE
    assert n // bt == H_RWKV, "one sequence tile per RWKV head"
    const = lambda shape: pl.BlockSpec(shape, lambda i: (0,) * len(shape))
    return pl.pallas_call(
        _decode_step_kernel,
        grid=(H_RWKV,),
        in_specs=[
            pl.BlockSpec((6, HEAD_DIM, n), lambda i: (0, i, 0)),
            const((2, n, D_RWKV)),
            const((16, D_RWKV)),
            const((QUAD, QUAD)),
            pl.BlockSpec((1, HEAD_DIM, HEAD_DIM, n), lambda i: (i, 0, 0, 0)),
            pl.BlockSpec((bt, D_ATTN), lambda i: (i, 0)),
            pl.BlockSpec((bt, 2 * D_KV), lambda i: (i, 0)),
            pl.BlockSpec((1, 2 * D_KV, bt), lambda i: (i, 0, 0)),
            pl.BlockSpec((bt, D_KV, WINDOW), lambda i: (i, 0, 0)),
            pl.BlockSpec((bt, D_KV, WINDOW), lambda i: (i, 0, 0)),
            const((N_Q_HEADS, 1)),
        ],
        out_specs=[
            pl.BlockSpec((1, HEAD_DIM, HEAD_DIM, n), lambda i: (i, 0, 0, 0)),
            const((n, D_RWKV)),
            pl.BlockSpec((bt, D_ATTN), lambda i: (i, 0)),
            pl.BlockSpec((bt, D_KV, WINDOW), lambda i: (i, 0, 0)),
            pl.BlockSpec((bt, D_KV, WINDOW), lambda i: (i, 0, 0)),
        ],
        out_shape=[
            jax.ShapeDtypeStruct((H_RWKV, HEAD_DIM, HEAD_DIM, n), F32),
            jax.ShapeDtypeStruct((n, D_RWKV), F32),
            jax.ShapeDtypeStruct((n, D_ATTN), F32),
            jax.ShapeDtypeStruct((n, D_KV, WINDOW), F32),
            jax.ShapeDtypeStruct((n, D_KV, WINDOW), F32),
        ],
        scratch_shapes=[pltpu.VMEM((D_RWKV, n), F32)],
        compiler_params=pltpu.CompilerParams(
            dimension_semantics=("arbitrary",), vmem_limit_bytes=VMEM_LIMIT),
        name="decode_step",
    )(vec_t, gb, prm, bmask, s_t, qn, kvn, cols, ck_t, cv_t, sinks_col)


def kernel(x_prompt, x_sample, state_wkv, state_shift, cache_k_win, cache_v_win, norm_mix_w, w_in, mu_shift, w0, w_decay_up, a0, w_a_up, w_g_up, k_k, k_a, r_k, ln_x_w, ln_x_b, q_norm_w, k_norm_w, sinks, w_out, norm_ffn_w, w_ffn_up, w_ffn_down):
    bsz, t, _ = x_prompt.shape
    nd = x_sample.shape[0]
    l = 0

    w_in_pad = jnp.swapaxes(w_in[l], 0, 1)
    mu_pad = mu_shift[l][None, :]
    wl = jnp.zeros((D_LORA_PAD, 3 * D_RWKV), F32)
    wl = wl.at[0:32, 0:D_RWKV].set(w_decay_up[l])
    wl = wl.at[32:64, D_RWKV:2 * D_RWKV].set(w_a_up[l])
    wl = wl.at[64:160, 2 * D_RWKV:].set(w_g_up[l])
    wl = wl.astype(BF16)
    prm = jnp.zeros((16, D_RWKV), F32)
    prm = prm.at[0].set(w0[l]).at[1].set(a0[l]).at[2].set(k_k[l]).at[3].set(k_a[l])
    prm = prm.at[4].set(r_k[l].reshape(-1)).at[5].set(ln_x_w[l]).at[6].set(ln_x_b[l])
    hid = np.arange(QUAD) // HEAD_DIM
    bmask = jnp.asarray(hid[:, None] == hid[None, :], BF16)
    tri = jnp.asarray(np.tril(np.ones((CHUNK, CHUNK))), BF16)
    qw = jnp.tile(q_norm_w[l][None, :], (1, N_Q_HEADS))
    kw = jnp.tile(k_norm_w[l][None, :], (1, N_KV_HEADS))
    nmw = norm_mix_w[l][None, :]
    nfw = norm_ffn_w[l][None, :]
    wo, wu, wd = w_out[l], w_ffn_up[l], w_ffn_down[l]
    tab_a, tab_b = _rope_block_tables(max(t, PAST_LEN + 1) // WINDOW + 1)
    ta, tb = tab_a[PAST_LEN // WINDOW], tab_b[PAST_LEN % WINDOW][None, :]
    tab_s = jnp.concatenate([ta[:, :128] * tb[:, :128] - ta[:, 128:] * tb[:, 128:],
                             ta[:, 128:] * tb[:, :128] + ta[:, :128] * tb[:, 128:]], axis=1)

    xs = x_sample.reshape(nd, D_MODEL)
    shift_in = state_shift[l].reshape(nd, D_SHIFT)
    p_s, vec_t, gb, qn_s, kvn_s, cols = _decode_prep(xs, nmw, w_in_pad, shift_in, mu_pad, prm, wl, bmask, tab_s, qw, kw)
    s_t = jnp.transpose(state_wkv[l], (1, 2, 3, 0))
    ck_t = jnp.swapaxes(cache_k_win[l].reshape(nd, WINDOW, D_KV), 1, 2)
    cv_t = jnp.swapaxes(cache_v_win[l].reshape(nd, WINDOW, D_KV), 1, 2)
    wkv_t, yr_s, ya_s, kc_t, vc_t = _decode_step(vec_t, gb, prm, bmask, s_t, qn_s, kvn_s, cols, ck_t, cv_t,
                                                 sinks[l][:, None])

    xp = x_prompt.reshape(bsz * t, D_MODEL)
    xs_p, plast, q_p, kv_p = _inproj_shift(xp, nmw, w_in_pad, mu_pad, t, 1024)
    yr_p, hbd = _rwkv_prompt(xs_p.reshape(bsz, t, D_SHIFT), prm, wl, tri, bmask)
    y_prompt, y_s, kwin_p, vwin_p = _attn_ffn(q_p.reshape(bsz, t, D_ATTN), kv_p.reshape(bsz, t, 2 * D_KV),
                                              tab_a, tab_b, qw, kw, sinks[l][None, :], bmask, _band_bias(),
                                              x_prompt, yr_p, xs, yr_s, ya_s, wo, nfw, wu, wd)
    hb = hbd.reshape(bsz, 2, 4, HEAD_DIM, 2, HEAD_DIM)
    wkv_prompt = jnp.stack([hb[:, :, j, :, j % 2, :] for j in range(4)], axis=2)
    wkv_prompt = wkv_prompt.reshape(bsz, H_RWKV, HEAD_DIM, HEAD_DIM)[None]
    shift_prompt = plast[None]
    k_win_prompt = jnp.swapaxes(kwin_p, 1, 2).reshape(bsz, WINDOW, N_KV_HEADS, HEAD_DIM)[None]
    v_win_prompt = jnp.swapaxes(vwin_p, 1, 2).reshape(bsz, WINDOW, N_KV_HEADS, HEAD_DIM)[None]

    y_sample = y_s.reshape(nd, 1, D_MODEL)
    wkv_sample = jnp.transpose(wkv_t, (3, 0, 1, 2))[None]
    shift_sample = p_s.reshape(nd, 1, D_SHIFT)[None]
    k_win_sample = jnp.swapaxes(kc_t, 1, 2).reshape(nd, WINDOW, N_KV_HEADS, HEAD_DIM)[None]
    v_win_sample = jnp.swapaxes(vc_t, 1, 2).reshape(nd, WINDOW, N_KV_HEADS, HEAD_DIM)[None]

    return (y_prompt, y_sample, wkv_prompt, shift_prompt, k_win_prompt, v_win_prompt,
            wkv_sample, shift_sample, k_win_sample, v_win_sample)
```

```python
import functools

import jax
import jax.numpy as jnp
import numpy as np
from jax import lax
from jax.experimental import pallas as pl
from jax.experimental.pallas import tpu as pltpu

F32 = jnp.float32
BF16 = jnp.bfloat16

D_MODEL = 1024
D_RWKV = 512
D_ATTN = 512
HEAD_DIM = 64
H_RWKV = 8
N_Q_HEADS = 8
N_KV_HEADS = 2
GQA_GROUP = 4
D_KV = 128
D_LORA = 160
D_LORA_PAD = 256
D_SHIFT = 3 * D_RWKV + D_LORA
D_IN = D_SHIFT + D_ATTN + 2 * D_KV
WINDOW = 128
ROPE_DIM = 16
ROPE_HALF = 8
ROPE_THETA = 500000.0
ATTN_SCALE = HEAD_DIM ** -0.5
D_FF = 4096
RMS_EPS = 1e-6
LNX_EPS = 64e-5
NEG_INF = -1e30
LOG2E = 1.4426950408889634
PAST_LEN = 16384

CHUNK = 64
QUAD = 4 * HEAD_DIM
V7X_VMEM_BYTES = 64 * 1024 * 1024
VMEM_LIMIT = V7X_VMEM_BYTES // 8 * 7


def _split2(x):
    hi = x.astype(BF16)
    lo = (x - hi.astype(F32)).astype(BF16)
    return hi, lo


def _head_sums(xs, ones_ref):
    n, w = xs[0].shape
    tile = min(w, QUAD)
    per = w // tile
    pieces = [x[:, j * tile:(j + 1) * tile] for x in xs for j in range(per)]
    stacked = jnp.concatenate(pieces, axis=0) if len(pieces) > 1 else pieces[0]
    ones = ones_ref[0:tile, 0:tile]
    out = jnp.dot(stacked.astype(BF16), ones, preferred_element_type=F32)
    res = []
    for i in range(len(xs)):
        cols = [out[(i * per + j) * n:(i * per + j + 1) * n] for j in range(per)]
        res.append(jnp.concatenate(cols, axis=1) if per > 1 else cols[0])
    return res


def _cumsum_rows(tri_bf16, x):
    hi, lo = _split2(x)
    return (jnp.dot(tri_bf16, hi, preferred_element_type=F32)
            + jnp.dot(tri_bf16, lo, preferred_element_type=F32))


def _mm(a, b):
    return jnp.dot(a.astype(BF16), b.astype(BF16), preferred_element_type=F32)


def _mm_nt(a, b):
    return lax.dot_general(a.astype(BF16), b.astype(BF16), (((1,), (1,)), ((), ())),
                           preferred_element_type=F32)


def _mm_tn(a, b):
    return lax.dot_general(a.astype(BF16), b.astype(BF16), (((0,), (0,)), ((), ())),
                           preferred_element_type=F32)


def _dot_nt_f32(a, b):
    return lax.dot_general(a, b, (((1,), (1,)), ((), ())), preferred_element_type=F32)


def _sigmoid(x):
    return 1.0 / (1.0 + jnp.exp(-x))


def _interleave(*gens):
    live = list(gens)
    while live:
        for g in list(live):
            try:
                next(g)
            except StopIteration:
                live.remove(g)


def _norm_project(x, nw_ref, wt_ref):
    ms = jnp.mean(x * x, axis=-1, keepdims=True)
    h = ((x * lax.rsqrt(ms + RMS_EPS)) * nw_ref[...]).astype(BF16)
    return _mm_nt(h, wt_ref[0:D_SHIFT, :]), _mm_nt(h, wt_ref[D_SHIFT:, :])


def _inproj_shift_kernel(tiles_per_seq, nside, x_ref, nw_ref, w_ref, mu_ref, *refs):
    side_in, (xs_ref, last_ref, q_ref, kv_ref) = refs[:nside], refs[nside:nside + 4]
    side_out, prev_ref = refs[nside + 4:2 * nside + 4], refs[2 * nside + 4]
    i = pl.program_id(0)
    for src, dst in zip(side_in, side_out):
        dst[...] = src[...].astype(BF16)

    @pl.when(i % tiles_per_seq == 0)
    def _():
        prev_ref[...] = jnp.zeros_like(prev_ref)

    p, qkv = _norm_project(x_ref[...], nw_ref, w_ref)
    tm = p.shape[0]
    row = lax.broadcasted_iota(jnp.int32, p.shape, 0)
    prev = jnp.where(row == 0, jnp.broadcast_to(prev_ref[0:1, :], p.shape), pltpu.roll(p, 1, 0))
    xs_ref[...] = p + (prev - p) * mu_ref[...]
    last = jnp.broadcast_to(p[tm - 1:tm, :], prev_ref.shape)
    prev_ref[...] = last
    last_ref[0] = p[tm - 1:tm, :]
    q_ref[...] = qkv[:, :D_ATTN]
    kv_ref[...] = qkv[:, D_ATTN:]


def _inproj_shift(x2d, norm_w, w_in_pad, mu_pad, seq_len, tm, side):
    m = x2d.shape[0]
    tiles_per_seq = seq_len // tm
    nsteps = m // tm
    side_specs = []
    for w in side:
        rows, rem = divmod(w.shape[0], nsteps)
        assert rem == 0 and rows % 16 == 0, "whole bf16 row tiles of every side weight per step"
        side_specs.append(pl.BlockSpec((rows, w.shape[1]), lambda i: (i, 0)))
    return pl.pallas_call(
        functools.partial(_inproj_shift_kernel, tiles_per_seq, len(side)),
        grid=(nsteps,),
        in_specs=[
            pl.BlockSpec((tm, D_MODEL), lambda i: (i, 0)),
            pl.BlockSpec((1, D_MODEL), lambda i: (0, 0)),
            pl.BlockSpec((D_IN, D_MODEL), lambda i: (0, 0)),
            pl.BlockSpec((1, D_SHIFT), lambda i: (0, 0)),
        ] + side_specs,
        out_specs=[
            pl.BlockSpec((tm, D_SHIFT), lambda i: (i, 0)),
            pl.BlockSpec((1, 1, D_SHIFT), lambda i: (i // tiles_per_seq, 0, 0)),
            pl.BlockSpec((tm, D_ATTN), lambda i: (i, 0)),
            pl.BlockSpec((tm, 2 * D_KV), lambda i: (i, 0)),
        ] + side_specs,
        out_shape=[
            jax.ShapeDtypeStruct((m, D_SHIFT), F32),
            jax.ShapeDtypeStruct((m // seq_len, 1, D_SHIFT), F32),
            jax.ShapeDtypeStruct((m, D_ATTN), F32),
            jax.ShapeDtypeStruct((m, 2 * D_KV), F32),
        ] + [jax.ShapeDtypeStruct(w.shape, BF16) for w in side],
        scratch_shapes=[pltpu.VMEM((8, D_SHIFT), F32)],
        compiler_params=pltpu.CompilerParams(
            dimension_semantics=("arbitrary",), vmem_limit_bytes=VMEM_LIMIT),
        name="inproj_shift",
    )(x2d, norm_w, w_in_pad, mu_pad, *side)


def _rwkv_features(xs, prm, wl_ref, ones_ref):
    r = xs[:, 0:D_RWKV]
    k = xs[:, D_RWKV:2 * D_RWKV]
    v = xs[:, 2 * D_RWKV:3 * D_RWKV]
    lora = xs[:, 3 * D_RWKV:]
    lora = jnp.concatenate([lora, jnp.zeros((lora.shape[0], wl_ref.shape[0] - D_LORA), F32)], axis=1)
    col = lax.broadcasted_iota(jnp.int32, lora.shape, 1)
    act = jnp.where(col < 32, jnp.tanh(lora), jnp.where(col < 64, lora, _sigmoid(lora)))
    up = jnp.dot(act.astype(BF16), wl_ref[...], preferred_element_type=F32)
    w0, a0, k_k, k_a, r_k = prm[0:1], prm[1:2], prm[2:3], prm[3:4], prm[4:5]
    logw = (-np.exp(-0.5)) * _sigmoid(w0 + up[:, 0:D_RWKV])
    asig = _sigmoid(a0 + up[:, D_RWKV:2 * D_RWKV])
    g = up[:, 2 * D_RWKV:]
    kk = k * k_k
    k_mod = k * (1.0 + (asig - 1.0) * k_a)
    ss, rk = _head_sums([kk * kk, r * k_mod * r_k], ones_ref)
    kk = kk / jnp.maximum(jnp.sqrt(ss), 1e-12)
    k = k_mod
    bonus = rk * v
    return r, logw, k, v, -kk, kk * asig, g, bonus


def _rwkv_finish(y, g, bonus, prm, ones_ref):
    ln_w, ln_b = prm[5:6], prm[6:7]
    mean = _head_sums([y], ones_ref)[0] * (1.0 / HEAD_DIM)
    d = y - mean
    var = _head_sums([d * d], ones_ref)[0] * (1.0 / HEAD_DIM)
    yn = d * lax.rsqrt(var + LNX_EPS) * ln_w + ln_b
    return (yn + bonus) * g


def _block_diag(x, bmask):
    return jnp.concatenate([x] * 4, axis=0) * bmask


def _chunk_prep(r, logw, k, v, a, b, tri_ref, bmask):
    n = len(r)
    ch = range(n)
    tri = tri_ref[...]
    cum = [_cumsum_rows(tri, logw[i]) for i in ch]
    yield
    e_in = [jnp.exp(cum[i]) for i in ch]
    e_ex = [jnp.exp(cum[i] - logw[i]) for i in ch]
    e_inv = [1.0 / e_in[i] for i in ch]
    e_last = [e_in[i][CHUNK - 1:CHUNK, :] for i in ch]
    rt = [(r[i] * e_in[i]).astype(BF16) for i in ch]
    at = [(a[i] * e_ex[i]).astype(BF16) for i in ch]
    kt = [(k[i] * e_inv[i]).astype(BF16) for i in ch]
    bt = [(b[i] * e_inv[i]).astype(BF16) for i in ch]
    vb = [v[i].astype(BF16) for i in ch]
    yield

    t_idx = lax.broadcasted_iota(jnp.int32, (CHUNK, QUAD), 0)
    s_idx = lax.broadcasted_iota(jnp.int32, (CHUNK, QUAD), 1) & (HEAD_DIM - 1)
    strict = s_idx < t_idx
    incl = s_idx <= t_idx

    gm = [_mm_nt(jnp.concatenate([at[i], rt[i]], axis=0),
                 jnp.concatenate([_block_diag(bt[i], bmask), _block_diag(kt[i], bmask)], axis=0))
          for i in ch]
    yield
    a_ab = [jnp.where(strict, gm[i][:CHUNK, :QUAD], 0.0) for i in ch]
    a_ak = [jnp.where(strict, gm[i][:CHUNK, QUAD:], 0.0) for i in ch]
    a_rb = [jnp.where(incl, gm[i][CHUNK:, :QUAD], 0.0) for i in ch]
    a_rk = [jnp.where(incl, gm[i][CHUNK:, QUAD:], 0.0) for i in ch]

    eye = jnp.where(s_idx == t_idx, 1.0, 0.0)
    pwb = [a_ab[i].astype(BF16) for i in ch]
    t_inv = [eye + a_ab[i] for i in ch]
    for it in range(6):
        rbd = [_block_diag(pwb[i], bmask) for i in ch]
        if it == 0:
            pwb = [_mm(pwb[i], rbd[i]).astype(BF16) for i in ch]
        elif it < 5:
            out = [_mm(jnp.concatenate([pwb[i], t_inv[i].astype(BF16)], axis=0), rbd[i]) for i in ch]
            pwb = [out[i][:CHUNK].astype(BF16) for i in ch]
            t_inv = [t_inv[i] + out[i][CHUNK:] for i in ch]
        else:
            t_inv = [t_inv[i] + _mm(t_inv[i], rbd[i]) for i in ch]
        yield

    vbd = [_block_diag(vb[i], bmask) for i in ch]
    xy0 = [_mm(jnp.concatenate([a_ak[i], a_rk[i]], axis=0), vbd[i]) for i in ch]
    return [dict(ar=jnp.concatenate([at[i], rt[i]], axis=0), x0=xy0[i][:CHUNK], y0=xy0[i][CHUNK:],
                 t_inv=t_inv[i].astype(BF16), a_rb=a_rb[i].astype(BF16), vb=vb[i],
                 bk=jnp.concatenate([bt[i], kt[i]], axis=0), e_last=e_last[i]) for i in ch]


def _chunk_step(pre, state, bmask, out):
    ch = range(len(pre))
    half = QUAD // 2
    zeros = jnp.zeros((half, half), BF16)
    sc = [state[i].astype(BF16) for i in ch]
    sb = [jnp.concatenate([jnp.concatenate([sc[i][:half], zeros], axis=1),
                           jnp.concatenate([zeros, sc[i][half:]], axis=1)], axis=0) for i in ch]
    xr = [_mm_nt(pre[i]["ar"], sb[i]) for i in ch]
    yield
    x = [xr[i][:CHUNK] + pre[i]["x0"] for i in ch]
    u = [_mm(pre[i]["t_inv"], _block_diag(x[i].astype(BF16), bmask)) for i in ch]
    yield
    ub = [u[i].astype(BF16) for i in ch]
    out["y"] = [xr[i][CHUNK:] + pre[i]["y0"] + _mm(pre[i]["a_rb"], _block_diag(ub[i], bmask)) for i in ch]
    upd = [_mm_tn(jnp.concatenate([ub[i], pre[i]["vb"]], axis=0), pre[i]["bk"]) for i in ch]
    yield
    bm = bmask[:half, :half].astype(F32)
    s_new = []
    for i in ch:
        e_last = pre[i]["e_last"]
        top = (state[i][:half] + upd[i][:half, :half] * bm) * e_last[:, :half]
        bot = (state[i][half:] + upd[i][half:, half:] * bm) * e_last[:, half:]
        s_new.append(jnp.concatenate([top, bot], axis=0))
    out["state"] = s_new


def _finish_stages(ys, g, bonus, prm, ones_ref, y_ref, j):
    nseq = len(ys) // 2
    ln_w, ln_b = prm[5:6], prm[6:7]
    y = jnp.concatenate([jnp.concatenate(ys[2 * s:2 * s + 2], axis=1) for s in range(nseq)], axis=0)
    mean = _head_sums([y], ones_ref)[0] * (1.0 / HEAD_DIM)
    yield
    d = y - mean
    var = _head_sums([d * d], ones_ref)[0] * (1.0 / HEAD_DIM)
    yield
    out = (d * lax.rsqrt(var + LNX_EPS) * ln_w + ln_b + bonus) * g
    for s in range(nseq):
        y_ref[s, j * CHUNK:(j + 1) * CHUNK, :] = out[s * CHUNK:(s + 1) * CHUNK]


def _rwkv_prompt_kernel(xs_ref, prm_ref, wl_ref, tri_ref, bmask_ref, y_ref, hout_ref, h_ref):
    c = pl.program_id(0)
    nseq, tstep, _ = xs_ref.shape
    nchunk = tstep // CHUNK

    @pl.when(c == 0)
    def _():
        h_ref[...] = jnp.zeros_like(h_ref)

    xs = jnp.concatenate([xs_ref[s] for s in range(nseq)], axis=0)
    prm = prm_ref[...]
    r, logw, k, v, a, b, g, bonus = _rwkv_features(xs, prm, wl_ref, bmask_ref)
    bmask = bmask_ref[...]
    lanes = [(s, q) for s in range(nseq) for q in range(2)]
    pre = {}

    def prep(chunks):
        chains = [(j, s, q) for j in chunks for s, q in lanes]
        cut = lambda x: [x[s * tstep + j * CHUNK:s * tstep + (j + 1) * CHUNK, q * QUAD:(q + 1) * QUAD]
                         for j, s, q in chains]
        res = yield from _chunk_prep(cut(r), cut(logw), cut(k), cut(v), cut(a), cut(b), tri_ref, bmask)
        for n, j in enumerate(chunks):
            pre[j] = res[n * len(lanes):(n + 1) * len(lanes)]

    nfirst = max(nchunk - 1, 1)
    _interleave(prep(range(nfirst)))
    later = prep(range(nfirst, nchunk))
    state = [h_ref[s, q] for s, q in lanes]
    crow = lambda x, j: jnp.concatenate([x[s * tstep + j * CHUNK:s * tstep + (j + 1) * CHUNK]
                                         for s in range(nseq)], axis=0)
    finish = iter(())
    for j in range(nchunk):
        res = {}
        for _ in _chunk_step(pre[j], state, bmask, res):
            next(later, None)
            next(finish, None)
        _interleave(finish)
        if j == nfirst - 1:
            _interleave(later)
        state = res["state"]
        finish = _finish_stages(res["y"], crow(g, j), crow(bonus, j), prm, bmask_ref, y_ref, j)
    _interleave(finish)
    for i, (s, q) in enumerate(lanes):
        h_ref[s, q] = state[i]

    @pl.when(c == pl.num_programs(0) - 1)
    def _():
        hout_ref[...] = h_ref[...]


RWKV_CHUNKS_PER_STEP = 4


def _rwkv_prompt(xs3d, prm, wl, tri, bmask):
    bsz, t, _ = xs3d.shape
    tstep = RWKV_CHUNKS_PER_STEP * CHUNK
    nc = t // tstep
    const = lambda shape: pl.BlockSpec(shape, lambda c: (0,) * len(shape))
    state_shape = (bsz, 2, QUAD, QUAD // 2)
    return pl.pallas_call(
        _rwkv_prompt_kernel,
        grid=(nc,),
        in_specs=[
            pl.BlockSpec((bsz, tstep, D_SHIFT), lambda c: (0, c, 0)),
            const((16, D_RWKV)),
            const((D_LORA_PAD, 3 * D_RWKV)),
            const((CHUNK, CHUNK)),
            const((QUAD, QUAD)),
        ],
        out_specs=[
            pl.BlockSpec((bsz, tstep, D_RWKV), lambda c: (0, c, 0)),
            const(state_shape),
        ],
        out_shape=[
            jax.ShapeDtypeStruct((bsz, t, D_RWKV), F32),
            jax.ShapeDtypeStruct(state_shape, F32),
        ],
        scratch_shapes=[pltpu.VMEM(state_shape, F32)],
        compiler_params=pltpu.CompilerParams(
            dimension_semantics=("arbitrary",), vmem_limit_bytes=VMEM_LIMIT),
        name="rwkv_prompt",
    )(xs3d, prm, wl, tri, bmask)


def _rope_lane_freq():
    pair = jnp.asarray((np.arange(2 * HEAD_DIM) % HEAD_DIM) % ROPE_HALF, F32)
    return jnp.power(ROPE_THETA, -pair * (2.0 / ROPE_DIM))[None, :]


def _rope_tables(cos, sin):
    dim = lax.broadcasted_iota(jnp.int32, cos.shape, 1) & (HEAD_DIM - 1)
    cos_t = jnp.where(dim < ROPE_DIM, cos, 1.0)
    sin_lo = jnp.where(dim < ROPE_HALF, -sin, 0.0)
    sin_hi = jnp.where((dim >= ROPE_HALF) & (dim < ROPE_DIM), sin, 0.0)
    return cos_t, sin_lo, sin_hi


def _qk_norm_rope(x, norm_w, cos_t, sin_lo, sin_hi, ones_ref):
    ms = _head_sums([x * x], ones_ref)[0] * (1.0 / HEAD_DIM)
    xn = x * lax.rsqrt(ms + RMS_EPS) * norm_w
    width = x.shape[1]
    fwd = pltpu.roll(xn, width - ROPE_HALF, 1)
    bwd = pltpu.roll(xn, ROPE_HALF, 1)
    return xn * cos_t + fwd * sin_lo + bwd * sin_hi


def _tile_lanes(x, reps):
    return jnp.concatenate([x] * reps, axis=1) if reps > 1 else x


def _attn_stages(q_ref, kv_ref, taba_ref, tabb_ref, qw_ref, kw_ref, sink_ref, ones_ref, bias_ref, first_bias,
                 kprev_ref, vprev_ref, out):
    nseq, tq, _ = q_ref.shape
    nblk = tq // WINDOW
    units = [(s, b) for s in range(nseq) for b in range(nblk)]
    blk = lambda b: slice(b * WINDOW, (b + 1) * WINDOW)
    tb = tabb_ref[...]
    cos_b, sin_b = tb[:, :128], tb[:, 128:]
    rope, rope4 = [], []
    for b in range(nblk):
        ta = taba_ref[b]
        cos_a, sin_a = ta[:, :128], ta[:, 128:]
        tabs = _rope_tables(cos_a * cos_b - sin_a * sin_b, sin_a * cos_b + cos_a * sin_b)
        rope.append(tabs)
        rope4.append([_tile_lanes(x, 4) for x in tabs])
    q = {(s, b): _qk_norm_rope(q_ref[s, blk(b), :], qw_ref[...], *rope4[b], ones_ref) * (ATTN_SCALE * LOG2E)
         for s, b in units}
    kv = {(s, b): kv_ref[s, blk(b), :] for s, b in units}
    k_cur = {u: _qk_norm_rope(kv[u][:, 0:D_KV], kw_ref[...], *rope[u[1]], ones_ref) for u in units}
    v_cur = {u: kv[u][:, D_KV:] for u in units}
    k_all = {(s, b): jnp.concatenate([kprev_ref[s] if b == 0 else k_cur[(s, b - 1)], k_cur[(s, b)]], axis=0)
             for s, b in units}
    v_all = {(s, b): jnp.concatenate([vprev_ref[s] if b == 0 else v_cur[(s, b - 1)], v_cur[(s, b)]], axis=0)
             for s, b in units}
    out["k_cur"] = [k_cur[(s, nblk - 1)] for s in range(nseq)]
    out["v_cur"] = [v_cur[(s, nblk - 1)] for s in range(nseq)]
    for s in range(nseq):
        kprev_ref[s] = out["k_cur"][s]
        vprev_ref[s] = out["v_cur"][s]
    yield

    nk = 2 * WINDOW
    bias = [bias_ref[first_bias] if b == 0 else bias_ref[1] for b in range(nblk)]
    sinks = sink_ref[...] * LOG2E
    low = lax.broadcasted_iota(jnp.int32, (nk, D_KV), 1) < HEAD_DIM
    lane_blk = [ones_ref[j * HEAD_DIM:j * HEAD_DIM + 1, :] for j in range(GQA_GROUP)]

    chains = [(u, g) for u in units for g in range(N_KV_HEADS)]
    ch = range(len(chains))
    k_rot = {u: pltpu.roll(k_all[u], HEAD_DIM, 1) for u in units}
    k2 = [jnp.where(low, k_all[u], k_rot[u]) if g == 0 else jnp.where(low, k_rot[u], k_all[u])
          for u, g in chains]
    k4 = [jnp.concatenate([k2[c], k2[c]], axis=1).astype(BF16) for c in ch]
    vb = {u: v_all[u].astype(BF16) for u in units}
    qg = [q[u][:, g * QUAD:(g + 1) * QUAD].astype(BF16) for u, g in chains]
    qstack = [jnp.concatenate([qg[c] * lane_blk[j] for j in range(GQA_GROUP)], axis=0) for c in ch]
    sink_row = [jnp.concatenate(
        [jnp.broadcast_to(sinks[:, g * GQA_GROUP + j:g * GQA_GROUP + j + 1], (1, WINDOW))
         for j in range(GQA_GROUP)], axis=1) for u, g in chains]
    yield
    sc = [_mm_nt(k4[c], qstack[c]) + bias[u[1]] for c, (u, g) in enumerate(chains)]
    yield
    m = [jnp.maximum(jnp.max(sc[c], axis=0, keepdims=True), sink_row[c]) for c in ch]
    e = [jnp.exp2(sc[c] - m[c]) for c in ch]
    yield
    denom = [jnp.sum(e[c], axis=0, keepdims=True) + jnp.exp2(sink_row[c] - m[c]) for c in ch]
    ot = {(u, g): _mm_tn(vb[u], e[c].astype(BF16))[g * HEAD_DIM:(g + 1) * HEAD_DIM, :] * (1.0 / denom[c])
          for c, (u, g) in enumerate(chains)}
    yield
    ya = []
    for s in range(nseq):
        blocks = []
        for b in range(nblk):
            yt = jnp.concatenate([ot[((s, b), g)][:, j * WINDOW:(j + 1) * WINDOW]
                                  for g in range(N_KV_HEADS) for j in range(GQA_GROUP)], axis=0)
            blocks.append(jnp.transpose(yt))
        ya.append(jnp.concatenate(blocks, axis=0) if nblk > 1 else blocks[0])
    out["ya"] = ya


def _ffn_stages(x, yr, ya, wo_ref, nw_ref, wu_ref, wd_ref, out, pieces=4):
    mix = jnp.concatenate([yr, ya], axis=1).astype(BF16)
    x1 = x + _mm(mix, wo_ref[...])
    yield
    ms = jnp.mean(x1 * x1, axis=-1, keepdims=True)
    hf = ((x1 * lax.rsqrt(ms + RMS_EPS)) * nw_ref[...]).astype(BF16)
    acc = x1
    step = D_FF // pieces
    for j in range(pieces):
        up = _mm(hf, wu_ref[:, j * step:(j + 1) * step])
        yield
        act = jnp.square(jnp.maximum(up, 0.0)).astype(BF16)
        acc = acc + _mm(act, wd_ref[j * step:(j + 1) * step, :])
        yield
    out["y"] = acc


def _attn_ffn_kernel(q_ref, kv_ref, taba_ref, tabb_ref, qw_ref, kw_ref, sink_ref, ones_ref, bias_ref,
                     x_ref, yr_ref, xd_ref, yrd_ref, yad_ref, wo_ref, nw_ref, wu_ref, wd_ref,
                     o_ref, od_ref, kwin_ref, vwin_ref, kprev_ref, vprev_ref, ya_ref):
    i = pl.program_id(0)
    nseq, tq, _ = q_ref.shape
    seqs = range(nseq)
    first = i == 0

    @pl.when(first)
    def _():
        kprev_ref[...] = jnp.zeros_like(kprev_ref)
        vprev_ref[...] = jnp.zeros_like(vprev_ref)
        ya_ref[...] = jnp.zeros_like(ya_ref)

    def rows(ref, dec_ref):
        tile = jnp.concatenate([ref[s] for s in seqs], axis=0)
        dec = dec_ref[...]
        return jnp.where(first, jnp.concatenate([dec] * (tile.shape[0] // dec.shape[0]), axis=0), tile)

    a_out, f_out = {}, {}
    _interleave(
        _ffn_stages(rows(x_ref, xd_ref), rows(yr_ref, yrd_ref), rows(ya_ref, yad_ref),
                    wo_ref, nw_ref, wu_ref, wd_ref, f_out),
        _attn_stages(q_ref, kv_ref, taba_ref, tabb_ref, qw_ref, kw_ref, sink_ref, ones_ref, bias_ref,
                     jnp.minimum(i, 1), kprev_ref, vprev_ref, a_out))
    for s in seqs:
        o_ref[s] = f_out["y"][s * tq:(s + 1) * tq]
        ya_ref[s] = a_out["ya"][s]

    @pl.when(first)
    def _():
        od_ref[...] = f_out["y"][:od_ref.shape[0]].reshape(od_ref.shape)

    @pl.when(i == pl.num_programs(0) - 2)
    def _():
        for s in seqs:
            kwin_ref[s] = jnp.transpose(a_out["k_cur"][s])
            vwin_ref[s] = jnp.transpose(a_out["v_cur"][s])


def _band_bias():
    ki = np.arange(2 * WINDOW)[:, None]
    qi = (np.arange(GQA_GROUP * WINDOW) % WINDOW + WINDOW)[None, :]
    dq = qi - ki
    band = (dq >= 0) & (dq < WINDOW)
    first = band & (ki >= WINDOW)
    return jnp.asarray(np.where(np.stack([first, band]), 0.0, NEG_INF), F32)


def _rope_block_tables(nb):
    freq = _rope_lane_freq()
    ang_a = (jnp.arange(nb, dtype=F32) * WINDOW)[:, None] * freq
    ang_b = jnp.arange(WINDOW, dtype=F32)[:, None] * freq
    tab_a = jnp.concatenate([jnp.cos(ang_a), jnp.sin(ang_a)], axis=1)
    tab_b = jnp.concatenate([jnp.cos(ang_b), jnp.sin(ang_b)], axis=1)
    return tab_a[:, None, :], tab_b


ATTN_BLOCKS_PER_STEP = 1


def _attn_ffn(q3d, kv3d, tab_a, tab_b, qw, kw, sinks, ones_bd, bias, x3d, yr3d, xd, yrd, yad, wo, nw, wu, wd):
    bsz, t, _ = q3d.shape
    nd = xd.shape[0]
    nblk = ATTN_BLOCKS_PER_STEP
    tq = nblk * WINDOW
    nt = t // tq
    const = lambda shape: pl.BlockSpec(shape, lambda i: (0,) * len(shape))
    single = lambda shape: pl.BlockSpec(shape, lambda i: (0,) * len(shape), pipeline_mode=pl.Buffered(1))
    cur = lambda i: jnp.minimum(i, nt - 1)
    prv = lambda i: jnp.maximum(i - 1, 0)
    return pl.pallas_call(
        _attn_ffn_kernel,
        grid=(nt + 1,),
        in_specs=[
            pl.BlockSpec((bsz, tq, D_ATTN), lambda i: (0, cur(i), 0)),
            pl.BlockSpec((bsz, tq, 2 * D_KV), lambda i: (0, cur(i), 0)),
            pl.BlockSpec((nblk, 1, 4 * HEAD_DIM), lambda i: (cur(i), 0, 0)),
            const((WINDOW, 4 * HEAD_DIM)),
            const((1, D_ATTN)),
            const((1, D_KV)),
            const((1, N_Q_HEADS)),
            const((QUAD, QUAD)),
            const((2, 2 * WINDOW, GQA_GROUP * WINDOW)),
            pl.BlockSpec((bsz, tq, D_MODEL), lambda i: (0, prv(i), 0)),
            pl.BlockSpec((bsz, tq, D_RWKV), lambda i: (0, prv(i), 0)),
            const((nd, D_MODEL)),
            const((nd, D_RWKV)),
            const((nd, D_ATTN)),
            single((D_MODEL, D_MODEL)),
            const((1, D_MODEL)),
            single((D_MODEL, D_FF)),
            single((D_FF, D_MODEL)),
        ],
        out_specs=[
            pl.BlockSpec((bsz, tq, D_MODEL), lambda i: (0, prv(i), 0)),
            const((nd, 1, D_MODEL)),
            const((bsz, WINDOW, D_KV)),
            const((bsz, WINDOW, D_KV)),
        ],
        out_shape=[
            jax.ShapeDtypeStruct((bsz, t, D_MODEL), F32),
            jax.ShapeDtypeStruct((nd, 1, D_MODEL), F32),
            jax.ShapeDtypeStruct((bsz, WINDOW, D_KV), F32),
            jax.ShapeDtypeStruct((bsz, WINDOW, D_KV), F32),
        ],
        scratch_shapes=[
            pltpu.VMEM((bsz, WINDOW, D_KV), F32),
            pltpu.VMEM((bsz, WINDOW, D_KV), F32),
            pltpu.VMEM((bsz, tq, D_ATTN), F32),
        ],
        compiler_params=pltpu.CompilerParams(
            dimension_semantics=("arbitrary",), vmem_limit_bytes=VMEM_LIMIT),
        name="attn_ffn",
    )(q3d, kv3d, tab_a, tab_b, qw, kw, sinks, ones_bd, bias, x3d, yr3d, xd, yrd, yad, wo, nw, wu, wd)


DEC_TILE = 16


def _decode_prep_kernel(x_ref, nw_ref, w_ref, sh_ref, mu_ref, prm_ref, wl_ref, ones_ref, tab_ref,
                        qw_ref, kw_ref, p_ref, vec_ref, vgb_ref, qn_ref, kvn_ref, col_ref):
    p, qkv = _norm_project(x_ref[...], nw_ref, w_ref)
    p_ref[...] = p
    xs = p + (sh_ref[...] - p) * mu_ref[...]
    r, logw, k, v, a, b, g, bonus = _rwkv_features(xs, prm_ref[...], wl_ref, ones_ref)
    for i, x in enumerate((a, b, k, jnp.exp(logw), r, v)):
        vec_ref[i] = jnp.transpose(x)
    vgb_ref[0] = g
    vgb_ref[1] = bonus
    n = p.shape[0]
    tab = jnp.broadcast_to(tab_ref[...], (n, 4 * HEAD_DIM))
    cos_t, sin_lo, sin_hi = _rope_tables(tab[:, :128], tab[:, 128:])
    qn_ref[...] = _qk_norm_rope(qkv[:, :D_ATTN], qw_ref[...], _tile_lanes(cos_t, 4), _tile_lanes(sin_lo, 4),
                                _tile_lanes(sin_hi, 4), ones_ref)
    k_new = _qk_norm_rope(qkv[:, D_ATTN:D_ATTN + D_KV], kw_ref[...], cos_t, sin_lo, sin_hi, ones_ref)
    kvn = jnp.concatenate([k_new, qkv[:, D_ATTN + D_KV:]], axis=1)
    kvn_ref[...] = kvn
    kvn_t = jnp.transpose(kvn)
    for i in range(n // DEC_TILE):
        col_ref[i] = kvn_t[:, i * DEC_TILE:(i + 1) * DEC_TILE]


def _decode_prep(x2d, norm_w, w_in_t, shift, mu_pad, prm, wl, bmask, tab, qw, kw):
    n = x2d.shape[0]
    full = lambda shape: pl.BlockSpec(shape, lambda i: (0,) * len(shape))
    return pl.pallas_call(
        _decode_prep_kernel,
        grid=(1,),
        in_specs=[full((n, D_MODEL)), full((1, D_MODEL)), full((D_IN, D_MODEL)), full((n, D_SHIFT)),
                  full((1, D_SHIFT)), full((16, D_RWKV)), full((D_LORA_PAD, 3 * D_RWKV)),
                  full((QUAD, QUAD)), full((1, 4 * HEAD_DIM)), full((1, D_ATTN)), full((1, D_KV))],
        out_specs=[full((n, D_SHIFT)), full((6, D_RWKV, n)), full((2, n, D_RWKV)), full((n, D_ATTN)),
                   full((n, 2 * D_KV)), full((n // DEC_TILE, 2 * D_KV, DEC_TILE))],
        out_shape=[
            jax.ShapeDtypeStruct((n, D_SHIFT), F32),
            jax.ShapeDtypeStruct((6, D_RWKV, n), F32),
            jax.ShapeDtypeStruct((2, n, D_RWKV), F32),
            jax.ShapeDtypeStruct((n, D_ATTN), F32),
            jax.ShapeDtypeStruct((n, 2 * D_KV), F32),
            jax.ShapeDtypeStruct((n // DEC_TILE, 2 * D_KV, DEC_TILE), F32),
        ],
        compiler_params=pltpu.CompilerParams(
            dimension_semantics=("arbitrary",), vmem_limit_bytes=VMEM_LIMIT),
        name="decode_prep",
    )(x2d, norm_w, w_in_t, shift, mu_pad, prm, wl, bmask, tab, qw, kw)


def _decode_state_stages(vec_ref, s_ref, sout_ref, yt_ref, h):
    a_t, b_t, k_t, w_t, r_t = (vec_ref[i] for i in range(5))
    for i in range(HEAD_DIM):
        s = s_ref[0, i]
        sa = jnp.sum(s * a_t, axis=0, keepdims=True)
        v_i = vec_ref[5, i:i + 1, :]
        s_new = s * w_t + sa * b_t + v_i * k_t
        sout_ref[0, i] = s_new
        yt_ref[pl.ds(h * HEAD_DIM + i, 1), :] = jnp.sum(s_new * r_t, axis=0, keepdims=True)
        if i % 8 == 7:
            yield


def _decode_attn_stages(qn_ref, kvn_ref, col_ref, ck_ref, cv_ref, sink_ref, ya_ref, kout_ref, vout_ref):
    nh = N_Q_HEADS
    seqs = range(DEC_TILE)
    hrow = lax.broadcasted_iota(jnp.int32, (nh, D_ATTN), 0)
    hlane = lax.broadcasted_iota(jnp.int32, (nh, D_ATTN), 1) // HEAD_DIM
    dmask = hrow == hlane
    grow = lax.broadcasted_iota(jnp.int32, (nh, D_KV), 0) // GQA_GROUP
    glane = lax.broadcasted_iota(jnp.int32, (nh, D_KV), 1) // HEAD_DIM
    low = glane == 0
    key_idx = lax.broadcasted_iota(jnp.int32, (nh, WINDOW), 1)
    last = lax.broadcasted_iota(jnp.int32, (D_KV, WINDOW), 1) == WINDOW - 1
    sink = sink_ref[...]
    kvn = kvn_ref[...]
    col = col_ref[0]
    k_new = [kvn[j:j + 1, 0:D_KV] for j in seqs]
    v_new = [kvn[j:j + 1, D_KV:] for j in seqs]
    ck = [ck_ref[j] for j in seqs]
    cv = [cv_ref[j] for j in seqs]
    for j in seqs:
        kout_ref[j] = jnp.where(last, col[0:D_KV, j:j + 1], pltpu.roll(ck[j], WINDOW - 1, 1))
        vout_ref[j] = jnp.where(last, col[D_KV:, j:j + 1], pltpu.roll(cv[j], WINDOW - 1, 1))
    yield
    qn = qn_ref[...]
    swap = (grow != (lax.broadcasted_iota(jnp.int32, (nh, D_KV), 0) & 1))
    qp = []
    for j in seqs:
        own = jnp.where(dmask, jnp.broadcast_to(qn[j:j + 1, :], (nh, D_ATTN)), 0.0)
        fold = own[:, 0:D_KV] + own[:, D_KV:2 * D_KV] + own[:, 2 * D_KV:3 * D_KV] + own[:, 3 * D_KV:]
        qp.append(jnp.where(swap, pltpu.roll(fold, HEAD_DIM, 1), fold))
    s_c = [jnp.where(key_idx >= 1, jnp.dot(qp[j], ck[j], preferred_element_type=F32) * ATTN_SCALE, NEG_INF)
           for j in seqs]
    s_n = [jnp.sum(qp[j] * k_new[j], axis=-1, keepdims=True) * ATTN_SCALE for j in seqs]
    yield
    m = [jnp.maximum(jnp.maximum(jnp.max(s_c[j], axis=-1, keepdims=True), s_n[j]), sink) for j in seqs]
    yield
    e_c = [jnp.exp(s_c[j] - m[j]) for j in seqs]
    e_n = [jnp.exp(s_n[j] - m[j]) for j in seqs]
    denom = [jnp.sum(e_c[j], axis=-1, keepdims=True) + e_n[j] + jnp.exp(sink - m[j]) for j in seqs]
    yield
    o = [(_dot_nt_f32(e_c[j], cv[j]) + e_n[j] * v_new[j]) / denom[j]
         for j in seqs]
    yield
    out_rows = []
    for j in seqs:
        rot = pltpu.roll(o[j], HEAD_DIM, 1)
        g0 = jnp.where(low, o[j], rot)
        g1 = jnp.where(low, rot, o[j])
        wide = jnp.concatenate([g0, g0, g1, g1], axis=1)
        out_rows.append(jnp.sum(jnp.where(dmask, wide, 0.0), axis=0, keepdims=True))
    ya_ref[...] = jnp.concatenate(out_rows, axis=0)


def _decode_step_kernel(vec_ref, gb_ref, prm_ref, ones_ref, s_ref, qn_ref, kvn_ref, col_ref, ck_ref, cv_ref,
                        sink_ref, sout_ref, yr_ref, ya_ref, kout_ref, vout_ref, yt_ref):
    i = pl.program_id(0)
    _interleave(
        _decode_attn_stages(qn_ref, kvn_ref, col_ref, ck_ref, cv_ref, sink_ref, ya_ref, kout_ref, vout_ref),
        _decode_state_stages(vec_ref, s_ref, sout_ref, yt_ref, i))

    @pl.when(i == pl.num_programs(0) - 1)
    def _():
        y = jnp.transpose(yt_ref[...])
        yr_ref[...] = _rwkv_finish(y, gb_ref[0], gb_ref[1], prm_ref[...], ones_ref)


def _decode_step(vec_t, gb, prm, bmask, s_t, qn, kvn, cols, ck_t, cv_t, sinks_col):
    n = s_t.shape[-1]
    bt = DEC_TILE
    assert n // bt == H_RWKV, "one sequence tile per RWKV head"
    const = lambda shape: pl.BlockSpec(shape, lambda i: (0,) * len(shape))
    return pl.pallas_call(
        _decode_step_kernel,
        grid=(H_RWKV,),
        in_specs=[
            pl.BlockSpec((6, HEAD_DIM, n), lambda i: (0, i, 0)),
            const((2, n, D_RWKV)),
            const((16, D_RWKV)),
            const((QUAD, QUAD)),
            pl.BlockSpec((1, HEAD_DIM, HEAD_DIM, n), lambda i: (i, 0, 0, 0)),
            pl.BlockSpec((bt, D_ATTN), lambda i: (i, 0)),
            pl.BlockSpec((bt, 2 * D_KV), lambda i: (i, 0)),
            pl.BlockSpec((1, 2 * D_KV, bt), lambda i: (i, 0, 0)),
            pl.BlockSpec((bt, D_KV, WINDOW), lambda i: (i, 0, 0)),
            pl.BlockSpec((bt, D_KV, WINDOW), lambda i: (i, 0, 0)),
            const((N_Q_HEADS, 1)),
        ],
        out_specs=[
            pl.BlockSpec((1, HEAD_DIM, HEAD_DIM, n), lambda i: (i, 0, 0, 0)),
            const((n, D_RWKV)),
            pl.BlockSpec((bt, D_ATTN), lambda i: (i, 0)),
            pl.BlockSpec((bt, D_KV, WINDOW), lambda i: (i, 0, 0)),
            pl.BlockSpec((bt, D_KV, WINDOW), lambda i: (i, 0, 0)),
        ],
        out_shape=[
            jax.ShapeDtypeStruct((H_RWKV, HEAD_DIM, HEAD_DIM, n), F32),
            jax.ShapeDtypeStruct((n, D_RWKV), F32),
            jax.ShapeDtypeStruct((n, D_ATTN), F32),
            jax.ShapeDtypeStruct((n, D_KV, WINDOW), F32),
            jax.ShapeDtypeStruct((n, D_KV, WINDOW), F32),
        ],
        scratch_shapes=[pltpu.VMEM((D_RWKV, n), F32)],
        compiler_params=pltpu.CompilerParams(
            dimension_semantics=("arbitrary",), vmem_limit_bytes=VMEM_LIMIT),
        name="decode_step",
    )(vec_t, gb, prm, bmask, s_t, qn, kvn, cols, ck_t, cv_t, sinks_col)


def kernel(x_prompt, x_sample, state_wkv, state_shift, cache_k_win, cache_v_win, norm_mix_w, w_in, mu_shift, w0, w_decay_up, a0, w_a_up, w_g_up, k_k, k_a, r_k, ln_x_w, ln_x_b, q_norm_w, k_norm_w, sinks, w_out, norm_ffn_w, w_ffn_up, w_ffn_down):
    bsz, t, _ = x_prompt.shape
    nd = x_sample.shape[0]
    l = 0

    w_in_pad = jnp.swapaxes(w_in[l], 0, 1)
    mu_pad = mu_shift[l][None, :]
    wl = jnp.zeros((D_LORA_PAD, 3 * D_RWKV), F32)
    wl = wl.at[0:32, 0:D_RWKV].set(w_decay_up[l])
    wl = wl.at[32:64, D_RWKV:2 * D_RWKV].set(w_a_up[l])
    wl = wl.at[64:160, 2 * D_RWKV:].set(w_g_up[l])
    wl = wl.astype(BF16)
    prm = jnp.zeros((16, D_RWKV), F32)
    prm = prm.at[0].set(w0[l]).at[1].set(a0[l]).at[2].set(k_k[l]).at[3].set(k_a[l])
    prm = prm.at[4].set(r_k[l].reshape(-1)).at[5].set(ln_x_w[l]).at[6].set(ln_x_b[l])
    hid = np.arange(QUAD) // HEAD_DIM
    bmask = jnp.asarray(hid[:, None] == hid[None, :], BF16)
    tri = jnp.asarray(np.tril(np.ones((CHUNK, CHUNK))), BF16)
    qw = jnp.tile(q_norm_w[l][None, :], (1, N_Q_HEADS))
    kw = jnp.tile(k_norm_w[l][None, :], (1, N_KV_HEADS))
    nmw = norm_mix_w[l][None, :]
    nfw = norm_ffn_w[l][None, :]
    tab_a, tab_b = _rope_block_tables(max(t, PAST_LEN + 1) // WINDOW + 1)
    ta, tb = tab_a[PAST_LEN // WINDOW], tab_b[PAST_LEN % WINDOW][None, :]
    tab_s = jnp.concatenate([ta[:, :128] * tb[:, :128] - ta[:, 128:] * tb[:, 128:],
                             ta[:, 128:] * tb[:, :128] + ta[:, :128] * tb[:, 128:]], axis=1)

    xs = x_sample.reshape(nd, D_MODEL)
    shift_in = state_shift[l].reshape(nd, D_SHIFT)
    p_s, vec_t, gb, qn_s, kvn_s, cols = _decode_prep(xs, nmw, w_in_pad, shift_in, mu_pad, prm, wl, bmask, tab_s, qw, kw)
    s_t = jnp.transpose(state_wkv[l], (1, 2, 3, 0))
    ck_t = jnp.swapaxes(cache_k_win[l].reshape(nd, WINDOW, D_KV), 1, 2)
    cv_t = jnp.swapaxes(cache_v_win[l].reshape(nd, WINDOW, D_KV), 1, 2)
    wkv_t, yr_s, ya_s, kc_t, vc_t = _decode_step(vec_t, gb, prm, bmask, s_t, qn_s, kvn_s, cols, ck_t, cv_t,
                                                 sinks[l][:, None])

    xp = x_prompt.reshape(bsz * t, D_MODEL)
    xs_p, plast, q_p, kv_p, wo, wu, wd = _inproj_shift(xp, nmw, w_in_pad, mu_pad, t, 1024,
                                                       (w_out[l], w_ffn_up[l], w_ffn_down[l]))
    yr_p, hbd = _rwkv_prompt(xs_p.reshape(bsz, t, D_SHIFT), prm, wl, tri, bmask)
    y_prompt, y_s, kwin_p, vwin_p = _attn_ffn(q_p.reshape(bsz, t, D_ATTN), kv_p.reshape(bsz, t, 2 * D_KV),
                                              tab_a, tab_b, qw, kw, sinks[l][None, :], bmask, _band_bias(),
                                              x_prompt, yr_p, xs, yr_s, ya_s, wo, nfw, wu, wd)
    hb = hbd.reshape(bsz, 2, 4, HEAD_DIM, 2, HEAD_DIM)
    wkv_prompt = jnp.stack([hb[:, :, j, :, j % 2, :] for j in range(4)], axis=2)
    wkv_prompt = wkv_prompt.reshape(bsz, H_RWKV, HEAD_DIM, HEAD_DIM)[None]
    shift_prompt = plast[None]
    k_win_prompt = jnp.swapaxes(kwin_p, 1, 2).reshape(bsz, WINDOW, N_KV_HEADS, HEAD_DIM)[None]
    v_win_prompt = jnp.swapaxes(vwin_p, 1, 2).reshape(bsz, WINDOW, N_KV_HEADS, HEAD_DIM)[None]

    y_sample = y_s.reshape(nd, 1, D_MODEL)
    wkv_sample = jnp.transpose(wkv_t, (3, 0, 1, 2))[None]
    shift_sample = p_s.reshape(nd, 1, D_SHIFT)[None]
    k_win_sample = jnp.swapaxes(kc_t, 1, 2).reshape(nd, WINDOW, N_KV_HEADS, HEAD_DIM)[None]
    v_win_sample = jnp.swapaxes(vc_t, 1, 2).reshape(nd, WINDOW, N_KV_HEADS, HEAD_DIM)[None]

    return (y_prompt, y_sample, wkv_prompt, shift_prompt, k_win_prompt, v_win_prompt,
            wkv_sample, shift_sample, k_win_sample, v_win_sample)
```

```python
import functools

import jax
import jax.numpy as jnp
import numpy as np
from jax import lax
from jax.experimental import pallas as pl
from jax.experimental.pallas import tpu as pltpu

F32 = jnp.float32
BF16 = jnp.bfloat16

D_MODEL = 1024
D_RWKV = 512
D_ATTN = 512
HEAD_DIM = 64
H_RWKV = 8
N_Q_HEADS = 8
N_KV_HEADS = 2
GQA_GROUP = 4
D_KV = 128
D_LORA = 160
D_LORA_PAD = 256
D_SHIFT = 3 * D_RWKV + D_LORA
D_IN = D_SHIFT + D_ATTN + 2 * D_KV
WINDOW = 128
ROPE_DIM = 16
ROPE_HALF = 8
ROPE_THETA = 500000.0
ATTN_SCALE = HEAD_DIM ** -0.5
D_FF = 4096
RMS_EPS = 1e-6
LNX_EPS = 64e-5
NEG_INF = -1e30
LOG2E = 1.4426950408889634
PAST_LEN = 16384

CHUNK = 64
QUAD = 4 * HEAD_DIM
V7X_VMEM_BYTES = 64 * 1024 * 1024
VMEM_LIMIT = V7X_VMEM_BYTES // 8 * 7


def _split2(x):
    hi = x.astype(BF16)
    lo = (x - hi.astype(F32)).astype(BF16)
    return hi, lo


def _head_sums(xs, ones_ref):
    n, w = xs[0].shape
    tile = min(w, QUAD)
    per = w // tile
    pieces = [x[:, j * tile:(j + 1) * tile] for x in xs for j in range(per)]
    stacked = jnp.concatenate(pieces, axis=0) if len(pieces) > 1 else pieces[0]
    ones = ones_ref[0:tile, 0:tile]
    out = jnp.dot(stacked.astype(BF16), ones, preferred_element_type=F32)
    res = []
    for i in range(len(xs)):
        cols = [out[(i * per + j) * n:(i * per + j + 1) * n] for j in range(per)]
        res.append(jnp.concatenate(cols, axis=1) if per > 1 else cols[0])
    return res


def _cumsum_rows(tri_bf16, x):
    hi, lo = _split2(x)
    return (jnp.dot(tri_bf16, hi, preferred_element_type=F32)
            + jnp.dot(tri_bf16, lo, preferred_element_type=F32))


def _mm(a, b):
    return jnp.dot(a.astype(BF16), b.astype(BF16), preferred_element_type=F32)


def _mm_nt(a, b):
    return lax.dot_general(a.astype(BF16), b.astype(BF16), (((1,), (1,)), ((), ())),
                           preferred_element_type=F32)


def _mm_tn(a, b):
    return lax.dot_general(a.astype(BF16), b.astype(BF16), (((0,), (0,)), ((), ())),
                           preferred_element_type=F32)


def _dot_nt_f32(a, b):
    return lax.dot_general(a, b, (((1,), (1,)), ((), ())), preferred_element_type=F32)


def _sigmoid(x):
    return 1.0 / (1.0 + jnp.exp(-x))


def _interleave(*gens):
    live = list(gens)
    while live:
        for g in list(live):
            try:
                next(g)
            except StopIteration:
                live.remove(g)


def _norm_project(x, nw_ref, wt_ref):
    ms = jnp.mean(x * x, axis=-1, keepdims=True)
    h = ((x * lax.rsqrt(ms + RMS_EPS)) * nw_ref[...]).astype(BF16)
    return _mm_nt(h, wt_ref[0:D_SHIFT, :]), _mm_nt(h, wt_ref[D_SHIFT:, :])


def _inproj_shift_kernel(tiles_per_seq, x_ref, nw_ref, w_ref, mu_ref, xs_ref, last_ref, q_ref, kv_ref, prev_ref):
    i = pl.program_id(0)

    @pl.when(i % tiles_per_seq == 0)
    def _():
        prev_ref[...] = jnp.zeros_like(prev_ref)

    p, qkv = _norm_project(x_ref[...], nw_ref, w_ref)
    tm = p.shape[0]
    row = lax.broadcasted_iota(jnp.int32, p.shape, 0)
    prev = jnp.where(row == 0, jnp.broadcast_to(prev_ref[0:1, :], p.shape), pltpu.roll(p, 1, 0))
    xs_ref[...] = p + (prev - p) * mu_ref[...]
    last = jnp.broadcast_to(p[tm - 1:tm, :], prev_ref.shape)
    prev_ref[...] = last
    last_ref[0] = p[tm - 1:tm, :]
    q_ref[...] = qkv[:, :D_ATTN]
    kv_ref[...] = qkv[:, D_ATTN:]


def _inproj_shift(x2d, norm_w, w_in_pad, mu_pad, seq_len, tm):
    m = x2d.shape[0]
    tiles_per_seq = seq_len // tm
    return pl.pallas_call(
        functools.partial(_inproj_shift_kernel, tiles_per_seq),
        grid=(m // tm,),
        in_specs=[
            pl.BlockSpec((tm, D_MODEL), lambda i: (i, 0)),
            pl.BlockSpec((1, D_MODEL), lambda i: (0, 0)),
            pl.BlockSpec((D_IN, D_MODEL), lambda i: (0, 0)),
            pl.BlockSpec((1, D_SHIFT), lambda i: (0, 0)),
        ],
        out_specs=[
            pl.BlockSpec((tm, D_SHIFT), lambda i: (i, 0)),
            pl.BlockSpec((1, 1, D_SHIFT), lambda i: (i // tiles_per_seq, 0, 0)),
            pl.BlockSpec((tm, D_ATTN), lambda i: (i, 0)),
            pl.BlockSpec((tm, 2 * D_KV), lambda i: (i, 0)),
        ],
        out_shape=[
            jax.ShapeDtypeStruct((m, D_SHIFT), F32),
            jax.ShapeDtypeStruct((m // seq_len, 1, D_SHIFT), F32),
            jax.ShapeDtypeStruct((m, D_ATTN), F32),
            jax.ShapeDtypeStruct((m, 2 * D_KV), F32),
        ],
        scratch_shapes=[pltpu.VMEM((8, D_SHIFT), F32)],
        compiler_params=pltpu.CompilerParams(
            dimension_semantics=("arbitrary",), vmem_limit_bytes=VMEM_LIMIT),
        name="inproj_shift",
    )(x2d, norm_w, w_in_pad, mu_pad)


def _rwkv_features(xs, prm, wl_ref, ones_ref):
    r = xs[:, 0:D_RWKV]
    k = xs[:, D_RWKV:2 * D_RWKV]
    v = xs[:, 2 * D_RWKV:3 * D_RWKV]
    lora = xs[:, 3 * D_RWKV:]
    lora = jnp.concatenate([lora, jnp.zeros((lora.shape[0], wl_ref.shape[0] - D_LORA), F32)], axis=1)
    col = lax.broadcasted_iota(jnp.int32, lora.shape, 1)
    act = jnp.where(col < 32, jnp.tanh(lora), jnp.where(col < 64, lora, _sigmoid(lora)))
    up = jnp.dot(act.astype(BF16), wl_ref[...], preferred_element_type=F32)
    w0, a0, k_k, k_a, r_k = prm[0:1], prm[1:2], prm[2:3], prm[3:4], prm[4:5]
    logw = (-np.exp(-0.5)) * _sigmoid(w0 + up[:, 0:D_RWKV])
    asig = _sigmoid(a0 + up[:, D_RWKV:2 * D_RWKV])
    g = up[:, 2 * D_RWKV:]
    kk = k * k_k
    k_mod = k * (1.0 + (asig - 1.0) * k_a)
    ss, rk = _head_sums([kk * kk, r * k_mod * r_k], ones_ref)
    kk = kk / jnp.maximum(jnp.sqrt(ss), 1e-12)
    k = k_mod
    bonus = rk * v
    return r, logw, k, v, -kk, kk * asig, g, bonus


def _rwkv_finish(y, g, bonus, prm, ones_ref):
    ln_w, ln_b = prm[5:6], prm[6:7]
    mean = _head_sums([y], ones_ref)[0] * (1.0 / HEAD_DIM)
    d = y - mean
    var = _head_sums([d * d], ones_ref)[0] * (1.0 / HEAD_DIM)
    yn = d * lax.rsqrt(var + LNX_EPS) * ln_w + ln_b
    return (yn + bonus) * g


def _block_diag(x, bmask):
    return jnp.concatenate([x] * 4, axis=0) * bmask


def _chunk_prep(r, logw, k, v, a, b, tri_ref, bmask):
    n = len(r)
    ch = range(n)
    tri = tri_ref[...]
    cum = [_cumsum_rows(tri, logw[i]) for i in ch]
    yield
    e_in = [jnp.exp(cum[i]) for i in ch]
    e_ex = [jnp.exp(cum[i] - logw[i]) for i in ch]
    e_inv = [1.0 / e_in[i] for i in ch]
    e_last = [e_in[i][CHUNK - 1:CHUNK, :] for i in ch]
    rt = [(r[i] * e_in[i]).astype(BF16) for i in ch]
    at = [(a[i] * e_ex[i]).astype(BF16) for i in ch]
    kt = [(k[i] * e_inv[i]).astype(BF16) for i in ch]
    bt = [(b[i] * e_inv[i]).astype(BF16) for i in ch]
    vb = [v[i].astype(BF16) for i in ch]
    yield

    t_idx = lax.broadcasted_iota(jnp.int32, (CHUNK, QUAD), 0)
    s_idx = lax.broadcasted_iota(jnp.int32, (CHUNK, QUAD), 1) & (HEAD_DIM - 1)
    strict = s_idx < t_idx
    incl = s_idx <= t_idx

    gm = [_mm_nt(jnp.concatenate([at[i], rt[i]], axis=0),
                 jnp.concatenate([_block_diag(bt[i], bmask), _block_diag(kt[i], bmask)], axis=0))
          for i in ch]
    yield
    a_ab = [jnp.where(strict, gm[i][:CHUNK, :QUAD], 0.0) for i in ch]
    a_ak = [jnp.where(strict, gm[i][:CHUNK, QUAD:], 0.0) for i in ch]
    a_rb = [jnp.where(incl, gm[i][CHUNK:, :QUAD], 0.0) for i in ch]
    a_rk = [jnp.where(incl, gm[i][CHUNK:, QUAD:], 0.0) for i in ch]

    eye = jnp.where(s_idx == t_idx, 1.0, 0.0)
    pwb = [a_ab[i].astype(BF16) for i in ch]
    t_inv = [eye + a_ab[i] for i in ch]
    for it in range(6):
        rbd = [_block_diag(pwb[i], bmask) for i in ch]
        if it == 0:
            pwb = [_mm(pwb[i], rbd[i]).astype(BF16) for i in ch]
        elif it < 5:
            out = [_mm(jnp.concatenate([pwb[i], t_inv[i].astype(BF16)], axis=0), rbd[i]) for i in ch]
            pwb = [out[i][:CHUNK].astype(BF16) for i in ch]
            t_inv = [t_inv[i] + out[i][CHUNK:] for i in ch]
        else:
            t_inv = [t_inv[i] + _mm(t_inv[i], rbd[i]) for i in ch]
        yield

    vbd = [_block_diag(vb[i], bmask) for i in ch]
    xy0 = [_mm(jnp.concatenate([a_ak[i], a_rk[i]], axis=0), vbd[i]) for i in ch]
    return [dict(ar=jnp.concatenate([at[i], rt[i]], axis=0), x0=xy0[i][:CHUNK], y0=xy0[i][CHUNK:],
                 t_inv=t_inv[i].astype(BF16), a_rb=a_rb[i].astype(BF16), vb=vb[i],
                 bk=jnp.concatenate([bt[i], kt[i]], axis=0), e_last=e_last[i]) for i in ch]


def _chunk_step(pre, state, bmask, out):
    ch = range(len(pre))
    half = QUAD // 2
    zeros = jnp.zeros((half, half), BF16)
    sc = [state[i].astype(BF16) for i in ch]
    sb = [jnp.concatenate([jnp.concatenate([sc[i][:half], zeros], axis=1),
                           jnp.concatenate([zeros, sc[i][half:]], axis=1)], axis=0) for i in ch]
    xr = [_mm_nt(pre[i]["ar"], sb[i]) for i in ch]
    yield
    x = [xr[i][:CHUNK] + pre[i]["x0"] for i in ch]
    u = [_mm(pre[i]["t_inv"], _block_diag(x[i].astype(BF16), bmask)) for i in ch]
    yield
    ub = [u[i].astype(BF16) for i in ch]
    out["y"] = [xr[i][CHUNK:] + pre[i]["y0"] + _mm(pre[i]["a_rb"], _block_diag(ub[i], bmask)) for i in ch]
    upd = [_mm_tn(jnp.concatenate([ub[i], pre[i]["vb"]], axis=0), pre[i]["bk"]) for i in ch]
    yield
    bm = bmask[:half, :half].astype(F32)
    s_new = []
    for i in ch:
        e_last = pre[i]["e_last"]
        top = (state[i][:half] + upd[i][:half, :half] * bm) * e_last[:, :half]
        bot = (state[i][half:] + upd[i][half:, half:] * bm) * e_last[:, half:]
        s_new.append(jnp.concatenate([top, bot], axis=0))
    out["state"] = s_new


def _finish_stages(ys, g, bonus, prm, ones_ref, y_ref, j):
    nseq = len(ys) // 2
    ln_w, ln_b = prm[5:6], prm[6:7]
    y = jnp.concatenate([jnp.concatenate(ys[2 * s:2 * s + 2], axis=1) for s in range(nseq)], axis=0)
    mean = _head_sums([y], ones_ref)[0] * (1.0 / HEAD_DIM)
    yield
    d = y - mean
    var = _head_sums([d * d], ones_ref)[0] * (1.0 / HEAD_DIM)
    yield
    out = (d * lax.rsqrt(var + LNX_EPS) * ln_w + ln_b + bonus) * g
    for s in range(nseq):
        y_ref[s, j * CHUNK:(j + 1) * CHUNK, :] = out[s * CHUNK:(s + 1) * CHUNK]


def _round_side(side_in, side_out):
    for src, dst in zip(side_in, side_out):
        half = src.shape[0] // 2
        for rows in (slice(0, half), slice(half, 2 * half)):
            dst[rows, :] = src[rows, :].astype(BF16)
            yield


def _rwkv_prompt_kernel(nside, xs_ref, prm_ref, wl_ref, tri_ref, bmask_ref, *refs):
    side_in, (y_ref, hout_ref) = refs[:nside], refs[nside:nside + 2]
    side_out, h_ref = refs[nside + 2:2 * nside + 2], refs[2 * nside + 2]
    c = pl.program_id(0)
    nseq, tstep, _ = xs_ref.shape
    nchunk = tstep // CHUNK

    @pl.when(c == 0)
    def _():
        h_ref[...] = jnp.zeros_like(h_ref)

    xs = jnp.concatenate([xs_ref[s] for s in range(nseq)], axis=0)
    prm = prm_ref[...]
    r, logw, k, v, a, b, g, bonus = _rwkv_features(xs, prm, wl_ref, bmask_ref)
    bmask = bmask_ref[...]
    lanes = [(s, q) for s in range(nseq) for q in range(2)]
    pre = {}

    def prep(chunks):
        chains = [(j, s, q) for j in chunks for s, q in lanes]
        cut = lambda x: [x[s * tstep + j * CHUNK:s * tstep + (j + 1) * CHUNK, q * QUAD:(q + 1) * QUAD]
                         for j, s, q in chains]
        res = yield from _chunk_prep(cut(r), cut(logw), cut(k), cut(v), cut(a), cut(b), tri_ref, bmask)
        for n, j in enumerate(chunks):
            pre[j] = res[n * len(lanes):(n + 1) * len(lanes)]

    nfirst = max(nchunk - 1, 1)
    _interleave(prep(range(nfirst)))
    later = prep(range(nfirst, nchunk))
    state = [h_ref[s, q] for s, q in lanes]
    crow = lambda x, j: jnp.concatenate([x[s * tstep + j * CHUNK:s * tstep + (j + 1) * CHUNK]
                                         for s in range(nseq)], axis=0)
    finish = iter(())
    rounding = _round_side(side_in, side_out)
    for j in range(nchunk):
        res = {}
        for _ in _chunk_step(pre[j], state, bmask, res):
            next(later, None)
            next(finish, None)
            next(rounding, None)
        _interleave(finish)
        if j == nfirst - 1:
            _interleave(later)
        state = res["state"]
        finish = _finish_stages(res["y"], crow(g, j), crow(bonus, j), prm, bmask_ref, y_ref, j)
    _interleave(finish, rounding)
    for i, (s, q) in enumerate(lanes):
        h_ref[s, q] = state[i]

    @pl.when(c == pl.num_programs(0) - 1)
    def _():
        hout_ref[...] = h_ref[...]


RWKV_CHUNKS_PER_STEP = 4


def _side_specs(side, nsteps):
    specs = []
    for w in side:
        rows, rem = divmod(w.shape[0], nsteps)
        assert rem == 0 and rows % 32 == 0, "two halves of whole bf16 row tiles per step"
        specs.append(pl.BlockSpec((rows, w.shape[1]), lambda c: (c, 0)))
    return specs


def _rwkv_prompt(xs3d, prm, wl, tri, bmask, side):
    bsz, t, _ = xs3d.shape
    tstep = RWKV_CHUNKS_PER_STEP * CHUNK
    nc = t // tstep
    const = lambda shape: pl.BlockSpec(shape, lambda c: (0,) * len(shape))
    state_shape = (bsz, 2, QUAD, QUAD // 2)
    side_specs = _side_specs(side, nc)
    return pl.pallas_call(
        functools.partial(_rwkv_prompt_kernel, len(side)),
        grid=(nc,),
        in_specs=[
            pl.BlockSpec((bsz, tstep, D_SHIFT), lambda c: (0, c, 0)),
            const((16, D_RWKV)),
            const((D_LORA_PAD, 3 * D_RWKV)),
            const((CHUNK, CHUNK)),
            const((QUAD, QUAD)),
        ] + side_specs,
        out_specs=[
            pl.BlockSpec((bsz, tstep, D_RWKV), lambda c: (0, c, 0)),
            const(state_shape),
        ] + side_specs,
        out_shape=[
            jax.ShapeDtypeStruct((bsz, t, D_RWKV), F32),
            jax.ShapeDtypeStruct(state_shape, F32),
        ] + [jax.ShapeDtypeStruct(w.shape, BF16) for w in side],
        scratch_shapes=[pltpu.VMEM(state_shape, F32)],
        compiler_params=pltpu.CompilerParams(
            dimension_semantics=("arbitrary",), vmem_limit_bytes=VMEM_LIMIT),
        name="rwkv_prompt",
    )(xs3d, prm, wl, tri, bmask, *side)


def _rope_lane_freq():
    pair = jnp.asarray((np.arange(2 * HEAD_DIM) % HEAD_DIM) % ROPE_HALF, F32)
    return jnp.power(ROPE_THETA, -pair * (2.0 / ROPE_DIM))[None, :]


def _rope_tables(cos, sin):
    dim = lax.broadcasted_iota(jnp.int32, cos.shape, 1) & (HEAD_DIM - 1)
    cos_t = jnp.where(dim < ROPE_DIM, cos, 1.0)
    sin_lo = jnp.where(dim < ROPE_HALF, -sin, 0.0)
    sin_hi = jnp.where((dim >= ROPE_HALF) & (dim < ROPE_DIM), sin, 0.0)
    return cos_t, sin_lo, sin_hi


def _qk_norm_rope(x, norm_w, cos_t, sin_lo, sin_hi, ones_ref):
    ms = _head_sums([x * x], ones_ref)[0] * (1.0 / HEAD_DIM)
    xn = x * lax.rsqrt(ms + RMS_EPS) * norm_w
    width = x.shape[1]
    fwd = pltpu.roll(xn, width - ROPE_HALF, 1)
    bwd = pltpu.roll(xn, ROPE_HALF, 1)
    return xn * cos_t + fwd * sin_lo + bwd * sin_hi


def _tile_lanes(x, reps):
    return jnp.concatenate([x] * reps, axis=1) if reps > 1 else x


def _attn_stages(q_ref, kv_ref, taba_ref, tabb_ref, qw_ref, kw_ref, sink_ref, ones_ref, bias_ref, first_bias,
                 kprev_ref, vprev_ref, out):
    nseq, tq, _ = q_ref.shape
    nblk = tq // WINDOW
    units = [(s, b) for s in range(nseq) for b in range(nblk)]
    blk = lambda b: slice(b * WINDOW, (b + 1) * WINDOW)
    tb = tabb_ref[...]
    cos_b, sin_b = tb[:, :128], tb[:, 128:]
    rope, rope4 = [], []
    for b in range(nblk):
        ta = taba_ref[b]
        cos_a, sin_a = ta[:, :128], ta[:, 128:]
        tabs = _rope_tables(cos_a * cos_b - sin_a * sin_b, sin_a * cos_b + cos_a * sin_b)
        rope.append(tabs)
        rope4.append([_tile_lanes(x, 4) for x in tabs])
    q = {(s, b): _qk_norm_rope(q_ref[s, blk(b), :], qw_ref[...], *rope4[b], ones_ref) * (ATTN_SCALE * LOG2E)
         for s, b in units}
    kv = {(s, b): kv_ref[s, blk(b), :] for s, b in units}
    k_cur = {u: _qk_norm_rope(kv[u][:, 0:D_KV], kw_ref[...], *rope[u[1]], ones_ref) for u in units}
    v_cur = {u: kv[u][:, D_KV:] for u in units}
    k_all = {(s, b): jnp.concatenate([kprev_ref[s] if b == 0 else k_cur[(s, b - 1)], k_cur[(s, b)]], axis=0)
             for s, b in units}
    v_all = {(s, b): jnp.concatenate([vprev_ref[s] if b == 0 else v_cur[(s, b - 1)], v_cur[(s, b)]], axis=0)
             for s, b in units}
    out["k_cur"] = [k_cur[(s, nblk - 1)] for s in range(nseq)]
    out["v_cur"] = [v_cur[(s, nblk - 1)] for s in range(nseq)]
    for s in range(nseq):
        kprev_ref[s] = out["k_cur"][s]
        vprev_ref[s] = out["v_cur"][s]
    yield

    nk = 2 * WINDOW
    bias = [bias_ref[first_bias] if b == 0 else bias_ref[1] for b in range(nblk)]
    sinks = sink_ref[...] * LOG2E
    low = lax.broadcasted_iota(jnp.int32, (nk, D_KV), 1) < HEAD_DIM
    lane_blk = [ones_ref[j * HEAD_DIM:j * HEAD_DIM + 1, :] for j in range(GQA_GROUP)]

    chains = [(u, g) for u in units for g in range(N_KV_HEADS)]
    ch = range(len(chains))
    k_rot = {u: pltpu.roll(k_all[u], HEAD_DIM, 1) for u in units}
    k2 = [jnp.where(low, k_all[u], k_rot[u]) if g == 0 else jnp.where(low, k_rot[u], k_all[u])
          for u, g in chains]
    k4 = [jnp.concatenate([k2[c], k2[c]], axis=1).astype(BF16) for c in ch]
    vb = {u: v_all[u].astype(BF16) for u in units}
    qg = [q[u][:, g * QUAD:(g + 1) * QUAD].astype(BF16) for u, g in chains]
    qstack = [jnp.concatenate([qg[c] * lane_blk[j] for j in range(GQA_GROUP)], axis=0) for c in ch]
    sink_row = [jnp.concatenate(
        [jnp.broadcast_to(sinks[:, g * GQA_GROUP + j:g * GQA_GROUP + j + 1], (1, WINDOW))
         for j in range(GQA_GROUP)], axis=1) for u, g in chains]
    yield
    sc = [_mm_nt(k4[c], qstack[c]) + bias[u[1]] for c, (u, g) in enumerate(chains)]
    yield
    m = [jnp.maximum(jnp.max(sc[c], axis=0, keepdims=True), sink_row[c]) for c in ch]
    e = [jnp.exp2(sc[c] - m[c]) for c in ch]
    yield
    denom = [jnp.sum(e[c], axis=0, keepdims=True) + jnp.exp2(sink_row[c] - m[c]) for c in ch]
    ot = {(u, g): _mm_tn(vb[u], e[c].astype(BF16))[g * HEAD_DIM:(g + 1) * HEAD_DIM, :] * (1.0 / denom[c])
          for c, (u, g) in enumerate(chains)}
    yield
    ya = []
    for s in range(nseq):
        blocks = []
        for b in range(nblk):
            yt = jnp.concatenate([ot[((s, b), g)][:, j * WINDOW:(j + 1) * WINDOW]
                                  for g in range(N_KV_HEADS) for j in range(GQA_GROUP)], axis=0)
            blocks.append(jnp.transpose(yt))
        ya.append(jnp.concatenate(blocks, axis=0) if nblk > 1 else blocks[0])
    out["ya"] = ya


def _ffn_stages(x, yr, ya, wo_ref, nw_ref, wu_ref, wd_ref, out, pieces=4):
    mix = jnp.concatenate([yr, ya], axis=1).astype(BF16)
    x1 = x + _mm(mix, wo_ref[...])
    yield
    ms = jnp.mean(x1 * x1, axis=-1, keepdims=True)
    hf = ((x1 * lax.rsqrt(ms + RMS_EPS)) * nw_ref[...]).astype(BF16)
    acc = x1
    step = D_FF // pieces
    for j in range(pieces):
        up = _mm(hf, wu_ref[:, j * step:(j + 1) * step])
        yield
        act = jnp.square(jnp.maximum(up, 0.0)).astype(BF16)
        acc = acc + _mm(act, wd_ref[j * step:(j + 1) * step, :])
        yield
    out["y"] = acc


def _attn_ffn_kernel(q_ref, kv_ref, taba_ref, tabb_ref, qw_ref, kw_ref, sink_ref, ones_ref, bias_ref,
                     x_ref, yr_ref, xd_ref, yrd_ref, yad_ref, wo_ref, nw_ref, wu_ref, wd_ref,
                     o_ref, od_ref, kwin_ref, vwin_ref, kprev_ref, vprev_ref, ya_ref):
    i = pl.program_id(0)
    nseq, tq, _ = q_ref.shape
    seqs = range(nseq)
    first = i == 0

    @pl.when(first)
    def _():
        kprev_ref[...] = jnp.zeros_like(kprev_ref)
        vprev_ref[...] = jnp.zeros_like(vprev_ref)
        ya_ref[...] = jnp.zeros_like(ya_ref)

    def rows(ref, dec_ref):
        tile = jnp.concatenate([ref[s] for s in seqs], axis=0)
        dec = dec_ref[...]
        return jnp.where(first, jnp.concatenate([dec] * (tile.shape[0] // dec.shape[0]), axis=0), tile)

    a_out, f_out = {}, {}
    _interleave(
        _ffn_stages(rows(x_ref, xd_ref), rows(yr_ref, yrd_ref), rows(ya_ref, yad_ref),
                    wo_ref, nw_ref, wu_ref, wd_ref, f_out),
        _attn_stages(q_ref, kv_ref, taba_ref, tabb_ref, qw_ref, kw_ref, sink_ref, ones_ref, bias_ref,
                     jnp.minimum(i, 1), kprev_ref, vprev_ref, a_out))
    for s in seqs:
        o_ref[s] = f_out["y"][s * tq:(s + 1) * tq]
        ya_ref[s] = a_out["ya"][s]

    @pl.when(first)
    def _():
        od_ref[...] = f_out["y"][:od_ref.shape[0]].reshape(od_ref.shape)

    @pl.when(i == pl.num_programs(0) - 2)
    def _():
        for s in seqs:
            kwin_ref[s] = jnp.transpose(a_out["k_cur"][s])
            vwin_ref[s] = jnp.transpose(a_out["v_cur"][s])


def _band_bias():
    ki = np.arange(2 * WINDOW)[:, None]
    qi = (np.arange(GQA_GROUP * WINDOW) % WINDOW + WINDOW)[None, :]
    dq = qi - ki
    band = (dq >= 0) & (dq < WINDOW)
    first = band & (ki >= WINDOW)
    return jnp.asarray(np.where(np.stack([first, band]), 0.0, NEG_INF), F32)


def _rope_block_tables(nb):
    freq = _rope_lane_freq()
    ang_a = (jnp.arange(nb, dtype=F32) * WINDOW)[:, None] * freq
    ang_b = jnp.arange(WINDOW, dtype=F32)[:, None] * freq
    tab_a = jnp.concatenate([jnp.cos(ang_a), jnp.sin(ang_a)], axis=1)
    tab_b = jnp.concatenate([jnp.cos(ang_b), jnp.sin(ang_b)], axis=1)
    return tab_a[:, None, :], tab_b


ATTN_BLOCKS_PER_STEP = 1


def _attn_ffn(q3d, kv3d, tab_a, tab_b, qw, kw, sinks, ones_bd, bias, x3d, yr3d, xd, yrd, yad, wo, nw, wu, wd):
    bsz, t, _ = q3d.shape
    nd = xd.shape[0]
    nblk = ATTN_BLOCKS_PER_STEP
    tq = nblk * WINDOW
    nt = t // tq
    const = lambda shape: pl.BlockSpec(shape, lambda i: (0,) * len(shape))
    single = lambda shape: pl.BlockSpec(shape, lambda i: (0,) * len(shape), pipeline_mode=pl.Buffered(1))
    cur = lambda i: jnp.minimum(i, nt - 1)
    prv = lambda i: jnp.maximum(i - 1, 0)
    return pl.pallas_call(
        _attn_ffn_kernel,
        grid=(nt + 1,),
        in_specs=[
            pl.BlockSpec((bsz, tq, D_ATTN), lambda i: (0, cur(i), 0)),
            pl.BlockSpec((bsz, tq, 2 * D_KV), lambda i: (0, cur(i), 0)),
            pl.BlockSpec((nblk, 1, 4 * HEAD_DIM), lambda i: (cur(i), 0, 0)),
            const((WINDOW, 4 * HEAD_DIM)),
            const((1, D_ATTN)),
            const((1, D_KV)),
            const((1, N_Q_HEADS)),
            const((QUAD, QUAD)),
            const((2, 2 * WINDOW, GQA_GROUP * WINDOW)),
            pl.BlockSpec((bsz, tq, D_MODEL), lambda i: (0, prv(i), 0)),
            pl.BlockSpec((bsz, tq, D_RWKV), lambda i: (0, prv(i), 0)),
            const((nd, D_MODEL)),
            const((nd, D_RWKV)),
            const((nd, D_ATTN)),
            single((D_MODEL, D_MODEL)),
            const((1, D_MODEL)),
            single((D_MODEL, D_FF)),
            single((D_FF, D_MODEL)),
        ],
        out_specs=[
            pl.BlockSpec((bsz, tq, D_MODEL), lambda i: (0, prv(i), 0)),
            const((nd, 1, D_MODEL)),
            const((bsz, WINDOW, D_KV)),
            const((bsz, WINDOW, D_KV)),
        ],
        out_shape=[
            jax.ShapeDtypeStruct((bsz, t, D_MODEL), F32),
            jax.ShapeDtypeStruct((nd, 1, D_MODEL), F32),
            jax.ShapeDtypeStruct((bsz, WINDOW, D_KV), F32),
            jax.ShapeDtypeStruct((bsz, WINDOW, D_KV), F32),
        ],
        scratch_shapes=[
            pltpu.VMEM((bsz, WINDOW, D_KV), F32),
            pltpu.VMEM((bsz, WINDOW, D_KV), F32),
            pltpu.VMEM((bsz, tq, D_ATTN), F32),
        ],
        compiler_params=pltpu.CompilerParams(
            dimension_semantics=("arbitrary",), vmem_limit_bytes=VMEM_LIMIT),
        name="attn_ffn",
    )(q3d, kv3d, tab_a, tab_b, qw, kw, sinks, ones_bd, bias, x3d, yr3d, xd, yrd, yad, wo, nw, wu, wd)


DEC_TILE = 16


def _decode_prep_kernel(x_ref, nw_ref, w_ref, sh_ref, mu_ref, prm_ref, wl_ref, ones_ref, tab_ref,
                        qw_ref, kw_ref, p_ref, vec_ref, vgb_ref, qn_ref, kvn_ref, col_ref):
    p, qkv = _norm_project(x_ref[...], nw_ref, w_ref)
    p_ref[...] = p
    xs = p + (sh_ref[...] - p) * mu_ref[...]
    r, logw, k, v, a, b, g, bonus = _rwkv_features(xs, prm_ref[...], wl_ref, ones_ref)
    for i, x in enumerate((a, b, k, jnp.exp(logw), r, v)):
        vec_ref[i] = jnp.transpose(x)
    vgb_ref[0] = g
    vgb_ref[1] = bonus
    n = p.shape[0]
    tab = jnp.broadcast_to(tab_ref[...], (n, 4 * HEAD_DIM))
    cos_t, sin_lo, sin_hi = _rope_tables(tab[:, :128], tab[:, 128:])
    qn_ref[...] = _qk_norm_rope(qkv[:, :D_ATTN], qw_ref[...], _tile_lanes(cos_t, 4), _tile_lanes(sin_lo, 4),
                                _tile_lanes(sin_hi, 4), ones_ref)
    k_new = _qk_norm_rope(qkv[:, D_ATTN:D_ATTN + D_KV], kw_ref[...], cos_t, sin_lo, sin_hi, ones_ref)
    kvn = jnp.concatenate([k_new, qkv[:, D_ATTN + D_KV:]], axis=1)
    kvn_ref[...] = kvn
    kvn_t = jnp.transpose(kvn)
    for i in range(n // DEC_TILE):
        col_ref[i] = kvn_t[:, i * DEC_TILE:(i + 1) * DEC_TILE]


def _decode_prep(x2d, norm_w, w_in_t, shift, mu_pad, prm, wl, bmask, tab, qw, kw):
    n = x2d.shape[0]
    full = lambda shape: pl.BlockSpec(shape, lambda i: (0,) * len(shape))
    return pl.pallas_call(
        _decode_prep_kernel,
        grid=(1,),
        in_specs=[full((n, D_MODEL)), full((1, D_MODEL)), full((D_IN, D_MODEL)), full((n, D_SHIFT)),
                  full((1, D_SHIFT)), full((16, D_RWKV)), full((D_LORA_PAD, 3 * D_RWKV)),
                  full((QUAD, QUAD)), full((1, 4 * HEAD_DIM)), full((1, D_ATTN)), full((1, D_KV))],
        out_specs=[full((n, D_SHIFT)), full((6, D_RWKV, n)), full((2, n, D_RWKV)), full((n, D_ATTN)),
                   full((n, 2 * D_KV)), full((n // DEC_TILE, 2 * D_KV, DEC_TILE))],
        out_shape=[
            jax.ShapeDtypeStruct((n, D_SHIFT), F32),
            jax.ShapeDtypeStruct((6, D_RWKV, n), F32),
            jax.ShapeDtypeStruct((2, n, D_RWKV), F32),
            jax.ShapeDtypeStruct((n, D_ATTN), F32),
            jax.ShapeDtypeStruct((n, 2 * D_KV), F32),
            jax.ShapeDtypeStruct((n // DEC_TILE, 2 * D_KV, DEC_TILE), F32),
        ],
        compiler_params=pltpu.CompilerParams(
            dimension_semantics=("arbitrary",), vmem_limit_bytes=VMEM_LIMIT),
        name="decode_prep",
    )(x2d, norm_w, w_in_t, shift, mu_pad, prm, wl, bmask, tab, qw, kw)


def _decode_state_stages(vec_ref, s_ref, sout_ref, yt_ref, h):
    a_t, b_t, k_t, w_t, r_t = (vec_ref[i] for i in range(5))
    for i in range(HEAD_DIM):
        s = s_ref[0, i]
        sa = jnp.sum(s * a_t, axis=0, keepdims=True)
        v_i = vec_ref[5, i:i + 1, :]
        s_new = s * w_t + sa * b_t + v_i * k_t
        sout_ref[0, i] = s_new
        yt_ref[pl.ds(h * HEAD_DIM + i, 1), :] = jnp.sum(s_new * r_t, axis=0, keepdims=True)
        if i % 8 == 7:
            yield


def _decode_attn_stages(qn_ref, kvn_ref, col_ref, ck_ref, cv_ref, sink_ref, ya_ref, kout_ref, vout_ref):
    nh = N_Q_HEADS
    seqs = range(DEC_TILE)
    hrow = lax.broadcasted_iota(jnp.int32, (nh, D_ATTN), 0)
    hlane = lax.broadcasted_iota(jnp.int32, (nh, D_ATTN), 1) // HEAD_DIM
    dmask = hrow == hlane
    grow = lax.broadcasted_iota(jnp.int32, (nh, D_KV), 0) // GQA_GROUP
    glane = lax.broadcasted_iota(jnp.int32, (nh, D_KV), 1) // HEAD_DIM
    low = glane == 0
    key_idx = lax.broadcasted_iota(jnp.int32, (nh, WINDOW), 1)
    last = lax.broadcasted_iota(jnp.int32, (D_KV, WINDOW), 1) == WINDOW - 1
    sink = sink_ref[...]
    kvn = kvn_ref[...]
    col = col_ref[0]
    k_new = [kvn[j:j + 1, 0:D_KV] for j in seqs]
    v_new = [kvn[j:j + 1, D_KV:] for j in seqs]
    ck = [ck_ref[j] for j in seqs]
    cv = [cv_ref[j] for j in seqs]
    for j in seqs:
        kout_ref[j] = jnp.where(last, col[0:D_KV, j:j + 1], pltpu.roll(ck[j], WINDOW - 1, 1))
        vout_ref[j] = jnp.where(last, col[D_KV:, j:j + 1], pltpu.roll(cv[j], WINDOW - 1, 1))
    yield
    qn = qn_ref[...]
    swap = (grow != (lax.broadcasted_iota(jnp.int32, (nh, D_KV), 0) & 1))
    qp = []
    for j in seqs:
        own = jnp.where(dmask, jnp.broadcast_to(qn[j:j + 1, :], (nh, D_ATTN)), 0.0)
        fold = own[:, 0:D_KV] + own[:, D_KV:2 * D_KV] + own[:, 2 * D_KV:3 * D_KV] + own[:, 3 * D_KV:]
        qp.append(jnp.where(swap, pltpu.roll(fold, HEAD_DIM, 1), fold))
    s_c = [jnp.where(key_idx >= 1, jnp.dot(qp[j], ck[j], preferred_element_type=F32) * ATTN_SCALE, NEG_INF)
           for j in seqs]
    s_n = [jnp.sum(qp[j] * k_new[j], axis=-1, keepdims=True) * ATTN_SCALE for j in seqs]
    yield
    m = [jnp.maximum(jnp.maximum(jnp.max(s_c[j], axis=-1, keepdims=True), s_n[j]), sink) for j in seqs]
    yield
    e_c = [jnp.exp(s_c[j] - m[j]) for j in seqs]
    e_n = [jnp.exp(s_n[j] - m[j]) for j in seqs]
    denom = [jnp.sum(e_c[j], axis=-1, keepdims=True) + e_n[j] + jnp.exp(sink - m[j]) for j in seqs]
    yield
    o = [(_dot_nt_f32(e_c[j], cv[j]) + e_n[j] * v_new[j]) / denom[j]
         for j in seqs]
    yield
    out_rows = []
    for j in seqs:
        rot = pltpu.roll(o[j], HEAD_DIM, 1)
        g0 = jnp.where(low, o[j], rot)
        g1 = jnp.where(low, rot, o[j])
        wide = jnp.concatenate([g0, g0, g1, g1], axis=1)
        out_rows.append(jnp.sum(jnp.where(dmask, wide, 0.0), axis=0, keepdims=True))
    ya_ref[...] = jnp.concatenate(out_rows, axis=0)


def _decode_step_kernel(vec_ref, gb_ref, prm_ref, ones_ref, s_ref, qn_ref, kvn_ref, col_ref, ck_ref, cv_ref,
                        sink_ref, sout_ref, yr_ref, ya_ref, kout_ref, vout_ref, yt_ref):
    i = pl.program_id(0)
    _interleave(
        _decode_attn_stages(qn_ref, kvn_ref, col_ref, ck_ref, cv_ref, sink_ref, ya_ref, kout_ref, vout_ref),
        _decode_state_stages(vec_ref, s_ref, sout_ref, yt_ref, i))

    @pl.when(i == pl.num_programs(0) - 1)
    def _():
        y = jnp.transpose(yt_ref[...])
        yr_ref[...] = _rwkv_finish(y, gb_ref[0], gb_ref[1], prm_ref[...], ones_ref)


def _decode_step(vec_t, gb, prm, bmask, s_t, qn, kvn, cols, ck_t, cv_t, sinks_col):
    n = s_t.shape[-1]
    bt = DEC_TILE
    assert n // bt == H_RWKV, "one sequence tile per RWKV head"
    const = lambda shape: pl.BlockSpec(shape, lambda i: (0,) * len(shape))
    return pl.pallas_call(
        _decode_step_kernel,
        grid=(H_RWKV,),
        in_specs=[
            pl.BlockSpec((6, HEAD_DIM, n), lambda i: (0, i, 0)),
            const((2, n, D_RWKV)),
            const((16, D_RWKV)),
            const((QUAD, QUAD)),
            pl.BlockSpec((1, HEAD_DIM, HEAD_DIM, n), lambda i: (i, 0, 0, 0)),
            pl.BlockSpec((bt, D_ATTN), lambda i: (i, 0)),
            pl.BlockSpec((bt, 2 * D_KV), lambda i: (i, 0)),
            pl.BlockSpec((1, 2 * D_KV, bt), lambda i: (i, 0, 0)),
            pl.BlockSpec((bt, D_KV, WINDOW), lambda i: (i, 0, 0)),
            pl.BlockSpec((bt, D_KV, WINDOW), lambda i: (i, 0, 0)),
            const((N_Q_HEADS, 1)),
        ],
        out_specs=[
            pl.BlockSpec((1, HEAD_DIM, HEAD_DIM, n), lambda i: (i, 0, 0, 0)),
            const((n, D_RWKV)),
            pl.BlockSpec((bt, D_ATTN), lambda i: (i, 0)),
            pl.BlockSpec((bt, D_KV, WINDOW), lambda i: (i, 0, 0)),
            pl.BlockSpec((bt, D_KV, WINDOW), lambda i: (i, 0, 0)),
        ],
        out_shape=[
            jax.ShapeDtypeStruct((H_RWKV, HEAD_DIM, HEAD_DIM, n), F32),
            jax.ShapeDtypeStruct((n, D_RWKV), F32),
            jax.ShapeDtypeStruct((n, D_ATTN), F32),
            jax.ShapeDtypeStruct((n, D_KV, WINDOW), F32),
            jax.ShapeDtypeStruct((n, D_KV, WINDOW), F32),
        ],
        scratch_shapes=[pltpu.VMEM((D_RWKV, n), F32)],
        compiler_params=pltpu.CompilerParams(
            dimension_semantics=("arbitrary",), vmem_limit_bytes=VMEM_LIMIT),
        name="decode_step",
    )(vec_t, gb, prm, bmask, s_t, qn, kvn, cols, ck_t, cv_t, sinks_col)


def kernel(x_prompt, x_sample, state_wkv, state_shift, cache_k_win, cache_v_win, norm_mix_w, w_in, mu_shift, w0, w_decay_up, a0, w_a_up, w_g_up, k_k, k_a, r_k, ln_x_w, ln_x_b, q_norm_w, k_norm_w, sinks, w_out, norm_ffn_w, w_ffn_up, w_ffn_down):
    bsz, t, _ = x_prompt.shape
    nd = x_sample.shape[0]
    l = 0

    w_in_pad = jnp.swapaxes(w_in[l], 0, 1)
    mu_pad = mu_shift[l][None, :]
    wl = jnp.zeros((D_LORA_PAD, 3 * D_RWKV), F32)
    wl = wl.at[0:32, 0:D_RWKV].set(w_decay_up[l])
    wl = wl.at[32:64, D_RWKV:2 * D_RWKV].set(w_a_up[l])
    wl = wl.at[64:160, 2 * D_RWKV:].set(w_g_up[l])
    wl = wl.astype(BF16)
    prm = jnp.zeros((16, D_RWKV), F32)
    prm = prm.at[0].set(w0[l]).at[1].set(a0[l]).at[2].set(k_k[l]).at[3].set(k_a[l])
    prm = prm.at[4].set(r_k[l].reshape(-1)).at[5].set(ln_x_w[l]).at[6].set(ln_x_b[l])
    hid = np.arange(QUAD) // HEAD_DIM
    bmask = jnp.asarray(hid[:, None] == hid[None, :], BF16)
    tri = jnp.asarray(np.tril(np.ones((CHUNK, CHUNK))), BF16)
    qw = jnp.tile(q_norm_w[l][None, :], (1, N_Q_HEADS))
    kw = jnp.tile(k_norm_w[l][None, :], (1, N_KV_HEADS))
    nmw = norm_mix_w[l][None, :]
    nfw = norm_ffn_w[l][None, :]
    tab_a, tab_b = _rope_block_tables(max(t, PAST_LEN + 1) // WINDOW + 1)
    ta, tb = tab_a[PAST_LEN // WINDOW], tab_b[PAST_LEN % WINDOW][None, :]
    tab_s = jnp.concatenate([ta[:, :128] * tb[:, :128] - ta[:, 128:] * tb[:, 128:],
                             ta[:, 128:] * tb[:, :128] + ta[:, :128] * tb[:, 128:]], axis=1)

    xs = x_sample.reshape(nd, D_MODEL)
    shift_in = state_shift[l].reshape(nd, D_SHIFT)
    p_s, vec_t, gb, qn_s, kvn_s, cols = _decode_prep(xs, nmw, w_in_pad, shift_in, mu_pad, prm, wl, bmask, tab_s, qw, kw)
    s_t = jnp.transpose(state_wkv[l], (1, 2, 3, 0))
    ck_t = jnp.swapaxes(cache_k_win[l].reshape(nd, WINDOW, D_KV), 1, 2)
    cv_t = jnp.swapaxes(cache_v_win[l].reshape(nd, WINDOW, D_KV), 1, 2)
    wkv_t, yr_s, ya_s, kc_t, vc_t = _decode_step(vec_t, gb, prm, bmask, s_t, qn_s, kvn_s, cols, ck_t, cv_t,
                                                 sinks[l][:, None])

    xp = x_prompt.reshape(bsz * t, D_MODEL)
    xs_p, plast, q_p, kv_p = _inproj_shift(xp, nmw, w_in_pad, mu_pad, t, 1024)
    yr_p, hbd, wo, wu, wd = _rwkv_prompt(xs_p.reshape(bsz, t, D_SHIFT), prm, wl, tri, bmask,
                                         (w_out[l], w_ffn_up[l], w_ffn_down[l]))
    y_prompt, y_s, kwin_p, vwin_p = _attn_ffn(q_p.reshape(bsz, t, D_ATTN), kv_p.reshape(bsz, t, 2 * D_KV),
                                              tab_a, tab_b, qw, kw, sinks[l][None, :], bmask, _band_bias(),
                                              x_prompt, yr_p, xs, yr_s, ya_s, wo, nfw, wu, wd)
    hb = hbd.reshape(bsz, 2, 4, HEAD_DIM, 2, HEAD_DIM)
    wkv_prompt = jnp.stack([hb[:, :, j, :, j % 2, :] for j in range(4)], axis=2)
    wkv_prompt = wkv_prompt.reshape(bsz, H_RWKV, HEAD_DIM, HEAD_DIM)[None]
    shift_prompt = plast[None]
    k_win_prompt = jnp.swapaxes(kwin_p, 1, 2).reshape(bsz, WINDOW, N_KV_HEADS, HEAD_DIM)[None]
    v_win_prompt = jnp.swapaxes(vwin_p, 1, 2).reshape(bsz, WINDOW, N_KV_HEADS, HEAD_DIM)[None]

    y_sample = y_s.reshape(nd, 1, D_MODEL)
    wkv_sample = jnp.transpose(wkv_t, (3, 0, 1, 2))[None]
    shift_sample = p_s.reshape(nd, 1, D_SHIFT)[None]
    k_win_sample = jnp.swapaxes(kc_t, 1, 2).reshape(nd, WINDOW, N_KV_HEADS, HEAD_DIM)[None]
    v_win_sample = jnp.swapaxes(vc_t, 1, 2).reshape(nd, WINDOW, N_KV_HEADS, HEAD_DIM)[None]

    return (y_prompt, y_sample, wkv_prompt, shift_prompt, k_win_prompt, v_win_prompt,
            wkv_sample, shift_sample, k_win_sample, v_win_sample)
```

```python
import functools

import jax
import jax.numpy as jnp
import numpy as np
from jax import lax
from jax.experimental import pallas as pl
from jax.experimental.pallas import tpu as pltpu

F32 = jnp.float32
BF16 = jnp.bfloat16

D_MODEL = 1024
D_RWKV = 512
D_ATTN = 512
HEAD_DIM = 64
H_RWKV = 8
N_Q_HEADS = 8
N_KV_HEADS = 2
GQA_GROUP = 4
D_KV = 128
D_LORA = 160
D_LORA_PAD = 256
D_SHIFT = 3 * D_RWKV + D_LORA
D_IN = D_SHIFT + D_ATTN + 2 * D_KV
WINDOW = 128
ROPE_DIM = 16
ROPE_HALF = 8
ROPE_THETA = 500000.0
ATTN_SCALE = HEAD_DIM ** -0.5
D_FF = 4096
RMS_EPS = 1e-6
LNX_EPS = 64e-5
NEG_INF = -1e30
LOG2E = 1.4426950408889634
PAST_LEN = 16384

CHUNK = 64
QUAD = 4 * HEAD_DIM
V7X_VMEM_BYTES = 64 * 1024 * 1024
VMEM_LIMIT = V7X_VMEM_BYTES // 8 * 7


def _split2(x):
    hi = x.astype(BF16)
    lo = (x - hi.astype(F32)).astype(BF16)
    return hi, lo


def _head_sums(xs, ones_ref):
    n, w = xs[0].shape
    tile = min(w, QUAD)
    per = w // tile
    pieces = [x[:, j * tile:(j + 1) * tile] for x in xs for j in range(per)]
    stacked = jnp.concatenate(pieces, axis=0) if len(pieces) > 1 else pieces[0]
    ones = ones_ref[0:tile, 0:tile]
    out = jnp.dot(stacked.astype(BF16), ones, preferred_element_type=F32)
    res = []
    for i in range(len(xs)):
        cols = [out[(i * per + j) * n:(i * per + j + 1) * n] for j in range(per)]
        res.append(jnp.concatenate(cols, axis=1) if per > 1 else cols[0])
    return res


def _cumsum_rows(tri_bf16, x):
    hi, lo = _split2(x)
    return (jnp.dot(tri_bf16, hi, preferred_element_type=F32)
            + jnp.dot(tri_bf16, lo, preferred_element_type=F32))


def _mm(a, b):
    return jnp.dot(a.astype(BF16), b.astype(BF16), preferred_element_type=F32)


def _mm_nt(a, b):
    return lax.dot_general(a.astype(BF16), b.astype(BF16), (((1,), (1,)), ((), ())),
                           preferred_element_type=F32)


def _mm_tn(a, b):
    return lax.dot_general(a.astype(BF16), b.astype(BF16), (((0,), (0,)), ((), ())),
                           preferred_element_type=F32)


def _dot_nt_f32(a, b):
    return lax.dot_general(a, b, (((1,), (1,)), ((), ())), preferred_element_type=F32)


def _sigmoid(x):
    return 1.0 / (1.0 + jnp.exp(-x))


def _interleave(*gens):
    live = list(gens)
    while live:
        for g in list(live):
            try:
                next(g)
            except StopIteration:
                live.remove(g)


def _norm_project(x, nw_ref, wt_ref):
    ms = jnp.mean(x * x, axis=-1, keepdims=True)
    h = ((x * lax.rsqrt(ms + RMS_EPS)) * nw_ref[...]).astype(BF16)
    return _mm_nt(h, wt_ref[0:D_SHIFT, :]), _mm_nt(h, wt_ref[D_SHIFT:, :])


def _inproj_shift_kernel(tiles_per_seq, x_ref, nw_ref, w_ref, mu_ref, xs_ref, last_ref, q_ref, kv_ref, prev_ref):
    i = pl.program_id(0)

    @pl.when(i % tiles_per_seq == 0)
    def _():
        prev_ref[...] = jnp.zeros_like(prev_ref)

    p, qkv = _norm_project(x_ref[...], nw_ref, w_ref)
    tm = p.shape[0]
    row = lax.broadcasted_iota(jnp.int32, p.shape, 0)
    prev = jnp.where(row == 0, jnp.broadcast_to(prev_ref[0:1, :], p.shape), pltpu.roll(p, 1, 0))
    xs_ref[...] = p + (prev - p) * mu_ref[...]
    last = jnp.broadcast_to(p[tm - 1:tm, :], prev_ref.shape)
    prev_ref[...] = last
    last_ref[0] = p[tm - 1:tm, :]
    q_ref[...] = qkv[:, :D_ATTN]
    kv_ref[...] = qkv[:, D_ATTN:]


def _inproj_shift(x2d, norm_w, w_in_pad, mu_pad, seq_len, tm):
    m = x2d.shape[0]
    tiles_per_seq = seq_len // tm
    return pl.pallas_call(
        functools.partial(_inproj_shift_kernel, tiles_per_seq),
        grid=(m // tm,),
        in_specs=[
            pl.BlockSpec((tm, D_MODEL), lambda i: (i, 0)),
            pl.BlockSpec((1, D_MODEL), lambda i: (0, 0)),
            pl.BlockSpec((D_IN, D_MODEL), lambda i: (0, 0)),
            pl.BlockSpec((1, D_SHIFT), lambda i: (0, 0)),
        ],
        out_specs=[
            pl.BlockSpec((tm, D_SHIFT), lambda i: (i, 0)),
            pl.BlockSpec((1, 1, D_SHIFT), lambda i: (i // tiles_per_seq, 0, 0)),
            pl.BlockSpec((tm, D_ATTN), lambda i: (i, 0)),
            pl.BlockSpec((tm, 2 * D_KV), lambda i: (i, 0)),
        ],
        out_shape=[
            jax.ShapeDtypeStruct((m, D_SHIFT), F32),
            jax.ShapeDtypeStruct((m // seq_len, 1, D_SHIFT), F32),
            jax.ShapeDtypeStruct((m, D_ATTN), F32),
            jax.ShapeDtypeStruct((m, 2 * D_KV), F32),
        ],
        scratch_shapes=[pltpu.VMEM((8, D_SHIFT), F32)],
        compiler_params=pltpu.CompilerParams(
            dimension_semantics=("arbitrary",), vmem_limit_bytes=VMEM_LIMIT),
        name="inproj_shift",
    )(x2d, norm_w, w_in_pad, mu_pad)


def _rwkv_features(xs, prm, wl_ref, ones_ref):
    r = xs[:, 0:D_RWKV]
    k = xs[:, D_RWKV:2 * D_RWKV]
    v = xs[:, 2 * D_RWKV:3 * D_RWKV]
    lora = xs[:, 3 * D_RWKV:]
    lora = jnp.concatenate([lora, jnp.zeros((lora.shape[0], wl_ref.shape[0] - D_LORA), F32)], axis=1)
    col = lax.broadcasted_iota(jnp.int32, lora.shape, 1)
    act = jnp.where(col < 32, jnp.tanh(lora), jnp.where(col < 64, lora, _sigmoid(lora)))
    up = jnp.dot(act.astype(BF16), wl_ref[...], preferred_element_type=F32)
    w0, a0, k_k, k_a, r_k = prm[0:1], prm[1:2], prm[2:3], prm[3:4], prm[4:5]
    logw = (-np.exp(-0.5)) * _sigmoid(w0 + up[:, 0:D_RWKV])
    asig = _sigmoid(a0 + up[:, D_RWKV:2 * D_RWKV])
    g = up[:, 2 * D_RWKV:]
    kk = k * k_k
    k_mod = k * (1.0 + (asig - 1.0) * k_a)
    ss, rk = _head_sums([kk * kk, r * k_mod * r_k], ones_ref)
    kk = kk / jnp.maximum(jnp.sqrt(ss), 1e-12)
    k = k_mod
    bonus = rk * v
    return r, logw, k, v, -kk, kk * asig, g, bonus


def _rwkv_finish(y, g, bonus, prm, ones_ref):
    ln_w, ln_b = prm[5:6], prm[6:7]
    mean = _head_sums([y], ones_ref)[0] * (1.0 / HEAD_DIM)
    d = y - mean
    var = _head_sums([d * d], ones_ref)[0] * (1.0 / HEAD_DIM)
    yn = d * lax.rsqrt(var + LNX_EPS) * ln_w + ln_b
    return (yn + bonus) * g


def _block_diag(x, bmask):
    return jnp.concatenate([x] * 4, axis=0) * bmask


def _chunk_prep(r, logw, k, v, a, b, tri_ref, bmask):
    n = len(r)
    ch = range(n)
    tri = tri_ref[...]
    cum = [_cumsum_rows(tri, logw[i]) for i in ch]
    yield
    e_in = [jnp.exp(cum[i]) for i in ch]
    e_ex = [jnp.exp(cum[i] - logw[i]) for i in ch]
    e_inv = [1.0 / e_in[i] for i in ch]
    e_last = [e_in[i][CHUNK - 1:CHUNK, :] for i in ch]
    rt = [(r[i] * e_in[i]).astype(BF16) for i in ch]
    at = [(a[i] * e_ex[i]).astype(BF16) for i in ch]
    kt = [(k[i] * e_inv[i]).astype(BF16) for i in ch]
    bt = [(b[i] * e_inv[i]).astype(BF16) for i in ch]
    vb = [v[i].astype(BF16) for i in ch]
    yield

    t_idx = lax.broadcasted_iota(jnp.int32, (CHUNK, QUAD), 0)
    s_idx = lax.broadcasted_iota(jnp.int32, (CHUNK, QUAD), 1) & (HEAD_DIM - 1)
    strict = s_idx < t_idx
    incl = s_idx <= t_idx

    gm = [_mm_nt(jnp.concatenate([at[i], rt[i]], axis=0),
                 jnp.concatenate([_block_diag(bt[i], bmask), _block_diag(kt[i], bmask)], axis=0))
          for i in ch]
    yield
    a_ab = [jnp.where(strict, gm[i][:CHUNK, :QUAD], 0.0) for i in ch]
    a_ak = [jnp.where(strict, gm[i][:CHUNK, QUAD:], 0.0) for i in ch]
    a_rb = [jnp.where(incl, gm[i][CHUNK:, :QUAD], 0.0) for i in ch]
    a_rk = [jnp.where(incl, gm[i][CHUNK:, QUAD:], 0.0) for i in ch]

    eye = jnp.where(s_idx == t_idx, 1.0, 0.0)
    pwb = [a_ab[i].astype(BF16) for i in ch]
    t_inv = [eye + a_ab[i] for i in ch]
    for it in range(6):
        rbd = [_block_diag(pwb[i], bmask) for i in ch]
        if it == 0:
            pwb = [_mm(pwb[i], rbd[i]).astype(BF16) for i in ch]
        elif it < 5:
            out = [_mm(jnp.concatenate([pwb[i], t_inv[i].astype(BF16)], axis=0), rbd[i]) for i in ch]
            pwb = [out[i][:CHUNK].astype(BF16) for i in ch]
            t_inv = [t_inv[i] + out[i][CHUNK:] for i in ch]
        else:
            t_inv = [t_inv[i] + _mm(t_inv[i], rbd[i]) for i in ch]
        yield

    vbd = [_block_diag(vb[i], bmask) for i in ch]
    xy0 = [_mm(jnp.concatenate([a_ak[i], a_rk[i]], axis=0), vbd[i]) for i in ch]
    return [dict(ar=jnp.concatenate([at[i], rt[i]], axis=0), x0=xy0[i][:CHUNK], y0=xy0[i][CHUNK:],
                 t_inv=t_inv[i].astype(BF16), a_rb=a_rb[i].astype(BF16), vb=vb[i],
                 bk=jnp.concatenate([bt[i], kt[i]], axis=0), e_last=e_last[i]) for i in ch]


def _chunk_step(pre, state, bmask, out):
    ch = range(len(pre))
    half = QUAD // 2
    zeros = jnp.zeros((half, half), BF16)
    sc = [state[i].astype(BF16) for i in ch]
    sb = [jnp.concatenate([jnp.concatenate([sc[i][:half], zeros], axis=1),
                           jnp.concatenate([zeros, sc[i][half:]], axis=1)], axis=0) for i in ch]
    xr = [_mm_nt(pre[i]["ar"], sb[i]) for i in ch]
    yield
    x = [xr[i][:CHUNK] + pre[i]["x0"] for i in ch]
    u = [_mm(pre[i]["t_inv"], _block_diag(x[i].astype(BF16), bmask)) for i in ch]
    yield
    ub = [u[i].astype(BF16) for i in ch]
    out["y"] = [xr[i][CHUNK:] + pre[i]["y0"] + _mm(pre[i]["a_rb"], _block_diag(ub[i], bmask)) for i in ch]
    upd = [_mm_tn(jnp.concatenate([ub[i], pre[i]["vb"]], axis=0), pre[i]["bk"]) for i in ch]
    yield
    bm = bmask[:half, :half].astype(F32)
    s_new = []
    for i in ch:
        e_last = pre[i]["e_last"]
        top = (state[i][:half] + upd[i][:half, :half] * bm) * e_last[:, :half]
        bot = (state[i][half:] + upd[i][half:, half:] * bm) * e_last[:, half:]
        s_new.append(jnp.concatenate([top, bot], axis=0))
    out["state"] = s_new


def _finish_stages(ys, g, bonus, prm, ones_ref, y_ref, j):
    nseq = len(ys) // 2
    ln_w, ln_b = prm[5:6], prm[6:7]
    y = jnp.concatenate([jnp.concatenate(ys[2 * s:2 * s + 2], axis=1) for s in range(nseq)], axis=0)
    mean = _head_sums([y], ones_ref)[0] * (1.0 / HEAD_DIM)
    yield
    d = y - mean
    var = _head_sums([d * d], ones_ref)[0] * (1.0 / HEAD_DIM)
    yield
    out = (d * lax.rsqrt(var + LNX_EPS) * ln_w + ln_b + bonus) * g
    for s in range(nseq):
        y_ref[s, j * CHUNK:(j + 1) * CHUNK, :] = out[s * CHUNK:(s + 1) * CHUNK]


def _round_side(side_in, side_out):
    for src, dst in zip(side_in, side_out):
        half = src.shape[0] // 2
        for rows in (slice(0, half), slice(half, 2 * half)):
            dst[rows, :] = src[rows, :].astype(BF16)
            yield


def _rwkv_prompt_kernel(nside, xs_ref, prm_ref, wl_ref, tri_ref, bmask_ref, *refs):
    side_in, (y_ref, hout_ref) = refs[:nside], refs[nside:nside + 2]
    side_out, h_ref = refs[nside + 2:2 * nside + 2], refs[2 * nside + 2]
    c = pl.program_id(0)
    nseq, tstep, _ = xs_ref.shape
    nchunk = tstep // CHUNK

    @pl.when(c == 0)
    def _():
        h_ref[...] = jnp.zeros_like(h_ref)

    xs = jnp.concatenate([xs_ref[s] for s in range(nseq)], axis=0)
    prm = prm_ref[...]
    r, logw, k, v, a, b, g, bonus = _rwkv_features(xs, prm, wl_ref, bmask_ref)
    bmask = bmask_ref[...]
    lanes = [(s, q) for s in range(nseq) for q in range(2)]
    pre = {}

    def prep(chunks):
        chains = [(j, s, q) for j in chunks for s, q in lanes]
        cut = lambda x: [x[s * tstep + j * CHUNK:s * tstep + (j + 1) * CHUNK, q * QUAD:(q + 1) * QUAD]
                         for j, s, q in chains]
        res = yield from _chunk_prep(cut(r), cut(logw), cut(k), cut(v), cut(a), cut(b), tri_ref, bmask)
        for n, j in enumerate(chunks):
            pre[j] = res[n * len(lanes):(n + 1) * len(lanes)]

    nfirst = max(nchunk - 1, 1)
    _interleave(prep(range(nfirst)))
    later = prep(range(nfirst, nchunk))
    state = [h_ref[s, q] for s, q in lanes]
    crow = lambda x, j: jnp.concatenate([x[s * tstep + j * CHUNK:s * tstep + (j + 1) * CHUNK]
                                         for s in range(nseq)], axis=0)
    finish = iter(())
    rounding = _round_side(side_in, side_out)
    for j in range(nchunk):
        res = {}
        for _ in _chunk_step(pre[j], state, bmask, res):
            next(later, None)
            next(finish, None)
            next(rounding, None)
        _interleave(finish)
        if j == nfirst - 1:
            _interleave(later)
        state = res["state"]
        finish = _finish_stages(res["y"], crow(g, j), crow(bonus, j), prm, bmask_ref, y_ref, j)
    _interleave(finish, rounding)
    for i, (s, q) in enumerate(lanes):
        h_ref[s, q] = state[i]

    @pl.when(c == pl.num_programs(0) - 1)
    def _():
        hout_ref[...] = h_ref[...]


RWKV_CHUNKS_PER_STEP = 8


def _side_specs(side, nsteps):
    specs = []
    for w in side:
        rows, rem = divmod(w.shape[0], nsteps)
        assert rem == 0 and rows % 32 == 0, "two halves of whole bf16 row tiles per step"
        specs.append(pl.BlockSpec((rows, w.shape[1]), lambda c: (c, 0)))
    return specs


def _rwkv_prompt(xs3d, prm, wl, tri, bmask, side):
    bsz, t, _ = xs3d.shape
    tstep = RWKV_CHUNKS_PER_STEP * CHUNK
    nc = t // tstep
    const = lambda shape: pl.BlockSpec(shape, lambda c: (0,) * len(shape))
    state_shape = (bsz, 2, QUAD, QUAD // 2)
    side_specs = _side_specs(side, nc)
    return pl.pallas_call(
        functools.partial(_rwkv_prompt_kernel, len(side)),
        grid=(nc,),
        in_specs=[
            pl.BlockSpec((bsz, tstep, D_SHIFT), lambda c: (0, c, 0)),
            const((16, D_RWKV)),
            const((D_LORA_PAD, 3 * D_RWKV)),
            const((CHUNK, CHUNK)),
            const((QUAD, QUAD)),
        ] + side_specs,
        out_specs=[
            pl.BlockSpec((bsz, tstep, D_RWKV), lambda c: (0, c, 0)),
            const(state_shape),
        ] + side_specs,
        out_shape=[
            jax.ShapeDtypeStruct((bsz, t, D_RWKV), F32),
            jax.ShapeDtypeStruct(state_shape, F32),
        ] + [jax.ShapeDtypeStruct(w.shape, BF16) for w in side],
        scratch_shapes=[pltpu.VMEM(state_shape, F32)],
        compiler_params=pltpu.CompilerParams(
            dimension_semantics=("arbitrary",), vmem_limit_bytes=VMEM_LIMIT),
        name="rwkv_prompt",
    )(xs3d, prm, wl, tri, bmask, *side)


def _rope_lane_freq():
    pair = jnp.asarray((np.arange(2 * HEAD_DIM) % HEAD_DIM) % ROPE_HALF, F32)
    return jnp.power(ROPE_THETA, -pair * (2.0 / ROPE_DIM))[None, :]


def _rope_tables(cos, sin):
    dim = lax.broadcasted_iota(jnp.int32, cos.shape, 1) & (HEAD_DIM - 1)
    cos_t = jnp.where(dim < ROPE_DIM, cos, 1.0)
    sin_lo = jnp.where(dim < ROPE_HALF, -sin, 0.0)
    sin_hi = jnp.where((dim >= ROPE_HALF) & (dim < ROPE_DIM), sin, 0.0)
    return cos_t, sin_lo, sin_hi


def _qk_norm_rope(x, norm_w, cos_t, sin_lo, sin_hi, ones_ref):
    ms = _head_sums([x * x], ones_ref)[0] * (1.0 / HEAD_DIM)
    xn = x * lax.rsqrt(ms + RMS_EPS) * norm_w
    width = x.shape[1]
    fwd = pltpu.roll(xn, width - ROPE_HALF, 1)
    bwd = pltpu.roll(xn, ROPE_HALF, 1)
    return xn * cos_t + fwd * sin_lo + bwd * sin_hi


def _tile_lanes(x, reps):
    return jnp.concatenate([x] * reps, axis=1) if reps > 1 else x


def _attn_stages(q_ref, kv_ref, taba_ref, tabb_ref, qw_ref, kw_ref, sink_ref, ones_ref, bias_ref, first_bias,
                 kprev_ref, vprev_ref, out):
    nseq, tq, _ = q_ref.shape
    nblk = tq // WINDOW
    units = [(s, b) for s in range(nseq) for b in range(nblk)]
    blk = lambda b: slice(b * WINDOW, (b + 1) * WINDOW)
    tb = tabb_ref[...]
    cos_b, sin_b = tb[:, :128], tb[:, 128:]
    rope, rope4 = [], []
    for b in range(nblk):
        ta = taba_ref[b]
        cos_a, sin_a = ta[:, :128], ta[:, 128:]
        tabs = _rope_tables(cos_a * cos_b - sin_a * sin_b, sin_a * cos_b + cos_a * sin_b)
        rope.append(tabs)
        rope4.append([_tile_lanes(x, 4) for x in tabs])
    q = {(s, b): _qk_norm_rope(q_ref[s, blk(b), :], qw_ref[...], *rope4[b], ones_ref) * (ATTN_SCALE * LOG2E)
         for s, b in units}
    kv = {(s, b): kv_ref[s, blk(b), :] for s, b in units}
    k_cur = {u: _qk_norm_rope(kv[u][:, 0:D_KV], kw_ref[...], *rope[u[1]], ones_ref) for u in units}
    v_cur = {u: kv[u][:, D_KV:] for u in units}
    k_all = {(s, b): jnp.concatenate([kprev_ref[s] if b == 0 else k_cur[(s, b - 1)], k_cur[(s, b)]], axis=0)
             for s, b in units}
    v_all = {(s, b): jnp.concatenate([vprev_ref[s] if b == 0 else v_cur[(s, b - 1)], v_cur[(s, b)]], axis=0)
             for s, b in units}
    out["k_cur"] = [k_cur[(s, nblk - 1)] for s in range(nseq)]
    out["v_cur"] = [v_cur[(s, nblk - 1)] for s in range(nseq)]
    for s in range(nseq):
        kprev_ref[s] = out["k_cur"][s]
        vprev_ref[s] = out["v_cur"][s]
    yield

    nk = 2 * WINDOW
    bias = [bias_ref[first_bias] if b == 0 else bias_ref[1] for b in range(nblk)]
    sinks = sink_ref[...] * LOG2E
    low = lax.broadcasted_iota(jnp.int32, (nk, D_KV), 1) < HEAD_DIM
    lane_blk = [ones_ref[j * HEAD_DIM:j * HEAD_DIM + 1, :] for j in range(GQA_GROUP)]

    chains = [(u, g) for u in units for g in range(N_KV_HEADS)]
    ch = range(len(chains))
    k_rot = {u: pltpu.roll(k_all[u], HEAD_DIM, 1) for u in units}
    k2 = [jnp.where(low, k_all[u], k_rot[u]) if g == 0 else jnp.where(low, k_rot[u], k_all[u])
          for u, g in chains]
    k4 = [jnp.concatenate([k2[c], k2[c]], axis=1).astype(BF16) for c in ch]
    vb = {u: v_all[u].astype(BF16) for u in units}
    qg = [q[u][:, g * QUAD:(g + 1) * QUAD].astype(BF16) for u, g in chains]
    qstack = [jnp.concatenate([qg[c] * lane_blk[j] for j in range(GQA_GROUP)], axis=0) for c in ch]
    sink_row = [jnp.concatenate(
        [jnp.broadcast_to(sinks[:, g * GQA_GROUP + j:g * GQA_GROUP + j + 1], (1, WINDOW))
         for j in range(GQA_GROUP)], axis=1) for u, g in chains]
    yield
    sc = [_mm_nt(k4[c], qstack[c]) + bias[u[1]] for c, (u, g) in enumerate(chains)]
    yield
    m = [jnp.maximum(jnp.max(sc[c], axis=0, keepdims=True), sink_row[c]) for c in ch]
    e = [jnp.exp2(sc[c] - m[c]) for c in ch]
    yield
    denom = [jnp.sum(e[c], axis=0, keepdims=True) + jnp.exp2(sink_row[c] - m[c]) for c in ch]
    ot = {(u, g): _mm_tn(vb[u], e[c].astype(BF16))[g * HEAD_DIM:(g + 1) * HEAD_DIM, :] * (1.0 / denom[c])
          for c, (u, g) in enumerate(chains)}
    yield
    ya = []
    for s in range(nseq):
        blocks = []
        for b in range(nblk):
            yt = jnp.concatenate([ot[((s, b), g)][:, j * WINDOW:(j + 1) * WINDOW]
                                  for g in range(N_KV_HEADS) for j in range(GQA_GROUP)], axis=0)
            blocks.append(jnp.transpose(yt))
        ya.append(jnp.concatenate(blocks, axis=0) if nblk > 1 else blocks[0])
    out["ya"] = ya


def _ffn_stages(x, yr, ya, wo_ref, nw_ref, wu_ref, wd_ref, out, pieces=4):
    mix = jnp.concatenate([yr, ya], axis=1).astype(BF16)
    x1 = x + _mm(mix, wo_ref[...])
    yield
    ms = jnp.mean(x1 * x1, axis=-1, keepdims=True)
    hf = ((x1 * lax.rsqrt(ms + RMS_EPS)) * nw_ref[...]).astype(BF16)
    acc = x1
    step = D_FF // pieces
    for j in range(pieces):
        up = _mm(hf, wu_ref[:, j * step:(j + 1) * step])
        yield
        act = jnp.square(jnp.maximum(up, 0.0)).astype(BF16)
        acc = acc + _mm(act, wd_ref[j * step:(j + 1) * step, :])
        yield
    out["y"] = acc


def _attn_ffn_kernel(q_ref, kv_ref, taba_ref, tabb_ref, qw_ref, kw_ref, sink_ref, ones_ref, bias_ref,
                     x_ref, yr_ref, xd_ref, yrd_ref, yad_ref, wo_ref, nw_ref, wu_ref, wd_ref,
                     o_ref, od_ref, kwin_ref, vwin_ref, kprev_ref, vprev_ref, ya_ref):
    i = pl.program_id(0)
    nseq, tq, _ = q_ref.shape
    seqs = range(nseq)
    first = i == 0

    @pl.when(first)
    def _():
        kprev_ref[...] = jnp.zeros_like(kprev_ref)
        vprev_ref[...] = jnp.zeros_like(vprev_ref)
        ya_ref[...] = jnp.zeros_like(ya_ref)

    def rows(ref, dec_ref):
        tile = jnp.concatenate([ref[s] for s in seqs], axis=0)
        dec = dec_ref[...]
        return jnp.where(first, jnp.concatenate([dec] * (tile.shape[0] // dec.shape[0]), axis=0), tile)

    a_out, f_out = {}, {}
    _interleave(
        _ffn_stages(rows(x_ref, xd_ref), rows(yr_ref, yrd_ref), rows(ya_ref, yad_ref),
                    wo_ref, nw_ref, wu_ref, wd_ref, f_out),
        _attn_stages(q_ref, kv_ref, taba_ref, tabb_ref, qw_ref, kw_ref, sink_ref, ones_ref, bias_ref,
                     jnp.minimum(i, 1), kprev_ref, vprev_ref, a_out))
    for s in seqs:
        o_ref[s] = f_out["y"][s * tq:(s + 1) * tq]
        ya_ref[s] = a_out["ya"][s]

    @pl.when(first)
    def _():
        od_ref[...] = f_out["y"][:od_ref.shape[0]].reshape(od_ref.shape)

    @pl.when(i == pl.num_programs(0) - 2)
    def _():
        for s in seqs:
            kwin_ref[s] = jnp.transpose(a_out["k_cur"][s])
            vwin_ref[s] = jnp.transpose(a_out["v_cur"][s])


def _band_bias():
    ki = np.arange(2 * WINDOW)[:, None]
    qi = (np.arange(GQA_GROUP * WINDOW) % WINDOW + WINDOW)[None, :]
    dq = qi - ki
    band = (dq >= 0) & (dq < WINDOW)
    first = band & (ki >= WINDOW)
    return jnp.asarray(np.where(np.stack([first, band]), 0.0, NEG_INF), F32)


def _rope_block_tables(nb):
    freq = _rope_lane_freq()
    ang_a = (jnp.arange(nb, dtype=F32) * WINDOW)[:, None] * freq
    ang_b = jnp.arange(WINDOW, dtype=F32)[:, None] * freq
    tab_a = jnp.concatenate([jnp.cos(ang_a), jnp.sin(ang_a)], axis=1)
    tab_b = jnp.concatenate([jnp.cos(ang_b), jnp.sin(ang_b)], axis=1)
    return tab_a[:, None, :], tab_b


ATTN_BLOCKS_PER_STEP = 1


def _attn_ffn(q3d, kv3d, tab_a, tab_b, qw, kw, sinks, ones_bd, bias, x3d, yr3d, xd, yrd, yad, wo, nw, wu, wd):
    bsz, t, _ = q3d.shape
    nd = xd.shape[0]
    nblk = ATTN_BLOCKS_PER_STEP
    tq = nblk * WINDOW
    nt = t // tq
    const = lambda shape: pl.BlockSpec(shape, lambda i: (0,) * len(shape))
    single = lambda shape: pl.BlockSpec(shape, lambda i: (0,) * len(shape), pipeline_mode=pl.Buffered(1))
    cur = lambda i: jnp.minimum(i, nt - 1)
    prv = lambda i: jnp.maximum(i - 1, 0)
    return pl.pallas_call(
        _attn_ffn_kernel,
        grid=(nt + 1,),
        in_specs=[
            pl.BlockSpec((bsz, tq, D_ATTN), lambda i: (0, cur(i), 0)),
            pl.BlockSpec((bsz, tq, 2 * D_KV), lambda i: (0, cur(i), 0)),
            pl.BlockSpec((nblk, 1, 4 * HEAD_DIM), lambda i: (cur(i), 0, 0)),
            const((WINDOW, 4 * HEAD_DIM)),
            const((1, D_ATTN)),
            const((1, D_KV)),
            const((1, N_Q_HEADS)),
            const((QUAD, QUAD)),
            const((2, 2 * WINDOW, GQA_GROUP * WINDOW)),
            pl.BlockSpec((bsz, tq, D_MODEL), lambda i: (0, prv(i), 0)),
            pl.BlockSpec((bsz, tq, D_RWKV), lambda i: (0, prv(i), 0)),
            const((nd, D_MODEL)),
            const((nd, D_RWKV)),
            const((nd, D_ATTN)),
            single((D_MODEL, D_MODEL)),
            const((1, D_MODEL)),
            single((D_MODEL, D_FF)),
            single((D_FF, D_MODEL)),
        ],
        out_specs=[
            pl.BlockSpec((bsz, tq, D_MODEL), lambda i: (0, prv(i), 0)),
            const((nd, 1, D_MODEL)),
            const((bsz, WINDOW, D_KV)),
            const((bsz, WINDOW, D_KV)),
        ],
        out_shape=[
            jax.ShapeDtypeStruct((bsz, t, D_MODEL), F32),
            jax.ShapeDtypeStruct((nd, 1, D_MODEL), F32),
            jax.ShapeDtypeStruct((bsz, WINDOW, D_KV), F32),
            jax.ShapeDtypeStruct((bsz, WINDOW, D_KV), F32),
        ],
        scratch_shapes=[
            pltpu.VMEM((bsz, WINDOW, D_KV), F32),
            pltpu.VMEM((bsz, WINDOW, D_KV), F32),
            pltpu.VMEM((bsz, tq, D_ATTN), F32),
        ],
        compiler_params=pltpu.CompilerParams(
            dimension_semantics=("arbitrary",), vmem_limit_bytes=VMEM_LIMIT),
        name="attn_ffn",
    )(q3d, kv3d, tab_a, tab_b, qw, kw, sinks, ones_bd, bias, x3d, yr3d, xd, yrd, yad, wo, nw, wu, wd)


DEC_TILE = 16


def _decode_prep_kernel(x_ref, nw_ref, w_ref, sh_ref, mu_ref, prm_ref, wl_ref, ones_ref, tab_ref,
                        qw_ref, kw_ref, p_ref, vec_ref, vgb_ref, qn_ref, kvn_ref, col_ref):
    p, qkv = _norm_project(x_ref[...], nw_ref, w_ref)
    p_ref[...] = p
    xs = p + (sh_ref[...] - p) * mu_ref[...]
    r, logw, k, v, a, b, g, bonus = _rwkv_features(xs, prm_ref[...], wl_ref, ones_ref)
    for i, x in enumerate((a, b, k, jnp.exp(logw), r, v)):
        vec_ref[i] = jnp.transpose(x)
    vgb_ref[0] = g
    vgb_ref[1] = bonus
    n = p.shape[0]
    tab = jnp.broadcast_to(tab_ref[...], (n, 4 * HEAD_DIM))
    cos_t, sin_lo, sin_hi = _rope_tables(tab[:, :128], tab[:, 128:])
    qn_ref[...] = _qk_norm_rope(qkv[:, :D_ATTN], qw_ref[...], _tile_lanes(cos_t, 4), _tile_lanes(sin_lo, 4),
                                _tile_lanes(sin_hi, 4), ones_ref)
    k_new = _qk_norm_rope(qkv[:, D_ATTN:D_ATTN + D_KV], kw_ref[...], cos_t, sin_lo, sin_hi, ones_ref)
    kvn = jnp.concatenate([k_new, qkv[:, D_ATTN + D_KV:]], axis=1)
    kvn_ref[...] = kvn
    kvn_t = jnp.transpose(kvn)
    for i in range(n // DEC_TILE):
        col_ref[i] = kvn_t[:, i * DEC_TILE:(i + 1) * DEC_TILE]


def _decode_prep(x2d, norm_w, w_in_t, shift, mu_pad, prm, wl, bmask, tab, qw, kw):
    n = x2d.shape[0]
    full = lambda shape: pl.BlockSpec(shape, lambda i: (0,) * len(shape))
    return pl.pallas_call(
        _decode_prep_kernel,
        grid=(1,),
        in_specs=[full((n, D_MODEL)), full((1, D_MODEL)), full((D_IN, D_MODEL)), full((n, D_SHIFT)),
                  full((1, D_SHIFT)), full((16, D_RWKV)), full((D_LORA_PAD, 3 * D_RWKV)),
                  full((QUAD, QUAD)), full((1, 4 * HEAD_DIM)), full((1, D_ATTN)), full((1, D_KV))],
        out_specs=[full((n, D_SHIFT)), full((6, D_RWKV, n)), full((2, n, D_RWKV)), full((n, D_ATTN)),
                   full((n, 2 * D_KV)), full((n // DEC_TILE, 2 * D_KV, DEC_TILE))],
        out_shape=[
            jax.ShapeDtypeStruct((n, D_SHIFT), F32),
            jax.ShapeDtypeStruct((6, D_RWKV, n), F32),
            jax.ShapeDtypeStruct((2, n, D_RWKV), F32),
            jax.ShapeDtypeStruct((n, D_ATTN), F32),
            jax.ShapeDtypeStruct((n, 2 * D_KV), F32),
            jax.ShapeDtypeStruct((n // DEC_TILE, 2 * D_KV, DEC_TILE), F32),
        ],
        compiler_params=pltpu.CompilerParams(
            dimension_semantics=("arbitrary",), vmem_limit_bytes=VMEM_LIMIT),
        name="decode_prep",
    )(x2d, norm_w, w_in_t, shift, mu_pad, prm, wl, bmask, tab, qw, kw)


def _decode_state_stages(vec_ref, s_ref, sout_ref, yt_ref, h):
    a_t, b_t, k_t, w_t, r_t = (vec_ref[i] for i in range(5))
    for i in range(HEAD_DIM):
        s = s_ref[0, i]
        sa = jnp.sum(s * a_t, axis=0, keepdims=True)
        v_i = vec_ref[5, i:i + 1, :]
        s_new = s * w_t + sa * b_t + v_i * k_t
        sout_ref[0, i] = s_new
        yt_ref[pl.ds(h * HEAD_DIM + i, 1), :] = jnp.sum(s_new * r_t, axis=0, keepdims=True)
        if i % 8 == 7:
            yield


def _decode_attn_stages(qn_ref, kvn_ref, col_ref, ck_ref, cv_ref, sink_ref, ya_ref, kout_ref, vout_ref):
    nh = N_Q_HEADS
    seqs = range(DEC_TILE)
    hrow = lax.broadcasted_iota(jnp.int32, (nh, D_ATTN), 0)
    hlane = lax.broadcasted_iota(jnp.int32, (nh, D_ATTN), 1) // HEAD_DIM
    dmask = hrow == hlane
    grow = lax.broadcasted_iota(jnp.int32, (nh, D_KV), 0) // GQA_GROUP
    glane = lax.broadcasted_iota(jnp.int32, (nh, D_KV), 1) // HEAD_DIM
    low = glane == 0
    key_idx = lax.broadcasted_iota(jnp.int32, (nh, WINDOW), 1)
    last = lax.broadcasted_iota(jnp.int32, (D_KV, WINDOW), 1) == WINDOW - 1
    sink = sink_ref[...]
    kvn = kvn_ref[...]
    col = col_ref[0]
    k_new = [kvn[j:j + 1, 0:D_KV] for j in seqs]
    v_new = [kvn[j:j + 1, D_KV:] for j in seqs]
    ck = [ck_ref[j] for j in seqs]
    cv = [cv_ref[j] for j in seqs]
    for j in seqs:
        kout_ref[j] = jnp.where(last, col[0:D_KV, j:j + 1], pltpu.roll(ck[j], WINDOW - 1, 1))
        vout_ref[j] = jnp.where(last, col[D_KV:, j:j + 1], pltpu.roll(cv[j], WINDOW - 1, 1))
    yield
    qn = qn_ref[...]
    swap = (grow != (lax.broadcasted_iota(jnp.int32, (nh, D_KV), 0) & 1))
    qp = []
    for j in seqs:
        own = jnp.where(dmask, jnp.broadcast_to(qn[j:j + 1, :], (nh, D_ATTN)), 0.0)
        fold = own[:, 0:D_KV] + own[:, D_KV:2 * D_KV] + own[:, 2 * D_KV:3 * D_KV] + own[:, 3 * D_KV:]
        qp.append(jnp.where(swap, pltpu.roll(fold, HEAD_DIM, 1), fold))
    s_c = [jnp.where(key_idx >= 1, jnp.dot(qp[j], ck[j], preferred_element_type=F32) * ATTN_SCALE, NEG_INF)
           for j in seqs]
    s_n = [jnp.sum(qp[j] * k_new[j], axis=-1, keepdims=True) * ATTN_SCALE for j in seqs]
    yield
    m = [jnp.maximum(jnp.maximum(jnp.max(s_c[j], axis=-1, keepdims=True), s_n[j]), sink) for j in seqs]
    yield
    e_c = [jnp.exp(s_c[j] - m[j]) for j in seqs]
    e_n = [jnp.exp(s_n[j] - m[j]) for j in seqs]
    denom = [jnp.sum(e_c[j], axis=-1, keepdims=True) + e_n[j] + jnp.exp(sink - m[j]) for j in seqs]
    yield
    o = [(_dot_nt_f32(e_c[j], cv[j]) + e_n[j] * v_new[j]) / denom[j]
         for j in seqs]
    yield
    out_rows = []
    for j in seqs:
        rot = pltpu.roll(o[j], HEAD_DIM, 1)
        g0 = jnp.where(low, o[j], rot)
        g1 = jnp.where(low, rot, o[j])
        wide = jnp.concatenate([g0, g0, g1, g1], axis=1)
        out_rows.append(jnp.sum(jnp.where(dmask, wide, 0.0), axis=0, keepdims=True))
    ya_ref[...] = jnp.concatenate(out_rows, axis=0)


def _decode_step_kernel(vec_ref, gb_ref, prm_ref, ones_ref, s_ref, qn_ref, kvn_ref, col_ref, ck_ref, cv_ref,
                        sink_ref, sout_ref, yr_ref, ya_ref, kout_ref, vout_ref, yt_ref):
    i = pl.program_id(0)
    _interleave(
        _decode_attn_stages(qn_ref, kvn_ref, col_ref, ck_ref, cv_ref, sink_ref, ya_ref, kout_ref, vout_ref),
        _decode_state_stages(vec_ref, s_ref, sout_ref, yt_ref, i))

    @pl.when(i == pl.num_programs(0) - 1)
    def _():
        y = jnp.transpose(yt_ref[...])
        yr_ref[...] = _rwkv_finish(y, gb_ref[0], gb_ref[1], prm_ref[...], ones_ref)


def _decode_step(vec_t, gb, prm, bmask, s_t, qn, kvn, cols, ck_t, cv_t, sinks_col):
    n = s_t.shape[-1]
    bt = DEC_TILE
    assert n // bt == H_RWKV, "one sequence tile per RWKV head"
    const = lambda shape: pl.BlockSpec(shape, lambda i: (0,) * len(shape))
    return pl.pallas_call(
        _decode_step_kernel,
        grid=(H_RWKV,),
        in_specs=[
            pl.BlockSpec((6, HEAD_DIM, n), lambda i: (0, i, 0)),
            const((2, n, D_RWKV)),
            const((16, D_RWKV)),
            const((QUAD, QUAD)),
            pl.BlockSpec((1, HEAD_DIM, HEAD_DIM, n), lambda i: (i, 0, 0, 0)),
            pl.BlockSpec((bt, D_ATTN), lambda i: (i, 0)),
            pl.BlockSpec((bt, 2 * D_KV), lambda i: (i, 0)),
            pl.BlockSpec((1, 2 * D_KV, bt), lambda i: (i, 0, 0)),
            pl.BlockSpec((bt, D_KV, WINDOW), lambda i: (i, 0, 0)),
            pl.BlockSpec((bt, D_KV, WINDOW), lambda i: (i, 0, 0)),
            const((N_Q_HEADS, 1)),
        ],
        out_specs=[
            pl.BlockSpec((1, HEAD_DIM, HEAD_DIM, n), lambda i: (i, 0, 0, 0)),
            const((n, D_RWKV)),
            pl.BlockSpec((bt, D_ATTN), lambda i: (i, 0)),
            pl.BlockSpec((bt, D_KV, WINDOW), lambda i: (i, 0, 0)),
            pl.BlockSpec((bt, D_KV, WINDOW), lambda i: (i, 0, 0)),
        ],
        out_shape=[
            jax.ShapeDtypeStruct((H_RWKV, HEAD_DIM, HEAD_DIM, n), F32),
            jax.ShapeDtypeStruct((n, D_RWKV), F32),
            jax.ShapeDtypeStruct((n, D_ATTN), F32),
            jax.ShapeDtypeStruct((n, D_KV, WINDOW), F32),
            jax.ShapeDtypeStruct((n, D_KV, WINDOW), F32),
        ],
        scratch_shapes=[pltpu.VMEM((D_RWKV, n), F32)],
        compiler_params=pltpu.CompilerParams(
            dimension_semantics=("arbitrary",), vmem_limit_bytes=VMEM_LIMIT),
        name="decode_step",
    )(vec_t, gb, prm, bmask, s_t, qn, kvn, cols, ck_t, cv_t, sinks_col)


def kernel(x_prompt, x_sample, state_wkv, state_shift, cache_k_win, cache_v_win, norm_mix_w, w_in, mu_shift, w0, w_decay_up, a0, w_a_up, w_g_up, k_k, k_a, r_k, ln_x_w, ln_x_b, q_norm_w, k_norm_w, sinks, w_out, norm_ffn_w, w_ffn_up, w_ffn_down):
    bsz, t, _ = x_prompt.shape
    nd = x_sample.shape[0]
    l = 0

    w_in_pad = jnp.swapaxes(w_in[l], 0, 1)
    mu_pad = mu_shift[l][None, :]
    wl = jnp.zeros((D_LORA_PAD, 3 * D_RWKV), F32)
    wl = wl.at[0:32, 0:D_RWKV].set(w_decay_up[l])
    wl = wl.at[32:64, D_RWKV:2 * D_RWKV].set(w_a_up[l])
    wl = wl.at[64:160, 2 * D_RWKV:].set(w_g_up[l])
    wl = wl.astype(BF16)
    prm = jnp.zeros((16, D_RWKV), F32)
    prm = prm.at[0].set(w0[l]).at[1].set(a0[l]).at[2].set(k_k[l]).at[3].set(k_a[l])
    prm = prm.at[4].set(r_k[l].reshape(-1)).at[5].set(ln_x_w[l]).at[6].set(ln_x_b[l])
    hid = np.arange(QUAD) // HEAD_DIM
    bmask = jnp.asarray(hid[:, None] == hid[None, :], BF16)
    tri = jnp.asarray(np.tril(np.ones((CHUNK, CHUNK))), BF16)
    qw = jnp.tile(q_norm_w[l][None, :], (1, N_Q_HEADS))
    kw = jnp.tile(k_norm_w[l][None, :], (1, N_KV_HEADS))
    nmw = norm_mix_w[l][None, :]
    nfw = norm_ffn_w[l][None, :]
    tab_a, tab_b = _rope_block_tables(max(t, PAST_LEN + 1) // WINDOW + 1)
    ta, tb = tab_a[PAST_LEN // WINDOW], tab_b[PAST_LEN % WINDOW][None, :]
    tab_s = jnp.concatenate([ta[:, :128] * tb[:, :128] - ta[:, 128:] * tb[:, 128:],
                             ta[:, 128:] * tb[:, :128] + ta[:, :128] * tb[:, 128:]], axis=1)

    xs = x_sample.reshape(nd, D_MODEL)
    shift_in = state_shift[l].reshape(nd, D_SHIFT)
    p_s, vec_t, gb, qn_s, kvn_s, cols = _decode_prep(xs, nmw, w_in_pad, shift_in, mu_pad, prm, wl, bmask, tab_s, qw, kw)
    s_t = jnp.transpose(state_wkv[l], (1, 2, 3, 0))
    ck_t = jnp.swapaxes(cache_k_win[l].reshape(nd, WINDOW, D_KV), 1, 2)
    cv_t = jnp.swapaxes(cache_v_win[l].reshape(nd, WINDOW, D_KV), 1, 2)
    wkv_t, yr_s, ya_s, kc_t, vc_t = _decode_step(vec_t, gb, prm, bmask, s_t, qn_s, kvn_s, cols, ck_t, cv_t,
                                                 sinks[l][:, None])

    xp = x_prompt.reshape(bsz * t, D_MODEL)
    xs_p, plast, q_p, kv_p = _inproj_shift(xp, nmw, w_in_pad, mu_pad, t, 1024)
    yr_p, hbd, wo, wu, wd = _rwkv_prompt(xs_p.reshape(bsz, t, D_SHIFT), prm, wl, tri, bmask,
                                         (w_out[l], w_ffn_up[l], w_ffn_down[l]))
    y_prompt, y_s, kwin_p, vwin_p = _attn_ffn(q_p.reshape(bsz, t, D_ATTN), kv_p.reshape(bsz, t, 2 * D_KV),
                                              tab_a, tab_b, qw, kw, sinks[l][None, :], bmask, _band_bias(),
                                              x_prompt, yr_p, xs, yr_s, ya_s, wo, nfw, wu, wd)
    hb = hbd.reshape(bsz, 2, 4, HEAD_DIM, 2, HEAD_DIM)
    wkv_prompt = jnp.stack([hb[:, :, j, :, j % 2, :] for j in range(4)], axis=2)
    wkv_prompt = wkv_prompt.reshape(bsz, H_RWKV, HEAD_DIM, HEAD_DIM)[None]
    shift_prompt = plast[None]
    k_win_prompt = jnp.swapaxes(kwin_p, 1, 2).reshape(bsz, WINDOW, N_KV_HEADS, HEAD_DIM)[None]
    v_win_prompt = jnp.swapaxes(vwin_p, 1, 2).reshape(bsz, WINDOW, N_KV_HEADS, HEAD_DIM)[None]

    y_sample = y_s.reshape(nd, 1, D_MODEL)
    wkv_sample = jnp.transpose(wkv_t, (3, 0, 1, 2))[None]
    shift_sample = p_s.reshape(nd, 1, D_SHIFT)[None]
    k_win_sample = jnp.swapaxes(kc_t, 1, 2).reshape(nd, WINDOW, N_KV_HEADS, HEAD_DIM)[None]
    v_win_sample = jnp.swapaxes(vc_t, 1, 2).reshape(nd, WINDOW, N_KV_HEADS, HEAD_DIM)[None]

    return (y_prompt, y_sample, wkv_prompt, shift_prompt, k_win_prompt, v_win_prompt,
            wkv_sample, shift_sample, k_win_sample, v_win_sample)
```

```python
import functools

import jax
import jax.numpy as jnp
import numpy as np
from jax import lax
from jax.experimental import pallas as pl
from jax.experimental.pallas import tpu as pltpu

F32 = jnp.float32
BF16 = jnp.bfloat16

D_MODEL = 1024
D_RWKV = 512
D_ATTN = 512
HEAD_DIM = 64
H_RWKV = 8
N_Q_HEADS = 8
N_KV_HEADS = 2
GQA_GROUP = 4
D_KV = 128
D_LORA = 160
D_LORA_PAD = 256
D_SHIFT = 3 * D_RWKV + D_LORA
D_IN = D_SHIFT + D_ATTN + 2 * D_KV
WINDOW = 128
ROPE_DIM = 16
ROPE_HALF = 8
ROPE_THETA = 500000.0
ATTN_SCALE = HEAD_DIM ** -0.5
D_FF = 4096
RMS_EPS = 1e-6
LNX_EPS = 64e-5
NEG_INF = -1e30
LOG2E = 1.4426950408889634
PAST_LEN = 16384

CHUNK = 64
QUAD = 4 * HEAD_DIM
V7X_VMEM_BYTES = 64 * 1024 * 1024
VMEM_LIMIT = V7X_VMEM_BYTES // 8 * 7


def _split2(x):
    hi = x.astype(BF16)
    lo = (x - hi.astype(F32)).astype(BF16)
    return hi, lo


def _head_sums(xs, ones_ref):
    n, w = xs[0].shape
    tile = min(w, QUAD)
    per = w // tile
    pieces = [x[:, j * tile:(j + 1) * tile] for x in xs for j in range(per)]
    stacked = jnp.concatenate(pieces, axis=0) if len(pieces) > 1 else pieces[0]
    ones = ones_ref[0:tile, 0:tile]
    out = jnp.dot(stacked.astype(BF16), ones, preferred_element_type=F32)
    res = []
    for i in range(len(xs)):
        cols = [out[(i * per + j) * n:(i * per + j + 1) * n] for j in range(per)]
        res.append(jnp.concatenate(cols, axis=1) if per > 1 else cols[0])
    return res


def _cumsum_rows(tri_bf16, x):
    hi, lo = _split2(x)
    return (jnp.dot(tri_bf16, hi, preferred_element_type=F32)
            + jnp.dot(tri_bf16, lo, preferred_element_type=F32))


def _mm(a, b):
    return jnp.dot(a.astype(BF16), b.astype(BF16), preferred_element_type=F32)


def _mm_nt(a, b):
    return lax.dot_general(a.astype(BF16), b.astype(BF16), (((1,), (1,)), ((), ())),
                           preferred_element_type=F32)


def _mm_tn(a, b):
    return lax.dot_general(a.astype(BF16), b.astype(BF16), (((0,), (0,)), ((), ())),
                           preferred_element_type=F32)


def _dot_nt_f32(a, b):
    return lax.dot_general(a, b, (((1,), (1,)), ((), ())), preferred_element_type=F32)


def _sigmoid(x):
    return 1.0 / (1.0 + jnp.exp(-x))


def _interleave(*gens):
    live = list(gens)
    while live:
        for g in list(live):
            try:
                next(g)
            except StopIteration:
                live.remove(g)


def _norm_project(x, nw_ref, wt_ref):
    ms = jnp.mean(x * x, axis=-1, keepdims=True)
    h = ((x * lax.rsqrt(ms + RMS_EPS)) * nw_ref[...]).astype(BF16)
    return _mm_nt(h, wt_ref[0:D_SHIFT, :]), _mm_nt(h, wt_ref[D_SHIFT:, :])


def _inproj_shift_kernel(tiles_per_seq, x_ref, nw_ref, w_ref, mu_ref, xs_ref, last_ref, q_ref, kv_ref, prev_ref):
    i = pl.program_id(0)

    @pl.when(i % tiles_per_seq == 0)
    def _():
        prev_ref[...] = jnp.zeros_like(prev_ref)

    p, qkv = _norm_project(x_ref[...], nw_ref, w_ref)
    tm = p.shape[0]
    row = lax.broadcasted_iota(jnp.int32, p.shape, 0)
    prev = jnp.where(row == 0, jnp.broadcast_to(prev_ref[0:1, :], p.shape), pltpu.roll(p, 1, 0))
    xs_ref[...] = p + (prev - p) * mu_ref[...]
    last = jnp.broadcast_to(p[tm - 1:tm, :], prev_ref.shape)
    prev_ref[...] = last
    last_ref[0] = p[tm - 1:tm, :]
    q_ref[...] = qkv[:, :D_ATTN]
    kv_ref[...] = qkv[:, D_ATTN:]


def _inproj_shift(x2d, norm_w, w_in_pad, mu_pad, seq_len, tm):
    m = x2d.shape[0]
    tiles_per_seq = seq_len // tm
    return pl.pallas_call(
        functools.partial(_inproj_shift_kernel, tiles_per_seq),
        grid=(m // tm,),
        in_specs=[
            pl.BlockSpec((tm, D_MODEL), lambda i: (i, 0)),
            pl.BlockSpec((1, D_MODEL), lambda i: (0, 0)),
            pl.BlockSpec((D_IN, D_MODEL), lambda i: (0, 0)),
            pl.BlockSpec((1, D_SHIFT), lambda i: (0, 0)),
        ],
        out_specs=[
            pl.BlockSpec((tm, D_SHIFT), lambda i: (i, 0)),
            pl.BlockSpec((1, 1, D_SHIFT), lambda i: (i // tiles_per_seq, 0, 0)),
            pl.BlockSpec((tm, D_ATTN), lambda i: (i, 0)),
            pl.BlockSpec((tm, 2 * D_KV), lambda i: (i, 0)),
        ],
        out_shape=[
            jax.ShapeDtypeStruct((m, D_SHIFT), F32),
            jax.ShapeDtypeStruct((m // seq_len, 1, D_SHIFT), F32),
            jax.ShapeDtypeStruct((m, D_ATTN), F32),
            jax.ShapeDtypeStruct((m, 2 * D_KV), F32),
        ],
        scratch_shapes=[pltpu.VMEM((8, D_SHIFT), F32)],
        compiler_params=pltpu.CompilerParams(
            dimension_semantics=("arbitrary",), vmem_limit_bytes=VMEM_LIMIT),
        name="inproj_shift",
    )(x2d, norm_w, w_in_pad, mu_pad)


def _rwkv_features(xs, prm, wl_ref, ones_ref):
    r = xs[:, 0:D_RWKV]
    k = xs[:, D_RWKV:2 * D_RWKV]
    v = xs[:, 2 * D_RWKV:3 * D_RWKV]
    lora = xs[:, 3 * D_RWKV:]
    lora = jnp.concatenate([lora, jnp.zeros((lora.shape[0], wl_ref.shape[0] - D_LORA), F32)], axis=1)
    col = lax.broadcasted_iota(jnp.int32, lora.shape, 1)
    act = jnp.where(col < 32, jnp.tanh(lora), jnp.where(col < 64, lora, _sigmoid(lora)))
    up = jnp.dot(act.astype(BF16), wl_ref[...], preferred_element_type=F32)
    w0, a0, k_k, k_a, r_k = prm[0:1], prm[1:2], prm[2:3], prm[3:4], prm[4:5]
    logw = (-np.exp(-0.5)) * _sigmoid(w0 + up[:, 0:D_RWKV])
    asig = _sigmoid(a0 + up[:, D_RWKV:2 * D_RWKV])
    g = up[:, 2 * D_RWKV:]
    kk = k * k_k
    k_mod = k * (1.0 + (asig - 1.0) * k_a)
    ss, rk = _head_sums([kk * kk, r * k_mod * r_k], ones_ref)
    kk = kk / jnp.maximum(jnp.sqrt(ss), 1e-12)
    k = k_mod
    bonus = rk * v
    return r, logw, k, v, -kk, kk * asig, g, bonus


def _rwkv_finish(y, g, bonus, prm, ones_ref):
    ln_w, ln_b = prm[5:6], prm[6:7]
    mean = _head_sums([y], ones_ref)[0] * (1.0 / HEAD_DIM)
    d = y - mean
    var = _head_sums([d * d], ones_ref)[0] * (1.0 / HEAD_DIM)
    yn = d * lax.rsqrt(var + LNX_EPS) * ln_w + ln_b
    return (yn + bonus) * g


def _block_diag(x, bmask):
    return jnp.concatenate([x] * 4, axis=0) * bmask


def _chunk_prep(r, logw, k, v, a, b, tri_ref, bmask):
    n = len(r)
    ch = range(n)
    tri = tri_ref[...]
    cum = [_cumsum_rows(tri, logw[i]) for i in ch]
    yield
    e_in = [jnp.exp(cum[i]) for i in ch]
    e_ex = [jnp.exp(cum[i] - logw[i]) for i in ch]
    e_inv = [1.0 / e_in[i] for i in ch]
    e_last = [e_in[i][CHUNK - 1:CHUNK, :] for i in ch]
    rt = [(r[i] * e_in[i]).astype(BF16) for i in ch]
    at = [(a[i] * e_ex[i]).astype(BF16) for i in ch]
    kt = [(k[i] * e_inv[i]).astype(BF16) for i in ch]
    bt = [(b[i] * e_inv[i]).astype(BF16) for i in ch]
    vb = [v[i].astype(BF16) for i in ch]
    yield

    t_idx = lax.broadcasted_iota(jnp.int32, (CHUNK, QUAD), 0)
    s_idx = lax.broadcasted_iota(jnp.int32, (CHUNK, QUAD), 1) & (HEAD_DIM - 1)
    strict = s_idx < t_idx
    incl = s_idx <= t_idx

    gm = [_mm_nt(jnp.concatenate([at[i], rt[i]], axis=0),
                 jnp.concatenate([_block_diag(bt[i], bmask), _block_diag(kt[i], bmask)], axis=0))
          for i in ch]
    yield
    a_ab = [jnp.where(strict, gm[i][:CHUNK, :QUAD], 0.0) for i in ch]
    a_ak = [jnp.where(strict, gm[i][:CHUNK, QUAD:], 0.0) for i in ch]
    a_rb = [jnp.where(incl, gm[i][CHUNK:, :QUAD], 0.0) for i in ch]
    a_rk = [jnp.where(incl, gm[i][CHUNK:, QUAD:], 0.0) for i in ch]

    eye = jnp.where(s_idx == t_idx, 1.0, 0.0)
    pwb = [a_ab[i].astype(BF16) for i in ch]
    t_inv = [eye + a_ab[i] for i in ch]
    for it in range(6):
        rbd = [_block_diag(pwb[i], bmask) for i in ch]
        if it == 0:
            pwb = [_mm(pwb[i], rbd[i]).astype(BF16) for i in ch]
        elif it < 5:
            out = [_mm(jnp.concatenate([pwb[i], t_inv[i].astype(BF16)], axis=0), rbd[i]) for i in ch]
            pwb = [out[i][:CHUNK].astype(BF16) for i in ch]
            t_inv = [t_inv[i] + out[i][CHUNK:] for i in ch]
        else:
            t_inv = [t_inv[i] + _mm(t_inv[i], rbd[i]) for i in ch]
        yield

    vbd = [_block_diag(vb[i], bmask) for i in ch]
    xy0 = [_mm(jnp.concatenate([a_ak[i], a_rk[i]], axis=0), vbd[i]) for i in ch]
    return [dict(ar=jnp.concatenate([at[i], rt[i]], axis=0), x0=xy0[i][:CHUNK], y0=xy0[i][CHUNK:],
                 t_inv=t_inv[i].astype(BF16), a_rb=a_rb[i].astype(BF16), vb=vb[i],
                 bk=jnp.concatenate([bt[i], kt[i]], axis=0), e_last=e_last[i]) for i in ch]


def _chunk_step(pre, state, bmask, out):
    ch = range(len(pre))
    half = QUAD // 2
    zeros = jnp.zeros((half, half), BF16)
    sc = [state[i].astype(BF16) for i in ch]
    sb = [jnp.concatenate([jnp.concatenate([sc[i][:half], zeros], axis=1),
                           jnp.concatenate([zeros, sc[i][half:]], axis=1)], axis=0) for i in ch]
    xr = [_mm_nt(pre[i]["ar"], sb[i]) for i in ch]
    yield
    x = [xr[i][:CHUNK] + pre[i]["x0"] for i in ch]
    u = [_mm(pre[i]["t_inv"], _block_diag(x[i].astype(BF16), bmask)) for i in ch]
    yield
    ub = [u[i].astype(BF16) for i in ch]
    out["y"] = [xr[i][CHUNK:] + pre[i]["y0"] + _mm(pre[i]["a_rb"], _block_diag(ub[i], bmask)) for i in ch]
    upd = [_mm_tn(jnp.concatenate([ub[i], pre[i]["vb"]], axis=0), pre[i]["bk"]) for i in ch]
    yield
    bm = bmask[:half, :half].astype(F32)
    s_new = []
    for i in ch:
        e_last = pre[i]["e_last"]
        top = (state[i][:half] + upd[i][:half, :half] * bm) * e_last[:, :half]
        bot = (state[i][half:] + upd[i][half:, half:] * bm) * e_last[:, half:]
        s_new.append(jnp.concatenate([top, bot], axis=0))
    out["state"] = s_new


def _finish_stages(ys, g, bonus, prm, ones_ref, y_ref, j):
    nseq = len(ys) // 2
    ln_w, ln_b = prm[5:6], prm[6:7]
    y = jnp.concatenate([jnp.concatenate(ys[2 * s:2 * s + 2], axis=1) for s in range(nseq)], axis=0)
    mean = _head_sums([y], ones_ref)[0] * (1.0 / HEAD_DIM)
    yield
    d = y - mean
    var = _head_sums([d * d], ones_ref)[0] * (1.0 / HEAD_DIM)
    yield
    out = (d * lax.rsqrt(var + LNX_EPS) * ln_w + ln_b + bonus) * g
    for s in range(nseq):
        y_ref[s, j * CHUNK:(j + 1) * CHUNK, :] = out[s * CHUNK:(s + 1) * CHUNK]


def _round_side(side_in, side_out):
    for src, dst in zip(side_in, side_out):
        half = src.shape[0] // 2
        for rows in (slice(0, half), slice(half, 2 * half)):
            dst[rows, :] = src[rows, :].astype(BF16)
            yield


def _rwkv_prompt_kernel(nside, xs_ref, prm_ref, wl_ref, tri_ref, bmask_ref, *refs):
    side_in, (y_ref, hout_ref) = refs[:nside], refs[nside:nside + 2]
    side_out, h_ref = refs[nside + 2:2 * nside + 2], refs[2 * nside + 2]
    c = pl.program_id(0)
    nseq, tstep, _ = xs_ref.shape
    nchunk = tstep // CHUNK

    @pl.when(c == 0)
    def _():
        h_ref[...] = jnp.zeros_like(h_ref)

    xs = jnp.concatenate([xs_ref[s] for s in range(nseq)], axis=0)
    prm = prm_ref[...]
    r, logw, k, v, a, b, g, bonus = _rwkv_features(xs, prm, wl_ref, bmask_ref)
    bmask = bmask_ref[...]
    lanes = [(s, q) for s in range(nseq) for q in range(2)]
    pre = {}

    def prep(chunks):
        chains = [(j, s, q) for j in chunks for s, q in lanes]
        cut = lambda x: [x[s * tstep + j * CHUNK:s * tstep + (j + 1) * CHUNK, q * QUAD:(q + 1) * QUAD]
                         for j, s, q in chains]
        res = yield from _chunk_prep(cut(r), cut(logw), cut(k), cut(v), cut(a), cut(b), tri_ref, bmask)
        for n, j in enumerate(chunks):
            pre[j] = res[n * len(lanes):(n + 1) * len(lanes)]

    nfirst = max(nchunk - 2, 1)
    _interleave(prep(range(nfirst)))
    later = prep(range(nfirst, nchunk))
    state = [h_ref[s, q] for s, q in lanes]
    crow = lambda x, j: jnp.concatenate([x[s * tstep + j * CHUNK:s * tstep + (j + 1) * CHUNK]
                                         for s in range(nseq)], axis=0)
    finish = iter(())
    rounding = _round_side(side_in, side_out)
    for j in range(nchunk):
        res = {}
        for _ in _chunk_step(pre[j], state, bmask, res):
            next(later, None)
            next(finish, None)
            next(rounding, None)
        _interleave(finish)
        if j == nfirst - 1:
            _interleave(later)
        state = res["state"]
        finish = _finish_stages(res["y"], crow(g, j), crow(bonus, j), prm, bmask_ref, y_ref, j)
    _interleave(finish, rounding)
    for i, (s, q) in enumerate(lanes):
        h_ref[s, q] = state[i]

    @pl.when(c == pl.num_programs(0) - 1)
    def _():
        hout_ref[...] = h_ref[...]


RWKV_CHUNKS_PER_STEP = 8


def _side_specs(side, nsteps):
    specs = []
    for w in side:
        rows, rem = divmod(w.shape[0], nsteps)
        assert rem == 0 and rows % 32 == 0, "two halves of whole bf16 row tiles per step"
        specs.append(pl.BlockSpec((rows, w.shape[1]), lambda c: (c, 0)))
    return specs


def _rwkv_prompt(xs3d, prm, wl, tri, bmask, side):
    bsz, t, _ = xs3d.shape
    tstep = RWKV_CHUNKS_PER_STEP * CHUNK
    nc = t // tstep
    const = lambda shape: pl.BlockSpec(shape, lambda c: (0,) * len(shape))
    state_shape = (bsz, 2, QUAD, QUAD // 2)
    side_specs = _side_specs(side, nc)
    return pl.pallas_call(
        functools.partial(_rwkv_prompt_kernel, len(side)),
        grid=(nc,),
        in_specs=[
            pl.BlockSpec((bsz, tstep, D_SHIFT), lambda c: (0, c, 0)),
            const((16, D_RWKV)),
            const((D_LORA_PAD, 3 * D_RWKV)),
            const((CHUNK, CHUNK)),
            const((QUAD, QUAD)),
        ] + side_specs,
        out_specs=[
            pl.BlockSpec((bsz, tstep, D_RWKV), lambda c: (0, c, 0)),
            const(state_shape),
        ] + side_specs,
        out_shape=[
            jax.ShapeDtypeStruct((bsz, t, D_RWKV), F32),
            jax.ShapeDtypeStruct(state_shape, F32),
        ] + [jax.ShapeDtypeStruct(w.shape, BF16) for w in side],
        scratch_shapes=[pltpu.VMEM(state_shape, F32)],
        compiler_params=pltpu.CompilerParams(
            dimension_semantics=("arbitrary",), vmem_limit_bytes=VMEM_LIMIT),
        name="rwkv_prompt",
    )(xs3d, prm, wl, tri, bmask, *side)


def _rope_lane_freq():
    pair = jnp.asarray((np.arange(2 * HEAD_DIM) % HEAD_DIM) % ROPE_HALF, F32)
    return jnp.power(ROPE_THETA, -pair * (2.0 / ROPE_DIM))[None, :]


def _rope_tables(cos, sin):
    dim = lax.broadcasted_iota(jnp.int32, cos.shape, 1) & (HEAD_DIM - 1)
    cos_t = jnp.where(dim < ROPE_DIM, cos, 1.0)
    sin_lo = jnp.where(dim < ROPE_HALF, -sin, 0.0)
    sin_hi = jnp.where((dim >= ROPE_HALF) & (dim < ROPE_DIM), sin, 0.0)
    return cos_t, sin_lo, sin_hi


def _qk_norm_rope(x, norm_w, cos_t, sin_lo, sin_hi, ones_ref):
    ms = _head_sums([x * x], ones_ref)[0] * (1.0 / HEAD_DIM)
    xn = x * lax.rsqrt(ms + RMS_EPS) * norm_w
    width = x.shape[1]
    fwd = pltpu.roll(xn, width - ROPE_HALF, 1)
    bwd = pltpu.roll(xn, ROPE_HALF, 1)
    return xn * cos_t + fwd * sin_lo + bwd * sin_hi


def _tile_lanes(x, reps):
    return jnp.concatenate([x] * reps, axis=1) if reps > 1 else x


def _attn_stages(q_ref, kv_ref, taba_ref, tabb_ref, qw_ref, kw_ref, sink_ref, ones_ref, bias_ref, first_bias,
                 kprev_ref, vprev_ref, out):
    nseq, tq, _ = q_ref.shape
    nblk = tq // WINDOW
    units = [(s, b) for s in range(nseq) for b in range(nblk)]
    blk = lambda b: slice(b * WINDOW, (b + 1) * WINDOW)
    tb = tabb_ref[...]
    cos_b, sin_b = tb[:, :128], tb[:, 128:]
    rope, rope4 = [], []
    for b in range(nblk):
        ta = taba_ref[b]
        cos_a, sin_a = ta[:, :128], ta[:, 128:]
        tabs = _rope_tables(cos_a * cos_b - sin_a * sin_b, sin_a * cos_b + cos_a * sin_b)
        rope.append(tabs)
        rope4.append([_tile_lanes(x, 4) for x in tabs])
    q = {(s, b): _qk_norm_rope(q_ref[s, blk(b), :], qw_ref[...], *rope4[b], ones_ref) * (ATTN_SCALE * LOG2E)
         for s, b in units}
    kv = {(s, b): kv_ref[s, blk(b), :] for s, b in units}
    k_cur = {u: _qk_norm_rope(kv[u][:, 0:D_KV], kw_ref[...], *rope[u[1]], ones_ref) for u in units}
    v_cur = {u: kv[u][:, D_KV:] for u in units}
    k_all = {(s, b): jnp.concatenate([kprev_ref[s] if b == 0 else k_cur[(s, b - 1)], k_cur[(s, b)]], axis=0)
             for s, b in units}
    v_all = {(s, b): jnp.concatenate([vprev_ref[s] if b == 0 else v_cur[(s, b - 1)], v_cur[(s, b)]], axis=0)
             for s, b in units}
    out["k_cur"] = [k_cur[(s, nblk - 1)] for s in range(nseq)]
    out["v_cur"] = [v_cur[(s, nblk - 1)] for s in range(nseq)]
    for s in range(nseq):
        kprev_ref[s] = out["k_cur"][s]
        vprev_ref[s] = out["v_cur"][s]
    yield

    nk = 2 * WINDOW
    bias = [bias_ref[first_bias] if b == 0 else bias_ref[1] for b in range(nblk)]
    sinks = sink_ref[...] * LOG2E
    low = lax.broadcasted_iota(jnp.int32, (nk, D_KV), 1) < HEAD_DIM
    lane_blk = [ones_ref[j * HEAD_DIM:j * HEAD_DIM + 1, :] for j in range(GQA_GROUP)]

    chains = [(u, g) for u in units for g in range(N_KV_HEADS)]
    ch = range(len(chains))
    k_rot = {u: pltpu.roll(k_all[u], HEAD_DIM, 1) for u in units}
    k2 = [jnp.where(low, k_all[u], k_rot[u]) if g == 0 else jnp.where(low, k_rot[u], k_all[u])
          for u, g in chains]
    k4 = [jnp.concatenate([k2[c], k2[c]], axis=1).astype(BF16) for c in ch]
    vb = {u: v_all[u].astype(BF16) for u in units}
    qg = [q[u][:, g * QUAD:(g + 1) * QUAD].astype(BF16) for u, g in chains]
    qstack = [jnp.concatenate([qg[c] * lane_blk[j] for j in range(GQA_GROUP)], axis=0) for c in ch]
    sink_row = [jnp.concatenate(
        [jnp.broadcast_to(sinks[:, g * GQA_GROUP + j:g * GQA_GROUP + j + 1], (1, WINDOW))
         for j in range(GQA_GROUP)], axis=1) for u, g in chains]
    yield
    sc = [_mm_nt(k4[c], qstack[c]) + bias[u[1]] for c, (u, g) in enumerate(chains)]
    yield
    m = [jnp.maximum(jnp.max(sc[c], axis=0, keepdims=True), sink_row[c]) for c in ch]
    e = [jnp.exp2(sc[c] - m[c]) for c in ch]
    yield
    denom = [jnp.sum(e[c], axis=0, keepdims=True) + jnp.exp2(sink_row[c] - m[c]) for c in ch]
    ot = {(u, g): _mm_tn(vb[u], e[c].astype(BF16))[g * HEAD_DIM:(g + 1) * HEAD_DIM, :] * (1.0 / denom[c])
          for c, (u, g) in enumerate(chains)}
    yield
    ya = []
    for s in range(nseq):
        blocks = []
        for b in range(nblk):
            yt = jnp.concatenate([ot[((s, b), g)][:, j * WINDOW:(j + 1) * WINDOW]
                                  for g in range(N_KV_HEADS) for j in range(GQA_GROUP)], axis=0)
            blocks.append(jnp.transpose(yt))
        ya.append(jnp.concatenate(blocks, axis=0) if nblk > 1 else blocks[0])
    out["ya"] = ya


def _ffn_stages(x, yr, ya, wo_ref, nw_ref, wu_ref, wd_ref, out, pieces=4):
    mix = jnp.concatenate([yr, ya], axis=1).astype(BF16)
    x1 = x + _mm(mix, wo_ref[...])
    yield
    ms = jnp.mean(x1 * x1, axis=-1, keepdims=True)
    hf = ((x1 * lax.rsqrt(ms + RMS_EPS)) * nw_ref[...]).astype(BF16)
    acc = x1
    step = D_FF // pieces
    for j in range(pieces):
        up = _mm(hf, wu_ref[:, j * step:(j + 1) * step])
        yield
        act = jnp.square(jnp.maximum(up, 0.0)).astype(BF16)
        acc = acc + _mm(act, wd_ref[j * step:(j + 1) * step, :])
        yield
    out["y"] = acc


def _attn_ffn_kernel(q_ref, kv_ref, taba_ref, tabb_ref, qw_ref, kw_ref, sink_ref, ones_ref, bias_ref,
                     x_ref, yr_ref, xd_ref, yrd_ref, yad_ref, wo_ref, nw_ref, wu_ref, wd_ref,
                     o_ref, od_ref, kwin_ref, vwin_ref, kprev_ref, vprev_ref, ya_ref):
    i = pl.program_id(0)
    nseq, tq, _ = q_ref.shape
    seqs = range(nseq)
    first = i == 0

    @pl.when(first)
    def _():
        kprev_ref[...] = jnp.zeros_like(kprev_ref)
        vprev_ref[...] = jnp.zeros_like(vprev_ref)
        ya_ref[...] = jnp.zeros_like(ya_ref)

    def rows(ref, dec_ref):
        tile = jnp.concatenate([ref[s] for s in seqs], axis=0)
        dec = dec_ref[...]
        return jnp.where(first, jnp.concatenate([dec] * (tile.shape[0] // dec.shape[0]), axis=0), tile)

    a_out, f_out = {}, {}
    _interleave(
        _ffn_stages(rows(x_ref, xd_ref), rows(yr_ref, yrd_ref), rows(ya_ref, yad_ref),
                    wo_ref, nw_ref, wu_ref, wd_ref, f_out),
        _attn_stages(q_ref, kv_ref, taba_ref, tabb_ref, qw_ref, kw_ref, sink_ref, ones_ref, bias_ref,
                     jnp.minimum(i, 1), kprev_ref, vprev_ref, a_out))
    for s in seqs:
        o_ref[s] = f_out["y"][s * tq:(s + 1) * tq]
        ya_ref[s] = a_out["ya"][s]

    @pl.when(first)
    def _():
        od_ref[...] = f_out["y"][:od_ref.shape[0]].reshape(od_ref.shape)

    @pl.when(i == pl.num_programs(0) - 2)
    def _():
        for s in seqs:
            kwin_ref[s] = jnp.transpose(a_out["k_cur"][s])
            vwin_ref[s] = jnp.transpose(a_out["v_cur"][s])


def _band_bias():
    ki = np.arange(2 * WINDOW)[:, None]
    qi = (np.arange(GQA_GROUP * WINDOW) % WINDOW + WINDOW)[None, :]
    dq = qi - ki
    band = (dq >= 0) & (dq < WINDOW)
    first = band & (ki >= WINDOW)
    return jnp.asarray(np.where(np.stack([first, band]), 0.0, NEG_INF), F32)


def _rope_block_tables(nb):
    freq = _rope_lane_freq()
    ang_a = (jnp.arange(nb, dtype=F32) * WINDOW)[:, None] * freq
    ang_b = jnp.arange(WINDOW, dtype=F32)[:, None] * freq
    tab_a = jnp.concatenate([jnp.cos(ang_a), jnp.sin(ang_a)], axis=1)
    tab_b = jnp.concatenate([jnp.cos(ang_b), jnp.sin(ang_b)], axis=1)
    return tab_a[:, None, :], tab_b


ATTN_BLOCKS_PER_STEP = 1


def _attn_ffn(q3d, kv3d, tab_a, tab_b, qw, kw, sinks, ones_bd, bias, x3d, yr3d, xd, yrd, yad, wo, nw, wu, wd):
    bsz, t, _ = q3d.shape
    nd = xd.shape[0]
    nblk = ATTN_BLOCKS_PER_STEP
    tq = nblk * WINDOW
    nt = t // tq
    const = lambda shape: pl.BlockSpec(shape, lambda i: (0,) * len(shape))
    single = lambda shape: pl.BlockSpec(shape, lambda i: (0,) * len(shape), pipeline_mode=pl.Buffered(1))
    cur = lambda i: jnp.minimum(i, nt - 1)
    prv = lambda i: jnp.maximum(i - 1, 0)
    return pl.pallas_call(
        _attn_ffn_kernel,
        grid=(nt + 1,),
        in_specs=[
            pl.BlockSpec((bsz, tq, D_ATTN), lambda i: (0, cur(i), 0)),
            pl.BlockSpec((bsz, tq, 2 * D_KV), lambda i: (0, cur(i), 0)),
            pl.BlockSpec((nblk, 1, 4 * HEAD_DIM), lambda i: (cur(i), 0, 0)),
            const((WINDOW, 4 * HEAD_DIM)),
            const((1, D_ATTN)),
            const((1, D_KV)),
            const((1, N_Q_HEADS)),
            const((QUAD, QUAD)),
            const((2, 2 * WINDOW, GQA_GROUP * WINDOW)),
            pl.BlockSpec((bsz, tq, D_MODEL), lambda i: (0, prv(i), 0)),
            pl.BlockSpec((bsz, tq, D_RWKV), lambda i: (0, prv(i), 0)),
            const((nd, D_MODEL)),
            const((nd, D_RWKV)),
            const((nd, D_ATTN)),
            single((D_MODEL, D_MODEL)),
            const((1, D_MODEL)),
            single((D_MODEL, D_FF)),
            single((D_FF, D_MODEL)),
        ],
        out_specs=[
            pl.BlockSpec((bsz, tq, D_MODEL), lambda i: (0, prv(i), 0)),
            const((nd, 1, D_MODEL)),
            const((bsz, WINDOW, D_KV)),
            const((bsz, WINDOW, D_KV)),
        ],
        out_shape=[
            jax.ShapeDtypeStruct((bsz, t, D_MODEL), F32),
            jax.ShapeDtypeStruct((nd, 1, D_MODEL), F32),
            jax.ShapeDtypeStruct((bsz, WINDOW, D_KV), F32),
            jax.ShapeDtypeStruct((bsz, WINDOW, D_KV), F32),
        ],
        scratch_shapes=[
            pltpu.VMEM((bsz, WINDOW, D_KV), F32),
            pltpu.VMEM((bsz, WINDOW, D_KV), F32),
            pltpu.VMEM((bsz, tq, D_ATTN), F32),
        ],
        compiler_params=pltpu.CompilerParams(
            dimension_semantics=("arbitrary",), vmem_limit_bytes=VMEM_LIMIT),
        name="attn_ffn",
    )(q3d, kv3d, tab_a, tab_b, qw, kw, sinks, ones_bd, bias, x3d, yr3d, xd, yrd, yad, wo, nw, wu, wd)


DEC_TILE = 16


def _decode_prep_kernel(x_ref, nw_ref, w_ref, sh_ref, mu_ref, prm_ref, wl_ref, ones_ref, tab_ref,
                        qw_ref, kw_ref, p_ref, vec_ref, vgb_ref, qn_ref, kvn_ref, col_ref):
    p, qkv = _norm_project(x_ref[...], nw_ref, w_ref)
    p_ref[...] = p
    xs = p + (sh_ref[...] - p) * mu_ref[...]
    r, logw, k, v, a, b, g, bonus = _rwkv_features(xs, prm_ref[...], wl_ref, ones_ref)
    for i, x in enumerate((a, b, k, jnp.exp(logw), r, v)):
        vec_ref[i] = jnp.transpose(x)
    vgb_ref[0] = g
    vgb_ref[1] = bonus
    n = p.shape[0]
    tab = jnp.broadcast_to(tab_ref[...], (n, 4 * HEAD_DIM))
    cos_t, sin_lo, sin_hi = _rope_tables(tab[:, :128], tab[:, 128:])
    qn_ref[...] = _qk_norm_rope(qkv[:, :D_ATTN], qw_ref[...], _tile_lanes(cos_t, 4), _tile_lanes(sin_lo, 4),
                                _tile_lanes(sin_hi, 4), ones_ref)
    k_new = _qk_norm_rope(qkv[:, D_ATTN:D_ATTN + D_KV], kw_ref[...], cos_t, sin_lo, sin_hi, ones_ref)
    kvn = jnp.concatenate([k_new, qkv[:, D_ATTN + D_KV:]], axis=1)
    kvn_ref[...] = kvn
    kvn_t = jnp.transpose(kvn)
    for i in range(n // DEC_TILE):
        col_ref[i] = kvn_t[:, i * DEC_TILE:(i + 1) * DEC_TILE]


def _decode_prep(x2d, norm_w, w_in_t, shift, mu_pad, prm, wl, bmask, tab, qw, kw):
    n = x2d.shape[0]
    full = lambda shape: pl.BlockSpec(shape, lambda i: (0,) * len(shape))
    return pl.pallas_call(
        _decode_prep_kernel,
        grid=(1,),
        in_specs=[full((n, D_MODEL)), full((1, D_MODEL)), full((D_IN, D_MODEL)), full((n, D_SHIFT)),
                  full((1, D_SHIFT)), full((16, D_RWKV)), full((D_LORA_PAD, 3 * D_RWKV)),
                  full((QUAD, QUAD)), full((1, 4 * HEAD_DIM)), full((1, D_ATTN)), full((1, D_KV))],
        out_specs=[full((n, D_SHIFT)), full((6, D_RWKV, n)), full((2, n, D_RWKV)), full((n, D_ATTN)),
                   full((n, 2 * D_KV)), full((n // DEC_TILE, 2 * D_KV, DEC_TILE))],
        out_shape=[
            jax.ShapeDtypeStruct((n, D_SHIFT), F32),
            jax.ShapeDtypeStruct((6, D_RWKV, n), F32),
            jax.ShapeDtypeStruct((2, n, D_RWKV), F32),
            jax.ShapeDtypeStruct((n, D_ATTN), F32),
            jax.ShapeDtypeStruct((n, 2 * D_KV), F32),
            jax.ShapeDtypeStruct((n // DEC_TILE, 2 * D_KV, DEC_TILE), F32),
        ],
        compiler_params=pltpu.CompilerParams(
            dimension_semantics=("arbitrary",), vmem_limit_bytes=VMEM_LIMIT),
        name="decode_prep",
    )(x2d, norm_w, w_in_t, shift, mu_pad, prm, wl, bmask, tab, qw, kw)


def _decode_state_stages(vec_ref, s_ref, sout_ref, yt_ref, h):
    a_t, b_t, k_t, w_t, r_t = (vec_ref[i] for i in range(5))
    for i in range(HEAD_DIM):
        s = s_ref[0, i]
        sa = jnp.sum(s * a_t, axis=0, keepdims=True)
        v_i = vec_ref[5, i:i + 1, :]
        s_new = s * w_t + sa * b_t + v_i * k_t
        sout_ref[0, i] = s_new
        yt_ref[pl.ds(h * HEAD_DIM + i, 1), :] = jnp.sum(s_new * r_t, axis=0, keepdims=True)
        if i % 8 == 7:
            yield


def _decode_attn_stages(qn_ref, kvn_ref, col_ref, ck_ref, cv_ref, sink_ref, ya_ref, kout_ref, vout_ref):
    nh = N_Q_HEADS
    seqs = range(DEC_TILE)
    hrow = lax.broadcasted_iota(jnp.int32, (nh, D_ATTN), 0)
    hlane = lax.broadcasted_iota(jnp.int32, (nh, D_ATTN), 1) // HEAD_DIM
    dmask = hrow == hlane
    grow = lax.broadcasted_iota(jnp.int32, (nh, D_KV), 0) // GQA_GROUP
    glane = lax.broadcasted_iota(jnp.int32, (nh, D_KV), 1) // HEAD_DIM
    low = glane == 0
    key_idx = lax.broadcasted_iota(jnp.int32, (nh, WINDOW), 1)
    last = lax.broadcasted_iota(jnp.int32, (D_KV, WINDOW), 1) == WINDOW - 1
    sink = sink_ref[...]
    kvn = kvn_ref[...]
    col = col_ref[0]
    k_new = [kvn[j:j + 1, 0:D_KV] for j in seqs]
    v_new = [kvn[j:j + 1, D_KV:] for j in seqs]
    ck = [ck_ref[j] for j in seqs]
    cv = [cv_ref[j] for j in seqs]
    for j in seqs:
        kout_ref[j] = jnp.where(last, col[0:D_KV, j:j + 1], pltpu.roll(ck[j], WINDOW - 1, 1))
        vout_ref[j] = jnp.where(last, col[D_KV:, j:j + 1], pltpu.roll(cv[j], WINDOW - 1, 1))
    yield
    qn = qn_ref[...]
    swap = (grow != (lax.broadcasted_iota(jnp.int32, (nh, D_KV), 0) & 1))
    qp = []
    for j in seqs:
        own = jnp.where(dmask, jnp.broadcast_to(qn[j:j + 1, :], (nh, D_ATTN)), 0.0)
        fold = own[:, 0:D_KV] + own[:, D_KV:2 * D_KV] + own[:, 2 * D_KV:3 * D_KV] + own[:, 3 * D_KV:]
        qp.append(jnp.where(swap, pltpu.roll(fold, HEAD_DIM, 1), fold))
    s_c = [jnp.where(key_idx >= 1, jnp.dot(qp[j], ck[j], preferred_element_type=F32) * ATTN_SCALE, NEG_INF)
           for j in seqs]
    s_n = [jnp.sum(qp[j] * k_new[j], axis=-1, keepdims=True) * ATTN_SCALE for j in seqs]
    yield
    m = [jnp.maximum(jnp.maximum(jnp.max(s_c[j], axis=-1, keepdims=True), s_n[j]), sink) for j in seqs]
    yield
    e_c = [jnp.exp(s_c[j] - m[j]) for j in seqs]
    e_n = [jnp.exp(s_n[j] - m[j]) for j in seqs]
    denom = [jnp.sum(e_c[j], axis=-1, keepdims=True) + e_n[j] + jnp.exp(sink - m[j]) for j in seqs]
    yield
    o = [(_dot_nt_f32(e_c[j], cv[j]) + e_n[j] * v_new[j]) / denom[j]
         for j in seqs]
    yield
    out_rows = []
    for j in seqs:
        rot = pltpu.roll(o[j], HEAD_DIM, 1)
        g0 = jnp.where(low, o[j], rot)
        g1 = jnp.where(low, rot, o[j])
        wide = jnp.concatenate([g0, g0, g1, g1], axis=1)
        out_rows.append(jnp.sum(jnp.where(dmask, wide, 0.0), axis=0, keepdims=True))
    ya_ref[...] = jnp.concatenate(out_rows, axis=0)


def _decode_step_kernel(vec_ref, gb_ref, prm_ref, ones_ref, s_ref, qn_ref, kvn_ref, col_ref, ck_ref, cv_ref,
                        sink_ref, sout_ref, yr_ref, ya_ref, kout_ref, vout_ref, yt_ref):
    i = pl.program_id(0)
    _interleave(
        _decode_attn_stages(qn_ref, kvn_ref, col_ref, ck_ref, cv_ref, sink_ref, ya_ref, kout_ref, vout_ref),
        _decode_state_stages(vec_ref, s_ref, sout_ref, yt_ref, i))

    @pl.when(i == pl.num_programs(0) - 1)
    def _():
        y = jnp.transpose(yt_ref[...])
        yr_ref[...] = _rwkv_finish(y, gb_ref[0], gb_ref[1], prm_ref[...], ones_ref)


def _decode_step(vec_t, gb, prm, bmask, s_t, qn, kvn, cols, ck_t, cv_t, sinks_col):
    n = s_t.shape[-1]
    bt = DEC_TILE
    assert n // bt == H_RWKV, "one sequence tile per RWKV head"
    const = lambda shape: pl.BlockSpec(shape, lambda i: (0,) * len(shape))
    return pl.pallas_call(
        _decode_step_kernel,
        grid=(H_RWKV,),
        in_specs=[
            pl.BlockSpec((6, HEAD_DIM, n), lambda i: (0, i, 0)),
            const((2, n, D_RWKV)),
            const((16, D_RWKV)),
            const((QUAD, QUAD)),
            pl.BlockSpec((1, HEAD_DIM, HEAD_DIM, n), lambda i: (i, 0, 0, 0)),
            pl.BlockSpec((bt, D_ATTN), lambda i: (i, 0)),
            pl.BlockSpec((bt, 2 * D_KV), lambda i: (i, 0)),
            pl.BlockSpec((1, 2 * D_KV, bt), lambda i: (i, 0, 0)),
            pl.BlockSpec((bt, D_KV, WINDOW), lambda i: (i, 0, 0)),
            pl.BlockSpec((bt, D_KV, WINDOW), lambda i: (i, 0, 0)),
            const((N_Q_HEADS, 1)),
        ],
        out_specs=[
            pl.BlockSpec((1, HEAD_DIM, HEAD_DIM, n), lambda i: (i, 0, 0, 0)),
            const((n, D_RWKV)),
            pl.BlockSpec((bt, D_ATTN), lambda i: (i, 0)),
            pl.BlockSpec((bt, D_KV, WINDOW), lambda i: (i, 0, 0)),
            pl.BlockSpec((bt, D_KV, WINDOW), lambda i: (i, 0, 0)),
        ],
        out_shape=[
            jax.ShapeDtypeStruct((H_RWKV, HEAD_DIM, HEAD_DIM, n), F32),
            jax.ShapeDtypeStruct((n, D_RWKV), F32),
            jax.ShapeDtypeStruct((n, D_ATTN), F32),
            jax.ShapeDtypeStruct((n, D_KV, WINDOW), F32),
            jax.ShapeDtypeStruct((n, D_KV, WINDOW), F32),
        ],
        scratch_shapes=[pltpu.VMEM((D_RWKV, n), F32)],
        compiler_params=pltpu.CompilerParams(
            dimension_semantics=("arbitrary",), vmem_limit_bytes=VMEM_LIMIT),
        name="decode_step",
    )(vec_t, gb, prm, bmask, s_t, qn, kvn, cols, ck_t, cv_t, sinks_col)


def kernel(x_prompt, x_sample, state_wkv, state_shift, cache_k_win, cache_v_win, norm_mix_w, w_in, mu_shift, w0, w_decay_up, a0, w_a_up, w_g_up, k_k, k_a, r_k, ln_x_w, ln_x_b, q_norm_w, k_norm_w, sinks, w_out, norm_ffn_w, w_ffn_up, w_ffn_down):
    bsz, t, _ = x_prompt.shape
    nd = x_sample.shape[0]
    l = 0

    w_in_pad = jnp.swapaxes(w_in[l], 0, 1)
    mu_pad = mu_shift[l][None, :]
    wl = jnp.zeros((D_LORA_PAD, 3 * D_RWKV), F32)
    wl = wl.at[0:32, 0:D_RWKV].set(w_decay_up[l])
    wl = wl.at[32:64, D_RWKV:2 * D_RWKV].set(w_a_up[l])
    wl = wl.at[64:160, 2 * D_RWKV:].set(w_g_up[l])
    wl = wl.astype(BF16)
    prm = jnp.zeros((16, D_RWKV), F32)
    prm = prm.at[0].set(w0[l]).at[1].set(a0[l]).at[2].set(k_k[l]).at[3].set(k_a[l])
    prm = prm.at[4].set(r_k[l].reshape(-1)).at[5].set(ln_x_w[l]).at[6].set(ln_x_b[l])
    hid = np.arange(QUAD) // HEAD_DIM
    bmask = jnp.asarray(hid[:, None] == hid[None, :], BF16)
    tri = jnp.asarray(np.tril(np.ones((CHUNK, CHUNK))), BF16)
    qw = jnp.tile(q_norm_w[l][None, :], (1, N_Q_HEADS))
    kw = jnp.tile(k_norm_w[l][None, :], (1, N_KV_HEADS))
    nmw = norm_mix_w[l][None, :]
    nfw = norm_ffn_w[l][None, :]
    tab_a, tab_b = _rope_block_tables(max(t, PAST_LEN + 1) // WINDOW + 1)
    ta, tb = tab_a[PAST_LEN // WINDOW], tab_b[PAST_LEN % WINDOW][None, :]
    tab_s = jnp.concatenate([ta[:, :128] * tb[:, :128] - ta[:, 128:] * tb[:, 128:],
                             ta[:, 128:] * tb[:, :128] + ta[:, :128] * tb[:, 128:]], axis=1)

    xs = x_sample.reshape(nd, D_MODEL)
    shift_in = state_shift[l].reshape(nd, D_SHIFT)
    p_s, vec_t, gb, qn_s, kvn_s, cols = _decode_prep(xs, nmw, w_in_pad, shift_in, mu_pad, prm, wl, bmask, tab_s, qw, kw)
    s_t = jnp.transpose(state_wkv[l], (1, 2, 3, 0))
    ck_t = jnp.swapaxes(cache_k_win[l].reshape(nd, WINDOW, D_KV), 1, 2)
    cv_t = jnp.swapaxes(cache_v_win[l].reshape(nd, WINDOW, D_KV), 1, 2)
    wkv_t, yr_s, ya_s, kc_t, vc_t = _decode_step(vec_t, gb, prm, bmask, s_t, qn_s, kvn_s, cols, ck_t, cv_t,
                                                 sinks[l][:, None])

    xp = x_prompt.reshape(bsz * t, D_MODEL)
    xs_p, plast, q_p, kv_p = _inproj_shift(xp, nmw, w_in_pad, mu_pad, t, 1024)
    yr_p, hbd, wo, wu, wd = _rwkv_prompt(xs_p.reshape(bsz, t, D_SHIFT), prm, wl, tri, bmask,
                                         (w_out[l], w_ffn_up[l], w_ffn_down[l]))
    y_prompt, y_s, kwin_p, vwin_p = _attn_ffn(q_p.reshape(bsz, t, D_ATTN), kv_p.reshape(bsz, t, 2 * D_KV),
                                              tab_a, tab_b, qw, kw, sinks[l][None, :], bmask, _band_bias(),
                                              x_prompt, yr_p, xs, yr_s, ya_s, wo, nfw, wu, wd)
    hb = hbd.reshape(bsz, 2, 4, HEAD_DIM, 2, HEAD_DIM)
    wkv_prompt = jnp.stack([hb[:, :, j, :, j % 2, :] for j in range(4)], axis=2)
    wkv_prompt = wkv_prompt.reshape(bsz, H_RWKV, HEAD_DIM, HEAD_DIM)[None]
    shift_prompt = plast[None]
    k_win_prompt = jnp.swapaxes(kwin_p, 1, 2).reshape(bsz, WINDOW, N_KV_HEADS, HEAD_DIM)[None]
    v_win_prompt = jnp.swapaxes(vwin_p, 1, 2).reshape(bsz, WINDOW, N_KV_HEADS, HEAD_DIM)[None]

    y_sample = y_s.reshape(nd, 1, D_MODEL)
    wkv_sample = jnp.transpose(wkv_t, (3, 0, 1, 2))[None]
    shift_sample = p_s.reshape(nd, 1, D_SHIFT)[None]
    k_win_sample = jnp.swapaxes(kc_t, 1, 2).reshape(nd, WINDOW, N_KV_HEADS, HEAD_DIM)[None]
    v_win_sample = jnp.swapaxes(vc_t, 1, 2).reshape(nd, WINDOW, N_KV_HEADS, HEAD_DIM)[None]

    return (y_prompt, y_sample, wkv_prompt, shift_prompt, k_win_prompt, v_win_prompt,
            wkv_sample, shift_sample, k_win_sample, v_win_sample)
```
